```python
import math
import jax, jax.numpy as jnp
from jax import lax
import numpy as np

D_MODEL = 2048
BATCH = 8
SEQ = 4096
DEPTH = 2

SSM_GROUP = 16
SSM_STATE = 64
SSM_GROUPS = D_MODEL // SSM_GROUP
HEAD_DIM = 128
N_HEADS = D_MODEL // HEAD_DIM
D_FF = 11 * D_MODEL // 4
CONV_WIDTH = 3
BLOCK_Q = 128
N_A_LAYERS = max(1, DEPTH // 2)
N_B_LAYERS = DEPTH - N_A_LAYERS
N_MOD = 6
NORM_EPS = 1e-6
STEP_MIN = 1e-3
STEP_MAX = 1e-1
FORGET_BIAS_MEAN = 3.0

kernel_name = "yoco_s5_fox_convffn_adaln"


def rms_norm(x, g):
    x32 = x.astype(jnp.float32)
    y = x32 * lax.rsqrt(jnp.mean(x32 * x32, axis=-1, keepdims=True) + NORM_EPS)
    return (y * g.astype(jnp.float32)).astype(x.dtype)


def modulate(h, shift, scale):
    return h * (1 + scale[:, None, :]) + shift[:, None, :]


def _complex_affine_combine(e1, e2):
    a1r, a1i, b1r, b1i = e1
    a2r, a2i, b2r, b2i = e2
    ar = a2r * a1r - a2i * a1i
    ai = a2r * a1i + a2i * a1r
    br = a2r * b1r - a2i * b1i + b2r
    bi = a2r * b1i + a2i * b1r + b2i
    return (ar, ai, br, bi)


def s5_ssm(u, log_step, a_re, a_im, b_re, b_im, c_re, c_im, d):
    bsz, seq, _ = u.shape
    f32 = jnp.float32
    ug = u.astype(f32).reshape(bsz, seq, SSM_GROUPS, SSM_GROUP)
    lam_re = a_re.astype(f32)
    lam_im = a_im.astype(f32)
    step = jnp.exp(log_step.astype(f32))[:, None]
    mag = jnp.exp(lam_re * step)
    abar_re = mag * jnp.cos(lam_im * step)
    abar_im = mag * jnp.sin(lam_im * step)
    den = lam_re * lam_re + lam_im * lam_im
    nr = abar_re - 1.0
    fr = (nr * lam_re + abar_im * lam_im) / den
    fi = (abar_im * lam_re - nr * lam_im) / den
    br = b_re.astype(f32)
    bi = b_im.astype(f32)
    bbar_re = fr[..., None] * br - fi[..., None] * bi
    bbar_im = fr[..., None] * bi + fi[..., None] * br
    bu_re = jnp.einsum('blgh,gph->blgp', ug, bbar_re)
    bu_im = jnp.einsum('blgh,gph->blgp', ug, bbar_im)
    shape = bu_re.shape
    elems = (jnp.broadcast_to(abar_re, shape), jnp.broadcast_to(abar_im, shape), bu_re, bu_im)
    _, _, s_re, s_im = lax.associative_scan(_complex_affine_combine, elems, axis=1)
    y = (jnp.einsum('blgp,ghp->blgh', s_re, c_re.astype(f32))
         - jnp.einsum('blgp,ghp->blgh', s_im, c_im.astype(f32))
         + d.astype(f32) * ug)
    return y.reshape(bsz, seq, D_MODEL).astype(u.dtype)


def s5_mixer(h, w_in, log_step, a_re, a_im, b_re, b_im, c_re, c_im, d, glu_w, glu_b, w_out):
    u = h @ w_in
    y = s5_ssm(u, log_step, a_re, a_im, b_re, b_im, c_re, c_im, d)
    g = jax.nn.gelu(y)
    z = g * jax.nn.sigmoid(g @ glu_w + glu_b)
    return z @ w_out


def forgetting_attention(q, k, v, f_cum):
    seq = q.shape[1]
    scale = HEAD_DIM ** -0.5
    outs = []
    for start in range(0, seq, BLOCK_Q):
        end = start + BLOCK_Q
        qb = q[:, start:end]
        kb = k[:, :end]
        vb = v[:, :end]
        s = jnp.einsum('bqhd,bkhd->bhqk', qb, kb).astype(jnp.float32) * scale
        s = s + f_cum[:, :, start:end, None] - f_cum[:, :, None, :end]
        q_pos = start + jnp.arange(BLOCK_Q)
        k_pos = jnp.arange(end)
        mask = k_pos[None, :] <= q_pos[:, None]
        s = jnp.where(mask[None, None], s, -jnp.inf)
        p = jax.nn.softmax(s, axis=-1).astype(vb.dtype)
        outs.append(jnp.einsum('bhqk,bkhd->bqhd', p, vb))
    return jnp.concatenate(outs, axis=1)


def causal_depthwise_conv(a, w, b):
    ch = a.shape[-1]
    out = lax.conv_general_dilated(
        a, w[:, None, :].astype(a.dtype), window_strides=(1,),
        padding=[(CONV_WIDTH - 1, 0)],
        dimension_numbers=('NWC', 'WIO', 'NWC'),
        feature_group_count=ch)
    return out + b


def conv_ffn(h, w_up, conv_w, conv_b, w_down):
    a = h @ w_up
    a = causal_depthwise_conv(a, conv_w, conv_b)
    u, v = jnp.split(a, 2, axis=-1)
    return (jax.nn.silu(u) * v) @ w_down


def _fwd_setup_inputs(seed: int = 0) -> dict:
    key = jax.random.key(seed)
    ks = jax.random.split(key, 32)
    f32 = jnp.float32
    D, G, P, H, F = D_MODEL, SSM_GROUPS, SSM_STATE, SSM_GROUP, D_FF
    nrm = lambda k, shape, s: jax.random.normal(k, shape, f32) * s
    inp = {}
    inp['x'] = nrm(ks[0], (BATCH, SEQ, D), 1.0)
    inp['c'] = nrm(ks[1], (BATCH, D), 1.0)
    inp['mod_w'] = nrm(ks[2], (DEPTH, D, N_MOD * D), 0.5 * D ** -0.5)
    inp['mod_b'] = nrm(ks[3], (DEPTH, N_MOD * D), 0.01)
    inp['norm_mix_g'] = 1.0 + nrm(ks[4], (DEPTH, D), 0.02)
    inp['norm_ffn_g'] = 1.0 + nrm(ks[5], (DEPTH, D), 0.02)
    inp['ssm_w_in'] = nrm(ks[6], (N_A_LAYERS, D, D), D ** -0.5)
    inp['ssm_log_step'] = jax.random.uniform(ks[7], (N_A_LAYERS, G), f32,
                                             math.log(STEP_MIN), math.log(STEP_MAX))
    inp['ssm_a_re'] = -0.5 + nrm(ks[8], (N_A_LAYERS, G, P), 0.01)
    inp['ssm_a_im'] = (math.pi * jnp.arange(P, dtype=f32))[None, None, :] + nrm(ks[9], (N_A_LAYERS, G, P), 0.01)
    inp['ssm_b_re'] = nrm(ks[10], (N_A_LAYERS, G, P, H), (2 * H) ** -0.5)
    inp['ssm_b_im'] = nrm(ks[11], (N_A_LAYERS, G, P, H), (2 * H) ** -0.5)
    inp['ssm_c_re'] = nrm(ks[12], (N_A_LAYERS, G, H, P), P ** -0.5)
    inp['ssm_c_im'] = nrm(ks[13], (N_A_LAYERS, G, H, P), P ** -0.5)
    inp['ssm_d'] = nrm(ks[14], (N_A_LAYERS, G, H), 1.0)
    inp['ssm_glu_w'] = nrm(ks[15], (N_A_LAYERS, D, D), D ** -0.5)
    inp['ssm_glu_b'] = nrm(ks[16], (N_A_LAYERS, D), 0.01)
    inp['ssm_w_out'] = nrm(ks[17], (N_A_LAYERS, D, D), D ** -0.5)
    inp['kv_mod_w'] = nrm(ks[18], (D, 2 * D), 0.5 * D ** -0.5)
    inp['kv_mod_b'] = nrm(ks[19], (2 * D,), 0.01)
    inp['kv_norm_g'] = 1.0 + nrm(ks[20], (D,), 0.02)
    inp['kv_w'] = nrm(ks[21], (D, 2 * D + N_HEADS), D ** -0.5)
    inp['forget_b'] = FORGET_BIAS_MEAN + nrm(ks[22], (N_HEADS,), 0.5)
    inp['attn_w_q'] = nrm(ks[23], (N_B_LAYERS, D, D), D ** -0.5)
    inp['attn_w_out'] = nrm(ks[24], (N_B_LAYERS, D, D), D ** -0.5)
    inp['ffn_w_up'] = nrm(ks[25], (DEPTH, D, 2 * F), D ** -0.5)
    inp['ffn_conv_w'] = nrm(ks[26], (DEPTH, CONV_WIDTH, 2 * F), CONV_WIDTH ** -0.5)
    inp['ffn_conv_b'] = nrm(ks[27], (DEPTH, 2 * F), 0.01)
    inp['ffn_w_down'] = nrm(ks[28], (DEPTH, F, D), F ** -0.5)
    inp['final_norm_g'] = 1.0 + nrm(ks[29], (D,), 0.02)
    return inp


def _fwd_reference(x, c, mod_w, mod_b, norm_mix_g, norm_ffn_g,
              ssm_w_in, ssm_log_step, ssm_a_re, ssm_a_im, ssm_b_re, ssm_b_im,
              ssm_c_re, ssm_c_im, ssm_d, ssm_glu_w, ssm_glu_b, ssm_w_out,
              kv_mod_w, kv_mod_b, kv_norm_g, kv_w, forget_b,
              attn_w_q, attn_w_out,
              ffn_w_up, ffn_conv_w, ffn_conv_b, ffn_w_down,
              final_norm_g):
    bsz, seq, _ = x.shape
    c_act = jax.nn.silu(c)
    k_sh = v_sh = f_cum = None
    for i in range(DEPTH):
        mod = c_act @ mod_w[i] + mod_b[i]
        sh_m, sc_m, g_m, sh_f, sc_f, g_f = jnp.split(mod, N_MOD, axis=-1)
        if i == N_A_LAYERS:
            kv_mod = c_act @ kv_mod_w + kv_mod_b
            sh_kv, sc_kv = jnp.split(kv_mod, 2, axis=-1)
            hk = modulate(rms_norm(x, kv_norm_g), sh_kv, sc_kv)
            kvf = hk @ kv_w
            k_sh = kvf[..., :D_MODEL].reshape(bsz, seq, N_HEADS, HEAD_DIM)
            v_sh = kvf[..., D_MODEL:2 * D_MODEL].reshape(bsz, seq, N_HEADS, HEAD_DIM)
            log_f = jax.nn.log_sigmoid(kvf[..., 2 * D_MODEL:].astype(jnp.float32)
                                       + forget_b.astype(jnp.float32))
            f_cum = jnp.transpose(jnp.cumsum(log_f, axis=1), (0, 2, 1))
        h = modulate(rms_norm(x, norm_mix_g[i]), sh_m, sc_m)
        if i < N_A_LAYERS:
            out = s5_mixer(h, ssm_w_in[i], ssm_log_step[i], ssm_a_re[i], ssm_a_im[i],
                           ssm_b_re[i], ssm_b_im[i], ssm_c_re[i], ssm_c_im[i], ssm_d[i],
                           ssm_glu_w[i], ssm_glu_b[i], ssm_w_out[i])
        else:
            j = i - N_A_LAYERS
            q = (h @ attn_w_q[j]).reshape(bsz, seq, N_HEADS, HEAD_DIM)
            o = forgetting_attention(q, k_sh, v_sh, f_cum)
            out = o.reshape(bsz, seq, D_MODEL) @ attn_w_out[j]
        x = x + g_m[:, None, :] * out
        h = modulate(rms_norm(x, norm_ffn_g[i]), sh_f, sc_f)
        out = conv_ffn(h, ffn_w_up[i], ffn_conv_w[i], ffn_conv_b[i], ffn_w_down[i])
        x = x + g_f[:, None, :] * out
    return rms_norm(x, final_norm_g)


import jax as _jax
import jax.numpy as _jnp

TWIN_FORMAT = 'train_step'
FWD_PARAMS = ['x', 'c', 'mod_w', 'mod_b', 'norm_mix_g', 'norm_ffn_g', 'ssm_w_in', 'ssm_log_step', 'ssm_a_re', 'ssm_a_im', 'ssm_b_re', 'ssm_b_im', 'ssm_c_re', 'ssm_c_im', 'ssm_d', 'ssm_glu_w', 'ssm_glu_b', 'ssm_w_out', 'kv_mod_w', 'kv_mod_b', 'kv_norm_g', 'kv_w', 'forget_b', 'attn_w_q', 'attn_w_out', 'ffn_w_up', 'ffn_conv_w', 'ffn_conv_b', 'ffn_w_down', 'final_norm_g']
TWIN_WEIGHTS = ['mod_w', 'mod_b', 'norm_mix_g', 'norm_ffn_g', 'ssm_w_in', 'ssm_log_step', 'ssm_a_re', 'ssm_a_im', 'ssm_b_re', 'ssm_b_im', 'ssm_c_re', 'ssm_c_im', 'ssm_d', 'ssm_glu_w', 'ssm_glu_b', 'ssm_w_out', 'kv_mod_w', 'kv_mod_b', 'kv_norm_g', 'kv_w', 'forget_b', 'attn_w_q', 'attn_w_out', 'ffn_w_up', 'ffn_conv_w', 'ffn_conv_b', 'ffn_w_down', 'final_norm_g']
TWIN_DIFF_INPUT = 'x'
TWIN_INPUTS = ['x', 'c', 'mod_w', 'mod_b', 'norm_mix_g', 'norm_ffn_g', 'ssm_w_in', 'ssm_log_step', 'ssm_a_re', 'ssm_a_im', 'ssm_b_re', 'ssm_b_im', 'ssm_c_re', 'ssm_c_im', 'ssm_d', 'ssm_glu_w', 'ssm_glu_b', 'ssm_w_out', 'kv_mod_w', 'kv_mod_b', 'kv_norm_g', 'kv_w', 'forget_b', 'attn_w_q', 'attn_w_out', 'ffn_w_up', 'ffn_conv_w', 'ffn_conv_b', 'ffn_w_down', 'final_norm_g', 'loss_target', 'm_mod_w', 'm_mod_b', 'm_norm_mix_g', 'm_norm_ffn_g', 'm_ssm_w_in', 'm_ssm_log_step', 'm_ssm_a_re', 'm_ssm_a_im', 'm_ssm_b_re', 'm_ssm_b_im', 'm_ssm_c_re', 'm_ssm_c_im', 'm_ssm_d', 'm_ssm_glu_w', 'm_ssm_glu_b', 'm_ssm_w_out', 'm_kv_mod_w', 'm_kv_mod_b', 'm_kv_norm_g', 'm_kv_w', 'm_forget_b', 'm_attn_w_q', 'm_attn_w_out', 'm_ffn_w_up', 'm_ffn_conv_w', 'm_ffn_conv_b', 'm_ffn_w_down', 'm_final_norm_g', 'v_mod_w', 'v_mod_b', 'v_norm_mix_g', 'v_norm_ffn_g', 'v_ssm_w_in', 'v_ssm_log_step', 'v_ssm_a_re', 'v_ssm_a_im', 'v_ssm_b_re', 'v_ssm_b_im', 'v_ssm_c_re', 'v_ssm_c_im', 'v_ssm_d', 'v_ssm_glu_w', 'v_ssm_glu_b', 'v_ssm_w_out', 'v_kv_mod_w', 'v_kv_mod_b', 'v_kv_norm_g', 'v_kv_w', 'v_forget_b', 'v_attn_w_q', 'v_attn_w_out', 'v_ffn_w_up', 'v_ffn_conv_w', 'v_ffn_conv_b', 'v_ffn_w_down', 'v_final_norm_g']
TWIN_OUTPUTS = ['loss', 'grad_x', 'grad_mod_w', 'grad_mod_b', 'grad_norm_mix_g', 'grad_norm_ffn_g', 'grad_ssm_w_in', 'grad_ssm_log_step', 'grad_ssm_a_re', 'grad_ssm_a_im', 'grad_ssm_b_re', 'grad_ssm_b_im', 'grad_ssm_c_re', 'grad_ssm_c_im', 'grad_ssm_d', 'grad_ssm_glu_w', 'grad_ssm_glu_b', 'grad_ssm_w_out', 'grad_kv_mod_w', 'grad_kv_mod_b', 'grad_kv_norm_g', 'grad_kv_w', 'grad_forget_b', 'grad_attn_w_q', 'grad_attn_w_out', 'grad_ffn_w_up', 'grad_ffn_conv_w', 'grad_ffn_conv_b', 'grad_ffn_w_down', 'grad_final_norm_g', 'delta_mod_w', 'delta_mod_b', 'delta_norm_mix_g', 'delta_norm_ffn_g', 'delta_ssm_w_in', 'delta_ssm_log_step', 'delta_ssm_a_re', 'delta_ssm_a_im', 'delta_ssm_b_re', 'delta_ssm_b_im', 'delta_ssm_c_re', 'delta_ssm_c_im', 'delta_ssm_d', 'delta_ssm_glu_w', 'delta_ssm_glu_b', 'delta_ssm_w_out', 'delta_kv_mod_w', 'delta_kv_mod_b', 'delta_kv_norm_g', 'delta_kv_w', 'delta_forget_b', 'delta_attn_w_q', 'delta_attn_w_out', 'delta_ffn_w_up', 'delta_ffn_conv_w', 'delta_ffn_conv_b', 'delta_ffn_w_down', 'delta_final_norm_g', 'new_m_mod_w', 'new_m_mod_b', 'new_m_norm_mix_g', 'new_m_norm_ffn_g', 'new_m_ssm_w_in', 'new_m_ssm_log_step', 'new_m_ssm_a_re', 'new_m_ssm_a_im', 'new_m_ssm_b_re', 'new_m_ssm_b_im', 'new_m_ssm_c_re', 'new_m_ssm_c_im', 'new_m_ssm_d', 'new_m_ssm_glu_w', 'new_m_ssm_glu_b', 'new_m_ssm_w_out', 'new_m_kv_mod_w', 'new_m_kv_mod_b', 'new_m_kv_norm_g', 'new_m_kv_w', 'new_m_forget_b', 'new_m_attn_w_q', 'new_m_attn_w_out', 'new_m_ffn_w_up', 'new_m_ffn_conv_w', 'new_m_ffn_conv_b', 'new_m_ffn_w_down', 'new_m_final_norm_g', 'new_v_mod_w', 'new_v_mod_b', 'new_v_norm_mix_g', 'new_v_norm_ffn_g', 'new_v_ssm_w_in', 'new_v_ssm_log_step', 'new_v_ssm_a_re', 'new_v_ssm_a_im', 'new_v_ssm_b_re', 'new_v_ssm_b_im', 'new_v_ssm_c_re', 'new_v_ssm_c_im', 'new_v_ssm_d', 'new_v_ssm_glu_w', 'new_v_ssm_glu_b', 'new_v_ssm_w_out', 'new_v_kv_mod_w', 'new_v_kv_mod_b', 'new_v_kv_norm_g', 'new_v_kv_w', 'new_v_forget_b', 'new_v_attn_w_q', 'new_v_attn_w_out', 'new_v_ffn_w_up', 'new_v_ffn_conv_w', 'new_v_ffn_conv_b', 'new_v_ffn_w_down', 'new_v_final_norm_g']
TWIN_LEAF_KINDS = {'loss': 'loss', 'grad_x': 'grad_x', 'grad_mod_w': 'grad_w', 'grad_mod_b': 'grad_w', 'grad_norm_mix_g': 'grad_w', 'grad_norm_ffn_g': 'grad_w', 'grad_ssm_w_in': 'grad_w', 'grad_ssm_log_step': 'grad_w', 'grad_ssm_a_re': 'grad_w', 'grad_ssm_a_im': 'grad_w', 'grad_ssm_b_re': 'grad_w', 'grad_ssm_b_im': 'grad_w', 'grad_ssm_c_re': 'grad_w', 'grad_ssm_c_im': 'grad_w', 'grad_ssm_d': 'grad_w', 'grad_ssm_glu_w': 'grad_w', 'grad_ssm_glu_b': 'grad_w', 'grad_ssm_w_out': 'grad_w', 'grad_kv_mod_w': 'grad_w', 'grad_kv_mod_b': 'grad_w', 'grad_kv_norm_g': 'grad_w', 'grad_kv_w': 'grad_w', 'grad_forget_b': 'grad_w', 'grad_attn_w_q': 'grad_w', 'grad_attn_w_out': 'grad_w', 'grad_ffn_w_up': 'grad_w', 'grad_ffn_conv_w': 'grad_w', 'grad_ffn_conv_b': 'grad_w', 'grad_ffn_w_down': 'grad_w', 'grad_final_norm_g': 'grad_w', 'delta_mod_w': 'delta_w', 'delta_mod_b': 'delta_w', 'delta_norm_mix_g': 'delta_w', 'delta_norm_ffn_g': 'delta_w', 'delta_ssm_w_in': 'delta_w', 'delta_ssm_log_step': 'delta_w', 'delta_ssm_a_re': 'delta_w', 'delta_ssm_a_im': 'delta_w', 'delta_ssm_b_re': 'delta_w', 'delta_ssm_b_im': 'delta_w', 'delta_ssm_c_re': 'delta_w', 'delta_ssm_c_im': 'delta_w', 'delta_ssm_d': 'delta_w', 'delta_ssm_glu_w': 'delta_w', 'delta_ssm_glu_b': 'delta_w', 'delta_ssm_w_out': 'delta_w', 'delta_kv_mod_w': 'delta_w', 'delta_kv_mod_b': 'delta_w', 'delta_kv_norm_g': 'delta_w', 'delta_kv_w': 'delta_w', 'delta_forget_b': 'delta_w', 'delta_attn_w_q': 'delta_w', 'delta_attn_w_out': 'delta_w', 'delta_ffn_w_up': 'delta_w', 'delta_ffn_conv_w': 'delta_w', 'delta_ffn_conv_b': 'delta_w', 'delta_ffn_w_down': 'delta_w', 'delta_final_norm_g': 'delta_w', 'new_m_mod_w': 'new_m', 'new_m_mod_b': 'new_m', 'new_m_norm_mix_g': 'new_m', 'new_m_norm_ffn_g': 'new_m', 'new_m_ssm_w_in': 'new_m', 'new_m_ssm_log_step': 'new_m', 'new_m_ssm_a_re': 'new_m', 'new_m_ssm_a_im': 'new_m', 'new_m_ssm_b_re': 'new_m', 'new_m_ssm_b_im': 'new_m', 'new_m_ssm_c_re': 'new_m', 'new_m_ssm_c_im': 'new_m', 'new_m_ssm_d': 'new_m', 'new_m_ssm_glu_w': 'new_m', 'new_m_ssm_glu_b': 'new_m', 'new_m_ssm_w_out': 'new_m', 'new_m_kv_mod_w': 'new_m', 'new_m_kv_mod_b': 'new_m', 'new_m_kv_norm_g': 'new_m', 'new_m_kv_w': 'new_m', 'new_m_forget_b': 'new_m', 'new_m_attn_w_q': 'new_m', 'new_m_attn_w_out': 'new_m', 'new_m_ffn_w_up': 'new_m', 'new_m_ffn_conv_w': 'new_m', 'new_m_ffn_conv_b': 'new_m', 'new_m_ffn_w_down': 'new_m', 'new_m_final_norm_g': 'new_m', 'new_v_mod_w': 'new_v', 'new_v_mod_b': 'new_v', 'new_v_norm_mix_g': 'new_v', 'new_v_norm_ffn_g': 'new_v', 'new_v_ssm_w_in': 'new_v', 'new_v_ssm_log_step': 'new_v', 'new_v_ssm_a_re': 'new_v', 'new_v_ssm_a_im': 'new_v', 'new_v_ssm_b_re': 'new_v', 'new_v_ssm_b_im': 'new_v', 'new_v_ssm_c_re': 'new_v', 'new_v_ssm_c_im': 'new_v', 'new_v_ssm_d': 'new_v', 'new_v_ssm_glu_w': 'new_v', 'new_v_ssm_glu_b': 'new_v', 'new_v_ssm_w_out': 'new_v', 'new_v_kv_mod_w': 'new_v', 'new_v_kv_mod_b': 'new_v', 'new_v_kv_norm_g': 'new_v', 'new_v_kv_w': 'new_v', 'new_v_forget_b': 'new_v', 'new_v_attn_w_q': 'new_v', 'new_v_attn_w_out': 'new_v', 'new_v_ffn_w_up': 'new_v', 'new_v_ffn_conv_w': 'new_v', 'new_v_ffn_conv_b': 'new_v', 'new_v_ffn_w_down': 'new_v', 'new_v_final_norm_g': 'new_v'}


def _forward(args):
    return _fwd_reference(*[args[k] for k in FWD_PARAMS])


def _output_shape():
    def fwd():
        inp = _fwd_setup_inputs(0)
        return _fwd_reference(*[inp[k] for k in FWD_PARAMS])
    out = _jax.eval_shape(fwd)
    return out.shape, out.dtype

N_MICROBATCH = 1
ADAM_LR = 0.001
ADAM_B1 = 0.9
ADAM_B2 = 0.999
ADAM_EPS = 1e-08
ADAM_WD = 0.01
ADAM_STEP = 10
PER_EXAMPLE_BATCH_AXIS = {'x': 0, 'c': 0, 'loss_target': 0}
SHARED_INPUTS = []
_WEIGHT_DTYPES = {'mod_w': _jnp.float32, 'mod_b': _jnp.float32, 'norm_mix_g': _jnp.float32, 'norm_ffn_g': _jnp.float32, 'ssm_w_in': _jnp.float32, 'ssm_log_step': _jnp.float32, 'ssm_a_re': _jnp.float32, 'ssm_a_im': _jnp.float32, 'ssm_b_re': _jnp.float32, 'ssm_b_im': _jnp.float32, 'ssm_c_re': _jnp.float32, 'ssm_c_im': _jnp.float32, 'ssm_d': _jnp.float32, 'ssm_glu_w': _jnp.float32, 'ssm_glu_b': _jnp.float32, 'ssm_w_out': _jnp.float32, 'kv_mod_w': _jnp.float32, 'kv_mod_b': _jnp.float32, 'kv_norm_g': _jnp.float32, 'kv_w': _jnp.float32, 'forget_b': _jnp.float32, 'attn_w_q': _jnp.float32, 'attn_w_out': _jnp.float32, 'ffn_w_up': _jnp.float32, 'ffn_conv_w': _jnp.float32, 'ffn_conv_b': _jnp.float32, 'ffn_w_down': _jnp.float32, 'final_norm_g': _jnp.float32}
MOMENT_SCALE = {'mod_w': 2.115713e-02, 'mod_b': 3.735458e-02, 'norm_mix_g': 9.995595e-03, 'norm_ffn_g': 2.784391e-02, 'ssm_w_in': 1.181375e-02, 'ssm_log_step': 5.497772e-01, 'ssm_a_re': 1.065405e-03, 'ssm_a_im': 1.382929e-03, 'ssm_b_re': 6.169404e-04, 'ssm_b_im': 6.842576e-04, 'ssm_c_re': 8.457344e-04, 'ssm_c_im': 8.992449e-04, 'ssm_d': 1.183609e-02, 'ssm_glu_w': 3.493203e-03, 'ssm_glu_b': 4.784930e-03, 'ssm_w_out': 1.135580e-02, 'kv_mod_w': 1.239707e-02, 'kv_mod_b': 2.055750e-02, 'kv_norm_g': 1.285385e-02, 'kv_w': 1.083191e-02, 'forget_b': 7.468238e-02, 'attn_w_q': 8.460981e-03, 'attn_w_out': 1.252792e-02, 'ffn_w_up': 1.161286e-02, 'ffn_conv_w': 1.164667e-02, 'ffn_conv_b': 1.048919e-02, 'ffn_w_down': 1.901358e-02, 'final_norm_g': 1.599264e+01}


def _to_microbatches(a, axis):
    t = _jnp.moveaxis(a, axis, 0)
    t = t.reshape((N_MICROBATCH, t.shape[0] // N_MICROBATCH) + t.shape[1:])
    return _jnp.moveaxis(t, 1, axis + 1)


def setup_inputs(seed: int = 0) -> dict:
    inp = _fwd_setup_inputs(seed)
    key = _jax.random.fold_in(_jax.random.key(seed), 7919)
    shape, _ = _output_shape()
    out = dict(inp)
    out["loss_target"] = _jax.random.normal(_jax.random.fold_in(key, 0), shape, _jnp.float32)
    for i, name in enumerate(TWIN_WEIGHTS):
        w = inp[name].astype(_jnp.float32)
        if MOMENT_SCALE is None:
            s = _jnp.sqrt(_jnp.mean(_jnp.square(w)) + 1e-30)
        else:
            s = MOMENT_SCALE[name]
        km, kv = _jax.random.split(_jax.random.fold_in(key, i + 1))
        out[name] = w
        out["m_" + name] = s * _jax.random.normal(km, w.shape, _jnp.float32)
        out["v_" + name] = (s * s) * _jax.random.uniform(kv, w.shape, _jnp.float32, 0.5, 1.5)
    if N_MICROBATCH > 1:
        for name, axis in PER_EXAMPLE_BATCH_AXIS.items():
            out[name] = _to_microbatches(out[name], axis)
    return {'x': out['x'], 'c': out['c'], 'mod_w': out['mod_w'], 'mod_b': out['mod_b'], 'norm_mix_g': out['norm_mix_g'], 'norm_ffn_g': out['norm_ffn_g'], 'ssm_w_in': out['ssm_w_in'], 'ssm_log_step': out['ssm_log_step'], 'ssm_a_re': out['ssm_a_re'], 'ssm_a_im': out['ssm_a_im'], 'ssm_b_re': out['ssm_b_re'], 'ssm_b_im': out['ssm_b_im'], 'ssm_c_re': out['ssm_c_re'], 'ssm_c_im': out['ssm_c_im'], 'ssm_d': out['ssm_d'], 'ssm_glu_w': out['ssm_glu_w'], 'ssm_glu_b': out['ssm_glu_b'], 'ssm_w_out': out['ssm_w_out'], 'kv_mod_w': out['kv_mod_w'], 'kv_mod_b': out['kv_mod_b'], 'kv_norm_g': out['kv_norm_g'], 'kv_w': out['kv_w'], 'forget_b': out['forget_b'], 'attn_w_q': out['attn_w_q'], 'attn_w_out': out['attn_w_out'], 'ffn_w_up': out['ffn_w_up'], 'ffn_conv_w': out['ffn_conv_w'], 'ffn_conv_b': out['ffn_conv_b'], 'ffn_w_down': out['ffn_w_down'], 'final_norm_g': out['final_norm_g'], 'loss_target': out['loss_target'], 'm_mod_w': out['m_mod_w'], 'm_mod_b': out['m_mod_b'], 'm_norm_mix_g': out['m_norm_mix_g'], 'm_norm_ffn_g': out['m_norm_ffn_g'], 'm_ssm_w_in': out['m_ssm_w_in'], 'm_ssm_log_step': out['m_ssm_log_step'], 'm_ssm_a_re': out['m_ssm_a_re'], 'm_ssm_a_im': out['m_ssm_a_im'], 'm_ssm_b_re': out['m_ssm_b_re'], 'm_ssm_b_im': out['m_ssm_b_im'], 'm_ssm_c_re': out['m_ssm_c_re'], 'm_ssm_c_im': out['m_ssm_c_im'], 'm_ssm_d': out['m_ssm_d'], 'm_ssm_glu_w': out['m_ssm_glu_w'], 'm_ssm_glu_b': out['m_ssm_glu_b'], 'm_ssm_w_out': out['m_ssm_w_out'], 'm_kv_mod_w': out['m_kv_mod_w'], 'm_kv_mod_b': out['m_kv_mod_b'], 'm_kv_norm_g': out['m_kv_norm_g'], 'm_kv_w': out['m_kv_w'], 'm_forget_b': out['m_forget_b'], 'm_attn_w_q': out['m_attn_w_q'], 'm_attn_w_out': out['m_attn_w_out'], 'm_ffn_w_up': out['m_ffn_w_up'], 'm_ffn_conv_w': out['m_ffn_conv_w'], 'm_ffn_conv_b': out['m_ffn_conv_b'], 'm_ffn_w_down': out['m_ffn_w_down'], 'm_final_norm_g': out['m_final_norm_g'], 'v_mod_w': out['v_mod_w'], 'v_mod_b': out['v_mod_b'], 'v_norm_mix_g': out['v_norm_mix_g'], 'v_norm_ffn_g': out['v_norm_ffn_g'], 'v_ssm_w_in': out['v_ssm_w_in'], 'v_ssm_log_step': out['v_ssm_log_step'], 'v_ssm_a_re': out['v_ssm_a_re'], 'v_ssm_a_im': out['v_ssm_a_im'], 'v_ssm_b_re': out['v_ssm_b_re'], 'v_ssm_b_im': out['v_ssm_b_im'], 'v_ssm_c_re': out['v_ssm_c_re'], 'v_ssm_c_im': out['v_ssm_c_im'], 'v_ssm_d': out['v_ssm_d'], 'v_ssm_glu_w': out['v_ssm_glu_w'], 'v_ssm_glu_b': out['v_ssm_glu_b'], 'v_ssm_w_out': out['v_ssm_w_out'], 'v_kv_mod_w': out['v_kv_mod_w'], 'v_kv_mod_b': out['v_kv_mod_b'], 'v_kv_norm_g': out['v_kv_norm_g'], 'v_kv_w': out['v_kv_w'], 'v_forget_b': out['v_forget_b'], 'v_attn_w_q': out['v_attn_w_q'], 'v_attn_w_out': out['v_attn_w_out'], 'v_ffn_w_up': out['v_ffn_w_up'], 'v_ffn_conv_w': out['v_ffn_conv_w'], 'v_ffn_conv_b': out['v_ffn_conv_b'], 'v_ffn_w_down': out['v_ffn_w_down'], 'v_final_norm_g': out['v_final_norm_g']}


def _loss(weights, diff, rest, loss_target):
    with _jax.named_scope("forward"):
        args = {**rest, TWIN_DIFF_INPUT: diff, **{k: w.astype(_WEIGHT_DTYPES[k]) for k, w in weights.items()}}
        y = _forward(args)
    with _jax.named_scope("loss_head"):
        err = _jnp.square(y.astype(_jnp.float32) - loss_target)
        return 0.5 * _jnp.sum(_jnp.mean(err, axis=-1)) if err.ndim else 0.5 * err


def _adamw(w, g, m, v):
    m = ADAM_B1 * m + (1.0 - ADAM_B1) * g
    v = ADAM_B2 * v + (1.0 - ADAM_B2) * _jnp.square(g)
    m_hat = m / (1.0 - ADAM_B1 ** ADAM_STEP)
    v_hat = v / (1.0 - ADAM_B2 ** ADAM_STEP)
    delta = -ADAM_LR * (m_hat / (_jnp.sqrt(v_hat) + ADAM_EPS) + ADAM_WD * w)
    return delta, m, v


def reference(x, c, mod_w, mod_b, norm_mix_g, norm_ffn_g, ssm_w_in, ssm_log_step, ssm_a_re, ssm_a_im, ssm_b_re, ssm_b_im, ssm_c_re, ssm_c_im, ssm_d, ssm_glu_w, ssm_glu_b, ssm_w_out, kv_mod_w, kv_mod_b, kv_norm_g, kv_w, forget_b, attn_w_q, attn_w_out, ffn_w_up, ffn_conv_w, ffn_conv_b, ffn_w_down, final_norm_g, loss_target, m_mod_w, m_mod_b, m_norm_mix_g, m_norm_ffn_g, m_ssm_w_in, m_ssm_log_step, m_ssm_a_re, m_ssm_a_im, m_ssm_b_re, m_ssm_b_im, m_ssm_c_re, m_ssm_c_im, m_ssm_d, m_ssm_glu_w, m_ssm_glu_b, m_ssm_w_out, m_kv_mod_w, m_kv_mod_b, m_kv_norm_g, m_kv_w, m_forget_b, m_attn_w_q, m_attn_w_out, m_ffn_w_up, m_ffn_conv_w, m_ffn_conv_b, m_ffn_w_down, m_final_norm_g, v_mod_w, v_mod_b, v_norm_mix_g, v_norm_ffn_g, v_ssm_w_in, v_ssm_log_step, v_ssm_a_re, v_ssm_a_im, v_ssm_b_re, v_ssm_b_im, v_ssm_c_re, v_ssm_c_im, v_ssm_d, v_ssm_glu_w, v_ssm_glu_b, v_ssm_w_out, v_kv_mod_w, v_kv_mod_b, v_kv_norm_g, v_kv_w, v_forget_b, v_attn_w_q, v_attn_w_out, v_ffn_w_up, v_ffn_conv_w, v_ffn_conv_b, v_ffn_w_down, v_final_norm_g):
    given = dict(x=x, c=c, mod_w=mod_w, mod_b=mod_b, norm_mix_g=norm_mix_g, norm_ffn_g=norm_ffn_g, ssm_w_in=ssm_w_in, ssm_log_step=ssm_log_step, ssm_a_re=ssm_a_re, ssm_a_im=ssm_a_im, ssm_b_re=ssm_b_re, ssm_b_im=ssm_b_im, ssm_c_re=ssm_c_re, ssm_c_im=ssm_c_im, ssm_d=ssm_d, ssm_glu_w=ssm_glu_w, ssm_glu_b=ssm_glu_b, ssm_w_out=ssm_w_out, kv_mod_w=kv_mod_w, kv_mod_b=kv_mod_b, kv_norm_g=kv_norm_g, kv_w=kv_w, forget_b=forget_b, attn_w_q=attn_w_q, attn_w_out=attn_w_out, ffn_w_up=ffn_w_up, ffn_conv_w=ffn_conv_w, ffn_conv_b=ffn_conv_b, ffn_w_down=ffn_w_down, final_norm_g=final_norm_g, loss_target=loss_target, m_mod_w=m_mod_w, m_mod_b=m_mod_b, m_norm_mix_g=m_norm_mix_g, m_norm_ffn_g=m_norm_ffn_g, m_ssm_w_in=m_ssm_w_in, m_ssm_log_step=m_ssm_log_step, m_ssm_a_re=m_ssm_a_re, m_ssm_a_im=m_ssm_a_im, m_ssm_b_re=m_ssm_b_re, m_ssm_b_im=m_ssm_b_im, m_ssm_c_re=m_ssm_c_re, m_ssm_c_im=m_ssm_c_im, m_ssm_d=m_ssm_d, m_ssm_glu_w=m_ssm_glu_w, m_ssm_glu_b=m_ssm_glu_b, m_ssm_w_out=m_ssm_w_out, m_kv_mod_w=m_kv_mod_w, m_kv_mod_b=m_kv_mod_b, m_kv_norm_g=m_kv_norm_g, m_kv_w=m_kv_w, m_forget_b=m_forget_b, m_attn_w_q=m_attn_w_q, m_attn_w_out=m_attn_w_out, m_ffn_w_up=m_ffn_w_up, m_ffn_conv_w=m_ffn_conv_w, m_ffn_conv_b=m_ffn_conv_b, m_ffn_w_down=m_ffn_w_down, m_final_norm_g=m_final_norm_g, v_mod_w=v_mod_w, v_mod_b=v_mod_b, v_norm_mix_g=v_norm_mix_g, v_norm_ffn_g=v_norm_ffn_g, v_ssm_w_in=v_ssm_w_in, v_ssm_log_step=v_ssm_log_step, v_ssm_a_re=v_ssm_a_re, v_ssm_a_im=v_ssm_a_im, v_ssm_b_re=v_ssm_b_re, v_ssm_b_im=v_ssm_b_im, v_ssm_c_re=v_ssm_c_re, v_ssm_c_im=v_ssm_c_im, v_ssm_d=v_ssm_d, v_ssm_glu_w=v_ssm_glu_w, v_ssm_glu_b=v_ssm_glu_b, v_ssm_w_out=v_ssm_w_out, v_kv_mod_w=v_kv_mod_w, v_kv_mod_b=v_kv_mod_b, v_kv_norm_g=v_kv_norm_g, v_kv_w=v_kv_w, v_forget_b=v_forget_b, v_attn_w_q=v_attn_w_q, v_attn_w_out=v_attn_w_out, v_ffn_w_up=v_ffn_w_up, v_ffn_conv_w=v_ffn_conv_w, v_ffn_conv_b=v_ffn_conv_b, v_ffn_w_down=v_ffn_w_down, v_final_norm_g=v_final_norm_g)
    weights = {n: given[n] for n in TWIN_WEIGHTS}
    shared = {n: given[n] for n in SHARED_INPUTS}
    per_example = {n: given[n] for n in ['x', 'c']}
    grad_fn = _jax.value_and_grad(_loss, argnums=(0, 1))

    def one_microbatch(ex, loss_target):
        ex = dict(ex)
        diff = ex.pop(TWIN_DIFF_INPUT)
        return grad_fn(weights, diff, {**shared, **ex}, loss_target)

    if N_MICROBATCH == 1:
        loss, (grad_w, grad_x) = one_microbatch(per_example, given["loss_target"])
    else:
        def body(carry, xs):
            loss_sum, grad_sum = carry
            l_k, (gw_k, gx_k) = one_microbatch(xs[0], xs[1])
            with _jax.named_scope("update"):
                return (loss_sum + l_k, _jax.tree.map(_jnp.add, grad_sum, gw_k)), gx_k

        init = (_jnp.zeros((), _jnp.float32), _jax.tree.map(_jnp.zeros_like, weights))
        (loss, grad_w), grad_x = _jax.lax.scan(body, init, (per_example, given["loss_target"]))
    with _jax.named_scope("update"):
        delta_w, new_m, new_v = {}, {}, {}
        for n in TWIN_WEIGHTS:
            delta_w[n], new_m[n], new_v[n] = _adamw(weights[n], grad_w[n], given["m_" + n], given["v_" + n])
    return (loss, grad_x, *[grad_w[n] for n in TWIN_WEIGHTS], *[delta_w[n] for n in TWIN_WEIGHTS],
            *[new_m[n] for n in TWIN_WEIGHTS], *[new_v[n] for n in TWIN_WEIGHTS])
```

```python
import collections
import functools
import math

import jax
import jax.numpy as jnp
from jax import lax
from jax.experimental import pallas as pl
from jax.experimental.pallas import tpu as pltpu

f32 = jnp.float32
bf16 = jnp.bfloat16
MESH = pl.DeviceIdType.MESH

LANES = 128
SUBLANES = 8
VMEM_BYTES_V7X = 64 * 1024 * 1024
VMEM_LIMIT = 56 * 1024 * 1024

Cfg = collections.namedtuple("Cfg", "L D G P H NH DH F TC BQ")
CFG = Cfg(L=4096, D=2048, G=128, P=64, H=16, NH=16, DH=128, F=5632, TC=512, BQ=512)
NORM_EPS = 1e-6
ADAM_LR, ADAM_B1, ADAM_B2, ADAM_EPS, ADAM_WD, ADAM_STEP = 0.001, 0.9, 0.999, 1e-08, 0.01, 10
N_DEV = 8


def _cp(sem=None):
    return pltpu.CompilerParams(dimension_semantics=sem, vmem_limit_bytes=VMEM_LIMIT)


def _tile(dim, pref, unit=LANES):
    if dim <= pref:
        return dim
    t = (pref // unit) * unit
    while t > unit and dim % t:
        t -= unit
    assert dim % t == 0, (dim, pref)
    return t


_DIMS = {"nn": (((1,), (0,)), ((), ())), "nt": (((1,), (1,)), ((), ())), "tn": (((0,), (0,)), ((), ()))}


def _mm(a, b, *, mode, name, tm=1024, tn=1024, tk=512, b4=False, out4=False, a_pro=None, extras=(), epi=None,
        out_dtypes=(f32,)):
    if mode == "tn":
        K, M = a.shape
    else:
        M, K = a.shape
    if b4:
        R, c4 = b.shape[1], b.shape[2]
        N = R if mode == "nt" else 4 * c4
        assert (K == 4 * c4) if mode == "nt" else (K == R)
    else:
        N = b.shape[0] if mode == "nt" else b.shape[1]
        assert K == (b.shape[1] if mode == "nt" else b.shape[0])
    n4 = N // 4
    tm = _tile(M, tm, LANES if mode == "tn" else SUBLANES * 2)
    tn = _tile(n4 if out4 or (b4 and mode != "nt") else N, tn)
    tk = _tile(b.shape[2] if (b4 and mode == "nt") else K, tk)
    nm, nn_, nk = M // tm, N // tn, K // tk

    a_spec = pl.BlockSpec((tk, tm), lambda i, j, k: (k, i)) if mode == "tn" else pl.BlockSpec((tm, tk), lambda i, j, k: (i, k))
    if b4 and mode == "nt":
        q = b.shape[2] // tk
        b_spec = pl.BlockSpec((None, tn, tk), lambda i, j, k: (lax.div(k, q), j, lax.rem(k, q)))
    elif b4:
        q = b.shape[2] // tn
        b_spec = pl.BlockSpec((None, tk, tn), lambda i, j, k: (lax.div(j, q), k, lax.rem(j, q)))
    elif mode == "nt":
        b_spec = pl.BlockSpec((tn, tk), lambda i, j, k: (j, k))
    else:
        b_spec = pl.BlockSpec((tk, tn), lambda i, j, k: (k, j))
    ex_specs = []
    for arr, kind in extras:
        if kind == "mn":
            ex_specs.append(pl.BlockSpec((tm, tn), lambda i, j, k: (i, j)))
        else:
            ex_specs.append(pl.BlockSpec((1, tn), lambda i, j, k: (0, j)))
    if out4:
        qo = n4 // tn
        o_spec = pl.BlockSpec((None, tm, tn), lambda i, j, k: (lax.div(j, qo), i, lax.rem(j, qo)))
        o_shapes = [jax.ShapeDtypeStruct((4, M, n4), dt) for dt in out_dtypes]
    else:
        o_spec = pl.BlockSpec((tm, tn), lambda i, j, k: (i, j))
        o_shapes = [jax.ShapeDtypeStruct((M, N), dt) for dt in out_dtypes]
    ne, no = len(extras), len(out_dtypes)
    dims = _DIMS[mode]

    def body(a_ref, b_ref, *rest):
        ex_refs, o_refs, acc_ref = rest[:ne], rest[ne:ne + no], rest[ne + no]
        k = pl.program_id(2)

        @pl.when(k == 0)
        def _():
            acc_ref[...] = jnp.zeros_like(acc_ref)

        av = a_ref[...]
        if a_pro is not None:
            av = a_pro(av)
        acc_ref[...] += lax.dot_general(av.astype(bf16), b_ref[...].astype(bf16), dims, preferred_element_type=f32)

        @pl.when(k == nk - 1)
        def _():
            acc = acc_ref[...]
            outs = (acc,) if epi is None else epi(acc, *[r[...] for r in ex_refs])
            for o_ref, o in zip(o_refs, outs):
                o_ref[...] = o.astype(o_ref.dtype)

    res = pl.pallas_call(
        body, name=name, grid=(nm, nn_, nk),
        in_specs=[a_spec, b_spec] + ex_specs, out_specs=[o_spec] * no, out_shape=o_shapes,
        scratch_shapes=[pltpu.VMEM((tm, tn), f32)],
        compiler_params=_cp(("parallel", "parallel", "arbitrary")),
    )(a, b, *[e[0] for e in extras])
    return res[0] if no == 1 else res


HALO = 16


def _rowwise(fn, ins, outs, accs, *, L, C, tl, tc, name):
    tl = _tile(L, tl, HALO)
    tc = _tile(C, tc)
    ni, nj = L // tl, C // tc
    hb = tl // HALO
    nh = L // HALO
    in_specs = []
    for spec in ins:
        kind = spec[1]
        off = spec[2] if len(spec) > 2 else 0
        if kind == "rc":
            in_specs.append(pl.BlockSpec((tl, tc), lambda j, i, off=off: (i, j + off)))
        elif kind == "c":
            in_specs.append(pl.BlockSpec((1, tc), lambda j, i, off=off: (0, j + off)))
        elif kind == "c3":
            in_specs.append(pl.BlockSpec((3, tc), lambda j, i, off=off: (0, j + off)))
        elif kind == "prev":
            in_specs.append(pl.BlockSpec((HALO, tc), lambda j, i, off=off: (jnp.maximum(i * hb - 1, 0), j + off)))
        elif kind == "next":
            in_specs.append(pl.BlockSpec((HALO, tc), lambda j, i, off=off: (jnp.minimum((i + 1) * hb, nh - 1), j + off)))
        else:
            raise ValueError(kind)
    out_specs = [pl.BlockSpec((tl, tc), lambda j, i: (i, j)) for _ in outs]
    out_specs += [pl.BlockSpec((r, tc), lambda j, i: (0, j)) for r in accs]
    out_shape = [jax.ShapeDtypeStruct((L, C), dt) for dt in outs] + [jax.ShapeDtypeStruct((r, C), f32) for r in accs]
    nin, nout, nacc = len(ins), len(outs), len(accs)

    def body(*refs):
        i = pl.program_id(1)
        tiles = [r[...] for r in refs[:nin]]
        o_vals, a_vals = fn(i, ni, *tiles)
        for r, v in zip(refs[nin:nin + nout], o_vals):
            r[...] = v.astype(r.dtype)
        if nacc:
            @pl.when(i == 0)
            def _():
                for r in refs[nin + nout:]:
                    r[...] = jnp.zeros_like(r)
            for r, v in zip(refs[nin + nout:], a_vals):
                r[...] += v

    res = pl.pallas_call(
        body, name=name, grid=(nj, ni), in_specs=in_specs, out_specs=out_specs, out_shape=out_shape,
        compiler_params=_cp(("parallel", "arbitrary")),
    )(*[s[0] for s in ins])
    return res


def _colsum(v):
    return jnp.sum(v, axis=0, keepdims=True)


def _sigmoid(x):
    return 1.0 / (1.0 + jnp.exp(-x))


_GELU_C = math.sqrt(2.0 / math.pi)


def _gelu(y):
    t = jnp.tanh(_GELU_C * (y + 0.044715 * y * y * y))
    return 0.5 * y * (1.0 + t)


def _gelu_grad(y):
    y2 = y * y
    t = jnp.tanh(_GELU_C * (y + 0.044715 * y * y2))
    return 0.5 * (1.0 + t) + 0.5 * y * (1.0 - t * t) * _GELU_C * (1.0 + 3.0 * 0.044715 * y2)


def _norm_mod_fwd(x, g, sc, sh, cfg, name):
    def fn(i, ni, xv, gv, scv, shv):
        rstd = lax.rsqrt(jnp.mean(xv * xv, axis=-1, keepdims=True) + NORM_EPS)
        return [xv * rstd * gv * (1.0 + scv) + shv], []
    return _rowwise(fn, [(x, "rc"), (g, "c"), (sc, "c"), (sh, "c")], [bf16], [], L=cfg.L, C=cfg.D, tl=256, tc=cfg.D, name=name)[0]


def _norm_mod_bwd(dh, x, g, sc, dres, cfg, name):
    def fn(i, ni, dhv, xv, gv, scv, *rest):
        dhv = dhv.astype(f32)
        rstd = lax.rsqrt(jnp.mean(xv * xv, axis=-1, keepdims=True) + NORM_EPS)
        xh = xv * rstd
        dxh = dhv * (gv * (1.0 + scv))
        dx = rstd * (dxh - xh * jnp.mean(dxh * xh, axis=-1, keepdims=True))
        if rest:
            dx = dx + rest[0]
        return [dx], [_colsum(dhv * xh), _colsum(dhv)]
    ins = [(dh, "rc"), (x, "rc"), (g, "c"), (sc, "c")] + ([(dres, "rc")] if dres is not None else [])
    return _rowwise(fn, ins, [f32], [1, 1], L=cfg.L, C=cfg.D, tl=256, tc=cfg.D, name=name)


def _final_loss(x, g, tgt, cfg):
    D = cfg.D

    def fn(i, ni, xv, gv, tv):
        rstd = lax.rsqrt(jnp.mean(xv * xv, axis=-1, keepdims=True) + NORM_EPS)
        xh = xv * rstd
        err = xh * gv - tv
        dy = err * (1.0 / D)
        dxh = dy * gv
        dx = rstd * (dxh - xh * jnp.mean(dxh * xh, axis=-1, keepdims=True))
        return [dx], [_colsum(dy * xh), _colsum(err * err)]
    return _rowwise(fn, [(x, "rc"), (g, "c"), (tgt, "rc")], [f32], [1, 1], L=cfg.L, C=D, tl=256, tc=D, name="final_loss")


def _gate_bwd(dx, out, gate, cfg, name):
    def fn(i, ni, dxv, ov, gv):
        return [dxv * gv], [_colsum(dxv * ov.astype(f32))]
    return _rowwise(fn, [(dx, "rc"), (out, "rc"), (gate, "c")], [bf16], [1], L=cfg.L, C=cfg.D, tl=512, tc=cfg.D, name=name)


def _glu_bwd(dz, g, pre, cfg):
    def fn(i, ni, dzv, gv, pv):
        dzv = dzv.astype(f32)
        gv = gv.astype(f32)
        s = _sigmoid(pv)
        dpre = dzv * gv * s * (1.0 - s)
        return [dpre, dzv * s], [_colsum(dpre)]
    return _rowwise(fn, [(dz, "rc"), (g, "rc"), (pre, "rc")], [bf16, f32], [1], L=cfg.L, C=cfg.D, tl=512, tc=cfg.D, name="glu_bwd")


def _shift_rows(av, pv, k, i):
    rows = lax.broadcasted_iota(jnp.int32, av.shape, 0)
    cur = pltpu.roll(av, k, 0)
    prev = pltpu.roll(pv, k, 0)
    prev = jnp.where(i > 0, prev, 0.0)
    prev_full = jnp.concatenate([prev, jnp.zeros((av.shape[0] - pv.shape[0], av.shape[1]), av.dtype)], axis=0) \
        if av.shape[0] > pv.shape[0] else prev
    return jnp.where(rows >= k, cur, prev_full)


def _shift_rows_up(av, nv, k, i, ni):
    n, h = av.shape[0], nv.shape[0]
    rows = lax.broadcasted_iota(jnp.int32, av.shape, 0)
    cur = pltpu.roll(av, n - k, 0)
    nxt = pltpu.roll(nv, h - k, 0)
    nxt = jnp.where(i < ni - 1, nxt, 0.0)
    nxt_full = jnp.concatenate([jnp.zeros((n - h, av.shape[1]), av.dtype), nxt], axis=0) if n > h else nxt
    return jnp.where(rows < n - k, cur, nxt_full)


def _conv3(av, pv, w, i):
    return w[0:1] * _shift_rows(av, pv, 2, i) + w[1:2] * _shift_rows(av, pv, 1, i) + w[2:3] * av


def _conv_act_fwd(a, conv_w, conv_b, cfg):
    F = cfg.F
    tc = _tile(F, 1408)
    nb = F // tc

    def fn(i, ni, au, av, pu, pv, wu, wv, bu, bv):
        cu = _conv3(au.astype(f32), pu.astype(f32), wu, i) + bu
        cv = _conv3(av.astype(f32), pv.astype(f32), wv, i) + bv
        return [cu * _sigmoid(cu) * cv], []
    ins = [(a, "rc"), (a, "rc", nb), (a, "prev"), (a, "prev", nb), (conv_w, "c3"), (conv_w, "c3", nb), (conv_b, "c"), (conv_b, "c", nb)]
    return _rowwise(fn, ins, [bf16], [], L=cfg.L, C=F, tl=512, tc=tc, name="conv_act_fwd")[0]


def _conv_act_bwd1(dact, a, conv_w, conv_b, cfg):
    F = cfg.F
    tc = _tile(F, 1408)
    nb = F // tc

    def fn(i, ni, dav, au, av, pu, pv, wu, wv, bu, bv):
        dav = dav.astype(f32)
        au, av, pu, pv = au.astype(f32), av.astype(f32), pu.astype(f32), pv.astype(f32)
        au1, au2 = _shift_rows(au, pu, 1, i), _shift_rows(au, pu, 2, i)
        av1, av2 = _shift_rows(av, pv, 1, i), _shift_rows(av, pv, 2, i)
        cu = wu[0:1] * au2 + wu[1:2] * au1 + wu[2:3] * au + bu
        cv = wv[0:1] * av2 + wv[1:2] * av1 + wv[2:3] * av + bv
        s = _sigmoid(cu)
        dcu = dav * cv * (s * (1.0 + cu * (1.0 - s)))
        dcv = dav * cu * s
        dwu = jnp.concatenate([_colsum(dcu * au2), _colsum(dcu * au1), _colsum(dcu * au)], axis=0)
        dwv = jnp.concatenate([_colsum(dcv * av2), _colsum(dcv * av1), _colsum(dcv * av)], axis=0)
        return [dcu, dcv], [dwu, dwv, _colsum(dcu), _colsum(dcv)]
    ins = [(dact, "rc"), (a, "rc"), (a, "rc", nb), (a, "prev"), (a, "prev", nb), (conv_w, "c3"), (conv_w, "c3", nb),
           (conv_b, "c"), (conv_b, "c", nb)]
    return _rowwise(fn, ins, [bf16, bf16], [3, 3, 1, 1], L=cfg.L, C=F, tl=512, tc=tc, name="conv_act_bwd1")


def _conv_bwd2(dc, w, cfg, name):
    F = cfg.F
    tc = _tile(F, 1408)

    def fn(i, ni, dcv, nxt, wv):
        dcv, nxt = dcv.astype(f32), nxt.astype(f32)
        return [wv[2:3] * dcv + wv[1:2] * _shift_rows_up(dcv, nxt, 1, i, ni) + wv[0:1] * _shift_rows_up(dcv, nxt, 2, i, ni)], []
    return _rowwise(fn, [(dc, "rc"), (dc, "next"), (w, "c3")], [bf16], [], L=cfg.L, C=F, tl=512, tc=tc, name=name)[0]


NSLAB = 8


def _s5_tables(abar_re, abar_im, lam_re, lam_im, step, cfg):
    J = cfg.G // 8
    expo = jnp.array([r + 1 for r in range(8)] + [8 * 2 ** p for p in range(8)], f32)[:, None, None]
    mag = jnp.exp(lam_re * step * expo)
    ang = lam_im * step * expo
    t_re = (mag * jnp.cos(ang)).reshape(16, J, 8 * cfg.P).transpose(1, 0, 2)
    t_im = (mag * jnp.sin(ang)).reshape(16, J, 8 * cfg.P).transpose(1, 0, 2)
    tab = jnp.concatenate([t_re, t_im], axis=-1)
    arow = jnp.concatenate([abar_re.reshape(J, 1, 8 * cfg.P), abar_im.reshape(J, 1, 8 * cfg.P)], axis=-1)
    return arow, tab


def _s5_mats(bbar_re, bbar_im, c_re, c_im, cfg):
    J, P, H = cfg.G // 8, cfg.P, cfg.H
    eye = jnp.eye(8, dtype=f32)

    def bd_in(bb):
        bb = bb.reshape(J, 8, P, H)
        return jnp.einsum("jgph,gk->jghkp", bb, eye).reshape(J, 8 * H, 8 * P)

    def bd_out(cc):
        cc = cc.reshape(J, 8, H, P)
        return jnp.einsum("jghp,gk->jgpkh", cc, eye).reshape(J, 8 * P, 8 * H)

    bmat = jnp.concatenate([bd_in(bbar_re), bd_in(bbar_im)], axis=2).astype(bf16)
    cmat = jnp.concatenate([bd_out(c_re), -bd_out(c_im)], axis=1).astype(bf16)
    return bmat, cmat


def _s5_unmats(dbmat, dcmat, cfg):
    J, P, H = cfg.G // 8, cfg.P, cfg.H
    eye = jnp.eye(8, dtype=f32)
    db = dbmat.reshape(J, 8, H, 2, 8, P)
    db = jnp.einsum("jghckp,gk->cjgph", db, eye).reshape(2, cfg.G, P, H)
    dc = dcmat.reshape(J, 2, 8, P, 8, H)
    dc = jnp.einsum("jcgpkh,gk->cjghp", dc, eye).reshape(2, cfg.G, H, P)
    return db[0], db[1], dc[0], -dc[1]


def _chunk_scan(x_ref, row0, nt, arow_ref, tab_ref, c0, reverse):
    sg = -1.0 if reverse else 1.0
    rows = lax.broadcasted_iota(jnp.int32, (nt, LANES), 0)
    order = list(range(7, -1, -1)) if reverse else list(range(8))

    def ld(k, r):
        return x_ref[k, pl.ds(row0 + r, nt, stride=8), :]

    def tab(row, k):
        return tab_ref[pl.ds(row, 1), pl.ds(k * LANES, LANES)]

    carries = [None] * NSLAB
    for k in range(4):
        ar = arow_ref[:, pl.ds(k * LANES, LANES)]
        ai = sg * arow_ref[:, pl.ds((4 + k) * LANES, LANES)]
        sr, si = ld(k, order[0]), ld(4 + k, order[0])
        for r in order[1:]:
            sr, si = ar * sr - ai * si + ld(k, r), ar * si + ai * sr + ld(4 + k, r)
        if reverse:
            cr = jnp.where(rows == nt - 1, c0[k], pltpu.roll(sr, nt - 1, 0))
            ci = jnp.where(rows == nt - 1, c0[4 + k], pltpu.roll(si, nt - 1, 0))
        else:
            cr = jnp.where(rows == 0, c0[k], pltpu.roll(sr, 1, 0))
            ci = jnp.where(rows == 0, c0[4 + k], pltpu.roll(si, 1, 0))
        d, p = 1, 0
        while d < nt:
            qr, qi = tab(8 + p, k), sg * tab(8 + p, 4 + k)
            if reverse:
                shr, shi, m = pltpu.roll(cr, nt - d, 0), pltpu.roll(ci, nt - d, 0), rows < nt - d
            else:
                shr, shi, m = pltpu.roll(cr, d, 0), pltpu.roll(ci, d, 0), rows >= d
            cr, ci = cr + jnp.where(m, qr * shr - qi * shi, 0.0), ci + jnp.where(m, qr * shi + qi * shr, 0.0)
            d, p = 2 * d, p + 1
        carries[k], carries[4 + k] = cr, ci
        sr, si = cr, ci
        for r in order:
            sr, si = ar * sr - ai * si + ld(k, r), ar * si + ai * sr + ld(4 + k, r)
            x_ref[k, pl.ds(row0 + r, nt, stride=8), :] = sr
            x_ref[4 + k, pl.ds(row0 + r, nt, stride=8), :] = si
    return carries


def _slabs_to_mat(x_ref, row0, n):
    return jnp.concatenate([x_ref[k, pl.ds(row0, n), :] for k in range(NSLAB)], axis=1)


def _mat_to_slabs(x_ref, row0, n, m):
    for k in range(NSLAB):
        x_ref[k, pl.ds(row0, n), :] = m[:, k * LANES:(k + 1) * LANES]


def _s5_fwd(u, bmat, cmat, drow, arow, tab, cfg):
    L, D, Tc = cfg.L, cfg.D, cfg.TC
    J, NC, nt = cfg.G // 8, L // Tc, Tc // 8
    W = NSLAB * LANES

    def body(u_ref, b_ref, c_ref, d_ref, a_ref, t_ref, y_ref, g_ref, cin_ref, x_ref, st_ref):
        c = pl.program_id(1)

        @pl.when(c == 0)
        def _():
            st_ref[...] = jnp.zeros_like(st_ref)

        cin_ref[...] = st_ref[...]
        ub = u_ref[...]
        _mat_to_slabs(x_ref, 0, Tc, jnp.dot(ub.astype(bf16), b_ref[...], preferred_element_type=f32))
        c0 = [st_ref[:, pl.ds(k * LANES, LANES)] for k in range(NSLAB)]
        _chunk_scan(x_ref, 0, nt, a_ref, t_ref, c0, False)
        for k in range(NSLAB):
            st_ref[:, pl.ds(k * LANES, LANES)] = x_ref[k, pl.ds(Tc - 1, 1), :]
        s = _slabs_to_mat(x_ref, 0, Tc).astype(bf16)
        y = jnp.dot(s, c_ref[...], preferred_element_type=f32) + d_ref[...] * ub
        y_ref[...] = y
        g_ref[...] = _gelu(y).astype(bf16)

    return pl.pallas_call(
        body, name="s5_fwd", grid=(J, NC),
        in_specs=[pl.BlockSpec((Tc, LANES), lambda j, c: (c, j)),
                  pl.BlockSpec((None, LANES, W), lambda j, c: (j, 0, 0)),
                  pl.BlockSpec((None, W, LANES), lambda j, c: (j, 0, 0)),
                  pl.BlockSpec((1, LANES), lambda j, c: (0, j)),
                  pl.BlockSpec((None, 1, W), lambda j, c: (j, 0, 0)),
                  pl.BlockSpec((None, 16, W), lambda j, c: (j, 0, 0))],
        out_specs=[pl.BlockSpec((Tc, LANES), lambda j, c: (c, j)),
                   pl.BlockSpec((Tc, LANES), lambda j, c: (c, j)),
                   pl.BlockSpec((None, None, 1, W), lambda j, c: (j, c, 0, 0))],
        out_shape=[jax.ShapeDtypeStruct((L, D), f32), jax.ShapeDtypeStruct((L, D), bf16),
                   jax.ShapeDtypeStruct((J, NC, 1, W), f32)],
        scratch_shapes=[pltpu.VMEM((NSLAB, Tc, LANES), f32), pltpu.VMEM((1, W), f32)],
        compiler_params=_cp(("parallel", "arbitrary")),
    )(u, bmat, cmat, drow, arow, tab)


def _s5_bwd(u, dy, cin, bmat, cmat, drow, arow, tab, cfg):
    L, D, Tc = cfg.L, cfg.D, cfg.TC
    J, NC, nt = cfg.G // 8, L // Tc, Tc // 8
    W = NSLAB * LANES
    PAD = 0

    def body(u_ref, dy_ref, cin_ref, b_ref, c_ref, d_ref, a_ref, t_ref,
             du_ref, db_ref, dc_ref, da_ref, dd_ref, s_ref, g_ref, gst_ref):
        c = pl.program_id(1)

        @pl.when(c == 0)
        def _():
            gst_ref[...] = jnp.zeros_like(gst_ref)
            db_ref[...] = jnp.zeros_like(db_ref)
            dc_ref[...] = jnp.zeros_like(dc_ref)
            da_ref[...] = jnp.zeros_like(da_ref)
            dd_ref[...] = jnp.zeros_like(dd_ref)

        ub, dyb = u_ref[...], dy_ref[...]
        ub16, dy16 = ub.astype(bf16), dyb.astype(bf16)
        _mat_to_slabs(s_ref, PAD, Tc, jnp.dot(ub16, b_ref[...], preferred_element_type=f32))
        c0 = [cin_ref[:, pl.ds(k * LANES, LANES)] for k in range(NSLAB)]
        tile_in = _chunk_scan(s_ref, PAD, nt, a_ref, t_ref, c0, False)
        _mat_to_slabs(g_ref, 0, Tc, lax.dot_general(dy16, c_ref[...], _DIMS["nt"], preferred_element_type=f32))
        g0 = [gst_ref[:, pl.ds(k * LANES, LANES)] for k in range(NSLAB)]
        _chunk_scan(g_ref, 0, nt, a_ref, t_ref, g0, True)
        for k in range(NSLAB):
            gst_ref[:, pl.ds(k * LANES, LANES)] = g_ref[k, pl.ds(0, 1), :]
        for k in range(4):
            acc_r = jnp.zeros((nt, LANES), f32)
            acc_i = jnp.zeros((nt, LANES), f32)
            for r in range(8):
                gr = g_ref[k, pl.ds(r, nt, stride=8), :]
                gi = g_ref[4 + k, pl.ds(r, nt, stride=8), :]
                if r == 0:
                    pr, pi = tile_in[k], tile_in[4 + k]
                else:
                    pr = s_ref[k, pl.ds(PAD + r - 1, nt, stride=8), :]
                    pi = s_ref[4 + k, pl.ds(PAD + r - 1, nt, stride=8), :]
                acc_r += gr * pr + gi * pi
                acc_i += gi * pr - gr * pi
            da_ref[:, pl.ds(k * LANES, LANES)] += _colsum(acc_r)
            da_ref[:, pl.ds((4 + k) * LANES, LANES)] += _colsum(acc_i)
        gm = _slabs_to_mat(g_ref, 0, Tc).astype(bf16)
        sm = _slabs_to_mat(s_ref, PAD, Tc).astype(bf16)
        du = lax.dot_general(gm, b_ref[...], _DIMS["nt"], preferred_element_type=f32) + d_ref[...] * dyb
        du_ref[...] = du.astype(bf16)
        db_ref[...] += lax.dot_general(ub16, gm, _DIMS["tn"], preferred_element_type=f32)
        dc_ref[...] += lax.dot_general(sm, dy16, _DIMS["tn"], preferred_element_type=f32)
        dd_ref[...] += _colsum(dyb * ub)

    rc = lambda j, c: (NC - 1 - c, j)
    return pl.pallas_call(
        body, name="s5_bwd", grid=(J, NC),
        in_specs=[pl.BlockSpec((Tc, LANES), rc), pl.BlockSpec((Tc, LANES), rc),
                  pl.BlockSpec((None, None, 1, W), lambda j, c: (j, NC - 1 - c, 0, 0)),
                  pl.BlockSpec((None, LANES, W), lambda j, c: (j, 0, 0)),
                  pl.BlockSpec((None, W, LANES), lambda j, c: (j, 0, 0)),
                  pl.BlockSpec((1, LANES), lambda j, c: (0, j)),
                  pl.BlockSpec((None, 1, W), lambda j, c: (j, 0, 0)),
                  pl.BlockSpec((None, 16, W), lambda j, c: (j, 0, 0))],
        out_specs=[pl.BlockSpec((Tc, LANES), rc),
                   pl.BlockSpec((None, LANES, W), lambda j, c: (j, 0, 0)),
                   pl.BlockSpec((None, W, LANES), lambda j, c: (j, 0, 0)),
                   pl.BlockSpec((None, 1, W), lambda j, c: (j, 0, 0)),
                   pl.BlockSpec((1, LANES), lambda j, c: (0, j))],
        out_shape=[jax.ShapeDtypeStruct((L, D), bf16), jax.ShapeDtypeStruct((J, LANES, W), f32),
                   jax.ShapeDtypeStruct((J, W, LANES), f32), jax.ShapeDtypeStruct((J, 1, W), f32),
                   jax.ShapeDtypeStruct((1, D), f32)],
        scratch_shapes=[pltpu.VMEM((NSLAB, Tc + PAD, LANES), f32), pltpu.VMEM((NSLAB, Tc, LANES), f32),
                        pltpu.VMEM((1, W), f32)],
        compiler_params=_cp(("parallel", "arbitrary")),
    )(u, dy, cin, bmat, cmat, drow, arow, tab)


NEG = -1e30


def _attn_logits(q_ref, k_ref, fq_ref, fk_ref, qi, ki, bq, scale):
    s = lax.dot_general(q_ref[...], k_ref[...], _DIMS["nt"], preferred_element_type=f32) * scale
    s = s + fq_ref[...] - fk_ref[...]
    rows = qi * bq + lax.broadcasted_iota(jnp.int32, s.shape, 0)
    cols = ki * bq + lax.broadcasted_iota(jnp.int32, s.shape, 1)
    return s, cols <= rows


def _attn_fwd(q, kv, fq, fk, cfg):
    L, D, NH, DH, B = cfg.L, cfg.D, cfg.NH, cfg.DH, cfg.BQ
    nq = L // B
    scale = DH ** -0.5

    def body(q_ref, k_ref, v_ref, fq_ref, fk_ref, o_ref, lse_ref, m_ref, l_ref, acc_ref):
        qi, ki = pl.program_id(1), pl.program_id(2)

        @pl.when(ki == 0)
        def _():
            m_ref[...] = jnp.full_like(m_ref, NEG)
            l_ref[...] = jnp.zeros_like(l_ref)
            acc_ref[...] = jnp.zeros_like(acc_ref)

        @pl.when(ki <= qi)
        def _():
            s, mask = _attn_logits(q_ref, k_ref, fq_ref, fk_ref, qi, ki, B, scale)
            s = jnp.where(mask, s, NEG)
            m_prev = m_ref[...]
            m_new = jnp.maximum(m_prev, jnp.max(s, axis=1, keepdims=True))
            alpha = jnp.exp(m_prev - m_new)
            p = jnp.exp(s - m_new)
            l_ref[...] = alpha * l_ref[...] + jnp.sum(p, axis=1, keepdims=True)
            acc_ref[...] = alpha * acc_ref[...] + jnp.dot(p.astype(bf16), v_ref[...], preferred_element_type=f32)
            m_ref[...] = m_new

        @pl.when(ki == qi)
        def _():
            o_ref[...] = (acc_ref[...] / l_ref[...]).astype(o_ref.dtype)
            lse_ref[...] = m_ref[...] + jnp.log(l_ref[...])

    kmap = lambda h, qi, ki: (jnp.minimum(ki, qi), h)
    vmap_ = lambda h, qi, ki: (jnp.minimum(ki, qi), NH + h)
    return pl.pallas_call(
        body, name="attn_fwd", grid=(NH, nq, nq),
        in_specs=[pl.BlockSpec((B, DH), lambda h, qi, ki: (qi, h)),
                  pl.BlockSpec((B, DH), kmap), pl.BlockSpec((B, DH), vmap_),
                  pl.BlockSpec((None, B, 1), lambda h, qi, ki: (h, qi, 0)),
                  pl.BlockSpec((None, 1, B), lambda h, qi, ki: (h, 0, jnp.minimum(ki, qi)))],
        out_specs=[pl.BlockSpec((B, DH), lambda h, qi, ki: (qi, h)),
                   pl.BlockSpec((None, B, 1), lambda h, qi, ki: (h, qi, 0))],
        out_shape=[jax.ShapeDtypeStruct((L, D), bf16), jax.ShapeDtypeStruct((NH, L, 1), f32)],
        scratch_shapes=[pltpu.VMEM((B, 1), f32), pltpu.VMEM((B, 1), f32), pltpu.VMEM((B, DH), f32)],
        compiler_params=_cp(("parallel", "parallel", "arbitrary")),
    )(q, kv, kv, fq, fk)


def _attn_bwd_dq(q, kv, do, o, lse, fq, fk, cfg):
    L, D, NH, DH, B = cfg.L, cfg.D, cfg.NH, cfg.DH, cfg.BQ
    nq = L // B
    scale = DH ** -0.5

    def body(q_ref, k_ref, v_ref, do_ref, o_ref, lse_ref, fq_ref, fk_ref, dq_ref, dfq_ref, acc_ref, df_ref, dl_ref):
        qi, ki = pl.program_id(1), pl.program_id(2)

        @pl.when(ki == 0)
        def _():
            dl_ref[...] = jnp.sum(do_ref[...].astype(f32) * o_ref[...].astype(f32), axis=1, keepdims=True)
            acc_ref[...] = jnp.zeros_like(acc_ref)
            df_ref[...] = jnp.zeros_like(df_ref)

        @pl.when(ki <= qi)
        def _():
            s, mask = _attn_logits(q_ref, k_ref, fq_ref, fk_ref, qi, ki, B, scale)
            p = jnp.where(mask, jnp.exp(s - lse_ref[...]), 0.0)
            dp = lax.dot_general(do_ref[...], v_ref[...], _DIMS["nt"], preferred_element_type=f32)
            ds = p * (dp - dl_ref[...])
            df_ref[...] += jnp.sum(ds, axis=1, keepdims=True)
            acc_ref[...] += jnp.dot(ds.astype(bf16), k_ref[...], preferred_element_type=f32)

        @pl.when(ki == qi)
        def _():
            dq_ref[...] = (acc_ref[...] * scale).astype(dq_ref.dtype)
            dfq_ref[...] = df_ref[...]

    qmap = lambda h, qi, ki: (qi, h)
    return pl.pallas_call(
        body, name="attn_bwd_dq", grid=(NH, nq, nq),
        in_specs=[pl.BlockSpec((B, DH), qmap),
                  pl.BlockSpec((B, DH), lambda h, qi, ki: (jnp.minimum(ki, qi), h)),
                  pl.BlockSpec((B, DH), lambda h, qi, ki: (jnp.minimum(ki, qi), NH + h)),
                  pl.BlockSpec((B, DH), qmap), pl.BlockSpec((B, DH), qmap),
                  pl.BlockSpec((None, B, 1), lambda h, qi, ki: (h, qi, 0)),
                  pl.BlockSpec((None, B, 1), lambda h, qi, ki: (h, qi, 0)),
                  pl.BlockSpec((None, 1, B), lambda h, qi, ki: (h, 0, jnp.minimum(ki, qi)))],
        out_specs=[pl.BlockSpec((B, DH), qmap), pl.BlockSpec((None, B, 1), lambda h, qi, ki: (h, qi, 0))],
        out_shape=[jax.ShapeDtypeStruct((L, D), bf16), jax.ShapeDtypeStruct((NH, L, 1), f32)],
        scratch_shapes=[pltpu.VMEM((B, DH), f32), pltpu.VMEM((B, 1), f32), pltpu.VMEM((B, 1), f32)],
        compiler_params=_cp(("parallel", "parallel", "arbitrary")),
    )(q, kv, kv, do, o, lse, fq, fk)


def _attn_bwd_dkv(q, kv, do, o, lse, fq, fk, cfg):
    L, D, NH, DH, B = cfg.L, cfg.D, cfg.NH, cfg.DH, cfg.BQ
    nq = L // B
    scale = DH ** -0.5

    def body(q_ref, k_ref, v_ref, do_ref, o_ref, lse_ref, fq_ref, fk_ref, dk_ref, dv_ref, dfk_ref, dka_ref, dva_ref, dfa_ref):
        ki, qi = pl.program_id(1), pl.program_id(2)

        @pl.when(qi == 0)
        def _():
            dka_ref[...] = jnp.zeros_like(dka_ref)
            dva_ref[...] = jnp.zeros_like(dva_ref)
            dfa_ref[...] = jnp.zeros_like(dfa_ref)

        @pl.when(qi >= ki)
        def _():
            do = do_ref[...]
            delta = jnp.sum(do.astype(f32) * o_ref[...].astype(f32), axis=1, keepdims=True)
            s, mask = _attn_logits(q_ref, k_ref, fq_ref, fk_ref, qi, ki, B, scale)
            p = jnp.where(mask, jnp.exp(s - lse_ref[...]), 0.0)
            dva_ref[...] += lax.dot_general(p.astype(bf16), do, _DIMS["tn"], preferred_element_type=f32)
            dp = lax.dot_general(do, v_ref[...], _DIMS["nt"], preferred_element_type=f32)
            ds = p * (dp - delta)
            dka_ref[...] += lax.dot_general(ds.astype(bf16), q_ref[...], _DIMS["tn"], preferred_element_type=f32)
            dfa_ref[...] -= jnp.sum(ds, axis=0, keepdims=True)

        @pl.when(qi == nq - 1)
        def _():
            dk_ref[...] = (dka_ref[...] * scale).astype(dk_ref.dtype)
            dv_ref[...] = dva_ref[...].astype(dv_ref.dtype)
            dfk_ref[...] = dfa_ref[...]

    qmap = lambda h, ki, qi: (jnp.maximum(qi, ki), h)
    fqmap = lambda h, ki, qi: (h, jnp.maximum(qi, ki), 0)
    return pl.pallas_call(
        body, name="attn_bwd_dkv", grid=(NH, nq, nq),
        in_specs=[pl.BlockSpec((B, DH), qmap),
                  pl.BlockSpec((B, DH), lambda h, ki, qi: (ki, h)),
                  pl.BlockSpec((B, DH), lambda h, ki, qi: (ki, NH + h)),
                  pl.BlockSpec((B, DH), qmap), pl.BlockSpec((B, DH), qmap),
                  pl.BlockSpec((None, B, 1), fqmap), pl.BlockSpec((None, B, 1), fqmap),
                  pl.BlockSpec((None, 1, B), lambda h, ki, qi: (h, 0, ki))],
        out_specs=[pl.BlockSpec((B, DH), lambda h, ki, qi: (ki, h)), pl.BlockSpec((B, DH), lambda h, ki, qi: (ki, h)),
                   pl.BlockSpec((None, 1, B), lambda h, ki, qi: (h, 0, ki))],
        out_shape=[jax.ShapeDtypeStruct((L, D), bf16), jax.ShapeDtypeStruct((L, D), bf16),
                   jax.ShapeDtypeStruct((NH, 1, L), f32)],
        scratch_shapes=[pltpu.VMEM((B, DH), f32), pltpu.VMEM((B, DH), f32), pltpu.VMEM((1, B), f32)],
        compiler_params=_cp(("parallel", "parallel", "arbitrary")),
    )(q, kv, kv, do, o, lse, fq, fk)


FCH = 256


def _split3(x):
    hi = x.astype(bf16)
    r1 = x - hi.astype(f32)
    mid = r1.astype(bf16)
    lo = (r1 - mid.astype(f32)).astype(bf16)
    return hi, mid, lo


def _tri_sum(tri, x):
    hi, mid, lo = _split3(x)
    return (jnp.dot(tri, hi, preferred_element_type=f32) + jnp.dot(tri, mid, preferred_element_type=f32)
            + jnp.dot(tri, lo, preferred_element_type=f32))


def _fgate_fwd(z, fb, cfg):
    L = cfg.L

    def body(z_ref, fb_ref, f_ref):
        r = lax.broadcasted_iota(jnp.int32, (FCH, FCH), 0)
        c = lax.broadcasted_iota(jnp.int32, (FCH, FCH), 1)
        tri = (c <= r).astype(bf16)
        carry = jnp.zeros((1, LANES), f32)
        for ch in range(L // FCH):
            x = z_ref[pl.ds(ch * FCH, FCH), :] + fb_ref[...]
            lf = jnp.minimum(x, 0.0) - jnp.log(1.0 + jnp.exp(-jnp.abs(x)))
            f_ref[pl.ds(ch * FCH, FCH), :] = _tri_sum(tri, lf) + carry
            carry = f_ref[pl.ds(ch * FCH + FCH - 1, 1), :]

    vm = pl.BlockSpec(memory_space=pltpu.VMEM)
    return pl.pallas_call(body, name="fgate_fwd", in_specs=[vm, vm], out_specs=vm,
                          out_shape=jax.ShapeDtypeStruct((L, LANES), f32), compiler_params=_cp())(z, fb)


def _fgate_bwd(df, z, fb, cfg):
    L = cfg.L

    def body(df_ref, z_ref, fb_ref, dz_ref, db_ref):
        r = lax.broadcasted_iota(jnp.int32, (FCH, FCH), 0)
        c = lax.broadcasted_iota(jnp.int32, (FCH, FCH), 1)
        tri = (c >= r).astype(bf16)
        carry = jnp.zeros((1, LANES), f32)
        dbs = jnp.zeros((1, LANES), f32)
        for ch in range(L // FCH - 1, -1, -1):
            suf = _tri_sum(tri, df_ref[pl.ds(ch * FCH, FCH), :]) + carry
            x = z_ref[pl.ds(ch * FCH, FCH), :] + fb_ref[...]
            dz = suf * _sigmoid(-x)
            dz_ref[pl.ds(ch * FCH, FCH), :] = dz
            dbs = dbs + _colsum(dz)
            carry = carry + _colsum(df_ref[pl.ds(ch * FCH, FCH), :])
        db_ref[...] = dbs

    vm = pl.BlockSpec(memory_space=pltpu.VMEM)
    return pl.pallas_call(body, name="fgate_bwd", in_specs=[vm, vm, vm], out_specs=[vm, vm],
                          out_shape=[jax.ShapeDtypeStruct((L, LANES), f32), jax.ShapeDtypeStruct((1, LANES), f32)],
                          compiler_params=_cp())(df, z, fb)


def _adamw(w, g, m, v, name):
    R, C = w.shape
    c1 = 1.0 - ADAM_B1 ** ADAM_STEP
    c2 = 1.0 - ADAM_B2 ** ADAM_STEP

    def fn(i, ni, wv, gv, mv, vv):
        mn = ADAM_B1 * mv + (1.0 - ADAM_B1) * gv
        vn = ADAM_B2 * vv + (1.0 - ADAM_B2) * (gv * gv)
        delta = -ADAM_LR * ((mn / c1) / (jnp.sqrt(vn / c2) + ADAM_EPS) + ADAM_WD * wv)
        return [delta, mn, vn], []
    tc = C if C % LANES else _tile(C, 1024)
    return _rowwise(fn, [(w, "rc"), (g, "rc"), (m, "rc"), (v, "rc")], [f32, f32, f32], [], L=R, C=C, tl=512, tc=tc, name=name)


def _sum_lead(x, out_dtype, name):
    n, R, C = x.shape
    tl = _tile(R, 512, HALO)
    tc = C if C % LANES else _tile(C, 1024)

    def body(x_ref, o_ref):
        acc = x_ref[0].astype(f32)
        for k in range(1, n):
            acc = acc + x_ref[k].astype(f32)
        o_ref[...] = acc.astype(o_ref.dtype)

    return pl.pallas_call(
        body, name=name, grid=(R // tl, C // tc),
        in_specs=[pl.BlockSpec((n, tl, tc), lambda i, j: (0, i, j))], out_specs=pl.BlockSpec((tl, tc), lambda i, j: (i, j)),
        out_shape=jax.ShapeDtypeStruct((R, C), out_dtype), compiler_params=_cp(("parallel", "parallel")),
    )(x)


def _add2(a, b, out_dtype, name):
    R, C = a.shape

    def fn(i, ni, av, bv):
        return [av.astype(f32) + bv.astype(f32)], []
    tc = C if C % LANES else _tile(C, 1024)
    return _rowwise(fn, [(a, "rc"), (b, "rc")], [out_dtype], [], L=R, C=C, tl=512, tc=tc, name=name)[0]


ANY = pl.BlockSpec(memory_space=pl.ANY)


def _place():
    x, y, c = lax.axis_index("x"), lax.axis_index("y"), lax.axis_index("c")
    return x, y, c


def _allgather8(blocks, name):
    n = len(blocks)

    def body(*refs):
        ins, outs = refs[:n], refs[n:2 * n]
        send_sems, recv_sems, local_sems = refs[2 * n:]
        x, y, c = _place()
        me, sibling = (x, y, c), (x, y, 1 - c)
        chips = [(1 - x, y), (x, 1 - y), (1 - x, 1 - y)]

        def slot(a, dev):
            return outs[a].at[4 * dev[0] + 2 * dev[1] + dev[2]]

        def copy(a, k, block, to, src=None):
            return pltpu.make_async_remote_copy(
                src_ref=slot(a, block) if src is None else src, dst_ref=slot(a, block),
                send_sem=send_sems.at[a * 7 + k], recv_sem=recv_sems.at[a * 7 + k], device_id=to, device_id_type=MESH)

        mine = [pltpu.make_async_copy(ins[a], slot(a, me), local_sems.at[a]) for a in range(n)]
        for cp in mine:
            cp.start()
        first = []
        for a in range(n):
            first.append(copy(a, 0, me, sibling, src=ins[a]))
            first += [copy(a, 1 + j, me, (*chip, c), src=ins[a]) for j, chip in enumerate(chips)]
        for cp in first:
            cp.start()
        passed = []
        for j, chip in enumerate(chips):
            for a in range(n):
                copy(a, 1 + j, (*chip, c), me).wait_recv()
                fwd = copy(a, 4 + j, (*chip, c), sibling)
                fwd.start()
                passed.append(fwd)
        for a in range(n):
            copy(a, 0, sibling, me).wait_recv()
        for j, chip in enumerate(chips):
            for a in range(n):
                copy(a, 4 + j, (*chip, 1 - c), me).wait_recv()
        for cp in first + passed:
            cp.wait_send()
        for cp in mine:
            cp.wait()

    outs = pl.pallas_call(
        body, name=name, in_specs=[ANY] * n, out_specs=[ANY] * n,
        out_shape=[jax.ShapeDtypeStruct((N_DEV,) + b.shape, b.dtype) for b in blocks],
        scratch_shapes=[pltpu.SemaphoreType.DMA((7 * n,)), pltpu.SemaphoreType.DMA((7 * n,)), pltpu.SemaphoreType.DMA((n,))],
    )(*blocks)
    return list(outs)


def _sibling_gather(halves, name):
    n = len(halves)

    def body(*refs):
        ins, outs = refs[:n], refs[n:2 * n]
        send_sems, recv_sems, local_sems = refs[2 * n:]
        x, y, c = _place()
        mine = [pltpu.make_async_copy(ins[a], outs[a].at[c], local_sems.at[a]) for a in range(n)]
        sends = [pltpu.make_async_remote_copy(src_ref=ins[a], dst_ref=outs[a].at[c], send_sem=send_sems.at[a],
                                              recv_sem=recv_sems.at[a], device_id=(x, y, 1 - c), device_id_type=MESH)
                 for a in range(n)]
        for cp in mine + sends:
            cp.start()
        for a in range(n):
            pltpu.make_async_remote_copy(src_ref=ins[a], dst_ref=outs[a].at[1 - c], send_sem=send_sems.at[a],
                                         recv_sem=recv_sems.at[a], device_id=(x, y, 1 - c), device_id_type=MESH).wait_recv()
        for cp in sends:
            cp.wait_send()
        for cp in mine:
            cp.wait()

    outs = pl.pallas_call(
        body, name=name, in_specs=[ANY] * n, out_specs=[ANY] * n,
        out_shape=[jax.ShapeDtypeStruct((2,) + h.shape, h.dtype) for h in halves],
        scratch_shapes=[pltpu.SemaphoreType.DMA((n,)), pltpu.SemaphoreType.DMA((n,)), pltpu.SemaphoreType.DMA((n,))],
    )(*halves)
    return list(outs)


def _sibling_swap_halves(grads, name):
    n = len(grads)

    def body(*refs):
        ins, outs = refs[:n], refs[n:2 * n]
        send_sems, recv_sems = refs[2 * n:]
        x, y, c = _place()
        sends = [pltpu.make_async_remote_copy(src_ref=ins[a].at[:, 1 - c], dst_ref=outs[a], send_sem=send_sems.at[a],
                                              recv_sem=recv_sems.at[a], device_id=(x, y, 1 - c), device_id_type=MESH)
                 for a in range(n)]
        for cp in sends:
            cp.start()
        for cp in sends:
            cp.wait_recv()
        for cp in sends:
            cp.wait_send()

    outs = pl.pallas_call(
        body, name=name, in_specs=[ANY] * n, out_specs=[ANY] * n,
        out_shape=[jax.ShapeDtypeStruct((4,) + g.shape[2:], g.dtype) for g in grads],
        scratch_shapes=[pltpu.SemaphoreType.DMA((n,)), pltpu.SemaphoreType.DMA((n,))],
    )(*grads)
    return list(outs)


def _chip_scatter(parts, name):
    n = len(parts)

    def body(*refs):
        ins, outs = refs[:n], refs[n:2 * n]
        send_sems, recv_sems, local_sems = refs[2 * n:]
        x, y, c = _place()
        me = 2 * x + y
        chips = [(1 - x, y), (x, 1 - y), (1 - x, 1 - y)]
        mine = [pltpu.make_async_copy(ins[a].at[me], outs[a].at[me], local_sems.at[a]) for a in range(n)]
        for cp in mine:
            cp.start()
        sends = []
        for a in range(n):
            for j, (px, py) in enumerate(chips):
                sends.append(pltpu.make_async_remote_copy(
                    src_ref=ins[a].at[2 * px + py], dst_ref=outs[a].at[me], send_sem=send_sems.at[a * 3 + j],
                    recv_sem=recv_sems.at[a * 3 + j], device_id=(px, py, c), device_id_type=MESH))
        for cp in sends:
            cp.start()
        for a in range(n):
            for j, (px, py) in enumerate(chips):
                pltpu.make_async_remote_copy(
                    src_ref=ins[a].at[me], dst_ref=outs[a].at[2 * px + py], send_sem=send_sems.at[a * 3 + j],
                    recv_sem=recv_sems.at[a * 3 + j], device_id=(px, py, c), device_id_type=MESH).wait_recv()
        for cp in sends:
            cp.wait_send()
        for cp in mine:
            cp.wait()

    outs = pl.pallas_call(
        body, name=name, in_specs=[ANY] * n, out_specs=[ANY] * n,
        out_shape=[jax.ShapeDtypeStruct(p.shape, p.dtype) for p in parts],
        scratch_shapes=[pltpu.SemaphoreType.DMA((3 * n,)), pltpu.SemaphoreType.DMA((3 * n,)), pltpu.SemaphoreType.DMA((n,))],
    )(*parts)
    return list(outs)


def _s5_discretize(log_step, a_re, a_im, b_re, b_im):
    step = jnp.exp(log_step)[:, None]
    mag = jnp.exp(a_re * step)
    abar_re = mag * jnp.cos(a_im * step)
    abar_im = mag * jnp.sin(a_im * step)
    den = a_re * a_re + a_im * a_im
    nr = abar_re - 1.0
    fr = (nr * a_re + abar_im * a_im) / den
    fi = (abar_im * a_re - nr * a_im) / den
    bbar_re = fr[..., None] * b_re - fi[..., None] * b_im
    bbar_im = fr[..., None] * b_im + fi[..., None] * b_re
    return abar_re, abar_im, bbar_re, bbar_im


def _s5_prepare(p, cfg):
    abar_re, abar_im, bbar_re, bbar_im = _s5_discretize(p["log_step"], p["a_re"], p["a_im"], p["b_re"], p["b_im"])
    step = jnp.exp(p["log_step"])[:, None]
    arow, tab = _s5_tables(abar_re, abar_im, p["a_re"], p["a_im"], step, cfg)
    bmat, cmat = _s5_mats(bbar_re, bbar_im, p["c_re"], p["c_im"], cfg)
    return dict(arow=arow, tab=tab, bmat=bmat, cmat=cmat, drow=p["d"].reshape(1, cfg.D))


def _s5_param_grads(p, dbmat, dcmat, dabar, dd, cfg):
    J, P = cfg.G // 8, cfg.P
    dbb_re, dbb_im, dc_re, dc_im = _s5_unmats(dbmat, dcmat, cfg)
    da = dabar.reshape(J, 2, 8, P)
    da_re, da_im = da[:, 0].reshape(cfg.G, P), da[:, 1].reshape(cfg.G, P)
    _, vjp = jax.vjp(_s5_discretize, p["log_step"], p["a_re"], p["a_im"], p["b_re"], p["b_im"])
    dls, dare, daim, dbre, dbim = vjp((da_re, da_im, dbb_re, dbb_im))
    return dict(log_step=dls, a_re=dare, a_im=daim, b_re=dbre, b_im=dbim, c_re=dc_re, c_im=dc_im, d=dd.reshape(cfg.G, cfg.H))


def _resid_epi(acc, xv, gv):
    return xv + gv * acc, acc


def _ffn_fwd(x_in, g_norm, sc, sh, gate, w_up4, w_down, conv_w, conv_b, cfg, tag):
    h = _norm_mod_fwd(x_in, g_norm, sc, sh, cfg, f"ffn_norm_{tag}")
    a = _mm(h, w_up4, mode="nn", b4=True, tn=1408, out_dtypes=(bf16,), name=f"ffn_up_{tag}")
    act = _conv_act_fwd(a, conv_w, conv_b, cfg)
    x_out, out = _mm(act, w_down, mode="nn", extras=[(x_in, "mn"), (gate, "n")], epi=_resid_epi,
                     out_dtypes=(f32, bf16), name=f"ffn_down_{tag}")
    return x_out, dict(h=h, a=a, act=act, out=out)


def _ffn_bwd(dx, x_in, sv, g_norm, sc, gate, w_up4, w_down, conv_w, conv_b, cfg, tag):
    F = cfg.F
    dout, dgate = _gate_bwd(dx, sv["out"], gate, cfg, f"ffn_gate_bwd_{tag}")
    dact = _mm(dout, w_down, mode="nt", out_dtypes=(bf16,), name=f"ffn_dact_{tag}")
    dw_down = _mm(sv["act"], dout, mode="tn", out_dtypes=(bf16,), name=f"ffn_dwdown_{tag}")
    dcu, dcv, dwu, dwv, dbu, dbv = _conv_act_bwd1(dact, sv["a"], conv_w, conv_b, cfg)
    dau = _conv_bwd2(dcu, conv_w[:, :F], cfg, f"conv_bwd2u_{tag}")
    dav = _conv_bwd2(dcv, conv_w[:, F:], cfg, f"conv_bwd2v_{tag}")
    da = jnp.concatenate([dau, dav], axis=1)
    dh = _mm(da, w_up4, mode="nt", b4=True, tk=1408, out_dtypes=(bf16,), name=f"ffn_dh_{tag}")
    dw_up = _mm(sv["h"], da, mode="tn", out4=True, tn=1408, out_dtypes=(bf16,), name=f"ffn_dwup_{tag}")
    dx_in, A, B = _norm_mod_bwd(dh, x_in, g_norm, sc, dx, cfg, f"ffn_norm_bwd_{tag}")
    small = dict(norm_g=(1.0 + sc) * A, sc=g_norm * A, sh=B, gate=dgate,
                 conv_w=jnp.concatenate([dwu, dwv], axis=1), conv_b=jnp.concatenate([dbu, dbv], axis=1))
    return dx_in, dw_up, dw_down, small


def _local_step(cfg, x, tgt, mod, W, sp):
    D, NH = cfg.D, cfg.NH
    row = lambda v: v.reshape(1, -1)
    nmg0, nmg1 = row(sp["norm_mix_g"][0]), row(sp["norm_mix_g"][1])
    nfg0, nfg1 = row(sp["norm_ffn_g"][0]), row(sp["norm_ffn_g"][1])
    kvg, fng = row(sp["kv_norm_g"]), row(sp["final_norm_g"])
    cw0, cw1 = sp["ffn_conv_w"][0], sp["ffn_conv_w"][1]
    cb0, cb1 = row(sp["ffn_conv_b"][0]), row(sp["ffn_conv_b"][1])
    glu_b = row(sp["ssm_glu_b"])
    fb = jnp.zeros((1, LANES), f32).at[0, :NH].set(sp["forget_b"])
    s5p = {k: sp["ssm_" + k][0] for k in ("log_step", "a_re", "a_im", "b_re", "b_im", "c_re", "c_im", "d")}
    s5 = _s5_prepare(s5p, cfg)
    m0, m1 = mod["l0"], mod["l1"]

    h0 = _norm_mod_fwd(x, nmg0, m0["sc_m"], m0["sh_m"], cfg, "mix_norm_0")
    u = _mm(h0, W["ssm_w_in"], mode="nn", name="ssm_in")
    y, gact, cin = _s5_fwd(u, s5["bmat"], s5["cmat"], s5["drow"], s5["arow"], s5["tab"], cfg)

    def glu_epi(acc, bv, gv):
        pre = acc + bv
        return pre, gv.astype(f32) * _sigmoid(pre)
    pre, z = _mm(gact, W["ssm_glu_w"], mode="nn", extras=[(glu_b, "n"), (gact, "mn")], epi=glu_epi,
                 out_dtypes=(f32, bf16), name="ssm_glu")
    x1, out_m0 = _mm(z, W["ssm_w_out"], mode="nn", extras=[(x, "mn"), (m0["g_m"], "n")], epi=_resid_epi,
                     out_dtypes=(f32, bf16), name="ssm_out")
    x2, ffn0 = _ffn_fwd(x1, nfg0, m0["sc_f"], m0["sh_f"], m0["g_f"], W["ffn_w_up0"], W["ffn_w_down0"], cw0, cb0, cfg, "0")

    hk = _norm_mod_fwd(x2, kvg, mod["sc_kv"], mod["sh_kv"], cfg, "kv_norm")
    kvb = _mm(hk, W["kv_w"], mode="nn", out_dtypes=(bf16,), name="kv_proj")
    zf = _mm(hk, W["kv_wf"], mode="nn", name="kv_fproj")
    fc = _fgate_fwd(zf, fb, cfg)
    fct = fc[:, :NH].T
    fq, fk = fct[:, :, None], fct[:, None, :]

    h1 = _norm_mod_fwd(x2, nmg1, m1["sc_m"], m1["sh_m"], cfg, "mix_norm_1")
    q = _mm(h1, W["attn_w_q"], mode="nn", out_dtypes=(bf16,), name="attn_q")
    o, lse = _attn_fwd(q, kvb, fq, fk, cfg)
    x3, out_m1 = _mm(o, W["attn_w_out"], mode="nn", extras=[(x2, "mn"), (m1["g_m"], "n")], epi=_resid_epi,
                     out_dtypes=(f32, bf16), name="attn_out")
    x4, ffn1 = _ffn_fwd(x3, nfg1, m1["sc_f"], m1["sh_f"], m1["g_f"], W["ffn_w_up1"], W["ffn_w_down1"], cw1, cb1, cfg, "1")

    dx, dfng, lcol = _final_loss(x4, fng, tgt, cfg)
    loss = (0.5 / D) * jnp.sum(lcol)

    dx, dw_up1, dw_down1, sf1 = _ffn_bwd(dx, x3, ffn1, nfg1, m1["sc_f"], m1["g_f"], W["ffn_w_up1"], W["ffn_w_down1"], cw1, cb1, cfg, "1")
    dout, dgm1 = _gate_bwd(dx, out_m1, m1["g_m"], cfg, "attn_gate_bwd")
    do = _mm(dout, W["attn_w_out"], mode="nt", out_dtypes=(bf16,), name="attn_do")
    dw_ao = _mm(o, dout, mode="tn", out_dtypes=(bf16,), name="attn_dwout")
    dq, dfq = _attn_bwd_dq(q, kvb, do, o, lse, fq, fk, cfg)
    dk, dv, dfk = _attn_bwd_dkv(q, kvb, do, o, lse, fq, fk, cfg)
    dh1 = _mm(dq, W["attn_w_q"], mode="nt", out_dtypes=(bf16,), name="attn_dh")
    dw_q = _mm(h1, dq, mode="tn", out_dtypes=(bf16,), name="attn_dwq")
    dx, A1, B1 = _norm_mod_bwd(dh1, x2, nmg1, m1["sc_m"], dx, cfg, "mix_norm_bwd_1")

    dfc = jnp.pad((dfq[:, :, 0] + dfk[:, 0, :]).T, ((0, 0), (0, LANES - NH)))
    dzf, dfb = _fgate_bwd(dfc, zf, fb, cfg)
    dkv = jnp.concatenate([dk, dv], axis=1)
    dhk1 = _mm(dkv, W["kv_w"], mode="nt", name="kv_dh1")
    dhk = _mm(dzf, W["kv_wf"], mode="nt", extras=[(dhk1, "mn")], epi=lambda acc, e: (acc + e,), out_dtypes=(bf16,), name="kv_dh2")
    dw_kv = _mm(hk, dkv, mode="tn", out_dtypes=(bf16,), name="kv_dw")
    dw_kf = _mm(hk, dzf, mode="tn", out_dtypes=(bf16,), name="kv_dwf")
    dx, Ak, Bk = _norm_mod_bwd(dhk, x2, kvg, mod["sc_kv"], dx, cfg, "kv_norm_bwd")

    dx, dw_up0, dw_down0, sf0 = _ffn_bwd(dx, x1, ffn0, nfg0, m0["sc_f"], m0["g_f"], W["ffn_w_up0"], W["ffn_w_down0"], cw0, cb0, cfg, "0")
    dout, dgm0 = _gate_bwd(dx, out_m0, m0["g_m"], cfg, "ssm_gate_bwd")
    dz = _mm(dout, W["ssm_w_out"], mode="nt", out_dtypes=(bf16,), name="ssm_dz")
    dw_so = _mm(z, dout, mode="tn", out_dtypes=(bf16,), name="ssm_dwout")
    dpre, dgd, dglub = _glu_bwd(dz, gact, pre, cfg)
    dy = _mm(dpre, W["ssm_glu_w"], mode="nt", extras=[(dgd, "mn"), (y, "mn")],
             epi=lambda acc, e, yv: ((acc + e) * _gelu_grad(yv),), name="ssm_dy")
    dw_glu = _mm(gact, dpre, mode="tn", out_dtypes=(bf16,), name="ssm_dwglu")
    du, dbm, dcm, dab, dd = _s5_bwd(u, dy, cin, s5["bmat"], s5["cmat"], s5["drow"], s5["arow"], s5["tab"], cfg)
    dh0 = _mm(du, W["ssm_w_in"], mode="nt", out_dtypes=(bf16,), name="ssm_dh")
    dw_in = _mm(h0, du, mode="tn", out_dtypes=(bf16,), name="ssm_dwin")
    dx, A0, B0 = _norm_mod_bwd(dh0, x, nmg0, m0["sc_m"], dx, cfg, "mix_norm_bwd_0")

    s5g = _s5_param_grads(s5p, dbm, dcm, dab, dd, cfg)
    big = dict(ssm_w_in=dw_in, ssm_glu_w=dw_glu, ssm_w_out=dw_so, attn_w_q=dw_q, attn_w_out=dw_ao,
               ffn_w_up0=dw_up0, ffn_w_up1=dw_up1, ffn_w_down0=dw_down0, ffn_w_down1=dw_down1,
               kv_w=jnp.concatenate([dw_kv, dw_kf[:, :NH]], axis=1))
    small = dict(
        norm_mix_g=jnp.concatenate([(1.0 + m0["sc_m"]) * A0, (1.0 + m1["sc_m"]) * A1], axis=0),
        norm_ffn_g=jnp.concatenate([sf0["norm_g"], sf1["norm_g"]], axis=0),
        ssm_glu_b=dglub, kv_norm_g=(1.0 + mod["sc_kv"]) * Ak, forget_b=dfb[0, :NH],
        ffn_conv_w=jnp.stack([sf0["conv_w"], sf1["conv_w"]]), ffn_conv_b=jnp.concatenate([sf0["conv_b"], sf1["conv_b"]], axis=0),
        final_norm_g=dfng, **{"ssm_" + k: v[None] for k, v in s5g.items()})
    dmod = [jnp.concatenate([B0, nmg0 * A0, dgm0, sf0["sh"], sf0["sc"], sf0["gate"]], axis=1),
            jnp.concatenate([B1, nmg1 * A1, dgm1, sf1["sh"], sf1["sc"], sf1["gate"]], axis=1),
            jnp.concatenate([Bk, kvg * Ak], axis=1)]
    return loss, dx, big, small, dmod


WEIGHTS = ["mod_w", "mod_b", "norm_mix_g", "norm_ffn_g", "ssm_w_in", "ssm_log_step", "ssm_a_re", "ssm_a_im", "ssm_b_re",
           "ssm_b_im", "ssm_c_re", "ssm_c_im", "ssm_d", "ssm_glu_w", "ssm_glu_b", "ssm_w_out", "kv_mod_w", "kv_mod_b",
           "kv_norm_g", "kv_w", "forget_b", "attn_w_q", "attn_w_out", "ffn_w_up", "ffn_conv_w", "ffn_conv_b", "ffn_w_down",
           "final_norm_g"]
ARGS = ["x", "c"] + WEIGHTS + ["loss_target"] + ["m_" + n for n in WEIGHTS] + ["v_" + n for n in WEIGHTS]
SMALL = ["mod_b", "norm_mix_g", "norm_ffn_g", "ssm_log_step", "ssm_a_re", "ssm_a_im", "ssm_b_re", "ssm_b_im", "ssm_c_re",
         "ssm_c_im", "ssm_d", "ssm_glu_b", "kv_mod_b", "kv_norm_g", "forget_b", "ffn_conv_w", "ffn_conv_b", "final_norm_g"]
PACK_ROWS = 16


def _pack(arrs):
    flat = jnp.concatenate([a.reshape(-1).astype(f32) for a in arrs])
    unit = PACK_ROWS * LANES
    n = -(-flat.shape[0] // unit) * unit
    return jnp.pad(flat, (0, n - flat.shape[0])).reshape(-1, LANES)


def _unpack(packed, shapes):
    flat, out, off = packed.reshape(-1), [], 0
    for s in shapes:
        n = math.prod(s)
        out.append(flat[off:off + n].reshape(s))
        off += n
    return out


def _silu(v):
    return v * _sigmoid(v)


def _half(w, c, axis):
    r = w.shape[axis] // 2
    return lax.dynamic_slice_in_dim(w, c * r, r, axis=axis)


def kernel(x, c, mod_w, mod_b, norm_mix_g, norm_ffn_g, ssm_w_in, ssm_log_step, ssm_a_re, ssm_a_im, ssm_b_re, ssm_b_im, ssm_c_re, ssm_c_im, ssm_d, ssm_glu_w, ssm_glu_b, ssm_w_out, kv_mod_w, kv_mod_b, kv_norm_g, kv_w, forget_b, attn_w_q, attn_w_out, ffn_w_up, ffn_conv_w, ffn_conv_b, ffn_w_down, final_norm_g, loss_target, m_mod_w, m_mod_b, m_norm_mix_g, m_norm_ffn_g, m_ssm_w_in, m_ssm_log_step, m_ssm_a_re, m_ssm_a_im, m_ssm_b_re, m_ssm_b_im, m_ssm_c_re, m_ssm_c_im, m_ssm_d, m_ssm_glu_w, m_ssm_glu_b, m_ssm_w_out, m_kv_mod_w, m_kv_mod_b, m_kv_norm_g, m_kv_w, m_forget_b, m_attn_w_q, m_attn_w_out, m_ffn_w_up, m_ffn_conv_w, m_ffn_conv_b, m_ffn_w_down, m_final_norm_g, v_mod_w, v_mod_b, v_norm_mix_g, v_norm_ffn_g, v_ssm_w_in, v_ssm_log_step, v_ssm_a_re, v_ssm_a_im, v_ssm_b_re, v_ssm_b_im, v_ssm_c_re, v_ssm_c_im, v_ssm_d, v_ssm_glu_w, v_ssm_glu_b, v_ssm_w_out, v_kv_mod_w, v_kv_mod_b, v_kv_norm_g, v_kv_w, v_forget_b, v_attn_w_q, v_attn_w_out, v_ffn_w_up, v_ffn_conv_w, v_ffn_conv_b, v_ffn_w_down, v_final_norm_g):
    a = dict(locals())
    assert list(a) == ARGS
    return _step(CFG, a)


def _step(cfg, a):
    D, F, NH = cfg.D, cfg.F, cfg.NH
    x_, y_, c_ = _place()
    chip, dev = 2 * x_ + y_, 4 * x_ + 2 * y_ + c_

    big_src = dict(ssm_w_in=a["ssm_w_in"][0], ssm_glu_w=a["ssm_glu_w"][0], ssm_w_out=a["ssm_w_out"][0],
                   attn_w_q=a["attn_w_q"][0], attn_w_out=a["attn_w_out"][0],
                   ffn_w_up0=a["ffn_w_up"][0], ffn_w_up1=a["ffn_w_up"][1],
                   ffn_w_down0=a["ffn_w_down"][0], ffn_w_down1=a["ffn_w_down"][1], kv_w=a["kv_w"])
    big_names = list(big_src)
    blocks = [_half(big_src[n], c_, 0).astype(bf16) for n in big_names]
    blocks += [_half(a["ssm_glu_b"], c_, 1), _half(a["ffn_conv_w"], c_, 2), a["c"]]
    got = _allgather8(blocks, "gather_weights")
    W = {}
    for n, g in zip(big_names, got):
        if n.startswith("ffn_w_up"):
            W[n] = g.reshape(4, D, 2 * F // 4)
        elif n == "kv_w":
            full = g.reshape(4, D, -1).transpose(1, 0, 2).reshape(D, -1)
            W["kv_w"] = full[:, :2 * D]
            W["kv_wf"] = jnp.pad(full[:, 2 * D:], ((0, 0), (0, LANES - NH)))
        else:
            W[n] = g.reshape(-1, D)
    glu_b_full = got[-3].reshape(D)
    conv_w_full = got[-2].transpose(1, 2, 0, 3).reshape(2, 3, 2 * F)
    c16 = jnp.pad(got[-1].reshape(N_DEV, D), ((0, 16 - N_DEV), (0, 0)))

    mcols = [_mm(c16, a["mod_w"][l], mode="nn", a_pro=_silu, name=f"mod_fwd_{l}") for l in range(2)]
    mcols.append(_mm(c16, a["kv_mod_w"], mode="nn", a_pro=_silu, name="mod_fwd_kv"))
    widths = [m.shape[1] for m in mcols]
    mall = _allgather8([jnp.concatenate(mcols, axis=1)[:N_DEV]], "gather_mod")[0][0::2]
    offs = [0, widths[0], widths[0] + widths[1]]
    rows = []
    for off, wd, bias in zip(offs, widths, [a["mod_b"][0], a["mod_b"][1], a["kv_mod_b"]]):
        fullm = mall[:, :, off:off + wd].transpose(1, 0, 2).reshape(N_DEV, 4 * wd) + bias
        rows.append(lax.dynamic_slice_in_dim(fullm, dev, 1, axis=0))
    mod = {}
    for l in range(2):
        mod[f"l{l}"] = dict(zip(["sh_m", "sc_m", "g_m", "sh_f", "sc_f", "g_f"], jnp.split(rows[l], 6, axis=1)))
    mod["sh_kv"], mod["sc_kv"] = jnp.split(rows[2], 2, axis=1)

    sp = {n: a[n] for n in ["norm_mix_g", "norm_ffn_g", "kv_norm_g", "final_norm_g", "ffn_conv_b", "forget_b", "ssm_log_step",
                            "ssm_a_re", "ssm_a_im", "ssm_b_re", "ssm_b_im", "ssm_c_re", "ssm_c_im", "ssm_d"]}
    sp["ssm_glu_b"], sp["ffn_conv_w"] = glu_b_full, conv_w_full
    loss, dx, big, small, dmod = _local_step(cfg, a["x"][0], a["loss_target"][0], mod, W, sp)
    loss = lax.psum(loss, ("x", "y", "c"))

    small["mod_b"] = jnp.concatenate([dmod[0], dmod[1]], axis=0)
    small["kv_mod_b"] = dmod[2]
    shapes = [(2, 6 * D) if n == "mod_b" else (1, D) if n == "ssm_glu_b" else (2, 3, 2 * F) if n == "ffn_conv_w"
              else a[n].shape for n in SMALL]
    packs = _allgather8([_pack([small[n] for n in SMALL])], "gather_small")[0]
    gsmall = dict(zip(SMALL, _unpack(_sum_lead(packs, f32, "sum_small"), shapes)))
    per_dev = packs.reshape(N_DEV, -1)
    sizes = [math.prod(s) for s in shapes]
    starts = dict(zip(SMALL, [sum(sizes[:i]) for i in range(len(sizes))]))

    def rows_of(name, l, width):
        st = starts[name] + l * 6 * D
        blk = lax.dynamic_slice(per_dev, (0, st + chip * width), (N_DEV, width))
        return jnp.pad(blk, ((0, 16 - N_DEV), (0, 0)))
    g_mod_w = jnp.stack([_mm(c16, rows_of("mod_b", l, 6 * D // 4), mode="tn", a_pro=_silu, name=f"mod_dw_{l}") for l in range(2)])
    g_kv_mod_w = _mm(c16, rows_of("kv_mod_b", 0, 2 * D // 4), mode="tn", a_pro=_silu, name="mod_dw_kv")
    gsmall["ssm_glu_b"] = lax.dynamic_slice_in_dim(gsmall["ssm_glu_b"], chip * (D // 4), D // 4, axis=1)
    gsmall["ffn_conv_w"] = lax.dynamic_slice_in_dim(gsmall["ffn_conv_w"], chip * (2 * F // 4), 2 * F // 4, axis=2)

    def blocks_of(n, g):
        if n.startswith("ffn_w_up"):
            return g.reshape(4, 2, D // 2, -1)
        if n == "kv_w":
            return g.reshape(D, 4, -1).transpose(1, 0, 2).reshape(4, 2, D // 2, -1)
        return g.reshape(4, 2, g.shape[0] // 8, g.shape[1])
    gb = [blocks_of(n, big[n]) for n in big_names]
    recv = _sibling_swap_halves(gb, "grad_sibling_swap")
    parts = []
    for n, g, r in zip(big_names, gb, recv):
        keep = lax.dynamic_index_in_dim(g, c_, axis=1, keepdims=False)
        rr, cc = keep.shape[1], keep.shape[2]
        parts.append(_add2(keep.reshape(4 * rr, cc), r.reshape(4 * rr, cc), bf16, f"grad_add_{n}").reshape(4, rr, cc))
    scattered = _chip_scatter(parts, "grad_chip_scatter")
    halves = [_sum_lead(q, f32, f"grad_sum_{n}") for n, q in zip(big_names, scattered)]
    full = _sibling_gather(halves, "grad_sibling_gather")
    gbig = {n: f.reshape(2 * f.shape[1], f.shape[2]) for n, f in zip(big_names, full)}

    grads = dict(gsmall)
    grads["mod_w"], grads["kv_mod_w"], grads["kv_w"] = g_mod_w, g_kv_mod_w, gbig["kv_w"]
    for n in ["ssm_w_in", "ssm_glu_w", "ssm_w_out", "attn_w_q", "attn_w_out"]:
        grads[n] = gbig[n][None]
    grads["ffn_w_up"] = jnp.stack([gbig["ffn_w_up0"], gbig["ffn_w_up1"]])
    grads["ffn_w_down"] = jnp.stack([gbig["ffn_w_down0"], gbig["ffn_w_down1"]])
    grads = {n: grads[n].reshape(a[n].shape) for n in WEIGHTS}

    delta, new_m, new_v = {}, {}, {}
    for n in WEIGHTS:
        if n in SMALL:
            continue
        shp = a[n].shape
        two = lambda t: t.reshape(-1, shp[-1])
        d_, m_, v_ = _adamw(two(a[n]), two(grads[n]), two(a["m_" + n]), two(a["v_" + n]), f"adamw_{n}")
        delta[n], new_m[n], new_v[n] = d_.reshape(shp), m_.reshape(shp), v_.reshape(shp)
    sshapes = [a[n].shape for n in SMALL]
    d_, m_, v_ = _adamw(_pack([a[n] for n in SMALL]), _pack([grads[n] for n in SMALL]), _pack([a["m_" + n] for n in SMALL]),
                        _pack([a["v_" + n] for n in SMALL]), "adamw_small")
    for n, dd_, mm_, vv_ in zip(SMALL, _unpack(d_, sshapes), _unpack(m_, sshapes), _unpack(v_, sshapes)):
        delta[n], new_m[n], new_v[n] = dd_, mm_, vv_

    return (loss, dx[None], *[grads[n] for n in WEIGHTS], *[delta[n] for n in WEIGHTS],
            *[new_m[n] for n in WEIGHTS], *[new_v[n] for n in WEIGHTS])
```

```python
import collections
import functools
import math

import jax
import jax.numpy as jnp
from jax import lax
from jax.experimental import pallas as pl
from jax.experimental.pallas import tpu as pltpu

f32 = jnp.float32
bf16 = jnp.bfloat16
MESH = pl.DeviceIdType.MESH

LANES = 128
SUBLANES = 8
VMEM_BYTES_V7X = 64 * 1024 * 1024
VMEM_LIMIT = 56 * 1024 * 1024

Cfg = collections.namedtuple("Cfg", "L D G P H NH DH F TC BQ")
CFG = Cfg(L=4096, D=2048, G=128, P=64, H=16, NH=16, DH=128, F=5632, TC=512, BQ=256)
NORM_EPS = 1e-6
ADAM_LR, ADAM_B1, ADAM_B2, ADAM_EPS, ADAM_WD, ADAM_STEP = 0.001, 0.9, 0.999, 1e-08, 0.01, 10
N_DEV = 8


def _cp(sem=None):
    return pltpu.CompilerParams(dimension_semantics=sem, vmem_limit_bytes=VMEM_LIMIT)


def _tile(dim, pref, unit=LANES):
    if dim <= pref:
        return dim
    t = (pref // unit) * unit
    while t > unit and dim % t:
        t -= unit
    assert dim % t == 0, (dim, pref)
    return t


_DIMS = {"nn": (((1,), (0,)), ((), ())), "nt": (((1,), (1,)), ((), ())), "tn": (((0,), (0,)), ((), ()))}


def _mm(a, b, *, mode, name, tm=1024, tn=1024, tk=512, b4=False, out4=False, a_pro=None, extras=(), epi=None,
        out_dtypes=(f32,)):
    if mode == "tn":
        K, M = a.shape
    else:
        M, K = a.shape
    if b4:
        R, c4 = b.shape[1], b.shape[2]
        N = R if mode == "nt" else 4 * c4
        assert (K == 4 * c4) if mode == "nt" else (K == R)
    else:
        N = b.shape[0] if mode == "nt" else b.shape[1]
        assert K == (b.shape[1] if mode == "nt" else b.shape[0])
    n4 = N // 4
    tm = _tile(M, tm, LANES if mode == "tn" else SUBLANES * 2)
    tn = _tile(n4 if out4 or (b4 and mode != "nt") else N, tn)
    tk = _tile(b.shape[2] if (b4 and mode == "nt") else K, tk)
    nm, nn_, nk = M // tm, N // tn, K // tk

    a_spec = pl.BlockSpec((tk, tm), lambda i, j, k: (k, i)) if mode == "tn" else pl.BlockSpec((tm, tk), lambda i, j, k: (i, k))
    if b4 and mode == "nt":
        q = b.shape[2] // tk
        b_spec = pl.BlockSpec((None, tn, tk), lambda i, j, k: (lax.div(k, q), j, lax.rem(k, q)))
    elif b4:
        q = b.shape[2] // tn
        b_spec = pl.BlockSpec((None, tk, tn), lambda i, j, k: (lax.div(j, q), k, lax.rem(j, q)))
    elif mode == "nt":
        b_spec = pl.BlockSpec((tn, tk), lambda i, j, k: (j, k))
    else:
        b_spec = pl.BlockSpec((tk, tn), lambda i, j, k: (k, j))
    ex_specs = []
    for arr, kind in extras:
        if kind == "mn":
            ex_specs.append(pl.BlockSpec((tm, tn), lambda i, j, k: (i, j)))
        else:
            ex_specs.append(pl.BlockSpec((1, tn), lambda i, j, k: (0, j)))
    if out4:
        qo = n4 // tn
        o_spec = pl.BlockSpec((None, tm, tn), lambda i, j, k: (lax.div(j, qo), i, lax.rem(j, qo)))
        o_shapes = [jax.ShapeDtypeStruct((4, M, n4), dt) for dt in out_dtypes]
    else:
        o_spec = pl.BlockSpec((tm, tn), lambda i, j, k: (i, j))
        o_shapes = [jax.ShapeDtypeStruct((M, N), dt) for dt in out_dtypes]
    ne, no = len(extras), len(out_dtypes)
    dims = _DIMS[mode]

    def body(a_ref, b_ref, *rest):
        ex_refs, o_refs, acc_ref = rest[:ne], rest[ne:ne + no], rest[ne + no]
        k = pl.program_id(2)

        @pl.when(k == 0)
        def _():
            acc_ref[...] = jnp.zeros_like(acc_ref)

        av = a_ref[...]
        if a_pro is not None:
            av = a_pro(av)
        acc_ref[...] += lax.dot_general(av.astype(bf16), b_ref[...].astype(bf16), dims, preferred_element_type=f32)

        @pl.when(k == nk - 1)
        def _():
            acc = acc_ref[...]
            outs = (acc,) if epi is None else epi(acc, *[r[...] for r in ex_refs])
            for o_ref, o in zip(o_refs, outs):
                o_ref[...] = o.astype(o_ref.dtype)

    res = pl.pallas_call(
        body, name=name, grid=(nm, nn_, nk),
        in_specs=[a_spec, b_spec] + ex_specs, out_specs=[o_spec] * no, out_shape=o_shapes,
        scratch_shapes=[pltpu.VMEM((tm, tn), f32)],
        compiler_params=_cp(("parallel", "parallel", "arbitrary")),
    )(a, b, *[e[0] for e in extras])
    return res[0] if no == 1 else res


HALO = 16


def _rowwise(fn, ins, outs, accs, *, L, C, tl, tc, name):
    tl = _tile(L, tl, HALO)
    tc = _tile(C, tc)
    ni, nj = L // tl, C // tc
    hb = tl // HALO
    nh = L // HALO
    in_specs = []
    for spec in ins:
        kind = spec[1]
        off = spec[2] if len(spec) > 2 else 0
        if kind == "rc":
            in_specs.append(pl.BlockSpec((tl, tc), lambda j, i, off=off: (i, j + off)))
        elif kind == "c":
            in_specs.append(pl.BlockSpec((1, tc), lambda j, i, off=off: (0, j + off)))
        elif kind == "c3":
            in_specs.append(pl.BlockSpec((3, tc), lambda j, i, off=off: (0, j + off)))
        elif kind == "prev":
            in_specs.append(pl.BlockSpec((HALO, tc), lambda j, i, off=off: (jnp.maximum(i * hb - 1, 0), j + off)))
        elif kind == "next":
            in_specs.append(pl.BlockSpec((HALO, tc), lambda j, i, off=off: (jnp.minimum((i + 1) * hb, nh - 1), j + off)))
        else:
            raise ValueError(kind)
    out_specs = [pl.BlockSpec((tl, tc), lambda j, i: (i, j)) for _ in outs]
    out_specs += [pl.BlockSpec((r, tc), lambda j, i: (0, j)) for r in accs]
    out_shape = [jax.ShapeDtypeStruct((L, C), dt) for dt in outs] + [jax.ShapeDtypeStruct((r, C), f32) for r in accs]
    nin, nout, nacc = len(ins), len(outs), len(accs)

    def body(*refs):
        i = pl.program_id(1)
        tiles = [r[...] for r in refs[:nin]]
        o_vals, a_vals = fn(i, ni, *tiles)
        for r, v in zip(refs[nin:nin + nout], o_vals):
            r[...] = v.astype(r.dtype)
        if nacc:
            @pl.when(i == 0)
            def _():
                for r in refs[nin + nout:]:
                    r[...] = jnp.zeros_like(r)
            for r, v in zip(refs[nin + nout:], a_vals):
                r[...] += v

    res = pl.pallas_call(
        body, name=name, grid=(nj, ni), in_specs=in_specs, out_specs=out_specs, out_shape=out_shape,
        compiler_params=_cp(("parallel", "arbitrary")),
    )(*[s[0] for s in ins])
    return res


def _colsum(v):
    return jnp.sum(v, axis=0, keepdims=True)


def _sigmoid(x):
    return 1.0 / (1.0 + jnp.exp(-x))


_GELU_C = math.sqrt(2.0 / math.pi)


def _gelu(y):
    t = jnp.tanh(_GELU_C * (y + 0.044715 * y * y * y))
    return 0.5 * y * (1.0 + t)


def _gelu_grad(y):
    y2 = y * y
    t = jnp.tanh(_GELU_C * (y + 0.044715 * y * y2))
    return 0.5 * (1.0 + t) + 0.5 * y * (1.0 - t * t) * _GELU_C * (1.0 + 3.0 * 0.044715 * y2)


def _norm_mod_fwd(x, g, sc, sh, cfg, name):
    def fn(i, ni, xv, gv, scv, shv):
        rstd = lax.rsqrt(jnp.mean(xv * xv, axis=-1, keepdims=True) + NORM_EPS)
        return [xv * rstd * gv * (1.0 + scv) + shv], []
    return _rowwise(fn, [(x, "rc"), (g, "c"), (sc, "c"), (sh, "c")], [bf16], [], L=cfg.L, C=cfg.D, tl=256, tc=cfg.D, name=name)[0]


def _norm_mod_bwd(dh, x, g, sc, dres, cfg, name):
    def fn(i, ni, dhv, xv, gv, scv, *rest):
        dhv = dhv.astype(f32)
        rstd = lax.rsqrt(jnp.mean(xv * xv, axis=-1, keepdims=True) + NORM_EPS)
        xh = xv * rstd
        dxh = dhv * (gv * (1.0 + scv))
        dx = rstd * (dxh - xh * jnp.mean(dxh * xh, axis=-1, keepdims=True))
        if rest:
            dx = dx + rest[0]
        return [dx], [_colsum(dhv * xh), _colsum(dhv)]
    ins = [(dh, "rc"), (x, "rc"), (g, "c"), (sc, "c")] + ([(dres, "rc")] if dres is not None else [])
    return _rowwise(fn, ins, [f32], [1, 1], L=cfg.L, C=cfg.D, tl=256, tc=cfg.D, name=name)


def _final_loss(x, g, tgt, cfg):
    D = cfg.D

    def fn(i, ni, xv, gv, tv):
        rstd = lax.rsqrt(jnp.mean(xv * xv, axis=-1, keepdims=True) + NORM_EPS)
        xh = xv * rstd
        err = xh * gv - tv
        dy = err * (1.0 / D)
        dxh = dy * gv
        dx = rstd * (dxh - xh * jnp.mean(dxh * xh, axis=-1, keepdims=True))
        return [dx], [_colsum(dy * xh), _colsum(err * err)]
    return _rowwise(fn, [(x, "rc"), (g, "c"), (tgt, "rc")], [f32], [1, 1], L=cfg.L, C=D, tl=256, tc=D, name="final_loss")


def _gate_bwd(dx, out, gate, cfg, name):
    def fn(i, ni, dxv, ov, gv):
        return [dxv * gv], [_colsum(dxv * ov.astype(f32))]
    return _rowwise(fn, [(dx, "rc"), (out, "rc"), (gate, "c")], [bf16], [1], L=cfg.L, C=cfg.D, tl=512, tc=cfg.D, name=name)


def _glu_bwd(dz, g, pre, cfg):
    def fn(i, ni, dzv, gv, pv):
        dzv = dzv.astype(f32)
        gv = gv.astype(f32)
        s = _sigmoid(pv)
        dpre = dzv * gv * s * (1.0 - s)
        return [dpre, dzv * s], [_colsum(dpre)]
    return _rowwise(fn, [(dz, "rc"), (g, "rc"), (pre, "rc")], [bf16, f32], [1], L=cfg.L, C=cfg.D, tl=512, tc=cfg.D, name="glu_bwd")


def _shift_rows(av, pv, k, i):
    rows = lax.broadcasted_iota(jnp.int32, av.shape, 0)
    cur = pltpu.roll(av, k, 0)
    prev = pltpu.roll(pv, k, 0)
    prev = jnp.where(i > 0, prev, 0.0)
    prev_full = jnp.concatenate([prev, jnp.zeros((av.shape[0] - pv.shape[0], av.shape[1]), av.dtype)], axis=0) \
        if av.shape[0] > pv.shape[0] else prev
    return jnp.where(rows >= k, cur, prev_full)


def _shift_rows_up(av, nv, k, i, ni):
    n, h = av.shape[0], nv.shape[0]
    rows = lax.broadcasted_iota(jnp.int32, av.shape, 0)
    cur = pltpu.roll(av, n - k, 0)
    nxt = pltpu.roll(nv, h - k, 0)
    nxt = jnp.where(i < ni - 1, nxt, 0.0)
    nxt_full = jnp.concatenate([jnp.zeros((n - h, av.shape[1]), av.dtype), nxt], axis=0) if n > h else nxt
    return jnp.where(rows < n - k, cur, nxt_full)


def _conv3(av, pv, w, i):
    return w[0:1] * _shift_rows(av, pv, 2, i) + w[1:2] * _shift_rows(av, pv, 1, i) + w[2:3] * av


def _conv_act_fwd(a, conv_w, conv_b, cfg):
    F = cfg.F
    tc = _tile(F, 1408)
    nb = F // tc

    def fn(i, ni, au, av, pu, pv, wu, wv, bu, bv):
        cu = _conv3(au.astype(f32), pu.astype(f32), wu, i) + bu
        cv = _conv3(av.astype(f32), pv.astype(f32), wv, i) + bv
        return [cu * _sigmoid(cu) * cv], []
    ins = [(a, "rc"), (a, "rc", nb), (a, "prev"), (a, "prev", nb), (conv_w, "c3"), (conv_w, "c3", nb), (conv_b, "c"), (conv_b, "c", nb)]
    return _rowwise(fn, ins, [bf16], [], L=cfg.L, C=F, tl=512, tc=tc, name="conv_act_fwd")[0]


def _conv_act_bwd1(dact, a, conv_w, conv_b, cfg):
    F = cfg.F
    tc = _tile(F, 1408)
    nb = F // tc

    def fn(i, ni, dav, au, av, pu, pv, wu, wv, bu, bv):
        dav = dav.astype(f32)
        au, av, pu, pv = au.astype(f32), av.astype(f32), pu.astype(f32), pv.astype(f32)
        au1, au2 = _shift_rows(au, pu, 1, i), _shift_rows(au, pu, 2, i)
        av1, av2 = _shift_rows(av, pv, 1, i), _shift_rows(av, pv, 2, i)
        cu = wu[0:1] * au2 + wu[1:2] * au1 + wu[2:3] * au + bu
        cv = wv[0:1] * av2 + wv[1:2] * av1 + wv[2:3] * av + bv
        s = _sigmoid(cu)
        dcu = dav * cv * (s * (1.0 + cu * (1.0 - s)))
        dcv = dav * cu * s
        dwu = jnp.concatenate([_colsum(dcu * au2), _colsum(dcu * au1), _colsum(dcu * au)], axis=0)
        dwv = jnp.concatenate([_colsum(dcv * av2), _colsum(dcv * av1), _colsum(dcv * av)], axis=0)
        return [dcu, dcv], [dwu, dwv, _colsum(dcu), _colsum(dcv)]
    ins = [(dact, "rc"), (a, "rc"), (a, "rc", nb), (a, "prev"), (a, "prev", nb), (conv_w, "c3"), (conv_w, "c3", nb),
           (conv_b, "c"), (conv_b, "c", nb)]
    return _rowwise(fn, ins, [bf16, bf16], [3, 3, 1, 1], L=cfg.L, C=F, tl=512, tc=tc, name="conv_act_bwd1")


def _conv_bwd2(dc, w, cfg, name):
    F = cfg.F
    tc = _tile(F, 1408)

    def fn(i, ni, dcv, nxt, wv):
        dcv, nxt = dcv.astype(f32), nxt.astype(f32)
        return [wv[2:3] * dcv + wv[1:2] * _shift_rows_up(dcv, nxt, 1, i, ni) + wv[0:1] * _shift_rows_up(dcv, nxt, 2, i, ni)], []
    return _rowwise(fn, [(dc, "rc"), (dc, "next"), (w, "c3")], [bf16], [], L=cfg.L, C=F, tl=512, tc=tc, name=name)[0]


NSLAB = 8


def _s5_tables(abar_re, abar_im, lam_re, lam_im, step, cfg):
    J = cfg.G // 8
    expo = jnp.array([r + 1 for r in range(8)] + [8 * 2 ** p for p in range(8)], f32)[:, None, None]
    mag = jnp.exp(lam_re * step * expo)
    ang = lam_im * step * expo
    t_re = (mag * jnp.cos(ang)).reshape(16, J, 8 * cfg.P).transpose(1, 0, 2)
    t_im = (mag * jnp.sin(ang)).reshape(16, J, 8 * cfg.P).transpose(1, 0, 2)
    tab = jnp.concatenate([t_re, t_im], axis=-1)
    arow = jnp.concatenate([abar_re.reshape(J, 1, 8 * cfg.P), abar_im.reshape(J, 1, 8 * cfg.P)], axis=-1)
    return arow, tab


def _s5_mats(bbar_re, bbar_im, c_re, c_im, cfg):
    J, P, H = cfg.G // 8, cfg.P, cfg.H
    eye = jnp.eye(8, dtype=f32)

    def bd_in(bb):
        bb = bb.reshape(J, 8, P, H)
        return jnp.einsum("jgph,gk->jghkp", bb, eye).reshape(J, 8 * H, 8 * P)

    def bd_out(cc):
        cc = cc.reshape(J, 8, H, P)
        return jnp.einsum("jghp,gk->jgpkh", cc, eye).reshape(J, 8 * P, 8 * H)

    bmat = jnp.concatenate([bd_in(bbar_re), bd_in(bbar_im)], axis=2).astype(bf16)
    cmat = jnp.concatenate([bd_out(c_re), -bd_out(c_im)], axis=1).astype(bf16)
    return bmat, cmat


def _s5_unmats(dbmat, dcmat, cfg):
    J, P, H = cfg.G // 8, cfg.P, cfg.H
    eye = jnp.eye(8, dtype=f32)
    db = dbmat.reshape(J, 8, H, 2, 8, P)
    db = jnp.einsum("jghckp,gk->cjgph", db, eye).reshape(2, cfg.G, P, H)
    dc = dcmat.reshape(J, 2, 8, P, 8, H)
    dc = jnp.einsum("jcgpkh,gk->cjghp", dc, eye).reshape(2, cfg.G, H, P)
    return db[0], db[1], dc[0], -dc[1]


def _chunk_scan(x_ref, row0, nt, arow_ref, tab_ref, c0, reverse):
    sg = -1.0 if reverse else 1.0
    rows = lax.broadcasted_iota(jnp.int32, (nt, LANES), 0)
    order = list(range(7, -1, -1)) if reverse else list(range(8))

    def ld(k, r):
        return x_ref[k, pl.ds(row0 + r, nt, stride=8), :]

    def tab(row, k):
        return tab_ref[pl.ds(row, 1), pl.ds(k * LANES, LANES)]

    carries = [None] * NSLAB
    for k in range(4):
        ar = arow_ref[:, pl.ds(k * LANES, LANES)]
        ai = sg * arow_ref[:, pl.ds((4 + k) * LANES, LANES)]
        sr, si = ld(k, order[0]), ld(4 + k, order[0])
        for r in order[1:]:
            sr, si = ar * sr - ai * si + ld(k, r), ar * si + ai * sr + ld(4 + k, r)
        if reverse:
            cr = jnp.where(rows == nt - 1, c0[k], pltpu.roll(sr, nt - 1, 0))
            ci = jnp.where(rows == nt - 1, c0[4 + k], pltpu.roll(si, nt - 1, 0))
        else:
            cr = jnp.where(rows == 0, c0[k], pltpu.roll(sr, 1, 0))
            ci = jnp.where(rows == 0, c0[4 + k], pltpu.roll(si, 1, 0))
        d, p = 1, 0
        while d < nt:
            qr, qi = tab(8 + p, k), sg * tab(8 + p, 4 + k)
            if reverse:
                shr, shi, m = pltpu.roll(cr, nt - d, 0), pltpu.roll(ci, nt - d, 0), rows < nt - d
            else:
                shr, shi, m = pltpu.roll(cr, d, 0), pltpu.roll(ci, d, 0), rows >= d
            cr, ci = cr + jnp.where(m, qr * shr - qi * shi, 0.0), ci + jnp.where(m, qr * shi + qi * shr, 0.0)
            d, p = 2 * d, p + 1
        carries[k], carries[4 + k] = cr, ci
        sr, si = cr, ci
        for r in order:
            sr, si = ar * sr - ai * si + ld(k, r), ar * si + ai * sr + ld(4 + k, r)
            x_ref[k, pl.ds(row0 + r, nt, stride=8), :] = sr
            x_ref[4 + k, pl.ds(row0 + r, nt, stride=8), :] = si
    return carries


def _slabs_to_mat(x_ref, row0, n):
    return jnp.concatenate([x_ref[k, pl.ds(row0, n), :] for k in range(NSLAB)], axis=1)


def _mat_to_slabs(x_ref, row0, n, m):
    for k in range(NSLAB):
        x_ref[k, pl.ds(row0, n), :] = m[:, k * LANES:(k + 1) * LANES]


def _s5_fwd(u, bmat, cmat, drow, arow, tab, cfg):
    L, D, Tc = cfg.L, cfg.D, cfg.TC
    J, NC, nt = cfg.G // 8, L // Tc, Tc // 8
    W = NSLAB * LANES

    def body(u_ref, b_ref, c_ref, d_ref, a_ref, t_ref, y_ref, g_ref, cin_ref, x_ref, st_ref):
        c = pl.program_id(1)

        @pl.when(c == 0)
        def _():
            st_ref[...] = jnp.zeros_like(st_ref)

        cin_ref[...] = st_ref[...]
        ub = u_ref[...]
        _mat_to_slabs(x_ref, 0, Tc, jnp.dot(ub.astype(bf16), b_ref[...], preferred_element_type=f32))
        c0 = [st_ref[:, pl.ds(k * LANES, LANES)] for k in range(NSLAB)]
        _chunk_scan(x_ref, 0, nt, a_ref, t_ref, c0, False)
        for k in range(NSLAB):
            st_ref[:, pl.ds(k * LANES, LANES)] = x_ref[k, pl.ds(Tc - 1, 1), :]
        s = _slabs_to_mat(x_ref, 0, Tc).astype(bf16)
        y = jnp.dot(s, c_ref[...], preferred_element_type=f32) + d_ref[...] * ub
        y_ref[...] = y
        g_ref[...] = _gelu(y).astype(bf16)

    return pl.pallas_call(
        body, name="s5_fwd", grid=(J, NC),
        in_specs=[pl.BlockSpec((Tc, LANES), lambda j, c: (c, j)),
                  pl.BlockSpec((None, LANES, W), lambda j, c: (j, 0, 0)),
                  pl.BlockSpec((None, W, LANES), lambda j, c: (j, 0, 0)),
                  pl.BlockSpec((1, LANES), lambda j, c: (0, j)),
                  pl.BlockSpec((None, 1, W), lambda j, c: (j, 0, 0)),
                  pl.BlockSpec((None, 16, W), lambda j, c: (j, 0, 0))],
        out_specs=[pl.BlockSpec((Tc, LANES), lambda j, c: (c, j)),
                   pl.BlockSpec((Tc, LANES), lambda j, c: (c, j)),
                   pl.BlockSpec((None, None, 1, W), lambda j, c: (j, c, 0, 0))],
        out_shape=[jax.ShapeDtypeStruct((L, D), f32), jax.ShapeDtypeStruct((L, D), bf16),
                   jax.ShapeDtypeStruct((J, NC, 1, W), f32)],
        scratch_shapes=[pltpu.VMEM((NSLAB, Tc, LANES), f32), pltpu.VMEM((1, W), f32)],
        compiler_params=_cp(("parallel", "arbitrary")),
    )(u, bmat, cmat, drow, arow, tab)


def _s5_bwd(u, dy, cin, bmat, cmat, drow, arow, tab, cfg):
    L, D, Tc = cfg.L, cfg.D, cfg.TC
    J, NC, nt = cfg.G // 8, L // Tc, Tc // 8
    W = NSLAB * LANES
    PAD = 0

    def body(u_ref, dy_ref, cin_ref, b_ref, c_ref, d_ref, a_ref, t_ref,
             du_ref, db_ref, dc_ref, da_ref, dd_ref, s_ref, g_ref, gst_ref):
        c = pl.program_id(1)

        @pl.when(c == 0)
        def _():
            gst_ref[...] = jnp.zeros_like(gst_ref)
            db_ref[...] = jnp.zeros_like(db_ref)
            dc_ref[...] = jnp.zeros_like(dc_ref)
            da_ref[...] = jnp.zeros_like(da_ref)
            dd_ref[...] = jnp.zeros_like(dd_ref)

        ub, dyb = u_ref[...], dy_ref[...]
        ub16, dy16 = ub.astype(bf16), dyb.astype(bf16)
        _mat_to_slabs(s_ref, PAD, Tc, jnp.dot(ub16, b_ref[...], preferred_element_type=f32))
        c0 = [cin_ref[:, pl.ds(k * LANES, LANES)] for k in range(NSLAB)]
        tile_in = _chunk_scan(s_ref, PAD, nt, a_ref, t_ref, c0, False)
        _mat_to_slabs(g_ref, 0, Tc, lax.dot_general(dy16, c_ref[...], _DIMS["nt"], preferred_element_type=f32))
        g0 = [gst_ref[:, pl.ds(k * LANES, LANES)] for k in range(NSLAB)]
        _chunk_scan(g_ref, 0, nt, a_ref, t_ref, g0, True)
        for k in range(NSLAB):
            gst_ref[:, pl.ds(k * LANES, LANES)] = g_ref[k, pl.ds(0, 1), :]
        for k in range(4):
            acc_r = jnp.zeros((nt, LANES), f32)
            acc_i = jnp.zeros((nt, LANES), f32)
            for r in range(8):
                gr = g_ref[k, pl.ds(r, nt, stride=8), :]
                gi = g_ref[4 + k, pl.ds(r, nt, stride=8), :]
                if r == 0:
                    pr, pi = tile_in[k], tile_in[4 + k]
                else:
                    pr = s_ref[k, pl.ds(PAD + r - 1, nt, stride=8), :]
                    pi = s_ref[4 + k, pl.ds(PAD + r - 1, nt, stride=8), :]
                acc_r += gr * pr + gi * pi
                acc_i += gi * pr - gr * pi
            da_ref[:, pl.ds(k * LANES, LANES)] += _colsum(acc_r)
            da_ref[:, pl.ds((4 + k) * LANES, LANES)] += _colsum(acc_i)
        gm = _slabs_to_mat(g_ref, 0, Tc).astype(bf16)
        sm = _slabs_to_mat(s_ref, PAD, Tc).astype(bf16)
        du = lax.dot_general(gm, b_ref[...], _DIMS["nt"], preferred_element_type=f32) + d_ref[...] * dyb
        du_ref[...] = du.astype(bf16)
        db_ref[...] += lax.dot_general(ub16, gm, _DIMS["tn"], preferred_element_type=f32)
        dc_ref[...] += lax.dot_general(sm, dy16, _DIMS["tn"], preferred_element_type=f32)
        dd_ref[...] += _colsum(dyb * ub)

    rc = lambda j, c: (NC - 1 - c, j)
    return pl.pallas_call(
        body, name="s5_bwd", grid=(J, NC),
        in_specs=[pl.BlockSpec((Tc, LANES), rc), pl.BlockSpec((Tc, LANES), rc),
                  pl.BlockSpec((None, None, 1, W), lambda j, c: (j, NC - 1 - c, 0, 0)),
                  pl.BlockSpec((None, LANES, W), lambda j, c: (j, 0, 0)),
                  pl.BlockSpec((None, W, LANES), lambda j, c: (j, 0, 0)),
                  pl.BlockSpec((1, LANES), lambda j, c: (0, j)),
                  pl.BlockSpec((None, 1, W), lambda j, c: (j, 0, 0)),
                  pl.BlockSpec((None, 16, W), lambda j, c: (j, 0, 0))],
        out_specs=[pl.BlockSpec((Tc, LANES), rc),
                   pl.BlockSpec((None, LANES, W), lambda j, c: (j, 0, 0)),
                   pl.BlockSpec((None, W, LANES), lambda j, c: (j, 0, 0)),
                   pl.BlockSpec((None, 1, W), lambda j, c: (j, 0, 0)),
                   pl.BlockSpec((1, LANES), lambda j, c: (0, j))],
        out_shape=[jax.ShapeDtypeStruct((L, D), bf16), jax.ShapeDtypeStruct((J, LANES, W), f32),
                   jax.ShapeDtypeStruct((J, W, LANES), f32), jax.ShapeDtypeStruct((J, 1, W), f32),
                   jax.ShapeDtypeStruct((1, D), f32)],
        scratch_shapes=[pltpu.VMEM((NSLAB, Tc + PAD, LANES), f32), pltpu.VMEM((NSLAB, Tc, LANES), f32),
                        pltpu.VMEM((1, W), f32)],
        compiler_params=_cp(("parallel", "arbitrary")),
    )(u, dy, cin, bmat, cmat, drow, arow, tab)


NEG = -1e30


def _attn_logits(q_ref, k_ref, fq_ref, fk_ref, qi, ki, bq, scale):
    s = lax.dot_general(q_ref[...], k_ref[...], _DIMS["nt"], preferred_element_type=f32) * scale
    s = s + fq_ref[...] - fk_ref[...]
    rows = qi * bq + lax.broadcasted_iota(jnp.int32, s.shape, 0)
    cols = ki * bq + lax.broadcasted_iota(jnp.int32, s.shape, 1)
    return s, cols <= rows


def _attn_fwd(q, kv, fq, fk, cfg):
    L, D, NH, DH, B = cfg.L, cfg.D, cfg.NH, cfg.DH, cfg.BQ
    nq = L // B
    scale = DH ** -0.5

    def body(q_ref, k_ref, v_ref, fq_ref, fk_ref, o_ref, lse_ref, m_ref, l_ref, acc_ref):
        qi, ki = pl.program_id(1), pl.program_id(2)

        @pl.when(ki == 0)
        def _():
            m_ref[...] = jnp.full_like(m_ref, NEG)
            l_ref[...] = jnp.zeros_like(l_ref)
            acc_ref[...] = jnp.zeros_like(acc_ref)

        @pl.when(ki <= qi)
        def _():
            s, mask = _attn_logits(q_ref, k_ref, fq_ref, fk_ref, qi, ki, B, scale)
            s = jnp.where(mask, s, NEG)
            m_prev = m_ref[...]
            m_new = jnp.maximum(m_prev, jnp.max(s, axis=1, keepdims=True))
            alpha = jnp.exp(m_prev - m_new)
            p = jnp.exp(s - m_new)
            l_ref[...] = alpha * l_ref[...] + jnp.sum(p, axis=1, keepdims=True)
            acc_ref[...] = alpha * acc_ref[...] + jnp.dot(p.astype(bf16), v_ref[...], preferred_element_type=f32)
            m_ref[...] = m_new

        @pl.when(ki == qi)
        def _():
            o_ref[...] = (acc_ref[...] / l_ref[...]).astype(o_ref.dtype)
            lse_ref[...] = m_ref[...] + jnp.log(l_ref[...])

    kmap = lambda h, qi, ki: (jnp.minimum(ki, qi), h)
    vmap_ = lambda h, qi, ki: (jnp.minimum(ki, qi), NH + h)
    return pl.pallas_call(
        body, name="attn_fwd", grid=(NH, nq, nq),
        in_specs=[pl.BlockSpec((B, DH), lambda h, qi, ki: (qi, h)),
                  pl.BlockSpec((B, DH), kmap), pl.BlockSpec((B, DH), vmap_),
                  pl.BlockSpec((None, B, 1), lambda h, qi, ki: (h, qi, 0)),
                  pl.BlockSpec((None, 1, B), lambda h, qi, ki: (h, 0, jnp.minimum(ki, qi)))],
        out_specs=[pl.BlockSpec((B, DH), lambda h, qi, ki: (qi, h)),
                   pl.BlockSpec((None, B, 1), lambda h, qi, ki: (h, qi, 0))],
        out_shape=[jax.ShapeDtypeStruct((L, D), bf16), jax.ShapeDtypeStruct((NH, L, 1), f32)],
        scratch_shapes=[pltpu.VMEM((B, 1), f32), pltpu.VMEM((B, 1), f32), pltpu.VMEM((B, DH), f32)],
        compiler_params=_cp(("parallel", "parallel", "arbitrary")),
    )(q, kv, kv, fq, fk)


def _attn_bwd_dq(q, kv, do, o, lse, fq, fk, cfg):
    L, D, NH, DH, B = cfg.L, cfg.D, cfg.NH, cfg.DH, cfg.BQ
    nq = L // B
    scale = DH ** -0.5

    def body(q_ref, k_ref, v_ref, do_ref, o_ref, lse_ref, fq_ref, fk_ref, dq_ref, dfq_ref, acc_ref, df_ref, dl_ref):
        qi, ki = pl.program_id(1), pl.program_id(2)

        @pl.when(ki == 0)
        def _():
            dl_ref[...] = jnp.sum(do_ref[...].astype(f32) * o_ref[...].astype(f32), axis=1, keepdims=True)
            acc_ref[...] = jnp.zeros_like(acc_ref)
            df_ref[...] = jnp.zeros_like(df_ref)

        @pl.when(ki <= qi)
        def _():
            s, mask = _attn_logits(q_ref, k_ref, fq_ref, fk_ref, qi, ki, B, scale)
            p = jnp.where(mask, jnp.exp(s - lse_ref[...]), 0.0)
            dp = lax.dot_general(do_ref[...], v_ref[...], _DIMS["nt"], preferred_element_type=f32)
            ds = p * (dp - dl_ref[...])
            df_ref[...] += jnp.sum(ds, axis=1, keepdims=True)
            acc_ref[...] += jnp.dot(ds.astype(bf16), k_ref[...], preferred_element_type=f32)

        @pl.when(ki == qi)
        def _():
            dq_ref[...] = (acc_ref[...] * scale).astype(dq_ref.dtype)
            dfq_ref[...] = df_ref[...]

    qmap = lambda h, qi, ki: (qi, h)
    return pl.pallas_call(
        body, name="attn_bwd_dq", grid=(NH, nq, nq),
        in_specs=[pl.BlockSpec((B, DH), qmap),
                  pl.BlockSpec((B, DH), lambda h, qi, ki: (jnp.minimum(ki, qi), h)),
                  pl.BlockSpec((B, DH), lambda h, qi, ki: (jnp.minimum(ki, qi), NH + h)),
                  pl.BlockSpec((B, DH), qmap), pl.BlockSpec((B, DH), qmap),
                  pl.BlockSpec((None, B, 1), lambda h, qi, ki: (h, qi, 0)),
                  pl.BlockSpec((None, B, 1), lambda h, qi, ki: (h, qi, 0)),
                  pl.BlockSpec((None, 1, B), lambda h, qi, ki: (h, 0, jnp.minimum(ki, qi)))],
        out_specs=[pl.BlockSpec((B, DH), qmap), pl.BlockSpec((None, B, 1), lambda h, qi, ki: (h, qi, 0))],
        out_shape=[jax.ShapeDtypeStruct((L, D), bf16), jax.ShapeDtypeStruct((NH, L, 1), f32)],
        scratch_shapes=[pltpu.VMEM((B, DH), f32), pltpu.VMEM((B, 1), f32), pltpu.VMEM((B, 1), f32)],
        compiler_params=_cp(("parallel", "parallel", "arbitrary")),
    )(q, kv, kv, do, o, lse, fq, fk)


def _attn_bwd_dkv(q, kv, do, o, lse, fq, fk, cfg):
    L, D, NH, DH, B = cfg.L, cfg.D, cfg.NH, cfg.DH, cfg.BQ
    nq = L // B
    scale = DH ** -0.5

    def body(q_ref, k_ref, v_ref, do_ref, o_ref, lse_ref, fq_ref, fk_ref, dk_ref, dv_ref, dfk_ref, dka_ref, dva_ref, dfa_ref):
        ki, qi = pl.program_id(1), pl.program_id(2)

        @pl.when(qi == 0)
        def _():
            dka_ref[...] = jnp.zeros_like(dka_ref)
            dva_ref[...] = jnp.zeros_like(dva_ref)
            dfa_ref[...] = jnp.zeros_like(dfa_ref)

        @pl.when(qi >= ki)
        def _():
            do = do_ref[...]
            delta = jnp.sum(do.astype(f32) * o_ref[...].astype(f32), axis=1, keepdims=True)
            s, mask = _attn_logits(q_ref, k_ref, fq_ref, fk_ref, qi, ki, B, scale)
            p = jnp.where(mask, jnp.exp(s - lse_ref[...]), 0.0)
            dva_ref[...] += lax.dot_general(p.astype(bf16), do, _DIMS["tn"], preferred_element_type=f32)
            dp = lax.dot_general(do, v_ref[...], _DIMS["nt"], preferred_element_type=f32)
            ds = p * (dp - delta)
            dka_ref[...] += lax.dot_general(ds.astype(bf16), q_ref[...], _DIMS["tn"], preferred_element_type=f32)
            dfa_ref[...] -= jnp.sum(ds, axis=0, keepdims=True)

        @pl.when(qi == nq - 1)
        def _():
            dk_ref[...] = (dka_ref[...] * scale).astype(dk_ref.dtype)
            dv_ref[...] = dva_ref[...].astype(dv_ref.dtype)
            dfk_ref[...] = dfa_ref[...]

    qmap = lambda h, ki, qi: (jnp.maximum(qi, ki), h)
    fqmap = lambda h, ki, qi: (h, jnp.maximum(qi, ki), 0)
    return pl.pallas_call(
        body, name="attn_bwd_dkv", grid=(NH, nq, nq),
        in_specs=[pl.BlockSpec((B, DH), qmap),
                  pl.BlockSpec((B, DH), lambda h, ki, qi: (ki, h)),
                  pl.BlockSpec((B, DH), lambda h, ki, qi: (ki, NH + h)),
                  pl.BlockSpec((B, DH), qmap), pl.BlockSpec((B, DH), qmap),
                  pl.BlockSpec((None, B, 1), fqmap), pl.BlockSpec((None, B, 1), fqmap),
                  pl.BlockSpec((None, 1, B), lambda h, ki, qi: (h, 0, ki))],
        out_specs=[pl.BlockSpec((B, DH), lambda h, ki, qi: (ki, h)), pl.BlockSpec((B, DH), lambda h, ki, qi: (ki, h)),
                   pl.BlockSpec((None, 1, B), lambda h, ki, qi: (h, 0, ki))],
        out_shape=[jax.ShapeDtypeStruct((L, D), bf16), jax.ShapeDtypeStruct((L, D), bf16),
                   jax.ShapeDtypeStruct((NH, 1, L), f32)],
        scratch_shapes=[pltpu.VMEM((B, DH), f32), pltpu.VMEM((B, DH), f32), pltpu.VMEM((1, B), f32)],
        compiler_params=_cp(("parallel", "parallel", "arbitrary")),
    )(q, kv, kv, do, o, lse, fq, fk)


def _fox_logits(q, k, fqv, fkv, scale, masked):
    s = lax.dot_general(q, k, _DIMS["nt"], preferred_element_type=f32) * scale + fqv - fkv
    if masked:
        rows = lax.broadcasted_iota(jnp.int32, s.shape, 0)
        cols = lax.broadcasted_iota(jnp.int32, s.shape, 1)
        return s, cols <= rows
    return s, None


def _fox_fwd(q, kv, fq, fk, cfg):
    L, D, NH, DH, B = cfg.L, cfg.D, cfg.NH, cfg.DH, cfg.BQ
    nq = L // B
    scale = DH ** -0.5

    def body(q_ref, k_ref, v_ref, fq_ref, fk_ref, o_ref, lse_ref):
        qi = pl.program_id(1)
        qv, fqv = q_ref[...], fq_ref[...]

        def chunk(kj, carry, masked):
            m, l, acc = carry
            rows = pl.ds(pl.multiple_of(kj * B, B), B)
            s, mask = _fox_logits(qv, k_ref[rows, :], fqv, fk_ref[kj], scale, masked)
            if masked:
                s = jnp.where(mask, s, NEG)
            m_new = jnp.maximum(m, jnp.max(s, axis=1, keepdims=True))
            alpha = jnp.exp(m - m_new)
            p = jnp.exp(s - m_new)
            l = alpha * l + jnp.sum(p, axis=1, keepdims=True)
            acc = alpha * acc + jnp.dot(p.astype(bf16), v_ref[rows, :], preferred_element_type=f32)
            return m_new, l, acc

        init = (jnp.full((B, 1), NEG, f32), jnp.zeros((B, 1), f32), jnp.zeros((B, DH), f32))
        carry = lax.fori_loop(0, qi, lambda kj, c: chunk(kj, c, False), init)
        m, l, acc = chunk(qi, carry, True)
        o_ref[...] = (acc / l).astype(o_ref.dtype)
        lse_ref[...] = m + jnp.log(l)

    return pl.pallas_call(
        body, name="attn_fwd", grid=(NH, nq),
        in_specs=[pl.BlockSpec((B, DH), lambda h, qi: (qi, h)),
                  pl.BlockSpec((L, DH), lambda h, qi: (0, h)), pl.BlockSpec((L, DH), lambda h, qi: (0, NH + h)),
                  pl.BlockSpec((None, B, 1), lambda h, qi: (h, qi, 0)),
                  pl.BlockSpec((None, nq, 1, B), lambda h, qi: (h, 0, 0, 0))],
        out_specs=[pl.BlockSpec((B, DH), lambda h, qi: (qi, h)), pl.BlockSpec((None, B, 1), lambda h, qi: (h, qi, 0))],
        out_shape=[jax.ShapeDtypeStruct((L, D), bf16), jax.ShapeDtypeStruct((NH, L, 1), f32)],
        compiler_params=_cp(("parallel", "arbitrary")),
    )(q, kv, kv, fq, fk)


def _fox_bwd_dq(q, kv, do, o, lse, fq, fk, cfg):
    L, D, NH, DH, B = cfg.L, cfg.D, cfg.NH, cfg.DH, cfg.BQ
    nq = L // B
    scale = DH ** -0.5

    def body(q_ref, k_ref, v_ref, do_ref, o_ref, lse_ref, fq_ref, fk_ref, dq_ref, dfq_ref, dl_ref):
        qi = pl.program_id(1)
        qv, fqv, dov, lsev = q_ref[...], fq_ref[...], do_ref[...], lse_ref[...]
        delta = jnp.sum(dov.astype(f32) * o_ref[...].astype(f32), axis=1, keepdims=True)

        def chunk(kj, carry, masked):
            acc, df = carry
            rows = pl.ds(pl.multiple_of(kj * B, B), B)
            kv_ = k_ref[rows, :]
            s, mask = _fox_logits(qv, kv_, fqv, fk_ref[kj], scale, masked)
            p = jnp.exp(s - lsev)
            if masked:
                p = jnp.where(mask, p, 0.0)
            dp = lax.dot_general(dov, v_ref[rows, :], _DIMS["nt"], preferred_element_type=f32)
            ds = p * (dp - delta)
            return acc + jnp.dot(ds.astype(bf16), kv_, preferred_element_type=f32), df + jnp.sum(ds, axis=1, keepdims=True)

        carry = lax.fori_loop(0, qi, lambda kj, c: chunk(kj, c, False), (jnp.zeros((B, DH), f32), jnp.zeros((B, 1), f32)))
        acc, df = chunk(qi, carry, True)
        dq_ref[...] = (acc * scale).astype(dq_ref.dtype)
        dfq_ref[...] = df
        dl_ref[...] = delta

    qmap = lambda h, qi: (qi, h)
    cmap = lambda h, qi: (h, qi, 0)
    return pl.pallas_call(
        body, name="attn_bwd_dq", grid=(NH, nq),
        in_specs=[pl.BlockSpec((B, DH), qmap),
                  pl.BlockSpec((L, DH), lambda h, qi: (0, h)), pl.BlockSpec((L, DH), lambda h, qi: (0, NH + h)),
                  pl.BlockSpec((B, DH), qmap), pl.BlockSpec((B, DH), qmap),
                  pl.BlockSpec((None, B, 1), cmap), pl.BlockSpec((None, B, 1), cmap),
                  pl.BlockSpec((None, nq, 1, B), lambda h, qi: (h, 0, 0, 0))],
        out_specs=[pl.BlockSpec((B, DH), qmap), pl.BlockSpec((None, B, 1), cmap), pl.BlockSpec((None, B, 1), cmap)],
        out_shape=[jax.ShapeDtypeStruct((L, D), bf16), jax.ShapeDtypeStruct((NH, L, 1), f32),
                   jax.ShapeDtypeStruct((NH, L, 1), f32)],
        compiler_params=_cp(("parallel", "arbitrary")),
    )(q, kv, kv, do, o, lse, fq, fk)


def _fox_bwd_dkv(q, kv, do, delta, lse, fq, fk, cfg):
    L, D, NH, DH, B = cfg.L, cfg.D, cfg.NH, cfg.DH, cfg.BQ
    nq = L // B
    scale = DH ** -0.5

    def body(q_ref, k_ref, v_ref, do_ref, dl_ref, lse_ref, fq_ref, fk_ref, dk_ref, dv_ref, dfk_ref):
        ki = pl.program_id(1)
        kv_, vv, fkv = k_ref[...], v_ref[...], fk_ref[...]

        def block(qj, carry, masked):
            dk, dv, df = carry
            rows = pl.ds(pl.multiple_of(qj * B, B), B)
            qv, dov = q_ref[rows, :], do_ref[rows, :]
            s, mask = _fox_logits(qv, kv_, fq_ref[rows, :], fkv, scale, masked)
            p = jnp.exp(s - lse_ref[rows, :])
            if masked:
                p = jnp.where(mask, p, 0.0)
            dv = dv + lax.dot_general(p.astype(bf16), dov, _DIMS["tn"], preferred_element_type=f32)
            dp = lax.dot_general(dov, vv, _DIMS["nt"], preferred_element_type=f32)
            ds = p * (dp - dl_ref[rows, :])
            dk = dk + lax.dot_general(ds.astype(bf16), qv, _DIMS["tn"], preferred_element_type=f32)
            return dk, dv, df - jnp.sum(ds, axis=0, keepdims=True)

        init = (jnp.zeros((B, DH), f32), jnp.zeros((B, DH), f32), jnp.zeros((1, B), f32))
        carry = block(ki, init, True)
        dk, dv, df = lax.fori_loop(ki + 1, nq, lambda qj, c: block(qj, c, False), carry)
        dk_ref[...] = (dk * scale).astype(dk_ref.dtype)
        dv_ref[...] = dv.astype(dv_ref.dtype)
        dfk_ref[...] = df

    whole = lambda h, ki: (0, h)
    col = lambda h, ki: (h, 0, 0)
    return pl.pallas_call(
        body, name="attn_bwd_dkv", grid=(NH, nq),
        in_specs=[pl.BlockSpec((L, DH), whole),
                  pl.BlockSpec((B, DH), lambda h, ki: (ki, h)), pl.BlockSpec((B, DH), lambda h, ki: (ki, NH + h)),
                  pl.BlockSpec((L, DH), whole),
                  pl.BlockSpec((None, L, 1), col), pl.BlockSpec((None, L, 1), col), pl.BlockSpec((None, L, 1), col),
                  pl.BlockSpec((None, None, 1, B), lambda h, ki: (h, ki, 0, 0))],
        out_specs=[pl.BlockSpec((B, DH), lambda h, ki: (ki, h)), pl.BlockSpec((B, DH), lambda h, ki: (ki, h)),
                   pl.BlockSpec((None, None, 1, B), lambda h, ki: (h, ki, 0, 0))],
        out_shape=[jax.ShapeDtypeStruct((L, D), bf16), jax.ShapeDtypeStruct((L, D), bf16),
                   jax.ShapeDtypeStruct((NH, nq, 1, B), f32)],
        compiler_params=_cp(("parallel", "arbitrary")),
    )(q, kv, kv, do, delta, lse, fq, fk)


FCH = 256


def _split3(x):
    hi = x.astype(bf16)
    r1 = x - hi.astype(f32)
    mid = r1.astype(bf16)
    lo = (r1 - mid.astype(f32)).astype(bf16)
    return hi, mid, lo


def _tri_sum(tri, x):
    hi, mid, lo = _split3(x)
    return (jnp.dot(tri, hi, preferred_element_type=f32) + jnp.dot(tri, mid, preferred_element_type=f32)
            + jnp.dot(tri, lo, preferred_element_type=f32))


def _fgate_fwd(z, fb, cfg):
    L = cfg.L

    def body(z_ref, fb_ref, f_ref):
        r = lax.broadcasted_iota(jnp.int32, (FCH, FCH), 0)
        c = lax.broadcasted_iota(jnp.int32, (FCH, FCH), 1)
        tri = (c <= r).astype(bf16)
        carry = jnp.zeros((1, LANES), f32)
        for ch in range(L // FCH):
            x = z_ref[pl.ds(ch * FCH, FCH), :] + fb_ref[...]
            lf = jnp.minimum(x, 0.0) - jnp.log(1.0 + jnp.exp(-jnp.abs(x)))
            f_ref[pl.ds(ch * FCH, FCH), :] = _tri_sum(tri, lf) + carry
            carry = f_ref[pl.ds(ch * FCH + FCH - 1, 1), :]

    vm = pl.BlockSpec(memory_space=pltpu.VMEM)
    return pl.pallas_call(body, name="fgate_fwd", in_specs=[vm, vm], out_specs=vm,
                          out_shape=jax.ShapeDtypeStruct((L, LANES), f32), compiler_params=_cp())(z, fb)


def _fgate_bwd(df, z, fb, cfg):
    L = cfg.L

    def body(df_ref, z_ref, fb_ref, dz_ref, db_ref):
        r = lax.broadcasted_iota(jnp.int32, (FCH, FCH), 0)
        c = lax.broadcasted_iota(jnp.int32, (FCH, FCH), 1)
        tri = (c >= r).astype(bf16)
        carry = jnp.zeros((1, LANES), f32)
        dbs = jnp.zeros((1, LANES), f32)
        for ch in range(L // FCH - 1, -1, -1):
            suf = _tri_sum(tri, df_ref[pl.ds(ch * FCH, FCH), :]) + carry
            x = z_ref[pl.ds(ch * FCH, FCH), :] + fb_ref[...]
            dz = suf * _sigmoid(-x)
            dz_ref[pl.ds(ch * FCH, FCH), :] = dz
            dbs = dbs + _colsum(dz)
            carry = carry + _colsum(df_ref[pl.ds(ch * FCH, FCH), :])
        db_ref[...] = dbs

    vm = pl.BlockSpec(memory_space=pltpu.VMEM)
    return pl.pallas_call(body, name="fgate_bwd", in_specs=[vm, vm, vm], out_specs=[vm, vm],
                          out_shape=[jax.ShapeDtypeStruct((L, LANES), f32), jax.ShapeDtypeStruct((1, LANES), f32)],
                          compiler_params=_cp())(df, z, fb)


def _adamw(w, g, m, v, name):
    R, C = w.shape
    c1 = 1.0 - ADAM_B1 ** ADAM_STEP
    c2 = 1.0 - ADAM_B2 ** ADAM_STEP

    def fn(i, ni, wv, gv, mv, vv):
        mn = ADAM_B1 * mv + (1.0 - ADAM_B1) * gv
        vn = ADAM_B2 * vv + (1.0 - ADAM_B2) * (gv * gv)
        delta = -ADAM_LR * ((mn / c1) / (jnp.sqrt(vn / c2) + ADAM_EPS) + ADAM_WD * wv)
        return [delta, mn, vn], []
    tc = C if C % LANES else _tile(C, 1024)
    return _rowwise(fn, [(w, "rc"), (g, "rc"), (m, "rc"), (v, "rc")], [f32, f32, f32], [], L=R, C=C, tl=512, tc=tc, name=name)


def _sum_lead(x, out_dtype, name):
    n, R, C = x.shape
    tl = _tile(R, 512, HALO)
    tc = C if C % LANES else _tile(C, 1024)

    def body(x_ref, o_ref):
        acc = x_ref[0].astype(f32)
        for k in range(1, n):
            acc = acc + x_ref[k].astype(f32)
        o_ref[...] = acc.astype(o_ref.dtype)

    return pl.pallas_call(
        body, name=name, grid=(R // tl, C // tc),
        in_specs=[pl.BlockSpec((n, tl, tc), lambda i, j: (0, i, j))], out_specs=pl.BlockSpec((tl, tc), lambda i, j: (i, j)),
        out_shape=jax.ShapeDtypeStruct((R, C), out_dtype), compiler_params=_cp(("parallel", "parallel")),
    )(x)


def _add2(a, b, out_dtype, name):
    R, C = a.shape

    def fn(i, ni, av, bv):
        return [av.astype(f32) + bv.astype(f32)], []
    tc = C if C % LANES else _tile(C, 1024)
    return _rowwise(fn, [(a, "rc"), (b, "rc")], [out_dtype], [], L=R, C=C, tl=512, tc=tc, name=name)[0]


ANY = pl.BlockSpec(memory_space=pl.ANY)


def _place():
    x, y, c = lax.axis_index("x"), lax.axis_index("y"), lax.axis_index("c")
    return x, y, c


def _allgather8(blocks, name):
    n = len(blocks)

    def body(*refs):
        ins, outs = refs[:n], refs[n:2 * n]
        send_sems, recv_sems, local_sems = refs[2 * n:]
        x, y, c = _place()
        me, sibling = (x, y, c), (x, y, 1 - c)
        chips = [(1 - x, y), (x, 1 - y), (1 - x, 1 - y)]

        def slot(a, dev):
            return outs[a].at[4 * dev[0] + 2 * dev[1] + dev[2]]

        def copy(a, k, block, to, src=None):
            return pltpu.make_async_remote_copy(
                src_ref=slot(a, block) if src is None else src, dst_ref=slot(a, block),
                send_sem=send_sems.at[a * 7 + k], recv_sem=recv_sems.at[a * 7 + k], device_id=to, device_id_type=MESH)

        mine = [pltpu.make_async_copy(ins[a], slot(a, me), local_sems.at[a]) for a in range(n)]
        for cp in mine:
            cp.start()
        first = []
        for a in range(n):
            first.append(copy(a, 0, me, sibling, src=ins[a]))
            first += [copy(a, 1 + j, me, (*chip, c), src=ins[a]) for j, chip in enumerate(chips)]
        for cp in first:
            cp.start()
        passed = []
        for j, chip in enumerate(chips):
            for a in range(n):
                copy(a, 1 + j, (*chip, c), me).wait_recv()
                fwd = copy(a, 4 + j, (*chip, c), sibling)
                fwd.start()
                passed.append(fwd)
        for a in range(n):
            copy(a, 0, sibling, me).wait_recv()
        for j, chip in enumerate(chips):
            for a in range(n):
                copy(a, 4 + j, (*chip, 1 - c), me).wait_recv()
        for cp in first + passed:
            cp.wait_send()
        for cp in mine:
            cp.wait()

    outs = pl.pallas_call(
        body, name=name, in_specs=[ANY] * n, out_specs=[ANY] * n,
        out_shape=[jax.ShapeDtypeStruct((N_DEV,) + b.shape, b.dtype) for b in blocks],
        scratch_shapes=[pltpu.SemaphoreType.DMA((7 * n,)), pltpu.SemaphoreType.DMA((7 * n,)), pltpu.SemaphoreType.DMA((n,))],
    )(*blocks)
    return list(outs)


def _sibling_send(halves, name):
    n = len(halves)

    def body(*refs):
        ins, outs = refs[:n], refs[n:2 * n]
        send_sems, recv_sems = refs[2 * n:]
        x, y, c = _place()
        sends = [pltpu.make_async_remote_copy(src_ref=ins[a], dst_ref=outs[a], send_sem=send_sems.at[a],
                                              recv_sem=recv_sems.at[a], device_id=(x, y, 1 - c), device_id_type=MESH)
                 for a in range(n)]
        for cp in sends:
            cp.start()
        for cp in sends:
            cp.wait_recv()
        for cp in sends:
            cp.wait_send()

    outs = pl.pallas_call(
        body, name=name, in_specs=[ANY] * n, out_specs=[ANY] * n,
        out_shape=[jax.ShapeDtypeStruct(h.shape, h.dtype) for h in halves],
        scratch_shapes=[pltpu.SemaphoreType.DMA((n,)), pltpu.SemaphoreType.DMA((n,))],
    )(*halves)
    return list(outs)


def _sibling_swap_halves(grads, name):
    n = len(grads)

    def body(*refs):
        ins, outs = refs[:n], refs[n:2 * n]
        send_sems, recv_sems = refs[2 * n:]
        x, y, c = _place()
        sends = [pltpu.make_async_remote_copy(src_ref=ins[a].at[:, 1 - c], dst_ref=outs[a], send_sem=send_sems.at[a],
                                              recv_sem=recv_sems.at[a], device_id=(x, y, 1 - c), device_id_type=MESH)
                 for a in range(n)]
        for cp in sends:
            cp.start()
        for cp in sends:
            cp.wait_recv()
        for cp in sends:
            cp.wait_send()

    outs = pl.pallas_call(
        body, name=name, in_specs=[ANY] * n, out_specs=[ANY] * n,
        out_shape=[jax.ShapeDtypeStruct((4,) + g.shape[2:], g.dtype) for g in grads],
        scratch_shapes=[pltpu.SemaphoreType.DMA((n,)), pltpu.SemaphoreType.DMA((n,))],
    )(*grads)
    return list(outs)


def _chip_scatter(parts, name):
    n = len(parts)

    def body(*refs):
        ins, outs = refs[:n], refs[n:2 * n]
        send_sems, recv_sems = refs[2 * n:]
        x, y, c = _place()
        chips = [(1 - x, y), (x, 1 - y), (1 - x, 1 - y)]
        sends = []
        for a in range(n):
            for j, (px, py) in enumerate(chips):
                sends.append(pltpu.make_async_remote_copy(
                    src_ref=ins[a].at[2 * px + py], dst_ref=outs[a].at[j], send_sem=send_sems.at[a * 3 + j],
                    recv_sem=recv_sems.at[a * 3 + j], device_id=(px, py, c), device_id_type=MESH))
        for cp in sends:
            cp.start()
        for cp in sends:
            cp.wait_recv()
        for cp in sends:
            cp.wait_send()

    outs = pl.pallas_call(
        body, name=name, in_specs=[ANY] * n, out_specs=[ANY] * n,
        out_shape=[jax.ShapeDtypeStruct((3,) + p.shape[1:], p.dtype) for p in parts],
        scratch_shapes=[pltpu.SemaphoreType.DMA((3 * n,)), pltpu.SemaphoreType.DMA((3 * n,))],
    )(*parts)
    return list(outs)


def _sum_parts(own, got, chip, name):
    _, R, C = own.shape
    tl = _tile(R, 512, HALO)
    tc = C if C % LANES else _tile(C, 1024)

    def body(chip_ref, own_ref, got_ref, o_ref):
        acc = own_ref[...].astype(f32)
        for k in range(3):
            acc = acc + got_ref[k].astype(f32)
        o_ref[...] = acc

    return pl.pallas_call(
        body, name=name,
        grid_spec=pltpu.PrefetchScalarGridSpec(
            num_scalar_prefetch=1, grid=(R // tl, C // tc),
            in_specs=[pl.BlockSpec((None, tl, tc), lambda i, j, ch: (ch[0], i, j)),
                      pl.BlockSpec((3, tl, tc), lambda i, j, ch: (0, i, j))],
            out_specs=pl.BlockSpec((tl, tc), lambda i, j, ch: (i, j))),
        out_shape=jax.ShapeDtypeStruct((R, C), f32), compiler_params=_cp(("parallel", "parallel")),
    )(chip, own, got)


def _adamw_halves(w, m, v, g_mine, g_other, core, name):
    NL, R, C = w.shape
    r = R // 2
    tl = _tile(r, 512, HALO)
    tc = C if C % LANES else _tile(C, 1024)
    nh = r // tl
    c1 = 1.0 - ADAM_B1 ** ADAM_STEP
    c2 = 1.0 - ADAM_B2 ** ADAM_STEP

    def body(core_ref, w_ref, m_ref, v_ref, gm_ref, go_ref, g_out, d_out, m_out, v_out):
        i = pl.program_id(1)
        mine = lax.div(i, nh) == core_ref[0]
        gv = jnp.where(mine, gm_ref[...], go_ref[...])
        mn = ADAM_B1 * m_ref[...] + (1.0 - ADAM_B1) * gv
        vn = ADAM_B2 * v_ref[...] + (1.0 - ADAM_B2) * (gv * gv)
        g_out[...] = gv
        d_out[...] = -ADAM_LR * ((mn / c1) / (jnp.sqrt(vn / c2) + ADAM_EPS) + ADAM_WD * w_ref[...])
        m_out[...] = mn
        v_out[...] = vn

    full = pl.BlockSpec((None, tl, tc), lambda l, i, j, co: (l, i, j))
    mine_spec = pl.BlockSpec((None, tl, tc), lambda l, i, j, co: (l, jnp.clip(i - co[0] * nh, 0, nh - 1), j))
    other_spec = pl.BlockSpec((None, tl, tc), lambda l, i, j, co: (l, jnp.clip(i - (1 - co[0]) * nh, 0, nh - 1), j))
    return pl.pallas_call(
        body, name=name,
        grid_spec=pltpu.PrefetchScalarGridSpec(
            num_scalar_prefetch=1, grid=(NL, R // tl, C // tc),
            in_specs=[full, full, full, mine_spec, other_spec], out_specs=[full] * 4),
        out_shape=[jax.ShapeDtypeStruct((NL, R, C), f32)] * 4,
        compiler_params=_cp(("parallel", "parallel", "parallel")),
    )(core, w, m, v, g_mine, g_other)


def _s5_discretize(log_step, a_re, a_im, b_re, b_im):
    step = jnp.exp(log_step)[:, None]
    mag = jnp.exp(a_re * step)
    abar_re = mag * jnp.cos(a_im * step)
    abar_im = mag * jnp.sin(a_im * step)
    den = a_re * a_re + a_im * a_im
    nr = abar_re - 1.0
    fr = (nr * a_re + abar_im * a_im) / den
    fi = (abar_im * a_re - nr * a_im) / den
    bbar_re = fr[..., None] * b_re - fi[..., None] * b_im
    bbar_im = fr[..., None] * b_im + fi[..., None] * b_re
    return abar_re, abar_im, bbar_re, bbar_im


def _s5_prepare(p, cfg):
    abar_re, abar_im, bbar_re, bbar_im = _s5_discretize(p["log_step"], p["a_re"], p["a_im"], p["b_re"], p["b_im"])
    step = jnp.exp(p["log_step"])[:, None]
    arow, tab = _s5_tables(abar_re, abar_im, p["a_re"], p["a_im"], step, cfg)
    bmat, cmat = _s5_mats(bbar_re, bbar_im, p["c_re"], p["c_im"], cfg)
    return dict(arow=arow, tab=tab, bmat=bmat, cmat=cmat, drow=p["d"].reshape(1, cfg.D))


def _s5_param_grads(p, dbmat, dcmat, dabar, dd, cfg):
    J, P = cfg.G // 8, cfg.P
    dbb_re, dbb_im, dc_re, dc_im = _s5_unmats(dbmat, dcmat, cfg)
    da = dabar.reshape(J, 2, 8, P)
    da_re, da_im = da[:, 0].reshape(cfg.G, P), da[:, 1].reshape(cfg.G, P)
    _, vjp = jax.vjp(_s5_discretize, p["log_step"], p["a_re"], p["a_im"], p["b_re"], p["b_im"])
    dls, dare, daim, dbre, dbim = vjp((da_re, da_im, dbb_re, dbb_im))
    return dict(log_step=dls, a_re=dare, a_im=daim, b_re=dbre, b_im=dbim, c_re=dc_re, c_im=dc_im, d=dd.reshape(cfg.G, cfg.H))


def _resid_epi(acc, xv, gv):
    return xv + gv * acc, acc


def _ffn_fwd(x_in, g_norm, sc, sh, gate, w_up4, w_down, conv_w, conv_b, cfg, tag):
    h = _norm_mod_fwd(x_in, g_norm, sc, sh, cfg, f"ffn_norm_{tag}")
    a = _mm(h, w_up4, mode="nn", b4=True, tn=1408, out_dtypes=(bf16,), name=f"ffn_up_{tag}")
    act = _conv_act_fwd(a, conv_w, conv_b, cfg)
    x_out, out = _mm(act, w_down, mode="nn", extras=[(x_in, "mn"), (gate, "n")], epi=_resid_epi,
                     out_dtypes=(f32, bf16), name=f"ffn_down_{tag}")
    return x_out, dict(h=h, a=a, act=act, out=out)


def _ffn_bwd(dx, x_in, sv, g_norm, sc, gate, w_up4, w_down, conv_w, conv_b, cfg, tag):
    F = cfg.F
    dout, dgate = _gate_bwd(dx, sv["out"], gate, cfg, f"ffn_gate_bwd_{tag}")
    dact = _mm(dout, w_down, mode="nt", out_dtypes=(bf16,), name=f"ffn_dact_{tag}")
    dw_down = _mm(sv["act"], dout, mode="tn", out_dtypes=(bf16,), name=f"ffn_dwdown_{tag}")
    dcu, dcv, dwu, dwv, dbu, dbv = _conv_act_bwd1(dact, sv["a"], conv_w, conv_b, cfg)
    dau = _conv_bwd2(dcu, conv_w[:, :F], cfg, f"conv_bwd2u_{tag}")
    dav = _conv_bwd2(dcv, conv_w[:, F:], cfg, f"conv_bwd2v_{tag}")
    da = jnp.concatenate([dau, dav], axis=1)
    dh = _mm(da, w_up4, mode="nt", b4=True, tk=1408, out_dtypes=(bf16,), name=f"ffn_dh_{tag}")
    dw_up = _mm(sv["h"], da, mode="tn", out4=True, tn=1408, out_dtypes=(bf16,), name=f"ffn_dwup_{tag}")
    dx_in, A, B = _norm_mod_bwd(dh, x_in, g_norm, sc, dx, cfg, f"ffn_norm_bwd_{tag}")
    small = dict(norm_g=(1.0 + sc) * A, sc=g_norm * A, sh=B, gate=dgate,
                 conv_w=jnp.concatenate([dwu, dwv], axis=1), conv_b=jnp.concatenate([dbu, dbv], axis=1))
    return dx_in, dw_up, dw_down, small


def _local_step(cfg, x, tgt, mod, W, sp):
    D, NH = cfg.D, cfg.NH
    row = lambda v: v.reshape(1, -1)
    nmg0, nmg1 = row(sp["norm_mix_g"][0]), row(sp["norm_mix_g"][1])
    nfg0, nfg1 = row(sp["norm_ffn_g"][0]), row(sp["norm_ffn_g"][1])
    kvg, fng = row(sp["kv_norm_g"]), row(sp["final_norm_g"])
    cw0, cw1 = sp["ffn_conv_w"][0], sp["ffn_conv_w"][1]
    cb0, cb1 = row(sp["ffn_conv_b"][0]), row(sp["ffn_conv_b"][1])
    glu_b = row(sp["ssm_glu_b"])
    fb = jnp.zeros((1, LANES), f32).at[0, :NH].set(sp["forget_b"])
    s5p = {k: sp["ssm_" + k][0] for k in ("log_step", "a_re", "a_im", "b_re", "b_im", "c_re", "c_im", "d")}
    s5 = _s5_prepare(s5p, cfg)
    m0, m1 = mod["l0"], mod["l1"]

    h0 = _norm_mod_fwd(x, nmg0, m0["sc_m"], m0["sh_m"], cfg, "mix_norm_0")
    u = _mm(h0, W["ssm_w_in"], mode="nn", name="ssm_in")
    y, gact, cin = _s5_fwd(u, s5["bmat"], s5["cmat"], s5["drow"], s5["arow"], s5["tab"], cfg)

    def glu_epi(acc, bv, gv):
        pre = acc + bv
        return pre, gv.astype(f32) * _sigmoid(pre)
    pre, z = _mm(gact, W["ssm_glu_w"], mode="nn", extras=[(glu_b, "n"), (gact, "mn")], epi=glu_epi,
                 out_dtypes=(f32, bf16), name="ssm_glu")
    x1, out_m0 = _mm(z, W["ssm_w_out"], mode="nn", extras=[(x, "mn"), (m0["g_m"], "n")], epi=_resid_epi,
                     out_dtypes=(f32, bf16), name="ssm_out")
    x2, ffn0 = _ffn_fwd(x1, nfg0, m0["sc_f"], m0["sh_f"], m0["g_f"], W["ffn_w_up0"], W["ffn_w_down0"], cw0, cb0, cfg, "0")

    hk = _norm_mod_fwd(x2, kvg, mod["sc_kv"], mod["sh_kv"], cfg, "kv_norm")
    kvb = _mm(hk, W["kv_w"], mode="nn", out_dtypes=(bf16,), name="kv_proj")
    zf = _mm(hk, W["kv_wf"], mode="nn", name="kv_fproj")
    fc = _fgate_fwd(zf, fb, cfg)
    fct = fc[:, :NH].T
    fq, fk = fct[:, :, None], fct.reshape(NH, cfg.L // cfg.BQ, 1, cfg.BQ)

    h1 = _norm_mod_fwd(x2, nmg1, m1["sc_m"], m1["sh_m"], cfg, "mix_norm_1")
    q = _mm(h1, W["attn_w_q"], mode="nn", out_dtypes=(bf16,), name="attn_q")
    o, lse = _fox_fwd(q, kvb, fq, fk, cfg)
    x3, out_m1 = _mm(o, W["attn_w_out"], mode="nn", extras=[(x2, "mn"), (m1["g_m"], "n")], epi=_resid_epi,
                     out_dtypes=(f32, bf16), name="attn_out")
    x4, ffn1 = _ffn_fwd(x3, nfg1, m1["sc_f"], m1["sh_f"], m1["g_f"], W["ffn_w_up1"], W["ffn_w_down1"], cw1, cb1, cfg, "1")

    dx, dfng, lcol = _final_loss(x4, fng, tgt, cfg)
    loss = (0.5 / D) * jnp.sum(lcol)

    dx, dw_up1, dw_down1, sf1 = _ffn_bwd(dx, x3, ffn1, nfg1, m1["sc_f"], m1["g_f"], W["ffn_w_up1"], W["ffn_w_down1"], cw1, cb1, cfg, "1")
    dout, dgm1 = _gate_bwd(dx, out_m1, m1["g_m"], cfg, "attn_gate_bwd")
    do = _mm(dout, W["attn_w_out"], mode="nt", out_dtypes=(bf16,), name="attn_do")
    dw_ao = _mm(o, dout, mode="tn", out_dtypes=(bf16,), name="attn_dwout")
    dq, dfq, delta = _fox_bwd_dq(q, kvb, do, o, lse, fq, fk, cfg)
    dk, dv, dfk = _fox_bwd_dkv(q, kvb, do, delta, lse, fq, fk, cfg)
    dh1 = _mm(dq, W["attn_w_q"], mode="nt", out_dtypes=(bf16,), name="attn_dh")
    dw_q = _mm(h1, dq, mode="tn", out_dtypes=(bf16,), name="attn_dwq")
    dx, A1, B1 = _norm_mod_bwd(dh1, x2, nmg1, m1["sc_m"], dx, cfg, "mix_norm_bwd_1")

    dfc = jnp.pad((dfq[:, :, 0] + dfk.reshape(NH, cfg.L)).T, ((0, 0), (0, LANES - NH)))
    dzf, dfb = _fgate_bwd(dfc, zf, fb, cfg)
    dkv = jnp.concatenate([dk, dv], axis=1)
    dhk1 = _mm(dkv, W["kv_w"], mode="nt", name="kv_dh1")
    dhk = _mm(dzf, W["kv_wf"], mode="nt", extras=[(dhk1, "mn")], epi=lambda acc, e: (acc + e,), out_dtypes=(bf16,), name="kv_dh2")
    dw_kv = _mm(hk, dkv, mode="tn", out_dtypes=(bf16,), name="kv_dw")
    dw_kf = _mm(hk, dzf, mode="tn", out_dtypes=(bf16,), name="kv_dwf")
    dx, Ak, Bk = _norm_mod_bwd(dhk, x2, kvg, mod["sc_kv"], dx, cfg, "kv_norm_bwd")

    dx, dw_up0, dw_down0, sf0 = _ffn_bwd(dx, x1, ffn0, nfg0, m0["sc_f"], m0["g_f"], W["ffn_w_up0"], W["ffn_w_down0"], cw0, cb0, cfg, "0")
    dout, dgm0 = _gate_bwd(dx, out_m0, m0["g_m"], cfg, "ssm_gate_bwd")
    dz = _mm(dout, W["ssm_w_out"], mode="nt", out_dtypes=(bf16,), name="ssm_dz")
    dw_so = _mm(z, dout, mode="tn", out_dtypes=(bf16,), name="ssm_dwout")
    dpre, dgd, dglub = _glu_bwd(dz, gact, pre, cfg)
    dy = _mm(dpre, W["ssm_glu_w"], mode="nt", extras=[(dgd, "mn"), (y, "mn")],
             epi=lambda acc, e, yv: ((acc + e) * _gelu_grad(yv),), name="ssm_dy")
    dw_glu = _mm(gact, dpre, mode="tn", out_dtypes=(bf16,), name="ssm_dwglu")
    du, dbm, dcm, dab, dd = _s5_bwd(u, dy, cin, s5["bmat"], s5["cmat"], s5["drow"], s5["arow"], s5["tab"], cfg)
    dh0 = _mm(du, W["ssm_w_in"], mode="nt", out_dtypes=(bf16,), name="ssm_dh")
    dw_in = _mm(h0, du, mode="tn", out_dtypes=(bf16,), name="ssm_dwin")
    dx, A0, B0 = _norm_mod_bwd(dh0, x, nmg0, m0["sc_m"], dx, cfg, "mix_norm_bwd_0")

    s5g = _s5_param_grads(s5p, dbm, dcm, dab, dd, cfg)
    big = dict(ssm_w_in=dw_in, ssm_glu_w=dw_glu, ssm_w_out=dw_so, attn_w_q=dw_q, attn_w_out=dw_ao,
               ffn_w_up0=dw_up0, ffn_w_up1=dw_up1, ffn_w_down0=dw_down0, ffn_w_down1=dw_down1,
               kv_w=jnp.concatenate([dw_kv, dw_kf[:, :NH]], axis=1))
    small = dict(
        norm_mix_g=jnp.concatenate([(1.0 + m0["sc_m"]) * A0, (1.0 + m1["sc_m"]) * A1], axis=0),
        norm_ffn_g=jnp.concatenate([sf0["norm_g"], sf1["norm_g"]], axis=0),
        ssm_glu_b=dglub, kv_norm_g=(1.0 + mod["sc_kv"]) * Ak, forget_b=dfb[0, :NH],
        ffn_conv_w=jnp.stack([sf0["conv_w"], sf1["conv_w"]]), ffn_conv_b=jnp.concatenate([sf0["conv_b"], sf1["conv_b"]], axis=0),
        final_norm_g=dfng, **{"ssm_" + k: v[None] for k, v in s5g.items()})
    dmod = [jnp.concatenate([B0, nmg0 * A0, dgm0, sf0["sh"], sf0["sc"], sf0["gate"]], axis=1),
            jnp.concatenate([B1, nmg1 * A1, dgm1, sf1["sh"], sf1["sc"], sf1["gate"]], axis=1),
            jnp.concatenate([Bk, kvg * Ak], axis=1)]
    return loss, dx, big, small, dmod


WEIGHTS = ["mod_w", "mod_b", "norm_mix_g", "norm_ffn_g", "ssm_w_in", "ssm_log_step", "ssm_a_re", "ssm_a_im", "ssm_b_re",
           "ssm_b_im", "ssm_c_re", "ssm_c_im", "ssm_d", "ssm_glu_w", "ssm_glu_b", "ssm_w_out", "kv_mod_w", "kv_mod_b",
           "kv_norm_g", "kv_w", "forget_b", "attn_w_q", "attn_w_out", "ffn_w_up", "ffn_conv_w", "ffn_conv_b", "ffn_w_down",
           "final_norm_g"]
ARGS = ["x", "c"] + WEIGHTS + ["loss_target"] + ["m_" + n for n in WEIGHTS] + ["v_" + n for n in WEIGHTS]
SMALL = ["mod_b", "norm_mix_g", "norm_ffn_g", "ssm_log_step", "ssm_a_re", "ssm_a_im", "ssm_b_re", "ssm_b_im", "ssm_c_re",
         "ssm_c_im", "ssm_d", "ssm_glu_b", "kv_mod_b", "kv_norm_g", "forget_b", "ffn_conv_w", "ffn_conv_b", "final_norm_g"]
PACK_ROWS = 512


def _pack(arrs):
    flat = jnp.concatenate([a.reshape(-1).astype(f32) for a in arrs])
    unit = PACK_ROWS * LANES
    n = -(-flat.shape[0] // unit) * unit
    return jnp.pad(flat, (0, n - flat.shape[0])).reshape(-1, LANES)


def _unpack(packed, shapes):
    flat, out, off = packed.reshape(-1), [], 0
    for s in shapes:
        n = math.prod(s)
        out.append(flat[off:off + n].reshape(s))
        off += n
    return out


def _silu(v):
    return v * _sigmoid(v)


def _half(w, c, axis):
    r = w.shape[axis] // 2
    return lax.dynamic_slice_in_dim(w, c * r, r, axis=axis)


def kernel(x, c, mod_w, mod_b, norm_mix_g, norm_ffn_g, ssm_w_in, ssm_log_step, ssm_a_re, ssm_a_im, ssm_b_re, ssm_b_im, ssm_c_re, ssm_c_im, ssm_d, ssm_glu_w, ssm_glu_b, ssm_w_out, kv_mod_w, kv_mod_b, kv_norm_g, kv_w, forget_b, attn_w_q, attn_w_out, ffn_w_up, ffn_conv_w, ffn_conv_b, ffn_w_down, final_norm_g, loss_target, m_mod_w, m_mod_b, m_norm_mix_g, m_norm_ffn_g, m_ssm_w_in, m_ssm_log_step, m_ssm_a_re, m_ssm_a_im, m_ssm_b_re, m_ssm_b_im, m_ssm_c_re, m_ssm_c_im, m_ssm_d, m_ssm_glu_w, m_ssm_glu_b, m_ssm_w_out, m_kv_mod_w, m_kv_mod_b, m_kv_norm_g, m_kv_w, m_forget_b, m_attn_w_q, m_attn_w_out, m_ffn_w_up, m_ffn_conv_w, m_ffn_conv_b, m_ffn_w_down, m_final_norm_g, v_mod_w, v_mod_b, v_norm_mix_g, v_norm_ffn_g, v_ssm_w_in, v_ssm_log_step, v_ssm_a_re, v_ssm_a_im, v_ssm_b_re, v_ssm_b_im, v_ssm_c_re, v_ssm_c_im, v_ssm_d, v_ssm_glu_w, v_ssm_glu_b, v_ssm_w_out, v_kv_mod_w, v_kv_mod_b, v_kv_norm_g, v_kv_w, v_forget_b, v_attn_w_q, v_attn_w_out, v_ffn_w_up, v_ffn_conv_w, v_ffn_conv_b, v_ffn_w_down, v_final_norm_g):
    a = dict(locals())
    assert list(a) == ARGS
    return _step(CFG, a)


def _step(cfg, a):
    D, F, NH = cfg.D, cfg.F, cfg.NH
    x_, y_, c_ = _place()
    chip, dev = 2 * x_ + y_, 4 * x_ + 2 * y_ + c_

    big_src = dict(ssm_w_in=a["ssm_w_in"][0], ssm_glu_w=a["ssm_glu_w"][0], ssm_w_out=a["ssm_w_out"][0],
                   attn_w_q=a["attn_w_q"][0], attn_w_out=a["attn_w_out"][0],
                   ffn_w_up0=a["ffn_w_up"][0], ffn_w_up1=a["ffn_w_up"][1],
                   ffn_w_down0=a["ffn_w_down"][0], ffn_w_down1=a["ffn_w_down"][1], kv_w=a["kv_w"])
    big_names = list(big_src)
    blocks = [_half(big_src[n], c_, 0).astype(bf16) for n in big_names]
    blocks += [_half(a["ssm_glu_b"], c_, 1), _half(a["ffn_conv_w"], c_, 2), a["c"]]
    got = _allgather8(blocks, "gather_weights")
    W = {}
    for n, g in zip(big_names, got):
        if n.startswith("ffn_w_up"):
            W[n] = g.reshape(4, D, 2 * F // 4)
        elif n == "kv_w":
            full = g.reshape(4, D, -1).transpose(1, 0, 2).reshape(D, -1)
            W["kv_w"] = full[:, :2 * D]
            W["kv_wf"] = jnp.pad(full[:, 2 * D:], ((0, 0), (0, LANES - NH)))
        else:
            W[n] = g.reshape(-1, D)
    glu_b_full = got[-3].reshape(D)
    conv_w_full = got[-2].transpose(1, 2, 0, 3).reshape(2, 3, 2 * F)
    c16 = jnp.pad(got[-1].reshape(N_DEV, D), ((0, 16 - N_DEV), (0, 0)))

    mcols = [_mm(c16, a["mod_w"][l], mode="nn", a_pro=_silu, name=f"mod_fwd_{l}") for l in range(2)]
    mcols.append(_mm(c16, a["kv_mod_w"], mode="nn", a_pro=_silu, name="mod_fwd_kv"))
    widths = [m.shape[1] for m in mcols]
    mall = _allgather8([jnp.concatenate(mcols, axis=1)[:N_DEV]], "gather_mod")[0][0::2]
    offs = [0, widths[0], widths[0] + widths[1]]
    rows = []
    for off, wd, bias in zip(offs, widths, [a["mod_b"][0], a["mod_b"][1], a["kv_mod_b"]]):
        fullm = mall[:, :, off:off + wd].transpose(1, 0, 2).reshape(N_DEV, 4 * wd) + bias
        rows.append(lax.dynamic_slice_in_dim(fullm, dev, 1, axis=0))
    mod = {}
    for l in range(2):
        mod[f"l{l}"] = dict(zip(["sh_m", "sc_m", "g_m", "sh_f", "sc_f", "g_f"], jnp.split(rows[l], 6, axis=1)))
    mod["sh_kv"], mod["sc_kv"] = jnp.split(rows[2], 2, axis=1)

    sp = {n: a[n] for n in ["norm_mix_g", "norm_ffn_g", "kv_norm_g", "final_norm_g", "ffn_conv_b", "forget_b", "ssm_log_step",
                            "ssm_a_re", "ssm_a_im", "ssm_b_re", "ssm_b_im", "ssm_c_re", "ssm_c_im", "ssm_d"]}
    sp["ssm_glu_b"], sp["ffn_conv_w"] = glu_b_full, conv_w_full
    loss, dx, big, small, dmod = _local_step(cfg, a["x"][0], a["loss_target"][0], mod, W, sp)
    loss = lax.psum(loss, ("x", "y", "c"))

    small["mod_b"] = jnp.concatenate([dmod[0], dmod[1]], axis=0)
    small["kv_mod_b"] = dmod[2]
    shapes = [(2, 6 * D) if n == "mod_b" else (1, D) if n == "ssm_glu_b" else (2, 3, 2 * F) if n == "ffn_conv_w"
              else a[n].shape for n in SMALL]
    packs = _allgather8([_pack([small[n] for n in SMALL])], "gather_small")[0]
    gsmall = dict(zip(SMALL, _unpack(_sum_lead(packs, f32, "sum_small"), shapes)))
    per_dev = packs.reshape(N_DEV, -1)
    sizes = [math.prod(s) for s in shapes]
    starts = dict(zip(SMALL, [sum(sizes[:i]) for i in range(len(sizes))]))

    def rows_of(name, l, width):
        st = starts[name] + l * 6 * D
        blk = lax.dynamic_slice(per_dev, (0, st + chip * width), (N_DEV, width))
        return jnp.pad(blk, ((0, 16 - N_DEV), (0, 0)))
    g_mod_w = jnp.stack([_mm(c16, rows_of("mod_b", l, 6 * D // 4), mode="tn", a_pro=_silu, name=f"mod_dw_{l}") for l in range(2)])
    g_kv_mod_w = _mm(c16, rows_of("kv_mod_b", 0, 2 * D // 4), mode="tn", a_pro=_silu, name="mod_dw_kv")
    gsmall["ssm_glu_b"] = lax.dynamic_slice_in_dim(gsmall["ssm_glu_b"], chip * (D // 4), D // 4, axis=1)
    gsmall["ffn_conv_w"] = lax.dynamic_slice_in_dim(gsmall["ffn_conv_w"], chip * (2 * F // 4), 2 * F // 4, axis=2)

    def blocks_of(n, g):
        if n.startswith("ffn_w_up"):
            return g.reshape(4, 2, D // 2, -1)
        if n == "kv_w":
            return g.reshape(D, 4, -1).transpose(1, 0, 2).reshape(4, 2, D // 2, -1)
        return g.reshape(4, 2, g.shape[0] // 8, g.shape[1])
    gb = [blocks_of(n, big[n]) for n in big_names]
    recv = _sibling_swap_halves(gb, "grad_sibling_swap")
    parts = []
    for n, g, r in zip(big_names, gb, recv):
        keep = lax.dynamic_index_in_dim(g, c_, axis=1, keepdims=False)
        rr, cc = keep.shape[1], keep.shape[2]
        parts.append(_add2(keep.reshape(4 * rr, cc), r.reshape(4 * rr, cc), bf16, f"grad_add_{n}").reshape(4, rr, cc))
    scattered = _chip_scatter(parts, "grad_chip_scatter")
    chip1, core1 = jnp.reshape(chip, (1,)).astype(jnp.int32), jnp.reshape(c_, (1,)).astype(jnp.int32)
    mine = {n: _sum_parts(p, q, chip1, f"grad_sum_{n}") for n, p, q in zip(big_names, parts, scattered)}
    other = dict(zip(big_names, _sibling_send([mine[n] for n in big_names], "grad_sibling_send")))

    grads = dict(gsmall)
    grads["mod_w"], grads["kv_mod_w"] = g_mod_w, g_kv_mod_w
    delta, new_m, new_v = {}, {}, {}
    members = dict(ssm_w_in=["ssm_w_in"], ssm_glu_w=["ssm_glu_w"], ssm_w_out=["ssm_w_out"], attn_w_q=["attn_w_q"],
                   attn_w_out=["attn_w_out"], kv_w=["kv_w"], ffn_w_up=["ffn_w_up0", "ffn_w_up1"],
                   ffn_w_down=["ffn_w_down0", "ffn_w_down1"])
    for n, parts_ in members.items():
        shp = a[n].shape
        three = lambda t: t.reshape(len(parts_), -1, shp[-1])
        g_, d_, m_, v_ = _adamw_halves(three(a[n]), three(a["m_" + n]), three(a["v_" + n]),
                                       jnp.stack([mine[p] for p in parts_]), jnp.stack([other[p] for p in parts_]),
                                       core1, f"adamw_{n}")
        grads[n], delta[n], new_m[n], new_v[n] = g_.reshape(shp), d_.reshape(shp), m_.reshape(shp), v_.reshape(shp)
    for n in ["mod_w", "kv_mod_w"]:
        shp = a[n].shape
        two = lambda t: t.reshape(-1, shp[-1])
        d_, m_, v_ = _adamw(two(a[n]), two(grads[n]), two(a["m_" + n]), two(a["v_" + n]), f"adamw_{n}")
        delta[n], new_m[n], new_v[n] = d_.reshape(shp), m_.reshape(shp), v_.reshape(shp)
    grads = {n: grads[n].reshape(a[n].shape) for n in WEIGHTS}
    sshapes = [a[n].shape for n in SMALL]
    d_, m_, v_ = _adamw(_pack([a[n] for n in SMALL]), _pack([grads[n] for n in SMALL]), _pack([a["m_" + n] for n in SMALL]),
                        _pack([a["v_" + n] for n in SMALL]), "adamw_small")
    for n, dd_, mm_, vv_ in zip(SMALL, _unpack(d_, sshapes), _unpack(m_, sshapes), _unpack(v_, sshapes)):
        delta[n], new_m[n], new_v[n] = dd_, mm_, vv_

    return (loss, dx[None], *[grads[n] for n in WEIGHTS], *[delta[n] for n in WEIGHTS],
            *[new_m[n] for n in WEIGHTS], *[new_v[n] for n in WEIGHTS])
```

```python
import collections
import functools
import math

import jax
import jax.numpy as jnp
from jax import lax
from jax.experimental import pallas as pl
from jax.experimental.pallas import tpu as pltpu

f32 = jnp.float32
bf16 = jnp.bfloat16
MESH = pl.DeviceIdType.MESH

LANES = 128
SUBLANES = 8
VMEM_BYTES_V7X = 64 * 1024 * 1024
VMEM_LIMIT = 56 * 1024 * 1024

Cfg = collections.namedtuple("Cfg", "L D G P H NH DH F TC BQ")
CFG = Cfg(L=4096, D=2048, G=128, P=64, H=16, NH=16, DH=128, F=5632, TC=512, BQ=512)
NORM_EPS = 1e-6
ADAM_LR, ADAM_B1, ADAM_B2, ADAM_EPS, ADAM_WD, ADAM_STEP = 0.001, 0.9, 0.999, 1e-08, 0.01, 10
N_DEV = 8


def _cp(sem=None):
    return pltpu.CompilerParams(dimension_semantics=sem, vmem_limit_bytes=VMEM_LIMIT)


def _tile(dim, pref, unit=LANES):
    if dim <= pref:
        return dim
    t = (pref // unit) * unit
    while t > unit and dim % t:
        t -= unit
    assert dim % t == 0, (dim, pref)
    return t


_DIMS = {"nn": (((1,), (0,)), ((), ())), "nt": (((1,), (1,)), ((), ())), "tn": (((0,), (0,)), ((), ()))}


def _mm(a, b, *, mode, name, tm=1024, tn=1024, tk=512, b4=False, out4=False, a_pro=None, extras=(), epi=None,
        out_dtypes=(f32,)):
    if mode == "tn":
        K, M = a.shape
    else:
        M, K = a.shape
    if b4:
        R, c4 = b.shape[1], b.shape[2]
        N = R if mode == "nt" else 4 * c4
        assert (K == 4 * c4) if mode == "nt" else (K == R)
    else:
        N = b.shape[0] if mode == "nt" else b.shape[1]
        assert K == (b.shape[1] if mode == "nt" else b.shape[0])
    n4 = N // 4
    tm = _tile(M, tm, LANES if mode == "tn" else SUBLANES * 2)
    tn = _tile(n4 if out4 or (b4 and mode != "nt") else N, tn)
    tk = _tile(b.shape[2] if (b4 and mode == "nt") else K, tk)
    nm, nn_, nk = M // tm, N // tn, K // tk

    a_spec = pl.BlockSpec((tk, tm), lambda i, j, k: (k, i)) if mode == "tn" else pl.BlockSpec((tm, tk), lambda i, j, k: (i, k))
    if b4 and mode == "nt":
        q = b.shape[2] // tk
        b_spec = pl.BlockSpec((None, tn, tk), lambda i, j, k: (lax.div(k, q), j, lax.rem(k, q)))
    elif b4:
        q = b.shape[2] // tn
        b_spec = pl.BlockSpec((None, tk, tn), lambda i, j, k: (lax.div(j, q), k, lax.rem(j, q)))
    elif mode == "nt":
        b_spec = pl.BlockSpec((tn, tk), lambda i, j, k: (j, k))
    else:
        b_spec = pl.BlockSpec((tk, tn), lambda i, j, k: (k, j))
    ex_specs = []
    for arr, kind in extras:
        if kind == "mn":
            ex_specs.append(pl.BlockSpec((tm, tn), lambda i, j, k: (i, j)))
        else:
            ex_specs.append(pl.BlockSpec((1, tn), lambda i, j, k: (0, j)))
    if out4:
        qo = n4 // tn
        o_spec = pl.BlockSpec((None, tm, tn), lambda i, j, k: (lax.div(j, qo), i, lax.rem(j, qo)))
        o_shapes = [jax.ShapeDtypeStruct((4, M, n4), dt) for dt in out_dtypes]
    else:
        o_spec = pl.BlockSpec((tm, tn), lambda i, j, k: (i, j))
        o_shapes = [jax.ShapeDtypeStruct((M, N), dt) for dt in out_dtypes]
    ne, no = len(extras), len(out_dtypes)
    dims = _DIMS[mode]

    def body(a_ref, b_ref, *rest):
        ex_refs, o_refs, acc_ref = rest[:ne], rest[ne:ne + no], rest[ne + no]
        k = pl.program_id(2)

        @pl.when(k == 0)
        def _():
            acc_ref[...] = jnp.zeros_like(acc_ref)

        av = a_ref[...]
        if a_pro is not None:
            av = a_pro(av)
        acc_ref[...] += lax.dot_general(av.astype(bf16), b_ref[...].astype(bf16), dims, preferred_element_type=f32)

        @pl.when(k == nk - 1)
        def _():
            acc = acc_ref[...]
            outs = (acc,) if epi is None else epi(acc, *[r[...] for r in ex_refs])
            for o_ref, o in zip(o_refs, outs):
                o_ref[...] = o.astype(o_ref.dtype)

    res = pl.pallas_call(
        body, name=name, grid=(nm, nn_, nk),
        in_specs=[a_spec, b_spec] + ex_specs, out_specs=[o_spec] * no, out_shape=o_shapes,
        scratch_shapes=[pltpu.VMEM((tm, tn), f32)],
        compiler_params=_cp(("parallel", "parallel", "arbitrary")),
    )(a, b, *[e[0] for e in extras])
    return res[0] if no == 1 else res


HALO = 16


def _rowwise(fn, ins, outs, accs, *, L, C, tl, tc, name):
    tl = _tile(L, tl, HALO)
    tc = _tile(C, tc)
    ni, nj = L // tl, C // tc
    hb = tl // HALO
    nh = L // HALO
    in_specs = []
    for spec in ins:
        kind = spec[1]
        off = spec[2] if len(spec) > 2 else 0
        if kind == "rc":
            in_specs.append(pl.BlockSpec((tl, tc), lambda j, i, off=off: (i, j + off)))
        elif kind == "c":
            in_specs.append(pl.BlockSpec((1, tc), lambda j, i, off=off: (0, j + off)))
        elif kind == "c3":
            in_specs.append(pl.BlockSpec((3, tc), lambda j, i, off=off: (0, j + off)))
        elif kind == "prev":
            in_specs.append(pl.BlockSpec((HALO, tc), lambda j, i, off=off: (jnp.maximum(i * hb - 1, 0), j + off)))
        elif kind == "next":
            in_specs.append(pl.BlockSpec((HALO, tc), lambda j, i, off=off: (jnp.minimum((i + 1) * hb, nh - 1), j + off)))
        else:
            raise ValueError(kind)
    out_specs = [pl.BlockSpec((tl, tc), lambda j, i: (i, j)) for _ in outs]
    out_specs += [pl.BlockSpec((r, tc), lambda j, i: (0, j)) for r in accs]
    out_shape = [jax.ShapeDtypeStruct((L, C), dt) for dt in outs] + [jax.ShapeDtypeStruct((r, C), f32) for r in accs]
    nin, nout, nacc = len(ins), len(outs), len(accs)

    def body(*refs):
        i = pl.program_id(1)
        tiles = [r[...] for r in refs[:nin]]
        o_vals, a_vals = fn(i, ni, *tiles)
        for r, v in zip(refs[nin:nin + nout], o_vals):
            r[...] = v.astype(r.dtype)
        if nacc:
            @pl.when(i == 0)
            def _():
                for r in refs[nin + nout:]:
                    r[...] = jnp.zeros_like(r)
            for r, v in zip(refs[nin + nout:], a_vals):
                r[...] += v

    res = pl.pallas_call(
        body, name=name, grid=(nj, ni), in_specs=in_specs, out_specs=out_specs, out_shape=out_shape,
        compiler_params=_cp(("parallel", "arbitrary")),
    )(*[s[0] for s in ins])
    return res


def _colsum(v):
    return jnp.sum(v, axis=0, keepdims=True)


def _sigmoid(x):
    return 1.0 / (1.0 + jnp.exp(-x))


_GELU_C = math.sqrt(2.0 / math.pi)


def _gelu(y):
    t = jnp.tanh(_GELU_C * (y + 0.044715 * y * y * y))
    return 0.5 * y * (1.0 + t)


def _gelu_grad(y):
    y2 = y * y
    t = jnp.tanh(_GELU_C * (y + 0.044715 * y * y2))
    return 0.5 * (1.0 + t) + 0.5 * y * (1.0 - t * t) * _GELU_C * (1.0 + 3.0 * 0.044715 * y2)


def _norm_mod_fwd(x, g, sc, sh, cfg, name):
    def fn(i, ni, xv, gv, scv, shv):
        rstd = lax.rsqrt(jnp.mean(xv * xv, axis=-1, keepdims=True) + NORM_EPS)
        return [xv * rstd * gv * (1.0 + scv) + shv], []
    return _rowwise(fn, [(x, "rc"), (g, "c"), (sc, "c"), (sh, "c")], [bf16], [], L=cfg.L, C=cfg.D, tl=256, tc=cfg.D, name=name)[0]


def _norm_mod_bwd(dh, x, g, sc, dres, cfg, name):
    def fn(i, ni, dhv, xv, gv, scv, *rest):
        dhv = dhv.astype(f32)
        rstd = lax.rsqrt(jnp.mean(xv * xv, axis=-1, keepdims=True) + NORM_EPS)
        xh = xv * rstd
        dxh = dhv * (gv * (1.0 + scv))
        dx = rstd * (dxh - xh * jnp.mean(dxh * xh, axis=-1, keepdims=True))
        if rest:
            dx = dx + rest[0]
        return [dx], [_colsum(dhv * xh), _colsum(dhv)]
    ins = [(dh, "rc"), (x, "rc"), (g, "c"), (sc, "c")] + ([(dres, "rc")] if dres is not None else [])
    return _rowwise(fn, ins, [f32], [1, 1], L=cfg.L, C=cfg.D, tl=256, tc=cfg.D, name=name)


def _final_loss(x, g, tgt, cfg):
    D = cfg.D

    def fn(i, ni, xv, gv, tv):
        rstd = lax.rsqrt(jnp.mean(xv * xv, axis=-1, keepdims=True) + NORM_EPS)
        xh = xv * rstd
        err = xh * gv - tv
        dy = err * (1.0 / D)
        dxh = dy * gv
        dx = rstd * (dxh - xh * jnp.mean(dxh * xh, axis=-1, keepdims=True))
        return [dx], [_colsum(dy * xh), _colsum(err * err)]
    return _rowwise(fn, [(x, "rc"), (g, "c"), (tgt, "rc")], [f32], [1, 1], L=cfg.L, C=D, tl=256, tc=D, name="final_loss")


def _gate_bwd(dx, out, gate, cfg, name):
    def fn(i, ni, dxv, ov, gv):
        return [dxv * gv], [_colsum(dxv * ov.astype(f32))]
    return _rowwise(fn, [(dx, "rc"), (out, "rc"), (gate, "c")], [bf16], [1], L=cfg.L, C=cfg.D, tl=512, tc=cfg.D, name=name)


def _glu_bwd(dz, g, pre, cfg):
    def fn(i, ni, dzv, gv, pv):
        dzv = dzv.astype(f32)
        gv = gv.astype(f32)
        s = _sigmoid(pv)
        dpre = dzv * gv * s * (1.0 - s)
        return [dpre, dzv * s], [_colsum(dpre)]
    return _rowwise(fn, [(dz, "rc"), (g, "rc"), (pre, "rc")], [bf16, f32], [1], L=cfg.L, C=cfg.D, tl=512, tc=cfg.D, name="glu_bwd")


def _shift_rows(av, pv, k, i):
    rows = lax.broadcasted_iota(jnp.int32, av.shape, 0)
    cur = pltpu.roll(av, k, 0)
    prev = pltpu.roll(pv, k, 0)
    prev = jnp.where(i > 0, prev, 0.0)
    prev_full = jnp.concatenate([prev, jnp.zeros((av.shape[0] - pv.shape[0], av.shape[1]), av.dtype)], axis=0) \
        if av.shape[0] > pv.shape[0] else prev
    return jnp.where(rows >= k, cur, prev_full)


def _shift_rows_up(av, nv, k, i, ni):
    n, h = av.shape[0], nv.shape[0]
    rows = lax.broadcasted_iota(jnp.int32, av.shape, 0)
    cur = pltpu.roll(av, n - k, 0)
    nxt = pltpu.roll(nv, h - k, 0)
    nxt = jnp.where(i < ni - 1, nxt, 0.0)
    nxt_full = jnp.concatenate([jnp.zeros((n - h, av.shape[1]), av.dtype), nxt], axis=0) if n > h else nxt
    return jnp.where(rows < n - k, cur, nxt_full)


def _conv3(av, pv, w, i):
    return w[0:1] * _shift_rows(av, pv, 2, i) + w[1:2] * _shift_rows(av, pv, 1, i) + w[2:3] * av


def _conv_act_fwd(a, conv_w, conv_b, cfg):
    F = cfg.F
    tc = _tile(F, 1408)
    nb = F // tc

    def fn(i, ni, au, av, pu, pv, wu, wv, bu, bv):
        cu = _conv3(au.astype(f32), pu.astype(f32), wu, i) + bu
        cv = _conv3(av.astype(f32), pv.astype(f32), wv, i) + bv
        return [cu * _sigmoid(cu) * cv], []
    ins = [(a, "rc"), (a, "rc", nb), (a, "prev"), (a, "prev", nb), (conv_w, "c3"), (conv_w, "c3", nb), (conv_b, "c"), (conv_b, "c", nb)]
    return _rowwise(fn, ins, [bf16], [], L=cfg.L, C=F, tl=512, tc=tc, name="conv_act_fwd")[0]


def _conv_act_bwd1(dact, a, conv_w, conv_b, cfg):
    F = cfg.F
    tc = _tile(F, 1408)
    nb = F // tc

    def fn(i, ni, dav, au, av, pu, pv, wu, wv, bu, bv):
        dav = dav.astype(f32)
        au, av, pu, pv = au.astype(f32), av.astype(f32), pu.astype(f32), pv.astype(f32)
        au1, au2 = _shift_rows(au, pu, 1, i), _shift_rows(au, pu, 2, i)
        av1, av2 = _shift_rows(av, pv, 1, i), _shift_rows(av, pv, 2, i)
        cu = wu[0:1] * au2 + wu[1:2] * au1 + wu[2:3] * au + bu
        cv = wv[0:1] * av2 + wv[1:2] * av1 + wv[2:3] * av + bv
        s = _sigmoid(cu)
        dcu = dav * cv * (s * (1.0 + cu * (1.0 - s)))
        dcv = dav * cu * s
        dwu = jnp.concatenate([_colsum(dcu * au2), _colsum(dcu * au1), _colsum(dcu * au)], axis=0)
        dwv = jnp.concatenate([_colsum(dcv * av2), _colsum(dcv * av1), _colsum(dcv * av)], axis=0)
        return [dcu, dcv], [dwu, dwv, _colsum(dcu), _colsum(dcv)]
    ins = [(dact, "rc"), (a, "rc"), (a, "rc", nb), (a, "prev"), (a, "prev", nb), (conv_w, "c3"), (conv_w, "c3", nb),
           (conv_b, "c"), (conv_b, "c", nb)]
    return _rowwise(fn, ins, [bf16, bf16], [3, 3, 1, 1], L=cfg.L, C=F, tl=512, tc=tc, name="conv_act_bwd1")


def _conv_bwd2(dc, w, cfg, name):
    F = cfg.F
    tc = _tile(F, 1408)

    def fn(i, ni, dcv, nxt, wv):
        dcv, nxt = dcv.astype(f32), nxt.astype(f32)
        return [wv[2:3] * dcv + wv[1:2] * _shift_rows_up(dcv, nxt, 1, i, ni) + wv[0:1] * _shift_rows_up(dcv, nxt, 2, i, ni)], []
    return _rowwise(fn, [(dc, "rc"), (dc, "next"), (w, "c3")], [bf16], [], L=cfg.L, C=F, tl=512, tc=tc, name=name)[0]


NSLAB = 8


def _s5_tables(abar_re, abar_im, lam_re, lam_im, step, cfg):
    J = cfg.G // 8
    expo = jnp.array([r + 1 for r in range(8)] + [8 * 2 ** p for p in range(8)], f32)[:, None, None]
    mag = jnp.exp(lam_re * step * expo)
    ang = lam_im * step * expo
    t_re = (mag * jnp.cos(ang)).reshape(16, J, 8 * cfg.P).transpose(1, 0, 2)
    t_im = (mag * jnp.sin(ang)).reshape(16, J, 8 * cfg.P).transpose(1, 0, 2)
    tab = jnp.concatenate([t_re, t_im], axis=-1)
    arow = jnp.concatenate([abar_re.reshape(J, 1, 8 * cfg.P), abar_im.reshape(J, 1, 8 * cfg.P)], axis=-1)
    return arow, tab


def _s5_mats(bbar_re, bbar_im, c_re, c_im, cfg):
    J, P, H = cfg.G // 8, cfg.P, cfg.H
    eye = jnp.eye(8, dtype=f32)

    def bd_in(bb):
        bb = bb.reshape(J, 8, P, H)
        return jnp.einsum("jgph,gk->jghkp", bb, eye).reshape(J, 8 * H, 8 * P)

    def bd_out(cc):
        cc = cc.reshape(J, 8, H, P)
        return jnp.einsum("jghp,gk->jgpkh", cc, eye).reshape(J, 8 * P, 8 * H)

    bmat = jnp.concatenate([bd_in(bbar_re), bd_in(bbar_im)], axis=2).astype(bf16)
    cmat = jnp.concatenate([bd_out(c_re), -bd_out(c_im)], axis=1).astype(bf16)
    return bmat, cmat


def _s5_unmats(dbmat, dcmat, cfg):
    J, P, H = cfg.G // 8, cfg.P, cfg.H
    eye = jnp.eye(8, dtype=f32)
    db = dbmat.reshape(J, 8, H, 2, 8, P)
    db = jnp.einsum("jghckp,gk->cjgph", db, eye).reshape(2, cfg.G, P, H)
    dc = dcmat.reshape(J, 2, 8, P, 8, H)
    dc = jnp.einsum("jcgpkh,gk->cjghp", dc, eye).reshape(2, cfg.G, H, P)
    return db[0], db[1], dc[0], -dc[1]


def _chunk_scan(x_ref, row0, nt, arow_ref, tab_ref, c0, reverse):
    sg = -1.0 if reverse else 1.0
    rows = lax.broadcasted_iota(jnp.int32, (nt, LANES), 0)
    order = list(range(7, -1, -1)) if reverse else list(range(8))

    def ld(k, r):
        return x_ref[k, pl.ds(row0 + r, nt, stride=8), :]

    def tab(row, k):
        return tab_ref[pl.ds(row, 1), pl.ds(k * LANES, LANES)]

    carries = [None] * NSLAB
    for k in range(4):
        ar = arow_ref[:, pl.ds(k * LANES, LANES)]
        ai = sg * arow_ref[:, pl.ds((4 + k) * LANES, LANES)]
        sr, si = ld(k, order[0]), ld(4 + k, order[0])
        for r in order[1:]:
            sr, si = ar * sr - ai * si + ld(k, r), ar * si + ai * sr + ld(4 + k, r)
        if reverse:
            cr = jnp.where(rows == nt - 1, c0[k], pltpu.roll(sr, nt - 1, 0))
            ci = jnp.where(rows == nt - 1, c0[4 + k], pltpu.roll(si, nt - 1, 0))
        else:
            cr = jnp.where(rows == 0, c0[k], pltpu.roll(sr, 1, 0))
            ci = jnp.where(rows == 0, c0[4 + k], pltpu.roll(si, 1, 0))
        d, p = 1, 0
        while d < nt:
            qr, qi = tab(8 + p, k), sg * tab(8 + p, 4 + k)
            if reverse:
                shr, shi, m = pltpu.roll(cr, nt - d, 0), pltpu.roll(ci, nt - d, 0), rows < nt - d
            else:
                shr, shi, m = pltpu.roll(cr, d, 0), pltpu.roll(ci, d, 0), rows >= d
            cr, ci = cr + jnp.where(m, qr * shr - qi * shi, 0.0), ci + jnp.where(m, qr * shi + qi * shr, 0.0)
            d, p = 2 * d, p + 1
        carries[k], carries[4 + k] = cr, ci
        sr, si = cr, ci
        for r in order:
            sr, si = ar * sr - ai * si + ld(k, r), ar * si + ai * sr + ld(4 + k, r)
            x_ref[k, pl.ds(row0 + r, nt, stride=8), :] = sr
            x_ref[4 + k, pl.ds(row0 + r, nt, stride=8), :] = si
    return carries


def _slabs_to_mat(x_ref, row0, n):
    return jnp.concatenate([x_ref[k, pl.ds(row0, n), :] for k in range(NSLAB)], axis=1)


def _mat_to_slabs(x_ref, row0, n, m):
    for k in range(NSLAB):
        x_ref[k, pl.ds(row0, n), :] = m[:, k * LANES:(k + 1) * LANES]


def _s5_fwd(u, bmat, cmat, drow, arow, tab, cfg):
    L, D, Tc = cfg.L, cfg.D, cfg.TC
    J, NC, nt = cfg.G // 8, L // Tc, Tc // 8
    W = NSLAB * LANES

    def body(u_ref, b_ref, c_ref, d_ref, a_ref, t_ref, y_ref, g_ref, cin_ref, x_ref, st_ref):
        c = pl.program_id(1)

        @pl.when(c == 0)
        def _():
            st_ref[...] = jnp.zeros_like(st_ref)

        cin_ref[...] = st_ref[...]
        ub = u_ref[...]
        _mat_to_slabs(x_ref, 0, Tc, jnp.dot(ub.astype(bf16), b_ref[...], preferred_element_type=f32))
        c0 = [st_ref[:, pl.ds(k * LANES, LANES)] for k in range(NSLAB)]
        _chunk_scan(x_ref, 0, nt, a_ref, t_ref, c0, False)
        for k in range(NSLAB):
            st_ref[:, pl.ds(k * LANES, LANES)] = x_ref[k, pl.ds(Tc - 1, 1), :]
        s = _slabs_to_mat(x_ref, 0, Tc).astype(bf16)
        y = jnp.dot(s, c_ref[...], preferred_element_type=f32) + d_ref[...] * ub
        y_ref[...] = y
        g_ref[...] = _gelu(y).astype(bf16)

    return pl.pallas_call(
        body, name="s5_fwd", grid=(J, NC),
        in_specs=[pl.BlockSpec((Tc, LANES), lambda j, c: (c, j)),
                  pl.BlockSpec((None, LANES, W), lambda j, c: (j, 0, 0)),
                  pl.BlockSpec((None, W, LANES), lambda j, c: (j, 0, 0)),
                  pl.BlockSpec((1, LANES), lambda j, c: (0, j)),
                  pl.BlockSpec((None, 1, W), lambda j, c: (j, 0, 0)),
                  pl.BlockSpec((None, 16, W), lambda j, c: (j, 0, 0))],
        out_specs=[pl.BlockSpec((Tc, LANES), lambda j, c: (c, j)),
                   pl.BlockSpec((Tc, LANES), lambda j, c: (c, j)),
                   pl.BlockSpec((None, None, 1, W), lambda j, c: (j, c, 0, 0))],
        out_shape=[jax.ShapeDtypeStruct((L, D), f32), jax.ShapeDtypeStruct((L, D), bf16),
                   jax.ShapeDtypeStruct((J, NC, 1, W), f32)],
        scratch_shapes=[pltpu.VMEM((NSLAB, Tc, LANES), f32), pltpu.VMEM((1, W), f32)],
        compiler_params=_cp(("parallel", "arbitrary")),
    )(u, bmat, cmat, drow, arow, tab)


def _s5_bwd(u, dy, cin, bmat, cmat, drow, arow, tab, cfg):
    L, D, Tc = cfg.L, cfg.D, cfg.TC
    J, NC, nt = cfg.G // 8, L // Tc, Tc // 8
    W = NSLAB * LANES
    PAD = 0

    def body(u_ref, dy_ref, cin_ref, b_ref, c_ref, d_ref, a_ref, t_ref,
             du_ref, db_ref, dc_ref, da_ref, dd_ref, s_ref, g_ref, gst_ref):
        c = pl.program_id(1)

        @pl.when(c == 0)
        def _():
            gst_ref[...] = jnp.zeros_like(gst_ref)
            db_ref[...] = jnp.zeros_like(db_ref)
            dc_ref[...] = jnp.zeros_like(dc_ref)
            da_ref[...] = jnp.zeros_like(da_ref)
            dd_ref[...] = jnp.zeros_like(dd_ref)

        ub, dyb = u_ref[...], dy_ref[...]
        ub16, dy16 = ub.astype(bf16), dyb.astype(bf16)
        _mat_to_slabs(s_ref, PAD, Tc, jnp.dot(ub16, b_ref[...], preferred_element_type=f32))
        c0 = [cin_ref[:, pl.ds(k * LANES, LANES)] for k in range(NSLAB)]
        tile_in = _chunk_scan(s_ref, PAD, nt, a_ref, t_ref, c0, False)
        _mat_to_slabs(g_ref, 0, Tc, lax.dot_general(dy16, c_ref[...], _DIMS["nt"], preferred_element_type=f32))
        g0 = [gst_ref[:, pl.ds(k * LANES, LANES)] for k in range(NSLAB)]
        _chunk_scan(g_ref, 0, nt, a_ref, t_ref, g0, True)
        for k in range(NSLAB):
            gst_ref[:, pl.ds(k * LANES, LANES)] = g_ref[k, pl.ds(0, 1), :]
        for k in range(4):
            acc_r = jnp.zeros((nt, LANES), f32)
            acc_i = jnp.zeros((nt, LANES), f32)
            for r in range(8):
                gr = g_ref[k, pl.ds(r, nt, stride=8), :]
                gi = g_ref[4 + k, pl.ds(r, nt, stride=8), :]
                if r == 0:
                    pr, pi = tile_in[k], tile_in[4 + k]
                else:
                    pr = s_ref[k, pl.ds(PAD + r - 1, nt, stride=8), :]
                    pi = s_ref[4 + k, pl.ds(PAD + r - 1, nt, stride=8), :]
                acc_r += gr * pr + gi * pi
                acc_i += gi * pr - gr * pi
            da_ref[:, pl.ds(k * LANES, LANES)] += _colsum(acc_r)
            da_ref[:, pl.ds((4 + k) * LANES, LANES)] += _colsum(acc_i)
        gm = _slabs_to_mat(g_ref, 0, Tc).astype(bf16)
        sm = _slabs_to_mat(s_ref, PAD, Tc).astype(bf16)
        du = lax.dot_general(gm, b_ref[...], _DIMS["nt"], preferred_element_type=f32) + d_ref[...] * dyb
        du_ref[...] = du.astype(bf16)
        db_ref[...] += lax.dot_general(ub16, gm, _DIMS["tn"], preferred_element_type=f32)
        dc_ref[...] += lax.dot_general(sm, dy16, _DIMS["tn"], preferred_element_type=f32)
        dd_ref[...] += _colsum(dyb * ub)

    rc = lambda j, c: (NC - 1 - c, j)
    return pl.pallas_call(
        body, name="s5_bwd", grid=(J, NC),
        in_specs=[pl.BlockSpec((Tc, LANES), rc), pl.BlockSpec((Tc, LANES), rc),
                  pl.BlockSpec((None, None, 1, W), lambda j, c: (j, NC - 1 - c, 0, 0)),
                  pl.BlockSpec((None, LANES, W), lambda j, c: (j, 0, 0)),
                  pl.BlockSpec((None, W, LANES), lambda j, c: (j, 0, 0)),
                  pl.BlockSpec((1, LANES), lambda j, c: (0, j)),
                  pl.BlockSpec((None, 1, W), lambda j, c: (j, 0, 0)),
                  pl.BlockSpec((None, 16, W), lambda j, c: (j, 0, 0))],
        out_specs=[pl.BlockSpec((Tc, LANES), rc),
                   pl.BlockSpec((None, LANES, W), lambda j, c: (j, 0, 0)),
                   pl.BlockSpec((None, W, LANES), lambda j, c: (j, 0, 0)),
                   pl.BlockSpec((None, 1, W), lambda j, c: (j, 0, 0)),
                   pl.BlockSpec((1, LANES), lambda j, c: (0, j))],
        out_shape=[jax.ShapeDtypeStruct((L, D), bf16), jax.ShapeDtypeStruct((J, LANES, W), f32),
                   jax.ShapeDtypeStruct((J, W, LANES), f32), jax.ShapeDtypeStruct((J, 1, W), f32),
                   jax.ShapeDtypeStruct((1, D), f32)],
        scratch_shapes=[pltpu.VMEM((NSLAB, Tc + PAD, LANES), f32), pltpu.VMEM((NSLAB, Tc, LANES), f32),
                        pltpu.VMEM((1, W), f32)],
        compiler_params=_cp(("parallel", "arbitrary")),
    )(u, dy, cin, bmat, cmat, drow, arow, tab)


NEG = -1e30


def _attn_logits(q_ref, k_ref, fq_ref, fk_ref, qi, ki, bq, scale):
    s = lax.dot_general(q_ref[...], k_ref[...], _DIMS["nt"], preferred_element_type=f32) * scale
    s = s + fq_ref[...] - fk_ref[...]
    rows = qi * bq + lax.broadcasted_iota(jnp.int32, s.shape, 0)
    cols = ki * bq + lax.broadcasted_iota(jnp.int32, s.shape, 1)
    return s, cols <= rows


def _attn_fwd(q, kv, fq, fk, cfg):
    L, D, NH, DH, B = cfg.L, cfg.D, cfg.NH, cfg.DH, cfg.BQ
    nq = L // B
    scale = DH ** -0.5

    def body(q_ref, k_ref, v_ref, fq_ref, fk_ref, o_ref, lse_ref, m_ref, l_ref, acc_ref):
        qi, ki = pl.program_id(1), pl.program_id(2)

        @pl.when(ki == 0)
        def _():
            m_ref[...] = jnp.full_like(m_ref, NEG)
            l_ref[...] = jnp.zeros_like(l_ref)
            acc_ref[...] = jnp.zeros_like(acc_ref)

        @pl.when(ki <= qi)
        def _():
            s, mask = _attn_logits(q_ref, k_ref, fq_ref, fk_ref, qi, ki, B, scale)
            s = jnp.where(mask, s, NEG)
            m_prev = m_ref[...]
            m_new = jnp.maximum(m_prev, jnp.max(s, axis=1, keepdims=True))
            alpha = jnp.exp(m_prev - m_new)
            p = jnp.exp(s - m_new)
            l_ref[...] = alpha * l_ref[...] + jnp.sum(p, axis=1, keepdims=True)
            acc_ref[...] = alpha * acc_ref[...] + jnp.dot(p.astype(bf16), v_ref[...], preferred_element_type=f32)
            m_ref[...] = m_new

        @pl.when(ki == qi)
        def _():
            o_ref[...] = (acc_ref[...] / l_ref[...]).astype(o_ref.dtype)
            lse_ref[...] = m_ref[...] + jnp.log(l_ref[...])

    kmap = lambda h, qi, ki: (jnp.minimum(ki, qi), h)
    vmap_ = lambda h, qi, ki: (jnp.minimum(ki, qi), NH + h)
    return pl.pallas_call(
        body, name="attn_fwd", grid=(NH, nq, nq),
        in_specs=[pl.BlockSpec((B, DH), lambda h, qi, ki: (qi, h)),
                  pl.BlockSpec((B, DH), kmap), pl.BlockSpec((B, DH), vmap_),
                  pl.BlockSpec((None, B, 1), lambda h, qi, ki: (h, qi, 0)),
                  pl.BlockSpec((None, 1, B), lambda h, qi, ki: (h, 0, jnp.minimum(ki, qi)))],
        out_specs=[pl.BlockSpec((B, DH), lambda h, qi, ki: (qi, h)),
                   pl.BlockSpec((None, B, 1), lambda h, qi, ki: (h, qi, 0))],
        out_shape=[jax.ShapeDtypeStruct((L, D), bf16), jax.ShapeDtypeStruct((NH, L, 1), f32)],
        scratch_shapes=[pltpu.VMEM((B, 1), f32), pltpu.VMEM((B, 1), f32), pltpu.VMEM((B, DH), f32)],
        compiler_params=_cp(("parallel", "parallel", "arbitrary")),
    )(q, kv, kv, fq, fk)


def _attn_bwd_dq(q, kv, do, o, lse, fq, fk, cfg):
    L, D, NH, DH, B = cfg.L, cfg.D, cfg.NH, cfg.DH, cfg.BQ
    nq = L // B
    scale = DH ** -0.5

    def body(q_ref, k_ref, v_ref, do_ref, o_ref, lse_ref, fq_ref, fk_ref, dq_ref, dfq_ref, acc_ref, df_ref, dl_ref):
        qi, ki = pl.program_id(1), pl.program_id(2)

        @pl.when(ki == 0)
        def _():
            dl_ref[...] = jnp.sum(do_ref[...].astype(f32) * o_ref[...].astype(f32), axis=1, keepdims=True)
            acc_ref[...] = jnp.zeros_like(acc_ref)
            df_ref[...] = jnp.zeros_like(df_ref)

        @pl.when(ki <= qi)
        def _():
            s, mask = _attn_logits(q_ref, k_ref, fq_ref, fk_ref, qi, ki, B, scale)
            p = jnp.where(mask, jnp.exp(s - lse_ref[...]), 0.0)
            dp = lax.dot_general(do_ref[...], v_ref[...], _DIMS["nt"], preferred_element_type=f32)
            ds = p * (dp - dl_ref[...])
            df_ref[...] += jnp.sum(ds, axis=1, keepdims=True)
            acc_ref[...] += jnp.dot(ds.astype(bf16), k_ref[...], preferred_element_type=f32)

        @pl.when(ki == qi)
        def _():
            dq_ref[...] = (acc_ref[...] * scale).astype(dq_ref.dtype)
            dfq_ref[...] = df_ref[...]

    qmap = lambda h, qi, ki: (qi, h)
    return pl.pallas_call(
        body, name="attn_bwd_dq", grid=(NH, nq, nq),
        in_specs=[pl.BlockSpec((B, DH), qmap),
                  pl.BlockSpec((B, DH), lambda h, qi, ki: (jnp.minimum(ki, qi), h)),
                  pl.BlockSpec((B, DH), lambda h, qi, ki: (jnp.minimum(ki, qi), NH + h)),
                  pl.BlockSpec((B, DH), qmap), pl.BlockSpec((B, DH), qmap),
                  pl.BlockSpec((None, B, 1), lambda h, qi, ki: (h, qi, 0)),
                  pl.BlockSpec((None, B, 1), lambda h, qi, ki: (h, qi, 0)),
                  pl.BlockSpec((None, 1, B), lambda h, qi, ki: (h, 0, jnp.minimum(ki, qi)))],
        out_specs=[pl.BlockSpec((B, DH), qmap), pl.BlockSpec((None, B, 1), lambda h, qi, ki: (h, qi, 0))],
        out_shape=[jax.ShapeDtypeStruct((L, D), bf16), jax.ShapeDtypeStruct((NH, L, 1), f32)],
        scratch_shapes=[pltpu.VMEM((B, DH), f32), pltpu.VMEM((B, 1), f32), pltpu.VMEM((B, 1), f32)],
        compiler_params=_cp(("parallel", "parallel", "arbitrary")),
    )(q, kv, kv, do, o, lse, fq, fk)


def _attn_bwd_dkv(q, kv, do, o, lse, fq, fk, cfg):
    L, D, NH, DH, B = cfg.L, cfg.D, cfg.NH, cfg.DH, cfg.BQ
    nq = L // B
    scale = DH ** -0.5

    def body(q_ref, k_ref, v_ref, do_ref, o_ref, lse_ref, fq_ref, fk_ref, dk_ref, dv_ref, dfk_ref, dka_ref, dva_ref, dfa_ref):
        ki, qi = pl.program_id(1), pl.program_id(2)

        @pl.when(qi == 0)
        def _():
            dka_ref[...] = jnp.zeros_like(dka_ref)
            dva_ref[...] = jnp.zeros_like(dva_ref)
            dfa_ref[...] = jnp.zeros_like(dfa_ref)

        @pl.when(qi >= ki)
        def _():
            do = do_ref[...]
            delta = jnp.sum(do.astype(f32) * o_ref[...].astype(f32), axis=1, keepdims=True)
            s, mask = _attn_logits(q_ref, k_ref, fq_ref, fk_ref, qi, ki, B, scale)
            p = jnp.where(mask, jnp.exp(s - lse_ref[...]), 0.0)
            dva_ref[...] += lax.dot_general(p.astype(bf16), do, _DIMS["tn"], preferred_element_type=f32)
            dp = lax.dot_general(do, v_ref[...], _DIMS["nt"], preferred_element_type=f32)
            ds = p * (dp - delta)
            dka_ref[...] += lax.dot_general(ds.astype(bf16), q_ref[...], _DIMS["tn"], preferred_element_type=f32)
            dfa_ref[...] -= jnp.sum(ds, axis=0, keepdims=True)

        @pl.when(qi == nq - 1)
        def _():
            dk_ref[...] = (dka_ref[...] * scale).astype(dk_ref.dtype)
            dv_ref[...] = dva_ref[...].astype(dv_ref.dtype)
            dfk_ref[...] = dfa_ref[...]

    qmap = lambda h, ki, qi: (jnp.maximum(qi, ki), h)
    fqmap = lambda h, ki, qi: (h, jnp.maximum(qi, ki), 0)
    return pl.pallas_call(
        body, name="attn_bwd_dkv", grid=(NH, nq, nq),
        in_specs=[pl.BlockSpec((B, DH), qmap),
                  pl.BlockSpec((B, DH), lambda h, ki, qi: (ki, h)),
                  pl.BlockSpec((B, DH), lambda h, ki, qi: (ki, NH + h)),
                  pl.BlockSpec((B, DH), qmap), pl.BlockSpec((B, DH), qmap),
                  pl.BlockSpec((None, B, 1), fqmap), pl.BlockSpec((None, B, 1), fqmap),
                  pl.BlockSpec((None, 1, B), lambda h, ki, qi: (h, 0, ki))],
        out_specs=[pl.BlockSpec((B, DH), lambda h, ki, qi: (ki, h)), pl.BlockSpec((B, DH), lambda h, ki, qi: (ki, h)),
                   pl.BlockSpec((None, 1, B), lambda h, ki, qi: (h, 0, ki))],
        out_shape=[jax.ShapeDtypeStruct((L, D), bf16), jax.ShapeDtypeStruct((L, D), bf16),
                   jax.ShapeDtypeStruct((NH, 1, L), f32)],
        scratch_shapes=[pltpu.VMEM((B, DH), f32), pltpu.VMEM((B, DH), f32), pltpu.VMEM((1, B), f32)],
        compiler_params=_cp(("parallel", "parallel", "arbitrary")),
    )(q, kv, kv, do, o, lse, fq, fk)


def _tri_tables(nq, by_key):
    pairs = [(qi, ki) for ki in range(nq) for qi in range(ki, nq)] if by_key else \
            [(qi, ki) for qi in range(nq) for ki in range(qi + 1)]
    return jnp.array([p[0] for p in pairs], jnp.int32), jnp.array([p[1] for p in pairs], jnp.int32)


def _tri_call(body, name, cfg, by_key, in_specs, out_specs, out_shape, scratch_shapes, args):
    nq = cfg.L // cfg.BQ
    qt, kt = _tri_tables(nq, by_key)
    return pl.pallas_call(
        body, name=name,
        grid_spec=pltpu.PrefetchScalarGridSpec(num_scalar_prefetch=2, grid=(cfg.NH, nq * (nq + 1) // 2), in_specs=in_specs,
                                               out_specs=out_specs, scratch_shapes=scratch_shapes),
        out_shape=out_shape, compiler_params=_cp(("parallel", "arbitrary")),
    )(qt, kt, *args)


def _ta_fwd(q, kv, fk, cfg):
    L, D, NH, DH, B = cfg.L, cfg.D, cfg.NH, cfg.DH, cfg.BQ
    scale = DH ** -0.5

    def body(qt_ref, kt_ref, q_ref, k_ref, v_ref, fk_ref, o_ref, lse_ref, m_ref, acc_ref, a_ref, s_ref, p_ref):
        pid = pl.program_id(1)
        qi, ki = qt_ref[pid], kt_ref[pid]

        @pl.when(ki == 0)
        def _():
            m_ref[...] = jnp.full_like(m_ref, NEG)
            acc_ref[...] = jnp.zeros_like(acc_ref)

        def compute(masked):
            s_ref[...] = lax.dot_general(q_ref[...], k_ref[...], _DIMS["nt"], preferred_element_type=f32)
            fkv = fk_ref[...]

            def strip(rows, row0, c):
                t = s_ref[rows, :] * scale - fkv
                if masked:
                    t = jnp.where(_fa_mask(row0, t.shape), t, NEG)
                m_prev = m_ref[rows, :]
                m_new = jnp.maximum(m_prev, jnp.max(t, axis=1, keepdims=True))
                m_ref[rows, :] = m_new
                a_ref[rows, :] = jnp.exp(m_prev - m_new)
                p_ref[rows, :] = jnp.exp(t - m_new).astype(bf16)
                return c
            _fa_strips(B, strip, 0)
            v1 = jnp.concatenate([v_ref[...], jnp.ones((B, DH), bf16)], axis=1)
            acc_ref[...] = a_ref[...] * acc_ref[...] + jnp.dot(p_ref[...], v1, preferred_element_type=f32)

        @pl.when(ki < qi)
        def _():
            compute(False)

        @pl.when(ki == qi)
        def _():
            compute(True)
            l = acc_ref[:, DH:]
            o_ref[...] = (acc_ref[:, :DH] / l).astype(o_ref.dtype)
            lse_ref[...] = m_ref[...] + jnp.log(l[:, :1])

    col = pltpu.VMEM((B, 1), f32)
    return _tri_call(
        body, "attn_fwd", cfg, False,
        [pl.BlockSpec((B, DH), lambda h, p, qt, kt: (qt[p], h)),
         pl.BlockSpec((B, DH), lambda h, p, qt, kt: (kt[p], h)),
         pl.BlockSpec((B, DH), lambda h, p, qt, kt: (kt[p], NH + h)),
         pl.BlockSpec((None, 1, B), lambda h, p, qt, kt: (h, 0, kt[p]))],
        [pl.BlockSpec((B, DH), lambda h, p, qt, kt: (qt[p], h)),
         pl.BlockSpec((None, B, 1), lambda h, p, qt, kt: (h, qt[p], 0))],
        [jax.ShapeDtypeStruct((L, D), bf16), jax.ShapeDtypeStruct((NH, L, 1), f32)],
        [col, pltpu.VMEM((B, 2 * DH), f32), col, pltpu.VMEM((B, B), f32), pltpu.VMEM((B, B), bf16)],
        (q, kv, kv, fk))


def _ta_bwd_dq(q, kv, do, o, lse, fk, cfg):
    L, D, NH, DH, B = cfg.L, cfg.D, cfg.NH, cfg.DH, cfg.BQ
    scale = DH ** -0.5

    def body(qt_ref, kt_ref, q_ref, k_ref, v_ref, do_ref, o_ref, lse_ref, fk_ref, dq_ref, dfq_ref, dl_ref,
             acc_ref, s_ref, dp_ref, ds_ref):
        pid = pl.program_id(1)
        qi, ki = qt_ref[pid], kt_ref[pid]

        @pl.when(ki == 0)
        def _():
            dl_ref[...] = jnp.sum(do_ref[...].astype(f32) * o_ref[...].astype(f32), axis=1, keepdims=True)
            acc_ref[...] = jnp.zeros_like(acc_ref)

        def compute(masked):
            s_ref[...] = lax.dot_general(q_ref[...], k_ref[...], _DIMS["nt"], preferred_element_type=f32)
            dp_ref[...] = lax.dot_general(do_ref[...], v_ref[...], _DIMS["nt"], preferred_element_type=f32)
            fkv = fk_ref[...]

            def strip(rows, row0, c):
                p = jnp.exp(s_ref[rows, :] * scale - fkv - lse_ref[rows, :])
                if masked:
                    p = jnp.where(_fa_mask(row0, p.shape), p, 0.0)
                ds_ref[rows, :] = (p * (dp_ref[rows, :] - dl_ref[rows, :])).astype(bf16)
                return c
            _fa_strips(B, strip, 0)
            k1 = jnp.concatenate([k_ref[...], jnp.ones((B, DH), bf16)], axis=1)
            acc_ref[...] += jnp.dot(ds_ref[...], k1, preferred_element_type=f32)

        @pl.when(ki < qi)
        def _():
            compute(False)

        @pl.when(ki == qi)
        def _():
            compute(True)
            dq_ref[...] = (acc_ref[:, :DH] * scale).astype(dq_ref.dtype)
            dfq_ref[...] = acc_ref[:, DH:DH + 1]

    qmap = lambda h, p, qt, kt: (qt[p], h)
    cmap = lambda h, p, qt, kt: (h, qt[p], 0)
    return _tri_call(
        body, "attn_bwd_dq", cfg, False,
        [pl.BlockSpec((B, DH), qmap),
         pl.BlockSpec((B, DH), lambda h, p, qt, kt: (kt[p], h)),
         pl.BlockSpec((B, DH), lambda h, p, qt, kt: (kt[p], NH + h)),
         pl.BlockSpec((B, DH), qmap), pl.BlockSpec((B, DH), qmap),
         pl.BlockSpec((None, B, 1), cmap),
         pl.BlockSpec((None, 1, B), lambda h, p, qt, kt: (h, 0, kt[p]))],
        [pl.BlockSpec((B, DH), qmap), pl.BlockSpec((None, B, 1), cmap), pl.BlockSpec((None, B, 1), cmap)],
        [jax.ShapeDtypeStruct((L, D), bf16), jax.ShapeDtypeStruct((NH, L, 1), f32), jax.ShapeDtypeStruct((NH, L, 1), f32)],
        [pltpu.VMEM((B, 2 * DH), f32), pltpu.VMEM((B, B), f32), pltpu.VMEM((B, B), f32), pltpu.VMEM((B, B), bf16)],
        (q, kv, kv, do, o, lse, fk))


def _ta_bwd_dkv(q, kv, do, delta, lse, fk, cfg):
    L, D, NH, DH, B = cfg.L, cfg.D, cfg.NH, cfg.DH, cfg.BQ
    nq = L // B
    scale = DH ** -0.5

    def body(qt_ref, kt_ref, q_ref, k_ref, v_ref, do_ref, dl_ref, lse_ref, fk_ref, dk_ref, dv_ref, dfk_ref,
             dka_ref, dva_ref, s_ref, dp_ref, p_ref, ds_ref):
        pid = pl.program_id(1)
        qi, ki = qt_ref[pid], kt_ref[pid]

        @pl.when(qi == ki)
        def _():
            dka_ref[...] = jnp.zeros_like(dka_ref)
            dva_ref[...] = jnp.zeros_like(dva_ref)

        def compute(masked):
            s_ref[...] = lax.dot_general(q_ref[...], k_ref[...], _DIMS["nt"], preferred_element_type=f32)
            dp_ref[...] = lax.dot_general(do_ref[...], v_ref[...], _DIMS["nt"], preferred_element_type=f32)
            fkv = fk_ref[...]

            def strip(rows, row0, c):
                p = jnp.exp(s_ref[rows, :] * scale - fkv - lse_ref[rows, :])
                if masked:
                    p = jnp.where(_fa_mask(row0, p.shape), p, 0.0)
                p_ref[rows, :] = p.astype(bf16)
                ds_ref[rows, :] = (p * (dp_ref[rows, :] - dl_ref[rows, :])).astype(bf16)
                return c
            _fa_strips(B, strip, 0)
            q1 = jnp.concatenate([q_ref[...], jnp.ones((B, DH), bf16)], axis=1)
            dva_ref[...] += lax.dot_general(p_ref[...], do_ref[...], _DIMS["tn"], preferred_element_type=f32)
            dka_ref[...] += lax.dot_general(ds_ref[...], q1, _DIMS["tn"], preferred_element_type=f32)

        @pl.when(qi == ki)
        def _():
            compute(True)

        @pl.when(qi > ki)
        def _():
            compute(False)

        @pl.when(qi == nq - 1)
        def _():
            dk_ref[...] = (dka_ref[:, :DH] * scale).astype(dk_ref.dtype)
            dv_ref[...] = dva_ref[...].astype(dv_ref.dtype)
            dfk_ref[...] = -dka_ref[:, DH:DH + 1]

    qmap = lambda h, p, qt, kt: (qt[p], h)
    cmap = lambda h, p, qt, kt: (h, qt[p], 0)
    kmap = lambda h, p, qt, kt: (kt[p], h)
    return _tri_call(
        body, "attn_bwd_dkv", cfg, True,
        [pl.BlockSpec((B, DH), qmap), pl.BlockSpec((B, DH), kmap),
         pl.BlockSpec((B, DH), lambda h, p, qt, kt: (kt[p], NH + h)),
         pl.BlockSpec((B, DH), qmap), pl.BlockSpec((None, B, 1), cmap), pl.BlockSpec((None, B, 1), cmap),
         pl.BlockSpec((None, 1, B), lambda h, p, qt, kt: (h, 0, kt[p]))],
        [pl.BlockSpec((B, DH), kmap), pl.BlockSpec((B, DH), kmap),
         pl.BlockSpec((None, B, 1), lambda h, p, qt, kt: (h, kt[p], 0))],
        [jax.ShapeDtypeStruct((L, D), bf16), jax.ShapeDtypeStruct((L, D), bf16), jax.ShapeDtypeStruct((NH, L, 1), f32)],
        [pltpu.VMEM((B, 2 * DH), f32), pltpu.VMEM((B, DH), f32), pltpu.VMEM((B, B), f32), pltpu.VMEM((B, B), f32),
         pltpu.VMEM((B, B), bf16), pltpu.VMEM((B, B), bf16)],
        (q, kv, kv, do, delta, lse, fk))


STRIP = 32


def _fa_strips(nrows, fn, init):
    return lax.fori_loop(0, nrows // STRIP, lambda r, c: fn(pl.ds(pl.multiple_of(r * STRIP, STRIP), STRIP), r * STRIP, c),
                         init, unroll=True)


def _fa_mask(row0, shape):
    rows = row0 + lax.broadcasted_iota(jnp.int32, shape, 0)
    cols = lax.broadcasted_iota(jnp.int32, shape, 1)
    return cols <= rows


def _fa_fwd(q, kv, fk, cfg):
    L, D, NH, DH, B = cfg.L, cfg.D, cfg.NH, cfg.DH, cfg.BQ
    nq = L // B
    scale = DH ** -0.5

    def body(q_ref, k_ref, v_ref, fk_ref, o_ref, lse_ref, m_ref, l_ref, acc_ref, a_ref, s_ref, p_ref):
        qi, ki = pl.program_id(1), pl.program_id(2)

        @pl.when(ki == 0)
        def _():
            m_ref[...] = jnp.full_like(m_ref, NEG)
            l_ref[...] = jnp.zeros_like(l_ref)
            acc_ref[...] = jnp.zeros_like(acc_ref)

        def compute(masked):
            s_ref[...] = lax.dot_general(q_ref[...], k_ref[...], _DIMS["nt"], preferred_element_type=f32)
            fkv = fk_ref[...]

            def strip(rows, row0, c):
                t = s_ref[rows, :] * scale - fkv
                if masked:
                    t = jnp.where(_fa_mask(row0, t.shape), t, NEG)
                m_prev = m_ref[rows, :]
                m_new = jnp.maximum(m_prev, jnp.max(t, axis=1, keepdims=True))
                p = jnp.exp(t - m_new)
                alpha = jnp.exp(m_prev - m_new)
                l_ref[rows, :] = alpha * l_ref[rows, :] + jnp.sum(p, axis=1, keepdims=True)
                m_ref[rows, :] = m_new
                a_ref[rows, :] = alpha
                p_ref[rows, :] = p.astype(bf16)
                return c
            _fa_strips(B, strip, 0)
            acc_ref[...] = a_ref[...] * acc_ref[...] + jnp.dot(p_ref[...], v_ref[...], preferred_element_type=f32)

        @pl.when(ki < qi)
        def _():
            compute(False)

        @pl.when(ki == qi)
        def _():
            compute(True)
            o_ref[...] = (acc_ref[...] / l_ref[...]).astype(o_ref.dtype)
            lse_ref[...] = m_ref[...] + jnp.log(l_ref[...])

    col = pltpu.VMEM((B, 1), f32)
    return pl.pallas_call(
        body, name="attn_fwd", grid=(NH, nq, nq),
        in_specs=[pl.BlockSpec((B, DH), lambda h, qi, ki: (qi, h)),
                  pl.BlockSpec((B, DH), lambda h, qi, ki: (jnp.minimum(ki, qi), h)),
                  pl.BlockSpec((B, DH), lambda h, qi, ki: (jnp.minimum(ki, qi), NH + h)),
                  pl.BlockSpec((None, 1, B), lambda h, qi, ki: (h, 0, jnp.minimum(ki, qi)))],
        out_specs=[pl.BlockSpec((B, DH), lambda h, qi, ki: (qi, h)),
                   pl.BlockSpec((None, B, 1), lambda h, qi, ki: (h, qi, 0))],
        out_shape=[jax.ShapeDtypeStruct((L, D), bf16), jax.ShapeDtypeStruct((NH, L, 1), f32)],
        scratch_shapes=[col, col, pltpu.VMEM((B, DH), f32), col, pltpu.VMEM((B, B), f32), pltpu.VMEM((B, B), bf16)],
        compiler_params=_cp(("parallel", "parallel", "arbitrary")),
    )(q, kv, kv, fk)


def _fa_bwd_dq(q, kv, do, o, lse, fk, cfg):
    L, D, NH, DH, B = cfg.L, cfg.D, cfg.NH, cfg.DH, cfg.BQ
    nq = L // B
    scale = DH ** -0.5

    def body(q_ref, k_ref, v_ref, do_ref, o_ref, lse_ref, fk_ref, dq_ref, dfq_ref, dl_ref, acc_ref, df_ref, s_ref, dp_ref, ds_ref):
        qi, ki = pl.program_id(1), pl.program_id(2)

        @pl.when(ki == 0)
        def _():
            dl_ref[...] = jnp.sum(do_ref[...].astype(f32) * o_ref[...].astype(f32), axis=1, keepdims=True)
            acc_ref[...] = jnp.zeros_like(acc_ref)
            df_ref[...] = jnp.zeros_like(df_ref)

        def compute(masked):
            s_ref[...] = lax.dot_general(q_ref[...], k_ref[...], _DIMS["nt"], preferred_element_type=f32)
            dp_ref[...] = lax.dot_general(do_ref[...], v_ref[...], _DIMS["nt"], preferred_element_type=f32)
            fkv = fk_ref[...]

            def strip(rows, row0, c):
                p = jnp.exp(s_ref[rows, :] * scale - fkv - lse_ref[rows, :])
                if masked:
                    p = jnp.where(_fa_mask(row0, p.shape), p, 0.0)
                ds = p * (dp_ref[rows, :] - dl_ref[rows, :])
                df_ref[rows, :] += jnp.sum(ds, axis=1, keepdims=True)
                ds_ref[rows, :] = ds.astype(bf16)
                return c
            _fa_strips(B, strip, 0)
            acc_ref[...] += jnp.dot(ds_ref[...], k_ref[...], preferred_element_type=f32)

        @pl.when(ki < qi)
        def _():
            compute(False)

        @pl.when(ki == qi)
        def _():
            compute(True)
            dq_ref[...] = (acc_ref[...] * scale).astype(dq_ref.dtype)
            dfq_ref[...] = df_ref[...]

    qmap = lambda h, qi, ki: (qi, h)
    cmap = lambda h, qi, ki: (h, qi, 0)
    return pl.pallas_call(
        body, name="attn_bwd_dq", grid=(NH, nq, nq),
        in_specs=[pl.BlockSpec((B, DH), qmap),
                  pl.BlockSpec((B, DH), lambda h, qi, ki: (jnp.minimum(ki, qi), h)),
                  pl.BlockSpec((B, DH), lambda h, qi, ki: (jnp.minimum(ki, qi), NH + h)),
                  pl.BlockSpec((B, DH), qmap), pl.BlockSpec((B, DH), qmap),
                  pl.BlockSpec((None, B, 1), cmap),
                  pl.BlockSpec((None, 1, B), lambda h, qi, ki: (h, 0, jnp.minimum(ki, qi)))],
        out_specs=[pl.BlockSpec((B, DH), qmap), pl.BlockSpec((None, B, 1), cmap), pl.BlockSpec((None, B, 1), cmap)],
        out_shape=[jax.ShapeDtypeStruct((L, D), bf16), jax.ShapeDtypeStruct((NH, L, 1), f32),
                   jax.ShapeDtypeStruct((NH, L, 1), f32)],
        scratch_shapes=[pltpu.VMEM((B, DH), f32), pltpu.VMEM((B, 1), f32), pltpu.VMEM((B, B), f32),
                        pltpu.VMEM((B, B), f32), pltpu.VMEM((B, B), bf16)],
        compiler_params=_cp(("parallel", "parallel", "arbitrary")),
    )(q, kv, kv, do, o, lse, fk)


def _fa_bwd_dkv(q, kv, do, delta, lse, fk, cfg):
    L, D, NH, DH, B = cfg.L, cfg.D, cfg.NH, cfg.DH, cfg.BQ
    nq = L // B
    scale = DH ** -0.5

    def body(q_ref, k_ref, v_ref, do_ref, dl_ref, lse_ref, fk_ref, dk_ref, dv_ref, dfk_ref,
             dka_ref, dva_ref, dfa_ref, s_ref, dp_ref, p_ref, ds_ref):
        ki, qi = pl.program_id(1), pl.program_id(2)

        @pl.when(qi == 0)
        def _():
            dka_ref[...] = jnp.zeros_like(dka_ref)
            dva_ref[...] = jnp.zeros_like(dva_ref)
            dfa_ref[...] = jnp.zeros_like(dfa_ref)

        def compute(masked):
            s_ref[...] = lax.dot_general(q_ref[...], k_ref[...], _DIMS["nt"], preferred_element_type=f32)
            dp_ref[...] = lax.dot_general(do_ref[...], v_ref[...], _DIMS["nt"], preferred_element_type=f32)
            fkv = fk_ref[...]

            def strip(rows, row0, cs):
                p = jnp.exp(s_ref[rows, :] * scale - fkv - lse_ref[rows, :])
                if masked:
                    p = jnp.where(_fa_mask(row0, p.shape), p, 0.0)
                ds = p * (dp_ref[rows, :] - dl_ref[rows, :])
                p_ref[rows, :] = p.astype(bf16)
                ds_ref[rows, :] = ds.astype(bf16)
                return cs + ds
            cs = _fa_strips(B, strip, jnp.zeros((STRIP, B), f32))
            dva_ref[...] += lax.dot_general(p_ref[...], do_ref[...], _DIMS["tn"], preferred_element_type=f32)
            dka_ref[...] += lax.dot_general(ds_ref[...], q_ref[...], _DIMS["tn"], preferred_element_type=f32)
            dfa_ref[...] -= jnp.sum(cs, axis=0, keepdims=True)

        @pl.when(qi == ki)
        def _():
            compute(True)

        @pl.when(qi > ki)
        def _():
            compute(False)

        @pl.when(qi == nq - 1)
        def _():
            dk_ref[...] = (dka_ref[...] * scale).astype(dk_ref.dtype)
            dv_ref[...] = dva_ref[...].astype(dv_ref.dtype)
            dfk_ref[...] = dfa_ref[...]

    qmap = lambda h, ki, qi: (jnp.maximum(qi, ki), h)
    cmap = lambda h, ki, qi: (h, jnp.maximum(qi, ki), 0)
    return pl.pallas_call(
        body, name="attn_bwd_dkv", grid=(NH, nq, nq),
        in_specs=[pl.BlockSpec((B, DH), qmap),
                  pl.BlockSpec((B, DH), lambda h, ki, qi: (ki, h)),
                  pl.BlockSpec((B, DH), lambda h, ki, qi: (ki, NH + h)),
                  pl.BlockSpec((B, DH), qmap),
                  pl.BlockSpec((None, B, 1), cmap), pl.BlockSpec((None, B, 1), cmap),
                  pl.BlockSpec((None, 1, B), lambda h, ki, qi: (h, 0, ki))],
        out_specs=[pl.BlockSpec((B, DH), lambda h, ki, qi: (ki, h)), pl.BlockSpec((B, DH), lambda h, ki, qi: (ki, h)),
                   pl.BlockSpec((None, 1, B), lambda h, ki, qi: (h, 0, ki))],
        out_shape=[jax.ShapeDtypeStruct((L, D), bf16), jax.ShapeDtypeStruct((L, D), bf16),
                   jax.ShapeDtypeStruct((NH, 1, L), f32)],
        scratch_shapes=[pltpu.VMEM((B, DH), f32), pltpu.VMEM((B, DH), f32), pltpu.VMEM((1, B), f32),
                        pltpu.VMEM((B, B), f32), pltpu.VMEM((B, B), f32), pltpu.VMEM((B, B), bf16), pltpu.VMEM((B, B), bf16)],
        compiler_params=_cp(("parallel", "parallel", "arbitrary")),
    )(q, kv, kv, do, delta, lse, fk)


def _fox_logits(q, k, fqv, fkv, scale, masked):
    s = lax.dot_general(q, k, _DIMS["nt"], preferred_element_type=f32) * scale + fqv - fkv
    if masked:
        rows = lax.broadcasted_iota(jnp.int32, s.shape, 0)
        cols = lax.broadcasted_iota(jnp.int32, s.shape, 1)
        return s, cols <= rows
    return s, None


def _fox_fwd(q, kv, fq, fk, cfg):
    L, D, NH, DH, B = cfg.L, cfg.D, cfg.NH, cfg.DH, cfg.BQ
    nq = L // B
    scale = DH ** -0.5

    def body(q_ref, k_ref, v_ref, fq_ref, fk_ref, o_ref, lse_ref):
        qi = pl.program_id(1)
        qv, fqv = q_ref[...], fq_ref[...]

        def chunk(kj, carry, masked):
            m, l, acc = carry
            rows = pl.ds(pl.multiple_of(kj * B, B), B)
            s, mask = _fox_logits(qv, k_ref[rows, :], fqv, fk_ref[kj], scale, masked)
            if masked:
                s = jnp.where(mask, s, NEG)
            m_new = jnp.maximum(m, jnp.max(s, axis=1, keepdims=True))
            alpha = jnp.exp(m - m_new)
            p = jnp.exp(s - m_new)
            l = alpha * l + jnp.sum(p, axis=1, keepdims=True)
            acc = alpha * acc + jnp.dot(p.astype(bf16), v_ref[rows, :], preferred_element_type=f32)
            return m_new, l, acc

        init = (jnp.full((B, 1), NEG, f32), jnp.zeros((B, 1), f32), jnp.zeros((B, DH), f32))
        carry = lax.fori_loop(0, qi, lambda kj, c: chunk(kj, c, False), init)
        m, l, acc = chunk(qi, carry, True)
        o_ref[...] = (acc / l).astype(o_ref.dtype)
        lse_ref[...] = m + jnp.log(l)

    return pl.pallas_call(
        body, name="attn_fwd", grid=(NH, nq),
        in_specs=[pl.BlockSpec((B, DH), lambda h, qi: (qi, h)),
                  pl.BlockSpec((L, DH), lambda h, qi: (0, h)), pl.BlockSpec((L, DH), lambda h, qi: (0, NH + h)),
                  pl.BlockSpec((None, B, 1), lambda h, qi: (h, qi, 0)),
                  pl.BlockSpec((None, nq, 1, B), lambda h, qi: (h, 0, 0, 0))],
        out_specs=[pl.BlockSpec((B, DH), lambda h, qi: (qi, h)), pl.BlockSpec((None, B, 1), lambda h, qi: (h, qi, 0))],
        out_shape=[jax.ShapeDtypeStruct((L, D), bf16), jax.ShapeDtypeStruct((NH, L, 1), f32)],
        compiler_params=_cp(("parallel", "arbitrary")),
    )(q, kv, kv, fq, fk)


def _fox_bwd_dq(q, kv, do, o, lse, fq, fk, cfg):
    L, D, NH, DH, B = cfg.L, cfg.D, cfg.NH, cfg.DH, cfg.BQ
    nq = L // B
    scale = DH ** -0.5

    def body(q_ref, k_ref, v_ref, do_ref, o_ref, lse_ref, fq_ref, fk_ref, dq_ref, dfq_ref, dl_ref):
        qi = pl.program_id(1)
        qv, fqv, dov, lsev = q_ref[...], fq_ref[...], do_ref[...], lse_ref[...]
        delta = jnp.sum(dov.astype(f32) * o_ref[...].astype(f32), axis=1, keepdims=True)

        def chunk(kj, carry, masked):
            acc, df = carry
            rows = pl.ds(pl.multiple_of(kj * B, B), B)
            kv_ = k_ref[rows, :]
            s, mask = _fox_logits(qv, kv_, fqv, fk_ref[kj], scale, masked)
            p = jnp.exp(s - lsev)
            if masked:
                p = jnp.where(mask, p, 0.0)
            dp = lax.dot_general(dov, v_ref[rows, :], _DIMS["nt"], preferred_element_type=f32)
            ds = p * (dp - delta)
            return acc + jnp.dot(ds.astype(bf16), kv_, preferred_element_type=f32), df + jnp.sum(ds, axis=1, keepdims=True)

        carry = lax.fori_loop(0, qi, lambda kj, c: chunk(kj, c, False), (jnp.zeros((B, DH), f32), jnp.zeros((B, 1), f32)))
        acc, df = chunk(qi, carry, True)
        dq_ref[...] = (acc * scale).astype(dq_ref.dtype)
        dfq_ref[...] = df
        dl_ref[...] = delta

    qmap = lambda h, qi: (qi, h)
    cmap = lambda h, qi: (h, qi, 0)
    return pl.pallas_call(
        body, name="attn_bwd_dq", grid=(NH, nq),
        in_specs=[pl.BlockSpec((B, DH), qmap),
                  pl.BlockSpec((L, DH), lambda h, qi: (0, h)), pl.BlockSpec((L, DH), lambda h, qi: (0, NH + h)),
                  pl.BlockSpec((B, DH), qmap), pl.BlockSpec((B, DH), qmap),
                  pl.BlockSpec((None, B, 1), cmap), pl.BlockSpec((None, B, 1), cmap),
                  pl.BlockSpec((None, nq, 1, B), lambda h, qi: (h, 0, 0, 0))],
        out_specs=[pl.BlockSpec((B, DH), qmap), pl.BlockSpec((None, B, 1), cmap), pl.BlockSpec((None, B, 1), cmap)],
        out_shape=[jax.ShapeDtypeStruct((L, D), bf16), jax.ShapeDtypeStruct((NH, L, 1), f32),
                   jax.ShapeDtypeStruct((NH, L, 1), f32)],
        compiler_params=_cp(("parallel", "arbitrary")),
    )(q, kv, kv, do, o, lse, fq, fk)


def _fox_bwd_dkv(q, kv, do, delta, lse, fq, fk, cfg):
    L, D, NH, DH, B = cfg.L, cfg.D, cfg.NH, cfg.DH, cfg.BQ
    nq = L // B
    scale = DH ** -0.5

    def body(q_ref, k_ref, v_ref, do_ref, dl_ref, lse_ref, fq_ref, fk_ref, dk_ref, dv_ref, dfk_ref):
        ki = pl.program_id(1)
        kv_, vv, fkv = k_ref[...], v_ref[...], fk_ref[...]

        def block(qj, carry, masked):
            dk, dv, df = carry
            rows = pl.ds(pl.multiple_of(qj * B, B), B)
            qv, dov = q_ref[rows, :], do_ref[rows, :]
            s, mask = _fox_logits(qv, kv_, fq_ref[rows, :], fkv, scale, masked)
            p = jnp.exp(s - lse_ref[rows, :])
            if masked:
                p = jnp.where(mask, p, 0.0)
            dv = dv + lax.dot_general(p.astype(bf16), dov, _DIMS["tn"], preferred_element_type=f32)
            dp = lax.dot_general(dov, vv, _DIMS["nt"], preferred_element_type=f32)
            ds = p * (dp - dl_ref[rows, :])
            dk = dk + lax.dot_general(ds.astype(bf16), qv, _DIMS["tn"], preferred_element_type=f32)
            return dk, dv, df - jnp.sum(ds, axis=0, keepdims=True)

        init = (jnp.zeros((B, DH), f32), jnp.zeros((B, DH), f32), jnp.zeros((1, B), f32))
        carry = block(ki, init, True)
        dk, dv, df = lax.fori_loop(ki + 1, nq, lambda qj, c: block(qj, c, False), carry)
        dk_ref[...] = (dk * scale).astype(dk_ref.dtype)
        dv_ref[...] = dv.astype(dv_ref.dtype)
        dfk_ref[...] = df

    whole = lambda h, ki: (0, h)
    col = lambda h, ki: (h, 0, 0)
    return pl.pallas_call(
        body, name="attn_bwd_dkv", grid=(NH, nq),
        in_specs=[pl.BlockSpec((L, DH), whole),
                  pl.BlockSpec((B, DH), lambda h, ki: (ki, h)), pl.BlockSpec((B, DH), lambda h, ki: (ki, NH + h)),
                  pl.BlockSpec((L, DH), whole),
                  pl.BlockSpec((None, L, 1), col), pl.BlockSpec((None, L, 1), col), pl.BlockSpec((None, L, 1), col),
                  pl.BlockSpec((None, None, 1, B), lambda h, ki: (h, ki, 0, 0))],
        out_specs=[pl.BlockSpec((B, DH), lambda h, ki: (ki, h)), pl.BlockSpec((B, DH), lambda h, ki: (ki, h)),
                   pl.BlockSpec((None, None, 1, B), lambda h, ki: (h, ki, 0, 0))],
        out_shape=[jax.ShapeDtypeStruct((L, D), bf16), jax.ShapeDtypeStruct((L, D), bf16),
                   jax.ShapeDtypeStruct((NH, nq, 1, B), f32)],
        compiler_params=_cp(("parallel", "arbitrary")),
    )(q, kv, kv, do, delta, lse, fq, fk)


FCH = 256


def _split3(x):
    hi = x.astype(bf16)
    r1 = x - hi.astype(f32)
    mid = r1.astype(bf16)
    lo = (r1 - mid.astype(f32)).astype(bf16)
    return hi, mid, lo


def _tri_sum(tri, x):
    hi, mid, lo = _split3(x)
    return (jnp.dot(tri, hi, preferred_element_type=f32) + jnp.dot(tri, mid, preferred_element_type=f32)
            + jnp.dot(tri, lo, preferred_element_type=f32))


def _fgate_fwd(z, fb, cfg):
    L = cfg.L

    def body(z_ref, fb_ref, f_ref):
        r = lax.broadcasted_iota(jnp.int32, (FCH, FCH), 0)
        c = lax.broadcasted_iota(jnp.int32, (FCH, FCH), 1)
        tri = (c <= r).astype(bf16)
        carry = jnp.zeros((1, LANES), f32)
        for ch in range(L // FCH):
            x = z_ref[pl.ds(ch * FCH, FCH), :] + fb_ref[...]
            lf = jnp.minimum(x, 0.0) - jnp.log(1.0 + jnp.exp(-jnp.abs(x)))
            f_ref[pl.ds(ch * FCH, FCH), :] = _tri_sum(tri, lf) + carry
            carry = f_ref[pl.ds(ch * FCH + FCH - 1, 1), :]

    vm = pl.BlockSpec(memory_space=pltpu.VMEM)
    return pl.pallas_call(body, name="fgate_fwd", in_specs=[vm, vm], out_specs=vm,
                          out_shape=jax.ShapeDtypeStruct((L, LANES), f32), compiler_params=_cp())(z, fb)


def _fgate_bwd(df, z, fb, cfg):
    L = cfg.L

    def body(df_ref, z_ref, fb_ref, dz_ref, db_ref):
        r = lax.broadcasted_iota(jnp.int32, (FCH, FCH), 0)
        c = lax.broadcasted_iota(jnp.int32, (FCH, FCH), 1)
        tri = (c >= r).astype(bf16)
        carry = jnp.zeros((1, LANES), f32)
        dbs = jnp.zeros((1, LANES), f32)
        for ch in range(L // FCH - 1, -1, -1):
            suf = _tri_sum(tri, df_ref[pl.ds(ch * FCH, FCH), :]) + carry
            x = z_ref[pl.ds(ch * FCH, FCH), :] + fb_ref[...]
            dz = suf * _sigmoid(-x)
            dz_ref[pl.ds(ch * FCH, FCH), :] = dz
            dbs = dbs + _colsum(dz)
            carry = carry + _colsum(df_ref[pl.ds(ch * FCH, FCH), :])
        db_ref[...] = dbs

    vm = pl.BlockSpec(memory_space=pltpu.VMEM)
    return pl.pallas_call(body, name="fgate_bwd", in_specs=[vm, vm, vm], out_specs=[vm, vm],
                          out_shape=[jax.ShapeDtypeStruct((L, LANES), f32), jax.ShapeDtypeStruct((1, LANES), f32)],
                          compiler_params=_cp())(df, z, fb)


def _adamw(w, g, m, v, name):
    R, C = w.shape
    c1 = 1.0 - ADAM_B1 ** ADAM_STEP
    c2 = 1.0 - ADAM_B2 ** ADAM_STEP

    def fn(i, ni, wv, gv, mv, vv):
        mn = ADAM_B1 * mv + (1.0 - ADAM_B1) * gv
        vn = ADAM_B2 * vv + (1.0 - ADAM_B2) * (gv * gv)
        delta = -ADAM_LR * ((mn / c1) / (jnp.sqrt(vn / c2) + ADAM_EPS) + ADAM_WD * wv)
        return [delta, mn, vn], []
    tc = C if C % LANES else _tile(C, 1024)
    return _rowwise(fn, [(w, "rc"), (g, "rc"), (m, "rc"), (v, "rc")], [f32, f32, f32], [], L=R, C=C, tl=512, tc=tc, name=name)


def _sum_lead(x, out_dtype, name):
    n, R, C = x.shape
    tl = _tile(R, 512, HALO)
    tc = C if C % LANES else _tile(C, 1024)

    def body(x_ref, o_ref):
        acc = x_ref[0].astype(f32)
        for k in range(1, n):
            acc = acc + x_ref[k].astype(f32)
        o_ref[...] = acc.astype(o_ref.dtype)

    return pl.pallas_call(
        body, name=name, grid=(R // tl, C // tc),
        in_specs=[pl.BlockSpec((n, tl, tc), lambda i, j: (0, i, j))], out_specs=pl.BlockSpec((tl, tc), lambda i, j: (i, j)),
        out_shape=jax.ShapeDtypeStruct((R, C), out_dtype), compiler_params=_cp(("parallel", "parallel")),
    )(x)


def _add2(a, b, out_dtype, name):
    R, C = a.shape

    def fn(i, ni, av, bv):
        return [av.astype(f32) + bv.astype(f32)], []
    tc = C if C % LANES else _tile(C, 1024)
    return _rowwise(fn, [(a, "rc"), (b, "rc")], [out_dtype], [], L=R, C=C, tl=512, tc=tc, name=name)[0]


ANY = pl.BlockSpec(memory_space=pl.ANY)


def _place():
    x, y, c = lax.axis_index("x"), lax.axis_index("y"), lax.axis_index("c")
    return x, y, c


def _allgather8(blocks, name):
    n = len(blocks)

    def body(*refs):
        ins, outs = refs[:n], refs[n:2 * n]
        send_sems, recv_sems, local_sems = refs[2 * n:]
        x, y, c = _place()
        me, sibling = (x, y, c), (x, y, 1 - c)
        chips = [(1 - x, y), (x, 1 - y), (1 - x, 1 - y)]

        def slot(a, dev):
            return outs[a].at[4 * dev[0] + 2 * dev[1] + dev[2]]

        def copy(a, k, block, to, src=None):
            return pltpu.make_async_remote_copy(
                src_ref=slot(a, block) if src is None else src, dst_ref=slot(a, block),
                send_sem=send_sems.at[a * 7 + k], recv_sem=recv_sems.at[a * 7 + k], device_id=to, device_id_type=MESH)

        mine = [pltpu.make_async_copy(ins[a], slot(a, me), local_sems.at[a]) for a in range(n)]
        for cp in mine:
            cp.start()
        first = []
        for a in range(n):
            first.append(copy(a, 0, me, sibling, src=ins[a]))
            first += [copy(a, 1 + j, me, (*chip, c), src=ins[a]) for j, chip in enumerate(chips)]
        for cp in first:
            cp.start()
        passed = []
        for j, chip in enumerate(chips):
            for a in range(n):
                copy(a, 1 + j, (*chip, c), me).wait_recv()
                fwd = copy(a, 4 + j, (*chip, c), sibling)
                fwd.start()
                passed.append(fwd)
        for a in range(n):
            copy(a, 0, sibling, me).wait_recv()
        for j, chip in enumerate(chips):
            for a in range(n):
                copy(a, 4 + j, (*chip, 1 - c), me).wait_recv()
        for cp in first + passed:
            cp.wait_send()
        for cp in mine:
            cp.wait()

    outs = pl.pallas_call(
        body, name=name, in_specs=[ANY] * n, out_specs=[ANY] * n,
        out_shape=[jax.ShapeDtypeStruct((N_DEV,) + b.shape, b.dtype) for b in blocks],
        scratch_shapes=[pltpu.SemaphoreType.DMA((7 * n,)), pltpu.SemaphoreType.DMA((7 * n,)), pltpu.SemaphoreType.DMA((n,))],
    )(*blocks)
    return list(outs)


def _sibling_send(halves, name):
    n = len(halves)

    def body(*refs):
        ins, outs = refs[:n], refs[n:2 * n]
        send_sems, recv_sems = refs[2 * n:]
        x, y, c = _place()
        sends = [pltpu.make_async_remote_copy(src_ref=ins[a], dst_ref=outs[a], send_sem=send_sems.at[a],
                                              recv_sem=recv_sems.at[a], device_id=(x, y, 1 - c), device_id_type=MESH)
                 for a in range(n)]
        for cp in sends:
            cp.start()
        for cp in sends:
            cp.wait_recv()
        for cp in sends:
            cp.wait_send()

    outs = pl.pallas_call(
        body, name=name, in_specs=[ANY] * n, out_specs=[ANY] * n,
        out_shape=[jax.ShapeDtypeStruct(h.shape, h.dtype) for h in halves],
        scratch_shapes=[pltpu.SemaphoreType.DMA((n,)), pltpu.SemaphoreType.DMA((n,))],
    )(*halves)
    return list(outs)


def _sibling_swap_halves(grads, name):
    n = len(grads)

    def body(*refs):
        ins, outs = refs[:n], refs[n:2 * n]
        send_sems, recv_sems = refs[2 * n:]
        x, y, c = _place()
        sends = [pltpu.make_async_remote_copy(src_ref=ins[a].at[:, 1 - c], dst_ref=outs[a], send_sem=send_sems.at[a],
                                              recv_sem=recv_sems.at[a], device_id=(x, y, 1 - c), device_id_type=MESH)
                 for a in range(n)]
        for cp in sends:
            cp.start()
        for cp in sends:
            cp.wait_recv()
        for cp in sends:
            cp.wait_send()

    outs = pl.pallas_call(
        body, name=name, in_specs=[ANY] * n, out_specs=[ANY] * n,
        out_shape=[jax.ShapeDtypeStruct((4,) + g.shape[2:], g.dtype) for g in grads],
        scratch_shapes=[pltpu.SemaphoreType.DMA((n,)), pltpu.SemaphoreType.DMA((n,))],
    )(*grads)
    return list(outs)


def _chip_scatter(parts, name):
    n = len(parts)

    def body(*refs):
        ins, outs = refs[:n], refs[n:2 * n]
        send_sems, recv_sems = refs[2 * n:]
        x, y, c = _place()
        chips = [(1 - x, y), (x, 1 - y), (1 - x, 1 - y)]
        sends = []
        for a in range(n):
            for j, (px, py) in enumerate(chips):
                sends.append(pltpu.make_async_remote_copy(
                    src_ref=ins[a].at[2 * px + py], dst_ref=outs[a].at[j], send_sem=send_sems.at[a * 3 + j],
                    recv_sem=recv_sems.at[a * 3 + j], device_id=(px, py, c), device_id_type=MESH))
        for cp in sends:
            cp.start()
        for cp in sends:
            cp.wait_recv()
        for cp in sends:
            cp.wait_send()

    outs = pl.pallas_call(
        body, name=name, in_specs=[ANY] * n, out_specs=[ANY] * n,
        out_shape=[jax.ShapeDtypeStruct((3,) + p.shape[1:], p.dtype) for p in parts],
        scratch_shapes=[pltpu.SemaphoreType.DMA((3 * n,)), pltpu.SemaphoreType.DMA((3 * n,))],
    )(*parts)
    return list(outs)


def _sum_parts(own, got, chip, name):
    _, R, C = own.shape
    tl = _tile(R, 512, HALO)
    tc = C if C % LANES else _tile(C, 1024)

    def body(chip_ref, own_ref, got_ref, o_ref):
        acc = own_ref[...].astype(f32)
        for k in range(3):
            acc = acc + got_ref[k].astype(f32)
        o_ref[...] = acc

    return pl.pallas_call(
        body, name=name,
        grid_spec=pltpu.PrefetchScalarGridSpec(
            num_scalar_prefetch=1, grid=(R // tl, C // tc),
            in_specs=[pl.BlockSpec((None, tl, tc), lambda i, j, ch: (ch[0], i, j)),
                      pl.BlockSpec((3, tl, tc), lambda i, j, ch: (0, i, j))],
            out_specs=pl.BlockSpec((tl, tc), lambda i, j, ch: (i, j))),
        out_shape=jax.ShapeDtypeStruct((R, C), f32), compiler_params=_cp(("parallel", "parallel")),
    )(chip, own, got)


def _adamw_halves(w, m, v, g_mine, g_other, core, name):
    NL, R, C = w.shape
    r = R // 2
    tl = _tile(r, 512, HALO)
    tc = C if C % LANES else _tile(C, 1024)
    nh = r // tl
    c1 = 1.0 - ADAM_B1 ** ADAM_STEP
    c2 = 1.0 - ADAM_B2 ** ADAM_STEP

    def body(core_ref, w_ref, m_ref, v_ref, gm_ref, go_ref, g_out, d_out, m_out, v_out):
        i = pl.program_id(1)
        mine = lax.div(i, nh) == core_ref[0]
        gv = jnp.where(mine, gm_ref[...], go_ref[...])
        mn = ADAM_B1 * m_ref[...] + (1.0 - ADAM_B1) * gv
        vn = ADAM_B2 * v_ref[...] + (1.0 - ADAM_B2) * (gv * gv)
        g_out[...] = gv
        d_out[...] = -ADAM_LR * ((mn / c1) / (jnp.sqrt(vn / c2) + ADAM_EPS) + ADAM_WD * w_ref[...])
        m_out[...] = mn
        v_out[...] = vn

    full = pl.BlockSpec((None, tl, tc), lambda l, i, j, co: (l, i, j))
    mine_spec = pl.BlockSpec((None, tl, tc), lambda l, i, j, co: (l, jnp.clip(i - co[0] * nh, 0, nh - 1), j))
    other_spec = pl.BlockSpec((None, tl, tc), lambda l, i, j, co: (l, jnp.clip(i - (1 - co[0]) * nh, 0, nh - 1), j))
    return pl.pallas_call(
        body, name=name,
        grid_spec=pltpu.PrefetchScalarGridSpec(
            num_scalar_prefetch=1, grid=(NL, R // tl, C // tc),
            in_specs=[full, full, full, mine_spec, other_spec], out_specs=[full] * 4),
        out_shape=[jax.ShapeDtypeStruct((NL, R, C), f32)] * 4,
        compiler_params=_cp(("parallel", "parallel", "parallel")),
    )(core, w, m, v, g_mine, g_other)


def _s5_discretize(log_step, a_re, a_im, b_re, b_im):
    step = jnp.exp(log_step)[:, None]
    mag = jnp.exp(a_re * step)
    abar_re = mag * jnp.cos(a_im * step)
    abar_im = mag * jnp.sin(a_im * step)
    den = a_re * a_re + a_im * a_im
    nr = abar_re - 1.0
    fr = (nr * a_re + abar_im * a_im) / den
    fi = (abar_im * a_re - nr * a_im) / den
    bbar_re = fr[..., None] * b_re - fi[..., None] * b_im
    bbar_im = fr[..., None] * b_im + fi[..., None] * b_re
    return abar_re, abar_im, bbar_re, bbar_im


def _s5_prepare(p, cfg):
    abar_re, abar_im, bbar_re, bbar_im = _s5_discretize(p["log_step"], p["a_re"], p["a_im"], p["b_re"], p["b_im"])
    step = jnp.exp(p["log_step"])[:, None]
    arow, tab = _s5_tables(abar_re, abar_im, p["a_re"], p["a_im"], step, cfg)
    bmat, cmat = _s5_mats(bbar_re, bbar_im, p["c_re"], p["c_im"], cfg)
    return dict(arow=arow, tab=tab, bmat=bmat, cmat=cmat, drow=p["d"].reshape(1, cfg.D))


def _s5_param_grads(p, dbmat, dcmat, dabar, dd, cfg):
    J, P = cfg.G // 8, cfg.P
    dbb_re, dbb_im, dc_re, dc_im = _s5_unmats(dbmat, dcmat, cfg)
    da = dabar.reshape(J, 2, 8, P)
    da_re, da_im = da[:, 0].reshape(cfg.G, P), da[:, 1].reshape(cfg.G, P)
    _, vjp = jax.vjp(_s5_discretize, p["log_step"], p["a_re"], p["a_im"], p["b_re"], p["b_im"])
    dls, dare, daim, dbre, dbim = vjp((da_re, da_im, dbb_re, dbb_im))
    return dict(log_step=dls, a_re=dare, a_im=daim, b_re=dbre, b_im=dbim, c_re=dc_re, c_im=dc_im, d=dd.reshape(cfg.G, cfg.H))


def _resid_epi(acc, xv, gv):
    return xv + gv * acc, acc


def _ffn_fwd(x_in, g_norm, sc, sh, gate, w_up4, w_down, conv_w, conv_b, cfg, tag):
    h = _norm_mod_fwd(x_in, g_norm, sc, sh, cfg, f"ffn_norm_{tag}")
    a = _mm(h, w_up4, mode="nn", b4=True, tn=1408, out_dtypes=(bf16,), name=f"ffn_up_{tag}")
    act = _conv_act_fwd(a, conv_w, conv_b, cfg)
    x_out, out = _mm(act, w_down, mode="nn", extras=[(x_in, "mn"), (gate, "n")], epi=_resid_epi,
                     out_dtypes=(f32, bf16), name=f"ffn_down_{tag}")
    return x_out, dict(h=h, a=a, act=act, out=out)


def _ffn_bwd(dx, x_in, sv, g_norm, sc, gate, w_up4, w_down, conv_w, conv_b, cfg, tag):
    F = cfg.F
    dout, dgate = _gate_bwd(dx, sv["out"], gate, cfg, f"ffn_gate_bwd_{tag}")
    dact = _mm(dout, w_down, mode="nt", out_dtypes=(bf16,), name=f"ffn_dact_{tag}")
    dw_down = _mm(sv["act"], dout, mode="tn", out_dtypes=(bf16,), name=f"ffn_dwdown_{tag}")
    dcu, dcv, dwu, dwv, dbu, dbv = _conv_act_bwd1(dact, sv["a"], conv_w, conv_b, cfg)
    dau = _conv_bwd2(dcu, conv_w[:, :F], cfg, f"conv_bwd2u_{tag}")
    dav = _conv_bwd2(dcv, conv_w[:, F:], cfg, f"conv_bwd2v_{tag}")
    da = jnp.concatenate([dau, dav], axis=1)
    dh = _mm(da, w_up4, mode="nt", b4=True, tk=1408, out_dtypes=(bf16,), name=f"ffn_dh_{tag}")
    dw_up = _mm(sv["h"], da, mode="tn", out4=True, tn=1408, out_dtypes=(bf16,), name=f"ffn_dwup_{tag}")
    dx_in, A, B = _norm_mod_bwd(dh, x_in, g_norm, sc, dx, cfg, f"ffn_norm_bwd_{tag}")
    small = dict(norm_g=(1.0 + sc) * A, sc=g_norm * A, sh=B, gate=dgate,
                 conv_w=jnp.concatenate([dwu, dwv], axis=1), conv_b=jnp.concatenate([dbu, dbv], axis=1))
    return dx_in, dw_up, dw_down, small


def _local_step(cfg, x, tgt, mod, W, sp):
    D, NH = cfg.D, cfg.NH
    row = lambda v: v.reshape(1, -1)
    nmg0, nmg1 = row(sp["norm_mix_g"][0]), row(sp["norm_mix_g"][1])
    nfg0, nfg1 = row(sp["norm_ffn_g"][0]), row(sp["norm_ffn_g"][1])
    kvg, fng = row(sp["kv_norm_g"]), row(sp["final_norm_g"])
    cw0, cw1 = sp["ffn_conv_w"][0], sp["ffn_conv_w"][1]
    cb0, cb1 = row(sp["ffn_conv_b"][0]), row(sp["ffn_conv_b"][1])
    glu_b = row(sp["ssm_glu_b"])
    fb = jnp.zeros((1, LANES), f32).at[0, :NH].set(sp["forget_b"])
    s5p = {k: sp["ssm_" + k][0] for k in ("log_step", "a_re", "a_im", "b_re", "b_im", "c_re", "c_im", "d")}
    s5 = _s5_prepare(s5p, cfg)
    m0, m1 = mod["l0"], mod["l1"]

    h0 = _norm_mod_fwd(x, nmg0, m0["sc_m"], m0["sh_m"], cfg, "mix_norm_0")
    u = _mm(h0, W["ssm_w_in"], mode="nn", name="ssm_in")
    y, gact, cin = _s5_fwd(u, s5["bmat"], s5["cmat"], s5["drow"], s5["arow"], s5["tab"], cfg)

    def glu_epi(acc, bv, gv):
        pre = acc + bv
        return pre, gv.astype(f32) * _sigmoid(pre)
    pre, z = _mm(gact, W["ssm_glu_w"], mode="nn", extras=[(glu_b, "n"), (gact, "mn")], epi=glu_epi,
                 out_dtypes=(f32, bf16), name="ssm_glu")
    x1, out_m0 = _mm(z, W["ssm_w_out"], mode="nn", extras=[(x, "mn"), (m0["g_m"], "n")], epi=_resid_epi,
                     out_dtypes=(f32, bf16), name="ssm_out")
    x2, ffn0 = _ffn_fwd(x1, nfg0, m0["sc_f"], m0["sh_f"], m0["g_f"], W["ffn_w_up0"], W["ffn_w_down0"], cw0, cb0, cfg, "0")

    hk = _norm_mod_fwd(x2, kvg, mod["sc_kv"], mod["sh_kv"], cfg, "kv_norm")
    kvb = _mm(hk, W["kv_w"], mode="nn", out_dtypes=(bf16,), name="kv_proj")
    zf = _mm(hk, W["kv_wf"], mode="nn", name="kv_fproj")
    fc = _fgate_fwd(zf, fb, cfg)
    fct = fc[:, :NH].T
    fk = fct[:, None, :]

    h1 = _norm_mod_fwd(x2, nmg1, m1["sc_m"], m1["sh_m"], cfg, "mix_norm_1")
    q = _mm(h1, W["attn_w_q"], mode="nn", out_dtypes=(bf16,), name="attn_q")
    o, lse = _ta_fwd(q, kvb, fk, cfg)
    x3, out_m1 = _mm(o, W["attn_w_out"], mode="nn", extras=[(x2, "mn"), (m1["g_m"], "n")], epi=_resid_epi,
                     out_dtypes=(f32, bf16), name="attn_out")
    x4, ffn1 = _ffn_fwd(x3, nfg1, m1["sc_f"], m1["sh_f"], m1["g_f"], W["ffn_w_up1"], W["ffn_w_down1"], cw1, cb1, cfg, "1")

    dx, dfng, lcol = _final_loss(x4, fng, tgt, cfg)
    loss = (0.5 / D) * jnp.sum(lcol)

    dx, dw_up1, dw_down1, sf1 = _ffn_bwd(dx, x3, ffn1, nfg1, m1["sc_f"], m1["g_f"], W["ffn_w_up1"], W["ffn_w_down1"], cw1, cb1, cfg, "1")
    dout, dgm1 = _gate_bwd(dx, out_m1, m1["g_m"], cfg, "attn_gate_bwd")
    do = _mm(dout, W["attn_w_out"], mode="nt", out_dtypes=(bf16,), name="attn_do")
    dw_ao = _mm(o, dout, mode="tn", out_dtypes=(bf16,), name="attn_dwout")
    dq, dfq, delta = _ta_bwd_dq(q, kvb, do, o, lse, fk, cfg)
    dk, dv, dfk = _ta_bwd_dkv(q, kvb, do, delta, lse, fk, cfg)
    dh1 = _mm(dq, W["attn_w_q"], mode="nt", out_dtypes=(bf16,), name="attn_dh")
    dw_q = _mm(h1, dq, mode="tn", out_dtypes=(bf16,), name="attn_dwq")
    dx, A1, B1 = _norm_mod_bwd(dh1, x2, nmg1, m1["sc_m"], dx, cfg, "mix_norm_bwd_1")

    dfc = jnp.pad((dfq[:, :, 0] + dfk[:, :, 0]).T, ((0, 0), (0, LANES - NH)))
    dzf, dfb = _fgate_bwd(dfc, zf, fb, cfg)
    dkv = jnp.concatenate([dk, dv], axis=1)
    dhk1 = _mm(dkv, W["kv_w"], mode="nt", name="kv_dh1")
    dhk = _mm(dzf, W["kv_wf"], mode="nt", extras=[(dhk1, "mn")], epi=lambda acc, e: (acc + e,), out_dtypes=(bf16,), name="kv_dh2")
    dw_kv = _mm(hk, dkv, mode="tn", out_dtypes=(bf16,), name="kv_dw")
    dw_kf = _mm(hk, dzf, mode="tn", out_dtypes=(bf16,), name="kv_dwf")
    dx, Ak, Bk = _norm_mod_bwd(dhk, x2, kvg, mod["sc_kv"], dx, cfg, "kv_norm_bwd")

    dx, dw_up0, dw_down0, sf0 = _ffn_bwd(dx, x1, ffn0, nfg0, m0["sc_f"], m0["g_f"], W["ffn_w_up0"], W["ffn_w_down0"], cw0, cb0, cfg, "0")
    dout, dgm0 = _gate_bwd(dx, out_m0, m0["g_m"], cfg, "ssm_gate_bwd")
    dz = _mm(dout, W["ssm_w_out"], mode="nt", out_dtypes=(bf16,), name="ssm_dz")
    dw_so = _mm(z, dout, mode="tn", out_dtypes=(bf16,), name="ssm_dwout")
    dpre, dgd, dglub = _glu_bwd(dz, gact, pre, cfg)
    dy = _mm(dpre, W["ssm_glu_w"], mode="nt", extras=[(dgd, "mn"), (y, "mn")],
             epi=lambda acc, e, yv: ((acc + e) * _gelu_grad(yv),), name="ssm_dy")
    dw_glu = _mm(gact, dpre, mode="tn", out_dtypes=(bf16,), name="ssm_dwglu")
    du, dbm, dcm, dab, dd = _s5_bwd(u, dy, cin, s5["bmat"], s5["cmat"], s5["drow"], s5["arow"], s5["tab"], cfg)
    dh0 = _mm(du, W["ssm_w_in"], mode="nt", out_dtypes=(bf16,), name="ssm_dh")
    dw_in = _mm(h0, du, mode="tn", out_dtypes=(bf16,), name="ssm_dwin")
    dx, A0, B0 = _norm_mod_bwd(dh0, x, nmg0, m0["sc_m"], dx, cfg, "mix_norm_bwd_0")

    s5g = _s5_param_grads(s5p, dbm, dcm, dab, dd, cfg)
    big = dict(ssm_w_in=dw_in, ssm_glu_w=dw_glu, ssm_w_out=dw_so, attn_w_q=dw_q, attn_w_out=dw_ao,
               ffn_w_up0=dw_up0, ffn_w_up1=dw_up1, ffn_w_down0=dw_down0, ffn_w_down1=dw_down1,
               kv_w=jnp.concatenate([dw_kv, dw_kf[:, :NH]], axis=1))
    small = dict(
        norm_mix_g=jnp.concatenate([(1.0 + m0["sc_m"]) * A0, (1.0 + m1["sc_m"]) * A1], axis=0),
        norm_ffn_g=jnp.concatenate([sf0["norm_g"], sf1["norm_g"]], axis=0),
        ssm_glu_b=dglub, kv_norm_g=(1.0 + mod["sc_kv"]) * Ak, forget_b=dfb[0, :NH],
        ffn_conv_w=jnp.stack([sf0["conv_w"], sf1["conv_w"]]), ffn_conv_b=jnp.concatenate([sf0["conv_b"], sf1["conv_b"]], axis=0),
        final_norm_g=dfng, **{"ssm_" + k: v[None] for k, v in s5g.items()})
    dmod = [jnp.concatenate([B0, nmg0 * A0, dgm0, sf0["sh"], sf0["sc"], sf0["gate"]], axis=1),
            jnp.concatenate([B1, nmg1 * A1, dgm1, sf1["sh"], sf1["sc"], sf1["gate"]], axis=1),
            jnp.concatenate([Bk, kvg * Ak], axis=1)]
    return loss, dx, big, small, dmod


WEIGHTS = ["mod_w", "mod_b", "norm_mix_g", "norm_ffn_g", "ssm_w_in", "ssm_log_step", "ssm_a_re", "ssm_a_im", "ssm_b_re",
           "ssm_b_im", "ssm_c_re", "ssm_c_im", "ssm_d", "ssm_glu_w", "ssm_glu_b", "ssm_w_out", "kv_mod_w", "kv_mod_b",
           "kv_norm_g", "kv_w", "forget_b", "attn_w_q", "attn_w_out", "ffn_w_up", "ffn_conv_w", "ffn_conv_b", "ffn_w_down",
           "final_norm_g"]
ARGS = ["x", "c"] + WEIGHTS + ["loss_target"] + ["m_" + n for n in WEIGHTS] + ["v_" + n for n in WEIGHTS]
SMALL = ["mod_b", "norm_mix_g", "norm_ffn_g", "ssm_log_step", "ssm_a_re", "ssm_a_im", "ssm_b_re", "ssm_b_im", "ssm_c_re",
         "ssm_c_im", "ssm_d", "ssm_glu_b", "kv_mod_b", "kv_norm_g", "forget_b", "ffn_conv_w", "ffn_conv_b", "final_norm_g"]
PACK_ROWS = 512


def _pack(arrs):
    flat = jnp.concatenate([a.reshape(-1).astype(f32) for a in arrs])
    unit = PACK_ROWS * LANES
    n = -(-flat.shape[0] // unit) * unit
    return jnp.pad(flat, (0, n - flat.shape[0])).reshape(-1, LANES)


def _unpack(packed, shapes):
    flat, out, off = packed.reshape(-1), [], 0
    for s in shapes:
        n = math.prod(s)
        out.append(flat[off:off + n].reshape(s))
        off += n
    return out


def _silu(v):
    return v * _sigmoid(v)


def _half(w, c, axis):
    r = w.shape[axis] // 2
    return lax.dynamic_slice_in_dim(w, c * r, r, axis=axis)


def kernel(x, c, mod_w, mod_b, norm_mix_g, norm_ffn_g, ssm_w_in, ssm_log_step, ssm_a_re, ssm_a_im, ssm_b_re, ssm_b_im, ssm_c_re, ssm_c_im, ssm_d, ssm_glu_w, ssm_glu_b, ssm_w_out, kv_mod_w, kv_mod_b, kv_norm_g, kv_w, forget_b, attn_w_q, attn_w_out, ffn_w_up, ffn_conv_w, ffn_conv_b, ffn_w_down, final_norm_g, loss_target, m_mod_w, m_mod_b, m_norm_mix_g, m_norm_ffn_g, m_ssm_w_in, m_ssm_log_step, m_ssm_a_re, m_ssm_a_im, m_ssm_b_re, m_ssm_b_im, m_ssm_c_re, m_ssm_c_im, m_ssm_d, m_ssm_glu_w, m_ssm_glu_b, m_ssm_w_out, m_kv_mod_w, m_kv_mod_b, m_kv_norm_g, m_kv_w, m_forget_b, m_attn_w_q, m_attn_w_out, m_ffn_w_up, m_ffn_conv_w, m_ffn_conv_b, m_ffn_w_down, m_final_norm_g, v_mod_w, v_mod_b, v_norm_mix_g, v_norm_ffn_g, v_ssm_w_in, v_ssm_log_step, v_ssm_a_re, v_ssm_a_im, v_ssm_b_re, v_ssm_b_im, v_ssm_c_re, v_ssm_c_im, v_ssm_d, v_ssm_glu_w, v_ssm_glu_b, v_ssm_w_out, v_kv_mod_w, v_kv_mod_b, v_kv_norm_g, v_kv_w, v_forget_b, v_attn_w_q, v_attn_w_out, v_ffn_w_up, v_ffn_conv_w, v_ffn_conv_b, v_ffn_w_down, v_final_norm_g):
    a = dict(locals())
    assert list(a) == ARGS
    return _step(CFG, a)


def _step(cfg, a):
    D, F, NH = cfg.D, cfg.F, cfg.NH
    x_, y_, c_ = _place()
    chip, dev = 2 * x_ + y_, 4 * x_ + 2 * y_ + c_

    big_src = dict(ssm_w_in=a["ssm_w_in"][0], ssm_glu_w=a["ssm_glu_w"][0], ssm_w_out=a["ssm_w_out"][0],
                   attn_w_q=a["attn_w_q"][0], attn_w_out=a["attn_w_out"][0],
                   ffn_w_up0=a["ffn_w_up"][0], ffn_w_up1=a["ffn_w_up"][1],
                   ffn_w_down0=a["ffn_w_down"][0], ffn_w_down1=a["ffn_w_down"][1], kv_w=a["kv_w"])
    big_names = list(big_src)
    blocks = [_half(big_src[n], c_, 0).astype(bf16) for n in big_names]
    blocks += [_half(a["ssm_glu_b"], c_, 1), _half(a["ffn_conv_w"], c_, 2), a["c"]]
    got = _allgather8(blocks, "gather_weights")
    W = {}
    for n, g in zip(big_names, got):
        if n.startswith("ffn_w_up"):
            W[n] = g.reshape(4, D, 2 * F // 4)
        elif n == "kv_w":
            full = g.reshape(4, D, -1).transpose(1, 0, 2).reshape(D, -1)
            W["kv_w"] = full[:, :2 * D]
            W["kv_wf"] = jnp.pad(full[:, 2 * D:], ((0, 0), (0, LANES - NH)))
        else:
            W[n] = g.reshape(-1, D)
    glu_b_full = got[-3].reshape(D)
    conv_w_full = got[-2].transpose(1, 2, 0, 3).reshape(2, 3, 2 * F)
    c16 = jnp.pad(got[-1].reshape(N_DEV, D), ((0, 16 - N_DEV), (0, 0)))

    mcols = [_mm(c16, a["mod_w"][l], mode="nn", a_pro=_silu, name=f"mod_fwd_{l}") for l in range(2)]
    mcols.append(_mm(c16, a["kv_mod_w"], mode="nn", a_pro=_silu, name="mod_fwd_kv"))
    widths = [m.shape[1] for m in mcols]
    mall = _allgather8([jnp.concatenate(mcols, axis=1)[:N_DEV]], "gather_mod")[0][0::2]
    offs = [0, widths[0], widths[0] + widths[1]]
    rows = []
    for off, wd, bias in zip(offs, widths, [a["mod_b"][0], a["mod_b"][1], a["kv_mod_b"]]):
        fullm = mall[:, :, off:off + wd].transpose(1, 0, 2).reshape(N_DEV, 4 * wd) + bias
        rows.append(lax.dynamic_slice_in_dim(fullm, dev, 1, axis=0))
    mod = {}
    for l in range(2):
        mod[f"l{l}"] = dict(zip(["sh_m", "sc_m", "g_m", "sh_f", "sc_f", "g_f"], jnp.split(rows[l], 6, axis=1)))
    mod["sh_kv"], mod["sc_kv"] = jnp.split(rows[2], 2, axis=1)

    sp = {n: a[n] for n in ["norm_mix_g", "norm_ffn_g", "kv_norm_g", "final_norm_g", "ffn_conv_b", "forget_b", "ssm_log_step",
                            "ssm_a_re", "ssm_a_im", "ssm_b_re", "ssm_b_im", "ssm_c_re", "ssm_c_im", "ssm_d"]}
    sp["ssm_glu_b"], sp["ffn_conv_w"] = glu_b_full, conv_w_full
    loss, dx, big, small, dmod = _local_step(cfg, a["x"][0], a["loss_target"][0], mod, W, sp)
    loss = lax.psum(loss, ("x", "y", "c"))

    small["mod_b"] = jnp.concatenate([dmod[0], dmod[1]], axis=0)
    small["kv_mod_b"] = dmod[2]
    shapes = [(2, 6 * D) if n == "mod_b" else (1, D) if n == "ssm_glu_b" else (2, 3, 2 * F) if n == "ffn_conv_w"
              else a[n].shape for n in SMALL]
    packs = _allgather8([_pack([small[n] for n in SMALL])], "gather_small")[0]
    gsmall = dict(zip(SMALL, _unpack(_sum_lead(packs, f32, "sum_small"), shapes)))
    per_dev = packs.reshape(N_DEV, -1)
    sizes = [math.prod(s) for s in shapes]
    starts = dict(zip(SMALL, [sum(sizes[:i]) for i in range(len(sizes))]))

    def rows_of(name, l, width):
        st = starts[name] + l * 6 * D
        blk = lax.dynamic_slice(per_dev, (0, st + chip * width), (N_DEV, width))
        return jnp.pad(blk, ((0, 16 - N_DEV), (0, 0)))
    g_mod_w = jnp.stack([_mm(c16, rows_of("mod_b", l, 6 * D // 4), mode="tn", a_pro=_silu, name=f"mod_dw_{l}") for l in range(2)])
    g_kv_mod_w = _mm(c16, rows_of("kv_mod_b", 0, 2 * D // 4), mode="tn", a_pro=_silu, name="mod_dw_kv")
    gsmall["ssm_glu_b"] = lax.dynamic_slice_in_dim(gsmall["ssm_glu_b"], chip * (D // 4), D // 4, axis=1)
    gsmall["ffn_conv_w"] = lax.dynamic_slice_in_dim(gsmall["ffn_conv_w"], chip * (2 * F // 4), 2 * F // 4, axis=2)

    def blocks_of(n, g):
        if n.startswith("ffn_w_up"):
            return g.reshape(4, 2, D // 2, -1)
        if n == "kv_w":
            return g.reshape(D, 4, -1).transpose(1, 0, 2).reshape(4, 2, D // 2, -1)
        return g.reshape(4, 2, g.shape[0] // 8, g.shape[1])
    gb = [blocks_of(n, big[n]) for n in big_names]
    recv = _sibling_swap_halves(gb, "grad_sibling_swap")
    parts = []
    for n, g, r in zip(big_names, gb, recv):
        keep = lax.dynamic_index_in_dim(g, c_, axis=1, keepdims=False)
        rr, cc = keep.shape[1], keep.shape[2]
        parts.append(_add2(keep.reshape(4 * rr, cc), r.reshape(4 * rr, cc), bf16, f"grad_add_{n}").reshape(4, rr, cc))
    scattered = _chip_scatter(parts, "grad_chip_scatter")
    chip1, core1 = jnp.reshape(chip, (1,)).astype(jnp.int32), jnp.reshape(c_, (1,)).astype(jnp.int32)
    mine = {n: _sum_parts(p, q, chip1, f"grad_sum_{n}") for n, p, q in zip(big_names, parts, scattered)}
    other = dict(zip(big_names, _sibling_send([mine[n] for n in big_names], "grad_sibling_send")))

    grads = dict(gsmall)
    grads["mod_w"], grads["kv_mod_w"] = g_mod_w, g_kv_mod_w
    delta, new_m, new_v = {}, {}, {}
    members = dict(ssm_w_in=["ssm_w_in"], ssm_glu_w=["ssm_glu_w"], ssm_w_out=["ssm_w_out"], attn_w_q=["attn_w_q"],
                   attn_w_out=["attn_w_out"], kv_w=["kv_w"], ffn_w_up=["ffn_w_up0", "ffn_w_up1"],
                   ffn_w_down=["ffn_w_down0", "ffn_w_down1"])
    for n, parts_ in members.items():
        shp = a[n].shape
        three = lambda t: t.reshape(len(parts_), -1, shp[-1])
        g_, d_, m_, v_ = _adamw_halves(three(a[n]), three(a["m_" + n]), three(a["v_" + n]),
                                       jnp.stack([mine[p] for p in parts_]), jnp.stack([other[p] for p in parts_]),
                                       core1, f"adamw_{n}")
        grads[n], delta[n], new_m[n], new_v[n] = g_.reshape(shp), d_.reshape(shp), m_.reshape(shp), v_.reshape(shp)
    for n in ["mod_w", "kv_mod_w"]:
        shp = a[n].shape
        two = lambda t: t.reshape(-1, shp[-1])
        d_, m_, v_ = _adamw(two(a[n]), two(grads[n]), two(a["m_" + n]), two(a["v_" + n]), f"adamw_{n}")
        delta[n], new_m[n], new_v[n] = d_.reshape(shp), m_.reshape(shp), v_.reshape(shp)
    grads = {n: grads[n].reshape(a[n].shape) for n in WEIGHTS}
    sshapes = [a[n].shape for n in SMALL]
    d_, m_, v_ = _adamw(_pack([a[n] for n in SMALL]), _pack([grads[n] for n in SMALL]), _pack([a["m_" + n] for n in SMALL]),
                        _pack([a["v_" + n] for n in SMALL]), "adamw_small")
    for n, dd_, mm_, vv_ in zip(SMALL, _unpack(d_, sshapes), _unpack(m_, sshapes), _unpack(v_, sshapes)):
        delta[n], new_m[n], new_v[n] = dd_, mm_, vv_

    return (loss, dx[None], *[grads[n] for n in WEIGHTS], *[delta[n] for n in WEIGHTS],
            *[new_m[n] for n in WEIGHTS], *[new_v[n] for n in WEIGHTS])
```

```python
import collections
import functools
import math

import jax
import jax.numpy as jnp
from jax import lax
from jax.experimental import pallas as pl
from jax.experimental.pallas import tpu as pltpu

f32 = jnp.float32
bf16 = jnp.bfloat16
MESH = pl.DeviceIdType.MESH

LANES = 128
SUBLANES = 8
VMEM_BYTES_V7X = 64 * 1024 * 1024
VMEM_LIMIT = 56 * 1024 * 1024

Cfg = collections.namedtuple("Cfg", "L D G P H NH DH F TC BQ")
CFG = Cfg(L=4096, D=2048, G=128, P=64, H=16, NH=16, DH=128, F=5632, TC=512, BQ=512)
NORM_EPS = 1e-6
ADAM_LR, ADAM_B1, ADAM_B2, ADAM_EPS, ADAM_WD, ADAM_STEP = 0.001, 0.9, 0.999, 1e-08, 0.01, 10
N_DEV = 8


def _cp(sem=None):
    return pltpu.CompilerParams(dimension_semantics=sem, vmem_limit_bytes=VMEM_LIMIT)


def _tile(dim, pref, unit=LANES):
    if dim <= pref:
        return dim
    t = (pref // unit) * unit
    while t > unit and dim % t:
        t -= unit
    assert dim % t == 0, (dim, pref)
    return t


_DIMS = {"nn": (((1,), (0,)), ((), ())), "nt": (((1,), (1,)), ((), ())), "tn": (((0,), (0,)), ((), ()))}


def _mm(a, b, *, mode, name, tm=1024, tn=1024, tk=512, b4=False, out4=False, a_pro=None, extras=(), epi=None,
        out_dtypes=(f32,)):
    if mode == "tn":
        K, M = a.shape
    else:
        M, K = a.shape
    if b4:
        R, c4 = b.shape[1], b.shape[2]
        N = R if mode == "nt" else 4 * c4
        assert (K == 4 * c4) if mode == "nt" else (K == R)
    else:
        N = b.shape[0] if mode == "nt" else b.shape[1]
        assert K == (b.shape[1] if mode == "nt" else b.shape[0])
    n4 = N // 4
    tm = _tile(M, tm, LANES if mode == "tn" else SUBLANES * 2)
    tn = _tile(n4 if out4 or (b4 and mode != "nt") else N, tn)
    tk = _tile(b.shape[2] if (b4 and mode == "nt") else K, tk)
    nm, nn_, nk = M // tm, N // tn, K // tk

    a_spec = pl.BlockSpec((tk, tm), lambda i, j, k: (k, i)) if mode == "tn" else pl.BlockSpec((tm, tk), lambda i, j, k: (i, k))
    if b4 and mode == "nt":
        q = b.shape[2] // tk
        b_spec = pl.BlockSpec((None, tn, tk), lambda i, j, k: (lax.div(k, q), j, lax.rem(k, q)))
    elif b4:
        q = b.shape[2] // tn
        b_spec = pl.BlockSpec((None, tk, tn), lambda i, j, k: (lax.div(j, q), k, lax.rem(j, q)))
    elif mode == "nt":
        b_spec = pl.BlockSpec((tn, tk), lambda i, j, k: (j, k))
    else:
        b_spec = pl.BlockSpec((tk, tn), lambda i, j, k: (k, j))
    ex_specs = []
    for arr, kind in extras:
        if kind == "mn":
            ex_specs.append(pl.BlockSpec((tm, tn), lambda i, j, k: (i, j)))
        else:
            ex_specs.append(pl.BlockSpec((1, tn), lambda i, j, k: (0, j)))
    if out4:
        qo = n4 // tn
        o_spec = pl.BlockSpec((None, tm, tn), lambda i, j, k: (lax.div(j, qo), i, lax.rem(j, qo)))
        o_shapes = [jax.ShapeDtypeStruct((4, M, n4), dt) for dt in out_dtypes]
    else:
        o_spec = pl.BlockSpec((tm, tn), lambda i, j, k: (i, j))
        o_shapes = [jax.ShapeDtypeStruct((M, N), dt) for dt in out_dtypes]
    ne, no = len(extras), len(out_dtypes)
    dims = _DIMS[mode]

    def body(a_ref, b_ref, *rest):
        ex_refs, o_refs, acc_ref = rest[:ne], rest[ne:ne + no], rest[ne + no]
        k = pl.program_id(2)

        @pl.when(k == 0)
        def _():
            acc_ref[...] = jnp.zeros_like(acc_ref)

        av = a_ref[...]
        if a_pro is not None:
            av = a_pro(av)
        acc_ref[...] += lax.dot_general(av.astype(bf16), b_ref[...].astype(bf16), dims, preferred_element_type=f32)

        @pl.when(k == nk - 1)
        def _():
            acc = acc_ref[...]
            outs = (acc,) if epi is None else epi(acc, *[r[...] for r in ex_refs])
            for o_ref, o in zip(o_refs, outs):
                o_ref[...] = o.astype(o_ref.dtype)

    res = pl.pallas_call(
        body, name=name, grid=(nm, nn_, nk),
        in_specs=[a_spec, b_spec] + ex_specs, out_specs=[o_spec] * no, out_shape=o_shapes,
        scratch_shapes=[pltpu.VMEM((tm, tn), f32)],
        compiler_params=_cp(("parallel", "parallel", "arbitrary")),
    )(a, b, *[e[0] for e in extras])
    return res[0] if no == 1 else res


HALO = 16


def _rowwise(fn, ins, outs, accs, *, L, C, tl, tc, name):
    tl = _tile(L, tl, HALO)
    tc = _tile(C, tc)
    ni, nj = L // tl, C // tc
    hb = tl // HALO
    nh = L // HALO
    in_specs = []
    for spec in ins:
        kind = spec[1]
        off = spec[2] if len(spec) > 2 else 0
        if kind == "rc":
            in_specs.append(pl.BlockSpec((tl, tc), lambda j, i, off=off: (i, j + off)))
        elif kind == "c":
            in_specs.append(pl.BlockSpec((1, tc), lambda j, i, off=off: (0, j + off)))
        elif kind == "c3":
            in_specs.append(pl.BlockSpec((3, tc), lambda j, i, off=off: (0, j + off)))
        elif kind == "prev":
            in_specs.append(pl.BlockSpec((HALO, tc), lambda j, i, off=off: (jnp.maximum(i * hb - 1, 0), j + off)))
        elif kind == "next":
            in_specs.append(pl.BlockSpec((HALO, tc), lambda j, i, off=off: (jnp.minimum((i + 1) * hb, nh - 1), j + off)))
        else:
            raise ValueError(kind)
    out_specs = [pl.BlockSpec((tl, tc), lambda j, i: (i, j)) for _ in outs]
    out_specs += [pl.BlockSpec((r, tc), lambda j, i: (0, j)) for r in accs]
    out_shape = [jax.ShapeDtypeStruct((L, C), dt) for dt in outs] + [jax.ShapeDtypeStruct((r, C), f32) for r in accs]
    nin, nout, nacc = len(ins), len(outs), len(accs)

    def body(*refs):
        i = pl.program_id(1)
        tiles = [r[...] for r in refs[:nin]]
        o_vals, a_vals = fn(i, ni, *tiles)
        for r, v in zip(refs[nin:nin + nout], o_vals):
            r[...] = v.astype(r.dtype)
        if nacc:
            @pl.when(i == 0)
            def _():
                for r in refs[nin + nout:]:
                    r[...] = jnp.zeros_like(r)
            for r, v in zip(refs[nin + nout:], a_vals):
                r[...] += v

    res = pl.pallas_call(
        body, name=name, grid=(nj, ni), in_specs=in_specs, out_specs=out_specs, out_shape=out_shape,
        compiler_params=_cp(("parallel", "arbitrary")),
    )(*[s[0] for s in ins])
    return res


def _colsum(v):
    return jnp.sum(v, axis=0, keepdims=True)


def _sigmoid(x):
    return 1.0 / (1.0 + jnp.exp(-x))


_GELU_C = math.sqrt(2.0 / math.pi)


def _gelu(y):
    t = jnp.tanh(_GELU_C * (y + 0.044715 * y * y * y))
    return 0.5 * y * (1.0 + t)


def _gelu_grad(y):
    y2 = y * y
    t = jnp.tanh(_GELU_C * (y + 0.044715 * y * y2))
    return 0.5 * (1.0 + t) + 0.5 * y * (1.0 - t * t) * _GELU_C * (1.0 + 3.0 * 0.044715 * y2)


def _norm_mod_fwd(x, g, sc, sh, cfg, name):
    def fn(i, ni, xv, gv, scv, shv):
        rstd = lax.rsqrt(jnp.mean(xv * xv, axis=-1, keepdims=True) + NORM_EPS)
        return [xv * rstd * gv * (1.0 + scv) + shv], []
    return _rowwise(fn, [(x, "rc"), (g, "c"), (sc, "c"), (sh, "c")], [bf16], [], L=cfg.L, C=cfg.D, tl=256, tc=cfg.D, name=name)[0]


def _norm_mod_bwd(dh, x, g, sc, dres, cfg, name):
    def fn(i, ni, dhv, xv, gv, scv, *rest):
        dhv = dhv.astype(f32)
        rstd = lax.rsqrt(jnp.mean(xv * xv, axis=-1, keepdims=True) + NORM_EPS)
        xh = xv * rstd
        dxh = dhv * (gv * (1.0 + scv))
        dx = rstd * (dxh - xh * jnp.mean(dxh * xh, axis=-1, keepdims=True))
        if rest:
            dx = dx + rest[0]
        return [dx], [_colsum(dhv * xh), _colsum(dhv)]
    ins = [(dh, "rc"), (x, "rc"), (g, "c"), (sc, "c")] + ([(dres, "rc")] if dres is not None else [])
    return _rowwise(fn, ins, [f32], [1, 1], L=cfg.L, C=cfg.D, tl=256, tc=cfg.D, name=name)


def _final_loss(x, g, tgt, cfg):
    D = cfg.D

    def fn(i, ni, xv, gv, tv):
        rstd = lax.rsqrt(jnp.mean(xv * xv, axis=-1, keepdims=True) + NORM_EPS)
        xh = xv * rstd
        err = xh * gv - tv
        dy = err * (1.0 / D)
        dxh = dy * gv
        dx = rstd * (dxh - xh * jnp.mean(dxh * xh, axis=-1, keepdims=True))
        return [dx], [_colsum(dy * xh), _colsum(err * err)]
    return _rowwise(fn, [(x, "rc"), (g, "c"), (tgt, "rc")], [f32], [1, 1], L=cfg.L, C=D, tl=256, tc=D, name="final_loss")


def _gate_bwd(dx, out, gate, cfg, name):
    def fn(i, ni, dxv, ov, gv):
        return [dxv * gv], [_colsum(dxv * ov.astype(f32))]
    return _rowwise(fn, [(dx, "rc"), (out, "rc"), (gate, "c")], [bf16], [1], L=cfg.L, C=cfg.D, tl=512, tc=cfg.D, name=name)


def _glu_bwd(dz, g, pre, cfg):
    def fn(i, ni, dzv, gv, pv):
        dzv = dzv.astype(f32)
        gv = gv.astype(f32)
        s = _sigmoid(pv)
        dpre = dzv * gv * s * (1.0 - s)
        return [dpre, dzv * s], [_colsum(dpre)]
    return _rowwise(fn, [(dz, "rc"), (g, "rc"), (pre, "rc")], [bf16, f32], [1], L=cfg.L, C=cfg.D, tl=512, tc=cfg.D, name="glu_bwd")


def _shift_rows(av, pv, k, i):
    rows = lax.broadcasted_iota(jnp.int32, av.shape, 0)
    cur = pltpu.roll(av, k, 0)
    prev = pltpu.roll(pv, k, 0)
    prev = jnp.where(i > 0, prev, 0.0)
    prev_full = jnp.concatenate([prev, jnp.zeros((av.shape[0] - pv.shape[0], av.shape[1]), av.dtype)], axis=0) \
        if av.shape[0] > pv.shape[0] else prev
    return jnp.where(rows >= k, cur, prev_full)


def _shift_rows_up(av, nv, k, i, ni):
    n, h = av.shape[0], nv.shape[0]
    rows = lax.broadcasted_iota(jnp.int32, av.shape, 0)
    cur = pltpu.roll(av, n - k, 0)
    nxt = pltpu.roll(nv, h - k, 0)
    nxt = jnp.where(i < ni - 1, nxt, 0.0)
    nxt_full = jnp.concatenate([jnp.zeros((n - h, av.shape[1]), av.dtype), nxt], axis=0) if n > h else nxt
    return jnp.where(rows < n - k, cur, nxt_full)


def _conv3(av, pv, w, i):
    return w[0:1] * _shift_rows(av, pv, 2, i) + w[1:2] * _shift_rows(av, pv, 1, i) + w[2:3] * av


def _conv_act_fwd(a, conv_w, conv_b, cfg):
    F = cfg.F
    tc = _tile(F, 1408)
    nb = F // tc

    def fn(i, ni, au, av, pu, pv, wu, wv, bu, bv):
        cu = _conv3(au.astype(f32), pu.astype(f32), wu, i) + bu
        cv = _conv3(av.astype(f32), pv.astype(f32), wv, i) + bv
        return [cu * _sigmoid(cu) * cv], []
    ins = [(a, "rc"), (a, "rc", nb), (a, "prev"), (a, "prev", nb), (conv_w, "c3"), (conv_w, "c3", nb), (conv_b, "c"), (conv_b, "c", nb)]
    return _rowwise(fn, ins, [bf16], [], L=cfg.L, C=F, tl=512, tc=tc, name="conv_act_fwd")[0]


def _conv_act_bwd1(dact, a, conv_w, conv_b, cfg):
    F = cfg.F
    tc = _tile(F, 1408)
    nb = F // tc

    def fn(i, ni, dav, au, av, pu, pv, wu, wv, bu, bv):
        dav = dav.astype(f32)
        au, av, pu, pv = au.astype(f32), av.astype(f32), pu.astype(f32), pv.astype(f32)
        au1, au2 = _shift_rows(au, pu, 1, i), _shift_rows(au, pu, 2, i)
        av1, av2 = _shift_rows(av, pv, 1, i), _shift_rows(av, pv, 2, i)
        cu = wu[0:1] * au2 + wu[1:2] * au1 + wu[2:3] * au + bu
        cv = wv[0:1] * av2 + wv[1:2] * av1 + wv[2:3] * av + bv
        s = _sigmoid(cu)
        dcu = dav * cv * (s * (1.0 + cu * (1.0 - s)))
        dcv = dav * cu * s
        dwu = jnp.concatenate([_colsum(dcu * au2), _colsum(dcu * au1), _colsum(dcu * au)], axis=0)
        dwv = jnp.concatenate([_colsum(dcv * av2), _colsum(dcv * av1), _colsum(dcv * av)], axis=0)
        return [dcu, dcv], [dwu, dwv, _colsum(dcu), _colsum(dcv)]
    ins = [(dact, "rc"), (a, "rc"), (a, "rc", nb), (a, "prev"), (a, "prev", nb), (conv_w, "c3"), (conv_w, "c3", nb),
           (conv_b, "c"), (conv_b, "c", nb)]
    return _rowwise(fn, ins, [bf16, bf16], [3, 3, 1, 1], L=cfg.L, C=F, tl=512, tc=tc, name="conv_act_bwd1")


def _conv_bwd2(dc, w, cfg, name):
    F = cfg.F
    tc = _tile(F, 1408)

    def fn(i, ni, dcv, nxt, wv):
        dcv, nxt = dcv.astype(f32), nxt.astype(f32)
        return [wv[2:3] * dcv + wv[1:2] * _shift_rows_up(dcv, nxt, 1, i, ni) + wv[0:1] * _shift_rows_up(dcv, nxt, 2, i, ni)], []
    return _rowwise(fn, [(dc, "rc"), (dc, "next"), (w, "c3")], [bf16], [], L=cfg.L, C=F, tl=512, tc=tc, name=name)[0]


NSLAB = 8


def _s5_tables(abar_re, abar_im, lam_re, lam_im, step, cfg):
    J = cfg.G // 8
    expo = jnp.array([r + 1 for r in range(8)] + [8 * 2 ** p for p in range(8)], f32)[:, None, None]
    mag = jnp.exp(lam_re * step * expo)
    ang = lam_im * step * expo
    t_re = (mag * jnp.cos(ang)).reshape(16, J, 8 * cfg.P).transpose(1, 0, 2)
    t_im = (mag * jnp.sin(ang)).reshape(16, J, 8 * cfg.P).transpose(1, 0, 2)
    tab = jnp.concatenate([t_re, t_im], axis=-1)
    arow = jnp.concatenate([abar_re.reshape(J, 1, 8 * cfg.P), abar_im.reshape(J, 1, 8 * cfg.P)], axis=-1)
    return arow, tab


def _s5_mats(bbar_re, bbar_im, c_re, c_im, cfg):
    J, P, H = cfg.G // 8, cfg.P, cfg.H
    eye = jnp.eye(8, dtype=f32)

    def bd_in(bb):
        bb = bb.reshape(J, 8, P, H)
        return jnp.einsum("jgph,gk->jghkp", bb, eye).reshape(J, 8 * H, 8 * P)

    def bd_out(cc):
        cc = cc.reshape(J, 8, H, P)
        return jnp.einsum("jghp,gk->jgpkh", cc, eye).reshape(J, 8 * P, 8 * H)

    bmat = jnp.concatenate([bd_in(bbar_re), bd_in(bbar_im)], axis=2).astype(bf16)
    cmat = jnp.concatenate([bd_out(c_re), -bd_out(c_im)], axis=1).astype(bf16)
    return bmat, cmat


def _s5_unmats(dbmat, dcmat, cfg):
    J, P, H = cfg.G // 8, cfg.P, cfg.H
    eye = jnp.eye(8, dtype=f32)
    db = dbmat.reshape(J, 8, H, 2, 8, P)
    db = jnp.einsum("jghckp,gk->cjgph", db, eye).reshape(2, cfg.G, P, H)
    dc = dcmat.reshape(J, 2, 8, P, 8, H)
    dc = jnp.einsum("jcgpkh,gk->cjghp", dc, eye).reshape(2, cfg.G, H, P)
    return db[0], db[1], dc[0], -dc[1]


def _chunk_scan(x_ref, row0, nt, arow_ref, tab_ref, c0, reverse):
    sg = -1.0 if reverse else 1.0
    rows = lax.broadcasted_iota(jnp.int32, (nt, LANES), 0)
    order = list(range(7, -1, -1)) if reverse else list(range(8))

    def ld(k, r):
        return x_ref[k, pl.ds(row0 + r, nt, stride=8), :]

    def tab(row, k):
        return tab_ref[pl.ds(row, 1), pl.ds(k * LANES, LANES)]

    carries = [None] * NSLAB
    for k in range(4):
        ar = arow_ref[:, pl.ds(k * LANES, LANES)]
        ai = sg * arow_ref[:, pl.ds((4 + k) * LANES, LANES)]
        sr, si = ld(k, order[0]), ld(4 + k, order[0])
        for r in order[1:]:
            sr, si = ar * sr - ai * si + ld(k, r), ar * si + ai * sr + ld(4 + k, r)
        if reverse:
            cr = jnp.where(rows == nt - 1, c0[k], pltpu.roll(sr, nt - 1, 0))
            ci = jnp.where(rows == nt - 1, c0[4 + k], pltpu.roll(si, nt - 1, 0))
        else:
            cr = jnp.where(rows == 0, c0[k], pltpu.roll(sr, 1, 0))
            ci = jnp.where(rows == 0, c0[4 + k], pltpu.roll(si, 1, 0))
        d, p = 1, 0
        while d < nt:
            qr, qi = tab(8 + p, k), sg * tab(8 + p, 4 + k)
            if reverse:
                shr, shi, m = pltpu.roll(cr, nt - d, 0), pltpu.roll(ci, nt - d, 0), rows < nt - d
            else:
                shr, shi, m = pltpu.roll(cr, d, 0), pltpu.roll(ci, d, 0), rows >= d
            cr, ci = cr + jnp.where(m, qr * shr - qi * shi, 0.0), ci + jnp.where(m, qr * shi + qi * shr, 0.0)
            d, p = 2 * d, p + 1
        carries[k], carries[4 + k] = cr, ci
        sr, si = cr, ci
        for r in order:
            sr, si = ar * sr - ai * si + ld(k, r), ar * si + ai * sr + ld(4 + k, r)
            x_ref[k, pl.ds(row0 + r, nt, stride=8), :] = sr
            x_ref[4 + k, pl.ds(row0 + r, nt, stride=8), :] = si
    return carries


def _slabs_to_mat(x_ref, row0, n):
    return jnp.concatenate([x_ref[k, pl.ds(row0, n), :] for k in range(NSLAB)], axis=1)


def _mat_to_slabs(x_ref, row0, n, m):
    for k in range(NSLAB):
        x_ref[k, pl.ds(row0, n), :] = m[:, k * LANES:(k + 1) * LANES]


def _s5_fwd(u, bmat, cmat, drow, arow, tab, cfg, rider=None):
    L, D, Tc = cfg.L, cfg.D, cfg.TC
    J, NC, nt = cfg.G // 8, L // Tc, Tc // 8
    W = NSLAB * LANES

    def body(u_ref, b_ref, c_ref, d_ref, a_ref, t_ref, y_ref, g_ref, cin_ref, x_ref, st_ref):
        c = pl.program_id(1)

        @pl.when(c == 0)
        def _():
            st_ref[...] = jnp.zeros_like(st_ref)

        cin_ref[...] = st_ref[...]
        ub = u_ref[...]
        _mat_to_slabs(x_ref, 0, Tc, jnp.dot(ub.astype(bf16), b_ref[...], preferred_element_type=f32))
        c0 = [st_ref[:, pl.ds(k * LANES, LANES)] for k in range(NSLAB)]
        _chunk_scan(x_ref, 0, nt, a_ref, t_ref, c0, False)
        for k in range(NSLAB):
            st_ref[:, pl.ds(k * LANES, LANES)] = x_ref[k, pl.ds(Tc - 1, 1), :]
        s = _slabs_to_mat(x_ref, 0, Tc).astype(bf16)
        y = jnp.dot(s, c_ref[...], preferred_element_type=f32) + d_ref[...] * ub
        y_ref[...] = y
        g_ref[...] = _gelu(y).astype(bf16)

    outs, extra = _host_call(
        body, name="s5_fwd", grid=(J, NC), rider=rider, args=(u, bmat, cmat, drow, arow, tab),
        in_specs=[pl.BlockSpec((Tc, LANES), lambda j, c: (c, j)),
                  pl.BlockSpec((None, LANES, W), lambda j, c: (j, 0, 0)),
                  pl.BlockSpec((None, W, LANES), lambda j, c: (j, 0, 0)),
                  pl.BlockSpec((1, LANES), lambda j, c: (0, j)),
                  pl.BlockSpec((None, 1, W), lambda j, c: (j, 0, 0)),
                  pl.BlockSpec((None, 16, W), lambda j, c: (j, 0, 0))],
        out_specs=[pl.BlockSpec((Tc, LANES), lambda j, c: (c, j)),
                   pl.BlockSpec((Tc, LANES), lambda j, c: (c, j)),
                   pl.BlockSpec((None, None, 1, W), lambda j, c: (j, c, 0, 0))],
        out_shape=[jax.ShapeDtypeStruct((L, D), f32), jax.ShapeDtypeStruct((L, D), bf16),
                   jax.ShapeDtypeStruct((J, NC, 1, W), f32)],
        scratch_shapes=[pltpu.VMEM((NSLAB, Tc, LANES), f32), pltpu.VMEM((1, W), f32)])
    return (*outs, extra)


def _s5_bwd(u, dy, cin, bmat, cmat, drow, arow, tab, cfg, rider=None):
    L, D, Tc = cfg.L, cfg.D, cfg.TC
    J, NC, nt = cfg.G // 8, L // Tc, Tc // 8
    W = NSLAB * LANES
    PAD = 0

    def body(u_ref, dy_ref, cin_ref, b_ref, c_ref, d_ref, a_ref, t_ref,
             du_ref, db_ref, dc_ref, da_ref, dd_ref, s_ref, g_ref, gst_ref):
        c = pl.program_id(1)

        @pl.when(c == 0)
        def _():
            gst_ref[...] = jnp.zeros_like(gst_ref)
            db_ref[...] = jnp.zeros_like(db_ref)
            dc_ref[...] = jnp.zeros_like(dc_ref)
            da_ref[...] = jnp.zeros_like(da_ref)
            dd_ref[...] = jnp.zeros_like(dd_ref)

        ub, dyb = u_ref[...], dy_ref[...]
        ub16, dy16 = ub.astype(bf16), dyb.astype(bf16)
        _mat_to_slabs(s_ref, PAD, Tc, jnp.dot(ub16, b_ref[...], preferred_element_type=f32))
        c0 = [cin_ref[:, pl.ds(k * LANES, LANES)] for k in range(NSLAB)]
        tile_in = _chunk_scan(s_ref, PAD, nt, a_ref, t_ref, c0, False)
        _mat_to_slabs(g_ref, 0, Tc, lax.dot_general(dy16, c_ref[...], _DIMS["nt"], preferred_element_type=f32))
        g0 = [gst_ref[:, pl.ds(k * LANES, LANES)] for k in range(NSLAB)]
        _chunk_scan(g_ref, 0, nt, a_ref, t_ref, g0, True)
        for k in range(NSLAB):
            gst_ref[:, pl.ds(k * LANES, LANES)] = g_ref[k, pl.ds(0, 1), :]
        for k in range(4):
            acc_r = jnp.zeros((nt, LANES), f32)
            acc_i = jnp.zeros((nt, LANES), f32)
            for r in range(8):
                gr = g_ref[k, pl.ds(r, nt, stride=8), :]
                gi = g_ref[4 + k, pl.ds(r, nt, stride=8), :]
                if r == 0:
                    pr, pi = tile_in[k], tile_in[4 + k]
                else:
                    pr = s_ref[k, pl.ds(PAD + r - 1, nt, stride=8), :]
                    pi = s_ref[4 + k, pl.ds(PAD + r - 1, nt, stride=8), :]
                acc_r += gr * pr + gi * pi
                acc_i += gi * pr - gr * pi
            da_ref[:, pl.ds(k * LANES, LANES)] += _colsum(acc_r)
            da_ref[:, pl.ds((4 + k) * LANES, LANES)] += _colsum(acc_i)
        gm = _slabs_to_mat(g_ref, 0, Tc).astype(bf16)
        sm = _slabs_to_mat(s_ref, PAD, Tc).astype(bf16)
        du = lax.dot_general(gm, b_ref[...], _DIMS["nt"], preferred_element_type=f32) + d_ref[...] * dyb
        du_ref[...] = du.astype(bf16)
        db_ref[...] += lax.dot_general(ub16, gm, _DIMS["tn"], preferred_element_type=f32)
        dc_ref[...] += lax.dot_general(sm, dy16, _DIMS["tn"], preferred_element_type=f32)
        dd_ref[...] += _colsum(dyb * ub)

    rc = lambda j, c: (NC - 1 - c, j)
    outs, extra = _host_call(
        body, name="s5_bwd", grid=(J, NC), rider=rider, args=(u, dy, cin, bmat, cmat, drow, arow, tab),
        in_specs=[pl.BlockSpec((Tc, LANES), rc), pl.BlockSpec((Tc, LANES), rc),
                  pl.BlockSpec((None, None, 1, W), lambda j, c: (j, NC - 1 - c, 0, 0)),
                  pl.BlockSpec((None, LANES, W), lambda j, c: (j, 0, 0)),
                  pl.BlockSpec((None, W, LANES), lambda j, c: (j, 0, 0)),
                  pl.BlockSpec((1, LANES), lambda j, c: (0, j)),
                  pl.BlockSpec((None, 1, W), lambda j, c: (j, 0, 0)),
                  pl.BlockSpec((None, 16, W), lambda j, c: (j, 0, 0))],
        out_specs=[pl.BlockSpec((Tc, LANES), rc),
                   pl.BlockSpec((None, LANES, W), lambda j, c: (j, 0, 0)),
                   pl.BlockSpec((None, W, LANES), lambda j, c: (j, 0, 0)),
                   pl.BlockSpec((None, 1, W), lambda j, c: (j, 0, 0)),
                   pl.BlockSpec((1, LANES), lambda j, c: (0, j))],
        out_shape=[jax.ShapeDtypeStruct((L, D), bf16), jax.ShapeDtypeStruct((J, LANES, W), f32),
                   jax.ShapeDtypeStruct((J, W, LANES), f32), jax.ShapeDtypeStruct((J, 1, W), f32),
                   jax.ShapeDtypeStruct((1, D), f32)],
        scratch_shapes=[pltpu.VMEM((NSLAB, Tc + PAD, LANES), f32), pltpu.VMEM((NSLAB, Tc, LANES), f32),
                        pltpu.VMEM((1, W), f32)])
    return (*outs, extra)


NEG = -1e30


def _attn_logits(q_ref, k_ref, fq_ref, fk_ref, qi, ki, bq, scale):
    s = lax.dot_general(q_ref[...], k_ref[...], _DIMS["nt"], preferred_element_type=f32) * scale
    s = s + fq_ref[...] - fk_ref[...]
    rows = qi * bq + lax.broadcasted_iota(jnp.int32, s.shape, 0)
    cols = ki * bq + lax.broadcasted_iota(jnp.int32, s.shape, 1)
    return s, cols <= rows


def _attn_fwd(q, kv, fq, fk, cfg):
    L, D, NH, DH, B = cfg.L, cfg.D, cfg.NH, cfg.DH, cfg.BQ
    nq = L // B
    scale = DH ** -0.5

    def body(q_ref, k_ref, v_ref, fq_ref, fk_ref, o_ref, lse_ref, m_ref, l_ref, acc_ref):
        qi, ki = pl.program_id(1), pl.program_id(2)

        @pl.when(ki == 0)
        def _():
            m_ref[...] = jnp.full_like(m_ref, NEG)
            l_ref[...] = jnp.zeros_like(l_ref)
            acc_ref[...] = jnp.zeros_like(acc_ref)

        @pl.when(ki <= qi)
        def _():
            s, mask = _attn_logits(q_ref, k_ref, fq_ref, fk_ref, qi, ki, B, scale)
            s = jnp.where(mask, s, NEG)
            m_prev = m_ref[...]
            m_new = jnp.maximum(m_prev, jnp.max(s, axis=1, keepdims=True))
            alpha = jnp.exp(m_prev - m_new)
            p = jnp.exp(s - m_new)
            l_ref[...] = alpha * l_ref[...] + jnp.sum(p, axis=1, keepdims=True)
            acc_ref[...] = alpha * acc_ref[...] + jnp.dot(p.astype(bf16), v_ref[...], preferred_element_type=f32)
            m_ref[...] = m_new

        @pl.when(ki == qi)
        def _():
            o_ref[...] = (acc_ref[...] / l_ref[...]).astype(o_ref.dtype)
            lse_ref[...] = m_ref[...] + jnp.log(l_ref[...])

    kmap = lambda h, qi, ki: (jnp.minimum(ki, qi), h)
    vmap_ = lambda h, qi, ki: (jnp.minimum(ki, qi), NH + h)
    return pl.pallas_call(
        body, name="attn_fwd", grid=(NH, nq, nq),
        in_specs=[pl.BlockSpec((B, DH), lambda h, qi, ki: (qi, h)),
                  pl.BlockSpec((B, DH), kmap), pl.BlockSpec((B, DH), vmap_),
                  pl.BlockSpec((None, B, 1), lambda h, qi, ki: (h, qi, 0)),
                  pl.BlockSpec((None, 1, B), lambda h, qi, ki: (h, 0, jnp.minimum(ki, qi)))],
        out_specs=[pl.BlockSpec((B, DH), lambda h, qi, ki: (qi, h)),
                   pl.BlockSpec((None, B, 1), lambda h, qi, ki: (h, qi, 0))],
        out_shape=[jax.ShapeDtypeStruct((L, D), bf16), jax.ShapeDtypeStruct((NH, L, 1), f32)],
        scratch_shapes=[pltpu.VMEM((B, 1), f32), pltpu.VMEM((B, 1), f32), pltpu.VMEM((B, DH), f32)],
        compiler_params=_cp(("parallel", "parallel", "arbitrary")),
    )(q, kv, kv, fq, fk)


def _attn_bwd_dq(q, kv, do, o, lse, fq, fk, cfg):
    L, D, NH, DH, B = cfg.L, cfg.D, cfg.NH, cfg.DH, cfg.BQ
    nq = L // B
    scale = DH ** -0.5

    def body(q_ref, k_ref, v_ref, do_ref, o_ref, lse_ref, fq_ref, fk_ref, dq_ref, dfq_ref, acc_ref, df_ref, dl_ref):
        qi, ki = pl.program_id(1), pl.program_id(2)

        @pl.when(ki == 0)
        def _():
            dl_ref[...] = jnp.sum(do_ref[...].astype(f32) * o_ref[...].astype(f32), axis=1, keepdims=True)
            acc_ref[...] = jnp.zeros_like(acc_ref)
            df_ref[...] = jnp.zeros_like(df_ref)

        @pl.when(ki <= qi)
        def _():
            s, mask = _attn_logits(q_ref, k_ref, fq_ref, fk_ref, qi, ki, B, scale)
            p = jnp.where(mask, jnp.exp(s - lse_ref[...]), 0.0)
            dp = lax.dot_general(do_ref[...], v_ref[...], _DIMS["nt"], preferred_element_type=f32)
            ds = p * (dp - dl_ref[...])
            df_ref[...] += jnp.sum(ds, axis=1, keepdims=True)
            acc_ref[...] += jnp.dot(ds.astype(bf16), k_ref[...], preferred_element_type=f32)

        @pl.when(ki == qi)
        def _():
            dq_ref[...] = (acc_ref[...] * scale).astype(dq_ref.dtype)
            dfq_ref[...] = df_ref[...]

    qmap = lambda h, qi, ki: (qi, h)
    return pl.pallas_call(
        body, name="attn_bwd_dq", grid=(NH, nq, nq),
        in_specs=[pl.BlockSpec((B, DH), qmap),
                  pl.BlockSpec((B, DH), lambda h, qi, ki: (jnp.minimum(ki, qi), h)),
                  pl.BlockSpec((B, DH), lambda h, qi, ki: (jnp.minimum(ki, qi), NH + h)),
                  pl.BlockSpec((B, DH), qmap), pl.BlockSpec((B, DH), qmap),
                  pl.BlockSpec((None, B, 1), lambda h, qi, ki: (h, qi, 0)),
                  pl.BlockSpec((None, B, 1), lambda h, qi, ki: (h, qi, 0)),
                  pl.BlockSpec((None, 1, B), lambda h, qi, ki: (h, 0, jnp.minimum(ki, qi)))],
        out_specs=[pl.BlockSpec((B, DH), qmap), pl.BlockSpec((None, B, 1), lambda h, qi, ki: (h, qi, 0))],
        out_shape=[jax.ShapeDtypeStruct((L, D), bf16), jax.ShapeDtypeStruct((NH, L, 1), f32)],
        scratch_shapes=[pltpu.VMEM((B, DH), f32), pltpu.VMEM((B, 1), f32), pltpu.VMEM((B, 1), f32)],
        compiler_params=_cp(("parallel", "parallel", "arbitrary")),
    )(q, kv, kv, do, o, lse, fq, fk)


def _attn_bwd_dkv(q, kv, do, o, lse, fq, fk, cfg):
    L, D, NH, DH, B = cfg.L, cfg.D, cfg.NH, cfg.DH, cfg.BQ
    nq = L // B
    scale = DH ** -0.5

    def body(q_ref, k_ref, v_ref, do_ref, o_ref, lse_ref, fq_ref, fk_ref, dk_ref, dv_ref, dfk_ref, dka_ref, dva_ref, dfa_ref):
        ki, qi = pl.program_id(1), pl.program_id(2)

        @pl.when(qi == 0)
        def _():
            dka_ref[...] = jnp.zeros_like(dka_ref)
            dva_ref[...] = jnp.zeros_like(dva_ref)
            dfa_ref[...] = jnp.zeros_like(dfa_ref)

        @pl.when(qi >= ki)
        def _():
            do = do_ref[...]
            delta = jnp.sum(do.astype(f32) * o_ref[...].astype(f32), axis=1, keepdims=True)
            s, mask = _attn_logits(q_ref, k_ref, fq_ref, fk_ref, qi, ki, B, scale)
            p = jnp.where(mask, jnp.exp(s - lse_ref[...]), 0.0)
            dva_ref[...] += lax.dot_general(p.astype(bf16), do, _DIMS["tn"], preferred_element_type=f32)
            dp = lax.dot_general(do, v_ref[...], _DIMS["nt"], preferred_element_type=f32)
            ds = p * (dp - delta)
            dka_ref[...] += lax.dot_general(ds.astype(bf16), q_ref[...], _DIMS["tn"], preferred_element_type=f32)
            dfa_ref[...] -= jnp.sum(ds, axis=0, keepdims=True)

        @pl.when(qi == nq - 1)
        def _():
            dk_ref[...] = (dka_ref[...] * scale).astype(dk_ref.dtype)
            dv_ref[...] = dva_ref[...].astype(dv_ref.dtype)
            dfk_ref[...] = dfa_ref[...]

    qmap = lambda h, ki, qi: (jnp.maximum(qi, ki), h)
    fqmap = lambda h, ki, qi: (h, jnp.maximum(qi, ki), 0)
    return pl.pallas_call(
        body, name="attn_bwd_dkv", grid=(NH, nq, nq),
        in_specs=[pl.BlockSpec((B, DH), qmap),
                  pl.BlockSpec((B, DH), lambda h, ki, qi: (ki, h)),
                  pl.BlockSpec((B, DH), lambda h, ki, qi: (ki, NH + h)),
                  pl.BlockSpec((B, DH), qmap), pl.BlockSpec((B, DH), qmap),
                  pl.BlockSpec((None, B, 1), fqmap), pl.BlockSpec((None, B, 1), fqmap),
                  pl.BlockSpec((None, 1, B), lambda h, ki, qi: (h, 0, ki))],
        out_specs=[pl.BlockSpec((B, DH), lambda h, ki, qi: (ki, h)), pl.BlockSpec((B, DH), lambda h, ki, qi: (ki, h)),
                   pl.BlockSpec((None, 1, B), lambda h, ki, qi: (h, 0, ki))],
        out_shape=[jax.ShapeDtypeStruct((L, D), bf16), jax.ShapeDtypeStruct((L, D), bf16),
                   jax.ShapeDtypeStruct((NH, 1, L), f32)],
        scratch_shapes=[pltpu.VMEM((B, DH), f32), pltpu.VMEM((B, DH), f32), pltpu.VMEM((1, B), f32)],
        compiler_params=_cp(("parallel", "parallel", "arbitrary")),
    )(q, kv, kv, do, o, lse, fq, fk)


def _tri_tables(nq, by_key):
    pairs = [(qi, ki) for ki in range(nq) for qi in range(ki, nq)] if by_key else \
            [(qi, ki) for qi in range(nq) for ki in range(qi + 1)]
    return jnp.array([p[0] for p in pairs], jnp.int32), jnp.array([p[1] for p in pairs], jnp.int32)


def _tri_call(body, name, cfg, by_key, in_specs, out_specs, out_shape, scratch_shapes, args, rider=None):
    nq = cfg.L // cfg.BQ
    outs, extra = _host_call(body, name=name, grid=(cfg.NH, nq * (nq + 1) // 2), in_specs=in_specs, out_specs=out_specs,
                             out_shape=out_shape, scratch_shapes=scratch_shapes, args=args,
                             prefetch=_tri_tables(nq, by_key), rider=rider)
    return (*outs, extra)


def _ta_fwd(q, kv, fk, cfg, rider=None):
    L, D, NH, DH, B = cfg.L, cfg.D, cfg.NH, cfg.DH, cfg.BQ
    scale = DH ** -0.5

    def body(qt_ref, kt_ref, q_ref, k_ref, v_ref, fk_ref, o_ref, lse_ref, m_ref, acc_ref, a_ref, s_ref, p_ref):
        pid = pl.program_id(1)
        qi, ki = qt_ref[pid], kt_ref[pid]

        @pl.when(ki == 0)
        def _():
            m_ref[...] = jnp.full_like(m_ref, NEG)
            acc_ref[...] = jnp.zeros_like(acc_ref)

        def compute(masked):
            s_ref[...] = lax.dot_general(q_ref[...], k_ref[...], _DIMS["nt"], preferred_element_type=f32)
            fkv = fk_ref[...]

            def strip(rows, row0, c):
                t = s_ref[rows, :] * scale - fkv
                if masked:
                    t = jnp.where(_fa_mask(row0, t.shape), t, NEG)
                m_prev = m_ref[rows, :]
                m_new = jnp.maximum(m_prev, jnp.max(t, axis=1, keepdims=True))
                m_ref[rows, :] = m_new
                a_ref[rows, :] = jnp.exp(m_prev - m_new)
                p_ref[rows, :] = jnp.exp(t - m_new).astype(bf16)
                return c
            _fa_strips(B, strip, 0)
            v1 = jnp.concatenate([v_ref[...], jnp.ones((B, DH), bf16)], axis=1)
            acc_ref[...] = a_ref[...] * acc_ref[...] + jnp.dot(p_ref[...], v1, preferred_element_type=f32)

        @pl.when(ki < qi)
        def _():
            compute(False)

        @pl.when(ki == qi)
        def _():
            compute(True)
            l = acc_ref[:, DH:]
            o_ref[...] = (acc_ref[:, :DH] / l).astype(o_ref.dtype)
            lse_ref[...] = m_ref[...] + jnp.log(l[:, :1])

    col = pltpu.VMEM((B, 1), f32)
    return _tri_call(
        body, "attn_fwd", cfg, False,
        [pl.BlockSpec((B, DH), lambda h, p, qt, kt: (qt[p], h)),
         pl.BlockSpec((B, DH), lambda h, p, qt, kt: (kt[p], h)),
         pl.BlockSpec((B, DH), lambda h, p, qt, kt: (kt[p], NH + h)),
         pl.BlockSpec((None, 1, B), lambda h, p, qt, kt: (h, 0, kt[p]))],
        [pl.BlockSpec((B, DH), lambda h, p, qt, kt: (qt[p], h)),
         pl.BlockSpec((None, B, 1), lambda h, p, qt, kt: (h, qt[p], 0))],
        [jax.ShapeDtypeStruct((L, D), bf16), jax.ShapeDtypeStruct((NH, L, 1), f32)],
        [col, pltpu.VMEM((B, 2 * DH), f32), col, pltpu.VMEM((B, B), f32), pltpu.VMEM((B, B), bf16)],
        (q, kv, kv, fk), rider)


def _ta_bwd_dq(q, kv, do, o, lse, fk, cfg, rider=None):
    L, D, NH, DH, B = cfg.L, cfg.D, cfg.NH, cfg.DH, cfg.BQ
    scale = DH ** -0.5

    def body(qt_ref, kt_ref, q_ref, k_ref, v_ref, do_ref, o_ref, lse_ref, fk_ref, dq_ref, dfq_ref, dl_ref,
             acc_ref, s_ref, dp_ref, ds_ref):
        pid = pl.program_id(1)
        qi, ki = qt_ref[pid], kt_ref[pid]

        @pl.when(ki == 0)
        def _():
            dl_ref[...] = jnp.sum(do_ref[...].astype(f32) * o_ref[...].astype(f32), axis=1, keepdims=True)
            acc_ref[...] = jnp.zeros_like(acc_ref)

        def compute(masked):
            s_ref[...] = lax.dot_general(q_ref[...], k_ref[...], _DIMS["nt"], preferred_element_type=f32)
            dp_ref[...] = lax.dot_general(do_ref[...], v_ref[...], _DIMS["nt"], preferred_element_type=f32)
            fkv = fk_ref[...]

            def strip(rows, row0, c):
                p = jnp.exp(s_ref[rows, :] * scale - fkv - lse_ref[rows, :])
                if masked:
                    p = jnp.where(_fa_mask(row0, p.shape), p, 0.0)
                ds_ref[rows, :] = (p * (dp_ref[rows, :] - dl_ref[rows, :])).astype(bf16)
                return c
            _fa_strips(B, strip, 0)
            k1 = jnp.concatenate([k_ref[...], jnp.ones((B, DH), bf16)], axis=1)
            acc_ref[...] += jnp.dot(ds_ref[...], k1, preferred_element_type=f32)

        @pl.when(ki < qi)
        def _():
            compute(False)

        @pl.when(ki == qi)
        def _():
            compute(True)
            dq_ref[...] = (acc_ref[:, :DH] * scale).astype(dq_ref.dtype)
            dfq_ref[...] = acc_ref[:, DH:DH + 1]

    qmap = lambda h, p, qt, kt: (qt[p], h)
    cmap = lambda h, p, qt, kt: (h, qt[p], 0)
    return _tri_call(
        body, "attn_bwd_dq", cfg, False,
        [pl.BlockSpec((B, DH), qmap),
         pl.BlockSpec((B, DH), lambda h, p, qt, kt: (kt[p], h)),
         pl.BlockSpec((B, DH), lambda h, p, qt, kt: (kt[p], NH + h)),
         pl.BlockSpec((B, DH), qmap), pl.BlockSpec((B, DH), qmap),
         pl.BlockSpec((None, B, 1), cmap),
         pl.BlockSpec((None, 1, B), lambda h, p, qt, kt: (h, 0, kt[p]))],
        [pl.BlockSpec((B, DH), qmap), pl.BlockSpec((None, B, 1), cmap), pl.BlockSpec((None, B, 1), cmap)],
        [jax.ShapeDtypeStruct((L, D), bf16), jax.ShapeDtypeStruct((NH, L, 1), f32), jax.ShapeDtypeStruct((NH, L, 1), f32)],
        [pltpu.VMEM((B, 2 * DH), f32), pltpu.VMEM((B, B), f32), pltpu.VMEM((B, B), f32), pltpu.VMEM((B, B), bf16)],
        (q, kv, kv, do, o, lse, fk), rider)


def _ta_bwd_dkv(q, kv, do, delta, lse, fk, cfg):
    L, D, NH, DH, B = cfg.L, cfg.D, cfg.NH, cfg.DH, cfg.BQ
    nq = L // B
    scale = DH ** -0.5

    def body(qt_ref, kt_ref, q_ref, k_ref, v_ref, do_ref, dl_ref, lse_ref, fk_ref, dk_ref, dv_ref, dfk_ref,
             dka_ref, dva_ref, s_ref, dp_ref, p_ref, ds_ref):
        pid = pl.program_id(1)
        qi, ki = qt_ref[pid], kt_ref[pid]

        @pl.when(qi == ki)
        def _():
            dka_ref[...] = jnp.zeros_like(dka_ref)
            dva_ref[...] = jnp.zeros_like(dva_ref)

        def compute(masked):
            s_ref[...] = lax.dot_general(q_ref[...], k_ref[...], _DIMS["nt"], preferred_element_type=f32)
            dp_ref[...] = lax.dot_general(do_ref[...], v_ref[...], _DIMS["nt"], preferred_element_type=f32)
            fkv = fk_ref[...]

            def strip(rows, row0, c):
                p = jnp.exp(s_ref[rows, :] * scale - fkv - lse_ref[rows, :])
                if masked:
                    p = jnp.where(_fa_mask(row0, p.shape), p, 0.0)
                p_ref[rows, :] = p.astype(bf16)
                ds_ref[rows, :] = (p * (dp_ref[rows, :] - dl_ref[rows, :])).astype(bf16)
                return c
            _fa_strips(B, strip, 0)
            q1 = jnp.concatenate([q_ref[...], jnp.ones((B, DH), bf16)], axis=1)
            dva_ref[...] += lax.dot_general(p_ref[...], do_ref[...], _DIMS["tn"], preferred_element_type=f32)
            dka_ref[...] += lax.dot_general(ds_ref[...], q1, _DIMS["tn"], preferred_element_type=f32)

        @pl.when(qi == ki)
        def _():
            compute(True)

        @pl.when(qi > ki)
        def _():
            compute(False)

        @pl.when(qi == nq - 1)
        def _():
            dk_ref[...] = (dka_ref[:, :DH] * scale).astype(dk_ref.dtype)
            dv_ref[...] = dva_ref[...].astype(dv_ref.dtype)
            dfk_ref[...] = -dka_ref[:, DH:DH + 1]

    qmap = lambda h, p, qt, kt: (qt[p], h)
    cmap = lambda h, p, qt, kt: (h, qt[p], 0)
    kmap = lambda h, p, qt, kt: (kt[p], h)
    return _tri_call(
        body, "attn_bwd_dkv", cfg, True,
        [pl.BlockSpec((B, DH), qmap), pl.BlockSpec((B, DH), kmap),
         pl.BlockSpec((B, DH), lambda h, p, qt, kt: (kt[p], NH + h)),
         pl.BlockSpec((B, DH), qmap), pl.BlockSpec((None, B, 1), cmap), pl.BlockSpec((None, B, 1), cmap),
         pl.BlockSpec((None, 1, B), lambda h, p, qt, kt: (h, 0, kt[p]))],
        [pl.BlockSpec((B, DH), kmap), pl.BlockSpec((B, DH), kmap),
         pl.BlockSpec((None, B, 1), lambda h, p, qt, kt: (h, kt[p], 0))],
        [jax.ShapeDtypeStruct((L, D), bf16), jax.ShapeDtypeStruct((L, D), bf16), jax.ShapeDtypeStruct((NH, L, 1), f32)],
        [pltpu.VMEM((B, 2 * DH), f32), pltpu.VMEM((B, DH), f32), pltpu.VMEM((B, B), f32), pltpu.VMEM((B, B), f32),
         pltpu.VMEM((B, B), bf16), pltpu.VMEM((B, B), bf16)],
        (q, kv, kv, do, delta, lse, fk))


STRIP = 32


def _fa_strips(nrows, fn, init):
    return lax.fori_loop(0, nrows // STRIP, lambda r, c: fn(pl.ds(pl.multiple_of(r * STRIP, STRIP), STRIP), r * STRIP, c),
                         init, unroll=True)


def _fa_mask(row0, shape):
    rows = row0 + lax.broadcasted_iota(jnp.int32, shape, 0)
    cols = lax.broadcasted_iota(jnp.int32, shape, 1)
    return cols <= rows


def _fa_fwd(q, kv, fk, cfg):
    L, D, NH, DH, B = cfg.L, cfg.D, cfg.NH, cfg.DH, cfg.BQ
    nq = L // B
    scale = DH ** -0.5

    def body(q_ref, k_ref, v_ref, fk_ref, o_ref, lse_ref, m_ref, l_ref, acc_ref, a_ref, s_ref, p_ref):
        qi, ki = pl.program_id(1), pl.program_id(2)

        @pl.when(ki == 0)
        def _():
            m_ref[...] = jnp.full_like(m_ref, NEG)
            l_ref[...] = jnp.zeros_like(l_ref)
            acc_ref[...] = jnp.zeros_like(acc_ref)

        def compute(masked):
            s_ref[...] = lax.dot_general(q_ref[...], k_ref[...], _DIMS["nt"], preferred_element_type=f32)
            fkv = fk_ref[...]

            def strip(rows, row0, c):
                t = s_ref[rows, :] * scale - fkv
                if masked:
                    t = jnp.where(_fa_mask(row0, t.shape), t, NEG)
                m_prev = m_ref[rows, :]
                m_new = jnp.maximum(m_prev, jnp.max(t, axis=1, keepdims=True))
                p = jnp.exp(t - m_new)
                alpha = jnp.exp(m_prev - m_new)
                l_ref[rows, :] = alpha * l_ref[rows, :] + jnp.sum(p, axis=1, keepdims=True)
                m_ref[rows, :] = m_new
                a_ref[rows, :] = alpha
                p_ref[rows, :] = p.astype(bf16)
                return c
            _fa_strips(B, strip, 0)
            acc_ref[...] = a_ref[...] * acc_ref[...] + jnp.dot(p_ref[...], v_ref[...], preferred_element_type=f32)

        @pl.when(ki < qi)
        def _():
            compute(False)

        @pl.when(ki == qi)
        def _():
            compute(True)
            o_ref[...] = (acc_ref[...] / l_ref[...]).astype(o_ref.dtype)
            lse_ref[...] = m_ref[...] + jnp.log(l_ref[...])

    col = pltpu.VMEM((B, 1), f32)
    return pl.pallas_call(
        body, name="attn_fwd", grid=(NH, nq, nq),
        in_specs=[pl.BlockSpec((B, DH), lambda h, qi, ki: (qi, h)),
                  pl.BlockSpec((B, DH), lambda h, qi, ki: (jnp.minimum(ki, qi), h)),
                  pl.BlockSpec((B, DH), lambda h, qi, ki: (jnp.minimum(ki, qi), NH + h)),
                  pl.BlockSpec((None, 1, B), lambda h, qi, ki: (h, 0, jnp.minimum(ki, qi)))],
        out_specs=[pl.BlockSpec((B, DH), lambda h, qi, ki: (qi, h)),
                   pl.BlockSpec((None, B, 1), lambda h, qi, ki: (h, qi, 0))],
        out_shape=[jax.ShapeDtypeStruct((L, D), bf16), jax.ShapeDtypeStruct((NH, L, 1), f32)],
        scratch_shapes=[col, col, pltpu.VMEM((B, DH), f32), col, pltpu.VMEM((B, B), f32), pltpu.VMEM((B, B), bf16)],
        compiler_params=_cp(("parallel", "parallel", "arbitrary")),
    )(q, kv, kv, fk)


def _fa_bwd_dq(q, kv, do, o, lse, fk, cfg):
    L, D, NH, DH, B = cfg.L, cfg.D, cfg.NH, cfg.DH, cfg.BQ
    nq = L // B
    scale = DH ** -0.5

    def body(q_ref, k_ref, v_ref, do_ref, o_ref, lse_ref, fk_ref, dq_ref, dfq_ref, dl_ref, acc_ref, df_ref, s_ref, dp_ref, ds_ref):
        qi, ki = pl.program_id(1), pl.program_id(2)

        @pl.when(ki == 0)
        def _():
            dl_ref[...] = jnp.sum(do_ref[...].astype(f32) * o_ref[...].astype(f32), axis=1, keepdims=True)
            acc_ref[...] = jnp.zeros_like(acc_ref)
            df_ref[...] = jnp.zeros_like(df_ref)

        def compute(masked):
            s_ref[...] = lax.dot_general(q_ref[...], k_ref[...], _DIMS["nt"], preferred_element_type=f32)
            dp_ref[...] = lax.dot_general(do_ref[...], v_ref[...], _DIMS["nt"], preferred_element_type=f32)
            fkv = fk_ref[...]

            def strip(rows, row0, c):
                p = jnp.exp(s_ref[rows, :] * scale - fkv - lse_ref[rows, :])
                if masked:
                    p = jnp.where(_fa_mask(row0, p.shape), p, 0.0)
                ds = p * (dp_ref[rows, :] - dl_ref[rows, :])
                df_ref[rows, :] += jnp.sum(ds, axis=1, keepdims=True)
                ds_ref[rows, :] = ds.astype(bf16)
                return c
            _fa_strips(B, strip, 0)
            acc_ref[...] += jnp.dot(ds_ref[...], k_ref[...], preferred_element_type=f32)

        @pl.when(ki < qi)
        def _():
            compute(False)

        @pl.when(ki == qi)
        def _():
            compute(True)
            dq_ref[...] = (acc_ref[...] * scale).astype(dq_ref.dtype)
            dfq_ref[...] = df_ref[...]

    qmap = lambda h, qi, ki: (qi, h)
    cmap = lambda h, qi, ki: (h, qi, 0)
    return pl.pallas_call(
        body, name="attn_bwd_dq", grid=(NH, nq, nq),
        in_specs=[pl.BlockSpec((B, DH), qmap),
                  pl.BlockSpec((B, DH), lambda h, qi, ki: (jnp.minimum(ki, qi), h)),
                  pl.BlockSpec((B, DH), lambda h, qi, ki: (jnp.minimum(ki, qi), NH + h)),
                  pl.BlockSpec((B, DH), qmap), pl.BlockSpec((B, DH), qmap),
                  pl.BlockSpec((None, B, 1), cmap),
                  pl.BlockSpec((None, 1, B), lambda h, qi, ki: (h, 0, jnp.minimum(ki, qi)))],
        out_specs=[pl.BlockSpec((B, DH), qmap), pl.BlockSpec((None, B, 1), cmap), pl.BlockSpec((None, B, 1), cmap)],
        out_shape=[jax.ShapeDtypeStruct((L, D), bf16), jax.ShapeDtypeStruct((NH, L, 1), f32),
                   jax.ShapeDtypeStruct((NH, L, 1), f32)],
        scratch_shapes=[pltpu.VMEM((B, DH), f32), pltpu.VMEM((B, 1), f32), pltpu.VMEM((B, B), f32),
                        pltpu.VMEM((B, B), f32), pltpu.VMEM((B, B), bf16)],
        compiler_params=_cp(("parallel", "parallel", "arbitrary")),
    )(q, kv, kv, do, o, lse, fk)


def _fa_bwd_dkv(q, kv, do, delta, lse, fk, cfg):
    L, D, NH, DH, B = cfg.L, cfg.D, cfg.NH, cfg.DH, cfg.BQ
    nq = L // B
    scale = DH ** -0.5

    def body(q_ref, k_ref, v_ref, do_ref, dl_ref, lse_ref, fk_ref, dk_ref, dv_ref, dfk_ref,
             dka_ref, dva_ref, dfa_ref, s_ref, dp_ref, p_ref, ds_ref):
        ki, qi = pl.program_id(1), pl.program_id(2)

        @pl.when(qi == 0)
        def _():
            dka_ref[...] = jnp.zeros_like(dka_ref)
            dva_ref[...] = jnp.zeros_like(dva_ref)
            dfa_ref[...] = jnp.zeros_like(dfa_ref)

        def compute(masked):
            s_ref[...] = lax.dot_general(q_ref[...], k_ref[...], _DIMS["nt"], preferred_element_type=f32)
            dp_ref[...] = lax.dot_general(do_ref[...], v_ref[...], _DIMS["nt"], preferred_element_type=f32)
            fkv = fk_ref[...]

            def strip(rows, row0, cs):
                p = jnp.exp(s_ref[rows, :] * scale - fkv - lse_ref[rows, :])
                if masked:
                    p = jnp.where(_fa_mask(row0, p.shape), p, 0.0)
                ds = p * (dp_ref[rows, :] - dl_ref[rows, :])
                p_ref[rows, :] = p.astype(bf16)
                ds_ref[rows, :] = ds.astype(bf16)
                return cs + ds
            cs = _fa_strips(B, strip, jnp.zeros((STRIP, B), f32))
            dva_ref[...] += lax.dot_general(p_ref[...], do_ref[...], _DIMS["tn"], preferred_element_type=f32)
            dka_ref[...] += lax.dot_general(ds_ref[...], q_ref[...], _DIMS["tn"], preferred_element_type=f32)
            dfa_ref[...] -= jnp.sum(cs, axis=0, keepdims=True)

        @pl.when(qi == ki)
        def _():
            compute(True)

        @pl.when(qi > ki)
        def _():
            compute(False)

        @pl.when(qi == nq - 1)
        def _():
            dk_ref[...] = (dka_ref[...] * scale).astype(dk_ref.dtype)
            dv_ref[...] = dva_ref[...].astype(dv_ref.dtype)
            dfk_ref[...] = dfa_ref[...]

    qmap = lambda h, ki, qi: (jnp.maximum(qi, ki), h)
    cmap = lambda h, ki, qi: (h, jnp.maximum(qi, ki), 0)
    return pl.pallas_call(
        body, name="attn_bwd_dkv", grid=(NH, nq, nq),
        in_specs=[pl.BlockSpec((B, DH), qmap),
                  pl.BlockSpec((B, DH), lambda h, ki, qi: (ki, h)),
                  pl.BlockSpec((B, DH), lambda h, ki, qi: (ki, NH + h)),
                  pl.BlockSpec((B, DH), qmap),
                  pl.BlockSpec((None, B, 1), cmap), pl.BlockSpec((None, B, 1), cmap),
                  pl.BlockSpec((None, 1, B), lambda h, ki, qi: (h, 0, ki))],
        out_specs=[pl.BlockSpec((B, DH), lambda h, ki, qi: (ki, h)), pl.BlockSpec((B, DH), lambda h, ki, qi: (ki, h)),
                   pl.BlockSpec((None, 1, B), lambda h, ki, qi: (h, 0, ki))],
        out_shape=[jax.ShapeDtypeStruct((L, D), bf16), jax.ShapeDtypeStruct((L, D), bf16),
                   jax.ShapeDtypeStruct((NH, 1, L), f32)],
        scratch_shapes=[pltpu.VMEM((B, DH), f32), pltpu.VMEM((B, DH), f32), pltpu.VMEM((1, B), f32),
                        pltpu.VMEM((B, B), f32), pltpu.VMEM((B, B), f32), pltpu.VMEM((B, B), bf16), pltpu.VMEM((B, B), bf16)],
        compiler_params=_cp(("parallel", "parallel", "arbitrary")),
    )(q, kv, kv, do, delta, lse, fk)


def _fox_logits(q, k, fqv, fkv, scale, masked):
    s = lax.dot_general(q, k, _DIMS["nt"], preferred_element_type=f32) * scale + fqv - fkv
    if masked:
        rows = lax.broadcasted_iota(jnp.int32, s.shape, 0)
        cols = lax.broadcasted_iota(jnp.int32, s.shape, 1)
        return s, cols <= rows
    return s, None


def _fox_fwd(q, kv, fq, fk, cfg):
    L, D, NH, DH, B = cfg.L, cfg.D, cfg.NH, cfg.DH, cfg.BQ
    nq = L // B
    scale = DH ** -0.5

    def body(q_ref, k_ref, v_ref, fq_ref, fk_ref, o_ref, lse_ref):
        qi = pl.program_id(1)
        qv, fqv = q_ref[...], fq_ref[...]

        def chunk(kj, carry, masked):
            m, l, acc = carry
            rows = pl.ds(pl.multiple_of(kj * B, B), B)
            s, mask = _fox_logits(qv, k_ref[rows, :], fqv, fk_ref[kj], scale, masked)
            if masked:
                s = jnp.where(mask, s, NEG)
            m_new = jnp.maximum(m, jnp.max(s, axis=1, keepdims=True))
            alpha = jnp.exp(m - m_new)
            p = jnp.exp(s - m_new)
            l = alpha * l + jnp.sum(p, axis=1, keepdims=True)
            acc = alpha * acc + jnp.dot(p.astype(bf16), v_ref[rows, :], preferred_element_type=f32)
            return m_new, l, acc

        init = (jnp.full((B, 1), NEG, f32), jnp.zeros((B, 1), f32), jnp.zeros((B, DH), f32))
        carry = lax.fori_loop(0, qi, lambda kj, c: chunk(kj, c, False), init)
        m, l, acc = chunk(qi, carry, True)
        o_ref[...] = (acc / l).astype(o_ref.dtype)
        lse_ref[...] = m + jnp.log(l)

    return pl.pallas_call(
        body, name="attn_fwd", grid=(NH, nq),
        in_specs=[pl.BlockSpec((B, DH), lambda h, qi: (qi, h)),
                  pl.BlockSpec((L, DH), lambda h, qi: (0, h)), pl.BlockSpec((L, DH), lambda h, qi: (0, NH + h)),
                  pl.BlockSpec((None, B, 1), lambda h, qi: (h, qi, 0)),
                  pl.BlockSpec((None, nq, 1, B), lambda h, qi: (h, 0, 0, 0))],
        out_specs=[pl.BlockSpec((B, DH), lambda h, qi: (qi, h)), pl.BlockSpec((None, B, 1), lambda h, qi: (h, qi, 0))],
        out_shape=[jax.ShapeDtypeStruct((L, D), bf16), jax.ShapeDtypeStruct((NH, L, 1), f32)],
        compiler_params=_cp(("parallel", "arbitrary")),
    )(q, kv, kv, fq, fk)


def _fox_bwd_dq(q, kv, do, o, lse, fq, fk, cfg):
    L, D, NH, DH, B = cfg.L, cfg.D, cfg.NH, cfg.DH, cfg.BQ
    nq = L // B
    scale = DH ** -0.5

    def body(q_ref, k_ref, v_ref, do_ref, o_ref, lse_ref, fq_ref, fk_ref, dq_ref, dfq_ref, dl_ref):
        qi = pl.program_id(1)
        qv, fqv, dov, lsev = q_ref[...], fq_ref[...], do_ref[...], lse_ref[...]
        delta = jnp.sum(dov.astype(f32) * o_ref[...].astype(f32), axis=1, keepdims=True)

        def chunk(kj, carry, masked):
            acc, df = carry
            rows = pl.ds(pl.multiple_of(kj * B, B), B)
            kv_ = k_ref[rows, :]
            s, mask = _fox_logits(qv, kv_, fqv, fk_ref[kj], scale, masked)
            p = jnp.exp(s - lsev)
            if masked:
                p = jnp.where(mask, p, 0.0)
            dp = lax.dot_general(dov, v_ref[rows, :], _DIMS["nt"], preferred_element_type=f32)
            ds = p * (dp - delta)
            return acc + jnp.dot(ds.astype(bf16), kv_, preferred_element_type=f32), df + jnp.sum(ds, axis=1, keepdims=True)

        carry = lax.fori_loop(0, qi, lambda kj, c: chunk(kj, c, False), (jnp.zeros((B, DH), f32), jnp.zeros((B, 1), f32)))
        acc, df = chunk(qi, carry, True)
        dq_ref[...] = (acc * scale).astype(dq_ref.dtype)
        dfq_ref[...] = df
        dl_ref[...] = delta

    qmap = lambda h, qi: (qi, h)
    cmap = lambda h, qi: (h, qi, 0)
    return pl.pallas_call(
        body, name="attn_bwd_dq", grid=(NH, nq),
        in_specs=[pl.BlockSpec((B, DH), qmap),
                  pl.BlockSpec((L, DH), lambda h, qi: (0, h)), pl.BlockSpec((L, DH), lambda h, qi: (0, NH + h)),
                  pl.BlockSpec((B, DH), qmap), pl.BlockSpec((B, DH), qmap),
                  pl.BlockSpec((None, B, 1), cmap), pl.BlockSpec((None, B, 1), cmap),
                  pl.BlockSpec((None, nq, 1, B), lambda h, qi: (h, 0, 0, 0))],
        out_specs=[pl.BlockSpec((B, DH), qmap), pl.BlockSpec((None, B, 1), cmap), pl.BlockSpec((None, B, 1), cmap)],
        out_shape=[jax.ShapeDtypeStruct((L, D), bf16), jax.ShapeDtypeStruct((NH, L, 1), f32),
                   jax.ShapeDtypeStruct((NH, L, 1), f32)],
        compiler_params=_cp(("parallel", "arbitrary")),
    )(q, kv, kv, do, o, lse, fq, fk)


def _fox_bwd_dkv(q, kv, do, delta, lse, fq, fk, cfg):
    L, D, NH, DH, B = cfg.L, cfg.D, cfg.NH, cfg.DH, cfg.BQ
    nq = L // B
    scale = DH ** -0.5

    def body(q_ref, k_ref, v_ref, do_ref, dl_ref, lse_ref, fq_ref, fk_ref, dk_ref, dv_ref, dfk_ref):
        ki = pl.program_id(1)
        kv_, vv, fkv = k_ref[...], v_ref[...], fk_ref[...]

        def block(qj, carry, masked):
            dk, dv, df = carry
            rows = pl.ds(pl.multiple_of(qj * B, B), B)
            qv, dov = q_ref[rows, :], do_ref[rows, :]
            s, mask = _fox_logits(qv, kv_, fq_ref[rows, :], fkv, scale, masked)
            p = jnp.exp(s - lse_ref[rows, :])
            if masked:
                p = jnp.where(mask, p, 0.0)
            dv = dv + lax.dot_general(p.astype(bf16), dov, _DIMS["tn"], preferred_element_type=f32)
            dp = lax.dot_general(dov, vv, _DIMS["nt"], preferred_element_type=f32)
            ds = p * (dp - dl_ref[rows, :])
            dk = dk + lax.dot_general(ds.astype(bf16), qv, _DIMS["tn"], preferred_element_type=f32)
            return dk, dv, df - jnp.sum(ds, axis=0, keepdims=True)

        init = (jnp.zeros((B, DH), f32), jnp.zeros((B, DH), f32), jnp.zeros((1, B), f32))
        carry = block(ki, init, True)
        dk, dv, df = lax.fori_loop(ki + 1, nq, lambda qj, c: block(qj, c, False), carry)
        dk_ref[...] = (dk * scale).astype(dk_ref.dtype)
        dv_ref[...] = dv.astype(dv_ref.dtype)
        dfk_ref[...] = df

    whole = lambda h, ki: (0, h)
    col = lambda h, ki: (h, 0, 0)
    return pl.pallas_call(
        body, name="attn_bwd_dkv", grid=(NH, nq),
        in_specs=[pl.BlockSpec((L, DH), whole),
                  pl.BlockSpec((B, DH), lambda h, ki: (ki, h)), pl.BlockSpec((B, DH), lambda h, ki: (ki, NH + h)),
                  pl.BlockSpec((L, DH), whole),
                  pl.BlockSpec((None, L, 1), col), pl.BlockSpec((None, L, 1), col), pl.BlockSpec((None, L, 1), col),
                  pl.BlockSpec((None, None, 1, B), lambda h, ki: (h, ki, 0, 0))],
        out_specs=[pl.BlockSpec((B, DH), lambda h, ki: (ki, h)), pl.BlockSpec((B, DH), lambda h, ki: (ki, h)),
                   pl.BlockSpec((None, None, 1, B), lambda h, ki: (h, ki, 0, 0))],
        out_shape=[jax.ShapeDtypeStruct((L, D), bf16), jax.ShapeDtypeStruct((L, D), bf16),
                   jax.ShapeDtypeStruct((NH, nq, 1, B), f32)],
        compiler_params=_cp(("parallel", "arbitrary")),
    )(q, kv, kv, do, delta, lse, fq, fk)


FCH = 256


def _split3(x):
    hi = x.astype(bf16)
    r1 = x - hi.astype(f32)
    mid = r1.astype(bf16)
    lo = (r1 - mid.astype(f32)).astype(bf16)
    return hi, mid, lo


def _tri_sum(tri, x):
    hi, mid, lo = _split3(x)
    return (jnp.dot(tri, hi, preferred_element_type=f32) + jnp.dot(tri, mid, preferred_element_type=f32)
            + jnp.dot(tri, lo, preferred_element_type=f32))


def _fgate_fwd(z, fb, cfg):
    L = cfg.L

    def body(z_ref, fb_ref, f_ref):
        r = lax.broadcasted_iota(jnp.int32, (FCH, FCH), 0)
        c = lax.broadcasted_iota(jnp.int32, (FCH, FCH), 1)
        tri = (c <= r).astype(bf16)
        carry = jnp.zeros((1, LANES), f32)
        for ch in range(L // FCH):
            x = z_ref[pl.ds(ch * FCH, FCH), :] + fb_ref[...]
            lf = jnp.minimum(x, 0.0) - jnp.log(1.0 + jnp.exp(-jnp.abs(x)))
            f_ref[pl.ds(ch * FCH, FCH), :] = _tri_sum(tri, lf) + carry
            carry = f_ref[pl.ds(ch * FCH + FCH - 1, 1), :]

    vm = pl.BlockSpec(memory_space=pltpu.VMEM)
    return pl.pallas_call(body, name="fgate_fwd", in_specs=[vm, vm], out_specs=vm,
                          out_shape=jax.ShapeDtypeStruct((L, LANES), f32), compiler_params=_cp())(z, fb)


def _fgate_bwd(df, z, fb, cfg):
    L = cfg.L

    def body(df_ref, z_ref, fb_ref, dz_ref, db_ref):
        r = lax.broadcasted_iota(jnp.int32, (FCH, FCH), 0)
        c = lax.broadcasted_iota(jnp.int32, (FCH, FCH), 1)
        tri = (c >= r).astype(bf16)
        carry = jnp.zeros((1, LANES), f32)
        dbs = jnp.zeros((1, LANES), f32)
        for ch in range(L // FCH - 1, -1, -1):
            suf = _tri_sum(tri, df_ref[pl.ds(ch * FCH, FCH), :]) + carry
            x = z_ref[pl.ds(ch * FCH, FCH), :] + fb_ref[...]
            dz = suf * _sigmoid(-x)
            dz_ref[pl.ds(ch * FCH, FCH), :] = dz
            dbs = dbs + _colsum(dz)
            carry = carry + _colsum(df_ref[pl.ds(ch * FCH, FCH), :])
        db_ref[...] = dbs

    vm = pl.BlockSpec(memory_space=pltpu.VMEM)
    return pl.pallas_call(body, name="fgate_bwd", in_specs=[vm, vm, vm], out_specs=[vm, vm],
                          out_shape=[jax.ShapeDtypeStruct((L, LANES), f32), jax.ShapeDtypeStruct((1, LANES), f32)],
                          compiler_params=_cp())(df, z, fb)


def _adamw(w, g, m, v, name):
    R, C = w.shape
    c1 = 1.0 - ADAM_B1 ** ADAM_STEP
    c2 = 1.0 - ADAM_B2 ** ADAM_STEP

    def fn(i, ni, wv, gv, mv, vv):
        mn = ADAM_B1 * mv + (1.0 - ADAM_B1) * gv
        vn = ADAM_B2 * vv + (1.0 - ADAM_B2) * (gv * gv)
        delta = -ADAM_LR * ((mn / c1) / (jnp.sqrt(vn / c2) + ADAM_EPS) + ADAM_WD * wv)
        return [delta, mn, vn], []
    tc = C if C % LANES else _tile(C, 1024)
    return _rowwise(fn, [(w, "rc"), (g, "rc"), (m, "rc"), (v, "rc")], [f32, f32, f32], [], L=R, C=C, tl=512, tc=tc, name=name)


def _sum_lead(x, out_dtype, name):
    n, R, C = x.shape
    tl = _tile(R, 512, HALO)
    tc = C if C % LANES else _tile(C, 1024)

    def body(x_ref, o_ref):
        acc = x_ref[0].astype(f32)
        for k in range(1, n):
            acc = acc + x_ref[k].astype(f32)
        o_ref[...] = acc.astype(o_ref.dtype)

    return pl.pallas_call(
        body, name=name, grid=(R // tl, C // tc),
        in_specs=[pl.BlockSpec((n, tl, tc), lambda i, j: (0, i, j))], out_specs=pl.BlockSpec((tl, tc), lambda i, j: (i, j)),
        out_shape=jax.ShapeDtypeStruct((R, C), out_dtype), compiler_params=_cp(("parallel", "parallel")),
    )(x)


def _add2(a, b, out_dtype, name):
    R, C = a.shape

    def fn(i, ni, av, bv):
        return [av.astype(f32) + bv.astype(f32)], []
    tc = C if C % LANES else _tile(C, 1024)
    return _rowwise(fn, [(a, "rc"), (b, "rc")], [out_dtype], [], L=R, C=C, tl=512, tc=tc, name=name)[0]


ANY = pl.BlockSpec(memory_space=pl.ANY)


def _place():
    x, y, c = lax.axis_index("x"), lax.axis_index("y"), lax.axis_index("c")
    return x, y, c


def _allgather8(blocks, name):
    return _run_rider(_gather_rider(blocks), name)


def _gather_rider(blocks):
    n = len(blocks)

    def steps(ins, outs, sems):
        send_sems, recv_sems, local_sems = sems
        x, y, c = _place()
        me, sibling = (x, y, c), (x, y, 1 - c)
        chips = [(1 - x, y), (x, 1 - y), (1 - x, 1 - y)]

        def slot(a, dev):
            return outs[a].at[4 * dev[0] + 2 * dev[1] + dev[2]]

        def copy(a, k, block, to, src=None):
            return pltpu.make_async_remote_copy(
                src_ref=slot(a, block) if src is None else src, dst_ref=slot(a, block),
                send_sem=send_sems.at[a * 7 + k], recv_sem=recv_sems.at[a * 7 + k], device_id=to, device_id_type=MESH)

        def mine():
            return [pltpu.make_async_copy(ins[a], slot(a, me), local_sems.at[a]) for a in range(n)]

        def first():
            out = []
            for a in range(n):
                out.append(copy(a, 0, me, sibling, src=ins[a]))
                out += [copy(a, 1 + j, me, (*chip, c), src=ins[a]) for j, chip in enumerate(chips)]
            return out

        def passed():
            return [copy(a, 4 + j, (*chip, c), sibling) for j, chip in enumerate(chips) for a in range(n)]

        def start():
            for cp in mine() + first():
                cp.start()

        def middle():
            for j, chip in enumerate(chips):
                for a in range(n):
                    copy(a, 1 + j, (*chip, c), me).wait_recv()
                    copy(a, 4 + j, (*chip, c), sibling).start()

        def finish():
            for a in range(n):
                copy(a, 0, sibling, me).wait_recv()
            for j, chip in enumerate(chips):
                for a in range(n):
                    copy(a, 4 + j, (*chip, 1 - c), me).wait_recv()
            for cp in first() + passed():
                cp.wait_send()
            for cp in mine():
                cp.wait()
        return start, middle, finish

    return dict(ins=list(blocks), out_shapes=[jax.ShapeDtypeStruct((N_DEV,) + b.shape, b.dtype) for b in blocks],
                sems=[pltpu.SemaphoreType.DMA((7 * n,)), pltpu.SemaphoreType.DMA((7 * n,)), pltpu.SemaphoreType.DMA((n,))],
                steps=steps)


def _run_rider(rider, name):
    ni, no = len(rider["ins"]), len(rider["out_shapes"])

    def body(*refs):
        start, middle, finish = rider["steps"](refs[:ni], refs[ni:ni + no], refs[ni + no:])
        start()
        if middle is not None:
            middle()
        finish()

    outs = pl.pallas_call(body, name=name, in_specs=[ANY] * ni, out_specs=[ANY] * no, out_shape=rider["out_shapes"],
                          scratch_shapes=rider["sems"])(*rider["ins"])
    return list(outs)


def _host_call(body, *, name, grid, in_specs, out_specs, out_shape, scratch_shapes, args, prefetch=(), rider=None):
    npre, nin, nout, nscr = len(prefetch), len(in_specs), len(out_specs), len(scratch_shapes)
    r_in, r_out, r_scr = (rider["ins"], rider["out_shapes"], rider["sems"]) if rider else ([], [], [])
    nri, nro = len(r_in), len(r_out)

    def kern(*refs):
        pre, rest = refs[:npre], refs[npre:]
        cin, rin = rest[:nin], rest[nin:nin + nri]
        o0 = nin + nri
        cout, rout = rest[o0:o0 + nout], rest[o0 + nout:o0 + nout + nro]
        s0 = o0 + nout + nro
        cscr, rscr = rest[s0:s0 + nscr], rest[s0 + nscr:]
        if rider:
            ids = [pl.program_id(d) for d in range(len(grid))]
            rest_zero = functools.reduce(jnp.logical_and, [i == 0 for i in ids[1:]], True)
            start, middle, finish = rider["steps"](rin, rout, rscr)
            pl.when(jnp.logical_and(ids[0] == 0, rest_zero))(start)
            if middle is not None:
                pl.when(jnp.logical_and(ids[0] == grid[0] // 2, rest_zero))(middle)
        body(*pre, *cin, *cout, *cscr)
        if rider:
            pl.when(functools.reduce(jnp.logical_and, [i == g - 1 for i, g in zip(ids, grid)]))(finish)

    res = pl.pallas_call(
        kern, name=name,
        grid_spec=pltpu.PrefetchScalarGridSpec(num_scalar_prefetch=npre, grid=grid, in_specs=list(in_specs) + [ANY] * nri,
                                               out_specs=list(out_specs) + [ANY] * nro,
                                               scratch_shapes=list(scratch_shapes) + list(r_scr)),
        out_shape=list(out_shape) + list(r_out),
        compiler_params=_cp(("arbitrary",) * len(grid) if rider else ("parallel",) + ("arbitrary",) * (len(grid) - 1)),
    )(*prefetch, *args, *r_in)
    return list(res[:nout]), list(res[nout:])


def _sibling_send(halves, name):
    n = len(halves)

    def body(*refs):
        ins, outs = refs[:n], refs[n:2 * n]
        send_sems, recv_sems = refs[2 * n:]
        x, y, c = _place()
        sends = [pltpu.make_async_remote_copy(src_ref=ins[a], dst_ref=outs[a], send_sem=send_sems.at[a],
                                              recv_sem=recv_sems.at[a], device_id=(x, y, 1 - c), device_id_type=MESH)
                 for a in range(n)]
        for cp in sends:
            cp.start()
        for cp in sends:
            cp.wait_recv()
        for cp in sends:
            cp.wait_send()

    outs = pl.pallas_call(
        body, name=name, in_specs=[ANY] * n, out_specs=[ANY] * n,
        out_shape=[jax.ShapeDtypeStruct(h.shape, h.dtype) for h in halves],
        scratch_shapes=[pltpu.SemaphoreType.DMA((n,)), pltpu.SemaphoreType.DMA((n,))],
    )(*halves)
    return list(outs)


def _sibling_swap_halves(grads, name):
    n = len(grads)

    def body(*refs):
        ins, outs = refs[:n], refs[n:2 * n]
        send_sems, recv_sems = refs[2 * n:]
        x, y, c = _place()
        sends = [pltpu.make_async_remote_copy(src_ref=ins[a].at[:, 1 - c], dst_ref=outs[a], send_sem=send_sems.at[a],
                                              recv_sem=recv_sems.at[a], device_id=(x, y, 1 - c), device_id_type=MESH)
                 for a in range(n)]
        for cp in sends:
            cp.start()
        for cp in sends:
            cp.wait_recv()
        for cp in sends:
            cp.wait_send()

    outs = pl.pallas_call(
        body, name=name, in_specs=[ANY] * n, out_specs=[ANY] * n,
        out_shape=[jax.ShapeDtypeStruct((4,) + g.shape[2:], g.dtype) for g in grads],
        scratch_shapes=[pltpu.SemaphoreType.DMA((n,)), pltpu.SemaphoreType.DMA((n,))],
    )(*grads)
    return list(outs)


def _chip_scatter(parts, name):
    return _run_rider(_scatter_rider(parts), name)


def _scatter_rider(parts):
    n = len(parts)

    def steps(ins, outs, sems):
        send_sems, recv_sems = sems
        x, y, c = _place()
        chips = [(1 - x, y), (x, 1 - y), (1 - x, 1 - y)]

        def sends():
            return [pltpu.make_async_remote_copy(
                src_ref=ins[a].at[2 * px + py], dst_ref=outs[a].at[j], send_sem=send_sems.at[a * 3 + j],
                recv_sem=recv_sems.at[a * 3 + j], device_id=(px, py, c), device_id_type=MESH)
                for a in range(n) for j, (px, py) in enumerate(chips)]

        def start():
            for cp in sends():
                cp.start()

        def finish():
            for cp in sends():
                cp.wait_recv()
            for cp in sends():
                cp.wait_send()
        return start, None, finish

    return dict(ins=list(parts), out_shapes=[jax.ShapeDtypeStruct((3,) + p.shape[1:], p.dtype) for p in parts],
                sems=[pltpu.SemaphoreType.DMA((3 * n,)), pltpu.SemaphoreType.DMA((3 * n,))], steps=steps)


def _sum_parts(own, got, chip, name):
    _, R, C = own.shape
    tl = _tile(R, 512, HALO)
    tc = C if C % LANES else _tile(C, 1024)

    def body(chip_ref, own_ref, got_ref, o_ref):
        acc = own_ref[...].astype(f32)
        for k in range(3):
            acc = acc + got_ref[k].astype(f32)
        o_ref[...] = acc

    return pl.pallas_call(
        body, name=name,
        grid_spec=pltpu.PrefetchScalarGridSpec(
            num_scalar_prefetch=1, grid=(R // tl, C // tc),
            in_specs=[pl.BlockSpec((None, tl, tc), lambda i, j, ch: (ch[0], i, j)),
                      pl.BlockSpec((3, tl, tc), lambda i, j, ch: (0, i, j))],
            out_specs=pl.BlockSpec((tl, tc), lambda i, j, ch: (i, j))),
        out_shape=jax.ShapeDtypeStruct((R, C), f32), compiler_params=_cp(("parallel", "parallel")),
    )(chip, own, got)


def _adamw_halves(w, m, v, g_mine, g_other, core, name):
    NL, R, C = w.shape
    r = R // 2
    tl = _tile(r, 512, HALO)
    tc = C if C % LANES else _tile(C, 1024)
    nh = r // tl
    c1 = 1.0 - ADAM_B1 ** ADAM_STEP
    c2 = 1.0 - ADAM_B2 ** ADAM_STEP

    def body(core_ref, w_ref, m_ref, v_ref, gm_ref, go_ref, g_out, d_out, m_out, v_out):
        i = pl.program_id(1)
        mine = lax.div(i, nh) == core_ref[0]
        gv = jnp.where(mine, gm_ref[...], go_ref[...])
        mn = ADAM_B1 * m_ref[...] + (1.0 - ADAM_B1) * gv
        vn = ADAM_B2 * v_ref[...] + (1.0 - ADAM_B2) * (gv * gv)
        g_out[...] = gv
        d_out[...] = -ADAM_LR * ((mn / c1) / (jnp.sqrt(vn / c2) + ADAM_EPS) + ADAM_WD * w_ref[...])
        m_out[...] = mn
        v_out[...] = vn

    full = pl.BlockSpec((None, tl, tc), lambda l, i, j, co: (l, i, j))
    mine_spec = pl.BlockSpec((None, tl, tc), lambda l, i, j, co: (l, jnp.clip(i - co[0] * nh, 0, nh - 1), j))
    other_spec = pl.BlockSpec((None, tl, tc), lambda l, i, j, co: (l, jnp.clip(i - (1 - co[0]) * nh, 0, nh - 1), j))
    return pl.pallas_call(
        body, name=name,
        grid_spec=pltpu.PrefetchScalarGridSpec(
            num_scalar_prefetch=1, grid=(NL, R // tl, C // tc),
            in_specs=[full, full, full, mine_spec, other_spec], out_specs=[full] * 4),
        out_shape=[jax.ShapeDtypeStruct((NL, R, C), f32)] * 4,
        compiler_params=_cp(("parallel", "parallel", "parallel")),
    )(core, w, m, v, g_mine, g_other)


def _s5_discretize(log_step, a_re, a_im, b_re, b_im):
    step = jnp.exp(log_step)[:, None]
    mag = jnp.exp(a_re * step)
    abar_re = mag * jnp.cos(a_im * step)
    abar_im = mag * jnp.sin(a_im * step)
    den = a_re * a_re + a_im * a_im
    nr = abar_re - 1.0
    fr = (nr * a_re + abar_im * a_im) / den
    fi = (abar_im * a_re - nr * a_im) / den
    bbar_re = fr[..., None] * b_re - fi[..., None] * b_im
    bbar_im = fr[..., None] * b_im + fi[..., None] * b_re
    return abar_re, abar_im, bbar_re, bbar_im


def _s5_prepare(p, cfg):
    abar_re, abar_im, bbar_re, bbar_im = _s5_discretize(p["log_step"], p["a_re"], p["a_im"], p["b_re"], p["b_im"])
    step = jnp.exp(p["log_step"])[:, None]
    arow, tab = _s5_tables(abar_re, abar_im, p["a_re"], p["a_im"], step, cfg)
    bmat, cmat = _s5_mats(bbar_re, bbar_im, p["c_re"], p["c_im"], cfg)
    return dict(arow=arow, tab=tab, bmat=bmat, cmat=cmat, drow=p["d"].reshape(1, cfg.D))


def _s5_param_grads(p, dbmat, dcmat, dabar, dd, cfg):
    J, P = cfg.G // 8, cfg.P
    dbb_re, dbb_im, dc_re, dc_im = _s5_unmats(dbmat, dcmat, cfg)
    da = dabar.reshape(J, 2, 8, P)
    da_re, da_im = da[:, 0].reshape(cfg.G, P), da[:, 1].reshape(cfg.G, P)
    _, vjp = jax.vjp(_s5_discretize, p["log_step"], p["a_re"], p["a_im"], p["b_re"], p["b_im"])
    dls, dare, daim, dbre, dbim = vjp((da_re, da_im, dbb_re, dbb_im))
    return dict(log_step=dls, a_re=dare, a_im=daim, b_re=dbre, b_im=dbim, c_re=dc_re, c_im=dc_im, d=dd.reshape(cfg.G, cfg.H))


def _resid_epi(acc, xv, gv):
    return xv + gv * acc, acc


def _ffn_fwd(x_in, g_norm, sc, sh, gate, w_up4, w_down, conv_w, conv_b, cfg, tag):
    h = _norm_mod_fwd(x_in, g_norm, sc, sh, cfg, f"ffn_norm_{tag}")
    a = _mm(h, w_up4, mode="nn", b4=True, tn=1408, out_dtypes=(bf16,), name=f"ffn_up_{tag}")
    act = _conv_act_fwd(a, conv_w, conv_b, cfg)
    x_out, out = _mm(act, w_down, mode="nn", extras=[(x_in, "mn"), (gate, "n")], epi=_resid_epi,
                     out_dtypes=(f32, bf16), name=f"ffn_down_{tag}")
    return x_out, dict(h=h, a=a, act=act, out=out)


def _ffn_bwd(dx, x_in, sv, g_norm, sc, gate, w_up4, w_down, conv_w, conv_b, cfg, tag):
    F = cfg.F
    dout, dgate = _gate_bwd(dx, sv["out"], gate, cfg, f"ffn_gate_bwd_{tag}")
    dact = _mm(dout, w_down, mode="nt", out_dtypes=(bf16,), name=f"ffn_dact_{tag}")
    dw_down = _mm(sv["act"], dout, mode="tn", out_dtypes=(bf16,), name=f"ffn_dwdown_{tag}")
    dcu, dcv, dwu, dwv, dbu, dbv = _conv_act_bwd1(dact, sv["a"], conv_w, conv_b, cfg)
    dau = _conv_bwd2(dcu, conv_w[:, :F], cfg, f"conv_bwd2u_{tag}")
    dav = _conv_bwd2(dcv, conv_w[:, F:], cfg, f"conv_bwd2v_{tag}")
    da = jnp.concatenate([dau, dav], axis=1)
    dh = _mm(da, w_up4, mode="nt", b4=True, tk=1408, out_dtypes=(bf16,), name=f"ffn_dh_{tag}")
    dw_up = _mm(sv["h"], da, mode="tn", out4=True, tn=1408, out_dtypes=(bf16,), name=f"ffn_dwup_{tag}")
    dx_in, A, B = _norm_mod_bwd(dh, x_in, g_norm, sc, dx, cfg, f"ffn_norm_bwd_{tag}")
    small = dict(norm_g=(1.0 + sc) * A, sc=g_norm * A, sh=B, gate=dgate,
                 conv_w=jnp.concatenate([dwu, dwv], axis=1), conv_b=jnp.concatenate([dbu, dbv], axis=1))
    return dx_in, dw_up, dw_down, small


class _NoExchange:
    def rider(self, key, grads=None):
        return None

    def done(self, key, extra):
        return {}


def _local_step(cfg, x, tgt, mod, W, sp, exch=None):
    D, NH = cfg.D, cfg.NH
    exch = exch or _NoExchange()
    W, big = dict(W), {}

    def hand_over(key, grads):
        rider = exch.rider(key, grads)
        if rider is None:
            big.update(grads)
        return rider
    row = lambda v: v.reshape(1, -1)
    nmg0, nmg1 = row(sp["norm_mix_g"][0]), row(sp["norm_mix_g"][1])
    nfg0, nfg1 = row(sp["norm_ffn_g"][0]), row(sp["norm_ffn_g"][1])
    kvg, fng = row(sp["kv_norm_g"]), row(sp["final_norm_g"])
    cw0, cw1 = sp["ffn_conv_w"][0], sp["ffn_conv_w"][1]
    cb0, cb1 = row(sp["ffn_conv_b"][0]), row(sp["ffn_conv_b"][1])
    glu_b = row(sp["ssm_glu_b"])
    fb = jnp.zeros((1, LANES), f32).at[0, :NH].set(sp["forget_b"])
    s5p = {k: sp["ssm_" + k][0] for k in ("log_step", "a_re", "a_im", "b_re", "b_im", "c_re", "c_im", "d")}
    s5 = _s5_prepare(s5p, cfg)
    m0, m1 = mod["l0"], mod["l1"]

    h0 = _norm_mod_fwd(x, nmg0, m0["sc_m"], m0["sh_m"], cfg, "mix_norm_0")
    u = _mm(h0, W["ssm_w_in"], mode="nn", name="ssm_in")
    y, gact, cin, extra = _s5_fwd(u, s5["bmat"], s5["cmat"], s5["drow"], s5["arow"], s5["tab"], cfg, exch.rider("s5_fwd"))
    W.update(exch.done("s5_fwd", extra))

    def glu_epi(acc, bv, gv):
        pre = acc + bv
        return pre, gv.astype(f32) * _sigmoid(pre)
    pre, z = _mm(gact, W["ssm_glu_w"], mode="nn", extras=[(glu_b, "n"), (gact, "mn")], epi=glu_epi,
                 out_dtypes=(f32, bf16), name="ssm_glu")
    x1, out_m0 = _mm(z, W["ssm_w_out"], mode="nn", extras=[(x, "mn"), (m0["g_m"], "n")], epi=_resid_epi,
                     out_dtypes=(f32, bf16), name="ssm_out")
    x2, ffn0 = _ffn_fwd(x1, nfg0, m0["sc_f"], m0["sh_f"], m0["g_f"], W["ffn_w_up0"], W["ffn_w_down0"], cw0, cb0, cfg, "0")

    hk = _norm_mod_fwd(x2, kvg, mod["sc_kv"], mod["sh_kv"], cfg, "kv_norm")
    kvb = _mm(hk, W["kv_w"], mode="nn", out_dtypes=(bf16,), name="kv_proj")
    zf = _mm(hk, W["kv_wf"], mode="nn", name="kv_fproj")
    fc = _fgate_fwd(zf, fb, cfg)
    fct = fc[:, :NH].T
    fk = fct[:, None, :]

    h1 = _norm_mod_fwd(x2, nmg1, m1["sc_m"], m1["sh_m"], cfg, "mix_norm_1")
    q = _mm(h1, W["attn_w_q"], mode="nn", out_dtypes=(bf16,), name="attn_q")
    o, lse, extra = _ta_fwd(q, kvb, fk, cfg, exch.rider("attn_fwd"))
    W.update(exch.done("attn_fwd", extra))
    x3, out_m1 = _mm(o, W["attn_w_out"], mode="nn", extras=[(x2, "mn"), (m1["g_m"], "n")], epi=_resid_epi,
                     out_dtypes=(f32, bf16), name="attn_out")
    x4, ffn1 = _ffn_fwd(x3, nfg1, m1["sc_f"], m1["sh_f"], m1["g_f"], W["ffn_w_up1"], W["ffn_w_down1"], cw1, cb1, cfg, "1")

    dx, dfng, lcol = _final_loss(x4, fng, tgt, cfg)
    loss = (0.5 / D) * jnp.sum(lcol)

    dx, dw_up1, dw_down1, sf1 = _ffn_bwd(dx, x3, ffn1, nfg1, m1["sc_f"], m1["g_f"], W["ffn_w_up1"], W["ffn_w_down1"], cw1, cb1, cfg, "1")
    dout, dgm1 = _gate_bwd(dx, out_m1, m1["g_m"], cfg, "attn_gate_bwd")
    do = _mm(dout, W["attn_w_out"], mode="nt", out_dtypes=(bf16,), name="attn_do")
    dw_ao = _mm(o, dout, mode="tn", out_dtypes=(bf16,), name="attn_dwout")
    dq, dfq, delta, extra = _ta_bwd_dq(q, kvb, do, o, lse, fk, cfg,
                                       hand_over("attn_bwd", dict(ffn_w_up1=dw_up1, ffn_w_down1=dw_down1)))
    exch.done("attn_bwd", extra)
    dk, dv, dfk, _ = _ta_bwd_dkv(q, kvb, do, delta, lse, fk, cfg)
    dh1 = _mm(dq, W["attn_w_q"], mode="nt", out_dtypes=(bf16,), name="attn_dh")
    dw_q = _mm(h1, dq, mode="tn", out_dtypes=(bf16,), name="attn_dwq")
    dx, A1, B1 = _norm_mod_bwd(dh1, x2, nmg1, m1["sc_m"], dx, cfg, "mix_norm_bwd_1")

    dfc = jnp.pad((dfq[:, :, 0] + dfk[:, :, 0]).T, ((0, 0), (0, LANES - NH)))
    dzf, dfb = _fgate_bwd(dfc, zf, fb, cfg)
    dkv = jnp.concatenate([dk, dv], axis=1)
    dhk1 = _mm(dkv, W["kv_w"], mode="nt", name="kv_dh1")
    dhk = _mm(dzf, W["kv_wf"], mode="nt", extras=[(dhk1, "mn")], epi=lambda acc, e: (acc + e,), out_dtypes=(bf16,), name="kv_dh2")
    dw_kv = _mm(hk, dkv, mode="tn", out_dtypes=(bf16,), name="kv_dw")
    dw_kf = _mm(hk, dzf, mode="tn", out_dtypes=(bf16,), name="kv_dwf")
    dx, Ak, Bk = _norm_mod_bwd(dhk, x2, kvg, mod["sc_kv"], dx, cfg, "kv_norm_bwd")

    dx, dw_up0, dw_down0, sf0 = _ffn_bwd(dx, x1, ffn0, nfg0, m0["sc_f"], m0["g_f"], W["ffn_w_up0"], W["ffn_w_down0"], cw0, cb0, cfg, "0")
    dout, dgm0 = _gate_bwd(dx, out_m0, m0["g_m"], cfg, "ssm_gate_bwd")
    dz = _mm(dout, W["ssm_w_out"], mode="nt", out_dtypes=(bf16,), name="ssm_dz")
    dw_so = _mm(z, dout, mode="tn", out_dtypes=(bf16,), name="ssm_dwout")
    dpre, dgd, dglub = _glu_bwd(dz, gact, pre, cfg)
    dy = _mm(dpre, W["ssm_glu_w"], mode="nt", extras=[(dgd, "mn"), (y, "mn")],
             epi=lambda acc, e, yv: ((acc + e) * _gelu_grad(yv),), name="ssm_dy")
    dw_glu = _mm(gact, dpre, mode="tn", out_dtypes=(bf16,), name="ssm_dwglu")
    rider = hand_over("s5_bwd", dict(attn_w_q=dw_q, attn_w_out=dw_ao, kv_w=jnp.concatenate([dw_kv, dw_kf[:, :NH]], axis=1),
                                     ffn_w_up0=dw_up0, ffn_w_down0=dw_down0))
    du, dbm, dcm, dab, dd, extra = _s5_bwd(u, dy, cin, s5["bmat"], s5["cmat"], s5["drow"], s5["arow"], s5["tab"], cfg, rider)
    exch.done("s5_bwd", extra)
    dh0 = _mm(du, W["ssm_w_in"], mode="nt", out_dtypes=(bf16,), name="ssm_dh")
    dw_in = _mm(h0, du, mode="tn", out_dtypes=(bf16,), name="ssm_dwin")
    dx, A0, B0 = _norm_mod_bwd(dh0, x, nmg0, m0["sc_m"], dx, cfg, "mix_norm_bwd_0")

    s5g = _s5_param_grads(s5p, dbm, dcm, dab, dd, cfg)
    big.update(ssm_w_in=dw_in, ssm_glu_w=dw_glu, ssm_w_out=dw_so)
    small = dict(
        norm_mix_g=jnp.concatenate([(1.0 + m0["sc_m"]) * A0, (1.0 + m1["sc_m"]) * A1], axis=0),
        norm_ffn_g=jnp.concatenate([sf0["norm_g"], sf1["norm_g"]], axis=0),
        ssm_glu_b=dglub, kv_norm_g=(1.0 + mod["sc_kv"]) * Ak, forget_b=dfb[0, :NH],
        ffn_conv_w=jnp.stack([sf0["conv_w"], sf1["conv_w"]]), ffn_conv_b=jnp.concatenate([sf0["conv_b"], sf1["conv_b"]], axis=0),
        final_norm_g=dfng, **{"ssm_" + k: v[None] for k, v in s5g.items()})
    dmod = [jnp.concatenate([B0, nmg0 * A0, dgm0, sf0["sh"], sf0["sc"], sf0["gate"]], axis=1),
            jnp.concatenate([B1, nmg1 * A1, dgm1, sf1["sh"], sf1["sc"], sf1["gate"]], axis=1),
            jnp.concatenate([Bk, kvg * Ak], axis=1)]
    return loss, dx, big, small, dmod


WEIGHTS = ["mod_w", "mod_b", "norm_mix_g", "norm_ffn_g", "ssm_w_in", "ssm_log_step", "ssm_a_re", "ssm_a_im", "ssm_b_re",
           "ssm_b_im", "ssm_c_re", "ssm_c_im", "ssm_d", "ssm_glu_w", "ssm_glu_b", "ssm_w_out", "kv_mod_w", "kv_mod_b",
           "kv_norm_g", "kv_w", "forget_b", "attn_w_q", "attn_w_out", "ffn_w_up", "ffn_conv_w", "ffn_conv_b", "ffn_w_down",
           "final_norm_g"]
ARGS = ["x", "c"] + WEIGHTS + ["loss_target"] + ["m_" + n for n in WEIGHTS] + ["v_" + n for n in WEIGHTS]
SMALL = ["mod_b", "norm_mix_g", "norm_ffn_g", "ssm_log_step", "ssm_a_re", "ssm_a_im", "ssm_b_re", "ssm_b_im", "ssm_c_re",
         "ssm_c_im", "ssm_d", "ssm_glu_b", "kv_mod_b", "kv_norm_g", "forget_b", "ffn_conv_w", "ffn_conv_b", "final_norm_g"]
PACK_ROWS = 512


def _pack(arrs):
    flat = jnp.concatenate([a.reshape(-1).astype(f32) for a in arrs])
    unit = PACK_ROWS * LANES
    n = -(-flat.shape[0] // unit) * unit
    return jnp.pad(flat, (0, n - flat.shape[0])).reshape(-1, LANES)


def _unpack(packed, shapes):
    flat, out, off = packed.reshape(-1), [], 0
    for s in shapes:
        n = math.prod(s)
        out.append(flat[off:off + n].reshape(s))
        off += n
    return out


def _silu(v):
    return v * _sigmoid(v)


def _half(w, c, axis):
    r = w.shape[axis] // 2
    return lax.dynamic_slice_in_dim(w, c * r, r, axis=axis)


class _Exchange:
    FWD = dict(s5_fwd=["kv_w", "attn_w_q", "attn_w_out"], attn_fwd=["ffn_w_up1", "ffn_w_down1"])

    def __init__(self, cfg, blocks, core):
        self.cfg, self.blocks, self.core = cfg, blocks, core
        self.parts, self.scattered, self.names = {}, {}, {}

    def weights(self, names, gathered):
        D, F, NH = self.cfg.D, self.cfg.F, self.cfg.NH
        W = {}
        for n, g in zip(names, gathered):
            if n.startswith("ffn_w_up"):
                W[n] = g.reshape(4, D, 2 * F // 4)
            elif n == "kv_w":
                full = g.reshape(4, D, -1).transpose(1, 0, 2).reshape(D, -1)
                W["kv_w"] = full[:, :2 * D]
                W["kv_wf"] = jnp.pad(full[:, 2 * D:], ((0, 0), (0, LANES - NH)))
            else:
                W[n] = g.reshape(-1, D)
        return W

    def sibling_sum(self, key, grads):
        D = self.cfg.D

        def blocks_of(n, g):
            if n.startswith("ffn_w_up"):
                return g.reshape(4, 2, D // 2, -1)
            if n == "kv_w":
                return g.reshape(D, 4, -1).transpose(1, 0, 2).reshape(4, 2, D // 2, -1)
            return g.reshape(4, 2, g.shape[0] // 8, g.shape[1])
        names = list(grads)
        gb = [blocks_of(n, grads[n]) for n in names]
        recv = _sibling_swap_halves(gb, f"grad_sibling_swap_{key}")
        for n, g, r in zip(names, gb, recv):
            keep = lax.dynamic_index_in_dim(g, self.core, axis=1, keepdims=False)
            rr, cc = keep.shape[1], keep.shape[2]
            self.parts[n] = _add2(keep.reshape(4 * rr, cc), r.reshape(4 * rr, cc), bf16, f"grad_add_{n}").reshape(4, rr, cc)
        return self.parts

    def rider(self, key, grads=None):
        if key in self.FWD:
            return _gather_rider([self.blocks[n] for n in self.FWD[key]])
        self.names[key] = list(grads)
        parts = self.sibling_sum(key, grads)
        return _scatter_rider([parts[n] for n in self.names[key]])

    def done(self, key, extra):
        if key in self.FWD:
            return self.weights(self.FWD[key], extra)
        self.scattered.update(zip(self.names[key], extra))
        return {}


def kernel(x, c, mod_w, mod_b, norm_mix_g, norm_ffn_g, ssm_w_in, ssm_log_step, ssm_a_re, ssm_a_im, ssm_b_re, ssm_b_im, ssm_c_re, ssm_c_im, ssm_d, ssm_glu_w, ssm_glu_b, ssm_w_out, kv_mod_w, kv_mod_b, kv_norm_g, kv_w, forget_b, attn_w_q, attn_w_out, ffn_w_up, ffn_conv_w, ffn_conv_b, ffn_w_down, final_norm_g, loss_target, m_mod_w, m_mod_b, m_norm_mix_g, m_norm_ffn_g, m_ssm_w_in, m_ssm_log_step, m_ssm_a_re, m_ssm_a_im, m_ssm_b_re, m_ssm_b_im, m_ssm_c_re, m_ssm_c_im, m_ssm_d, m_ssm_glu_w, m_ssm_glu_b, m_ssm_w_out, m_kv_mod_w, m_kv_mod_b, m_kv_norm_g, m_kv_w, m_forget_b, m_attn_w_q, m_attn_w_out, m_ffn_w_up, m_ffn_conv_w, m_ffn_conv_b, m_ffn_w_down, m_final_norm_g, v_mod_w, v_mod_b, v_norm_mix_g, v_norm_ffn_g, v_ssm_w_in, v_ssm_log_step, v_ssm_a_re, v_ssm_a_im, v_ssm_b_re, v_ssm_b_im, v_ssm_c_re, v_ssm_c_im, v_ssm_d, v_ssm_glu_w, v_ssm_glu_b, v_ssm_w_out, v_kv_mod_w, v_kv_mod_b, v_kv_norm_g, v_kv_w, v_forget_b, v_attn_w_q, v_attn_w_out, v_ffn_w_up, v_ffn_conv_w, v_ffn_conv_b, v_ffn_w_down, v_final_norm_g):
    a = dict(locals())
    assert list(a) == ARGS
    return _step(CFG, a)


def _step(cfg, a):
    D, F, NH = cfg.D, cfg.F, cfg.NH
    x_, y_, c_ = _place()
    chip, dev = 2 * x_ + y_, 4 * x_ + 2 * y_ + c_

    big_src = dict(ssm_w_in=a["ssm_w_in"][0], ssm_glu_w=a["ssm_glu_w"][0], ssm_w_out=a["ssm_w_out"][0],
                   attn_w_q=a["attn_w_q"][0], attn_w_out=a["attn_w_out"][0],
                   ffn_w_up0=a["ffn_w_up"][0], ffn_w_up1=a["ffn_w_up"][1],
                   ffn_w_down0=a["ffn_w_down"][0], ffn_w_down1=a["ffn_w_down"][1], kv_w=a["kv_w"])
    big_names = list(big_src)
    exch = _Exchange(cfg, {n: _half(big_src[n], c_, 0).astype(bf16) for n in big_names}, c_)
    first = ["ssm_w_in", "ssm_glu_w", "ssm_w_out", "ffn_w_up0", "ffn_w_down0"]
    blocks = [exch.blocks[n] for n in first] + [_half(a["ssm_glu_b"], c_, 1), _half(a["ffn_conv_w"], c_, 2), a["c"]]
    got = _allgather8(blocks, "gather_weights")
    W = exch.weights(first, got)
    glu_b_full = got[-3].reshape(D)
    conv_w_full = got[-2].transpose(1, 2, 0, 3).reshape(2, 3, 2 * F)
    c16 = jnp.pad(got[-1].reshape(N_DEV, D), ((0, 16 - N_DEV), (0, 0)))

    mcols = [_mm(c16, a["mod_w"][l], mode="nn", a_pro=_silu, name=f"mod_fwd_{l}") for l in range(2)]
    mcols.append(_mm(c16, a["kv_mod_w"], mode="nn", a_pro=_silu, name="mod_fwd_kv"))
    widths = [m.shape[1] for m in mcols]
    mall = _allgather8([jnp.concatenate(mcols, axis=1)[:N_DEV]], "gather_mod")[0][0::2]
    offs = [0, widths[0], widths[0] + widths[1]]
    rows = []
    for off, wd, bias in zip(offs, widths, [a["mod_b"][0], a["mod_b"][1], a["kv_mod_b"]]):
        fullm = mall[:, :, off:off + wd].transpose(1, 0, 2).reshape(N_DEV, 4 * wd) + bias
        rows.append(lax.dynamic_slice_in_dim(fullm, dev, 1, axis=0))
    mod = {}
    for l in range(2):
        mod[f"l{l}"] = dict(zip(["sh_m", "sc_m", "g_m", "sh_f", "sc_f", "g_f"], jnp.split(rows[l], 6, axis=1)))
    mod["sh_kv"], mod["sc_kv"] = jnp.split(rows[2], 2, axis=1)

    sp = {n: a[n] for n in ["norm_mix_g", "norm_ffn_g", "kv_norm_g", "final_norm_g", "ffn_conv_b", "forget_b", "ssm_log_step",
                            "ssm_a_re", "ssm_a_im", "ssm_b_re", "ssm_b_im", "ssm_c_re", "ssm_c_im", "ssm_d"]}
    sp["ssm_glu_b"], sp["ffn_conv_w"] = glu_b_full, conv_w_full
    loss, dx, big, small, dmod = _local_step(cfg, a["x"][0], a["loss_target"][0], mod, W, sp, exch)
    loss = lax.psum(loss, ("x", "y", "c"))

    small["mod_b"] = jnp.concatenate([dmod[0], dmod[1]], axis=0)
    small["kv_mod_b"] = dmod[2]
    shapes = [(2, 6 * D) if n == "mod_b" else (1, D) if n == "ssm_glu_b" else (2, 3, 2 * F) if n == "ffn_conv_w"
              else a[n].shape for n in SMALL]
    packs = _allgather8([_pack([small[n] for n in SMALL])], "gather_small")[0]
    gsmall = dict(zip(SMALL, _unpack(_sum_lead(packs, f32, "sum_small"), shapes)))
    per_dev = packs.reshape(N_DEV, -1)
    sizes = [math.prod(s) for s in shapes]
    starts = dict(zip(SMALL, [sum(sizes[:i]) for i in range(len(sizes))]))

    def rows_of(name, l, width):
        st = starts[name] + l * 6 * D
        blk = lax.dynamic_slice(per_dev, (0, st + chip * width), (N_DEV, width))
        return jnp.pad(blk, ((0, 16 - N_DEV), (0, 0)))
    g_mod_w = jnp.stack([_mm(c16, rows_of("mod_b", l, 6 * D // 4), mode="tn", a_pro=_silu, name=f"mod_dw_{l}") for l in range(2)])
    g_kv_mod_w = _mm(c16, rows_of("kv_mod_b", 0, 2 * D // 4), mode="tn", a_pro=_silu, name="mod_dw_kv")
    gsmall["ssm_glu_b"] = lax.dynamic_slice_in_dim(gsmall["ssm_glu_b"], chip * (D // 4), D // 4, axis=1)
    gsmall["ffn_conv_w"] = lax.dynamic_slice_in_dim(gsmall["ffn_conv_w"], chip * (2 * F // 4), 2 * F // 4, axis=2)

    last = list(big)
    exch.scattered.update(zip(last, _chip_scatter([exch.sibling_sum("tail", big)[n] for n in last], "grad_chip_scatter")))
    chip1, core1 = jnp.reshape(chip, (1,)).astype(jnp.int32), jnp.reshape(c_, (1,)).astype(jnp.int32)
    mine = {n: _sum_parts(exch.parts[n], exch.scattered[n], chip1, f"grad_sum_{n}") for n in big_names}
    other = dict(zip(big_names, _sibling_send([mine[n] for n in big_names], "grad_sibling_send")))

    grads = dict(gsmall)
    grads["mod_w"], grads["kv_mod_w"] = g_mod_w, g_kv_mod_w
    delta, new_m, new_v = {}, {}, {}
    members = dict(ssm_w_in=["ssm_w_in"], ssm_glu_w=["ssm_glu_w"], ssm_w_out=["ssm_w_out"], attn_w_q=["attn_w_q"],
                   attn_w_out=["attn_w_out"], kv_w=["kv_w"], ffn_w_up=["ffn_w_up0", "ffn_w_up1"],
                   ffn_w_down=["ffn_w_down0", "ffn_w_down1"])
    for n, parts_ in members.items():
        shp = a[n].shape
        three = lambda t: t.reshape(len(parts_), -1, shp[-1])
        g_, d_, m_, v_ = _adamw_halves(three(a[n]), three(a["m_" + n]), three(a["v_" + n]),
                                       jnp.stack([mine[p] for p in parts_]), jnp.stack([other[p] for p in parts_]),
                                       core1, f"adamw_{n}")
        grads[n], delta[n], new_m[n], new_v[n] = g_.reshape(shp), d_.reshape(shp), m_.reshape(shp), v_.reshape(shp)
    for n in ["mod_w", "kv_mod_w"]:
        shp = a[n].shape
        two = lambda t: t.reshape(-1, shp[-1])
        d_, m_, v_ = _adamw(two(a[n]), two(grads[n]), two(a["m_" + n]), two(a["v_" + n]), f"adamw_{n}")
        delta[n], new_m[n], new_v[n] = d_.reshape(shp), m_.reshape(shp), v_.reshape(shp)
    grads = {n: grads[n].reshape(a[n].shape) for n in WEIGHTS}
    sshapes = [a[n].shape for n in SMALL]
    d_, m_, v_ = _adamw(_pack([a[n] for n in SMALL]), _pack([grads[n] for n in SMALL]), _pack([a["m_" + n] for n in SMALL]),
                        _pack([a["v_" + n] for n in SMALL]), "adamw_small")
    for n, dd_, mm_, vv_ in zip(SMALL, _unpack(d_, sshapes), _unpack(m_, sshapes), _unpack(v_, sshapes)):
        delta[n], new_m[n], new_v[n] = dd_, mm_, vv_

    return (loss, dx[None], *[grads[n] for n in WEIGHTS], *[delta[n] for n in WEIGHTS],
            *[new_m[n] for n in WEIGHTS], *[new_v[n] for n in WEIGHTS])
```

```python
import collections
import functools
import math

import jax
import jax.numpy as jnp
from jax import lax
from jax.experimental import pallas as pl
from jax.experimental.pallas import tpu as pltpu

f32 = jnp.float32
bf16 = jnp.bfloat16
MESH = pl.DeviceIdType.MESH

LANES = 128
SUBLANES = 8
VMEM_BYTES_V7X = 64 * 1024 * 1024
VMEM_LIMIT = 56 * 1024 * 1024

Cfg = collections.namedtuple("Cfg", "L D G P H NH DH F TC BQ")
CFG = Cfg(L=4096, D=2048, G=128, P=64, H=16, NH=16, DH=128, F=5632, TC=512, BQ=512)
NORM_EPS = 1e-6
ADAM_LR, ADAM_B1, ADAM_B2, ADAM_EPS, ADAM_WD, ADAM_STEP = 0.001, 0.9, 0.999, 1e-08, 0.01, 10
N_DEV = 8


def _cp(sem=None):
    return pltpu.CompilerParams(dimension_semantics=sem, vmem_limit_bytes=VMEM_LIMIT)


def _tile(dim, pref, unit=LANES):
    if dim <= pref:
        return dim
    t = (pref // unit) * unit
    while t > unit and dim % t:
        t -= unit
    assert dim % t == 0, (dim, pref)
    return t


_DIMS = {"nn": (((1,), (0,)), ((), ())), "nt": (((1,), (1,)), ((), ())), "tn": (((0,), (0,)), ((), ()))}


def _mm(a, b, *, mode, name, tm=1024, tn=1024, tk=2048, b4=False, out4=False, a_pro=None, extras=(), epi=None,
        out_dtypes=(f32,), rider=None):
    if mode == "tn":
        K, M = a.shape
    else:
        M, K = a.shape
    if b4:
        R, c4 = b.shape[1], b.shape[2]
        N = R if mode == "nt" else 4 * c4
        assert (K == 4 * c4) if mode == "nt" else (K == R)
    else:
        N = b.shape[0] if mode == "nt" else b.shape[1]
        assert K == (b.shape[1] if mode == "nt" else b.shape[0])
    n4 = N // 4
    tm = _tile(M, tm, LANES if mode == "tn" else SUBLANES * 2)
    tn = _tile(n4 if out4 or (b4 and mode != "nt") else N, tn)
    tk = _tile(b.shape[2] if (b4 and mode == "nt") else K, tk)
    nm, nn_, nk = M // tm, N // tn, K // tk

    a_spec = pl.BlockSpec((tk, tm), lambda i, j, k: (k, i)) if mode == "tn" else pl.BlockSpec((tm, tk), lambda i, j, k: (i, k))
    if b4 and mode == "nt":
        q = b.shape[2] // tk
        b_spec = pl.BlockSpec((None, tn, tk), lambda i, j, k: (lax.div(k, q), j, lax.rem(k, q)))
    elif b4:
        q = b.shape[2] // tn
        b_spec = pl.BlockSpec((None, tk, tn), lambda i, j, k: (lax.div(j, q), k, lax.rem(j, q)))
    elif mode == "nt":
        b_spec = pl.BlockSpec((tn, tk), lambda i, j, k: (j, k))
    else:
        b_spec = pl.BlockSpec((tk, tn), lambda i, j, k: (k, j))
    ex_specs = []
    for arr, kind in extras:
        if kind == "mn":
            ex_specs.append(pl.BlockSpec((tm, tn), lambda i, j, k: (i, j)))
        else:
            ex_specs.append(pl.BlockSpec((1, tn), lambda i, j, k: (0, j)))
    if out4:
        qo = n4 // tn
        o_spec = pl.BlockSpec((None, tm, tn), lambda i, j, k: (lax.div(j, qo), i, lax.rem(j, qo)))
        o_shapes = [jax.ShapeDtypeStruct((4, M, n4), dt) for dt in out_dtypes]
    else:
        o_spec = pl.BlockSpec((tm, tn), lambda i, j, k: (i, j))
        o_shapes = [jax.ShapeDtypeStruct((M, N), dt) for dt in out_dtypes]
    ne, no = len(extras), len(out_dtypes)
    dims = _DIMS[mode]

    def body(a_ref, b_ref, *rest):
        ex_refs, o_refs, acc_ref = rest[:ne], rest[ne:ne + no], rest[ne + no]
        k = pl.program_id(2)

        @pl.when(k == 0)
        def _():
            acc_ref[...] = jnp.zeros_like(acc_ref)

        av = a_ref[...]
        if a_pro is not None:
            av = a_pro(av)
        acc_ref[...] += lax.dot_general(av.astype(bf16), b_ref[...].astype(bf16), dims, preferred_element_type=f32)

        @pl.when(k == nk - 1)
        def _():
            acc = acc_ref[...]
            outs = (acc,) if epi is None else epi(acc, *[r[...] for r in ex_refs])
            for o_ref, o in zip(o_refs, outs):
                o_ref[...] = o.astype(o_ref.dtype)

    res, extra = _host_call(
        body, name=name, grid=(nm, nn_, nk), in_specs=[a_spec, b_spec] + ex_specs, out_specs=[o_spec] * no,
        out_shape=o_shapes, scratch_shapes=[pltpu.VMEM((tm, tn), f32)], args=(a, b, *[e[0] for e in extras]), rider=rider)
    res = res[0] if no == 1 else res
    return (res, extra) if rider else res


HALO = 16


def _rowwise(fn, ins, outs, accs, *, L, C, tl, tc, name):
    tl = _tile(L, tl, HALO)
    tc = _tile(C, tc)
    ni, nj = L // tl, C // tc
    hb = tl // HALO
    nh = L // HALO
    in_specs = []
    for spec in ins:
        kind = spec[1]
        off = spec[2] if len(spec) > 2 else 0
        if kind == "rc":
            in_specs.append(pl.BlockSpec((tl, tc), lambda j, i, off=off: (i, j + off)))
        elif kind == "c":
            in_specs.append(pl.BlockSpec((1, tc), lambda j, i, off=off: (0, j + off)))
        elif kind == "c3":
            in_specs.append(pl.BlockSpec((3, tc), lambda j, i, off=off: (0, j + off)))
        elif kind == "prev":
            in_specs.append(pl.BlockSpec((HALO, tc), lambda j, i, off=off: (jnp.maximum(i * hb - 1, 0), j + off)))
        elif kind == "next":
            in_specs.append(pl.BlockSpec((HALO, tc), lambda j, i, off=off: (jnp.minimum((i + 1) * hb, nh - 1), j + off)))
        else:
            raise ValueError(kind)
    out_specs = [pl.BlockSpec((tl, tc), lambda j, i: (i, j)) for _ in outs]
    out_specs += [pl.BlockSpec((r, tc), lambda j, i: (0, j)) for r in accs]
    out_shape = [jax.ShapeDtypeStruct((L, C), dt) for dt in outs] + [jax.ShapeDtypeStruct((r, C), f32) for r in accs]
    nin, nout, nacc = len(ins), len(outs), len(accs)

    def body(*refs):
        i = pl.program_id(1)
        tiles = [r[...] for r in refs[:nin]]
        o_vals, a_vals = fn(i, ni, *tiles)
        for r, v in zip(refs[nin:nin + nout], o_vals):
            r[...] = v.astype(r.dtype)
        if nacc:
            @pl.when(i == 0)
            def _():
                for r in refs[nin + nout:]:
                    r[...] = jnp.zeros_like(r)
            for r, v in zip(refs[nin + nout:], a_vals):
                r[...] += v

    res = pl.pallas_call(
        body, name=name, grid=(nj, ni), in_specs=in_specs, out_specs=out_specs, out_shape=out_shape,
        compiler_params=_cp(("parallel", "arbitrary")),
    )(*[s[0] for s in ins])
    return res


def _colsum(v):
    return jnp.sum(v, axis=0, keepdims=True)


def _sigmoid(x):
    return 1.0 / (1.0 + jnp.exp(-x))


_GELU_C = math.sqrt(2.0 / math.pi)


def _gelu(y):
    t = jnp.tanh(_GELU_C * (y + 0.044715 * y * y * y))
    return 0.5 * y * (1.0 + t)


def _gelu_grad(y):
    y2 = y * y
    t = jnp.tanh(_GELU_C * (y + 0.044715 * y * y2))
    return 0.5 * (1.0 + t) + 0.5 * y * (1.0 - t * t) * _GELU_C * (1.0 + 3.0 * 0.044715 * y2)


def _norm_mod_fwd(x, g, sc, sh, cfg, name):
    def fn(i, ni, xv, gv, scv, shv):
        rstd = lax.rsqrt(jnp.mean(xv * xv, axis=-1, keepdims=True) + NORM_EPS)
        return [xv * rstd * gv * (1.0 + scv) + shv], []
    return _rowwise(fn, [(x, "rc"), (g, "c"), (sc, "c"), (sh, "c")], [bf16], [], L=cfg.L, C=cfg.D, tl=256, tc=cfg.D, name=name)[0]


def _norm_mod_bwd(dh, x, g, sc, dres, cfg, name):
    def fn(i, ni, dhv, xv, gv, scv, *rest):
        dhv = dhv.astype(f32)
        rstd = lax.rsqrt(jnp.mean(xv * xv, axis=-1, keepdims=True) + NORM_EPS)
        xh = xv * rstd
        dxh = dhv * (gv * (1.0 + scv))
        dx = rstd * (dxh - xh * jnp.mean(dxh * xh, axis=-1, keepdims=True))
        if rest:
            dx = dx + rest[0]
        return [dx], [_colsum(dhv * xh), _colsum(dhv)]
    ins = [(dh, "rc"), (x, "rc"), (g, "c"), (sc, "c")] + ([(dres, "rc")] if dres is not None else [])
    return _rowwise(fn, ins, [f32], [1, 1], L=cfg.L, C=cfg.D, tl=256, tc=cfg.D, name=name)


def _final_loss(x, g, tgt, cfg):
    D = cfg.D

    def fn(i, ni, xv, gv, tv):
        rstd = lax.rsqrt(jnp.mean(xv * xv, axis=-1, keepdims=True) + NORM_EPS)
        xh = xv * rstd
        err = xh * gv - tv
        dy = err * (1.0 / D)
        dxh = dy * gv
        dx = rstd * (dxh - xh * jnp.mean(dxh * xh, axis=-1, keepdims=True))
        return [dx], [_colsum(dy * xh), _colsum(err * err)]
    return _rowwise(fn, [(x, "rc"), (g, "c"), (tgt, "rc")], [f32], [1, 1], L=cfg.L, C=D, tl=256, tc=D, name="final_loss")


def _gate_bwd(dx, out, gate, cfg, name):
    def fn(i, ni, dxv, ov, gv):
        return [dxv * gv], [_colsum(dxv * ov.astype(f32))]
    return _rowwise(fn, [(dx, "rc"), (out, "rc"), (gate, "c")], [bf16], [1], L=cfg.L, C=cfg.D, tl=512, tc=cfg.D, name=name)


def _glu_bwd(dz, g, pre, cfg):
    def fn(i, ni, dzv, gv, pv):
        dzv = dzv.astype(f32)
        gv = gv.astype(f32)
        s = _sigmoid(pv)
        dpre = dzv * gv * s * (1.0 - s)
        return [dpre, dzv * s], [_colsum(dpre)]
    return _rowwise(fn, [(dz, "rc"), (g, "rc"), (pre, "rc")], [bf16, f32], [1], L=cfg.L, C=cfg.D, tl=512, tc=cfg.D, name="glu_bwd")


def _shift_rows(av, pv, k, i):
    rows = lax.broadcasted_iota(jnp.int32, av.shape, 0)
    cur = pltpu.roll(av, k, 0)
    prev = pltpu.roll(pv, k, 0)
    prev = jnp.where(i > 0, prev, 0.0)
    prev_full = jnp.concatenate([prev, jnp.zeros((av.shape[0] - pv.shape[0], av.shape[1]), av.dtype)], axis=0) \
        if av.shape[0] > pv.shape[0] else prev
    return jnp.where(rows >= k, cur, prev_full)


def _shift_rows_up(av, nv, k, i, ni):
    n, h = av.shape[0], nv.shape[0]
    rows = lax.broadcasted_iota(jnp.int32, av.shape, 0)
    cur = pltpu.roll(av, n - k, 0)
    nxt = pltpu.roll(nv, h - k, 0)
    nxt = jnp.where(i < ni - 1, nxt, 0.0)
    nxt_full = jnp.concatenate([jnp.zeros((n - h, av.shape[1]), av.dtype), nxt], axis=0) if n > h else nxt
    return jnp.where(rows < n - k, cur, nxt_full)


def _conv3(av, pv, w, i):
    return w[0:1] * _shift_rows(av, pv, 2, i) + w[1:2] * _shift_rows(av, pv, 1, i) + w[2:3] * av


def _conv_act_fwd(a, conv_w, conv_b, cfg):
    F = cfg.F
    tc = _tile(F, 1408)
    nb = F // tc

    def fn(i, ni, au, av, pu, pv, wu, wv, bu, bv):
        cu = _conv3(au.astype(f32), pu.astype(f32), wu, i) + bu
        cv = _conv3(av.astype(f32), pv.astype(f32), wv, i) + bv
        return [cu * _sigmoid(cu) * cv], []
    ins = [(a, "rc"), (a, "rc", nb), (a, "prev"), (a, "prev", nb), (conv_w, "c3"), (conv_w, "c3", nb), (conv_b, "c"), (conv_b, "c", nb)]
    return _rowwise(fn, ins, [bf16], [], L=cfg.L, C=F, tl=512, tc=tc, name="conv_act_fwd")[0]


def _conv_act_bwd1(dact, a, conv_w, conv_b, cfg):
    F = cfg.F
    tc = _tile(F, 1408)
    nb = F // tc

    def fn(i, ni, dav, au, av, pu, pv, wu, wv, bu, bv):
        dav = dav.astype(f32)
        au, av, pu, pv = au.astype(f32), av.astype(f32), pu.astype(f32), pv.astype(f32)
        au1, au2 = _shift_rows(au, pu, 1, i), _shift_rows(au, pu, 2, i)
        av1, av2 = _shift_rows(av, pv, 1, i), _shift_rows(av, pv, 2, i)
        cu = wu[0:1] * au2 + wu[1:2] * au1 + wu[2:3] * au + bu
        cv = wv[0:1] * av2 + wv[1:2] * av1 + wv[2:3] * av + bv
        s = _sigmoid(cu)
        dcu = dav * cv * (s * (1.0 + cu * (1.0 - s)))
        dcv = dav * cu * s
        dwu = jnp.concatenate([_colsum(dcu * au2), _colsum(dcu * au1), _colsum(dcu * au)], axis=0)
        dwv = jnp.concatenate([_colsum(dcv * av2), _colsum(dcv * av1), _colsum(dcv * av)], axis=0)
        return [dcu, dcv], [dwu, dwv, _colsum(dcu), _colsum(dcv)]
    ins = [(dact, "rc"), (a, "rc"), (a, "rc", nb), (a, "prev"), (a, "prev", nb), (conv_w, "c3"), (conv_w, "c3", nb),
           (conv_b, "c"), (conv_b, "c", nb)]
    return _rowwise(fn, ins, [bf16, bf16], [3, 3, 1, 1], L=cfg.L, C=F, tl=512, tc=tc, name="conv_act_bwd1")


def _conv_bwd2(dc, w, cfg, name):
    F = cfg.F
    tc = _tile(F, 1408)

    def fn(i, ni, dcv, nxt, wv):
        dcv, nxt = dcv.astype(f32), nxt.astype(f32)
        return [wv[2:3] * dcv + wv[1:2] * _shift_rows_up(dcv, nxt, 1, i, ni) + wv[0:1] * _shift_rows_up(dcv, nxt, 2, i, ni)], []
    return _rowwise(fn, [(dc, "rc"), (dc, "next"), (w, "c3")], [bf16], [], L=cfg.L, C=F, tl=512, tc=tc, name=name)[0]


NSLAB = 8


def _s5_tables(abar_re, abar_im, lam_re, lam_im, step, cfg):
    J = cfg.G // 8
    expo = jnp.array([r + 1 for r in range(8)] + [8 * 2 ** p for p in range(8)], f32)[:, None, None]
    mag = jnp.exp(lam_re * step * expo)
    ang = lam_im * step * expo
    t_re = (mag * jnp.cos(ang)).reshape(16, J, 8 * cfg.P).transpose(1, 0, 2)
    t_im = (mag * jnp.sin(ang)).reshape(16, J, 8 * cfg.P).transpose(1, 0, 2)
    tab = jnp.concatenate([t_re, t_im], axis=-1)
    arow = jnp.concatenate([abar_re.reshape(J, 1, 8 * cfg.P), abar_im.reshape(J, 1, 8 * cfg.P)], axis=-1)
    return arow, tab


def _s5_mats(bbar_re, bbar_im, c_re, c_im, cfg):
    J, P, H = cfg.G // 8, cfg.P, cfg.H
    eye = jnp.eye(8, dtype=f32)

    def bd_in(bb):
        bb = bb.reshape(J, 8, P, H)
        return jnp.einsum("jgph,gk->jghkp", bb, eye).reshape(J, 8 * H, 8 * P)

    def bd_out(cc):
        cc = cc.reshape(J, 8, H, P)
        return jnp.einsum("jghp,gk->jgpkh", cc, eye).reshape(J, 8 * P, 8 * H)

    bmat = jnp.concatenate([bd_in(bbar_re), bd_in(bbar_im)], axis=2).astype(bf16)
    cmat = jnp.concatenate([bd_out(c_re), -bd_out(c_im)], axis=1).astype(bf16)
    return bmat, cmat


def _s5_unmats(dbmat, dcmat, cfg):
    J, P, H = cfg.G // 8, cfg.P, cfg.H
    eye = jnp.eye(8, dtype=f32)
    db = dbmat.reshape(J, 8, H, 2, 8, P)
    db = jnp.einsum("jghckp,gk->cjgph", db, eye).reshape(2, cfg.G, P, H)
    dc = dcmat.reshape(J, 2, 8, P, 8, H)
    dc = jnp.einsum("jcgpkh,gk->cjghp", dc, eye).reshape(2, cfg.G, H, P)
    return db[0], db[1], dc[0], -dc[1]


def _chunk_scan(x_ref, row0, nt, arow_ref, tab_ref, c0, reverse):
    sg = -1.0 if reverse else 1.0
    rows = lax.broadcasted_iota(jnp.int32, (nt, LANES), 0)
    order = list(range(7, -1, -1)) if reverse else list(range(8))

    def ld(k, r):
        return x_ref[k, pl.ds(row0 + r, nt, stride=8), :]

    def tab(row, k):
        return tab_ref[pl.ds(row, 1), pl.ds(k * LANES, LANES)]

    carries = [None] * NSLAB
    for k in range(4):
        ar = arow_ref[:, pl.ds(k * LANES, LANES)]
        ai = sg * arow_ref[:, pl.ds((4 + k) * LANES, LANES)]
        sr, si = ld(k, order[0]), ld(4 + k, order[0])
        for r in order[1:]:
            sr, si = ar * sr - ai * si + ld(k, r), ar * si + ai * sr + ld(4 + k, r)
        if reverse:
            cr = jnp.where(rows == nt - 1, c0[k], pltpu.roll(sr, nt - 1, 0))
            ci = jnp.where(rows == nt - 1, c0[4 + k], pltpu.roll(si, nt - 1, 0))
        else:
            cr = jnp.where(rows == 0, c0[k], pltpu.roll(sr, 1, 0))
            ci = jnp.where(rows == 0, c0[4 + k], pltpu.roll(si, 1, 0))
        d, p = 1, 0
        while d < nt:
            qr, qi = tab(8 + p, k), sg * tab(8 + p, 4 + k)
            if reverse:
                shr, shi, m = pltpu.roll(cr, nt - d, 0), pltpu.roll(ci, nt - d, 0), rows < nt - d
            else:
                shr, shi, m = pltpu.roll(cr, d, 0), pltpu.roll(ci, d, 0), rows >= d
            cr, ci = cr + jnp.where(m, qr * shr - qi * shi, 0.0), ci + jnp.where(m, qr * shi + qi * shr, 0.0)
            d, p = 2 * d, p + 1
        carries[k], carries[4 + k] = cr, ci
        sr, si = cr, ci
        for r in order:
            sr, si = ar * sr - ai * si + ld(k, r), ar * si + ai * sr + ld(4 + k, r)
            x_ref[k, pl.ds(row0 + r, nt, stride=8), :] = sr
            x_ref[4 + k, pl.ds(row0 + r, nt, stride=8), :] = si
    return carries


def _slabs_to_mat(x_ref, row0, n):
    return jnp.concatenate([x_ref[k, pl.ds(row0, n), :] for k in range(NSLAB)], axis=1)


def _mat_to_slabs(x_ref, row0, n, m):
    for k in range(NSLAB):
        x_ref[k, pl.ds(row0, n), :] = m[:, k * LANES:(k + 1) * LANES]


def _s5_fwd(u, bmat, cmat, drow, arow, tab, cfg, rider=None):
    L, D, Tc = cfg.L, cfg.D, cfg.TC
    J, NC, nt = cfg.G // 8, L // Tc, Tc // 8
    W = NSLAB * LANES

    def body(u_ref, b_ref, c_ref, d_ref, a_ref, t_ref, y_ref, g_ref, cin_ref, x_ref, st_ref):
        c = pl.program_id(1)

        @pl.when(c == 0)
        def _():
            st_ref[...] = jnp.zeros_like(st_ref)

        cin_ref[...] = st_ref[...]
        ub = u_ref[...]
        _mat_to_slabs(x_ref, 0, Tc, jnp.dot(ub.astype(bf16), b_ref[...], preferred_element_type=f32))
        c0 = [st_ref[:, pl.ds(k * LANES, LANES)] for k in range(NSLAB)]
        _chunk_scan(x_ref, 0, nt, a_ref, t_ref, c0, False)
        for k in range(NSLAB):
            st_ref[:, pl.ds(k * LANES, LANES)] = x_ref[k, pl.ds(Tc - 1, 1), :]
        s = _slabs_to_mat(x_ref, 0, Tc).astype(bf16)
        y = jnp.dot(s, c_ref[...], preferred_element_type=f32) + d_ref[...] * ub
        y_ref[...] = y
        g_ref[...] = _gelu(y).astype(bf16)

    outs, extra = _host_call(
        body, name="s5_fwd", grid=(J, NC), rider=rider, args=(u, bmat, cmat, drow, arow, tab),
        in_specs=[pl.BlockSpec((Tc, LANES), lambda j, c: (c, j)),
                  pl.BlockSpec((None, LANES, W), lambda j, c: (j, 0, 0)),
                  pl.BlockSpec((None, W, LANES), lambda j, c: (j, 0, 0)),
                  pl.BlockSpec((1, LANES), lambda j, c: (0, j)),
                  pl.BlockSpec((None, 1, W), lambda j, c: (j, 0, 0)),
                  pl.BlockSpec((None, 16, W), lambda j, c: (j, 0, 0))],
        out_specs=[pl.BlockSpec((Tc, LANES), lambda j, c: (c, j)),
                   pl.BlockSpec((Tc, LANES), lambda j, c: (c, j)),
                   pl.BlockSpec((None, None, 1, W), lambda j, c: (j, c, 0, 0))],
        out_shape=[jax.ShapeDtypeStruct((L, D), f32), jax.ShapeDtypeStruct((L, D), bf16),
                   jax.ShapeDtypeStruct((J, NC, 1, W), f32)],
        scratch_shapes=[pltpu.VMEM((NSLAB, Tc, LANES), f32), pltpu.VMEM((1, W), f32)])
    return (*outs, extra)


def _s5_bwd(u, dy, cin, bmat, cmat, drow, arow, tab, cfg, rider=None):
    L, D, Tc = cfg.L, cfg.D, cfg.TC
    J, NC, nt = cfg.G // 8, L // Tc, Tc // 8
    W = NSLAB * LANES
    PAD = 0

    def body(u_ref, dy_ref, cin_ref, b_ref, c_ref, d_ref, a_ref, t_ref,
             du_ref, db_ref, dc_ref, da_ref, dd_ref, s_ref, g_ref, gst_ref):
        c = pl.program_id(1)

        @pl.when(c == 0)
        def _():
            gst_ref[...] = jnp.zeros_like(gst_ref)
            db_ref[...] = jnp.zeros_like(db_ref)
            dc_ref[...] = jnp.zeros_like(dc_ref)
            da_ref[...] = jnp.zeros_like(da_ref)
            dd_ref[...] = jnp.zeros_like(dd_ref)

        ub, dyb = u_ref[...], dy_ref[...]
        ub16, dy16 = ub.astype(bf16), dyb.astype(bf16)
        _mat_to_slabs(s_ref, PAD, Tc, jnp.dot(ub16, b_ref[...], preferred_element_type=f32))
        c0 = [cin_ref[:, pl.ds(k * LANES, LANES)] for k in range(NSLAB)]
        tile_in = _chunk_scan(s_ref, PAD, nt, a_ref, t_ref, c0, False)
        _mat_to_slabs(g_ref, 0, Tc, lax.dot_general(dy16, c_ref[...], _DIMS["nt"], preferred_element_type=f32))
        g0 = [gst_ref[:, pl.ds(k * LANES, LANES)] for k in range(NSLAB)]
        _chunk_scan(g_ref, 0, nt, a_ref, t_ref, g0, True)
        for k in range(NSLAB):
            gst_ref[:, pl.ds(k * LANES, LANES)] = g_ref[k, pl.ds(0, 1), :]
        for k in range(4):
            acc_r = jnp.zeros((nt, LANES), f32)
            acc_i = jnp.zeros((nt, LANES), f32)
            for r in range(8):
                gr = g_ref[k, pl.ds(r, nt, stride=8), :]
                gi = g_ref[4 + k, pl.ds(r, nt, stride=8), :]
                if r == 0:
                    pr, pi = tile_in[k], tile_in[4 + k]
                else:
                    pr = s_ref[k, pl.ds(PAD + r - 1, nt, stride=8), :]
                    pi = s_ref[4 + k, pl.ds(PAD + r - 1, nt, stride=8), :]
                acc_r += gr * pr + gi * pi
                acc_i += gi * pr - gr * pi
            da_ref[:, pl.ds(k * LANES, LANES)] += _colsum(acc_r)
            da_ref[:, pl.ds((4 + k) * LANES, LANES)] += _colsum(acc_i)
        gm = _slabs_to_mat(g_ref, 0, Tc).astype(bf16)
        sm = _slabs_to_mat(s_ref, PAD, Tc).astype(bf16)
        du = lax.dot_general(gm, b_ref[...], _DIMS["nt"], preferred_element_type=f32) + d_ref[...] * dyb
        du_ref[...] = du.astype(bf16)
        db_ref[...] += lax.dot_general(ub16, gm, _DIMS["tn"], preferred_element_type=f32)
        dc_ref[...] += lax.dot_general(sm, dy16, _DIMS["tn"], preferred_element_type=f32)
        dd_ref[...] += _colsum(dyb * ub)

    rc = lambda j, c: (NC - 1 - c, j)
    outs, extra = _host_call(
        body, name="s5_bwd", grid=(J, NC), rider=rider, args=(u, dy, cin, bmat, cmat, drow, arow, tab),
        in_specs=[pl.BlockSpec((Tc, LANES), rc), pl.BlockSpec((Tc, LANES), rc),
                  pl.BlockSpec((None, None, 1, W), lambda j, c: (j, NC - 1 - c, 0, 0)),
                  pl.BlockSpec((None, LANES, W), lambda j, c: (j, 0, 0)),
                  pl.BlockSpec((None, W, LANES), lambda j, c: (j, 0, 0)),
                  pl.BlockSpec((1, LANES), lambda j, c: (0, j)),
                  pl.BlockSpec((None, 1, W), lambda j, c: (j, 0, 0)),
                  pl.BlockSpec((None, 16, W), lambda j, c: (j, 0, 0))],
        out_specs=[pl.BlockSpec((Tc, LANES), rc),
                   pl.BlockSpec((None, LANES, W), lambda j, c: (j, 0, 0)),
                   pl.BlockSpec((None, W, LANES), lambda j, c: (j, 0, 0)),
                   pl.BlockSpec((None, 1, W), lambda j, c: (j, 0, 0)),
                   pl.BlockSpec((1, LANES), lambda j, c: (0, j))],
        out_shape=[jax.ShapeDtypeStruct((L, D), bf16), jax.ShapeDtypeStruct((J, LANES, W), f32),
                   jax.ShapeDtypeStruct((J, W, LANES), f32), jax.ShapeDtypeStruct((J, 1, W), f32),
                   jax.ShapeDtypeStruct((1, D), f32)],
        scratch_shapes=[pltpu.VMEM((NSLAB, Tc + PAD, LANES), f32), pltpu.VMEM((NSLAB, Tc, LANES), f32),
                        pltpu.VMEM((1, W), f32)])
    return (*outs, extra)


NEG = -1e30


def _attn_logits(q_ref, k_ref, fq_ref, fk_ref, qi, ki, bq, scale):
    s = lax.dot_general(q_ref[...], k_ref[...], _DIMS["nt"], preferred_element_type=f32) * scale
    s = s + fq_ref[...] - fk_ref[...]
    rows = qi * bq + lax.broadcasted_iota(jnp.int32, s.shape, 0)
    cols = ki * bq + lax.broadcasted_iota(jnp.int32, s.shape, 1)
    return s, cols <= rows


def _attn_fwd(q, kv, fq, fk, cfg):
    L, D, NH, DH, B = cfg.L, cfg.D, cfg.NH, cfg.DH, cfg.BQ
    nq = L // B
    scale = DH ** -0.5

    def body(q_ref, k_ref, v_ref, fq_ref, fk_ref, o_ref, lse_ref, m_ref, l_ref, acc_ref):
        qi, ki = pl.program_id(1), pl.program_id(2)

        @pl.when(ki == 0)
        def _():
            m_ref[...] = jnp.full_like(m_ref, NEG)
            l_ref[...] = jnp.zeros_like(l_ref)
            acc_ref[...] = jnp.zeros_like(acc_ref)

        @pl.when(ki <= qi)
        def _():
            s, mask = _attn_logits(q_ref, k_ref, fq_ref, fk_ref, qi, ki, B, scale)
            s = jnp.where(mask, s, NEG)
            m_prev = m_ref[...]
            m_new = jnp.maximum(m_prev, jnp.max(s, axis=1, keepdims=True))
            alpha = jnp.exp(m_prev - m_new)
            p = jnp.exp(s - m_new)
            l_ref[...] = alpha * l_ref[...] + jnp.sum(p, axis=1, keepdims=True)
            acc_ref[...] = alpha * acc_ref[...] + jnp.dot(p.astype(bf16), v_ref[...], preferred_element_type=f32)
            m_ref[...] = m_new

        @pl.when(ki == qi)
        def _():
            o_ref[...] = (acc_ref[...] / l_ref[...]).astype(o_ref.dtype)
            lse_ref[...] = m_ref[...] + jnp.log(l_ref[...])

    kmap = lambda h, qi, ki: (jnp.minimum(ki, qi), h)
    vmap_ = lambda h, qi, ki: (jnp.minimum(ki, qi), NH + h)
    return pl.pallas_call(
        body, name="attn_fwd", grid=(NH, nq, nq),
        in_specs=[pl.BlockSpec((B, DH), lambda h, qi, ki: (qi, h)),
                  pl.BlockSpec((B, DH), kmap), pl.BlockSpec((B, DH), vmap_),
                  pl.BlockSpec((None, B, 1), lambda h, qi, ki: (h, qi, 0)),
                  pl.BlockSpec((None, 1, B), lambda h, qi, ki: (h, 0, jnp.minimum(ki, qi)))],
        out_specs=[pl.BlockSpec((B, DH), lambda h, qi, ki: (qi, h)),
                   pl.BlockSpec((None, B, 1), lambda h, qi, ki: (h, qi, 0))],
        out_shape=[jax.ShapeDtypeStruct((L, D), bf16), jax.ShapeDtypeStruct((NH, L, 1), f32)],
        scratch_shapes=[pltpu.VMEM((B, 1), f32), pltpu.VMEM((B, 1), f32), pltpu.VMEM((B, DH), f32)],
        compiler_params=_cp(("parallel", "parallel", "arbitrary")),
    )(q, kv, kv, fq, fk)


def _attn_bwd_dq(q, kv, do, o, lse, fq, fk, cfg):
    L, D, NH, DH, B = cfg.L, cfg.D, cfg.NH, cfg.DH, cfg.BQ
    nq = L // B
    scale = DH ** -0.5

    def body(q_ref, k_ref, v_ref, do_ref, o_ref, lse_ref, fq_ref, fk_ref, dq_ref, dfq_ref, acc_ref, df_ref, dl_ref):
        qi, ki = pl.program_id(1), pl.program_id(2)

        @pl.when(ki == 0)
        def _():
            dl_ref[...] = jnp.sum(do_ref[...].astype(f32) * o_ref[...].astype(f32), axis=1, keepdims=True)
            acc_ref[...] = jnp.zeros_like(acc_ref)
            df_ref[...] = jnp.zeros_like(df_ref)

        @pl.when(ki <= qi)
        def _():
            s, mask = _attn_logits(q_ref, k_ref, fq_ref, fk_ref, qi, ki, B, scale)
            p = jnp.where(mask, jnp.exp(s - lse_ref[...]), 0.0)
            dp = lax.dot_general(do_ref[...], v_ref[...], _DIMS["nt"], preferred_element_type=f32)
            ds = p * (dp - dl_ref[...])
            df_ref[...] += jnp.sum(ds, axis=1, keepdims=True)
            acc_ref[...] += jnp.dot(ds.astype(bf16), k_ref[...], preferred_element_type=f32)

        @pl.when(ki == qi)
        def _():
            dq_ref[...] = (acc_ref[...] * scale).astype(dq_ref.dtype)
            dfq_ref[...] = df_ref[...]

    qmap = lambda h, qi, ki: (qi, h)
    return pl.pallas_call(
        body, name="attn_bwd_dq", grid=(NH, nq, nq),
        in_specs=[pl.BlockSpec((B, DH), qmap),
                  pl.BlockSpec((B, DH), lambda h, qi, ki: (jnp.minimum(ki, qi), h)),
                  pl.BlockSpec((B, DH), lambda h, qi, ki: (jnp.minimum(ki, qi), NH + h)),
                  pl.BlockSpec((B, DH), qmap), pl.BlockSpec((B, DH), qmap),
                  pl.BlockSpec((None, B, 1), lambda h, qi, ki: (h, qi, 0)),
                  pl.BlockSpec((None, B, 1), lambda h, qi, ki: (h, qi, 0)),
                  pl.BlockSpec((None, 1, B), lambda h, qi, ki: (h, 0, jnp.minimum(ki, qi)))],
        out_specs=[pl.BlockSpec((B, DH), qmap), pl.BlockSpec((None, B, 1), lambda h, qi, ki: (h, qi, 0))],
        out_shape=[jax.ShapeDtypeStruct((L, D), bf16), jax.ShapeDtypeStruct((NH, L, 1), f32)],
        scratch_shapes=[pltpu.VMEM((B, DH), f32), pltpu.VMEM((B, 1), f32), pltpu.VMEM((B, 1), f32)],
        compiler_params=_cp(("parallel", "parallel", "arbitrary")),
    )(q, kv, kv, do, o, lse, fq, fk)


def _attn_bwd_dkv(q, kv, do, o, lse, fq, fk, cfg):
    L, D, NH, DH, B = cfg.L, cfg.D, cfg.NH, cfg.DH, cfg.BQ
    nq = L // B
    scale = DH ** -0.5

    def body(q_ref, k_ref, v_ref, do_ref, o_ref, lse_ref, fq_ref, fk_ref, dk_ref, dv_ref, dfk_ref, dka_ref, dva_ref, dfa_ref):
        ki, qi = pl.program_id(1), pl.program_id(2)

        @pl.when(qi == 0)
        def _():
            dka_ref[...] = jnp.zeros_like(dka_ref)
            dva_ref[...] = jnp.zeros_like(dva_ref)
            dfa_ref[...] = jnp.zeros_like(dfa_ref)

        @pl.when(qi >= ki)
        def _():
            do = do_ref[...]
            delta = jnp.sum(do.astype(f32) * o_ref[...].astype(f32), axis=1, keepdims=True)
            s, mask = _attn_logits(q_ref, k_ref, fq_ref, fk_ref, qi, ki, B, scale)
            p = jnp.where(mask, jnp.exp(s - lse_ref[...]), 0.0)
            dva_ref[...] += lax.dot_general(p.astype(bf16), do, _DIMS["tn"], preferred_element_type=f32)
            dp = lax.dot_general(do, v_ref[...], _DIMS["nt"], preferred_element_type=f32)
            ds = p * (dp - delta)
            dka_ref[...] += lax.dot_general(ds.astype(bf16), q_ref[...], _DIMS["tn"], preferred_element_type=f32)
            dfa_ref[...] -= jnp.sum(ds, axis=0, keepdims=True)

        @pl.when(qi == nq - 1)
        def _():
            dk_ref[...] = (dka_ref[...] * scale).astype(dk_ref.dtype)
            dv_ref[...] = dva_ref[...].astype(dv_ref.dtype)
            dfk_ref[...] = dfa_ref[...]

    qmap = lambda h, ki, qi: (jnp.maximum(qi, ki), h)
    fqmap = lambda h, ki, qi: (h, jnp.maximum(qi, ki), 0)
    return pl.pallas_call(
        body, name="attn_bwd_dkv", grid=(NH, nq, nq),
        in_specs=[pl.BlockSpec((B, DH), qmap),
                  pl.BlockSpec((B, DH), lambda h, ki, qi: (ki, h)),
                  pl.BlockSpec((B, DH), lambda h, ki, qi: (ki, NH + h)),
                  pl.BlockSpec((B, DH), qmap), pl.BlockSpec((B, DH), qmap),
                  pl.BlockSpec((None, B, 1), fqmap), pl.BlockSpec((None, B, 1), fqmap),
                  pl.BlockSpec((None, 1, B), lambda h, ki, qi: (h, 0, ki))],
        out_specs=[pl.BlockSpec((B, DH), lambda h, ki, qi: (ki, h)), pl.BlockSpec((B, DH), lambda h, ki, qi: (ki, h)),
                   pl.BlockSpec((None, 1, B), lambda h, ki, qi: (h, 0, ki))],
        out_shape=[jax.ShapeDtypeStruct((L, D), bf16), jax.ShapeDtypeStruct((L, D), bf16),
                   jax.ShapeDtypeStruct((NH, 1, L), f32)],
        scratch_shapes=[pltpu.VMEM((B, DH), f32), pltpu.VMEM((B, DH), f32), pltpu.VMEM((1, B), f32)],
        compiler_params=_cp(("parallel", "parallel", "arbitrary")),
    )(q, kv, kv, do, o, lse, fq, fk)


def _tri_tables(nq, by_key):
    pairs = [(qi, ki) for ki in range(nq) for qi in range(ki, nq)] if by_key else \
            [(qi, ki) for qi in range(nq) for ki in range(qi + 1)]
    return jnp.array([p[0] for p in pairs], jnp.int32), jnp.array([p[1] for p in pairs], jnp.int32)


def _tri_call(body, name, cfg, by_key, in_specs, out_specs, out_shape, scratch_shapes, args, rider=None):
    nq = cfg.L // cfg.BQ
    outs, extra = _host_call(body, name=name, grid=(cfg.NH, nq * (nq + 1) // 2), in_specs=in_specs, out_specs=out_specs,
                             out_shape=out_shape, scratch_shapes=scratch_shapes, args=args,
                             prefetch=_tri_tables(nq, by_key), rider=rider)
    return (*outs, extra)


def _ta_fwd(q, kv, fk, cfg, rider=None):
    L, D, NH, DH, B = cfg.L, cfg.D, cfg.NH, cfg.DH, cfg.BQ
    scale = DH ** -0.5

    def body(qt_ref, kt_ref, q_ref, k_ref, v_ref, fk_ref, o_ref, lse_ref, m_ref, acc_ref, a_ref, s_ref, p_ref):
        pid = pl.program_id(1)
        qi, ki = qt_ref[pid], kt_ref[pid]

        @pl.when(ki == 0)
        def _():
            m_ref[...] = jnp.full_like(m_ref, NEG)
            acc_ref[...] = jnp.zeros_like(acc_ref)

        def compute(masked):
            s_ref[...] = lax.dot_general(q_ref[...], k_ref[...], _DIMS["nt"], preferred_element_type=f32)
            fkv = fk_ref[...]

            def strip(rows, row0, c):
                t = s_ref[rows, :] * scale - fkv
                if masked:
                    t = jnp.where(_fa_mask(row0, t.shape), t, NEG)
                m_prev = m_ref[rows, :]
                m_new = jnp.maximum(m_prev, jnp.max(t, axis=1, keepdims=True))
                m_ref[rows, :] = m_new
                a_ref[rows, :] = jnp.exp(m_prev - m_new)
                p_ref[rows, :] = jnp.exp(t - m_new).astype(bf16)
                return c
            _fa_strips(B, strip, 0)
            v1 = jnp.concatenate([v_ref[...], jnp.ones((B, DH), bf16)], axis=1)
            acc_ref[...] = a_ref[...] * acc_ref[...] + jnp.dot(p_ref[...], v1, preferred_element_type=f32)

        @pl.when(ki < qi)
        def _():
            compute(False)

        @pl.when(ki == qi)
        def _():
            compute(True)
            l = acc_ref[:, DH:]
            o_ref[...] = (acc_ref[:, :DH] / l).astype(o_ref.dtype)
            lse_ref[...] = m_ref[...] + jnp.log(l[:, :1])

    col = pltpu.VMEM((B, 1), f32)
    return _tri_call(
        body, "attn_fwd", cfg, False,
        [pl.BlockSpec((B, DH), lambda h, p, qt, kt: (qt[p], h)),
         pl.BlockSpec((B, DH), lambda h, p, qt, kt: (kt[p], h)),
         pl.BlockSpec((B, DH), lambda h, p, qt, kt: (kt[p], NH + h)),
         pl.BlockSpec((None, 1, B), lambda h, p, qt, kt: (h, 0, kt[p]))],
        [pl.BlockSpec((B, DH), lambda h, p, qt, kt: (qt[p], h)),
         pl.BlockSpec((None, B, 1), lambda h, p, qt, kt: (h, qt[p], 0))],
        [jax.ShapeDtypeStruct((L, D), bf16), jax.ShapeDtypeStruct((NH, L, 1), f32)],
        [col, pltpu.VMEM((B, 2 * DH), f32), col, pltpu.VMEM((B, B), f32), pltpu.VMEM((B, B), bf16)],
        (q, kv, kv, fk), rider)


def _ta_bwd_dq(q, kv, do, o, lse, fk, cfg, rider=None):
    L, D, NH, DH, B = cfg.L, cfg.D, cfg.NH, cfg.DH, cfg.BQ
    scale = DH ** -0.5

    def body(qt_ref, kt_ref, q_ref, k_ref, v_ref, do_ref, o_ref, lse_ref, fk_ref, dq_ref, dfq_ref, dl_ref,
             acc_ref, s_ref, dp_ref, ds_ref):
        pid = pl.program_id(1)
        qi, ki = qt_ref[pid], kt_ref[pid]

        @pl.when(ki == 0)
        def _():
            dl_ref[...] = jnp.sum(do_ref[...].astype(f32) * o_ref[...].astype(f32), axis=1, keepdims=True)
            acc_ref[...] = jnp.zeros_like(acc_ref)

        def compute(masked):
            s_ref[...] = lax.dot_general(q_ref[...], k_ref[...], _DIMS["nt"], preferred_element_type=f32)
            dp_ref[...] = lax.dot_general(do_ref[...], v_ref[...], _DIMS["nt"], preferred_element_type=f32)
            fkv = fk_ref[...]

            def strip(rows, row0, c):
                p = jnp.exp(s_ref[rows, :] * scale - fkv - lse_ref[rows, :])
                if masked:
                    p = jnp.where(_fa_mask(row0, p.shape), p, 0.0)
                ds_ref[rows, :] = (p * (dp_ref[rows, :] - dl_ref[rows, :])).astype(bf16)
                return c
            _fa_strips(B, strip, 0)
            k1 = jnp.concatenate([k_ref[...], jnp.ones((B, DH), bf16)], axis=1)
            acc_ref[...] += jnp.dot(ds_ref[...], k1, preferred_element_type=f32)

        @pl.when(ki < qi)
        def _():
            compute(False)

        @pl.when(ki == qi)
        def _():
            compute(True)
            dq_ref[...] = (acc_ref[:, :DH] * scale).astype(dq_ref.dtype)
            dfq_ref[...] = acc_ref[:, DH:DH + 1]

    qmap = lambda h, p, qt, kt: (qt[p], h)
    cmap = lambda h, p, qt, kt: (h, qt[p], 0)
    return _tri_call(
        body, "attn_bwd_dq", cfg, False,
        [pl.BlockSpec((B, DH), qmap),
         pl.BlockSpec((B, DH), lambda h, p, qt, kt: (kt[p], h)),
         pl.BlockSpec((B, DH), lambda h, p, qt, kt: (kt[p], NH + h)),
         pl.BlockSpec((B, DH), qmap), pl.BlockSpec((B, DH), qmap),
         pl.BlockSpec((None, B, 1), cmap),
         pl.BlockSpec((None, 1, B), lambda h, p, qt, kt: (h, 0, kt[p]))],
        [pl.BlockSpec((B, DH), qmap), pl.BlockSpec((None, B, 1), cmap), pl.BlockSpec((None, B, 1), cmap)],
        [jax.ShapeDtypeStruct((L, D), bf16), jax.ShapeDtypeStruct((NH, L, 1), f32), jax.ShapeDtypeStruct((NH, L, 1), f32)],
        [pltpu.VMEM((B, 2 * DH), f32), pltpu.VMEM((B, B), f32), pltpu.VMEM((B, B), f32), pltpu.VMEM((B, B), bf16)],
        (q, kv, kv, do, o, lse, fk), rider)


def _ta_bwd_dkv(q, kv, do, delta, lse, fk, cfg):
    L, D, NH, DH, B = cfg.L, cfg.D, cfg.NH, cfg.DH, cfg.BQ
    nq = L // B
    scale = DH ** -0.5

    def body(qt_ref, kt_ref, q_ref, k_ref, v_ref, do_ref, dl_ref, lse_ref, fk_ref, dk_ref, dv_ref, dfk_ref,
             dka_ref, dva_ref, s_ref, dp_ref, p_ref, ds_ref):
        pid = pl.program_id(1)
        qi, ki = qt_ref[pid], kt_ref[pid]

        @pl.when(qi == ki)
        def _():
            dka_ref[...] = jnp.zeros_like(dka_ref)
            dva_ref[...] = jnp.zeros_like(dva_ref)

        def compute(masked):
            s_ref[...] = lax.dot_general(q_ref[...], k_ref[...], _DIMS["nt"], preferred_element_type=f32)
            dp_ref[...] = lax.dot_general(do_ref[...], v_ref[...], _DIMS["nt"], preferred_element_type=f32)
            fkv = fk_ref[...]

            def strip(rows, row0, c):
                p = jnp.exp(s_ref[rows, :] * scale - fkv - lse_ref[rows, :])
                if masked:
                    p = jnp.where(_fa_mask(row0, p.shape), p, 0.0)
                p_ref[rows, :] = p.astype(bf16)
                ds_ref[rows, :] = (p * (dp_ref[rows, :] - dl_ref[rows, :])).astype(bf16)
                return c
            _fa_strips(B, strip, 0)
            q1 = jnp.concatenate([q_ref[...], jnp.ones((B, DH), bf16)], axis=1)
            dva_ref[...] += lax.dot_general(p_ref[...], do_ref[...], _DIMS["tn"], preferred_element_type=f32)
            dka_ref[...] += lax.dot_general(ds_ref[...], q1, _DIMS["tn"], preferred_element_type=f32)

        @pl.when(qi == ki)
        def _():
            compute(True)

        @pl.when(qi > ki)
        def _():
            compute(False)

        @pl.when(qi == nq - 1)
        def _():
            dk_ref[...] = (dka_ref[:, :DH] * scale).astype(dk_ref.dtype)
            dv_ref[...] = dva_ref[...].astype(dv_ref.dtype)
            dfk_ref[...] = -dka_ref[:, DH:DH + 1]

    qmap = lambda h, p, qt, kt: (qt[p], h)
    cmap = lambda h, p, qt, kt: (h, qt[p], 0)
    kmap = lambda h, p, qt, kt: (kt[p], h)
    return _tri_call(
        body, "attn_bwd_dkv", cfg, True,
        [pl.BlockSpec((B, DH), qmap), pl.BlockSpec((B, DH), kmap),
         pl.BlockSpec((B, DH), lambda h, p, qt, kt: (kt[p], NH + h)),
         pl.BlockSpec((B, DH), qmap), pl.BlockSpec((None, B, 1), cmap), pl.BlockSpec((None, B, 1), cmap),
         pl.BlockSpec((None, 1, B), lambda h, p, qt, kt: (h, 0, kt[p]))],
        [pl.BlockSpec((B, DH), kmap), pl.BlockSpec((B, DH), kmap),
         pl.BlockSpec((None, B, 1), lambda h, p, qt, kt: (h, kt[p], 0))],
        [jax.ShapeDtypeStruct((L, D), bf16), jax.ShapeDtypeStruct((L, D), bf16), jax.ShapeDtypeStruct((NH, L, 1), f32)],
        [pltpu.VMEM((B, 2 * DH), f32), pltpu.VMEM((B, DH), f32), pltpu.VMEM((B, B), f32), pltpu.VMEM((B, B), f32),
         pltpu.VMEM((B, B), bf16), pltpu.VMEM((B, B), bf16)],
        (q, kv, kv, do, delta, lse, fk))


STRIP = 32


def _fa_strips(nrows, fn, init):
    return lax.fori_loop(0, nrows // STRIP, lambda r, c: fn(pl.ds(pl.multiple_of(r * STRIP, STRIP), STRIP), r * STRIP, c),
                         init, unroll=True)


def _fa_mask(row0, shape):
    rows = row0 + lax.broadcasted_iota(jnp.int32, shape, 0)
    cols = lax.broadcasted_iota(jnp.int32, shape, 1)
    return cols <= rows


def _fa_fwd(q, kv, fk, cfg):
    L, D, NH, DH, B = cfg.L, cfg.D, cfg.NH, cfg.DH, cfg.BQ
    nq = L // B
    scale = DH ** -0.5

    def body(q_ref, k_ref, v_ref, fk_ref, o_ref, lse_ref, m_ref, l_ref, acc_ref, a_ref, s_ref, p_ref):
        qi, ki = pl.program_id(1), pl.program_id(2)

        @pl.when(ki == 0)
        def _():
            m_ref[...] = jnp.full_like(m_ref, NEG)
            l_ref[...] = jnp.zeros_like(l_ref)
            acc_ref[...] = jnp.zeros_like(acc_ref)

        def compute(masked):
            s_ref[...] = lax.dot_general(q_ref[...], k_ref[...], _DIMS["nt"], preferred_element_type=f32)
            fkv = fk_ref[...]

            def strip(rows, row0, c):
                t = s_ref[rows, :] * scale - fkv
                if masked:
                    t = jnp.where(_fa_mask(row0, t.shape), t, NEG)
                m_prev = m_ref[rows, :]
                m_new = jnp.maximum(m_prev, jnp.max(t, axis=1, keepdims=True))
                p = jnp.exp(t - m_new)
                alpha = jnp.exp(m_prev - m_new)
                l_ref[rows, :] = alpha * l_ref[rows, :] + jnp.sum(p, axis=1, keepdims=True)
                m_ref[rows, :] = m_new
                a_ref[rows, :] = alpha
                p_ref[rows, :] = p.astype(bf16)
                return c
            _fa_strips(B, strip, 0)
            acc_ref[...] = a_ref[...] * acc_ref[...] + jnp.dot(p_ref[...], v_ref[...], preferred_element_type=f32)

        @pl.when(ki < qi)
        def _():
            compute(False)

        @pl.when(ki == qi)
        def _():
            compute(True)
            o_ref[...] = (acc_ref[...] / l_ref[...]).astype(o_ref.dtype)
            lse_ref[...] = m_ref[...] + jnp.log(l_ref[...])

    col = pltpu.VMEM((B, 1), f32)
    return pl.pallas_call(
        body, name="attn_fwd", grid=(NH, nq, nq),
        in_specs=[pl.BlockSpec((B, DH), lambda h, qi, ki: (qi, h)),
                  pl.BlockSpec((B, DH), lambda h, qi, ki: (jnp.minimum(ki, qi), h)),
                  pl.BlockSpec((B, DH), lambda h, qi, ki: (jnp.minimum(ki, qi), NH + h)),
                  pl.BlockSpec((None, 1, B), lambda h, qi, ki: (h, 0, jnp.minimum(ki, qi)))],
        out_specs=[pl.BlockSpec((B, DH), lambda h, qi, ki: (qi, h)),
                   pl.BlockSpec((None, B, 1), lambda h, qi, ki: (h, qi, 0))],
        out_shape=[jax.ShapeDtypeStruct((L, D), bf16), jax.ShapeDtypeStruct((NH, L, 1), f32)],
        scratch_shapes=[col, col, pltpu.VMEM((B, DH), f32), col, pltpu.VMEM((B, B), f32), pltpu.VMEM((B, B), bf16)],
        compiler_params=_cp(("parallel", "parallel", "arbitrary")),
    )(q, kv, kv, fk)


def _fa_bwd_dq(q, kv, do, o, lse, fk, cfg):
    L, D, NH, DH, B = cfg.L, cfg.D, cfg.NH, cfg.DH, cfg.BQ
    nq = L // B
    scale = DH ** -0.5

    def body(q_ref, k_ref, v_ref, do_ref, o_ref, lse_ref, fk_ref, dq_ref, dfq_ref, dl_ref, acc_ref, df_ref, s_ref, dp_ref, ds_ref):
        qi, ki = pl.program_id(1), pl.program_id(2)

        @pl.when(ki == 0)
        def _():
            dl_ref[...] = jnp.sum(do_ref[...].astype(f32) * o_ref[...].astype(f32), axis=1, keepdims=True)
            acc_ref[...] = jnp.zeros_like(acc_ref)
            df_ref[...] = jnp.zeros_like(df_ref)

        def compute(masked):
            s_ref[...] = lax.dot_general(q_ref[...], k_ref[...], _DIMS["nt"], preferred_element_type=f32)
            dp_ref[...] = lax.dot_general(do_ref[...], v_ref[...], _DIMS["nt"], preferred_element_type=f32)
            fkv = fk_ref[...]

            def strip(rows, row0, c):
                p = jnp.exp(s_ref[rows, :] * scale - fkv - lse_ref[rows, :])
                if masked:
                    p = jnp.where(_fa_mask(row0, p.shape), p, 0.0)
                ds = p * (dp_ref[rows, :] - dl_ref[rows, :])
                df_ref[rows, :] += jnp.sum(ds, axis=1, keepdims=True)
                ds_ref[rows, :] = ds.astype(bf16)
                return c
            _fa_strips(B, strip, 0)
            acc_ref[...] += jnp.dot(ds_ref[...], k_ref[...], preferred_element_type=f32)

        @pl.when(ki < qi)
        def _():
            compute(False)

        @pl.when(ki == qi)
        def _():
            compute(True)
            dq_ref[...] = (acc_ref[...] * scale).astype(dq_ref.dtype)
            dfq_ref[...] = df_ref[...]

    qmap = lambda h, qi, ki: (qi, h)
    cmap = lambda h, qi, ki: (h, qi, 0)
    return pl.pallas_call(
        body, name="attn_bwd_dq", grid=(NH, nq, nq),
        in_specs=[pl.BlockSpec((B, DH), qmap),
                  pl.BlockSpec((B, DH), lambda h, qi, ki: (jnp.minimum(ki, qi), h)),
                  pl.BlockSpec((B, DH), lambda h, qi, ki: (jnp.minimum(ki, qi), NH + h)),
                  pl.BlockSpec((B, DH), qmap), pl.BlockSpec((B, DH), qmap),
                  pl.BlockSpec((None, B, 1), cmap),
                  pl.BlockSpec((None, 1, B), lambda h, qi, ki: (h, 0, jnp.minimum(ki, qi)))],
        out_specs=[pl.BlockSpec((B, DH), qmap), pl.BlockSpec((None, B, 1), cmap), pl.BlockSpec((None, B, 1), cmap)],
        out_shape=[jax.ShapeDtypeStruct((L, D), bf16), jax.ShapeDtypeStruct((NH, L, 1), f32),
                   jax.ShapeDtypeStruct((NH, L, 1), f32)],
        scratch_shapes=[pltpu.VMEM((B, DH), f32), pltpu.VMEM((B, 1), f32), pltpu.VMEM((B, B), f32),
                        pltpu.VMEM((B, B), f32), pltpu.VMEM((B, B), bf16)],
        compiler_params=_cp(("parallel", "parallel", "arbitrary")),
    )(q, kv, kv, do, o, lse, fk)


def _fa_bwd_dkv(q, kv, do, delta, lse, fk, cfg):
    L, D, NH, DH, B = cfg.L, cfg.D, cfg.NH, cfg.DH, cfg.BQ
    nq = L // B
    scale = DH ** -0.5

    def body(q_ref, k_ref, v_ref, do_ref, dl_ref, lse_ref, fk_ref, dk_ref, dv_ref, dfk_ref,
             dka_ref, dva_ref, dfa_ref, s_ref, dp_ref, p_ref, ds_ref):
        ki, qi = pl.program_id(1), pl.program_id(2)

        @pl.when(qi == 0)
        def _():
            dka_ref[...] = jnp.zeros_like(dka_ref)
            dva_ref[...] = jnp.zeros_like(dva_ref)
            dfa_ref[...] = jnp.zeros_like(dfa_ref)

        def compute(masked):
            s_ref[...] = lax.dot_general(q_ref[...], k_ref[...], _DIMS["nt"], preferred_element_type=f32)
            dp_ref[...] = lax.dot_general(do_ref[...], v_ref[...], _DIMS["nt"], preferred_element_type=f32)
            fkv = fk_ref[...]

            def strip(rows, row0, cs):
                p = jnp.exp(s_ref[rows, :] * scale - fkv - lse_ref[rows, :])
                if masked:
                    p = jnp.where(_fa_mask(row0, p.shape), p, 0.0)
                ds = p * (dp_ref[rows, :] - dl_ref[rows, :])
                p_ref[rows, :] = p.astype(bf16)
                ds_ref[rows, :] = ds.astype(bf16)
                return cs + ds
            cs = _fa_strips(B, strip, jnp.zeros((STRIP, B), f32))
            dva_ref[...] += lax.dot_general(p_ref[...], do_ref[...], _DIMS["tn"], preferred_element_type=f32)
            dka_ref[...] += lax.dot_general(ds_ref[...], q_ref[...], _DIMS["tn"], preferred_element_type=f32)
            dfa_ref[...] -= jnp.sum(cs, axis=0, keepdims=True)

        @pl.when(qi == ki)
        def _():
            compute(True)

        @pl.when(qi > ki)
        def _():
            compute(False)

        @pl.when(qi == nq - 1)
        def _():
            dk_ref[...] = (dka_ref[...] * scale).astype(dk_ref.dtype)
            dv_ref[...] = dva_ref[...].astype(dv_ref.dtype)
            dfk_ref[...] = dfa_ref[...]

    qmap = lambda h, ki, qi: (jnp.maximum(qi, ki), h)
    cmap = lambda h, ki, qi: (h, jnp.maximum(qi, ki), 0)
    return pl.pallas_call(
        body, name="attn_bwd_dkv", grid=(NH, nq, nq),
        in_specs=[pl.BlockSpec((B, DH), qmap),
                  pl.BlockSpec((B, DH), lambda h, ki, qi: (ki, h)),
                  pl.BlockSpec((B, DH), lambda h, ki, qi: (ki, NH + h)),
                  pl.BlockSpec((B, DH), qmap),
                  pl.BlockSpec((None, B, 1), cmap), pl.BlockSpec((None, B, 1), cmap),
                  pl.BlockSpec((None, 1, B), lambda h, ki, qi: (h, 0, ki))],
        out_specs=[pl.BlockSpec((B, DH), lambda h, ki, qi: (ki, h)), pl.BlockSpec((B, DH), lambda h, ki, qi: (ki, h)),
                   pl.BlockSpec((None, 1, B), lambda h, ki, qi: (h, 0, ki))],
        out_shape=[jax.ShapeDtypeStruct((L, D), bf16), jax.ShapeDtypeStruct((L, D), bf16),
                   jax.ShapeDtypeStruct((NH, 1, L), f32)],
        scratch_shapes=[pltpu.VMEM((B, DH), f32), pltpu.VMEM((B, DH), f32), pltpu.VMEM((1, B), f32),
                        pltpu.VMEM((B, B), f32), pltpu.VMEM((B, B), f32), pltpu.VMEM((B, B), bf16), pltpu.VMEM((B, B), bf16)],
        compiler_params=_cp(("parallel", "parallel", "arbitrary")),
    )(q, kv, kv, do, delta, lse, fk)


def _fox_logits(q, k, fqv, fkv, scale, masked):
    s = lax.dot_general(q, k, _DIMS["nt"], preferred_element_type=f32) * scale + fqv - fkv
    if masked:
        rows = lax.broadcasted_iota(jnp.int32, s.shape, 0)
        cols = lax.broadcasted_iota(jnp.int32, s.shape, 1)
        return s, cols <= rows
    return s, None


def _fox_fwd(q, kv, fq, fk, cfg):
    L, D, NH, DH, B = cfg.L, cfg.D, cfg.NH, cfg.DH, cfg.BQ
    nq = L // B
    scale = DH ** -0.5

    def body(q_ref, k_ref, v_ref, fq_ref, fk_ref, o_ref, lse_ref):
        qi = pl.program_id(1)
        qv, fqv = q_ref[...], fq_ref[...]

        def chunk(kj, carry, masked):
            m, l, acc = carry
            rows = pl.ds(pl.multiple_of(kj * B, B), B)
            s, mask = _fox_logits(qv, k_ref[rows, :], fqv, fk_ref[kj], scale, masked)
            if masked:
                s = jnp.where(mask, s, NEG)
            m_new = jnp.maximum(m, jnp.max(s, axis=1, keepdims=True))
            alpha = jnp.exp(m - m_new)
            p = jnp.exp(s - m_new)
            l = alpha * l + jnp.sum(p, axis=1, keepdims=True)
            acc = alpha * acc + jnp.dot(p.astype(bf16), v_ref[rows, :], preferred_element_type=f32)
            return m_new, l, acc

        init = (jnp.full((B, 1), NEG, f32), jnp.zeros((B, 1), f32), jnp.zeros((B, DH), f32))
        carry = lax.fori_loop(0, qi, lambda kj, c: chunk(kj, c, False), init)
        m, l, acc = chunk(qi, carry, True)
        o_ref[...] = (acc / l).astype(o_ref.dtype)
        lse_ref[...] = m + jnp.log(l)

    return pl.pallas_call(
        body, name="attn_fwd", grid=(NH, nq),
        in_specs=[pl.BlockSpec((B, DH), lambda h, qi: (qi, h)),
                  pl.BlockSpec((L, DH), lambda h, qi: (0, h)), pl.BlockSpec((L, DH), lambda h, qi: (0, NH + h)),
                  pl.BlockSpec((None, B, 1), lambda h, qi: (h, qi, 0)),
                  pl.BlockSpec((None, nq, 1, B), lambda h, qi: (h, 0, 0, 0))],
        out_specs=[pl.BlockSpec((B, DH), lambda h, qi: (qi, h)), pl.BlockSpec((None, B, 1), lambda h, qi: (h, qi, 0))],
        out_shape=[jax.ShapeDtypeStruct((L, D), bf16), jax.ShapeDtypeStruct((NH, L, 1), f32)],
        compiler_params=_cp(("parallel", "arbitrary")),
    )(q, kv, kv, fq, fk)


def _fox_bwd_dq(q, kv, do, o, lse, fq, fk, cfg):
    L, D, NH, DH, B = cfg.L, cfg.D, cfg.NH, cfg.DH, cfg.BQ
    nq = L // B
    scale = DH ** -0.5

    def body(q_ref, k_ref, v_ref, do_ref, o_ref, lse_ref, fq_ref, fk_ref, dq_ref, dfq_ref, dl_ref):
        qi = pl.program_id(1)
        qv, fqv, dov, lsev = q_ref[...], fq_ref[...], do_ref[...], lse_ref[...]
        delta = jnp.sum(dov.astype(f32) * o_ref[...].astype(f32), axis=1, keepdims=True)

        def chunk(kj, carry, masked):
            acc, df = carry
            rows = pl.ds(pl.multiple_of(kj * B, B), B)
            kv_ = k_ref[rows, :]
            s, mask = _fox_logits(qv, kv_, fqv, fk_ref[kj], scale, masked)
            p = jnp.exp(s - lsev)
            if masked:
                p = jnp.where(mask, p, 0.0)
            dp = lax.dot_general(dov, v_ref[rows, :], _DIMS["nt"], preferred_element_type=f32)
            ds = p * (dp - delta)
            return acc + jnp.dot(ds.astype(bf16), kv_, preferred_element_type=f32), df + jnp.sum(ds, axis=1, keepdims=True)

        carry = lax.fori_loop(0, qi, lambda kj, c: chunk(kj, c, False), (jnp.zeros((B, DH), f32), jnp.zeros((B, 1), f32)))
        acc, df = chunk(qi, carry, True)
        dq_ref[...] = (acc * scale).astype(dq_ref.dtype)
        dfq_ref[...] = df
        dl_ref[...] = delta

    qmap = lambda h, qi: (qi, h)
    cmap = lambda h, qi: (h, qi, 0)
    return pl.pallas_call(
        body, name="attn_bwd_dq", grid=(NH, nq),
        in_specs=[pl.BlockSpec((B, DH), qmap),
                  pl.BlockSpec((L, DH), lambda h, qi: (0, h)), pl.BlockSpec((L, DH), lambda h, qi: (0, NH + h)),
                  pl.BlockSpec((B, DH), qmap), pl.BlockSpec((B, DH), qmap),
                  pl.BlockSpec((None, B, 1), cmap), pl.BlockSpec((None, B, 1), cmap),
                  pl.BlockSpec((None, nq, 1, B), lambda h, qi: (h, 0, 0, 0))],
        out_specs=[pl.BlockSpec((B, DH), qmap), pl.BlockSpec((None, B, 1), cmap), pl.BlockSpec((None, B, 1), cmap)],
        out_shape=[jax.ShapeDtypeStruct((L, D), bf16), jax.ShapeDtypeStruct((NH, L, 1), f32),
                   jax.ShapeDtypeStruct((NH, L, 1), f32)],
        compiler_params=_cp(("parallel", "arbitrary")),
    )(q, kv, kv, do, o, lse, fq, fk)


def _fox_bwd_dkv(q, kv, do, delta, lse, fq, fk, cfg):
    L, D, NH, DH, B = cfg.L, cfg.D, cfg.NH, cfg.DH, cfg.BQ
    nq = L // B
    scale = DH ** -0.5

    def body(q_ref, k_ref, v_ref, do_ref, dl_ref, lse_ref, fq_ref, fk_ref, dk_ref, dv_ref, dfk_ref):
        ki = pl.program_id(1)
        kv_, vv, fkv = k_ref[...], v_ref[...], fk_ref[...]

        def block(qj, carry, masked):
            dk, dv, df = carry
            rows = pl.ds(pl.multiple_of(qj * B, B), B)
            qv, dov = q_ref[rows, :], do_ref[rows, :]
            s, mask = _fox_logits(qv, kv_, fq_ref[rows, :], fkv, scale, masked)
            p = jnp.exp(s - lse_ref[rows, :])
            if masked:
                p = jnp.where(mask, p, 0.0)
            dv = dv + lax.dot_general(p.astype(bf16), dov, _DIMS["tn"], preferred_element_type=f32)
            dp = lax.dot_general(dov, vv, _DIMS["nt"], preferred_element_type=f32)
            ds = p * (dp - dl_ref[rows, :])
            dk = dk + lax.dot_general(ds.astype(bf16), qv, _DIMS["tn"], preferred_element_type=f32)
            return dk, dv, df - jnp.sum(ds, axis=0, keepdims=True)

        init = (jnp.zeros((B, DH), f32), jnp.zeros((B, DH), f32), jnp.zeros((1, B), f32))
        carry = block(ki, init, True)
        dk, dv, df = lax.fori_loop(ki + 1, nq, lambda qj, c: block(qj, c, False), carry)
        dk_ref[...] = (dk * scale).astype(dk_ref.dtype)
        dv_ref[...] = dv.astype(dv_ref.dtype)
        dfk_ref[...] = df

    whole = lambda h, ki: (0, h)
    col = lambda h, ki: (h, 0, 0)
    return pl.pallas_call(
        body, name="attn_bwd_dkv", grid=(NH, nq),
        in_specs=[pl.BlockSpec((L, DH), whole),
                  pl.BlockSpec((B, DH), lambda h, ki: (ki, h)), pl.BlockSpec((B, DH), lambda h, ki: (ki, NH + h)),
                  pl.BlockSpec((L, DH), whole),
                  pl.BlockSpec((None, L, 1), col), pl.BlockSpec((None, L, 1), col), pl.BlockSpec((None, L, 1), col),
                  pl.BlockSpec((None, None, 1, B), lambda h, ki: (h, ki, 0, 0))],
        out_specs=[pl.BlockSpec((B, DH), lambda h, ki: (ki, h)), pl.BlockSpec((B, DH), lambda h, ki: (ki, h)),
                   pl.BlockSpec((None, None, 1, B), lambda h, ki: (h, ki, 0, 0))],
        out_shape=[jax.ShapeDtypeStruct((L, D), bf16), jax.ShapeDtypeStruct((L, D), bf16),
                   jax.ShapeDtypeStruct((NH, nq, 1, B), f32)],
        compiler_params=_cp(("parallel", "arbitrary")),
    )(q, kv, kv, do, delta, lse, fq, fk)


FCH = 256


def _split3(x):
    hi = x.astype(bf16)
    r1 = x - hi.astype(f32)
    mid = r1.astype(bf16)
    lo = (r1 - mid.astype(f32)).astype(bf16)
    return hi, mid, lo


def _tri_sum(tri, x):
    hi, mid, lo = _split3(x)
    return (jnp.dot(tri, hi, preferred_element_type=f32) + jnp.dot(tri, mid, preferred_element_type=f32)
            + jnp.dot(tri, lo, preferred_element_type=f32))


def _fgate_fwd(z, fb, cfg):
    L = cfg.L

    def body(z_ref, fb_ref, f_ref):
        r = lax.broadcasted_iota(jnp.int32, (FCH, FCH), 0)
        c = lax.broadcasted_iota(jnp.int32, (FCH, FCH), 1)
        tri = (c <= r).astype(bf16)
        carry = jnp.zeros((1, LANES), f32)
        for ch in range(L // FCH):
            x = z_ref[pl.ds(ch * FCH, FCH), :] + fb_ref[...]
            lf = jnp.minimum(x, 0.0) - jnp.log(1.0 + jnp.exp(-jnp.abs(x)))
            f_ref[pl.ds(ch * FCH, FCH), :] = _tri_sum(tri, lf) + carry
            carry = f_ref[pl.ds(ch * FCH + FCH - 1, 1), :]

    vm = pl.BlockSpec(memory_space=pltpu.VMEM)
    return pl.pallas_call(body, name="fgate_fwd", in_specs=[vm, vm], out_specs=vm,
                          out_shape=jax.ShapeDtypeStruct((L, LANES), f32), compiler_params=_cp())(z, fb)


def _fgate_bwd(df, z, fb, cfg):
    L = cfg.L

    def body(df_ref, z_ref, fb_ref, dz_ref, db_ref):
        r = lax.broadcasted_iota(jnp.int32, (FCH, FCH), 0)
        c = lax.broadcasted_iota(jnp.int32, (FCH, FCH), 1)
        tri = (c >= r).astype(bf16)
        carry = jnp.zeros((1, LANES), f32)
        dbs = jnp.zeros((1, LANES), f32)
        for ch in range(L // FCH - 1, -1, -1):
            suf = _tri_sum(tri, df_ref[pl.ds(ch * FCH, FCH), :]) + carry
            x = z_ref[pl.ds(ch * FCH, FCH), :] + fb_ref[...]
            dz = suf * _sigmoid(-x)
            dz_ref[pl.ds(ch * FCH, FCH), :] = dz
            dbs = dbs + _colsum(dz)
            carry = carry + _colsum(df_ref[pl.ds(ch * FCH, FCH), :])
        db_ref[...] = dbs

    vm = pl.BlockSpec(memory_space=pltpu.VMEM)
    return pl.pallas_call(body, name="fgate_bwd", in_specs=[vm, vm, vm], out_specs=[vm, vm],
                          out_shape=[jax.ShapeDtypeStruct((L, LANES), f32), jax.ShapeDtypeStruct((1, LANES), f32)],
                          compiler_params=_cp())(df, z, fb)


def _adamw(w, g, m, v, name):
    R, C = w.shape
    c1 = 1.0 - ADAM_B1 ** ADAM_STEP
    c2 = 1.0 - ADAM_B2 ** ADAM_STEP

    def fn(i, ni, wv, gv, mv, vv):
        mn = ADAM_B1 * mv + (1.0 - ADAM_B1) * gv
        vn = ADAM_B2 * vv + (1.0 - ADAM_B2) * (gv * gv)
        delta = -ADAM_LR * ((mn / c1) / (jnp.sqrt(vn / c2) + ADAM_EPS) + ADAM_WD * wv)
        return [delta, mn, vn], []
    tc = C if C % LANES else _tile(C, 1024)
    return _rowwise(fn, [(w, "rc"), (g, "rc"), (m, "rc"), (v, "rc")], [f32, f32, f32], [], L=R, C=C, tl=512, tc=tc, name=name)


def _sum_lead(x, out_dtype, name):
    n, R, C = x.shape
    tl = _tile(R, 512, HALO)
    tc = C if C % LANES else _tile(C, 1024)

    def body(x_ref, o_ref):
        acc = x_ref[0].astype(f32)
        for k in range(1, n):
            acc = acc + x_ref[k].astype(f32)
        o_ref[...] = acc.astype(o_ref.dtype)

    return pl.pallas_call(
        body, name=name, grid=(R // tl, C // tc),
        in_specs=[pl.BlockSpec((n, tl, tc), lambda i, j: (0, i, j))], out_specs=pl.BlockSpec((tl, tc), lambda i, j: (i, j)),
        out_shape=jax.ShapeDtypeStruct((R, C), out_dtype), compiler_params=_cp(("parallel", "parallel")),
    )(x)


def _add2(a, b, out_dtype, name):
    R, C = a.shape

    def fn(i, ni, av, bv):
        return [av.astype(f32) + bv.astype(f32)], []
    tc = C if C % LANES else _tile(C, 1024)
    return _rowwise(fn, [(a, "rc"), (b, "rc")], [out_dtype], [], L=R, C=C, tl=512, tc=tc, name=name)[0]


ANY = pl.BlockSpec(memory_space=pl.ANY)


def _place():
    x, y, c = lax.axis_index("x"), lax.axis_index("y"), lax.axis_index("c")
    return x, y, c


def _allgather8(blocks, name):
    return _run_rider(_gather_rider(blocks), name)


def _gather_rider(blocks):
    n = len(blocks)

    def steps(ins, outs, sems):
        send_sems, recv_sems, local_sems = sems
        x, y, c = _place()
        me, sibling = (x, y, c), (x, y, 1 - c)
        chips = [(1 - x, y), (x, 1 - y), (1 - x, 1 - y)]

        def slot(a, dev):
            return outs[a].at[4 * dev[0] + 2 * dev[1] + dev[2]]

        def copy(a, k, block, to, src=None):
            return pltpu.make_async_remote_copy(
                src_ref=slot(a, block) if src is None else src, dst_ref=slot(a, block),
                send_sem=send_sems.at[a * 7 + k], recv_sem=recv_sems.at[a * 7 + k], device_id=to, device_id_type=MESH)

        def mine():
            return [pltpu.make_async_copy(ins[a], slot(a, me), local_sems.at[a]) for a in range(n)]

        def first():
            out = []
            for a in range(n):
                out.append(copy(a, 0, me, sibling, src=ins[a]))
                out += [copy(a, 1 + j, me, (*chip, c), src=ins[a]) for j, chip in enumerate(chips)]
            return out

        def passed():
            return [copy(a, 4 + j, (*chip, c), sibling) for j, chip in enumerate(chips) for a in range(n)]

        def start():
            for cp in mine() + first():
                cp.start()

        def middle():
            for j, chip in enumerate(chips):
                for a in range(n):
                    copy(a, 1 + j, (*chip, c), me).wait_recv()
                    copy(a, 4 + j, (*chip, c), sibling).start()

        def finish():
            for a in range(n):
                copy(a, 0, sibling, me).wait_recv()
            for j, chip in enumerate(chips):
                for a in range(n):
                    copy(a, 4 + j, (*chip, 1 - c), me).wait_recv()
            for cp in first() + passed():
                cp.wait_send()
            for cp in mine():
                cp.wait()
        return start, middle, finish

    return dict(ins=list(blocks), out_shapes=[jax.ShapeDtypeStruct((N_DEV,) + b.shape, b.dtype) for b in blocks],
                sems=[pltpu.SemaphoreType.DMA((7 * n,)), pltpu.SemaphoreType.DMA((7 * n,)), pltpu.SemaphoreType.DMA((n,))],
                steps=steps)


def _run_rider(rider, name):
    ni, no = len(rider["ins"]), len(rider["out_shapes"])

    def body(*refs):
        start, middle, finish = rider["steps"](refs[:ni], refs[ni:ni + no], refs[ni + no:])
        start()
        if middle is not None:
            middle()
        finish()

    outs = pl.pallas_call(body, name=name, in_specs=[ANY] * ni, out_specs=[ANY] * no, out_shape=rider["out_shapes"],
                          scratch_shapes=rider["sems"])(*rider["ins"])
    return list(outs)


def _host_call(body, *, name, grid, in_specs, out_specs, out_shape, scratch_shapes, args, prefetch=(), rider=None):
    npre, nin, nout, nscr = len(prefetch), len(in_specs), len(out_specs), len(scratch_shapes)
    r_in, r_out, r_scr = (rider["ins"], rider["out_shapes"], rider["sems"]) if rider else ([], [], [])
    nri, nro = len(r_in), len(r_out)

    def kern(*refs):
        pre, rest = refs[:npre], refs[npre:]
        cin, rin = rest[:nin], rest[nin:nin + nri]
        o0 = nin + nri
        cout, rout = rest[o0:o0 + nout], rest[o0 + nout:o0 + nout + nro]
        s0 = o0 + nout + nro
        cscr, rscr = rest[s0:s0 + nscr], rest[s0 + nscr:]
        if rider:
            ids = [pl.program_id(d) for d in range(len(grid))]
            rest_zero = functools.reduce(jnp.logical_and, [i == 0 for i in ids[1:]], True)
            start, middle, finish = rider["steps"](rin, rout, rscr)
            pl.when(jnp.logical_and(ids[0] == 0, rest_zero))(start)
            if middle is not None:
                pl.when(jnp.logical_and(ids[0] == grid[0] // 2, rest_zero))(middle)
        body(*pre, *cin, *cout, *cscr)
        if rider:
            pl.when(functools.reduce(jnp.logical_and, [i == g - 1 for i, g in zip(ids, grid)]))(finish)

    res = pl.pallas_call(
        kern, name=name,
        grid_spec=pltpu.PrefetchScalarGridSpec(num_scalar_prefetch=npre, grid=grid, in_specs=list(in_specs) + [ANY] * nri,
                                               out_specs=list(out_specs) + [ANY] * nro,
                                               scratch_shapes=list(scratch_shapes) + list(r_scr)),
        out_shape=list(out_shape) + list(r_out),
        compiler_params=_cp(("arbitrary",) * len(grid) if rider else ("parallel",) + ("arbitrary",) * (len(grid) - 1)),
    )(*prefetch, *args, *r_in)
    return list(res[:nout]), list(res[nout:])


def _sibling_send(halves, name):
    n = len(halves)

    def body(*refs):
        ins, outs = refs[:n], refs[n:2 * n]
        send_sems, recv_sems = refs[2 * n:]
        x, y, c = _place()
        sends = [pltpu.make_async_remote_copy(src_ref=ins[a], dst_ref=outs[a], send_sem=send_sems.at[a],
                                              recv_sem=recv_sems.at[a], device_id=(x, y, 1 - c), device_id_type=MESH)
                 for a in range(n)]
        for cp in sends:
            cp.start()
        for cp in sends:
            cp.wait_recv()
        for cp in sends:
            cp.wait_send()

    outs = pl.pallas_call(
        body, name=name, in_specs=[ANY] * n, out_specs=[ANY] * n,
        out_shape=[jax.ShapeDtypeStruct(h.shape, h.dtype) for h in halves],
        scratch_shapes=[pltpu.SemaphoreType.DMA((n,)), pltpu.SemaphoreType.DMA((n,))],
    )(*halves)
    return list(outs)


def _sibling_swap_halves(grads, name):
    n = len(grads)

    def body(*refs):
        ins, outs = refs[:n], refs[n:2 * n]
        send_sems, recv_sems = refs[2 * n:]
        x, y, c = _place()
        sends = [pltpu.make_async_remote_copy(src_ref=ins[a].at[:, 1 - c], dst_ref=outs[a], send_sem=send_sems.at[a],
                                              recv_sem=recv_sems.at[a], device_id=(x, y, 1 - c), device_id_type=MESH)
                 for a in range(n)]
        for cp in sends:
            cp.start()
        for cp in sends:
            cp.wait_recv()
        for cp in sends:
            cp.wait_send()

    outs = pl.pallas_call(
        body, name=name, in_specs=[ANY] * n, out_specs=[ANY] * n,
        out_shape=[jax.ShapeDtypeStruct((4,) + g.shape[2:], g.dtype) for g in grads],
        scratch_shapes=[pltpu.SemaphoreType.DMA((n,)), pltpu.SemaphoreType.DMA((n,))],
    )(*grads)
    return list(outs)


def _chip_scatter(parts, name):
    return _run_rider(_scatter_rider(parts), name)


def _scatter_rider(parts):
    n = len(parts)

    def steps(ins, outs, sems):
        send_sems, recv_sems = sems
        x, y, c = _place()
        chips = [(1 - x, y), (x, 1 - y), (1 - x, 1 - y)]

        def sends():
            return [pltpu.make_async_remote_copy(
                src_ref=ins[a].at[2 * px + py], dst_ref=outs[a].at[j], send_sem=send_sems.at[a * 3 + j],
                recv_sem=recv_sems.at[a * 3 + j], device_id=(px, py, c), device_id_type=MESH)
                for a in range(n) for j, (px, py) in enumerate(chips)]

        def start():
            for cp in sends():
                cp.start()

        def finish():
            for cp in sends():
                cp.wait_recv()
            for cp in sends():
                cp.wait_send()
        return start, None, finish

    return dict(ins=list(parts), out_shapes=[jax.ShapeDtypeStruct((3,) + p.shape[1:], p.dtype) for p in parts],
                sems=[pltpu.SemaphoreType.DMA((3 * n,)), pltpu.SemaphoreType.DMA((3 * n,))], steps=steps)


def _sum_parts(own, got, chip, name):
    _, R, C = own.shape
    tl = _tile(R, 512, HALO)
    tc = C if C % LANES else _tile(C, 1024)

    def body(chip_ref, own_ref, got_ref, o_ref):
        acc = own_ref[...].astype(f32)
        for k in range(3):
            acc = acc + got_ref[k].astype(f32)
        o_ref[...] = acc

    return pl.pallas_call(
        body, name=name,
        grid_spec=pltpu.PrefetchScalarGridSpec(
            num_scalar_prefetch=1, grid=(R // tl, C // tc),
            in_specs=[pl.BlockSpec((None, tl, tc), lambda i, j, ch: (ch[0], i, j)),
                      pl.BlockSpec((3, tl, tc), lambda i, j, ch: (0, i, j))],
            out_specs=pl.BlockSpec((tl, tc), lambda i, j, ch: (i, j))),
        out_shape=jax.ShapeDtypeStruct((R, C), f32), compiler_params=_cp(("parallel", "parallel")),
    )(chip, own, got)


def _adamw_halves(w, m, v, g_mine, g_other, core, name):
    NL, R, C = w.shape
    r = R // 2
    tl = _tile(r, 512, HALO)
    tc = C if C % LANES else _tile(C, 1024)
    nh = r // tl
    c1 = 1.0 - ADAM_B1 ** ADAM_STEP
    c2 = 1.0 - ADAM_B2 ** ADAM_STEP

    def body(core_ref, w_ref, m_ref, v_ref, gm_ref, go_ref, g_out, d_out, m_out, v_out):
        i = pl.program_id(1)
        mine = lax.div(i, nh) == core_ref[0]
        gv = jnp.where(mine, gm_ref[...], go_ref[...])
        mn = ADAM_B1 * m_ref[...] + (1.0 - ADAM_B1) * gv
        vn = ADAM_B2 * v_ref[...] + (1.0 - ADAM_B2) * (gv * gv)
        g_out[...] = gv
        d_out[...] = -ADAM_LR * ((mn / c1) / (jnp.sqrt(vn / c2) + ADAM_EPS) + ADAM_WD * w_ref[...])
        m_out[...] = mn
        v_out[...] = vn

    full = pl.BlockSpec((None, tl, tc), lambda l, i, j, co: (l, i, j))
    mine_spec = pl.BlockSpec((None, tl, tc), lambda l, i, j, co: (l, jnp.clip(i - co[0] * nh, 0, nh - 1), j))
    other_spec = pl.BlockSpec((None, tl, tc), lambda l, i, j, co: (l, jnp.clip(i - (1 - co[0]) * nh, 0, nh - 1), j))
    return pl.pallas_call(
        body, name=name,
        grid_spec=pltpu.PrefetchScalarGridSpec(
            num_scalar_prefetch=1, grid=(NL, R // tl, C // tc),
            in_specs=[full, full, full, mine_spec, other_spec], out_specs=[full] * 4),
        out_shape=[jax.ShapeDtypeStruct((NL, R, C), f32)] * 4,
        compiler_params=_cp(("parallel", "parallel", "parallel")),
    )(core, w, m, v, g_mine, g_other)


def _s5_discretize(log_step, a_re, a_im, b_re, b_im):
    step = jnp.exp(log_step)[:, None]
    mag = jnp.exp(a_re * step)
    abar_re = mag * jnp.cos(a_im * step)
    abar_im = mag * jnp.sin(a_im * step)
    den = a_re * a_re + a_im * a_im
    nr = abar_re - 1.0
    fr = (nr * a_re + abar_im * a_im) / den
    fi = (abar_im * a_re - nr * a_im) / den
    bbar_re = fr[..., None] * b_re - fi[..., None] * b_im
    bbar_im = fr[..., None] * b_im + fi[..., None] * b_re
    return abar_re, abar_im, bbar_re, bbar_im


def _s5_prepare(p, cfg):
    abar_re, abar_im, bbar_re, bbar_im = _s5_discretize(p["log_step"], p["a_re"], p["a_im"], p["b_re"], p["b_im"])
    step = jnp.exp(p["log_step"])[:, None]
    arow, tab = _s5_tables(abar_re, abar_im, p["a_re"], p["a_im"], step, cfg)
    bmat, cmat = _s5_mats(bbar_re, bbar_im, p["c_re"], p["c_im"], cfg)
    return dict(arow=arow, tab=tab, bmat=bmat, cmat=cmat, drow=p["d"].reshape(1, cfg.D))


def _s5_param_grads(p, dbmat, dcmat, dabar, dd, cfg):
    J, P = cfg.G // 8, cfg.P
    dbb_re, dbb_im, dc_re, dc_im = _s5_unmats(dbmat, dcmat, cfg)
    da = dabar.reshape(J, 2, 8, P)
    da_re, da_im = da[:, 0].reshape(cfg.G, P), da[:, 1].reshape(cfg.G, P)
    _, vjp = jax.vjp(_s5_discretize, p["log_step"], p["a_re"], p["a_im"], p["b_re"], p["b_im"])
    dls, dare, daim, dbre, dbim = vjp((da_re, da_im, dbb_re, dbb_im))
    return dict(log_step=dls, a_re=dare, a_im=daim, b_re=dbre, b_im=dbim, c_re=dc_re, c_im=dc_im, d=dd.reshape(cfg.G, cfg.H))


def _resid_epi(acc, xv, gv):
    return xv + gv * acc, acc


def _ffn_fwd(x_in, g_norm, sc, sh, gate, W, exch, conv_w, conv_b, cfg, tag):
    h = _norm_mod_fwd(x_in, g_norm, sc, sh, cfg, f"ffn_norm_{tag}")
    rider = exch.rider(f"ffn_up_{tag}")
    a = _mm(h, W[f"ffn_w_up{tag}"], mode="nn", b4=True, tn=1408, out_dtypes=(bf16,), name=f"ffn_up_{tag}", rider=rider)
    if rider:
        a, extra = a
        W.update(exch.done(f"ffn_up_{tag}", extra))
    act = _conv_act_fwd(a, conv_w, conv_b, cfg)
    x_out, out = _mm(act, W[f"ffn_w_down{tag}"], mode="nn", extras=[(x_in, "mn"), (gate, "n")], epi=_resid_epi,
                     out_dtypes=(f32, bf16), name=f"ffn_down_{tag}")
    return x_out, dict(h=h, a=a, act=act, out=out)


def _ffn_bwd(dx, x_in, sv, g_norm, sc, gate, w_up4, w_down, conv_w, conv_b, cfg, tag):
    F = cfg.F
    dout, dgate = _gate_bwd(dx, sv["out"], gate, cfg, f"ffn_gate_bwd_{tag}")
    dact = _mm(dout, w_down, mode="nt", tn=1408, out_dtypes=(bf16,), name=f"ffn_dact_{tag}")
    dw_down = _mm(sv["act"], dout, mode="tn", tm=1408, out_dtypes=(bf16,), name=f"ffn_dwdown_{tag}")
    dcu, dcv, dwu, dwv, dbu, dbv = _conv_act_bwd1(dact, sv["a"], conv_w, conv_b, cfg)
    dau = _conv_bwd2(dcu, conv_w[:, :F], cfg, f"conv_bwd2u_{tag}")
    dav = _conv_bwd2(dcv, conv_w[:, F:], cfg, f"conv_bwd2v_{tag}")
    da = jnp.concatenate([dau, dav], axis=1)
    dh = _mm(da, w_up4, mode="nt", b4=True, tk=1408, out_dtypes=(bf16,), name=f"ffn_dh_{tag}")
    dw_up = _mm(sv["h"], da, mode="tn", out4=True, tn=1408, out_dtypes=(bf16,), name=f"ffn_dwup_{tag}")
    dx_in, A, B = _norm_mod_bwd(dh, x_in, g_norm, sc, dx, cfg, f"ffn_norm_bwd_{tag}")
    small = dict(norm_g=(1.0 + sc) * A, sc=g_norm * A, sh=B, gate=dgate,
                 conv_w=jnp.concatenate([dwu, dwv], axis=1), conv_b=jnp.concatenate([dbu, dbv], axis=1))
    return dx_in, dw_up, dw_down, small


class _NoExchange:
    def rider(self, key, grads=None):
        return None

    def done(self, key, extra):
        return {}


def _local_step(cfg, x, tgt, mod, W, sp, exch=None):
    D, NH = cfg.D, cfg.NH
    exch = exch or _NoExchange()
    W, big = dict(W), {}

    def hand_over(key, grads):
        rider = exch.rider(key, grads)
        if rider is None:
            big.update(grads)
        return rider
    row = lambda v: v.reshape(1, -1)
    nmg0, nmg1 = row(sp["norm_mix_g"][0]), row(sp["norm_mix_g"][1])
    nfg0, nfg1 = row(sp["norm_ffn_g"][0]), row(sp["norm_ffn_g"][1])
    kvg, fng = row(sp["kv_norm_g"]), row(sp["final_norm_g"])
    cw0, cw1 = sp["ffn_conv_w"][0], sp["ffn_conv_w"][1]
    cb0, cb1 = row(sp["ffn_conv_b"][0]), row(sp["ffn_conv_b"][1])
    glu_b = row(sp["ssm_glu_b"])
    fb = jnp.zeros((1, LANES), f32).at[0, :NH].set(sp["forget_b"])
    s5p = {k: sp["ssm_" + k][0] for k in ("log_step", "a_re", "a_im", "b_re", "b_im", "c_re", "c_im", "d")}
    s5 = _s5_prepare(s5p, cfg)
    m0, m1 = mod["l0"], mod["l1"]

    h0 = _norm_mod_fwd(x, nmg0, m0["sc_m"], m0["sh_m"], cfg, "mix_norm_0")
    u = _mm(h0, W["ssm_w_in"], mode="nn", name="ssm_in")
    y, gact, cin, extra = _s5_fwd(u, s5["bmat"], s5["cmat"], s5["drow"], s5["arow"], s5["tab"], cfg, exch.rider("s5_fwd"))
    W.update(exch.done("s5_fwd", extra))

    def glu_epi(acc, bv, gv):
        pre = acc + bv
        return pre, gv.astype(f32) * _sigmoid(pre)
    pre, z = _mm(gact, W["ssm_glu_w"], mode="nn", extras=[(glu_b, "n"), (gact, "mn")], epi=glu_epi,
                 out_dtypes=(f32, bf16), name="ssm_glu")
    x1, out_m0 = _mm(z, W["ssm_w_out"], mode="nn", extras=[(x, "mn"), (m0["g_m"], "n")], epi=_resid_epi,
                     out_dtypes=(f32, bf16), name="ssm_out")
    x2, ffn0 = _ffn_fwd(x1, nfg0, m0["sc_f"], m0["sh_f"], m0["g_f"], W, exch, cw0, cb0, cfg, "0")

    hk = _norm_mod_fwd(x2, kvg, mod["sc_kv"], mod["sh_kv"], cfg, "kv_norm")
    kvb = _mm(hk, W["kv_w"], mode="nn", out_dtypes=(bf16,), name="kv_proj")
    zf = _mm(hk, W["kv_wf"], mode="nn", name="kv_fproj")
    fc = _fgate_fwd(zf, fb, cfg)
    fct = fc[:, :NH].T
    fk = fct[:, None, :]

    h1 = _norm_mod_fwd(x2, nmg1, m1["sc_m"], m1["sh_m"], cfg, "mix_norm_1")
    q = _mm(h1, W["attn_w_q"], mode="nn", out_dtypes=(bf16,), name="attn_q")
    o, lse, extra = _ta_fwd(q, kvb, fk, cfg, exch.rider("attn_fwd"))
    W.update(exch.done("attn_fwd", extra))
    x3, out_m1 = _mm(o, W["attn_w_out"], mode="nn", extras=[(x2, "mn"), (m1["g_m"], "n")], epi=_resid_epi,
                     out_dtypes=(f32, bf16), name="attn_out")
    x4, ffn1 = _ffn_fwd(x3, nfg1, m1["sc_f"], m1["sh_f"], m1["g_f"], W, exch, cw1, cb1, cfg, "1")

    dx, dfng, lcol = _final_loss(x4, fng, tgt, cfg)
    loss = (0.5 / D) * jnp.sum(lcol)

    dx, dw_up1, dw_down1, sf1 = _ffn_bwd(dx, x3, ffn1, nfg1, m1["sc_f"], m1["g_f"], W["ffn_w_up1"], W["ffn_w_down1"], cw1, cb1, cfg, "1")
    dout, dgm1 = _gate_bwd(dx, out_m1, m1["g_m"], cfg, "attn_gate_bwd")
    do = _mm(dout, W["attn_w_out"], mode="nt", out_dtypes=(bf16,), name="attn_do")
    dw_ao = _mm(o, dout, mode="tn", out_dtypes=(bf16,), name="attn_dwout")
    dq, dfq, delta, extra = _ta_bwd_dq(q, kvb, do, o, lse, fk, cfg,
                                       hand_over("attn_bwd", dict(ffn_w_up1=dw_up1, ffn_w_down1=dw_down1)))
    exch.done("attn_bwd", extra)
    dk, dv, dfk, _ = _ta_bwd_dkv(q, kvb, do, delta, lse, fk, cfg)
    dh1 = _mm(dq, W["attn_w_q"], mode="nt", out_dtypes=(bf16,), name="attn_dh")
    dw_q = _mm(h1, dq, mode="tn", out_dtypes=(bf16,), name="attn_dwq")
    dx, A1, B1 = _norm_mod_bwd(dh1, x2, nmg1, m1["sc_m"], dx, cfg, "mix_norm_bwd_1")

    dfc = jnp.pad((dfq[:, :, 0] + dfk[:, :, 0]).T, ((0, 0), (0, LANES - NH)))
    dzf, dfb = _fgate_bwd(dfc, zf, fb, cfg)
    dkv = jnp.concatenate([dk, dv], axis=1)
    dhk1 = _mm(dkv, W["kv_w"], mode="nt", name="kv_dh1")
    dhk = _mm(dzf, W["kv_wf"], mode="nt", extras=[(dhk1, "mn")], epi=lambda acc, e: (acc + e,), out_dtypes=(bf16,), name="kv_dh2")
    dw_kv = _mm(hk, dkv, mode="tn", out_dtypes=(bf16,), name="kv_dw")
    dw_kf = _mm(hk, dzf, mode="tn", out_dtypes=(bf16,), name="kv_dwf")
    dx, Ak, Bk = _norm_mod_bwd(dhk, x2, kvg, mod["sc_kv"], dx, cfg, "kv_norm_bwd")

    dx, dw_up0, dw_down0, sf0 = _ffn_bwd(dx, x1, ffn0, nfg0, m0["sc_f"], m0["g_f"], W["ffn_w_up0"], W["ffn_w_down0"], cw0, cb0, cfg, "0")
    dout, dgm0 = _gate_bwd(dx, out_m0, m0["g_m"], cfg, "ssm_gate_bwd")
    dz = _mm(dout, W["ssm_w_out"], mode="nt", out_dtypes=(bf16,), name="ssm_dz")
    dw_so = _mm(z, dout, mode="tn", out_dtypes=(bf16,), name="ssm_dwout")
    dpre, dgd, dglub = _glu_bwd(dz, gact, pre, cfg)
    dy = _mm(dpre, W["ssm_glu_w"], mode="nt", extras=[(dgd, "mn"), (y, "mn")],
             epi=lambda acc, e, yv: ((acc + e) * _gelu_grad(yv),), name="ssm_dy")
    dw_glu = _mm(gact, dpre, mode="tn", out_dtypes=(bf16,), name="ssm_dwglu")
    rider = hand_over("s5_bwd", dict(attn_w_q=dw_q, attn_w_out=dw_ao, kv_w=jnp.concatenate([dw_kv, dw_kf[:, :NH]], axis=1),
                                     ffn_w_up0=dw_up0, ffn_w_down0=dw_down0))
    du, dbm, dcm, dab, dd, extra = _s5_bwd(u, dy, cin, s5["bmat"], s5["cmat"], s5["drow"], s5["arow"], s5["tab"], cfg, rider)
    exch.done("s5_bwd", extra)
    dh0 = _mm(du, W["ssm_w_in"], mode="nt", out_dtypes=(bf16,), name="ssm_dh")
    dw_in = _mm(h0, du, mode="tn", out_dtypes=(bf16,), name="ssm_dwin")
    dx, A0, B0 = _norm_mod_bwd(dh0, x, nmg0, m0["sc_m"], dx, cfg, "mix_norm_bwd_0")

    s5g = _s5_param_grads(s5p, dbm, dcm, dab, dd, cfg)
    big.update(ssm_w_in=dw_in, ssm_glu_w=dw_glu, ssm_w_out=dw_so)
    small = dict(
        norm_mix_g=jnp.concatenate([(1.0 + m0["sc_m"]) * A0, (1.0 + m1["sc_m"]) * A1], axis=0),
        norm_ffn_g=jnp.concatenate([sf0["norm_g"], sf1["norm_g"]], axis=0),
        ssm_glu_b=dglub, kv_norm_g=(1.0 + mod["sc_kv"]) * Ak, forget_b=dfb[0, :NH],
        ffn_conv_w=jnp.stack([sf0["conv_w"], sf1["conv_w"]]), ffn_conv_b=jnp.concatenate([sf0["conv_b"], sf1["conv_b"]], axis=0),
        final_norm_g=dfng, **{"ssm_" + k: v[None] for k, v in s5g.items()})
    dmod = [jnp.concatenate([B0, nmg0 * A0, dgm0, sf0["sh"], sf0["sc"], sf0["gate"]], axis=1),
            jnp.concatenate([B1, nmg1 * A1, dgm1, sf1["sh"], sf1["sc"], sf1["gate"]], axis=1),
            jnp.concatenate([Bk, kvg * Ak], axis=1)]
    return loss, dx, big, small, dmod


WEIGHTS = ["mod_w", "mod_b", "norm_mix_g", "norm_ffn_g", "ssm_w_in", "ssm_log_step", "ssm_a_re", "ssm_a_im", "ssm_b_re",
           "ssm_b_im", "ssm_c_re", "ssm_c_im", "ssm_d", "ssm_glu_w", "ssm_glu_b", "ssm_w_out", "kv_mod_w", "kv_mod_b",
           "kv_norm_g", "kv_w", "forget_b", "attn_w_q", "attn_w_out", "ffn_w_up", "ffn_conv_w", "ffn_conv_b", "ffn_w_down",
           "final_norm_g"]
ARGS = ["x", "c"] + WEIGHTS + ["loss_target"] + ["m_" + n for n in WEIGHTS] + ["v_" + n for n in WEIGHTS]
SMALL = ["mod_b", "norm_mix_g", "norm_ffn_g", "ssm_log_step", "ssm_a_re", "ssm_a_im", "ssm_b_re", "ssm_b_im", "ssm_c_re",
         "ssm_c_im", "ssm_d", "ssm_glu_b", "kv_mod_b", "kv_norm_g", "forget_b", "ffn_conv_w", "ffn_conv_b", "final_norm_g"]
PACK_ROWS = 512


def _pack(arrs):
    flat = jnp.concatenate([a.reshape(-1).astype(f32) for a in arrs])
    unit = PACK_ROWS * LANES
    n = -(-flat.shape[0] // unit) * unit
    return jnp.pad(flat, (0, n - flat.shape[0])).reshape(-1, LANES)


def _unpack(packed, shapes):
    flat, out, off = packed.reshape(-1), [], 0
    for s in shapes:
        n = math.prod(s)
        out.append(flat[off:off + n].reshape(s))
        off += n
    return out


def _silu(v):
    return v * _sigmoid(v)


def _half(w, c, axis):
    r = w.shape[axis] // 2
    return lax.dynamic_slice_in_dim(w, c * r, r, axis=axis)


class _Exchange:
    FIRST = ["ssm_w_in", "ssm_glu_w", "ssm_w_out"]
    FWD = dict(s5_fwd=["ffn_w_up0"], ffn_up_0=["ffn_w_down0", "kv_w", "attn_w_q", "attn_w_out"],
               attn_fwd=["ffn_w_up1", "ffn_w_down1"])

    def __init__(self, cfg, blocks, core):
        self.cfg, self.blocks, self.core = cfg, blocks, core
        self.parts, self.scattered, self.names = {}, {}, {}

    def weights(self, names, gathered):
        D, F, NH = self.cfg.D, self.cfg.F, self.cfg.NH
        W = {}
        for n, g in zip(names, gathered):
            if n.startswith("ffn_w_up"):
                W[n] = g.reshape(4, D, 2 * F // 4)
            elif n == "kv_w":
                full = g.reshape(4, D, -1).transpose(1, 0, 2).reshape(D, -1)
                W["kv_w"] = full[:, :2 * D]
                W["kv_wf"] = jnp.pad(full[:, 2 * D:], ((0, 0), (0, LANES - NH)))
            else:
                W[n] = g.reshape(-1, D)
        return W

    def sibling_sum(self, key, grads):
        D = self.cfg.D

        def blocks_of(n, g):
            if n.startswith("ffn_w_up"):
                return g.reshape(4, 2, D // 2, -1)
            if n == "kv_w":
                return g.reshape(D, 4, -1).transpose(1, 0, 2).reshape(4, 2, D // 2, -1)
            return g.reshape(4, 2, g.shape[0] // 8, g.shape[1])
        names = list(grads)
        gb = [blocks_of(n, grads[n]) for n in names]
        recv = _sibling_swap_halves(gb, f"grad_sibling_swap_{key}")
        for n, g, r in zip(names, gb, recv):
            keep = lax.dynamic_index_in_dim(g, self.core, axis=1, keepdims=False)
            rr, cc = keep.shape[1], keep.shape[2]
            self.parts[n] = _add2(keep.reshape(4 * rr, cc), r.reshape(4 * rr, cc), bf16, f"grad_add_{n}").reshape(4, rr, cc)
        return self.parts

    def rider(self, key, grads=None):
        if key in self.FWD:
            return _gather_rider([self.blocks[n] for n in self.FWD[key]])
        if grads is None:
            return None
        self.names[key] = list(grads)
        parts = self.sibling_sum(key, grads)
        return _scatter_rider([parts[n] for n in self.names[key]])

    def done(self, key, extra):
        if key in self.FWD:
            return self.weights(self.FWD[key], extra)
        self.scattered.update(zip(self.names[key], extra))
        return {}


def kernel(x, c, mod_w, mod_b, norm_mix_g, norm_ffn_g, ssm_w_in, ssm_log_step, ssm_a_re, ssm_a_im, ssm_b_re, ssm_b_im, ssm_c_re, ssm_c_im, ssm_d, ssm_glu_w, ssm_glu_b, ssm_w_out, kv_mod_w, kv_mod_b, kv_norm_g, kv_w, forget_b, attn_w_q, attn_w_out, ffn_w_up, ffn_conv_w, ffn_conv_b, ffn_w_down, final_norm_g, loss_target, m_mod_w, m_mod_b, m_norm_mix_g, m_norm_ffn_g, m_ssm_w_in, m_ssm_log_step, m_ssm_a_re, m_ssm_a_im, m_ssm_b_re, m_ssm_b_im, m_ssm_c_re, m_ssm_c_im, m_ssm_d, m_ssm_glu_w, m_ssm_glu_b, m_ssm_w_out, m_kv_mod_w, m_kv_mod_b, m_kv_norm_g, m_kv_w, m_forget_b, m_attn_w_q, m_attn_w_out, m_ffn_w_up, m_ffn_conv_w, m_ffn_conv_b, m_ffn_w_down, m_final_norm_g, v_mod_w, v_mod_b, v_norm_mix_g, v_norm_ffn_g, v_ssm_w_in, v_ssm_log_step, v_ssm_a_re, v_ssm_a_im, v_ssm_b_re, v_ssm_b_im, v_ssm_c_re, v_ssm_c_im, v_ssm_d, v_ssm_glu_w, v_ssm_glu_b, v_ssm_w_out, v_kv_mod_w, v_kv_mod_b, v_kv_norm_g, v_kv_w, v_forget_b, v_attn_w_q, v_attn_w_out, v_ffn_w_up, v_ffn_conv_w, v_ffn_conv_b, v_ffn_w_down, v_final_norm_g):
    a = dict(locals())
    assert list(a) == ARGS
    return _step(CFG, a)


def _step(cfg, a):
    D, F, NH = cfg.D, cfg.F, cfg.NH
    x_, y_, c_ = _place()
    chip, dev = 2 * x_ + y_, 4 * x_ + 2 * y_ + c_

    big_src = dict(ssm_w_in=a["ssm_w_in"][0], ssm_glu_w=a["ssm_glu_w"][0], ssm_w_out=a["ssm_w_out"][0],
                   attn_w_q=a["attn_w_q"][0], attn_w_out=a["attn_w_out"][0],
                   ffn_w_up0=a["ffn_w_up"][0], ffn_w_up1=a["ffn_w_up"][1],
                   ffn_w_down0=a["ffn_w_down"][0], ffn_w_down1=a["ffn_w_down"][1], kv_w=a["kv_w"])
    big_names = list(big_src)
    exch = _Exchange(cfg, {n: _half(big_src[n], c_, 0).astype(bf16) for n in big_names}, c_)
    first = exch.FIRST
    blocks = [exch.blocks[n] for n in first] + [_half(a["ssm_glu_b"], c_, 1), _half(a["ffn_conv_w"], c_, 2), a["c"]]
    got = _allgather8(blocks, "gather_weights")
    W = exch.weights(first, got)
    glu_b_full = got[-3].reshape(D)
    conv_w_full = got[-2].transpose(1, 2, 0, 3).reshape(2, 3, 2 * F)
    c16 = jnp.pad(got[-1].reshape(N_DEV, D), ((0, 16 - N_DEV), (0, 0)))

    mcols = [_mm(c16, a["mod_w"][l], mode="nn", a_pro=_silu, name=f"mod_fwd_{l}") for l in range(2)]
    mcols.append(_mm(c16, a["kv_mod_w"], mode="nn", a_pro=_silu, name="mod_fwd_kv"))
    widths = [m.shape[1] for m in mcols]
    mall = _allgather8([jnp.concatenate(mcols, axis=1)[:N_DEV]], "gather_mod")[0][0::2]
    offs = [0, widths[0], widths[0] + widths[1]]
    rows = []
    for off, wd, bias in zip(offs, widths, [a["mod_b"][0], a["mod_b"][1], a["kv_mod_b"]]):
        fullm = mall[:, :, off:off + wd].transpose(1, 0, 2).reshape(N_DEV, 4 * wd) + bias
        rows.append(lax.dynamic_slice_in_dim(fullm, dev, 1, axis=0))
    mod = {}
    for l in range(2):
        mod[f"l{l}"] = dict(zip(["sh_m", "sc_m", "g_m", "sh_f", "sc_f", "g_f"], jnp.split(rows[l], 6, axis=1)))
    mod["sh_kv"], mod["sc_kv"] = jnp.split(rows[2], 2, axis=1)

    sp = {n: a[n] for n in ["norm_mix_g", "norm_ffn_g", "kv_norm_g", "final_norm_g", "ffn_conv_b", "forget_b", "ssm_log_step",
                            "ssm_a_re", "ssm_a_im", "ssm_b_re", "ssm_b_im", "ssm_c_re", "ssm_c_im", "ssm_d"]}
    sp["ssm_glu_b"], sp["ffn_conv_w"] = glu_b_full, conv_w_full
    loss, dx, big, small, dmod = _local_step(cfg, a["x"][0], a["loss_target"][0], mod, W, sp, exch)
    loss = lax.psum(loss, ("x", "y", "c"))

    small["mod_b"] = jnp.concatenate([dmod[0], dmod[1]], axis=0)
    small["kv_mod_b"] = dmod[2]
    shapes = [(2, 6 * D) if n == "mod_b" else (1, D) if n == "ssm_glu_b" else (2, 3, 2 * F) if n == "ffn_conv_w"
              else a[n].shape for n in SMALL]
    packs = _allgather8([_pack([small[n] for n in SMALL])], "gather_small")[0]
    gsmall = dict(zip(SMALL, _unpack(_sum_lead(packs, f32, "sum_small"), shapes)))
    per_dev = packs.reshape(N_DEV, -1)
    sizes = [math.prod(s) for s in shapes]
    starts = dict(zip(SMALL, [sum(sizes[:i]) for i in range(len(sizes))]))

    def rows_of(name, l, width):
        st = starts[name] + l * 6 * D
        blk = lax.dynamic_slice(per_dev, (0, st + chip * width), (N_DEV, width))
        return jnp.pad(blk, ((0, 16 - N_DEV), (0, 0)))
    g_mod_w = jnp.stack([_mm(c16, rows_of("mod_b", l, 6 * D // 4), mode="tn", a_pro=_silu, name=f"mod_dw_{l}") for l in range(2)])
    g_kv_mod_w = _mm(c16, rows_of("kv_mod_b", 0, 2 * D // 4), mode="tn", a_pro=_silu, name="mod_dw_kv")
    gsmall["ssm_glu_b"] = lax.dynamic_slice_in_dim(gsmall["ssm_glu_b"], chip * (D // 4), D // 4, axis=1)
    gsmall["ffn_conv_w"] = lax.dynamic_slice_in_dim(gsmall["ffn_conv_w"], chip * (2 * F // 4), 2 * F // 4, axis=2)

    last = list(big)
    exch.scattered.update(zip(last, _chip_scatter([exch.sibling_sum("tail", big)[n] for n in last], "grad_chip_scatter")))
    chip1, core1 = jnp.reshape(chip, (1,)).astype(jnp.int32), jnp.reshape(c_, (1,)).astype(jnp.int32)
    mine = {n: _sum_parts(exch.parts[n], exch.scattered[n], chip1, f"grad_sum_{n}") for n in big_names}
    other = dict(zip(big_names, _sibling_send([mine[n] for n in big_names], "grad_sibling_send")))

    grads = dict(gsmall)
    grads["mod_w"], grads["kv_mod_w"] = g_mod_w, g_kv_mod_w
    delta, new_m, new_v = {}, {}, {}
    members = dict(ssm_w_in=["ssm_w_in"], ssm_glu_w=["ssm_glu_w"], ssm_w_out=["ssm_w_out"], attn_w_q=["attn_w_q"],
                   attn_w_out=["attn_w_out"], kv_w=["kv_w"], ffn_w_up=["ffn_w_up0", "ffn_w_up1"],
                   ffn_w_down=["ffn_w_down0", "ffn_w_down1"])
    for n, parts_ in members.items():
        shp = a[n].shape
        three = lambda t: t.reshape(len(parts_), -1, shp[-1])
        g_, d_, m_, v_ = _adamw_halves(three(a[n]), three(a["m_" + n]), three(a["v_" + n]),
                                       jnp.stack([mine[p] for p in parts_]), jnp.stack([other[p] for p in parts_]),
                                       core1, f"adamw_{n}")
        grads[n], delta[n], new_m[n], new_v[n] = g_.reshape(shp), d_.reshape(shp), m_.reshape(shp), v_.reshape(shp)
    for n in ["mod_w", "kv_mod_w"]:
        shp = a[n].shape
        two = lambda t: t.reshape(-1, shp[-1])
        d_, m_, v_ = _adamw(two(a[n]), two(grads[n]), two(a["m_" + n]), two(a["v_" + n]), f"adamw_{n}")
        delta[n], new_m[n], new_v[n] = d_.reshape(shp), m_.reshape(shp), v_.reshape(shp)
    grads = {n: grads[n].reshape(a[n].shape) for n in WEIGHTS}
    sshapes = [a[n].shape for n in SMALL]
    d_, m_, v_ = _adamw(_pack([a[n] for n in SMALL]), _pack([grads[n] for n in SMALL]), _pack([a["m_" + n] for n in SMALL]),
                        _pack([a["v_" + n] for n in SMALL]), "adamw_small")
    for n, dd_, mm_, vv_ in zip(SMALL, _unpack(d_, sshapes), _unpack(m_, sshapes), _unpack(v_, sshapes)):
        delta[n], new_m[n], new_v[n] = dd_, mm_, vv_

    return (loss, dx[None], *[grads[n] for n in WEIGHTS], *[delta[n] for n in WEIGHTS],
            *[new_m[n] for n in WEIGHTS], *[new_v[n] for n in WEIGHTS])
```

```python
import collections
import functools
import math

import jax
import jax.numpy as jnp
from jax import lax
from jax.experimental import pallas as pl
from jax.experimental.pallas import tpu as pltpu

f32 = jnp.float32
bf16 = jnp.bfloat16
MESH = pl.DeviceIdType.MESH

LANES = 128
SUBLANES = 8
VMEM_BYTES_V7X = 64 * 1024 * 1024
VMEM_LIMIT = 56 * 1024 * 1024

Cfg = collections.namedtuple("Cfg", "L D G P H NH DH F TC BQ")
CFG = Cfg(L=4096, D=2048, G=128, P=64, H=16, NH=16, DH=128, F=5632, TC=512, BQ=512)
NORM_EPS = 1e-6
ADAM_LR, ADAM_B1, ADAM_B2, ADAM_EPS, ADAM_WD, ADAM_STEP = 0.001, 0.9, 0.999, 1e-08, 0.01, 10
N_DEV = 8


def _cp(sem=None):
    return pltpu.CompilerParams(dimension_semantics=sem, vmem_limit_bytes=VMEM_LIMIT)


def _tile(dim, pref, unit=LANES):
    if dim <= pref:
        return dim
    t = (pref // unit) * unit
    while t > unit and dim % t:
        t -= unit
    assert dim % t == 0, (dim, pref)
    return t


_DIMS = {"nn": (((1,), (0,)), ((), ())), "nt": (((1,), (1,)), ((), ())), "tn": (((0,), (0,)), ((), ()))}


def _mm(a, b, *, mode, name, tm=1024, tn=1024, tk=2048, b4=False, out4=False, a_pro=None, extras=(), epi=None,
        out_dtypes=(f32,), rider=None):
    if mode == "tn":
        K, M = a.shape
    else:
        M, K = a.shape
    if b4:
        R, c4 = b.shape[1], b.shape[2]
        N = R if mode == "nt" else 4 * c4
        assert (K == 4 * c4) if mode == "nt" else (K == R)
    else:
        N = b.shape[0] if mode == "nt" else b.shape[1]
        assert K == (b.shape[1] if mode == "nt" else b.shape[0])
    n4 = N // 4
    tm = _tile(M, tm, LANES if mode == "tn" else SUBLANES * 2)
    tn = _tile(n4 if out4 or (b4 and mode != "nt") else N, tn)
    tk = _tile(b.shape[2] if (b4 and mode == "nt") else K, tk)
    nm, nn_, nk = M // tm, N // tn, K // tk

    a_spec = pl.BlockSpec((tk, tm), lambda i, j, k: (k, i)) if mode == "tn" else pl.BlockSpec((tm, tk), lambda i, j, k: (i, k))
    if b4 and mode == "nt":
        q = b.shape[2] // tk
        b_spec = pl.BlockSpec((None, tn, tk), lambda i, j, k: (lax.div(k, q), j, lax.rem(k, q)))
    elif b4:
        q = b.shape[2] // tn
        b_spec = pl.BlockSpec((None, tk, tn), lambda i, j, k: (lax.div(j, q), k, lax.rem(j, q)))
    elif mode == "nt":
        b_spec = pl.BlockSpec((tn, tk), lambda i, j, k: (j, k))
    else:
        b_spec = pl.BlockSpec((tk, tn), lambda i, j, k: (k, j))
    ex_specs = []
    for arr, kind in extras:
        if kind == "mn":
            ex_specs.append(pl.BlockSpec((tm, tn), lambda i, j, k: (i, j)))
        else:
            ex_specs.append(pl.BlockSpec((1, tn), lambda i, j, k: (0, j)))
    if out4:
        qo = n4 // tn
        o_spec = pl.BlockSpec((None, tm, tn), lambda i, j, k: (lax.div(j, qo), i, lax.rem(j, qo)))
        o_shapes = [jax.ShapeDtypeStruct((4, M, n4), dt) for dt in out_dtypes]
    else:
        o_spec = pl.BlockSpec((tm, tn), lambda i, j, k: (i, j))
        o_shapes = [jax.ShapeDtypeStruct((M, N), dt) for dt in out_dtypes]
    ne, no = len(extras), len(out_dtypes)
    dims = _DIMS[mode]

    def body(a_ref, b_ref, *rest):
        ex_refs, o_refs, acc_ref = rest[:ne], rest[ne:ne + no], rest[ne + no]
        k = pl.program_id(2)

        @pl.when(k == 0)
        def _():
            acc_ref[...] = jnp.zeros_like(acc_ref)

        av = a_ref[...]
        if a_pro is not None:
            av = a_pro(av)
        acc_ref[...] += lax.dot_general(av.astype(bf16), b_ref[...].astype(bf16), dims, preferred_element_type=f32)

        @pl.when(k == nk - 1)
        def _():
            acc = acc_ref[...]
            outs = (acc,) if epi is None else epi(acc, *[r[...] for r in ex_refs])
            for o_ref, o in zip(o_refs, outs):
                o_ref[...] = o.astype(o_ref.dtype)

    res, extra = _host_call(
        body, name=name, grid=(nm, nn_, nk), in_specs=[a_spec, b_spec] + ex_specs, out_specs=[o_spec] * no,
        out_shape=o_shapes, scratch_shapes=[pltpu.VMEM((tm, tn), f32)], args=(a, b, *[e[0] for e in extras]), rider=rider)
    res = res[0] if no == 1 else res
    return (res, extra) if rider else res


HALO = 16


def _rowwise(fn, ins, outs, accs, *, L, C, tl, tc, name):
    tl = _tile(L, tl, HALO)
    tc = _tile(C, tc)
    ni, nj = L // tl, C // tc
    hb = tl // HALO
    nh = L // HALO
    in_specs = []
    for spec in ins:
        kind = spec[1]
        off = spec[2] if len(spec) > 2 else 0
        if kind == "rc":
            in_specs.append(pl.BlockSpec((tl, tc), lambda j, i, off=off: (i, j + off)))
        elif kind == "c":
            in_specs.append(pl.BlockSpec((1, tc), lambda j, i, off=off: (0, j + off)))
        elif kind == "c3":
            in_specs.append(pl.BlockSpec((3, tc), lambda j, i, off=off: (0, j + off)))
        elif kind == "prev":
            in_specs.append(pl.BlockSpec((HALO, tc), lambda j, i, off=off: (jnp.maximum(i * hb - 1, 0), j + off)))
        elif kind == "next":
            in_specs.append(pl.BlockSpec((HALO, tc), lambda j, i, off=off: (jnp.minimum((i + 1) * hb, nh - 1), j + off)))
        else:
            raise ValueError(kind)
    out_specs = [pl.BlockSpec((tl, tc), lambda j, i: (i, j)) for _ in outs]
    out_specs += [pl.BlockSpec((r, tc), lambda j, i: (0, j)) for r in accs]
    out_shape = [jax.ShapeDtypeStruct((L, C), dt) for dt in outs] + [jax.ShapeDtypeStruct((r, C), f32) for r in accs]
    nin, nout, nacc = len(ins), len(outs), len(accs)

    def body(*refs):
        i = pl.program_id(1)
        tiles = [r[...] for r in refs[:nin]]
        o_vals, a_vals = fn(i, ni, *tiles)
        for r, v in zip(refs[nin:nin + nout], o_vals):
            r[...] = v.astype(r.dtype)
        if nacc:
            @pl.when(i == 0)
            def _():
                for r in refs[nin + nout:]:
                    r[...] = jnp.zeros_like(r)
            for r, v in zip(refs[nin + nout:], a_vals):
                r[...] += v

    res = pl.pallas_call(
        body, name=name, grid=(nj, ni), in_specs=in_specs, out_specs=out_specs, out_shape=out_shape,
        compiler_params=_cp(("parallel", "arbitrary")),
    )(*[s[0] for s in ins])
    return res


def _colsum(v):
    return jnp.sum(v, axis=0, keepdims=True)


def _sigmoid(x):
    return 1.0 / (1.0 + jnp.exp(-x))


_GELU_C = math.sqrt(2.0 / math.pi)


def _gelu(y):
    t = jnp.tanh(_GELU_C * (y + 0.044715 * y * y * y))
    return 0.5 * y * (1.0 + t)


def _gelu_grad(y):
    y2 = y * y
    t = jnp.tanh(_GELU_C * (y + 0.044715 * y * y2))
    return 0.5 * (1.0 + t) + 0.5 * y * (1.0 - t * t) * _GELU_C * (1.0 + 3.0 * 0.044715 * y2)


def _norm_mod_fwd(x, g, sc, sh, cfg, name):
    def fn(i, ni, xv, gv, scv, shv):
        rstd = lax.rsqrt(jnp.mean(xv * xv, axis=-1, keepdims=True) + NORM_EPS)
        return [xv * rstd * gv * (1.0 + scv) + shv], []
    return _rowwise(fn, [(x, "rc"), (g, "c"), (sc, "c"), (sh, "c")], [bf16], [], L=cfg.L, C=cfg.D, tl=256, tc=cfg.D, name=name)[0]


def _norm_mod_bwd(dh, x, g, sc, dres, cfg, name):
    def fn(i, ni, dhv, xv, gv, scv, *rest):
        dhv = dhv.astype(f32)
        rstd = lax.rsqrt(jnp.mean(xv * xv, axis=-1, keepdims=True) + NORM_EPS)
        xh = xv * rstd
        dxh = dhv * (gv * (1.0 + scv))
        dx = rstd * (dxh - xh * jnp.mean(dxh * xh, axis=-1, keepdims=True))
        if rest:
            dx = dx + rest[0]
        return [dx], [_colsum(dhv * xh), _colsum(dhv)]
    ins = [(dh, "rc"), (x, "rc"), (g, "c"), (sc, "c")] + ([(dres, "rc")] if dres is not None else [])
    return _rowwise(fn, ins, [f32], [1, 1], L=cfg.L, C=cfg.D, tl=256, tc=cfg.D, name=name)


def _final_loss(x, g, tgt, cfg):
    D = cfg.D

    def fn(i, ni, xv, gv, tv):
        rstd = lax.rsqrt(jnp.mean(xv * xv, axis=-1, keepdims=True) + NORM_EPS)
        xh = xv * rstd
        err = xh * gv - tv
        dy = err * (1.0 / D)
        dxh = dy * gv
        dx = rstd * (dxh - xh * jnp.mean(dxh * xh, axis=-1, keepdims=True))
        return [dx], [_colsum(dy * xh), _colsum(err * err)]
    return _rowwise(fn, [(x, "rc"), (g, "c"), (tgt, "rc")], [f32], [1, 1], L=cfg.L, C=D, tl=256, tc=D, name="final_loss")


def _gate_bwd(dx, out, gate, cfg, name):
    def fn(i, ni, dxv, ov, gv):
        return [dxv * gv], [_colsum(dxv * ov.astype(f32))]
    return _rowwise(fn, [(dx, "rc"), (out, "rc"), (gate, "c")], [bf16], [1], L=cfg.L, C=cfg.D, tl=512, tc=cfg.D, name=name)


def _glu_bwd(dz, g, pre, cfg):
    def fn(i, ni, dzv, gv, pv):
        dzv = dzv.astype(f32)
        gv = gv.astype(f32)
        s = _sigmoid(pv)
        dpre = dzv * gv * s * (1.0 - s)
        return [dpre, dzv * s], [_colsum(dpre)]
    return _rowwise(fn, [(dz, "rc"), (g, "rc"), (pre, "rc")], [bf16, f32], [1], L=cfg.L, C=cfg.D, tl=512, tc=cfg.D, name="glu_bwd")


def _shift_rows(av, pv, k, i):
    rows = lax.broadcasted_iota(jnp.int32, av.shape, 0)
    cur = pltpu.roll(av, k, 0)
    prev = pltpu.roll(pv, k, 0)
    prev = jnp.where(i > 0, prev, 0.0)
    prev_full = jnp.concatenate([prev, jnp.zeros((av.shape[0] - pv.shape[0], av.shape[1]), av.dtype)], axis=0) \
        if av.shape[0] > pv.shape[0] else prev
    return jnp.where(rows >= k, cur, prev_full)


def _shift_rows_up(av, nv, k, i, ni):
    n, h = av.shape[0], nv.shape[0]
    rows = lax.broadcasted_iota(jnp.int32, av.shape, 0)
    cur = pltpu.roll(av, n - k, 0)
    nxt = pltpu.roll(nv, h - k, 0)
    nxt = jnp.where(i < ni - 1, nxt, 0.0)
    nxt_full = jnp.concatenate([jnp.zeros((n - h, av.shape[1]), av.dtype), nxt], axis=0) if n > h else nxt
    return jnp.where(rows < n - k, cur, nxt_full)


def _conv3(av, pv, w, i):
    return w[0:1] * _shift_rows(av, pv, 2, i) + w[1:2] * _shift_rows(av, pv, 1, i) + w[2:3] * av


def _conv_act_fwd(a, conv_w, conv_b, cfg):
    F = cfg.F
    tc = _tile(F, 1408)
    nb = F // tc

    def fn(i, ni, au, av, pu, pv, wu, wv, bu, bv):
        cu = _conv3(au.astype(f32), pu.astype(f32), wu, i) + bu
        cv = _conv3(av.astype(f32), pv.astype(f32), wv, i) + bv
        return [cu * _sigmoid(cu) * cv], []
    ins = [(a, "rc"), (a, "rc", nb), (a, "prev"), (a, "prev", nb), (conv_w, "c3"), (conv_w, "c3", nb), (conv_b, "c"), (conv_b, "c", nb)]
    return _rowwise(fn, ins, [bf16], [], L=cfg.L, C=F, tl=512, tc=tc, name="conv_act_fwd")[0]


def _conv_act_bwd1(dact, a, conv_w, conv_b, cfg):
    F = cfg.F
    tc = _tile(F, 1408)
    nb = F // tc

    def fn(i, ni, dav, au, av, pu, pv, wu, wv, bu, bv):
        dav = dav.astype(f32)
        au, av, pu, pv = au.astype(f32), av.astype(f32), pu.astype(f32), pv.astype(f32)
        au1, au2 = _shift_rows(au, pu, 1, i), _shift_rows(au, pu, 2, i)
        av1, av2 = _shift_rows(av, pv, 1, i), _shift_rows(av, pv, 2, i)
        cu = wu[0:1] * au2 + wu[1:2] * au1 + wu[2:3] * au + bu
        cv = wv[0:1] * av2 + wv[1:2] * av1 + wv[2:3] * av + bv
        s = _sigmoid(cu)
        dcu = dav * cv * (s * (1.0 + cu * (1.0 - s)))
        dcv = dav * cu * s
        dwu = jnp.concatenate([_colsum(dcu * au2), _colsum(dcu * au1), _colsum(dcu * au)], axis=0)
        dwv = jnp.concatenate([_colsum(dcv * av2), _colsum(dcv * av1), _colsum(dcv * av)], axis=0)
        return [dcu, dcv], [dwu, dwv, _colsum(dcu), _colsum(dcv)]
    ins = [(dact, "rc"), (a, "rc"), (a, "rc", nb), (a, "prev"), (a, "prev", nb), (conv_w, "c3"), (conv_w, "c3", nb),
           (conv_b, "c"), (conv_b, "c", nb)]
    return _rowwise(fn, ins, [bf16, bf16], [3, 3, 1, 1], L=cfg.L, C=F, tl=512, tc=tc, name="conv_act_bwd1")


def _conv_bwd2(dc, w, cfg, name):
    F = cfg.F
    tc = _tile(F, 1408)

    def fn(i, ni, dcv, nxt, wv):
        dcv, nxt = dcv.astype(f32), nxt.astype(f32)
        return [wv[2:3] * dcv + wv[1:2] * _shift_rows_up(dcv, nxt, 1, i, ni) + wv[0:1] * _shift_rows_up(dcv, nxt, 2, i, ni)], []
    return _rowwise(fn, [(dc, "rc"), (dc, "next"), (w, "c3")], [bf16], [], L=cfg.L, C=F, tl=512, tc=tc, name=name)[0]


NSLAB = 8


def _s5_tables(abar_re, abar_im, lam_re, lam_im, step, cfg):
    J = cfg.G // 8
    expo = jnp.array([r + 1 for r in range(8)] + [8 * 2 ** p for p in range(8)], f32)[:, None, None]
    mag = jnp.exp(lam_re * step * expo)
    ang = lam_im * step * expo
    t_re = (mag * jnp.cos(ang)).reshape(16, J, 8 * cfg.P).transpose(1, 0, 2)
    t_im = (mag * jnp.sin(ang)).reshape(16, J, 8 * cfg.P).transpose(1, 0, 2)
    tab = jnp.concatenate([t_re, t_im], axis=-1)
    arow = jnp.concatenate([abar_re.reshape(J, 1, 8 * cfg.P), abar_im.reshape(J, 1, 8 * cfg.P)], axis=-1)
    return arow, tab


def _s5_mats(bbar_re, bbar_im, c_re, c_im, cfg):
    J, P, H = cfg.G // 8, cfg.P, cfg.H
    eye = jnp.eye(8, dtype=f32)

    def bd_in(bb):
        bb = bb.reshape(J, 8, P, H)
        return jnp.einsum("jgph,gk->jghkp", bb, eye).reshape(J, 8 * H, 8 * P)

    def bd_out(cc):
        cc = cc.reshape(J, 8, H, P)
        return jnp.einsum("jghp,gk->jgpkh", cc, eye).reshape(J, 8 * P, 8 * H)

    bmat = jnp.concatenate([bd_in(bbar_re), bd_in(bbar_im)], axis=2).astype(bf16)
    cmat = jnp.concatenate([bd_out(c_re), -bd_out(c_im)], axis=1).astype(bf16)
    return bmat, cmat


def _s5_unmats(dbmat, dcmat, cfg):
    J, P, H = cfg.G // 8, cfg.P, cfg.H
    eye = jnp.eye(8, dtype=f32)
    db = dbmat.reshape(J, 8, H, 2, 8, P)
    db = jnp.einsum("jghckp,gk->cjgph", db, eye).reshape(2, cfg.G, P, H)
    dc = dcmat.reshape(J, 2, 8, P, 8, H)
    dc = jnp.einsum("jcgpkh,gk->cjghp", dc, eye).reshape(2, cfg.G, H, P)
    return db[0], db[1], dc[0], -dc[1]


def _chunk_scan(x_ref, row0, nt, arow_ref, tab_ref, c0, reverse):
    sg = -1.0 if reverse else 1.0
    rows = lax.broadcasted_iota(jnp.int32, (nt, LANES), 0)
    order = list(range(7, -1, -1)) if reverse else list(range(8))

    def ld(k, r):
        return x_ref[k, pl.ds(row0 + r, nt, stride=8), :]

    def tab(row, k):
        return tab_ref[pl.ds(row, 1), pl.ds(k * LANES, LANES)]

    carries = [None] * NSLAB
    for k in range(4):
        ar = arow_ref[:, pl.ds(k * LANES, LANES)]
        ai = sg * arow_ref[:, pl.ds((4 + k) * LANES, LANES)]
        sr, si = ld(k, order[0]), ld(4 + k, order[0])
        for r in order[1:]:
            sr, si = ar * sr - ai * si + ld(k, r), ar * si + ai * sr + ld(4 + k, r)
        if reverse:
            cr = jnp.where(rows == nt - 1, c0[k], pltpu.roll(sr, nt - 1, 0))
            ci = jnp.where(rows == nt - 1, c0[4 + k], pltpu.roll(si, nt - 1, 0))
        else:
            cr = jnp.where(rows == 0, c0[k], pltpu.roll(sr, 1, 0))
            ci = jnp.where(rows == 0, c0[4 + k], pltpu.roll(si, 1, 0))
        d, p = 1, 0
        while d < nt:
            qr, qi = tab(8 + p, k), sg * tab(8 + p, 4 + k)
            if reverse:
                shr, shi, m = pltpu.roll(cr, nt - d, 0), pltpu.roll(ci, nt - d, 0), rows < nt - d
            else:
                shr, shi, m = pltpu.roll(cr, d, 0), pltpu.roll(ci, d, 0), rows >= d
            cr, ci = cr + jnp.where(m, qr * shr - qi * shi, 0.0), ci + jnp.where(m, qr * shi + qi * shr, 0.0)
            d, p = 2 * d, p + 1
        carries[k], carries[4 + k] = cr, ci
        sr, si = cr, ci
        for r in order:
            sr, si = ar * sr - ai * si + ld(k, r), ar * si + ai * sr + ld(4 + k, r)
            x_ref[k, pl.ds(row0 + r, nt, stride=8), :] = sr
            x_ref[4 + k, pl.ds(row0 + r, nt, stride=8), :] = si
    return carries


def _slabs_to_mat(x_ref, row0, n):
    return jnp.concatenate([x_ref[k, pl.ds(row0, n), :] for k in range(NSLAB)], axis=1)


def _mat_to_slabs(x_ref, row0, n, m):
    for k in range(NSLAB):
        x_ref[k, pl.ds(row0, n), :] = m[:, k * LANES:(k + 1) * LANES]


def _s5_fwd(u, bmat, cmat, drow, arow, tab, cfg, rider=None):
    L, D, Tc = cfg.L, cfg.D, cfg.TC
    J, NC, nt = cfg.G // 8, L // Tc, Tc // 8
    W = NSLAB * LANES

    def body(u_ref, b_ref, c_ref, d_ref, a_ref, t_ref, y_ref, g_ref, cin_ref, x_ref, st_ref):
        c = pl.program_id(1)

        @pl.when(c == 0)
        def _():
            st_ref[...] = jnp.zeros_like(st_ref)

        cin_ref[...] = st_ref[...]
        ub = u_ref[...]
        _mat_to_slabs(x_ref, 0, Tc, jnp.dot(ub.astype(bf16), b_ref[...], preferred_element_type=f32))
        c0 = [st_ref[:, pl.ds(k * LANES, LANES)] for k in range(NSLAB)]
        _chunk_scan(x_ref, 0, nt, a_ref, t_ref, c0, False)
        for k in range(NSLAB):
            st_ref[:, pl.ds(k * LANES, LANES)] = x_ref[k, pl.ds(Tc - 1, 1), :]
        s = _slabs_to_mat(x_ref, 0, Tc).astype(bf16)
        y = jnp.dot(s, c_ref[...], preferred_element_type=f32) + d_ref[...] * ub
        y_ref[...] = y
        g_ref[...] = _gelu(y).astype(bf16)

    outs, extra = _host_call(
        body, name="s5_fwd", grid=(J, NC), rider=rider, args=(u, bmat, cmat, drow, arow, tab),
        in_specs=[pl.BlockSpec((Tc, LANES), lambda j, c: (c, j)),
                  pl.BlockSpec((None, LANES, W), lambda j, c: (j, 0, 0)),
                  pl.BlockSpec((None, W, LANES), lambda j, c: (j, 0, 0)),
                  pl.BlockSpec((1, LANES), lambda j, c: (0, j)),
                  pl.BlockSpec((None, 1, W), lambda j, c: (j, 0, 0)),
                  pl.BlockSpec((None, 16, W), lambda j, c: (j, 0, 0))],
        out_specs=[pl.BlockSpec((Tc, LANES), lambda j, c: (c, j)),
                   pl.BlockSpec((Tc, LANES), lambda j, c: (c, j)),
                   pl.BlockSpec((None, None, 1, W), lambda j, c: (j, c, 0, 0))],
        out_shape=[jax.ShapeDtypeStruct((L, D), f32), jax.ShapeDtypeStruct((L, D), bf16),
                   jax.ShapeDtypeStruct((J, NC, 1, W), f32)],
        scratch_shapes=[pltpu.VMEM((NSLAB, Tc, LANES), f32), pltpu.VMEM((1, W), f32)])
    return (*outs, extra)


def _s5_bwd(u, dy, cin, bmat, cmat, drow, arow, tab, cfg, rider=None):
    L, D, Tc = cfg.L, cfg.D, cfg.TC
    J, NC, nt = cfg.G // 8, L // Tc, Tc // 8
    W = NSLAB * LANES
    PAD = 0

    def body(u_ref, dy_ref, cin_ref, b_ref, c_ref, d_ref, a_ref, t_ref,
             du_ref, db_ref, dc_ref, da_ref, dd_ref, s_ref, g_ref, gst_ref):
        c = pl.program_id(1)

        @pl.when(c == 0)
        def _():
            gst_ref[...] = jnp.zeros_like(gst_ref)
            db_ref[...] = jnp.zeros_like(db_ref)
            dc_ref[...] = jnp.zeros_like(dc_ref)
            da_ref[...] = jnp.zeros_like(da_ref)
            dd_ref[...] = jnp.zeros_like(dd_ref)

        ub, dyb = u_ref[...], dy_ref[...]
        ub16, dy16 = ub.astype(bf16), dyb.astype(bf16)
        _mat_to_slabs(s_ref, PAD, Tc, jnp.dot(ub16, b_ref[...], preferred_element_type=f32))
        c0 = [cin_ref[:, pl.ds(k * LANES, LANES)] for k in range(NSLAB)]
        tile_in = _chunk_scan(s_ref, PAD, nt, a_ref, t_ref, c0, False)
        _mat_to_slabs(g_ref, 0, Tc, lax.dot_general(dy16, c_ref[...], _DIMS["nt"], preferred_element_type=f32))
        g0 = [gst_ref[:, pl.ds(k * LANES, LANES)] for k in range(NSLAB)]
        _chunk_scan(g_ref, 0, nt, a_ref, t_ref, g0, True)
        for k in range(NSLAB):
            gst_ref[:, pl.ds(k * LANES, LANES)] = g_ref[k, pl.ds(0, 1), :]
        for k in range(4):
            acc_r = jnp.zeros((nt, LANES), f32)
            acc_i = jnp.zeros((nt, LANES), f32)
            for r in range(8):
                gr = g_ref[k, pl.ds(r, nt, stride=8), :]
                gi = g_ref[4 + k, pl.ds(r, nt, stride=8), :]
                if r == 0:
                    pr, pi = tile_in[k], tile_in[4 + k]
                else:
                    pr = s_ref[k, pl.ds(PAD + r - 1, nt, stride=8), :]
                    pi = s_ref[4 + k, pl.ds(PAD + r - 1, nt, stride=8), :]
                acc_r += gr * pr + gi * pi
                acc_i += gi * pr - gr * pi
            da_ref[:, pl.ds(k * LANES, LANES)] += _colsum(acc_r)
            da_ref[:, pl.ds((4 + k) * LANES, LANES)] += _colsum(acc_i)
        gm = _slabs_to_mat(g_ref, 0, Tc).astype(bf16)
        sm = _slabs_to_mat(s_ref, PAD, Tc).astype(bf16)
        du = lax.dot_general(gm, b_ref[...], _DIMS["nt"], preferred_element_type=f32) + d_ref[...] * dyb
        du_ref[...] = du.astype(bf16)
        db_ref[...] += lax.dot_general(ub16, gm, _DIMS["tn"], preferred_element_type=f32)
        dc_ref[...] += lax.dot_general(sm, dy16, _DIMS["tn"], preferred_element_type=f32)
        dd_ref[...] += _colsum(dyb * ub)

    rc = lambda j, c: (NC - 1 - c, j)
    outs, extra = _host_call(
        body, name="s5_bwd", grid=(J, NC), rider=rider, args=(u, dy, cin, bmat, cmat, drow, arow, tab),
        in_specs=[pl.BlockSpec((Tc, LANES), rc), pl.BlockSpec((Tc, LANES), rc),
                  pl.BlockSpec((None, None, 1, W), lambda j, c: (j, NC - 1 - c, 0, 0)),
                  pl.BlockSpec((None, LANES, W), lambda j, c: (j, 0, 0)),
                  pl.BlockSpec((None, W, LANES), lambda j, c: (j, 0, 0)),
                  pl.BlockSpec((1, LANES), lambda j, c: (0, j)),
                  pl.BlockSpec((None, 1, W), lambda j, c: (j, 0, 0)),
                  pl.BlockSpec((None, 16, W), lambda j, c: (j, 0, 0))],
        out_specs=[pl.BlockSpec((Tc, LANES), rc),
                   pl.BlockSpec((None, LANES, W), lambda j, c: (j, 0, 0)),
                   pl.BlockSpec((None, W, LANES), lambda j, c: (j, 0, 0)),
                   pl.BlockSpec((None, 1, W), lambda j, c: (j, 0, 0)),
                   pl.BlockSpec((1, LANES), lambda j, c: (0, j))],
        out_shape=[jax.ShapeDtypeStruct((L, D), bf16), jax.ShapeDtypeStruct((J, LANES, W), f32),
                   jax.ShapeDtypeStruct((J, W, LANES), f32), jax.ShapeDtypeStruct((J, 1, W), f32),
                   jax.ShapeDtypeStruct((1, D), f32)],
        scratch_shapes=[pltpu.VMEM((NSLAB, Tc + PAD, LANES), f32), pltpu.VMEM((NSLAB, Tc, LANES), f32),
                        pltpu.VMEM((1, W), f32)])
    return (*outs, extra)


NEG = -1e30


def _attn_logits(q_ref, k_ref, fq_ref, fk_ref, qi, ki, bq, scale):
    s = lax.dot_general(q_ref[...], k_ref[...], _DIMS["nt"], preferred_element_type=f32) * scale
    s = s + fq_ref[...] - fk_ref[...]
    rows = qi * bq + lax.broadcasted_iota(jnp.int32, s.shape, 0)
    cols = ki * bq + lax.broadcasted_iota(jnp.int32, s.shape, 1)
    return s, cols <= rows


def _attn_fwd(q, kv, fq, fk, cfg):
    L, D, NH, DH, B = cfg.L, cfg.D, cfg.NH, cfg.DH, cfg.BQ
    nq = L // B
    scale = DH ** -0.5

    def body(q_ref, k_ref, v_ref, fq_ref, fk_ref, o_ref, lse_ref, m_ref, l_ref, acc_ref):
        qi, ki = pl.program_id(1), pl.program_id(2)

        @pl.when(ki == 0)
        def _():
            m_ref[...] = jnp.full_like(m_ref, NEG)
            l_ref[...] = jnp.zeros_like(l_ref)
            acc_ref[...] = jnp.zeros_like(acc_ref)

        @pl.when(ki <= qi)
        def _():
            s, mask = _attn_logits(q_ref, k_ref, fq_ref, fk_ref, qi, ki, B, scale)
            s = jnp.where(mask, s, NEG)
            m_prev = m_ref[...]
            m_new = jnp.maximum(m_prev, jnp.max(s, axis=1, keepdims=True))
            alpha = jnp.exp(m_prev - m_new)
            p = jnp.exp(s - m_new)
            l_ref[...] = alpha * l_ref[...] + jnp.sum(p, axis=1, keepdims=True)
            acc_ref[...] = alpha * acc_ref[...] + jnp.dot(p.astype(bf16), v_ref[...], preferred_element_type=f32)
            m_ref[...] = m_new

        @pl.when(ki == qi)
        def _():
            o_ref[...] = (acc_ref[...] / l_ref[...]).astype(o_ref.dtype)
            lse_ref[...] = m_ref[...] + jnp.log(l_ref[...])

    kmap = lambda h, qi, ki: (jnp.minimum(ki, qi), h)
    vmap_ = lambda h, qi, ki: (jnp.minimum(ki, qi), NH + h)
    return pl.pallas_call(
        body, name="attn_fwd", grid=(NH, nq, nq),
        in_specs=[pl.BlockSpec((B, DH), lambda h, qi, ki: (qi, h)),
                  pl.BlockSpec((B, DH), kmap), pl.BlockSpec((B, DH), vmap_),
                  pl.BlockSpec((None, B, 1), lambda h, qi, ki: (h, qi, 0)),
                  pl.BlockSpec((None, 1, B), lambda h, qi, ki: (h, 0, jnp.minimum(ki, qi)))],
        out_specs=[pl.BlockSpec((B, DH), lambda h, qi, ki: (qi, h)),
                   pl.BlockSpec((None, B, 1), lambda h, qi, ki: (h, qi, 0))],
        out_shape=[jax.ShapeDtypeStruct((L, D), bf16), jax.ShapeDtypeStruct((NH, L, 1), f32)],
        scratch_shapes=[pltpu.VMEM((B, 1), f32), pltpu.VMEM((B, 1), f32), pltpu.VMEM((B, DH), f32)],
        compiler_params=_cp(("parallel", "parallel", "arbitrary")),
    )(q, kv, kv, fq, fk)


def _attn_bwd_dq(q, kv, do, o, lse, fq, fk, cfg):
    L, D, NH, DH, B = cfg.L, cfg.D, cfg.NH, cfg.DH, cfg.BQ
    nq = L // B
    scale = DH ** -0.5

    def body(q_ref, k_ref, v_ref, do_ref, o_ref, lse_ref, fq_ref, fk_ref, dq_ref, dfq_ref, acc_ref, df_ref, dl_ref):
        qi, ki = pl.program_id(1), pl.program_id(2)

        @pl.when(ki == 0)
        def _():
            dl_ref[...] = jnp.sum(do_ref[...].astype(f32) * o_ref[...].astype(f32), axis=1, keepdims=True)
            acc_ref[...] = jnp.zeros_like(acc_ref)
            df_ref[...] = jnp.zeros_like(df_ref)

        @pl.when(ki <= qi)
        def _():
            s, mask = _attn_logits(q_ref, k_ref, fq_ref, fk_ref, qi, ki, B, scale)
            p = jnp.where(mask, jnp.exp(s - lse_ref[...]), 0.0)
            dp = lax.dot_general(do_ref[...], v_ref[...], _DIMS["nt"], preferred_element_type=f32)
            ds = p * (dp - dl_ref[...])
            df_ref[...] += jnp.sum(ds, axis=1, keepdims=True)
            acc_ref[...] += jnp.dot(ds.astype(bf16), k_ref[...], preferred_element_type=f32)

        @pl.when(ki == qi)
        def _():
            dq_ref[...] = (acc_ref[...] * scale).astype(dq_ref.dtype)
            dfq_ref[...] = df_ref[...]

    qmap = lambda h, qi, ki: (qi, h)
    return pl.pallas_call(
        body, name="attn_bwd_dq", grid=(NH, nq, nq),
        in_specs=[pl.BlockSpec((B, DH), qmap),
                  pl.BlockSpec((B, DH), lambda h, qi, ki: (jnp.minimum(ki, qi), h)),
                  pl.BlockSpec((B, DH), lambda h, qi, ki: (jnp.minimum(ki, qi), NH + h)),
                  pl.BlockSpec((B, DH), qmap), pl.BlockSpec((B, DH), qmap),
                  pl.BlockSpec((None, B, 1), lambda h, qi, ki: (h, qi, 0)),
                  pl.BlockSpec((None, B, 1), lambda h, qi, ki: (h, qi, 0)),
                  pl.BlockSpec((None, 1, B), lambda h, qi, ki: (h, 0, jnp.minimum(ki, qi)))],
        out_specs=[pl.BlockSpec((B, DH), qmap), pl.BlockSpec((None, B, 1), lambda h, qi, ki: (h, qi, 0))],
        out_shape=[jax.ShapeDtypeStruct((L, D), bf16), jax.ShapeDtypeStruct((NH, L, 1), f32)],
        scratch_shapes=[pltpu.VMEM((B, DH), f32), pltpu.VMEM((B, 1), f32), pltpu.VMEM((B, 1), f32)],
        compiler_params=_cp(("parallel", "parallel", "arbitrary")),
    )(q, kv, kv, do, o, lse, fq, fk)


def _attn_bwd_dkv(q, kv, do, o, lse, fq, fk, cfg):
    L, D, NH, DH, B = cfg.L, cfg.D, cfg.NH, cfg.DH, cfg.BQ
    nq = L // B
    scale = DH ** -0.5

    def body(q_ref, k_ref, v_ref, do_ref, o_ref, lse_ref, fq_ref, fk_ref, dk_ref, dv_ref, dfk_ref, dka_ref, dva_ref, dfa_ref):
        ki, qi = pl.program_id(1), pl.program_id(2)

        @pl.when(qi == 0)
        def _():
            dka_ref[...] = jnp.zeros_like(dka_ref)
            dva_ref[...] = jnp.zeros_like(dva_ref)
            dfa_ref[...] = jnp.zeros_like(dfa_ref)

        @pl.when(qi >= ki)
        def _():
            do = do_ref[...]
            delta = jnp.sum(do.astype(f32) * o_ref[...].astype(f32), axis=1, keepdims=True)
            s, mask = _attn_logits(q_ref, k_ref, fq_ref, fk_ref, qi, ki, B, scale)
            p = jnp.where(mask, jnp.exp(s - lse_ref[...]), 0.0)
            dva_ref[...] += lax.dot_general(p.astype(bf16), do, _DIMS["tn"], preferred_element_type=f32)
            dp = lax.dot_general(do, v_ref[...], _DIMS["nt"], preferred_element_type=f32)
            ds = p * (dp - delta)
            dka_ref[...] += lax.dot_general(ds.astype(bf16), q_ref[...], _DIMS["tn"], preferred_element_type=f32)
            dfa_ref[...] -= jnp.sum(ds, axis=0, keepdims=True)

        @pl.when(qi == nq - 1)
        def _():
            dk_ref[...] = (dka_ref[...] * scale).astype(dk_ref.dtype)
            dv_ref[...] = dva_ref[...].astype(dv_ref.dtype)
            dfk_ref[...] = dfa_ref[...]

    qmap = lambda h, ki, qi: (jnp.maximum(qi, ki), h)
    fqmap = lambda h, ki, qi: (h, jnp.maximum(qi, ki), 0)
    return pl.pallas_call(
        body, name="attn_bwd_dkv", grid=(NH, nq, nq),
        in_specs=[pl.BlockSpec((B, DH), qmap),
                  pl.BlockSpec((B, DH), lambda h, ki, qi: (ki, h)),
                  pl.BlockSpec((B, DH), lambda h, ki, qi: (ki, NH + h)),
                  pl.BlockSpec((B, DH), qmap), pl.BlockSpec((B, DH), qmap),
                  pl.BlockSpec((None, B, 1), fqmap), pl.BlockSpec((None, B, 1), fqmap),
                  pl.BlockSpec((None, 1, B), lambda h, ki, qi: (h, 0, ki))],
        out_specs=[pl.BlockSpec((B, DH), lambda h, ki, qi: (ki, h)), pl.BlockSpec((B, DH), lambda h, ki, qi: (ki, h)),
                   pl.BlockSpec((None, 1, B), lambda h, ki, qi: (h, 0, ki))],
        out_shape=[jax.ShapeDtypeStruct((L, D), bf16), jax.ShapeDtypeStruct((L, D), bf16),
                   jax.ShapeDtypeStruct((NH, 1, L), f32)],
        scratch_shapes=[pltpu.VMEM((B, DH), f32), pltpu.VMEM((B, DH), f32), pltpu.VMEM((1, B), f32)],
        compiler_params=_cp(("parallel", "parallel", "arbitrary")),
    )(q, kv, kv, do, o, lse, fq, fk)


def _tri_tables(nq, by_key):
    pairs = [(qi, ki) for ki in range(nq) for qi in range(ki, nq)] if by_key else \
            [(qi, ki) for qi in range(nq) for ki in range(qi + 1)]
    return jnp.array([p[0] for p in pairs], jnp.int32), jnp.array([p[1] for p in pairs], jnp.int32)


def _tri_call(body, name, cfg, by_key, in_specs, out_specs, out_shape, scratch_shapes, args, rider=None):
    nq = cfg.L // cfg.BQ
    outs, extra = _host_call(body, name=name, grid=(cfg.NH, nq * (nq + 1) // 2), in_specs=in_specs, out_specs=out_specs,
                             out_shape=out_shape, scratch_shapes=scratch_shapes, args=args,
                             prefetch=_tri_tables(nq, by_key), rider=rider)
    return (*outs, extra)


def _ta_fwd(q, kv, fk, cfg, rider=None):
    L, D, NH, DH, B = cfg.L, cfg.D, cfg.NH, cfg.DH, cfg.BQ
    scale = DH ** -0.5

    def body(qt_ref, kt_ref, q_ref, k_ref, v_ref, fk_ref, o_ref, lse_ref, m_ref, acc_ref, a_ref, s_ref, p_ref):
        pid = pl.program_id(1)
        qi, ki = qt_ref[pid], kt_ref[pid]

        @pl.when(ki == 0)
        def _():
            m_ref[...] = jnp.full_like(m_ref, NEG)
            acc_ref[...] = jnp.zeros_like(acc_ref)

        def compute(masked):
            s_ref[...] = lax.dot_general(q_ref[...], k_ref[...], _DIMS["nt"], preferred_element_type=f32)
            fkv = fk_ref[...]

            def strip(rows, row0, c):
                t = s_ref[rows, :] * scale - fkv
                if masked:
                    t = jnp.where(_fa_mask(row0, t.shape), t, NEG)
                m_prev = m_ref[rows, :]
                m_new = jnp.maximum(m_prev, jnp.max(t, axis=1, keepdims=True))
                m_ref[rows, :] = m_new
                a_ref[rows, :] = jnp.exp(m_prev - m_new)
                p_ref[rows, :] = jnp.exp(t - m_new).astype(bf16)
                return c
            _fa_strips(B, strip, 0)
            v1 = jnp.concatenate([v_ref[...], jnp.ones((B, DH), bf16)], axis=1)
            acc_ref[...] = a_ref[...] * acc_ref[...] + jnp.dot(p_ref[...], v1, preferred_element_type=f32)

        @pl.when(ki < qi)
        def _():
            compute(False)

        @pl.when(ki == qi)
        def _():
            compute(True)
            l = acc_ref[:, DH:]
            o_ref[...] = (acc_ref[:, :DH] / l).astype(o_ref.dtype)
            lse_ref[...] = m_ref[...] + jnp.log(l[:, :1])

    col = pltpu.VMEM((B, 1), f32)
    return _tri_call(
        body, "attn_fwd", cfg, False,
        [pl.BlockSpec((B, DH), lambda h, p, qt, kt: (qt[p], h)),
         pl.BlockSpec((B, DH), lambda h, p, qt, kt: (kt[p], h)),
         pl.BlockSpec((B, DH), lambda h, p, qt, kt: (kt[p], NH + h)),
         pl.BlockSpec((None, 1, B), lambda h, p, qt, kt: (h, 0, kt[p]))],
        [pl.BlockSpec((B, DH), lambda h, p, qt, kt: (qt[p], h)),
         pl.BlockSpec((None, B, 1), lambda h, p, qt, kt: (h, qt[p], 0))],
        [jax.ShapeDtypeStruct((L, D), bf16), jax.ShapeDtypeStruct((NH, L, 1), f32)],
        [col, pltpu.VMEM((B, 2 * DH), f32), col, pltpu.VMEM((B, B), f32), pltpu.VMEM((B, B), bf16)],
        (q, kv, kv, fk), rider)


def _ta_bwd_dq(q, kv, do, o, lse, fk, cfg, rider=None):
    L, D, NH, DH, B = cfg.L, cfg.D, cfg.NH, cfg.DH, cfg.BQ
    scale = DH ** -0.5

    def body(qt_ref, kt_ref, q_ref, k_ref, v_ref, do_ref, o_ref, lse_ref, fk_ref, dq_ref, dfq_ref, dl_ref,
             acc_ref, s_ref, dp_ref, ds_ref):
        pid = pl.program_id(1)
        qi, ki = qt_ref[pid], kt_ref[pid]

        @pl.when(ki == 0)
        def _():
            dl_ref[...] = jnp.sum(do_ref[...].astype(f32) * o_ref[...].astype(f32), axis=1, keepdims=True)
            acc_ref[...] = jnp.zeros_like(acc_ref)

        def compute(masked):
            s_ref[...] = lax.dot_general(q_ref[...], k_ref[...], _DIMS["nt"], preferred_element_type=f32)
            dp_ref[...] = lax.dot_general(do_ref[...], v_ref[...], _DIMS["nt"], preferred_element_type=f32)
            fkv = fk_ref[...]

            def strip(rows, row0, c):
                p = jnp.exp(s_ref[rows, :] * scale - fkv - lse_ref[rows, :])
                if masked:
                    p = jnp.where(_fa_mask(row0, p.shape), p, 0.0)
                ds_ref[rows, :] = (p * (dp_ref[rows, :] - dl_ref[rows, :])).astype(bf16)
                return c
            _fa_strips(B, strip, 0)
            k1 = jnp.concatenate([k_ref[...], jnp.ones((B, DH), bf16)], axis=1)
            acc_ref[...] += jnp.dot(ds_ref[...], k1, preferred_element_type=f32)

        @pl.when(ki < qi)
        def _():
            compute(False)

        @pl.when(ki == qi)
        def _():
            compute(True)
            dq_ref[...] = (acc_ref[:, :DH] * scale).astype(dq_ref.dtype)
            dfq_ref[...] = acc_ref[:, DH:DH + 1]

    qmap = lambda h, p, qt, kt: (qt[p], h)
    cmap = lambda h, p, qt, kt: (h, qt[p], 0)
    return _tri_call(
        body, "attn_bwd_dq", cfg, False,
        [pl.BlockSpec((B, DH), qmap),
         pl.BlockSpec((B, DH), lambda h, p, qt, kt: (kt[p], h)),
         pl.BlockSpec((B, DH), lambda h, p, qt, kt: (kt[p], NH + h)),
         pl.BlockSpec((B, DH), qmap), pl.BlockSpec((B, DH), qmap),
         pl.BlockSpec((None, B, 1), cmap),
         pl.BlockSpec((None, 1, B), lambda h, p, qt, kt: (h, 0, kt[p]))],
        [pl.BlockSpec((B, DH), qmap), pl.BlockSpec((None, B, 1), cmap), pl.BlockSpec((None, B, 1), cmap)],
        [jax.ShapeDtypeStruct((L, D), bf16), jax.ShapeDtypeStruct((NH, L, 1), f32), jax.ShapeDtypeStruct((NH, L, 1), f32)],
        [pltpu.VMEM((B, 2 * DH), f32), pltpu.VMEM((B, B), f32), pltpu.VMEM((B, B), f32), pltpu.VMEM((B, B), bf16)],
        (q, kv, kv, do, o, lse, fk), rider)


def _ta_bwd_dkv(q, kv, do, delta, lse, fk, cfg, rider=None):
    L, D, NH, DH, B = cfg.L, cfg.D, cfg.NH, cfg.DH, cfg.BQ
    nq = L // B
    scale = DH ** -0.5

    def body(qt_ref, kt_ref, q_ref, k_ref, v_ref, do_ref, dl_ref, lse_ref, fk_ref, dk_ref, dv_ref, dfk_ref,
             dka_ref, dva_ref, s_ref, dp_ref, p_ref, ds_ref):
        pid = pl.program_id(1)
        qi, ki = qt_ref[pid], kt_ref[pid]

        @pl.when(qi == ki)
        def _():
            dka_ref[...] = jnp.zeros_like(dka_ref)
            dva_ref[...] = jnp.zeros_like(dva_ref)

        def compute(masked):
            s_ref[...] = lax.dot_general(q_ref[...], k_ref[...], _DIMS["nt"], preferred_element_type=f32)
            dp_ref[...] = lax.dot_general(do_ref[...], v_ref[...], _DIMS["nt"], preferred_element_type=f32)
            fkv = fk_ref[...]

            def strip(rows, row0, c):
                p = jnp.exp(s_ref[rows, :] * scale - fkv - lse_ref[rows, :])
                if masked:
                    p = jnp.where(_fa_mask(row0, p.shape), p, 0.0)
                p_ref[rows, :] = p.astype(bf16)
                ds_ref[rows, :] = (p * (dp_ref[rows, :] - dl_ref[rows, :])).astype(bf16)
                return c
            _fa_strips(B, strip, 0)
            q1 = jnp.concatenate([q_ref[...], jnp.ones((B, DH), bf16)], axis=1)
            dva_ref[...] += lax.dot_general(p_ref[...], do_ref[...], _DIMS["tn"], preferred_element_type=f32)
            dka_ref[...] += lax.dot_general(ds_ref[...], q1, _DIMS["tn"], preferred_element_type=f32)

        @pl.when(qi == ki)
        def _():
            compute(True)

        @pl.when(qi > ki)
        def _():
            compute(False)

        @pl.when(qi == nq - 1)
        def _():
            dk_ref[...] = (dka_ref[:, :DH] * scale).astype(dk_ref.dtype)
            dv_ref[...] = dva_ref[...].astype(dv_ref.dtype)
            dfk_ref[...] = -dka_ref[:, DH:DH + 1]

    qmap = lambda h, p, qt, kt: (qt[p], h)
    cmap = lambda h, p, qt, kt: (h, qt[p], 0)
    kmap = lambda h, p, qt, kt: (kt[p], h)
    return _tri_call(
        body, "attn_bwd_dkv", cfg, True,
        [pl.BlockSpec((B, DH), qmap), pl.BlockSpec((B, DH), kmap),
         pl.BlockSpec((B, DH), lambda h, p, qt, kt: (kt[p], NH + h)),
         pl.BlockSpec((B, DH), qmap), pl.BlockSpec((None, B, 1), cmap), pl.BlockSpec((None, B, 1), cmap),
         pl.BlockSpec((None, 1, B), lambda h, p, qt, kt: (h, 0, kt[p]))],
        [pl.BlockSpec((B, DH), kmap), pl.BlockSpec((B, DH), kmap),
         pl.BlockSpec((None, B, 1), lambda h, p, qt, kt: (h, kt[p], 0))],
        [jax.ShapeDtypeStruct((L, D), bf16), jax.ShapeDtypeStruct((L, D), bf16), jax.ShapeDtypeStruct((NH, L, 1), f32)],
        [pltpu.VMEM((B, 2 * DH), f32), pltpu.VMEM((B, DH), f32), pltpu.VMEM((B, B), f32), pltpu.VMEM((B, B), f32),
         pltpu.VMEM((B, B), bf16), pltpu.VMEM((B, B), bf16)],
        (q, kv, kv, do, delta, lse, fk), rider)


STRIP = 32


def _fa_strips(nrows, fn, init):
    return lax.fori_loop(0, nrows // STRIP, lambda r, c: fn(pl.ds(pl.multiple_of(r * STRIP, STRIP), STRIP), r * STRIP, c),
                         init, unroll=True)


def _fa_mask(row0, shape):
    rows = row0 + lax.broadcasted_iota(jnp.int32, shape, 0)
    cols = lax.broadcasted_iota(jnp.int32, shape, 1)
    return cols <= rows


def _fa_fwd(q, kv, fk, cfg):
    L, D, NH, DH, B = cfg.L, cfg.D, cfg.NH, cfg.DH, cfg.BQ
    nq = L // B
    scale = DH ** -0.5

    def body(q_ref, k_ref, v_ref, fk_ref, o_ref, lse_ref, m_ref, l_ref, acc_ref, a_ref, s_ref, p_ref):
        qi, ki = pl.program_id(1), pl.program_id(2)

        @pl.when(ki == 0)
        def _():
            m_ref[...] = jnp.full_like(m_ref, NEG)
            l_ref[...] = jnp.zeros_like(l_ref)
            acc_ref[...] = jnp.zeros_like(acc_ref)

        def compute(masked):
            s_ref[...] = lax.dot_general(q_ref[...], k_ref[...], _DIMS["nt"], preferred_element_type=f32)
            fkv = fk_ref[...]

            def strip(rows, row0, c):
                t = s_ref[rows, :] * scale - fkv
                if masked:
                    t = jnp.where(_fa_mask(row0, t.shape), t, NEG)
                m_prev = m_ref[rows, :]
                m_new = jnp.maximum(m_prev, jnp.max(t, axis=1, keepdims=True))
                p = jnp.exp(t - m_new)
                alpha = jnp.exp(m_prev - m_new)
                l_ref[rows, :] = alpha * l_ref[rows, :] + jnp.sum(p, axis=1, keepdims=True)
                m_ref[rows, :] = m_new
                a_ref[rows, :] = alpha
                p_ref[rows, :] = p.astype(bf16)
                return c
            _fa_strips(B, strip, 0)
            acc_ref[...] = a_ref[...] * acc_ref[...] + jnp.dot(p_ref[...], v_ref[...], preferred_element_type=f32)

        @pl.when(ki < qi)
        def _():
            compute(False)

        @pl.when(ki == qi)
        def _():
            compute(True)
            o_ref[...] = (acc_ref[...] / l_ref[...]).astype(o_ref.dtype)
            lse_ref[...] = m_ref[...] + jnp.log(l_ref[...])

    col = pltpu.VMEM((B, 1), f32)
    return pl.pallas_call(
        body, name="attn_fwd", grid=(NH, nq, nq),
        in_specs=[pl.BlockSpec((B, DH), lambda h, qi, ki: (qi, h)),
                  pl.BlockSpec((B, DH), lambda h, qi, ki: (jnp.minimum(ki, qi), h)),
                  pl.BlockSpec((B, DH), lambda h, qi, ki: (jnp.minimum(ki, qi), NH + h)),
                  pl.BlockSpec((None, 1, B), lambda h, qi, ki: (h, 0, jnp.minimum(ki, qi)))],
        out_specs=[pl.BlockSpec((B, DH), lambda h, qi, ki: (qi, h)),
                   pl.BlockSpec((None, B, 1), lambda h, qi, ki: (h, qi, 0))],
        out_shape=[jax.ShapeDtypeStruct((L, D), bf16), jax.ShapeDtypeStruct((NH, L, 1), f32)],
        scratch_shapes=[col, col, pltpu.VMEM((B, DH), f32), col, pltpu.VMEM((B, B), f32), pltpu.VMEM((B, B), bf16)],
        compiler_params=_cp(("parallel", "parallel", "arbitrary")),
    )(q, kv, kv, fk)


def _fa_bwd_dq(q, kv, do, o, lse, fk, cfg):
    L, D, NH, DH, B = cfg.L, cfg.D, cfg.NH, cfg.DH, cfg.BQ
    nq = L // B
    scale = DH ** -0.5

    def body(q_ref, k_ref, v_ref, do_ref, o_ref, lse_ref, fk_ref, dq_ref, dfq_ref, dl_ref, acc_ref, df_ref, s_ref, dp_ref, ds_ref):
        qi, ki = pl.program_id(1), pl.program_id(2)

        @pl.when(ki == 0)
        def _():
            dl_ref[...] = jnp.sum(do_ref[...].astype(f32) * o_ref[...].astype(f32), axis=1, keepdims=True)
            acc_ref[...] = jnp.zeros_like(acc_ref)
            df_ref[...] = jnp.zeros_like(df_ref)

        def compute(masked):
            s_ref[...] = lax.dot_general(q_ref[...], k_ref[...], _DIMS["nt"], preferred_element_type=f32)
            dp_ref[...] = lax.dot_general(do_ref[...], v_ref[...], _DIMS["nt"], preferred_element_type=f32)
            fkv = fk_ref[...]

            def strip(rows, row0, c):
                p = jnp.exp(s_ref[rows, :] * scale - fkv - lse_ref[rows, :])
                if masked:
                    p = jnp.where(_fa_mask(row0, p.shape), p, 0.0)
                ds = p * (dp_ref[rows, :] - dl_ref[rows, :])
                df_ref[rows, :] += jnp.sum(ds, axis=1, keepdims=True)
                ds_ref[rows, :] = ds.astype(bf16)
                return c
            _fa_strips(B, strip, 0)
            acc_ref[...] += jnp.dot(ds_ref[...], k_ref[...], preferred_element_type=f32)

        @pl.when(ki < qi)
        def _():
            compute(False)

        @pl.when(ki == qi)
        def _():
            compute(True)
            dq_ref[...] = (acc_ref[...] * scale).astype(dq_ref.dtype)
            dfq_ref[...] = df_ref[...]

    qmap = lambda h, qi, ki: (qi, h)
    cmap = lambda h, qi, ki: (h, qi, 0)
    return pl.pallas_call(
        body, name="attn_bwd_dq", grid=(NH, nq, nq),
        in_specs=[pl.BlockSpec((B, DH), qmap),
                  pl.BlockSpec((B, DH), lambda h, qi, ki: (jnp.minimum(ki, qi), h)),
                  pl.BlockSpec((B, DH), lambda h, qi, ki: (jnp.minimum(ki, qi), NH + h)),
                  pl.BlockSpec((B, DH), qmap), pl.BlockSpec((B, DH), qmap),
                  pl.BlockSpec((None, B, 1), cmap),
                  pl.BlockSpec((None, 1, B), lambda h, qi, ki: (h, 0, jnp.minimum(ki, qi)))],
        out_specs=[pl.BlockSpec((B, DH), qmap), pl.BlockSpec((None, B, 1), cmap), pl.BlockSpec((None, B, 1), cmap)],
        out_shape=[jax.ShapeDtypeStruct((L, D), bf16), jax.ShapeDtypeStruct((NH, L, 1), f32),
                   jax.ShapeDtypeStruct((NH, L, 1), f32)],
        scratch_shapes=[pltpu.VMEM((B, DH), f32), pltpu.VMEM((B, 1), f32), pltpu.VMEM((B, B), f32),
                        pltpu.VMEM((B, B), f32), pltpu.VMEM((B, B), bf16)],
        compiler_params=_cp(("parallel", "parallel", "arbitrary")),
    )(q, kv, kv, do, o, lse, fk)


def _fa_bwd_dkv(q, kv, do, delta, lse, fk, cfg):
    L, D, NH, DH, B = cfg.L, cfg.D, cfg.NH, cfg.DH, cfg.BQ
    nq = L // B
    scale = DH ** -0.5

    def body(q_ref, k_ref, v_ref, do_ref, dl_ref, lse_ref, fk_ref, dk_ref, dv_ref, dfk_ref,
             dka_ref, dva_ref, dfa_ref, s_ref, dp_ref, p_ref, ds_ref):
        ki, qi = pl.program_id(1), pl.program_id(2)

        @pl.when(qi == 0)
        def _():
            dka_ref[...] = jnp.zeros_like(dka_ref)
            dva_ref[...] = jnp.zeros_like(dva_ref)
            dfa_ref[...] = jnp.zeros_like(dfa_ref)

        def compute(masked):
            s_ref[...] = lax.dot_general(q_ref[...], k_ref[...], _DIMS["nt"], preferred_element_type=f32)
            dp_ref[...] = lax.dot_general(do_ref[...], v_ref[...], _DIMS["nt"], preferred_element_type=f32)
            fkv = fk_ref[...]

            def strip(rows, row0, cs):
                p = jnp.exp(s_ref[rows, :] * scale - fkv - lse_ref[rows, :])
                if masked:
                    p = jnp.where(_fa_mask(row0, p.shape), p, 0.0)
                ds = p * (dp_ref[rows, :] - dl_ref[rows, :])
                p_ref[rows, :] = p.astype(bf16)
                ds_ref[rows, :] = ds.astype(bf16)
                return cs + ds
            cs = _fa_strips(B, strip, jnp.zeros((STRIP, B), f32))
            dva_ref[...] += lax.dot_general(p_ref[...], do_ref[...], _DIMS["tn"], preferred_element_type=f32)
            dka_ref[...] += lax.dot_general(ds_ref[...], q_ref[...], _DIMS["tn"], preferred_element_type=f32)
            dfa_ref[...] -= jnp.sum(cs, axis=0, keepdims=True)

        @pl.when(qi == ki)
        def _():
            compute(True)

        @pl.when(qi > ki)
        def _():
            compute(False)

        @pl.when(qi == nq - 1)
        def _():
            dk_ref[...] = (dka_ref[...] * scale).astype(dk_ref.dtype)
            dv_ref[...] = dva_ref[...].astype(dv_ref.dtype)
            dfk_ref[...] = dfa_ref[...]

    qmap = lambda h, ki, qi: (jnp.maximum(qi, ki), h)
    cmap = lambda h, ki, qi: (h, jnp.maximum(qi, ki), 0)
    return pl.pallas_call(
        body, name="attn_bwd_dkv", grid=(NH, nq, nq),
        in_specs=[pl.BlockSpec((B, DH), qmap),
                  pl.BlockSpec((B, DH), lambda h, ki, qi: (ki, h)),
                  pl.BlockSpec((B, DH), lambda h, ki, qi: (ki, NH + h)),
                  pl.BlockSpec((B, DH), qmap),
                  pl.BlockSpec((None, B, 1), cmap), pl.BlockSpec((None, B, 1), cmap),
                  pl.BlockSpec((None, 1, B), lambda h, ki, qi: (h, 0, ki))],
        out_specs=[pl.BlockSpec((B, DH), lambda h, ki, qi: (ki, h)), pl.BlockSpec((B, DH), lambda h, ki, qi: (ki, h)),
                   pl.BlockSpec((None, 1, B), lambda h, ki, qi: (h, 0, ki))],
        out_shape=[jax.ShapeDtypeStruct((L, D), bf16), jax.ShapeDtypeStruct((L, D), bf16),
                   jax.ShapeDtypeStruct((NH, 1, L), f32)],
        scratch_shapes=[pltpu.VMEM((B, DH), f32), pltpu.VMEM((B, DH), f32), pltpu.VMEM((1, B), f32),
                        pltpu.VMEM((B, B), f32), pltpu.VMEM((B, B), f32), pltpu.VMEM((B, B), bf16), pltpu.VMEM((B, B), bf16)],
        compiler_params=_cp(("parallel", "parallel", "arbitrary")),
    )(q, kv, kv, do, delta, lse, fk)


def _fox_logits(q, k, fqv, fkv, scale, masked):
    s = lax.dot_general(q, k, _DIMS["nt"], preferred_element_type=f32) * scale + fqv - fkv
    if masked:
        rows = lax.broadcasted_iota(jnp.int32, s.shape, 0)
        cols = lax.broadcasted_iota(jnp.int32, s.shape, 1)
        return s, cols <= rows
    return s, None


def _fox_fwd(q, kv, fq, fk, cfg):
    L, D, NH, DH, B = cfg.L, cfg.D, cfg.NH, cfg.DH, cfg.BQ
    nq = L // B
    scale = DH ** -0.5

    def body(q_ref, k_ref, v_ref, fq_ref, fk_ref, o_ref, lse_ref):
        qi = pl.program_id(1)
        qv, fqv = q_ref[...], fq_ref[...]

        def chunk(kj, carry, masked):
            m, l, acc = carry
            rows = pl.ds(pl.multiple_of(kj * B, B), B)
            s, mask = _fox_logits(qv, k_ref[rows, :], fqv, fk_ref[kj], scale, masked)
            if masked:
                s = jnp.where(mask, s, NEG)
            m_new = jnp.maximum(m, jnp.max(s, axis=1, keepdims=True))
            alpha = jnp.exp(m - m_new)
            p = jnp.exp(s - m_new)
            l = alpha * l + jnp.sum(p, axis=1, keepdims=True)
            acc = alpha * acc + jnp.dot(p.astype(bf16), v_ref[rows, :], preferred_element_type=f32)
            return m_new, l, acc

        init = (jnp.full((B, 1), NEG, f32), jnp.zeros((B, 1), f32), jnp.zeros((B, DH), f32))
        carry = lax.fori_loop(0, qi, lambda kj, c: chunk(kj, c, False), init)
        m, l, acc = chunk(qi, carry, True)
        o_ref[...] = (acc / l).astype(o_ref.dtype)
        lse_ref[...] = m + jnp.log(l)

    return pl.pallas_call(
        body, name="attn_fwd", grid=(NH, nq),
        in_specs=[pl.BlockSpec((B, DH), lambda h, qi: (qi, h)),
                  pl.BlockSpec((L, DH), lambda h, qi: (0, h)), pl.BlockSpec((L, DH), lambda h, qi: (0, NH + h)),
                  pl.BlockSpec((None, B, 1), lambda h, qi: (h, qi, 0)),
                  pl.BlockSpec((None, nq, 1, B), lambda h, qi: (h, 0, 0, 0))],
        out_specs=[pl.BlockSpec((B, DH), lambda h, qi: (qi, h)), pl.BlockSpec((None, B, 1), lambda h, qi: (h, qi, 0))],
        out_shape=[jax.ShapeDtypeStruct((L, D), bf16), jax.ShapeDtypeStruct((NH, L, 1), f32)],
        compiler_params=_cp(("parallel", "arbitrary")),
    )(q, kv, kv, fq, fk)


def _fox_bwd_dq(q, kv, do, o, lse, fq, fk, cfg):
    L, D, NH, DH, B = cfg.L, cfg.D, cfg.NH, cfg.DH, cfg.BQ
    nq = L // B
    scale = DH ** -0.5

    def body(q_ref, k_ref, v_ref, do_ref, o_ref, lse_ref, fq_ref, fk_ref, dq_ref, dfq_ref, dl_ref):
        qi = pl.program_id(1)
        qv, fqv, dov, lsev = q_ref[...], fq_ref[...], do_ref[...], lse_ref[...]
        delta = jnp.sum(dov.astype(f32) * o_ref[...].astype(f32), axis=1, keepdims=True)

        def chunk(kj, carry, masked):
            acc, df = carry
            rows = pl.ds(pl.multiple_of(kj * B, B), B)
            kv_ = k_ref[rows, :]
            s, mask = _fox_logits(qv, kv_, fqv, fk_ref[kj], scale, masked)
            p = jnp.exp(s - lsev)
            if masked:
                p = jnp.where(mask, p, 0.0)
            dp = lax.dot_general(dov, v_ref[rows, :], _DIMS["nt"], preferred_element_type=f32)
            ds = p * (dp - delta)
            return acc + jnp.dot(ds.astype(bf16), kv_, preferred_element_type=f32), df + jnp.sum(ds, axis=1, keepdims=True)

        carry = lax.fori_loop(0, qi, lambda kj, c: chunk(kj, c, False), (jnp.zeros((B, DH), f32), jnp.zeros((B, 1), f32)))
        acc, df = chunk(qi, carry, True)
        dq_ref[...] = (acc * scale).astype(dq_ref.dtype)
        dfq_ref[...] = df
        dl_ref[...] = delta

    qmap = lambda h, qi: (qi, h)
    cmap = lambda h, qi: (h, qi, 0)
    return pl.pallas_call(
        body, name="attn_bwd_dq", grid=(NH, nq),
        in_specs=[pl.BlockSpec((B, DH), qmap),
                  pl.BlockSpec((L, DH), lambda h, qi: (0, h)), pl.BlockSpec((L, DH), lambda h, qi: (0, NH + h)),
                  pl.BlockSpec((B, DH), qmap), pl.BlockSpec((B, DH), qmap),
                  pl.BlockSpec((None, B, 1), cmap), pl.BlockSpec((None, B, 1), cmap),
                  pl.BlockSpec((None, nq, 1, B), lambda h, qi: (h, 0, 0, 0))],
        out_specs=[pl.BlockSpec((B, DH), qmap), pl.BlockSpec((None, B, 1), cmap), pl.BlockSpec((None, B, 1), cmap)],
        out_shape=[jax.ShapeDtypeStruct((L, D), bf16), jax.ShapeDtypeStruct((NH, L, 1), f32),
                   jax.ShapeDtypeStruct((NH, L, 1), f32)],
        compiler_params=_cp(("parallel", "arbitrary")),
    )(q, kv, kv, do, o, lse, fq, fk)


def _fox_bwd_dkv(q, kv, do, delta, lse, fq, fk, cfg):
    L, D, NH, DH, B = cfg.L, cfg.D, cfg.NH, cfg.DH, cfg.BQ
    nq = L // B
    scale = DH ** -0.5

    def body(q_ref, k_ref, v_ref, do_ref, dl_ref, lse_ref, fq_ref, fk_ref, dk_ref, dv_ref, dfk_ref):
        ki = pl.program_id(1)
        kv_, vv, fkv = k_ref[...], v_ref[...], fk_ref[...]

        def block(qj, carry, masked):
            dk, dv, df = carry
            rows = pl.ds(pl.multiple_of(qj * B, B), B)
            qv, dov = q_ref[rows, :], do_ref[rows, :]
            s, mask = _fox_logits(qv, kv_, fq_ref[rows, :], fkv, scale, masked)
            p = jnp.exp(s - lse_ref[rows, :])
            if masked:
                p = jnp.where(mask, p, 0.0)
            dv = dv + lax.dot_general(p.astype(bf16), dov, _DIMS["tn"], preferred_element_type=f32)
            dp = lax.dot_general(dov, vv, _DIMS["nt"], preferred_element_type=f32)
            ds = p * (dp - dl_ref[rows, :])
            dk = dk + lax.dot_general(ds.astype(bf16), qv, _DIMS["tn"], preferred_element_type=f32)
            return dk, dv, df - jnp.sum(ds, axis=0, keepdims=True)

        init = (jnp.zeros((B, DH), f32), jnp.zeros((B, DH), f32), jnp.zeros((1, B), f32))
        carry = block(ki, init, True)
        dk, dv, df = lax.fori_loop(ki + 1, nq, lambda qj, c: block(qj, c, False), carry)
        dk_ref[...] = (dk * scale).astype(dk_ref.dtype)
        dv_ref[...] = dv.astype(dv_ref.dtype)
        dfk_ref[...] = df

    whole = lambda h, ki: (0, h)
    col = lambda h, ki: (h, 0, 0)
    return pl.pallas_call(
        body, name="attn_bwd_dkv", grid=(NH, nq),
        in_specs=[pl.BlockSpec((L, DH), whole),
                  pl.BlockSpec((B, DH), lambda h, ki: (ki, h)), pl.BlockSpec((B, DH), lambda h, ki: (ki, NH + h)),
                  pl.BlockSpec((L, DH), whole),
                  pl.BlockSpec((None, L, 1), col), pl.BlockSpec((None, L, 1), col), pl.BlockSpec((None, L, 1), col),
                  pl.BlockSpec((None, None, 1, B), lambda h, ki: (h, ki, 0, 0))],
        out_specs=[pl.BlockSpec((B, DH), lambda h, ki: (ki, h)), pl.BlockSpec((B, DH), lambda h, ki: (ki, h)),
                   pl.BlockSpec((None, None, 1, B), lambda h, ki: (h, ki, 0, 0))],
        out_shape=[jax.ShapeDtypeStruct((L, D), bf16), jax.ShapeDtypeStruct((L, D), bf16),
                   jax.ShapeDtypeStruct((NH, nq, 1, B), f32)],
        compiler_params=_cp(("parallel", "arbitrary")),
    )(q, kv, kv, do, delta, lse, fq, fk)


FCH = 256


def _split3(x):
    hi = x.astype(bf16)
    r1 = x - hi.astype(f32)
    mid = r1.astype(bf16)
    lo = (r1 - mid.astype(f32)).astype(bf16)
    return hi, mid, lo


def _tri_sum(tri, x):
    hi, mid, lo = _split3(x)
    return (jnp.dot(tri, hi, preferred_element_type=f32) + jnp.dot(tri, mid, preferred_element_type=f32)
            + jnp.dot(tri, lo, preferred_element_type=f32))


def _fgate_fwd(z, fb, cfg):
    L = cfg.L

    def body(z_ref, fb_ref, f_ref):
        r = lax.broadcasted_iota(jnp.int32, (FCH, FCH), 0)
        c = lax.broadcasted_iota(jnp.int32, (FCH, FCH), 1)
        tri = (c <= r).astype(bf16)
        carry = jnp.zeros((1, LANES), f32)
        for ch in range(L // FCH):
            x = z_ref[pl.ds(ch * FCH, FCH), :] + fb_ref[...]
            lf = jnp.minimum(x, 0.0) - jnp.log(1.0 + jnp.exp(-jnp.abs(x)))
            f_ref[pl.ds(ch * FCH, FCH), :] = _tri_sum(tri, lf) + carry
            carry = f_ref[pl.ds(ch * FCH + FCH - 1, 1), :]

    vm = pl.BlockSpec(memory_space=pltpu.VMEM)
    return pl.pallas_call(body, name="fgate_fwd", in_specs=[vm, vm], out_specs=vm,
                          out_shape=jax.ShapeDtypeStruct((L, LANES), f32), compiler_params=_cp())(z, fb)


def _fgate_bwd(df, z, fb, cfg):
    L = cfg.L

    def body(df_ref, z_ref, fb_ref, dz_ref, db_ref):
        r = lax.broadcasted_iota(jnp.int32, (FCH, FCH), 0)
        c = lax.broadcasted_iota(jnp.int32, (FCH, FCH), 1)
        tri = (c >= r).astype(bf16)
        carry = jnp.zeros((1, LANES), f32)
        dbs = jnp.zeros((1, LANES), f32)
        for ch in range(L // FCH - 1, -1, -1):
            suf = _tri_sum(tri, df_ref[pl.ds(ch * FCH, FCH), :]) + carry
            x = z_ref[pl.ds(ch * FCH, FCH), :] + fb_ref[...]
            dz = suf * _sigmoid(-x)
            dz_ref[pl.ds(ch * FCH, FCH), :] = dz
            dbs = dbs + _colsum(dz)
            carry = carry + _colsum(df_ref[pl.ds(ch * FCH, FCH), :])
        db_ref[...] = dbs

    vm = pl.BlockSpec(memory_space=pltpu.VMEM)
    return pl.pallas_call(body, name="fgate_bwd", in_specs=[vm, vm, vm], out_specs=[vm, vm],
                          out_shape=[jax.ShapeDtypeStruct((L, LANES), f32), jax.ShapeDtypeStruct((1, LANES), f32)],
                          compiler_params=_cp())(df, z, fb)


def _adamw(w, g, m, v, name):
    R, C = w.shape
    c1 = 1.0 - ADAM_B1 ** ADAM_STEP
    c2 = 1.0 - ADAM_B2 ** ADAM_STEP

    def fn(i, ni, wv, gv, mv, vv):
        mn = ADAM_B1 * mv + (1.0 - ADAM_B1) * gv
        vn = ADAM_B2 * vv + (1.0 - ADAM_B2) * (gv * gv)
        delta = -ADAM_LR * ((mn / c1) / (jnp.sqrt(vn / c2) + ADAM_EPS) + ADAM_WD * wv)
        return [delta, mn, vn], []
    tc = C if C % LANES else _tile(C, 1024)
    return _rowwise(fn, [(w, "rc"), (g, "rc"), (m, "rc"), (v, "rc")], [f32, f32, f32], [], L=R, C=C, tl=512, tc=tc, name=name)


def _sum_lead(x, out_dtype, name):
    n, R, C = x.shape
    tl = _tile(R, 512, HALO)
    tc = C if C % LANES else _tile(C, 1024)

    def body(x_ref, o_ref):
        acc = x_ref[0].astype(f32)
        for k in range(1, n):
            acc = acc + x_ref[k].astype(f32)
        o_ref[...] = acc.astype(o_ref.dtype)

    return pl.pallas_call(
        body, name=name, grid=(R // tl, C // tc),
        in_specs=[pl.BlockSpec((n, tl, tc), lambda i, j: (0, i, j))], out_specs=pl.BlockSpec((tl, tc), lambda i, j: (i, j)),
        out_shape=jax.ShapeDtypeStruct((R, C), out_dtype), compiler_params=_cp(("parallel", "parallel")),
    )(x)


def _add2(a, b, out_dtype, name):
    R, C = a.shape

    def fn(i, ni, av, bv):
        return [av.astype(f32) + bv.astype(f32)], []
    tc = C if C % LANES else _tile(C, 1024)
    return _rowwise(fn, [(a, "rc"), (b, "rc")], [out_dtype], [], L=R, C=C, tl=512, tc=tc, name=name)[0]


ANY = pl.BlockSpec(memory_space=pl.ANY)
LOCAL_CHUNKS = 4


def _place():
    x, y, c = lax.axis_index("x"), lax.axis_index("y"), lax.axis_index("c")
    return x, y, c


def _allgather8(blocks, name):
    return _run_rider(_gather_rider(blocks), name)


def _gather_rider(blocks, middle_at=(1, 2)):
    n = len(blocks)

    def steps(ins, outs, sems):
        send_sems, recv_sems, local_sems = sems
        x, y, c = _place()
        me, sibling = (x, y, c), (x, y, 1 - c)
        chips = [(1 - x, y), (x, 1 - y), (1 - x, 1 - y)]

        def slot(a, dev):
            return outs[a].at[4 * dev[0] + 2 * dev[1] + dev[2]]

        def copy(a, k, block, to, src=None):
            return pltpu.make_async_remote_copy(
                src_ref=slot(a, block) if src is None else src, dst_ref=slot(a, block),
                send_sem=send_sems.at[a * 7 + k], recv_sem=recv_sems.at[a * 7 + k], device_id=to, device_id_type=MESH)

        def mine():
            out = []
            for a in range(n):
                rows = blocks[a].shape[0]
                k = LOCAL_CHUNKS if rows % (LOCAL_CHUNKS * HALO) == 0 else 1
                for i in range(k):
                    piece = pl.ds(i * (rows // k), rows // k)
                    out.append(pltpu.make_async_copy(ins[a].at[piece], slot(a, me).at[piece], local_sems.at[a * LOCAL_CHUNKS + i]))
            return out

        def first():
            out = []
            for a in range(n):
                out.append(copy(a, 0, me, sibling, src=ins[a]))
                out += [copy(a, 1 + j, me, (*chip, c), src=ins[a]) for j, chip in enumerate(chips)]
            return out

        def passed():
            return [copy(a, 4 + j, (*chip, c), sibling) for j, chip in enumerate(chips) for a in range(n)]

        def start():
            for cp in mine() + first():
                cp.start()

        def middle():
            for j, chip in enumerate(chips):
                for a in range(n):
                    copy(a, 1 + j, (*chip, c), me).wait_recv()
                    copy(a, 4 + j, (*chip, c), sibling).start()

        def finish():
            for a in range(n):
                copy(a, 0, sibling, me).wait_recv()
            for j, chip in enumerate(chips):
                for a in range(n):
                    copy(a, 4 + j, (*chip, 1 - c), me).wait_recv()
            for cp in first() + passed():
                cp.wait_send()
            for cp in mine():
                cp.wait()
        return start, middle, finish

    return dict(ins=list(blocks), out_shapes=[jax.ShapeDtypeStruct((N_DEV,) + b.shape, b.dtype) for b in blocks],
                sems=[pltpu.SemaphoreType.DMA((7 * n,)), pltpu.SemaphoreType.DMA((7 * n,)),
                      pltpu.SemaphoreType.DMA((LOCAL_CHUNKS * n,))],
                steps=steps, middle_at=middle_at)


def _run_rider(rider, name):
    ni, no = len(rider["ins"]), len(rider["out_shapes"])

    def body(*refs):
        start, middle, finish = rider["steps"](refs[:ni], refs[ni:ni + no], refs[ni + no:])
        start()
        if middle is not None:
            middle()
        finish()

    outs = pl.pallas_call(body, name=name, in_specs=[ANY] * ni, out_specs=[ANY] * no, out_shape=rider["out_shapes"],
                          scratch_shapes=rider["sems"])(*rider["ins"])
    return list(outs)


def _host_call(body, *, name, grid, in_specs, out_specs, out_shape, scratch_shapes, args, prefetch=(), rider=None):
    npre, nin, nout, nscr = len(prefetch), len(in_specs), len(out_specs), len(scratch_shapes)
    r_in, r_out, r_scr = (rider["ins"], rider["out_shapes"], rider["sems"]) if rider else ([], [], [])
    nri, nro = len(r_in), len(r_out)

    def kern(*refs):
        pre, rest = refs[:npre], refs[npre:]
        cin, rin = rest[:nin], rest[nin:nin + nri]
        o0 = nin + nri
        cout, rout = rest[o0:o0 + nout], rest[o0 + nout:o0 + nout + nro]
        s0 = o0 + nout + nro
        cscr, rscr = rest[s0:s0 + nscr], rest[s0 + nscr:]
        if rider:
            ids = [pl.program_id(d) for d in range(len(grid))]
            rest_zero = functools.reduce(jnp.logical_and, [i == 0 for i in ids[1:]], True)
            start, middle, finish = rider["steps"](rin, rout, rscr)
            pl.when(jnp.logical_and(ids[0] == 0, rest_zero))(start)
            if middle is not None:
                num, den = rider.get("middle_at", (1, 2))
                pl.when(jnp.logical_and(ids[0] == grid[0] * num // den, rest_zero))(middle)
        body(*pre, *cin, *cout, *cscr)
        if rider:
            pl.when(functools.reduce(jnp.logical_and, [i == g - 1 for i, g in zip(ids, grid)]))(finish)

    res = pl.pallas_call(
        kern, name=name,
        grid_spec=pltpu.PrefetchScalarGridSpec(num_scalar_prefetch=npre, grid=grid, in_specs=list(in_specs) + [ANY] * nri,
                                               out_specs=list(out_specs) + [ANY] * nro,
                                               scratch_shapes=list(scratch_shapes) + list(r_scr)),
        out_shape=list(out_shape) + list(r_out),
        compiler_params=_cp(("arbitrary",) * len(grid) if rider else ("parallel",) + ("arbitrary",) * (len(grid) - 1)),
    )(*prefetch, *args, *r_in)
    return list(res[:nout]), list(res[nout:])


def _sibling_send(halves, name):
    n = len(halves)

    def body(*refs):
        ins, outs = refs[:n], refs[n:2 * n]
        send_sems, recv_sems = refs[2 * n:]
        x, y, c = _place()
        sends = [pltpu.make_async_remote_copy(src_ref=ins[a], dst_ref=outs[a], send_sem=send_sems.at[a],
                                              recv_sem=recv_sems.at[a], device_id=(x, y, 1 - c), device_id_type=MESH)
                 for a in range(n)]
        for cp in sends:
            cp.start()
        for cp in sends:
            cp.wait_recv()
        for cp in sends:
            cp.wait_send()

    outs = pl.pallas_call(
        body, name=name, in_specs=[ANY] * n, out_specs=[ANY] * n,
        out_shape=[jax.ShapeDtypeStruct(h.shape, h.dtype) for h in halves],
        scratch_shapes=[pltpu.SemaphoreType.DMA((n,)), pltpu.SemaphoreType.DMA((n,))],
    )(*halves)
    return list(outs)


def _sibling_swap_halves(grads, name):
    n = len(grads)

    def body(*refs):
        ins, outs = refs[:n], refs[n:2 * n]
        send_sems, recv_sems = refs[2 * n:]
        x, y, c = _place()
        sends = [pltpu.make_async_remote_copy(src_ref=ins[a].at[:, 1 - c], dst_ref=outs[a], send_sem=send_sems.at[a],
                                              recv_sem=recv_sems.at[a], device_id=(x, y, 1 - c), device_id_type=MESH)
                 for a in range(n)]
        for cp in sends:
            cp.start()
        for cp in sends:
            cp.wait_recv()
        for cp in sends:
            cp.wait_send()

    outs = pl.pallas_call(
        body, name=name, in_specs=[ANY] * n, out_specs=[ANY] * n,
        out_shape=[jax.ShapeDtypeStruct((4,) + g.shape[2:], g.dtype) for g in grads],
        scratch_shapes=[pltpu.SemaphoreType.DMA((n,)), pltpu.SemaphoreType.DMA((n,))],
    )(*grads)
    return list(outs)


def _chip_scatter(parts, name):
    return _run_rider(_scatter_rider(parts), name)


def _scatter_rider(parts):
    n = len(parts)

    def steps(ins, outs, sems):
        send_sems, recv_sems = sems
        x, y, c = _place()
        chips = [(1 - x, y), (x, 1 - y), (1 - x, 1 - y)]

        def sends():
            return [pltpu.make_async_remote_copy(
                src_ref=ins[a].at[2 * px + py], dst_ref=outs[a].at[j], send_sem=send_sems.at[a * 3 + j],
                recv_sem=recv_sems.at[a * 3 + j], device_id=(px, py, c), device_id_type=MESH)
                for a in range(n) for j, (px, py) in enumerate(chips)]

        def start():
            for cp in sends():
                cp.start()

        def finish():
            for cp in sends():
                cp.wait_recv()
            for cp in sends():
                cp.wait_send()
        return start, None, finish

    return dict(ins=list(parts), out_shapes=[jax.ShapeDtypeStruct((3,) + p.shape[1:], p.dtype) for p in parts],
                sems=[pltpu.SemaphoreType.DMA((3 * n,)), pltpu.SemaphoreType.DMA((3 * n,))], steps=steps)


def _sum_parts(own, got, chip, name):
    _, R, C = own.shape
    tl = _tile(R, 512, HALO)
    tc = C if C % LANES else _tile(C, 1024)

    def body(chip_ref, own_ref, got_ref, o_ref):
        acc = own_ref[...].astype(f32)
        for k in range(3):
            acc = acc + got_ref[k].astype(f32)
        o_ref[...] = acc

    return pl.pallas_call(
        body, name=name,
        grid_spec=pltpu.PrefetchScalarGridSpec(
            num_scalar_prefetch=1, grid=(R // tl, C // tc),
            in_specs=[pl.BlockSpec((None, tl, tc), lambda i, j, ch: (ch[0], i, j)),
                      pl.BlockSpec((3, tl, tc), lambda i, j, ch: (0, i, j))],
            out_specs=pl.BlockSpec((tl, tc), lambda i, j, ch: (i, j))),
        out_shape=jax.ShapeDtypeStruct((R, C), f32), compiler_params=_cp(("parallel", "parallel")),
    )(chip, own, got)


def _adamw_halves(w, m, v, g_mine, g_other, core, name):
    NL, R, C = w.shape
    r = R // 2
    tl = _tile(r, 512, HALO)
    tc = C if C % LANES else _tile(C, 1024)
    nh = r // tl
    c1 = 1.0 - ADAM_B1 ** ADAM_STEP
    c2 = 1.0 - ADAM_B2 ** ADAM_STEP

    def body(core_ref, w_ref, m_ref, v_ref, gm_ref, go_ref, g_out, d_out, m_out, v_out):
        i = pl.program_id(1)
        mine = lax.div(i, nh) == core_ref[0]
        gv = jnp.where(mine, gm_ref[...], go_ref[...])
        mn = ADAM_B1 * m_ref[...] + (1.0 - ADAM_B1) * gv
        vn = ADAM_B2 * v_ref[...] + (1.0 - ADAM_B2) * (gv * gv)
        g_out[...] = gv
        d_out[...] = -ADAM_LR * ((mn / c1) / (jnp.sqrt(vn / c2) + ADAM_EPS) + ADAM_WD * w_ref[...])
        m_out[...] = mn
        v_out[...] = vn

    full = pl.BlockSpec((None, tl, tc), lambda l, i, j, co: (l, i, j))
    mine_spec = pl.BlockSpec((None, tl, tc), lambda l, i, j, co: (l, jnp.clip(i - co[0] * nh, 0, nh - 1), j))
    other_spec = pl.BlockSpec((None, tl, tc), lambda l, i, j, co: (l, jnp.clip(i - (1 - co[0]) * nh, 0, nh - 1), j))
    return pl.pallas_call(
        body, name=name,
        grid_spec=pltpu.PrefetchScalarGridSpec(
            num_scalar_prefetch=1, grid=(NL, R // tl, C // tc),
            in_specs=[full, full, full, mine_spec, other_spec], out_specs=[full] * 4),
        out_shape=[jax.ShapeDtypeStruct((NL, R, C), f32)] * 4,
        compiler_params=_cp(("parallel", "parallel", "parallel")),
    )(core, w, m, v, g_mine, g_other)


def _s5_discretize(log_step, a_re, a_im, b_re, b_im):
    step = jnp.exp(log_step)[:, None]
    mag = jnp.exp(a_re * step)
    abar_re = mag * jnp.cos(a_im * step)
    abar_im = mag * jnp.sin(a_im * step)
    den = a_re * a_re + a_im * a_im
    nr = abar_re - 1.0
    fr = (nr * a_re + abar_im * a_im) / den
    fi = (abar_im * a_re - nr * a_im) / den
    bbar_re = fr[..., None] * b_re - fi[..., None] * b_im
    bbar_im = fr[..., None] * b_im + fi[..., None] * b_re
    return abar_re, abar_im, bbar_re, bbar_im


def _s5_prepare(p, cfg):
    abar_re, abar_im, bbar_re, bbar_im = _s5_discretize(p["log_step"], p["a_re"], p["a_im"], p["b_re"], p["b_im"])
    step = jnp.exp(p["log_step"])[:, None]
    arow, tab = _s5_tables(abar_re, abar_im, p["a_re"], p["a_im"], step, cfg)
    bmat, cmat = _s5_mats(bbar_re, bbar_im, p["c_re"], p["c_im"], cfg)
    return dict(arow=arow, tab=tab, bmat=bmat, cmat=cmat, drow=p["d"].reshape(1, cfg.D))


def _s5_param_grads(p, dbmat, dcmat, dabar, dd, cfg):
    J, P = cfg.G // 8, cfg.P
    dbb_re, dbb_im, dc_re, dc_im = _s5_unmats(dbmat, dcmat, cfg)
    da = dabar.reshape(J, 2, 8, P)
    da_re, da_im = da[:, 0].reshape(cfg.G, P), da[:, 1].reshape(cfg.G, P)
    _, vjp = jax.vjp(_s5_discretize, p["log_step"], p["a_re"], p["a_im"], p["b_re"], p["b_im"])
    dls, dare, daim, dbre, dbim = vjp((da_re, da_im, dbb_re, dbb_im))
    return dict(log_step=dls, a_re=dare, a_im=daim, b_re=dbre, b_im=dbim, c_re=dc_re, c_im=dc_im, d=dd.reshape(cfg.G, cfg.H))


def _resid_epi(acc, xv, gv):
    return xv + gv * acc, acc


def _ffn_fwd(x_in, g_norm, sc, sh, gate, W, exch, conv_w, conv_b, cfg, tag):
    h = _norm_mod_fwd(x_in, g_norm, sc, sh, cfg, f"ffn_norm_{tag}")
    rider = exch.rider(f"ffn_up_{tag}")
    a = _mm(h, W[f"ffn_w_up{tag}"], mode="nn", b4=True, tn=1408, out_dtypes=(bf16,), name=f"ffn_up_{tag}", rider=rider)
    if rider:
        a, extra = a
        W.update(exch.done(f"ffn_up_{tag}", extra))
    act = _conv_act_fwd(a, conv_w, conv_b, cfg)
    rider = exch.rider(f"ffn_down_{tag}")
    res = _mm(act, W[f"ffn_w_down{tag}"], mode="nn", extras=[(x_in, "mn"), (gate, "n")], epi=_resid_epi,
              out_dtypes=(f32, bf16), name=f"ffn_down_{tag}", rider=rider)
    if rider:
        res, extra = res
        W.update(exch.done(f"ffn_down_{tag}", extra))
    x_out, out = res
    return x_out, dict(h=h, a=a, act=act, out=out)


def _ffn_bwd(dx, x_in, sv, g_norm, sc, gate, w_up4, w_down, conv_w, conv_b, cfg, tag):
    F = cfg.F
    dout, dgate = _gate_bwd(dx, sv["out"], gate, cfg, f"ffn_gate_bwd_{tag}")
    dact = _mm(dout, w_down, mode="nt", tn=1408, out_dtypes=(bf16,), name=f"ffn_dact_{tag}")
    dw_down = _mm(sv["act"], dout, mode="tn", tm=1408, out_dtypes=(bf16,), name=f"ffn_dwdown_{tag}")
    dcu, dcv, dwu, dwv, dbu, dbv = _conv_act_bwd1(dact, sv["a"], conv_w, conv_b, cfg)
    dau = _conv_bwd2(dcu, conv_w[:, :F], cfg, f"conv_bwd2u_{tag}")
    dav = _conv_bwd2(dcv, conv_w[:, F:], cfg, f"conv_bwd2v_{tag}")
    da = jnp.concatenate([dau, dav], axis=1)
    dh = _mm(da, w_up4, mode="nt", b4=True, tk=1408, out_dtypes=(bf16,), name=f"ffn_dh_{tag}")
    dw_up = _mm(sv["h"], da, mode="tn", out4=True, tn=1408, out_dtypes=(bf16,), name=f"ffn_dwup_{tag}")
    dx_in, A, B = _norm_mod_bwd(dh, x_in, g_norm, sc, dx, cfg, f"ffn_norm_bwd_{tag}")
    small = dict(norm_g=(1.0 + sc) * A, sc=g_norm * A, sh=B, gate=dgate,
                 conv_w=jnp.concatenate([dwu, dwv], axis=1), conv_b=jnp.concatenate([dbu, dbv], axis=1))
    return dx_in, dw_up, dw_down, small


class _NoExchange:
    def rider(self, key, grads=None):
        return None

    def done(self, key, extra):
        return {}


def _local_step(cfg, x, tgt, mod, W, sp, exch=None):
    D, NH = cfg.D, cfg.NH
    exch = exch or _NoExchange()
    W, big = dict(W), {}

    def hand_over(key, grads):
        rider = exch.rider(key, grads)
        if rider is None:
            big.update(grads)
        return rider
    row = lambda v: v.reshape(1, -1)
    nmg0, nmg1 = row(sp["norm_mix_g"][0]), row(sp["norm_mix_g"][1])
    nfg0, nfg1 = row(sp["norm_ffn_g"][0]), row(sp["norm_ffn_g"][1])
    kvg, fng = row(sp["kv_norm_g"]), row(sp["final_norm_g"])
    cw0, cw1 = sp["ffn_conv_w"][0], sp["ffn_conv_w"][1]
    cb0, cb1 = row(sp["ffn_conv_b"][0]), row(sp["ffn_conv_b"][1])
    glu_b = row(sp["ssm_glu_b"])
    fb = jnp.zeros((1, LANES), f32).at[0, :NH].set(sp["forget_b"])
    s5p = {k: sp["ssm_" + k][0] for k in ("log_step", "a_re", "a_im", "b_re", "b_im", "c_re", "c_im", "d")}
    s5 = _s5_prepare(s5p, cfg)
    m0, m1 = mod["l0"], mod["l1"]

    h0 = _norm_mod_fwd(x, nmg0, m0["sc_m"], m0["sh_m"], cfg, "mix_norm_0")
    u = _mm(h0, W["ssm_w_in"], mode="nn", name="ssm_in")
    y, gact, cin, extra = _s5_fwd(u, s5["bmat"], s5["cmat"], s5["drow"], s5["arow"], s5["tab"], cfg, exch.rider("s5_fwd"))
    W.update(exch.done("s5_fwd", extra))

    def glu_epi(acc, bv, gv):
        pre = acc + bv
        return pre, gv.astype(f32) * _sigmoid(pre)
    pre, z = _mm(gact, W["ssm_glu_w"], mode="nn", extras=[(glu_b, "n"), (gact, "mn")], epi=glu_epi,
                 out_dtypes=(f32, bf16), name="ssm_glu")
    x1, out_m0 = _mm(z, W["ssm_w_out"], mode="nn", extras=[(x, "mn"), (m0["g_m"], "n")], epi=_resid_epi,
                     out_dtypes=(f32, bf16), name="ssm_out")
    x2, ffn0 = _ffn_fwd(x1, nfg0, m0["sc_f"], m0["sh_f"], m0["g_f"], W, exch, cw0, cb0, cfg, "0")

    hk = _norm_mod_fwd(x2, kvg, mod["sc_kv"], mod["sh_kv"], cfg, "kv_norm")
    kvb = _mm(hk, W["kv_w"], mode="nn", out_dtypes=(bf16,), name="kv_proj")
    zf = _mm(hk, W["kv_wf"], mode="nn", name="kv_fproj")
    fc = _fgate_fwd(zf, fb, cfg)
    fct = fc[:, :NH].T
    fk = fct[:, None, :]

    h1 = _norm_mod_fwd(x2, nmg1, m1["sc_m"], m1["sh_m"], cfg, "mix_norm_1")
    q = _mm(h1, W["attn_w_q"], mode="nn", out_dtypes=(bf16,), name="attn_q")
    o, lse, extra = _ta_fwd(q, kvb, fk, cfg, exch.rider("attn_fwd"))
    W.update(exch.done("attn_fwd", extra))
    x3, out_m1 = _mm(o, W["attn_w_out"], mode="nn", extras=[(x2, "mn"), (m1["g_m"], "n")], epi=_resid_epi,
                     out_dtypes=(f32, bf16), name="attn_out")
    x4, ffn1 = _ffn_fwd(x3, nfg1, m1["sc_f"], m1["sh_f"], m1["g_f"], W, exch, cw1, cb1, cfg, "1")

    dx, dfng, lcol = _final_loss(x4, fng, tgt, cfg)
    loss = (0.5 / D) * jnp.sum(lcol)

    dx, dw_up1, dw_down1, sf1 = _ffn_bwd(dx, x3, ffn1, nfg1, m1["sc_f"], m1["g_f"], W["ffn_w_up1"], W["ffn_w_down1"], cw1, cb1, cfg, "1")
    dout, dgm1 = _gate_bwd(dx, out_m1, m1["g_m"], cfg, "attn_gate_bwd")
    do = _mm(dout, W["attn_w_out"], mode="nt", out_dtypes=(bf16,), name="attn_do")
    dw_ao = _mm(o, dout, mode="tn", out_dtypes=(bf16,), name="attn_dwout")
    dq, dfq, delta, extra = _ta_bwd_dq(q, kvb, do, o, lse, fk, cfg,
                                       hand_over("attn_bwd", dict(ffn_w_up1=dw_up1, ffn_w_down1=dw_down1)))
    exch.done("attn_bwd", extra)
    dw_q = _mm(h1, dq, mode="tn", out_dtypes=(bf16,), name="attn_dwq")
    dk, dv, dfk, extra = _ta_bwd_dkv(q, kvb, do, delta, lse, fk, cfg,
                                     hand_over("attn_bwd_dkv", dict(attn_w_q=dw_q, attn_w_out=dw_ao)))
    exch.done("attn_bwd_dkv", extra)
    dh1 = _mm(dq, W["attn_w_q"], mode="nt", out_dtypes=(bf16,), name="attn_dh")
    dx, A1, B1 = _norm_mod_bwd(dh1, x2, nmg1, m1["sc_m"], dx, cfg, "mix_norm_bwd_1")

    dfc = jnp.pad((dfq[:, :, 0] + dfk[:, :, 0]).T, ((0, 0), (0, LANES - NH)))
    dzf, dfb = _fgate_bwd(dfc, zf, fb, cfg)
    dkv = jnp.concatenate([dk, dv], axis=1)
    dhk1 = _mm(dkv, W["kv_w"], mode="nt", name="kv_dh1")
    dhk = _mm(dzf, W["kv_wf"], mode="nt", extras=[(dhk1, "mn")], epi=lambda acc, e: (acc + e,), out_dtypes=(bf16,), name="kv_dh2")
    dw_kv = _mm(hk, dkv, mode="tn", out_dtypes=(bf16,), name="kv_dw")
    dw_kf = _mm(hk, dzf, mode="tn", out_dtypes=(bf16,), name="kv_dwf")
    dx, Ak, Bk = _norm_mod_bwd(dhk, x2, kvg, mod["sc_kv"], dx, cfg, "kv_norm_bwd")

    dx, dw_up0, dw_down0, sf0 = _ffn_bwd(dx, x1, ffn0, nfg0, m0["sc_f"], m0["g_f"], W["ffn_w_up0"], W["ffn_w_down0"], cw0, cb0, cfg, "0")
    dout, dgm0 = _gate_bwd(dx, out_m0, m0["g_m"], cfg, "ssm_gate_bwd")
    dz = _mm(dout, W["ssm_w_out"], mode="nt", out_dtypes=(bf16,), name="ssm_dz")
    dw_so = _mm(z, dout, mode="tn", out_dtypes=(bf16,), name="ssm_dwout")
    dpre, dgd, dglub = _glu_bwd(dz, gact, pre, cfg)
    dy = _mm(dpre, W["ssm_glu_w"], mode="nt", extras=[(dgd, "mn"), (y, "mn")],
             epi=lambda acc, e, yv: ((acc + e) * _gelu_grad(yv),), name="ssm_dy")
    dw_glu = _mm(gact, dpre, mode="tn", out_dtypes=(bf16,), name="ssm_dwglu")
    rider = hand_over("s5_bwd", dict(kv_w=jnp.concatenate([dw_kv, dw_kf[:, :NH]], axis=1), ffn_w_up0=dw_up0,
                                     ffn_w_down0=dw_down0, ssm_w_out=dw_so, ssm_glu_w=dw_glu))
    du, dbm, dcm, dab, dd, extra = _s5_bwd(u, dy, cin, s5["bmat"], s5["cmat"], s5["drow"], s5["arow"], s5["tab"], cfg, rider)
    exch.done("s5_bwd", extra)
    dh0 = _mm(du, W["ssm_w_in"], mode="nt", out_dtypes=(bf16,), name="ssm_dh")
    dw_in = _mm(h0, du, mode="tn", out_dtypes=(bf16,), name="ssm_dwin")
    dx, A0, B0 = _norm_mod_bwd(dh0, x, nmg0, m0["sc_m"], dx, cfg, "mix_norm_bwd_0")

    s5g = _s5_param_grads(s5p, dbm, dcm, dab, dd, cfg)
    big["ssm_w_in"] = dw_in
    small = dict(
        norm_mix_g=jnp.concatenate([(1.0 + m0["sc_m"]) * A0, (1.0 + m1["sc_m"]) * A1], axis=0),
        norm_ffn_g=jnp.concatenate([sf0["norm_g"], sf1["norm_g"]], axis=0),
        ssm_glu_b=dglub, kv_norm_g=(1.0 + mod["sc_kv"]) * Ak, forget_b=dfb[0, :NH],
        ffn_conv_w=jnp.stack([sf0["conv_w"], sf1["conv_w"]]), ffn_conv_b=jnp.concatenate([sf0["conv_b"], sf1["conv_b"]], axis=0),
        final_norm_g=dfng, **{"ssm_" + k: v[None] for k, v in s5g.items()})
    dmod = [jnp.concatenate([B0, nmg0 * A0, dgm0, sf0["sh"], sf0["sc"], sf0["gate"]], axis=1),
            jnp.concatenate([B1, nmg1 * A1, dgm1, sf1["sh"], sf1["sc"], sf1["gate"]], axis=1),
            jnp.concatenate([Bk, kvg * Ak], axis=1)]
    return loss, dx, big, small, dmod


WEIGHTS = ["mod_w", "mod_b", "norm_mix_g", "norm_ffn_g", "ssm_w_in", "ssm_log_step", "ssm_a_re", "ssm_a_im", "ssm_b_re",
           "ssm_b_im", "ssm_c_re", "ssm_c_im", "ssm_d", "ssm_glu_w", "ssm_glu_b", "ssm_w_out", "kv_mod_w", "kv_mod_b",
           "kv_norm_g", "kv_w", "forget_b", "attn_w_q", "attn_w_out", "ffn_w_up", "ffn_conv_w", "ffn_conv_b", "ffn_w_down",
           "final_norm_g"]
ARGS = ["x", "c"] + WEIGHTS + ["loss_target"] + ["m_" + n for n in WEIGHTS] + ["v_" + n for n in WEIGHTS]
SMALL = ["mod_b", "norm_mix_g", "norm_ffn_g", "ssm_log_step", "ssm_a_re", "ssm_a_im", "ssm_b_re", "ssm_b_im", "ssm_c_re",
         "ssm_c_im", "ssm_d", "ssm_glu_b", "kv_mod_b", "kv_norm_g", "forget_b", "ffn_conv_w", "ffn_conv_b", "final_norm_g"]
PACK_ROWS = 512


def _pack(arrs):
    flat = jnp.concatenate([a.reshape(-1).astype(f32) for a in arrs])
    unit = PACK_ROWS * LANES
    n = -(-flat.shape[0] // unit) * unit
    return jnp.pad(flat, (0, n - flat.shape[0])).reshape(-1, LANES)


def _unpack(packed, shapes):
    flat, out, off = packed.reshape(-1), [], 0
    for s in shapes:
        n = math.prod(s)
        out.append(flat[off:off + n].reshape(s))
        off += n
    return out


def _silu(v):
    return v * _sigmoid(v)


def _half(w, c, axis):
    r = w.shape[axis] // 2
    return lax.dynamic_slice_in_dim(w, c * r, r, axis=axis)


class _Exchange:
    FIRST = ["ssm_w_in", "ssm_glu_w", "ssm_w_out"]
    FWD = dict(s5_fwd=["ffn_w_up0"], ffn_up_0=["ffn_w_down0", "kv_w"], ffn_down_0=["attn_w_q", "attn_w_out"],
               attn_fwd=["ffn_w_up1", "ffn_w_down1"])
    LATE = (3, 4)

    def __init__(self, cfg, blocks, core):
        self.cfg, self.blocks, self.core = cfg, blocks, core
        self.parts, self.scattered, self.names = {}, {}, {}

    def weights(self, names, gathered):
        D, F, NH = self.cfg.D, self.cfg.F, self.cfg.NH
        W = {}
        for n, g in zip(names, gathered):
            if n.startswith("ffn_w_up"):
                W[n] = g.reshape(4, D, 2 * F // 4)
            elif n == "kv_w":
                full = g.reshape(4, D, -1).transpose(1, 0, 2).reshape(D, -1)
                W["kv_w"] = full[:, :2 * D]
                W["kv_wf"] = jnp.pad(full[:, 2 * D:], ((0, 0), (0, LANES - NH)))
            else:
                W[n] = g.reshape(-1, D)
        return W

    def sibling_sum(self, key, grads):
        D = self.cfg.D

        def blocks_of(n, g):
            if n.startswith("ffn_w_up"):
                return g.reshape(4, 2, D // 2, -1)
            if n == "kv_w":
                return g.reshape(D, 4, -1).transpose(1, 0, 2).reshape(4, 2, D // 2, -1)
            return g.reshape(4, 2, g.shape[0] // 8, g.shape[1])
        names = list(grads)
        gb = [blocks_of(n, grads[n]) for n in names]
        recv = _sibling_swap_halves(gb, f"grad_sibling_swap_{key}")
        for n, g, r in zip(names, gb, recv):
            keep = lax.dynamic_index_in_dim(g, self.core, axis=1, keepdims=False)
            rr, cc = keep.shape[1], keep.shape[2]
            self.parts[n] = _add2(keep.reshape(4 * rr, cc), r.reshape(4 * rr, cc), bf16, f"grad_add_{n}").reshape(4, rr, cc)
        return self.parts

    def rider(self, key, grads=None):
        if key in self.FWD:
            return _gather_rider([self.blocks[n] for n in self.FWD[key]], (1, 2) if key == "attn_fwd" else self.LATE)
        if grads is None:
            return None
        self.names[key] = list(grads)
        parts = self.sibling_sum(key, grads)
        return _scatter_rider([parts[n] for n in self.names[key]])

    def done(self, key, extra):
        if key in self.FWD:
            return self.weights(self.FWD[key], extra)
        self.scattered.update(zip(self.names[key], extra))
        return {}


def kernel(x, c, mod_w, mod_b, norm_mix_g, norm_ffn_g, ssm_w_in, ssm_log_step, ssm_a_re, ssm_a_im, ssm_b_re, ssm_b_im, ssm_c_re, ssm_c_im, ssm_d, ssm_glu_w, ssm_glu_b, ssm_w_out, kv_mod_w, kv_mod_b, kv_norm_g, kv_w, forget_b, attn_w_q, attn_w_out, ffn_w_up, ffn_conv_w, ffn_conv_b, ffn_w_down, final_norm_g, loss_target, m_mod_w, m_mod_b, m_norm_mix_g, m_norm_ffn_g, m_ssm_w_in, m_ssm_log_step, m_ssm_a_re, m_ssm_a_im, m_ssm_b_re, m_ssm_b_im, m_ssm_c_re, m_ssm_c_im, m_ssm_d, m_ssm_glu_w, m_ssm_glu_b, m_ssm_w_out, m_kv_mod_w, m_kv_mod_b, m_kv_norm_g, m_kv_w, m_forget_b, m_attn_w_q, m_attn_w_out, m_ffn_w_up, m_ffn_conv_w, m_ffn_conv_b, m_ffn_w_down, m_final_norm_g, v_mod_w, v_mod_b, v_norm_mix_g, v_norm_ffn_g, v_ssm_w_in, v_ssm_log_step, v_ssm_a_re, v_ssm_a_im, v_ssm_b_re, v_ssm_b_im, v_ssm_c_re, v_ssm_c_im, v_ssm_d, v_ssm_glu_w, v_ssm_glu_b, v_ssm_w_out, v_kv_mod_w, v_kv_mod_b, v_kv_norm_g, v_kv_w, v_forget_b, v_attn_w_q, v_attn_w_out, v_ffn_w_up, v_ffn_conv_w, v_ffn_conv_b, v_ffn_w_down, v_final_norm_g):
    a = dict(locals())
    assert list(a) == ARGS
    return _step(CFG, a)


def _step(cfg, a):
    D, F, NH = cfg.D, cfg.F, cfg.NH
    x_, y_, c_ = _place()
    chip, dev = 2 * x_ + y_, 4 * x_ + 2 * y_ + c_

    big_src = dict(ssm_w_in=a["ssm_w_in"][0], ssm_glu_w=a["ssm_glu_w"][0], ssm_w_out=a["ssm_w_out"][0],
                   attn_w_q=a["attn_w_q"][0], attn_w_out=a["attn_w_out"][0],
                   ffn_w_up0=a["ffn_w_up"][0], ffn_w_up1=a["ffn_w_up"][1],
                   ffn_w_down0=a["ffn_w_down"][0], ffn_w_down1=a["ffn_w_down"][1], kv_w=a["kv_w"])
    big_names = list(big_src)
    exch = _Exchange(cfg, {n: _half(big_src[n], c_, 0).astype(bf16) for n in big_names}, c_)
    first = exch.FIRST
    blocks = [exch.blocks[n] for n in first] + [_half(a["ssm_glu_b"], c_, 1), _half(a["ffn_conv_w"], c_, 2), a["c"]]
    got = _allgather8(blocks, "gather_weights")
    W = exch.weights(first, got)
    glu_b_full = got[-3].reshape(D)
    conv_w_full = got[-2].transpose(1, 2, 0, 3).reshape(2, 3, 2 * F)
    c16 = jnp.pad(got[-1].reshape(N_DEV, D), ((0, 16 - N_DEV), (0, 0)))

    mcols = [_mm(c16, a["mod_w"][l], mode="nn", a_pro=_silu, name=f"mod_fwd_{l}") for l in range(2)]
    mcols.append(_mm(c16, a["kv_mod_w"], mode="nn", a_pro=_silu, name="mod_fwd_kv"))
    widths = [m.shape[1] for m in mcols]
    mall = _allgather8([jnp.concatenate(mcols, axis=1)[:N_DEV]], "gather_mod")[0][0::2]
    offs = [0, widths[0], widths[0] + widths[1]]
    rows = []
    for off, wd, bias in zip(offs, widths, [a["mod_b"][0], a["mod_b"][1], a["kv_mod_b"]]):
        fullm = mall[:, :, off:off + wd].transpose(1, 0, 2).reshape(N_DEV, 4 * wd) + bias
        rows.append(lax.dynamic_slice_in_dim(fullm, dev, 1, axis=0))
    mod = {}
    for l in range(2):
        mod[f"l{l}"] = dict(zip(["sh_m", "sc_m", "g_m", "sh_f", "sc_f", "g_f"], jnp.split(rows[l], 6, axis=1)))
    mod["sh_kv"], mod["sc_kv"] = jnp.split(rows[2], 2, axis=1)

    sp = {n: a[n] for n in ["norm_mix_g", "norm_ffn_g", "kv_norm_g", "final_norm_g", "ffn_conv_b", "forget_b", "ssm_log_step",
                            "ssm_a_re", "ssm_a_im", "ssm_b_re", "ssm_b_im", "ssm_c_re", "ssm_c_im", "ssm_d"]}
    sp["ssm_glu_b"], sp["ffn_conv_w"] = glu_b_full, conv_w_full
    loss, dx, big, small, dmod = _local_step(cfg, a["x"][0], a["loss_target"][0], mod, W, sp, exch)
    loss = lax.psum(loss, ("x", "y", "c"))

    small["mod_b"] = jnp.concatenate([dmod[0], dmod[1]], axis=0)
    small["kv_mod_b"] = dmod[2]
    shapes = [(2, 6 * D) if n == "mod_b" else (1, D) if n == "ssm_glu_b" else (2, 3, 2 * F) if n == "ffn_conv_w"
              else a[n].shape for n in SMALL]
    packs = _allgather8([_pack([small[n] for n in SMALL])], "gather_small")[0]
    gsmall = dict(zip(SMALL, _unpack(_sum_lead(packs, f32, "sum_small"), shapes)))
    per_dev = packs.reshape(N_DEV, -1)
    sizes = [math.prod(s) for s in shapes]
    starts = dict(zip(SMALL, [sum(sizes[:i]) for i in range(len(sizes))]))

    def rows_of(name, l, width):
        st = starts[name] + l * 6 * D
        blk = lax.dynamic_slice(per_dev, (0, st + chip * width), (N_DEV, width))
        return jnp.pad(blk, ((0, 16 - N_DEV), (0, 0)))
    g_mod_w = jnp.stack([_mm(c16, rows_of("mod_b", l, 6 * D // 4), mode="tn", a_pro=_silu, name=f"mod_dw_{l}") for l in range(2)])
    g_kv_mod_w = _mm(c16, rows_of("kv_mod_b", 0, 2 * D // 4), mode="tn", a_pro=_silu, name="mod_dw_kv")
    gsmall["ssm_glu_b"] = lax.dynamic_slice_in_dim(gsmall["ssm_glu_b"], chip * (D // 4), D // 4, axis=1)
    gsmall["ffn_conv_w"] = lax.dynamic_slice_in_dim(gsmall["ffn_conv_w"], chip * (2 * F // 4), 2 * F // 4, axis=2)

    last = list(big)
    exch.scattered.update(zip(last, _chip_scatter([exch.sibling_sum("tail", big)[n] for n in last], "grad_chip_scatter")))
    chip1, core1 = jnp.reshape(chip, (1,)).astype(jnp.int32), jnp.reshape(c_, (1,)).astype(jnp.int32)
    mine = {n: _sum_parts(exch.parts[n], exch.scattered[n], chip1, f"grad_sum_{n}") for n in big_names}
    other = dict(zip(big_names, _sibling_send([mine[n] for n in big_names], "grad_sibling_send")))

    grads = dict(gsmall)
    grads["mod_w"], grads["kv_mod_w"] = g_mod_w, g_kv_mod_w
    delta, new_m, new_v = {}, {}, {}
    members = dict(ssm_w_in=["ssm_w_in"], ssm_glu_w=["ssm_glu_w"], ssm_w_out=["ssm_w_out"], attn_w_q=["attn_w_q"],
                   attn_w_out=["attn_w_out"], kv_w=["kv_w"], ffn_w_up=["ffn_w_up0", "ffn_w_up1"],
                   ffn_w_down=["ffn_w_down0", "ffn_w_down1"])
    for n, parts_ in members.items():
        shp = a[n].shape
        three = lambda t: t.reshape(len(parts_), -1, shp[-1])
        g_, d_, m_, v_ = _adamw_halves(three(a[n]), three(a["m_" + n]), three(a["v_" + n]),
                                       jnp.stack([mine[p] for p in parts_]), jnp.stack([other[p] for p in parts_]),
                                       core1, f"adamw_{n}")
        grads[n], delta[n], new_m[n], new_v[n] = g_.reshape(shp), d_.reshape(shp), m_.reshape(shp), v_.reshape(shp)
    for n in ["mod_w", "kv_mod_w"]:
        shp = a[n].shape
        two = lambda t: t.reshape(-1, shp[-1])
        d_, m_, v_ = _adamw(two(a[n]), two(grads[n]), two(a["m_" + n]), two(a["v_" + n]), f"adamw_{n}")
        delta[n], new_m[n], new_v[n] = d_.reshape(shp), m_.reshape(shp), v_.reshape(shp)
    grads = {n: grads[n].reshape(a[n].shape) for n in WEIGHTS}
    sshapes = [a[n].shape for n in SMALL]
    d_, m_, v_ = _adamw(_pack([a[n] for n in SMALL]), _pack([grads[n] for n in SMALL]), _pack([a["m_" + n] for n in SMALL]),
                        _pack([a["v_" + n] for n in SMALL]), "adamw_small")
    for n, dd_, mm_, vv_ in zip(SMALL, _unpack(d_, sshapes), _unpack(m_, sshapes), _unpack(v_, sshapes)):
        delta[n], new_m[n], new_v[n] = dd_, mm_, vv_

    return (loss, dx[None], *[grads[n] for n in WEIGHTS], *[delta[n] for n in WEIGHTS],
            *[new_m[n] for n in WEIGHTS], *[new_v[n] for n in WEIGHTS])
```

```python
import collections
import functools
import math

import jax
import jax.numpy as jnp
from jax import lax
from jax.experimental import pallas as pl
from jax.experimental.pallas import tpu as pltpu

f32 = jnp.float32
bf16 = jnp.bfloat16
MESH = pl.DeviceIdType.MESH

LANES = 128
SUBLANES = 8
VMEM_BYTES_V7X = 64 * 1024 * 1024
VMEM_LIMIT = 56 * 1024 * 1024

Cfg = collections.namedtuple("Cfg", "L D G P H NH DH F TC BQ")
CFG = Cfg(L=4096, D=2048, G=128, P=64, H=16, NH=16, DH=128, F=5632, TC=512, BQ=512)
NORM_EPS = 1e-6
ADAM_LR, ADAM_B1, ADAM_B2, ADAM_EPS, ADAM_WD, ADAM_STEP = 0.001, 0.9, 0.999, 1e-08, 0.01, 10
N_DEV = 8


def _cp(sem=None):
    return pltpu.CompilerParams(dimension_semantics=sem, vmem_limit_bytes=VMEM_LIMIT)


def _tile(dim, pref, unit=LANES):
    if dim <= pref:
        return dim
    t = (pref // unit) * unit
    while t > unit and dim % t:
        t -= unit
    assert dim % t == 0, (dim, pref)
    return t


_DIMS = {"nn": (((1,), (0,)), ((), ())), "nt": (((1,), (1,)), ((), ())), "tn": (((0,), (0,)), ((), ()))}


def _mm(a, b, *, mode, name, tm=1024, tn=1024, tk=2048, b4=False, out4=False, a_pro=None, extras=(), epi=None,
        out_dtypes=(f32,), rider=None):
    if mode == "tn":
        K, M = a.shape
    else:
        M, K = a.shape
    if b4:
        R, c4 = b.shape[1], b.shape[2]
        N = R if mode == "nt" else 4 * c4
        assert (K == 4 * c4) if mode == "nt" else (K == R)
    else:
        N = b.shape[0] if mode == "nt" else b.shape[1]
        assert K == (b.shape[1] if mode == "nt" else b.shape[0])
    n4 = N // 4
    tm = _tile(M, tm, LANES if mode == "tn" else SUBLANES * 2)
    tn = _tile(n4 if out4 or (b4 and mode != "nt") else N, tn)
    tk = _tile(b.shape[2] if (b4 and mode == "nt") else K, tk)
    nm, nn_, nk = M // tm, N // tn, K // tk

    a_spec = pl.BlockSpec((tk, tm), lambda i, j, k: (k, i)) if mode == "tn" else pl.BlockSpec((tm, tk), lambda i, j, k: (i, k))
    if b4 and mode == "nt":
        q = b.shape[2] // tk
        b_spec = pl.BlockSpec((None, tn, tk), lambda i, j, k: (lax.div(k, q), j, lax.rem(k, q)))
    elif b4:
        q = b.shape[2] // tn
        b_spec = pl.BlockSpec((None, tk, tn), lambda i, j, k: (lax.div(j, q), k, lax.rem(j, q)))
    elif mode == "nt":
        b_spec = pl.BlockSpec((tn, tk), lambda i, j, k: (j, k))
    else:
        b_spec = pl.BlockSpec((tk, tn), lambda i, j, k: (k, j))
    ex_specs = []
    for arr, kind in extras:
        if kind == "mn":
            ex_specs.append(pl.BlockSpec((tm, tn), lambda i, j, k: (i, j)))
        else:
            ex_specs.append(pl.BlockSpec((1, tn), lambda i, j, k: (0, j)))
    if out4:
        qo = n4 // tn
        o_spec = pl.BlockSpec((None, tm, tn), lambda i, j, k: (lax.div(j, qo), i, lax.rem(j, qo)))
        o_shapes = [jax.ShapeDtypeStruct((4, M, n4), dt) for dt in out_dtypes]
    else:
        o_spec = pl.BlockSpec((tm, tn), lambda i, j, k: (i, j))
        o_shapes = [jax.ShapeDtypeStruct((M, N), dt) for dt in out_dtypes]
    ne, no = len(extras), len(out_dtypes)
    dims = _DIMS[mode]

    def body(a_ref, b_ref, *rest):
        ex_refs, o_refs, acc_ref = rest[:ne], rest[ne:ne + no], rest[ne + no]
        k = pl.program_id(2)

        @pl.when(k == 0)
        def _():
            acc_ref[...] = jnp.zeros_like(acc_ref)

        av = a_ref[...]
        if a_pro is not None:
            av = a_pro(av)
        acc_ref[...] += lax.dot_general(av.astype(bf16), b_ref[...].astype(bf16), dims, preferred_element_type=f32)

        @pl.when(k == nk - 1)
        def _():
            acc = acc_ref[...]
            outs = (acc,) if epi is None else epi(acc, *[r[...] for r in ex_refs])
            for o_ref, o in zip(o_refs, outs):
                o_ref[...] = o.astype(o_ref.dtype)

    res, extra = _host_call(
        body, name=name, grid=(nm, nn_, nk), in_specs=[a_spec, b_spec] + ex_specs, out_specs=[o_spec] * no,
        out_shape=o_shapes, scratch_shapes=[pltpu.VMEM((tm, tn), f32)], args=(a, b, *[e[0] for e in extras]), rider=rider)
    res = res[0] if no == 1 else res
    return (res, extra) if rider else res


HALO = 16


def _rowwise(fn, ins, outs, accs, *, L, C, tl, tc, name):
    tl = _tile(L, tl, HALO)
    tc = _tile(C, tc)
    ni, nj = L // tl, C // tc
    hb = tl // HALO
    nh = L // HALO
    in_specs = []
    for spec in ins:
        kind = spec[1]
        off = spec[2] if len(spec) > 2 else 0
        if kind == "rc":
            in_specs.append(pl.BlockSpec((tl, tc), lambda j, i, off=off: (i, j + off)))
        elif kind == "c":
            in_specs.append(pl.BlockSpec((1, tc), lambda j, i, off=off: (0, j + off)))
        elif kind == "c3":
            in_specs.append(pl.BlockSpec((3, tc), lambda j, i, off=off: (0, j + off)))
        elif kind == "prev":
            in_specs.append(pl.BlockSpec((HALO, tc), lambda j, i, off=off: (jnp.maximum(i * hb - 1, 0), j + off)))
        elif kind == "next":
            in_specs.append(pl.BlockSpec((HALO, tc), lambda j, i, off=off: (jnp.minimum((i + 1) * hb, nh - 1), j + off)))
        else:
            raise ValueError(kind)
    out_specs = [pl.BlockSpec((tl, tc), lambda j, i: (i, j)) for _ in outs]
    out_specs += [pl.BlockSpec((r, tc), lambda j, i: (0, j)) for r in accs]
    out_shape = [jax.ShapeDtypeStruct((L, C), dt) for dt in outs] + [jax.ShapeDtypeStruct((r, C), f32) for r in accs]
    nin, nout, nacc = len(ins), len(outs), len(accs)

    def body(*refs):
        i = pl.program_id(1)
        tiles = [r[...] for r in refs[:nin]]
        o_vals, a_vals = fn(i, ni, *tiles)
        for r, v in zip(refs[nin:nin + nout], o_vals):
            r[...] = v.astype(r.dtype)
        if nacc:
            @pl.when(i == 0)
            def _():
                for r in refs[nin + nout:]:
                    r[...] = jnp.zeros_like(r)
            for r, v in zip(refs[nin + nout:], a_vals):
                r[...] += v

    res = pl.pallas_call(
        body, name=name, grid=(nj, ni), in_specs=in_specs, out_specs=out_specs, out_shape=out_shape,
        compiler_params=_cp(("parallel", "arbitrary")),
    )(*[s[0] for s in ins])
    return res


def _colsum(v):
    return jnp.sum(v, axis=0, keepdims=True)


def _sigmoid(x):
    return 1.0 / (1.0 + jnp.exp(-x))


_GELU_C = math.sqrt(2.0 / math.pi)


def _gelu(y):
    t = jnp.tanh(_GELU_C * (y + 0.044715 * y * y * y))
    return 0.5 * y * (1.0 + t)


def _gelu_grad(y):
    y2 = y * y
    t = jnp.tanh(_GELU_C * (y + 0.044715 * y * y2))
    return 0.5 * (1.0 + t) + 0.5 * y * (1.0 - t * t) * _GELU_C * (1.0 + 3.0 * 0.044715 * y2)


def _norm_mod_fwd(x, g, sc, sh, cfg, name):
    def fn(i, ni, xv, gv, scv, shv):
        rstd = lax.rsqrt(jnp.mean(xv * xv, axis=-1, keepdims=True) + NORM_EPS)
        return [xv * rstd * gv * (1.0 + scv) + shv], []
    return _rowwise(fn, [(x, "rc"), (g, "c"), (sc, "c"), (sh, "c")], [bf16], [], L=cfg.L, C=cfg.D, tl=256, tc=cfg.D, name=name)[0]


def _norm_mod_bwd(dh, x, g, sc, dres, cfg, name):
    def fn(i, ni, dhv, xv, gv, scv, *rest):
        dhv = dhv.astype(f32)
        rstd = lax.rsqrt(jnp.mean(xv * xv, axis=-1, keepdims=True) + NORM_EPS)
        xh = xv * rstd
        dxh = dhv * (gv * (1.0 + scv))
        dx = rstd * (dxh - xh * jnp.mean(dxh * xh, axis=-1, keepdims=True))
        if rest:
            dx = dx + rest[0]
        return [dx], [_colsum(dhv * xh), _colsum(dhv)]
    ins = [(dh, "rc"), (x, "rc"), (g, "c"), (sc, "c")] + ([(dres, "rc")] if dres is not None else [])
    return _rowwise(fn, ins, [f32], [1, 1], L=cfg.L, C=cfg.D, tl=256, tc=cfg.D, name=name)


def _final_loss(x, g, tgt, cfg):
    D = cfg.D

    def fn(i, ni, xv, gv, tv):
        rstd = lax.rsqrt(jnp.mean(xv * xv, axis=-1, keepdims=True) + NORM_EPS)
        xh = xv * rstd
        err = xh * gv - tv
        dy = err * (1.0 / D)
        dxh = dy * gv
        dx = rstd * (dxh - xh * jnp.mean(dxh * xh, axis=-1, keepdims=True))
        return [dx], [_colsum(dy * xh), _colsum(err * err)]
    return _rowwise(fn, [(x, "rc"), (g, "c"), (tgt, "rc")], [f32], [1, 1], L=cfg.L, C=D, tl=256, tc=D, name="final_loss")


def _gate_bwd(dx, out, gate, cfg, name):
    def fn(i, ni, dxv, ov, gv):
        return [dxv * gv], [_colsum(dxv * ov.astype(f32))]
    return _rowwise(fn, [(dx, "rc"), (out, "rc"), (gate, "c")], [bf16], [1], L=cfg.L, C=cfg.D, tl=512, tc=cfg.D, name=name)


def _glu_bwd(dz, g, pre, cfg):
    def fn(i, ni, dzv, gv, pv):
        dzv = dzv.astype(f32)
        gv = gv.astype(f32)
        s = _sigmoid(pv)
        dpre = dzv * gv * s * (1.0 - s)
        return [dpre, dzv * s], [_colsum(dpre)]
    return _rowwise(fn, [(dz, "rc"), (g, "rc"), (pre, "rc")], [bf16, f32], [1], L=cfg.L, C=cfg.D, tl=512, tc=cfg.D, name="glu_bwd")


def _shift_rows(av, pv, k, i):
    rows = lax.broadcasted_iota(jnp.int32, av.shape, 0)
    cur = pltpu.roll(av, k, 0)
    prev = pltpu.roll(pv, k, 0)
    prev = jnp.where(i > 0, prev, 0.0)
    prev_full = jnp.concatenate([prev, jnp.zeros((av.shape[0] - pv.shape[0], av.shape[1]), av.dtype)], axis=0) \
        if av.shape[0] > pv.shape[0] else prev
    return jnp.where(rows >= k, cur, prev_full)


def _shift_rows_up(av, nv, k, i, ni):
    n, h = av.shape[0], nv.shape[0]
    rows = lax.broadcasted_iota(jnp.int32, av.shape, 0)
    cur = pltpu.roll(av, n - k, 0)
    nxt = pltpu.roll(nv, h - k, 0)
    nxt = jnp.where(i < ni - 1, nxt, 0.0)
    nxt_full = jnp.concatenate([jnp.zeros((n - h, av.shape[1]), av.dtype), nxt], axis=0) if n > h else nxt
    return jnp.where(rows < n - k, cur, nxt_full)


def _conv3(av, pv, w, i):
    return w[0:1] * _shift_rows(av, pv, 2, i) + w[1:2] * _shift_rows(av, pv, 1, i) + w[2:3] * av


def _conv_act_fwd(a, conv_w, conv_b, cfg):
    F = cfg.F
    tc = _tile(F, 1408)
    nb = F // tc

    def fn(i, ni, au, av, pu, pv, wu, wv, bu, bv):
        cu = _conv3(au.astype(f32), pu.astype(f32), wu, i) + bu
        cv = _conv3(av.astype(f32), pv.astype(f32), wv, i) + bv
        return [cu * _sigmoid(cu) * cv], []
    ins = [(a, "rc"), (a, "rc", nb), (a, "prev"), (a, "prev", nb), (conv_w, "c3"), (conv_w, "c3", nb), (conv_b, "c"), (conv_b, "c", nb)]
    return _rowwise(fn, ins, [bf16], [], L=cfg.L, C=F, tl=512, tc=tc, name="conv_act_fwd")[0]


def _conv_act_bwd1(dact, a, conv_w, conv_b, cfg):
    F = cfg.F
    tc = _tile(F, 1408)
    nb = F // tc

    def fn(i, ni, dav, au, av, pu, pv, wu, wv, bu, bv):
        dav = dav.astype(f32)
        au, av, pu, pv = au.astype(f32), av.astype(f32), pu.astype(f32), pv.astype(f32)
        au1, au2 = _shift_rows(au, pu, 1, i), _shift_rows(au, pu, 2, i)
        av1, av2 = _shift_rows(av, pv, 1, i), _shift_rows(av, pv, 2, i)
        cu = wu[0:1] * au2 + wu[1:2] * au1 + wu[2:3] * au + bu
        cv = wv[0:1] * av2 + wv[1:2] * av1 + wv[2:3] * av + bv
        s = _sigmoid(cu)
        dcu = dav * cv * (s * (1.0 + cu * (1.0 - s)))
        dcv = dav * cu * s
        dwu = jnp.concatenate([_colsum(dcu * au2), _colsum(dcu * au1), _colsum(dcu * au)], axis=0)
        dwv = jnp.concatenate([_colsum(dcv * av2), _colsum(dcv * av1), _colsum(dcv * av)], axis=0)
        return [dcu, dcv], [dwu, dwv, _colsum(dcu), _colsum(dcv)]
    ins = [(dact, "rc"), (a, "rc"), (a, "rc", nb), (a, "prev"), (a, "prev", nb), (conv_w, "c3"), (conv_w, "c3", nb),
           (conv_b, "c"), (conv_b, "c", nb)]
    return _rowwise(fn, ins, [bf16, bf16], [3, 3, 1, 1], L=cfg.L, C=F, tl=512, tc=tc, name="conv_act_bwd1")


def _conv_bwd2(dc, w, cfg, name):
    F = cfg.F
    tc = _tile(F, 1408)

    def fn(i, ni, dcv, nxt, wv):
        dcv, nxt = dcv.astype(f32), nxt.astype(f32)
        return [wv[2:3] * dcv + wv[1:2] * _shift_rows_up(dcv, nxt, 1, i, ni) + wv[0:1] * _shift_rows_up(dcv, nxt, 2, i, ni)], []
    return _rowwise(fn, [(dc, "rc"), (dc, "next"), (w, "c3")], [bf16], [], L=cfg.L, C=F, tl=512, tc=tc, name=name)[0]


NSLAB = 8


def _s5_tables(abar_re, abar_im, lam_re, lam_im, step, cfg):
    J = cfg.G // 8
    expo = jnp.array([r + 1 for r in range(8)] + [8 * 2 ** p for p in range(8)], f32)[:, None, None]
    mag = jnp.exp(lam_re * step * expo)
    ang = lam_im * step * expo
    t_re = (mag * jnp.cos(ang)).reshape(16, J, 8 * cfg.P).transpose(1, 0, 2)
    t_im = (mag * jnp.sin(ang)).reshape(16, J, 8 * cfg.P).transpose(1, 0, 2)
    tab = jnp.concatenate([t_re, t_im], axis=-1)
    arow = jnp.concatenate([abar_re.reshape(J, 1, 8 * cfg.P), abar_im.reshape(J, 1, 8 * cfg.P)], axis=-1)
    return arow, tab


def _s5_mats(bbar_re, bbar_im, c_re, c_im, cfg):
    J, P, H = cfg.G // 8, cfg.P, cfg.H
    eye = jnp.eye(8, dtype=f32)

    def bd_in(bb):
        bb = bb.reshape(J, 8, P, H)
        return jnp.einsum("jgph,gk->jghkp", bb, eye).reshape(J, 8 * H, 8 * P)

    def bd_out(cc):
        cc = cc.reshape(J, 8, H, P)
        return jnp.einsum("jghp,gk->jgpkh", cc, eye).reshape(J, 8 * P, 8 * H)

    bmat = jnp.concatenate([bd_in(bbar_re), bd_in(bbar_im)], axis=2).astype(bf16)
    cmat = jnp.concatenate([bd_out(c_re), -bd_out(c_im)], axis=1).astype(bf16)
    return bmat, cmat


def _s5_unmats(dbmat, dcmat, cfg):
    J, P, H = cfg.G // 8, cfg.P, cfg.H
    eye = jnp.eye(8, dtype=f32)
    db = dbmat.reshape(J, 8, H, 2, 8, P)
    db = jnp.einsum("jghckp,gk->cjgph", db, eye).reshape(2, cfg.G, P, H)
    dc = dcmat.reshape(J, 2, 8, P, 8, H)
    dc = jnp.einsum("jcgpkh,gk->cjghp", dc, eye).reshape(2, cfg.G, H, P)
    return db[0], db[1], dc[0], -dc[1]


def _chunk_scan(x_ref, row0, nt, arow_ref, tab_ref, c0, reverse):
    sg = -1.0 if reverse else 1.0
    rows = lax.broadcasted_iota(jnp.int32, (nt, LANES), 0)
    order = list(range(7, -1, -1)) if reverse else list(range(8))

    def ld(k, r):
        return x_ref[k, pl.ds(row0 + r, nt, stride=8), :]

    def tab(row, k):
        return tab_ref[pl.ds(row, 1), pl.ds(k * LANES, LANES)]

    carries = [None] * NSLAB
    for k in range(4):
        ar = arow_ref[:, pl.ds(k * LANES, LANES)]
        ai = sg * arow_ref[:, pl.ds((4 + k) * LANES, LANES)]
        sr, si = ld(k, order[0]), ld(4 + k, order[0])
        for r in order[1:]:
            sr, si = ar * sr - ai * si + ld(k, r), ar * si + ai * sr + ld(4 + k, r)
        if reverse:
            cr = jnp.where(rows == nt - 1, c0[k], pltpu.roll(sr, nt - 1, 0))
            ci = jnp.where(rows == nt - 1, c0[4 + k], pltpu.roll(si, nt - 1, 0))
        else:
            cr = jnp.where(rows == 0, c0[k], pltpu.roll(sr, 1, 0))
            ci = jnp.where(rows == 0, c0[4 + k], pltpu.roll(si, 1, 0))
        d, p = 1, 0
        while d < nt:
            qr, qi = tab(8 + p, k), sg * tab(8 + p, 4 + k)
            if reverse:
                shr, shi, m = pltpu.roll(cr, nt - d, 0), pltpu.roll(ci, nt - d, 0), rows < nt - d
            else:
                shr, shi, m = pltpu.roll(cr, d, 0), pltpu.roll(ci, d, 0), rows >= d
            cr, ci = cr + jnp.where(m, qr * shr - qi * shi, 0.0), ci + jnp.where(m, qr * shi + qi * shr, 0.0)
            d, p = 2 * d, p + 1
        carries[k], carries[4 + k] = cr, ci
        sr, si = cr, ci
        for r in order:
            sr, si = ar * sr - ai * si + ld(k, r), ar * si + ai * sr + ld(4 + k, r)
            x_ref[k, pl.ds(row0 + r, nt, stride=8), :] = sr
            x_ref[4 + k, pl.ds(row0 + r, nt, stride=8), :] = si
    return carries


def _slabs_to_mat(x_ref, row0, n):
    return jnp.concatenate([x_ref[k, pl.ds(row0, n), :] for k in range(NSLAB)], axis=1)


def _mat_to_slabs(x_ref, row0, n, m):
    for k in range(NSLAB):
        x_ref[k, pl.ds(row0, n), :] = m[:, k * LANES:(k + 1) * LANES]


def _s5_fwd(u, bmat, cmat, drow, arow, tab, cfg, rider=None):
    L, D, Tc = cfg.L, cfg.D, cfg.TC
    J, NC, nt = cfg.G // 8, L // Tc, Tc // 8
    W = NSLAB * LANES

    def body(u_ref, b_ref, c_ref, d_ref, a_ref, t_ref, y_ref, g_ref, cin_ref, x_ref, st_ref):
        c = pl.program_id(1)

        @pl.when(c == 0)
        def _():
            st_ref[...] = jnp.zeros_like(st_ref)

        cin_ref[...] = st_ref[...]
        ub = u_ref[...]
        _mat_to_slabs(x_ref, 0, Tc, jnp.dot(ub.astype(bf16), b_ref[...], preferred_element_type=f32))
        c0 = [st_ref[:, pl.ds(k * LANES, LANES)] for k in range(NSLAB)]
        _chunk_scan(x_ref, 0, nt, a_ref, t_ref, c0, False)
        for k in range(NSLAB):
            st_ref[:, pl.ds(k * LANES, LANES)] = x_ref[k, pl.ds(Tc - 1, 1), :]
        s = _slabs_to_mat(x_ref, 0, Tc).astype(bf16)
        y = jnp.dot(s, c_ref[...], preferred_element_type=f32) + d_ref[...] * ub
        y_ref[...] = y
        g_ref[...] = _gelu(y).astype(bf16)

    outs, extra = _host_call(
        body, name="s5_fwd", grid=(J, NC), rider=rider, args=(u, bmat, cmat, drow, arow, tab),
        in_specs=[pl.BlockSpec((Tc, LANES), lambda j, c: (c, j)),
                  pl.BlockSpec((None, LANES, W), lambda j, c: (j, 0, 0)),
                  pl.BlockSpec((None, W, LANES), lambda j, c: (j, 0, 0)),
                  pl.BlockSpec((1, LANES), lambda j, c: (0, j)),
                  pl.BlockSpec((None, 1, W), lambda j, c: (j, 0, 0)),
                  pl.BlockSpec((None, 16, W), lambda j, c: (j, 0, 0))],
        out_specs=[pl.BlockSpec((Tc, LANES), lambda j, c: (c, j)),
                   pl.BlockSpec((Tc, LANES), lambda j, c: (c, j)),
                   pl.BlockSpec((None, None, 1, W), lambda j, c: (j, c, 0, 0))],
        out_shape=[jax.ShapeDtypeStruct((L, D), f32), jax.ShapeDtypeStruct((L, D), bf16),
                   jax.ShapeDtypeStruct((J, NC, 1, W), f32)],
        scratch_shapes=[pltpu.VMEM((NSLAB, Tc, LANES), f32), pltpu.VMEM((1, W), f32)])
    return (*outs, extra)


def _s5_bwd(u, dy, cin, bmat, cmat, drow, arow, tab, cfg, rider=None):
    L, D, Tc = cfg.L, cfg.D, cfg.TC
    J, NC, nt = cfg.G // 8, L // Tc, Tc // 8
    W = NSLAB * LANES
    PAD = 0

    def body(u_ref, dy_ref, cin_ref, b_ref, c_ref, d_ref, a_ref, t_ref,
             du_ref, db_ref, dc_ref, da_ref, dd_ref, s_ref, g_ref, gst_ref):
        c = pl.program_id(1)

        @pl.when(c == 0)
        def _():
            gst_ref[...] = jnp.zeros_like(gst_ref)
            db_ref[...] = jnp.zeros_like(db_ref)
            dc_ref[...] = jnp.zeros_like(dc_ref)
            da_ref[...] = jnp.zeros_like(da_ref)
            dd_ref[...] = jnp.zeros_like(dd_ref)

        ub, dyb = u_ref[...], dy_ref[...]
        ub16, dy16 = ub.astype(bf16), dyb.astype(bf16)
        _mat_to_slabs(s_ref, PAD, Tc, jnp.dot(ub16, b_ref[...], preferred_element_type=f32))
        c0 = [cin_ref[:, pl.ds(k * LANES, LANES)] for k in range(NSLAB)]
        tile_in = _chunk_scan(s_ref, PAD, nt, a_ref, t_ref, c0, False)
        _mat_to_slabs(g_ref, 0, Tc, lax.dot_general(dy16, c_ref[...], _DIMS["nt"], preferred_element_type=f32))
        g0 = [gst_ref[:, pl.ds(k * LANES, LANES)] for k in range(NSLAB)]
        _chunk_scan(g_ref, 0, nt, a_ref, t_ref, g0, True)
        for k in range(NSLAB):
            gst_ref[:, pl.ds(k * LANES, LANES)] = g_ref[k, pl.ds(0, 1), :]
        for k in range(4):
            acc_r = jnp.zeros((nt, LANES), f32)
            acc_i = jnp.zeros((nt, LANES), f32)
            for r in range(8):
                gr = g_ref[k, pl.ds(r, nt, stride=8), :]
                gi = g_ref[4 + k, pl.ds(r, nt, stride=8), :]
                if r == 0:
                    pr, pi = tile_in[k], tile_in[4 + k]
                else:
                    pr = s_ref[k, pl.ds(PAD + r - 1, nt, stride=8), :]
                    pi = s_ref[4 + k, pl.ds(PAD + r - 1, nt, stride=8), :]
                acc_r += gr * pr + gi * pi
                acc_i += gi * pr - gr * pi
            da_ref[:, pl.ds(k * LANES, LANES)] += _colsum(acc_r)
            da_ref[:, pl.ds((4 + k) * LANES, LANES)] += _colsum(acc_i)
        gm = _slabs_to_mat(g_ref, 0, Tc).astype(bf16)
        sm = _slabs_to_mat(s_ref, PAD, Tc).astype(bf16)
        du = lax.dot_general(gm, b_ref[...], _DIMS["nt"], preferred_element_type=f32) + d_ref[...] * dyb
        du_ref[...] = du.astype(bf16)
        db_ref[...] += lax.dot_general(ub16, gm, _DIMS["tn"], preferred_element_type=f32)
        dc_ref[...] += lax.dot_general(sm, dy16, _DIMS["tn"], preferred_element_type=f32)
        dd_ref[...] += _colsum(dyb * ub)

    rc = lambda j, c: (NC - 1 - c, j)
    outs, extra = _host_call(
        body, name="s5_bwd", grid=(J, NC), rider=rider, args=(u, dy, cin, bmat, cmat, drow, arow, tab),
        in_specs=[pl.BlockSpec((Tc, LANES), rc), pl.BlockSpec((Tc, LANES), rc),
                  pl.BlockSpec((None, None, 1, W), lambda j, c: (j, NC - 1 - c, 0, 0)),
                  pl.BlockSpec((None, LANES, W), lambda j, c: (j, 0, 0)),
                  pl.BlockSpec((None, W, LANES), lambda j, c: (j, 0, 0)),
                  pl.BlockSpec((1, LANES), lambda j, c: (0, j)),
                  pl.BlockSpec((None, 1, W), lambda j, c: (j, 0, 0)),
                  pl.BlockSpec((None, 16, W), lambda j, c: (j, 0, 0))],
        out_specs=[pl.BlockSpec((Tc, LANES), rc),
                   pl.BlockSpec((None, LANES, W), lambda j, c: (j, 0, 0)),
                   pl.BlockSpec((None, W, LANES), lambda j, c: (j, 0, 0)),
                   pl.BlockSpec((None, 1, W), lambda j, c: (j, 0, 0)),
                   pl.BlockSpec((1, LANES), lambda j, c: (0, j))],
        out_shape=[jax.ShapeDtypeStruct((L, D), bf16), jax.ShapeDtypeStruct((J, LANES, W), f32),
                   jax.ShapeDtypeStruct((J, W, LANES), f32), jax.ShapeDtypeStruct((J, 1, W), f32),
                   jax.ShapeDtypeStruct((1, D), f32)],
        scratch_shapes=[pltpu.VMEM((NSLAB, Tc + PAD, LANES), f32), pltpu.VMEM((NSLAB, Tc, LANES), f32),
                        pltpu.VMEM((1, W), f32)])
    return (*outs, extra)


NEG = -1e30


def _attn_logits(q_ref, k_ref, fq_ref, fk_ref, qi, ki, bq, scale):
    s = lax.dot_general(q_ref[...], k_ref[...], _DIMS["nt"], preferred_element_type=f32) * scale
    s = s + fq_ref[...] - fk_ref[...]
    rows = qi * bq + lax.broadcasted_iota(jnp.int32, s.shape, 0)
    cols = ki * bq + lax.broadcasted_iota(jnp.int32, s.shape, 1)
    return s, cols <= rows


def _attn_fwd(q, kv, fq, fk, cfg):
    L, D, NH, DH, B = cfg.L, cfg.D, cfg.NH, cfg.DH, cfg.BQ
    nq = L // B
    scale = DH ** -0.5

    def body(q_ref, k_ref, v_ref, fq_ref, fk_ref, o_ref, lse_ref, m_ref, l_ref, acc_ref):
        qi, ki = pl.program_id(1), pl.program_id(2)

        @pl.when(ki == 0)
        def _():
            m_ref[...] = jnp.full_like(m_ref, NEG)
            l_ref[...] = jnp.zeros_like(l_ref)
            acc_ref[...] = jnp.zeros_like(acc_ref)

        @pl.when(ki <= qi)
        def _():
            s, mask = _attn_logits(q_ref, k_ref, fq_ref, fk_ref, qi, ki, B, scale)
            s = jnp.where(mask, s, NEG)
            m_prev = m_ref[...]
            m_new = jnp.maximum(m_prev, jnp.max(s, axis=1, keepdims=True))
            alpha = jnp.exp(m_prev - m_new)
            p = jnp.exp(s - m_new)
            l_ref[...] = alpha * l_ref[...] + jnp.sum(p, axis=1, keepdims=True)
            acc_ref[...] = alpha * acc_ref[...] + jnp.dot(p.astype(bf16), v_ref[...], preferred_element_type=f32)
            m_ref[...] = m_new

        @pl.when(ki == qi)
        def _():
            o_ref[...] = (acc_ref[...] / l_ref[...]).astype(o_ref.dtype)
            lse_ref[...] = m_ref[...] + jnp.log(l_ref[...])

    kmap = lambda h, qi, ki: (jnp.minimum(ki, qi), h)
    vmap_ = lambda h, qi, ki: (jnp.minimum(ki, qi), NH + h)
    return pl.pallas_call(
        body, name="attn_fwd", grid=(NH, nq, nq),
        in_specs=[pl.BlockSpec((B, DH), lambda h, qi, ki: (qi, h)),
                  pl.BlockSpec((B, DH), kmap), pl.BlockSpec((B, DH), vmap_),
                  pl.BlockSpec((None, B, 1), lambda h, qi, ki: (h, qi, 0)),
                  pl.BlockSpec((None, 1, B), lambda h, qi, ki: (h, 0, jnp.minimum(ki, qi)))],
        out_specs=[pl.BlockSpec((B, DH), lambda h, qi, ki: (qi, h)),
                   pl.BlockSpec((None, B, 1), lambda h, qi, ki: (h, qi, 0))],
        out_shape=[jax.ShapeDtypeStruct((L, D), bf16), jax.ShapeDtypeStruct((NH, L, 1), f32)],
        scratch_shapes=[pltpu.VMEM((B, 1), f32), pltpu.VMEM((B, 1), f32), pltpu.VMEM((B, DH), f32)],
        compiler_params=_cp(("parallel", "parallel", "arbitrary")),
    )(q, kv, kv, fq, fk)


def _attn_bwd_dq(q, kv, do, o, lse, fq, fk, cfg):
    L, D, NH, DH, B = cfg.L, cfg.D, cfg.NH, cfg.DH, cfg.BQ
    nq = L // B
    scale = DH ** -0.5

    def body(q_ref, k_ref, v_ref, do_ref, o_ref, lse_ref, fq_ref, fk_ref, dq_ref, dfq_ref, acc_ref, df_ref, dl_ref):
        qi, ki = pl.program_id(1), pl.program_id(2)

        @pl.when(ki == 0)
        def _():
            dl_ref[...] = jnp.sum(do_ref[...].astype(f32) * o_ref[...].astype(f32), axis=1, keepdims=True)
            acc_ref[...] = jnp.zeros_like(acc_ref)
            df_ref[...] = jnp.zeros_like(df_ref)

        @pl.when(ki <= qi)
        def _():
            s, mask = _attn_logits(q_ref, k_ref, fq_ref, fk_ref, qi, ki, B, scale)
            p = jnp.where(mask, jnp.exp(s - lse_ref[...]), 0.0)
            dp = lax.dot_general(do_ref[...], v_ref[...], _DIMS["nt"], preferred_element_type=f32)
            ds = p * (dp - dl_ref[...])
            df_ref[...] += jnp.sum(ds, axis=1, keepdims=True)
            acc_ref[...] += jnp.dot(ds.astype(bf16), k_ref[...], preferred_element_type=f32)

        @pl.when(ki == qi)
        def _():
            dq_ref[...] = (acc_ref[...] * scale).astype(dq_ref.dtype)
            dfq_ref[...] = df_ref[...]

    qmap = lambda h, qi, ki: (qi, h)
    return pl.pallas_call(
        body, name="attn_bwd_dq", grid=(NH, nq, nq),
        in_specs=[pl.BlockSpec((B, DH), qmap),
                  pl.BlockSpec((B, DH), lambda h, qi, ki: (jnp.minimum(ki, qi), h)),
                  pl.BlockSpec((B, DH), lambda h, qi, ki: (jnp.minimum(ki, qi), NH + h)),
                  pl.BlockSpec((B, DH), qmap), pl.BlockSpec((B, DH), qmap),
                  pl.BlockSpec((None, B, 1), lambda h, qi, ki: (h, qi, 0)),
                  pl.BlockSpec((None, B, 1), lambda h, qi, ki: (h, qi, 0)),
                  pl.BlockSpec((None, 1, B), lambda h, qi, ki: (h, 0, jnp.minimum(ki, qi)))],
        out_specs=[pl.BlockSpec((B, DH), qmap), pl.BlockSpec((None, B, 1), lambda h, qi, ki: (h, qi, 0))],
        out_shape=[jax.ShapeDtypeStruct((L, D), bf16), jax.ShapeDtypeStruct((NH, L, 1), f32)],
        scratch_shapes=[pltpu.VMEM((B, DH), f32), pltpu.VMEM((B, 1), f32), pltpu.VMEM((B, 1), f32)],
        compiler_params=_cp(("parallel", "parallel", "arbitrary")),
    )(q, kv, kv, do, o, lse, fq, fk)


def _attn_bwd_dkv(q, kv, do, o, lse, fq, fk, cfg):
    L, D, NH, DH, B = cfg.L, cfg.D, cfg.NH, cfg.DH, cfg.BQ
    nq = L // B
    scale = DH ** -0.5

    def body(q_ref, k_ref, v_ref, do_ref, o_ref, lse_ref, fq_ref, fk_ref, dk_ref, dv_ref, dfk_ref, dka_ref, dva_ref, dfa_ref):
        ki, qi = pl.program_id(1), pl.program_id(2)

        @pl.when(qi == 0)
        def _():
            dka_ref[...] = jnp.zeros_like(dka_ref)
            dva_ref[...] = jnp.zeros_like(dva_ref)
            dfa_ref[...] = jnp.zeros_like(dfa_ref)

        @pl.when(qi >= ki)
        def _():
            do = do_ref[...]
            delta = jnp.sum(do.astype(f32) * o_ref[...].astype(f32), axis=1, keepdims=True)
            s, mask = _attn_logits(q_ref, k_ref, fq_ref, fk_ref, qi, ki, B, scale)
            p = jnp.where(mask, jnp.exp(s - lse_ref[...]), 0.0)
            dva_ref[...] += lax.dot_general(p.astype(bf16), do, _DIMS["tn"], preferred_element_type=f32)
            dp = lax.dot_general(do, v_ref[...], _DIMS["nt"], preferred_element_type=f32)
            ds = p * (dp - delta)
            dka_ref[...] += lax.dot_general(ds.astype(bf16), q_ref[...], _DIMS["tn"], preferred_element_type=f32)
            dfa_ref[...] -= jnp.sum(ds, axis=0, keepdims=True)

        @pl.when(qi == nq - 1)
        def _():
            dk_ref[...] = (dka_ref[...] * scale).astype(dk_ref.dtype)
            dv_ref[...] = dva_ref[...].astype(dv_ref.dtype)
            dfk_ref[...] = dfa_ref[...]

    qmap = lambda h, ki, qi: (jnp.maximum(qi, ki), h)
    fqmap = lambda h, ki, qi: (h, jnp.maximum(qi, ki), 0)
    return pl.pallas_call(
        body, name="attn_bwd_dkv", grid=(NH, nq, nq),
        in_specs=[pl.BlockSpec((B, DH), qmap),
                  pl.BlockSpec((B, DH), lambda h, ki, qi: (ki, h)),
                  pl.BlockSpec((B, DH), lambda h, ki, qi: (ki, NH + h)),
                  pl.BlockSpec((B, DH), qmap), pl.BlockSpec((B, DH), qmap),
                  pl.BlockSpec((None, B, 1), fqmap), pl.BlockSpec((None, B, 1), fqmap),
                  pl.BlockSpec((None, 1, B), lambda h, ki, qi: (h, 0, ki))],
        out_specs=[pl.BlockSpec((B, DH), lambda h, ki, qi: (ki, h)), pl.BlockSpec((B, DH), lambda h, ki, qi: (ki, h)),
                   pl.BlockSpec((None, 1, B), lambda h, ki, qi: (h, 0, ki))],
        out_shape=[jax.ShapeDtypeStruct((L, D), bf16), jax.ShapeDtypeStruct((L, D), bf16),
                   jax.ShapeDtypeStruct((NH, 1, L), f32)],
        scratch_shapes=[pltpu.VMEM((B, DH), f32), pltpu.VMEM((B, DH), f32), pltpu.VMEM((1, B), f32)],
        compiler_params=_cp(("parallel", "parallel", "arbitrary")),
    )(q, kv, kv, do, o, lse, fq, fk)


def _tri_tables(nq, by_key):
    pairs = [(qi, ki) for ki in range(nq) for qi in range(ki, nq)] if by_key else \
            [(qi, ki) for qi in range(nq) for ki in range(qi + 1)]
    return jnp.array([p[0] for p in pairs], jnp.int32), jnp.array([p[1] for p in pairs], jnp.int32)


def _tri_call(body, name, cfg, by_key, in_specs, out_specs, out_shape, scratch_shapes, args, rider=None):
    nq = cfg.L // cfg.BQ
    outs, extra = _host_call(body, name=name, grid=(cfg.NH, nq * (nq + 1) // 2), in_specs=in_specs, out_specs=out_specs,
                             out_shape=out_shape, scratch_shapes=scratch_shapes, args=args,
                             prefetch=_tri_tables(nq, by_key), rider=rider)
    return (*outs, extra)


def _ta_fwd(q, kv, fk, cfg, rider=None):
    L, D, NH, DH, B = cfg.L, cfg.D, cfg.NH, cfg.DH, cfg.BQ
    scale = DH ** -0.5

    def body(qt_ref, kt_ref, q_ref, k_ref, v_ref, fk_ref, o_ref, lse_ref, m_ref, acc_ref, a_ref, s_ref, p_ref):
        pid = pl.program_id(1)
        qi, ki = qt_ref[pid], kt_ref[pid]

        @pl.when(ki == 0)
        def _():
            m_ref[...] = jnp.full_like(m_ref, NEG)
            acc_ref[...] = jnp.zeros_like(acc_ref)

        def compute(masked):
            s_ref[...] = lax.dot_general(q_ref[...], k_ref[...], _DIMS["nt"], preferred_element_type=f32)
            fkv = fk_ref[...]

            def strip(rows, row0, c):
                t = s_ref[rows, :] - fkv
                if masked:
                    t = jnp.where(_fa_mask(row0, t.shape), t, NEG)
                m_prev = m_ref[rows, :]
                m_new = jnp.maximum(m_prev, jnp.max(t, axis=1, keepdims=True))
                m_ref[rows, :] = m_new
                a_ref[rows, :] = jnp.exp(m_prev - m_new)
                p_ref[rows, :] = jnp.exp(t - m_new).astype(bf16)
                return c
            _fa_strips(B, strip, 0)
            v1 = jnp.concatenate([v_ref[...], jnp.ones((B, DH), bf16)], axis=1)
            acc_ref[...] = a_ref[...] * acc_ref[...] + jnp.dot(p_ref[...], v1, preferred_element_type=f32)

        @pl.when(ki < qi)
        def _():
            compute(False)

        @pl.when(ki == qi)
        def _():
            compute(True)
            l = acc_ref[:, DH:]
            o_ref[...] = (acc_ref[:, :DH] / l).astype(o_ref.dtype)
            lse_ref[...] = m_ref[...] + jnp.log(l[:, :1])

    col = pltpu.VMEM((B, 1), f32)
    return _tri_call(
        body, "attn_fwd", cfg, False,
        [pl.BlockSpec((B, DH), lambda h, p, qt, kt: (qt[p], h)),
         pl.BlockSpec((B, DH), lambda h, p, qt, kt: (kt[p], h)),
         pl.BlockSpec((B, DH), lambda h, p, qt, kt: (kt[p], NH + h)),
         pl.BlockSpec((None, 1, B), lambda h, p, qt, kt: (h, 0, kt[p]))],
        [pl.BlockSpec((B, DH), lambda h, p, qt, kt: (qt[p], h)),
         pl.BlockSpec((None, B, 1), lambda h, p, qt, kt: (h, qt[p], 0))],
        [jax.ShapeDtypeStruct((L, D), bf16), jax.ShapeDtypeStruct((NH, L, 1), f32)],
        [col, pltpu.VMEM((B, 2 * DH), f32), col, pltpu.VMEM((B, B), f32), pltpu.VMEM((B, B), bf16)],
        (q, kv, kv, fk), rider)


def _ta_bwd_dq(q, kv, do, o, lse, fk, cfg, rider=None):
    L, D, NH, DH, B = cfg.L, cfg.D, cfg.NH, cfg.DH, cfg.BQ
    scale = DH ** -0.5

    def body(qt_ref, kt_ref, q_ref, k_ref, v_ref, do_ref, o_ref, lse_ref, fk_ref, dq_ref, dfq_ref, dl_ref,
             acc_ref, s_ref, dp_ref, ds_ref):
        pid = pl.program_id(1)
        qi, ki = qt_ref[pid], kt_ref[pid]

        @pl.when(ki == 0)
        def _():
            dl_ref[...] = jnp.sum(do_ref[...].astype(f32) * o_ref[...].astype(f32), axis=1, keepdims=True)
            acc_ref[...] = jnp.zeros_like(acc_ref)

        def compute(masked):
            s_ref[...] = lax.dot_general(q_ref[...], k_ref[...], _DIMS["nt"], preferred_element_type=f32)
            dp_ref[...] = lax.dot_general(do_ref[...], v_ref[...], _DIMS["nt"], preferred_element_type=f32)
            fkv = fk_ref[...]

            def strip(rows, row0, c):
                p = jnp.exp(s_ref[rows, :] - fkv - lse_ref[rows, :])
                if masked:
                    p = jnp.where(_fa_mask(row0, p.shape), p, 0.0)
                ds_ref[rows, :] = (p * (dp_ref[rows, :] - dl_ref[rows, :])).astype(bf16)
                return c
            _fa_strips(B, strip, 0)
            k1 = jnp.concatenate([k_ref[...], jnp.ones((B, DH), bf16)], axis=1)
            acc_ref[...] += jnp.dot(ds_ref[...], k1, preferred_element_type=f32)

        @pl.when(ki < qi)
        def _():
            compute(False)

        @pl.when(ki == qi)
        def _():
            compute(True)
            dq_ref[...] = (acc_ref[:, :DH] * scale).astype(dq_ref.dtype)
            dfq_ref[...] = acc_ref[:, DH:DH + 1]

    qmap = lambda h, p, qt, kt: (qt[p], h)
    cmap = lambda h, p, qt, kt: (h, qt[p], 0)
    return _tri_call(
        body, "attn_bwd_dq", cfg, False,
        [pl.BlockSpec((B, DH), qmap),
         pl.BlockSpec((B, DH), lambda h, p, qt, kt: (kt[p], h)),
         pl.BlockSpec((B, DH), lambda h, p, qt, kt: (kt[p], NH + h)),
         pl.BlockSpec((B, DH), qmap), pl.BlockSpec((B, DH), qmap),
         pl.BlockSpec((None, B, 1), cmap),
         pl.BlockSpec((None, 1, B), lambda h, p, qt, kt: (h, 0, kt[p]))],
        [pl.BlockSpec((B, DH), qmap), pl.BlockSpec((None, B, 1), cmap), pl.BlockSpec((None, B, 1), cmap)],
        [jax.ShapeDtypeStruct((L, D), bf16), jax.ShapeDtypeStruct((NH, L, 1), f32), jax.ShapeDtypeStruct((NH, L, 1), f32)],
        [pltpu.VMEM((B, 2 * DH), f32), pltpu.VMEM((B, B), f32), pltpu.VMEM((B, B), f32), pltpu.VMEM((B, B), bf16)],
        (q, kv, kv, do, o, lse, fk), rider)


def _ta_bwd_dkv(q, kv, do, delta, lse, fk, cfg, rider=None):
    L, D, NH, DH, B = cfg.L, cfg.D, cfg.NH, cfg.DH, cfg.BQ
    nq = L // B
    scale = DH ** -0.5

    def body(qt_ref, kt_ref, q_ref, k_ref, v_ref, do_ref, dl_ref, lse_ref, fk_ref, dk_ref, dv_ref, dfk_ref,
             dka_ref, dva_ref, s_ref, dp_ref, p_ref, ds_ref):
        pid = pl.program_id(1)
        qi, ki = qt_ref[pid], kt_ref[pid]

        @pl.when(qi == ki)
        def _():
            dka_ref[...] = jnp.zeros_like(dka_ref)
            dva_ref[...] = jnp.zeros_like(dva_ref)

        def compute(masked):
            s_ref[...] = lax.dot_general(q_ref[...], k_ref[...], _DIMS["nt"], preferred_element_type=f32)
            dp_ref[...] = lax.dot_general(do_ref[...], v_ref[...], _DIMS["nt"], preferred_element_type=f32)
            fkv = fk_ref[...]

            def strip(rows, row0, c):
                p = jnp.exp(s_ref[rows, :] - fkv - lse_ref[rows, :])
                if masked:
                    p = jnp.where(_fa_mask(row0, p.shape), p, 0.0)
                p_ref[rows, :] = p.astype(bf16)
                ds_ref[rows, :] = (p * (dp_ref[rows, :] - dl_ref[rows, :])).astype(bf16)
                return c
            _fa_strips(B, strip, 0)
            q1 = jnp.concatenate([q_ref[...], jnp.ones((B, DH), bf16)], axis=1)
            dva_ref[...] += lax.dot_general(p_ref[...], do_ref[...], _DIMS["tn"], preferred_element_type=f32)
            dka_ref[...] += lax.dot_general(ds_ref[...], q1, _DIMS["tn"], preferred_element_type=f32)

        @pl.when(qi == ki)
        def _():
            compute(True)

        @pl.when(qi > ki)
        def _():
            compute(False)

        @pl.when(qi == nq - 1)
        def _():
            dk_ref[...] = dka_ref[:, :DH].astype(dk_ref.dtype)
            dv_ref[...] = dva_ref[...].astype(dv_ref.dtype)
            dfk_ref[...] = -dka_ref[:, DH:DH + 1]

    qmap = lambda h, p, qt, kt: (qt[p], h)
    cmap = lambda h, p, qt, kt: (h, qt[p], 0)
    kmap = lambda h, p, qt, kt: (kt[p], h)
    return _tri_call(
        body, "attn_bwd_dkv", cfg, True,
        [pl.BlockSpec((B, DH), qmap), pl.BlockSpec((B, DH), kmap),
         pl.BlockSpec((B, DH), lambda h, p, qt, kt: (kt[p], NH + h)),
         pl.BlockSpec((B, DH), qmap), pl.BlockSpec((None, B, 1), cmap), pl.BlockSpec((None, B, 1), cmap),
         pl.BlockSpec((None, 1, B), lambda h, p, qt, kt: (h, 0, kt[p]))],
        [pl.BlockSpec((B, DH), kmap), pl.BlockSpec((B, DH), kmap),
         pl.BlockSpec((None, B, 1), lambda h, p, qt, kt: (h, kt[p], 0))],
        [jax.ShapeDtypeStruct((L, D), bf16), jax.ShapeDtypeStruct((L, D), bf16), jax.ShapeDtypeStruct((NH, L, 1), f32)],
        [pltpu.VMEM((B, 2 * DH), f32), pltpu.VMEM((B, DH), f32), pltpu.VMEM((B, B), f32), pltpu.VMEM((B, B), f32),
         pltpu.VMEM((B, B), bf16), pltpu.VMEM((B, B), bf16)],
        (q, kv, kv, do, delta, lse, fk), rider)


STRIP = 32


def _fa_strips(nrows, fn, init):
    return lax.fori_loop(0, nrows // STRIP, lambda r, c: fn(pl.ds(pl.multiple_of(r * STRIP, STRIP), STRIP), r * STRIP, c),
                         init, unroll=True)


def _fa_mask(row0, shape):
    rows = row0 + lax.broadcasted_iota(jnp.int32, shape, 0)
    cols = lax.broadcasted_iota(jnp.int32, shape, 1)
    return cols <= rows


def _fa_fwd(q, kv, fk, cfg):
    L, D, NH, DH, B = cfg.L, cfg.D, cfg.NH, cfg.DH, cfg.BQ
    nq = L // B
    scale = DH ** -0.5

    def body(q_ref, k_ref, v_ref, fk_ref, o_ref, lse_ref, m_ref, l_ref, acc_ref, a_ref, s_ref, p_ref):
        qi, ki = pl.program_id(1), pl.program_id(2)

        @pl.when(ki == 0)
        def _():
            m_ref[...] = jnp.full_like(m_ref, NEG)
            l_ref[...] = jnp.zeros_like(l_ref)
            acc_ref[...] = jnp.zeros_like(acc_ref)

        def compute(masked):
            s_ref[...] = lax.dot_general(q_ref[...], k_ref[...], _DIMS["nt"], preferred_element_type=f32)
            fkv = fk_ref[...]

            def strip(rows, row0, c):
                t = s_ref[rows, :] - fkv
                if masked:
                    t = jnp.where(_fa_mask(row0, t.shape), t, NEG)
                m_prev = m_ref[rows, :]
                m_new = jnp.maximum(m_prev, jnp.max(t, axis=1, keepdims=True))
                p = jnp.exp(t - m_new)
                alpha = jnp.exp(m_prev - m_new)
                l_ref[rows, :] = alpha * l_ref[rows, :] + jnp.sum(p, axis=1, keepdims=True)
                m_ref[rows, :] = m_new
                a_ref[rows, :] = alpha
                p_ref[rows, :] = p.astype(bf16)
                return c
            _fa_strips(B, strip, 0)
            acc_ref[...] = a_ref[...] * acc_ref[...] + jnp.dot(p_ref[...], v_ref[...], preferred_element_type=f32)

        @pl.when(ki < qi)
        def _():
            compute(False)

        @pl.when(ki == qi)
        def _():
            compute(True)
            o_ref[...] = (acc_ref[...] / l_ref[...]).astype(o_ref.dtype)
            lse_ref[...] = m_ref[...] + jnp.log(l_ref[...])

    col = pltpu.VMEM((B, 1), f32)
    return pl.pallas_call(
        body, name="attn_fwd", grid=(NH, nq, nq),
        in_specs=[pl.BlockSpec((B, DH), lambda h, qi, ki: (qi, h)),
                  pl.BlockSpec((B, DH), lambda h, qi, ki: (jnp.minimum(ki, qi), h)),
                  pl.BlockSpec((B, DH), lambda h, qi, ki: (jnp.minimum(ki, qi), NH + h)),
                  pl.BlockSpec((None, 1, B), lambda h, qi, ki: (h, 0, jnp.minimum(ki, qi)))],
        out_specs=[pl.BlockSpec((B, DH), lambda h, qi, ki: (qi, h)),
                   pl.BlockSpec((None, B, 1), lambda h, qi, ki: (h, qi, 0))],
        out_shape=[jax.ShapeDtypeStruct((L, D), bf16), jax.ShapeDtypeStruct((NH, L, 1), f32)],
        scratch_shapes=[col, col, pltpu.VMEM((B, DH), f32), col, pltpu.VMEM((B, B), f32), pltpu.VMEM((B, B), bf16)],
        compiler_params=_cp(("parallel", "parallel", "arbitrary")),
    )(q, kv, kv, fk)


def _fa_bwd_dq(q, kv, do, o, lse, fk, cfg):
    L, D, NH, DH, B = cfg.L, cfg.D, cfg.NH, cfg.DH, cfg.BQ
    nq = L // B
    scale = DH ** -0.5

    def body(q_ref, k_ref, v_ref, do_ref, o_ref, lse_ref, fk_ref, dq_ref, dfq_ref, dl_ref, acc_ref, df_ref, s_ref, dp_ref, ds_ref):
        qi, ki = pl.program_id(1), pl.program_id(2)

        @pl.when(ki == 0)
        def _():
            dl_ref[...] = jnp.sum(do_ref[...].astype(f32) * o_ref[...].astype(f32), axis=1, keepdims=True)
            acc_ref[...] = jnp.zeros_like(acc_ref)
            df_ref[...] = jnp.zeros_like(df_ref)

        def compute(masked):
            s_ref[...] = lax.dot_general(q_ref[...], k_ref[...], _DIMS["nt"], preferred_element_type=f32)
            dp_ref[...] = lax.dot_general(do_ref[...], v_ref[...], _DIMS["nt"], preferred_element_type=f32)
            fkv = fk_ref[...]

            def strip(rows, row0, c):
                p = jnp.exp(s_ref[rows, :] - fkv - lse_ref[rows, :])
                if masked:
                    p = jnp.where(_fa_mask(row0, p.shape), p, 0.0)
                ds = p * (dp_ref[rows, :] - dl_ref[rows, :])
                df_ref[rows, :] += jnp.sum(ds, axis=1, keepdims=True)
                ds_ref[rows, :] = ds.astype(bf16)
                return c
            _fa_strips(B, strip, 0)
            acc_ref[...] += jnp.dot(ds_ref[...], k_ref[...], preferred_element_type=f32)

        @pl.when(ki < qi)
        def _():
            compute(False)

        @pl.when(ki == qi)
        def _():
            compute(True)
            dq_ref[...] = (acc_ref[...] * scale).astype(dq_ref.dtype)
            dfq_ref[...] = df_ref[...]

    qmap = lambda h, qi, ki: (qi, h)
    cmap = lambda h, qi, ki: (h, qi, 0)
    return pl.pallas_call(
        body, name="attn_bwd_dq", grid=(NH, nq, nq),
        in_specs=[pl.BlockSpec((B, DH), qmap),
                  pl.BlockSpec((B, DH), lambda h, qi, ki: (jnp.minimum(ki, qi), h)),
                  pl.BlockSpec((B, DH), lambda h, qi, ki: (jnp.minimum(ki, qi), NH + h)),
                  pl.BlockSpec((B, DH), qmap), pl.BlockSpec((B, DH), qmap),
                  pl.BlockSpec((None, B, 1), cmap),
                  pl.BlockSpec((None, 1, B), lambda h, qi, ki: (h, 0, jnp.minimum(ki, qi)))],
        out_specs=[pl.BlockSpec((B, DH), qmap), pl.BlockSpec((None, B, 1), cmap), pl.BlockSpec((None, B, 1), cmap)],
        out_shape=[jax.ShapeDtypeStruct((L, D), bf16), jax.ShapeDtypeStruct((NH, L, 1), f32),
                   jax.ShapeDtypeStruct((NH, L, 1), f32)],
        scratch_shapes=[pltpu.VMEM((B, DH), f32), pltpu.VMEM((B, 1), f32), pltpu.VMEM((B, B), f32),
                        pltpu.VMEM((B, B), f32), pltpu.VMEM((B, B), bf16)],
        compiler_params=_cp(("parallel", "parallel", "arbitrary")),
    )(q, kv, kv, do, o, lse, fk)


def _fa_bwd_dkv(q, kv, do, delta, lse, fk, cfg):
    L, D, NH, DH, B = cfg.L, cfg.D, cfg.NH, cfg.DH, cfg.BQ
    nq = L // B
    scale = DH ** -0.5

    def body(q_ref, k_ref, v_ref, do_ref, dl_ref, lse_ref, fk_ref, dk_ref, dv_ref, dfk_ref,
             dka_ref, dva_ref, dfa_ref, s_ref, dp_ref, p_ref, ds_ref):
        ki, qi = pl.program_id(1), pl.program_id(2)

        @pl.when(qi == 0)
        def _():
            dka_ref[...] = jnp.zeros_like(dka_ref)
            dva_ref[...] = jnp.zeros_like(dva_ref)
            dfa_ref[...] = jnp.zeros_like(dfa_ref)

        def compute(masked):
            s_ref[...] = lax.dot_general(q_ref[...], k_ref[...], _DIMS["nt"], preferred_element_type=f32)
            dp_ref[...] = lax.dot_general(do_ref[...], v_ref[...], _DIMS["nt"], preferred_element_type=f32)
            fkv = fk_ref[...]

            def strip(rows, row0, cs):
                p = jnp.exp(s_ref[rows, :] - fkv - lse_ref[rows, :])
                if masked:
                    p = jnp.where(_fa_mask(row0, p.shape), p, 0.0)
                ds = p * (dp_ref[rows, :] - dl_ref[rows, :])
                p_ref[rows, :] = p.astype(bf16)
                ds_ref[rows, :] = ds.astype(bf16)
                return cs + ds
            cs = _fa_strips(B, strip, jnp.zeros((STRIP, B), f32))
            dva_ref[...] += lax.dot_general(p_ref[...], do_ref[...], _DIMS["tn"], preferred_element_type=f32)
            dka_ref[...] += lax.dot_general(ds_ref[...], q_ref[...], _DIMS["tn"], preferred_element_type=f32)
            dfa_ref[...] -= jnp.sum(cs, axis=0, keepdims=True)

        @pl.when(qi == ki)
        def _():
            compute(True)

        @pl.when(qi > ki)
        def _():
            compute(False)

        @pl.when(qi == nq - 1)
        def _():
            dk_ref[...] = (dka_ref[...] * scale).astype(dk_ref.dtype)
            dv_ref[...] = dva_ref[...].astype(dv_ref.dtype)
            dfk_ref[...] = dfa_ref[...]

    qmap = lambda h, ki, qi: (jnp.maximum(qi, ki), h)
    cmap = lambda h, ki, qi: (h, jnp.maximum(qi, ki), 0)
    return pl.pallas_call(
        body, name="attn_bwd_dkv", grid=(NH, nq, nq),
        in_specs=[pl.BlockSpec((B, DH), qmap),
                  pl.BlockSpec((B, DH), lambda h, ki, qi: (ki, h)),
                  pl.BlockSpec((B, DH), lambda h, ki, qi: (ki, NH + h)),
                  pl.BlockSpec((B, DH), qmap),
                  pl.BlockSpec((None, B, 1), cmap), pl.BlockSpec((None, B, 1), cmap),
                  pl.BlockSpec((None, 1, B), lambda h, ki, qi: (h, 0, ki))],
        out_specs=[pl.BlockSpec((B, DH), lambda h, ki, qi: (ki, h)), pl.BlockSpec((B, DH), lambda h, ki, qi: (ki, h)),
                   pl.BlockSpec((None, 1, B), lambda h, ki, qi: (h, 0, ki))],
        out_shape=[jax.ShapeDtypeStruct((L, D), bf16), jax.ShapeDtypeStruct((L, D), bf16),
                   jax.ShapeDtypeStruct((NH, 1, L), f32)],
        scratch_shapes=[pltpu.VMEM((B, DH), f32), pltpu.VMEM((B, DH), f32), pltpu.VMEM((1, B), f32),
                        pltpu.VMEM((B, B), f32), pltpu.VMEM((B, B), f32), pltpu.VMEM((B, B), bf16), pltpu.VMEM((B, B), bf16)],
        compiler_params=_cp(("parallel", "parallel", "arbitrary")),
    )(q, kv, kv, do, delta, lse, fk)


def _fox_logits(q, k, fqv, fkv, scale, masked):
    s = lax.dot_general(q, k, _DIMS["nt"], preferred_element_type=f32) * scale + fqv - fkv
    if masked:
        rows = lax.broadcasted_iota(jnp.int32, s.shape, 0)
        cols = lax.broadcasted_iota(jnp.int32, s.shape, 1)
        return s, cols <= rows
    return s, None


def _fox_fwd(q, kv, fq, fk, cfg):
    L, D, NH, DH, B = cfg.L, cfg.D, cfg.NH, cfg.DH, cfg.BQ
    nq = L // B
    scale = DH ** -0.5

    def body(q_ref, k_ref, v_ref, fq_ref, fk_ref, o_ref, lse_ref):
        qi = pl.program_id(1)
        qv, fqv = q_ref[...], fq_ref[...]

        def chunk(kj, carry, masked):
            m, l, acc = carry
            rows = pl.ds(pl.multiple_of(kj * B, B), B)
            s, mask = _fox_logits(qv, k_ref[rows, :], fqv, fk_ref[kj], scale, masked)
            if masked:
                s = jnp.where(mask, s, NEG)
            m_new = jnp.maximum(m, jnp.max(s, axis=1, keepdims=True))
            alpha = jnp.exp(m - m_new)
            p = jnp.exp(s - m_new)
            l = alpha * l + jnp.sum(p, axis=1, keepdims=True)
            acc = alpha * acc + jnp.dot(p.astype(bf16), v_ref[rows, :], preferred_element_type=f32)
            return m_new, l, acc

        init = (jnp.full((B, 1), NEG, f32), jnp.zeros((B, 1), f32), jnp.zeros((B, DH), f32))
        carry = lax.fori_loop(0, qi, lambda kj, c: chunk(kj, c, False), init)
        m, l, acc = chunk(qi, carry, True)
        o_ref[...] = (acc / l).astype(o_ref.dtype)
        lse_ref[...] = m + jnp.log(l)

    return pl.pallas_call(
        body, name="attn_fwd", grid=(NH, nq),
        in_specs=[pl.BlockSpec((B, DH), lambda h, qi: (qi, h)),
                  pl.BlockSpec((L, DH), lambda h, qi: (0, h)), pl.BlockSpec((L, DH), lambda h, qi: (0, NH + h)),
                  pl.BlockSpec((None, B, 1), lambda h, qi: (h, qi, 0)),
                  pl.BlockSpec((None, nq, 1, B), lambda h, qi: (h, 0, 0, 0))],
        out_specs=[pl.BlockSpec((B, DH), lambda h, qi: (qi, h)), pl.BlockSpec((None, B, 1), lambda h, qi: (h, qi, 0))],
        out_shape=[jax.ShapeDtypeStruct((L, D), bf16), jax.ShapeDtypeStruct((NH, L, 1), f32)],
        compiler_params=_cp(("parallel", "arbitrary")),
    )(q, kv, kv, fq, fk)


def _fox_bwd_dq(q, kv, do, o, lse, fq, fk, cfg):
    L, D, NH, DH, B = cfg.L, cfg.D, cfg.NH, cfg.DH, cfg.BQ
    nq = L // B
    scale = DH ** -0.5

    def body(q_ref, k_ref, v_ref, do_ref, o_ref, lse_ref, fq_ref, fk_ref, dq_ref, dfq_ref, dl_ref):
        qi = pl.program_id(1)
        qv, fqv, dov, lsev = q_ref[...], fq_ref[...], do_ref[...], lse_ref[...]
        delta = jnp.sum(dov.astype(f32) * o_ref[...].astype(f32), axis=1, keepdims=True)

        def chunk(kj, carry, masked):
            acc, df = carry
            rows = pl.ds(pl.multiple_of(kj * B, B), B)
            kv_ = k_ref[rows, :]
            s, mask = _fox_logits(qv, kv_, fqv, fk_ref[kj], scale, masked)
            p = jnp.exp(s - lsev)
            if masked:
                p = jnp.where(mask, p, 0.0)
            dp = lax.dot_general(dov, v_ref[rows, :], _DIMS["nt"], preferred_element_type=f32)
            ds = p * (dp - delta)
            return acc + jnp.dot(ds.astype(bf16), kv_, preferred_element_type=f32), df + jnp.sum(ds, axis=1, keepdims=True)

        carry = lax.fori_loop(0, qi, lambda kj, c: chunk(kj, c, False), (jnp.zeros((B, DH), f32), jnp.zeros((B, 1), f32)))
        acc, df = chunk(qi, carry, True)
        dq_ref[...] = (acc * scale).astype(dq_ref.dtype)
        dfq_ref[...] = df
        dl_ref[...] = delta

    qmap = lambda h, qi: (qi, h)
    cmap = lambda h, qi: (h, qi, 0)
    return pl.pallas_call(
        body, name="attn_bwd_dq", grid=(NH, nq),
        in_specs=[pl.BlockSpec((B, DH), qmap),
                  pl.BlockSpec((L, DH), lambda h, qi: (0, h)), pl.BlockSpec((L, DH), lambda h, qi: (0, NH + h)),
                  pl.BlockSpec((B, DH), qmap), pl.BlockSpec((B, DH), qmap),
                  pl.BlockSpec((None, B, 1), cmap), pl.BlockSpec((None, B, 1), cmap),
                  pl.BlockSpec((None, nq, 1, B), lambda h, qi: (h, 0, 0, 0))],
        out_specs=[pl.BlockSpec((B, DH), qmap), pl.BlockSpec((None, B, 1), cmap), pl.BlockSpec((None, B, 1), cmap)],
        out_shape=[jax.ShapeDtypeStruct((L, D), bf16), jax.ShapeDtypeStruct((NH, L, 1), f32),
                   jax.ShapeDtypeStruct((NH, L, 1), f32)],
        compiler_params=_cp(("parallel", "arbitrary")),
    )(q, kv, kv, do, o, lse, fq, fk)


def _fox_bwd_dkv(q, kv, do, delta, lse, fq, fk, cfg):
    L, D, NH, DH, B = cfg.L, cfg.D, cfg.NH, cfg.DH, cfg.BQ
    nq = L // B
    scale = DH ** -0.5

    def body(q_ref, k_ref, v_ref, do_ref, dl_ref, lse_ref, fq_ref, fk_ref, dk_ref, dv_ref, dfk_ref):
        ki = pl.program_id(1)
        kv_, vv, fkv = k_ref[...], v_ref[...], fk_ref[...]

        def block(qj, carry, masked):
            dk, dv, df = carry
            rows = pl.ds(pl.multiple_of(qj * B, B), B)
            qv, dov = q_ref[rows, :], do_ref[rows, :]
            s, mask = _fox_logits(qv, kv_, fq_ref[rows, :], fkv, scale, masked)
            p = jnp.exp(s - lse_ref[rows, :])
            if masked:
                p = jnp.where(mask, p, 0.0)
            dv = dv + lax.dot_general(p.astype(bf16), dov, _DIMS["tn"], preferred_element_type=f32)
            dp = lax.dot_general(dov, vv, _DIMS["nt"], preferred_element_type=f32)
            ds = p * (dp - dl_ref[rows, :])
            dk = dk + lax.dot_general(ds.astype(bf16), qv, _DIMS["tn"], preferred_element_type=f32)
            return dk, dv, df - jnp.sum(ds, axis=0, keepdims=True)

        init = (jnp.zeros((B, DH), f32), jnp.zeros((B, DH), f32), jnp.zeros((1, B), f32))
        carry = block(ki, init, True)
        dk, dv, df = lax.fori_loop(ki + 1, nq, lambda qj, c: block(qj, c, False), carry)
        dk_ref[...] = (dk * scale).astype(dk_ref.dtype)
        dv_ref[...] = dv.astype(dv_ref.dtype)
        dfk_ref[...] = df

    whole = lambda h, ki: (0, h)
    col = lambda h, ki: (h, 0, 0)
    return pl.pallas_call(
        body, name="attn_bwd_dkv", grid=(NH, nq),
        in_specs=[pl.BlockSpec((L, DH), whole),
                  pl.BlockSpec((B, DH), lambda h, ki: (ki, h)), pl.BlockSpec((B, DH), lambda h, ki: (ki, NH + h)),
                  pl.BlockSpec((L, DH), whole),
                  pl.BlockSpec((None, L, 1), col), pl.BlockSpec((None, L, 1), col), pl.BlockSpec((None, L, 1), col),
                  pl.BlockSpec((None, None, 1, B), lambda h, ki: (h, ki, 0, 0))],
        out_specs=[pl.BlockSpec((B, DH), lambda h, ki: (ki, h)), pl.BlockSpec((B, DH), lambda h, ki: (ki, h)),
                   pl.BlockSpec((None, None, 1, B), lambda h, ki: (h, ki, 0, 0))],
        out_shape=[jax.ShapeDtypeStruct((L, D), bf16), jax.ShapeDtypeStruct((L, D), bf16),
                   jax.ShapeDtypeStruct((NH, nq, 1, B), f32)],
        compiler_params=_cp(("parallel", "arbitrary")),
    )(q, kv, kv, do, delta, lse, fq, fk)


FCH = 256


def _split3(x):
    hi = x.astype(bf16)
    r1 = x - hi.astype(f32)
    mid = r1.astype(bf16)
    lo = (r1 - mid.astype(f32)).astype(bf16)
    return hi, mid, lo


def _tri_sum(tri, x):
    hi, mid, lo = _split3(x)
    return (jnp.dot(tri, hi, preferred_element_type=f32) + jnp.dot(tri, mid, preferred_element_type=f32)
            + jnp.dot(tri, lo, preferred_element_type=f32))


def _fgate_fwd(z, fb, cfg):
    L = cfg.L

    def body(z_ref, fb_ref, f_ref):
        r = lax.broadcasted_iota(jnp.int32, (FCH, FCH), 0)
        c = lax.broadcasted_iota(jnp.int32, (FCH, FCH), 1)
        tri = (c <= r).astype(bf16)
        carry = jnp.zeros((1, LANES), f32)
        for ch in range(L // FCH):
            x = z_ref[pl.ds(ch * FCH, FCH), :] + fb_ref[...]
            lf = jnp.minimum(x, 0.0) - jnp.log(1.0 + jnp.exp(-jnp.abs(x)))
            f_ref[pl.ds(ch * FCH, FCH), :] = _tri_sum(tri, lf) + carry
            carry = f_ref[pl.ds(ch * FCH + FCH - 1, 1), :]

    vm = pl.BlockSpec(memory_space=pltpu.VMEM)
    return pl.pallas_call(body, name="fgate_fwd", in_specs=[vm, vm], out_specs=vm,
                          out_shape=jax.ShapeDtypeStruct((L, LANES), f32), compiler_params=_cp())(z, fb)


def _fgate_bwd(df, z, fb, cfg):
    L = cfg.L

    def body(df_ref, z_ref, fb_ref, dz_ref, db_ref):
        r = lax.broadcasted_iota(jnp.int32, (FCH, FCH), 0)
        c = lax.broadcasted_iota(jnp.int32, (FCH, FCH), 1)
        tri = (c >= r).astype(bf16)
        carry = jnp.zeros((1, LANES), f32)
        dbs = jnp.zeros((1, LANES), f32)
        for ch in range(L // FCH - 1, -1, -1):
            suf = _tri_sum(tri, df_ref[pl.ds(ch * FCH, FCH), :]) + carry
            x = z_ref[pl.ds(ch * FCH, FCH), :] + fb_ref[...]
            dz = suf * _sigmoid(-x)
            dz_ref[pl.ds(ch * FCH, FCH), :] = dz
            dbs = dbs + _colsum(dz)
            carry = carry + _colsum(df_ref[pl.ds(ch * FCH, FCH), :])
        db_ref[...] = dbs

    vm = pl.BlockSpec(memory_space=pltpu.VMEM)
    return pl.pallas_call(body, name="fgate_bwd", in_specs=[vm, vm, vm], out_specs=[vm, vm],
                          out_shape=[jax.ShapeDtypeStruct((L, LANES), f32), jax.ShapeDtypeStruct((1, LANES), f32)],
                          compiler_params=_cp())(df, z, fb)


def _adamw(w, g, m, v, name):
    R, C = w.shape
    c1 = 1.0 - ADAM_B1 ** ADAM_STEP
    c2 = 1.0 - ADAM_B2 ** ADAM_STEP

    def fn(i, ni, wv, gv, mv, vv):
        mn = ADAM_B1 * mv + (1.0 - ADAM_B1) * gv
        vn = ADAM_B2 * vv + (1.0 - ADAM_B2) * (gv * gv)
        delta = -ADAM_LR * ((mn / c1) / (jnp.sqrt(vn / c2) + ADAM_EPS) + ADAM_WD * wv)
        return [delta, mn, vn], []
    tc = C if C % LANES else _tile(C, 1024)
    return _rowwise(fn, [(w, "rc"), (g, "rc"), (m, "rc"), (v, "rc")], [f32, f32, f32], [], L=R, C=C, tl=512, tc=tc, name=name)


def _sum_lead(x, out_dtype, name):
    n, R, C = x.shape
    tl = _tile(R, 512, HALO)
    tc = C if C % LANES else _tile(C, 1024)

    def body(x_ref, o_ref):
        acc = x_ref[0].astype(f32)
        for k in range(1, n):
            acc = acc + x_ref[k].astype(f32)
        o_ref[...] = acc.astype(o_ref.dtype)

    return pl.pallas_call(
        body, name=name, grid=(R // tl, C // tc),
        in_specs=[pl.BlockSpec((n, tl, tc), lambda i, j: (0, i, j))], out_specs=pl.BlockSpec((tl, tc), lambda i, j: (i, j)),
        out_shape=jax.ShapeDtypeStruct((R, C), out_dtype), compiler_params=_cp(("parallel", "parallel")),
    )(x)


def _add2(a, b, out_dtype, name):
    R, C = a.shape

    def fn(i, ni, av, bv):
        return [av.astype(f32) + bv.astype(f32)], []
    tc = C if C % LANES else _tile(C, 1024)
    return _rowwise(fn, [(a, "rc"), (b, "rc")], [out_dtype], [], L=R, C=C, tl=512, tc=tc, name=name)[0]


ANY = pl.BlockSpec(memory_space=pl.ANY)
LOCAL_CHUNKS = 4


def _place():
    x, y, c = lax.axis_index("x"), lax.axis_index("y"), lax.axis_index("c")
    return x, y, c


def _allgather8(blocks, name):
    return _run_rider(_gather_rider(blocks), name)


def _gather_rider(blocks, middle_at=(1, 2)):
    n = len(blocks)

    def steps(ins, outs, sems):
        send_sems, recv_sems, local_sems = sems
        x, y, c = _place()
        me, sibling = (x, y, c), (x, y, 1 - c)
        chips = [(1 - x, y), (x, 1 - y), (1 - x, 1 - y)]

        def slot(a, dev):
            return outs[a].at[4 * dev[0] + 2 * dev[1] + dev[2]]

        def copy(a, k, block, to, src=None):
            return pltpu.make_async_remote_copy(
                src_ref=slot(a, block) if src is None else src, dst_ref=slot(a, block),
                send_sem=send_sems.at[a * 7 + k], recv_sem=recv_sems.at[a * 7 + k], device_id=to, device_id_type=MESH)

        def mine():
            out = []
            for a in range(n):
                rows = blocks[a].shape[0]
                k = LOCAL_CHUNKS if rows % (LOCAL_CHUNKS * HALO) == 0 else 1
                for i in range(k):
                    piece = pl.ds(i * (rows // k), rows // k)
                    out.append(pltpu.make_async_copy(ins[a].at[piece], slot(a, me).at[piece], local_sems.at[a * LOCAL_CHUNKS + i]))
            return out

        def first():
            out = []
            for a in range(n):
                out.append(copy(a, 0, me, sibling, src=ins[a]))
                out += [copy(a, 1 + j, me, (*chip, c), src=ins[a]) for j, chip in enumerate(chips)]
            return out

        def passed():
            return [copy(a, 4 + j, (*chip, c), sibling) for j, chip in enumerate(chips) for a in range(n)]

        def start():
            for cp in mine() + first():
                cp.start()

        def middle():
            for j, chip in enumerate(chips):
                for a in range(n):
                    copy(a, 1 + j, (*chip, c), me).wait_recv()
                    copy(a, 4 + j, (*chip, c), sibling).start()

        def finish():
            for a in range(n):
                copy(a, 0, sibling, me).wait_recv()
            for j, chip in enumerate(chips):
                for a in range(n):
                    copy(a, 4 + j, (*chip, 1 - c), me).wait_recv()
            for cp in first() + passed():
                cp.wait_send()
            for cp in mine():
                cp.wait()
        return start, middle, finish

    return dict(ins=list(blocks), out_shapes=[jax.ShapeDtypeStruct((N_DEV,) + b.shape, b.dtype) for b in blocks],
                sems=[pltpu.SemaphoreType.DMA((7 * n,)), pltpu.SemaphoreType.DMA((7 * n,)),
                      pltpu.SemaphoreType.DMA((LOCAL_CHUNKS * n,))],
                steps=steps, middle_at=middle_at)


def _run_rider(rider, name):
    ni, no = len(rider["ins"]), len(rider["out_shapes"])

    def body(*refs):
        start, middle, finish = rider["steps"](refs[:ni], refs[ni:ni + no], refs[ni + no:])
        start()
        if middle is not None:
            middle()
        finish()

    outs = pl.pallas_call(body, name=name, in_specs=[ANY] * ni, out_specs=[ANY] * no, out_shape=rider["out_shapes"],
                          scratch_shapes=rider["sems"])(*rider["ins"])
    return list(outs)


def _host_call(body, *, name, grid, in_specs, out_specs, out_shape, scratch_shapes, args, prefetch=(), rider=None):
    npre, nin, nout, nscr = len(prefetch), len(in_specs), len(out_specs), len(scratch_shapes)
    r_in, r_out, r_scr = (rider["ins"], rider["out_shapes"], rider["sems"]) if rider else ([], [], [])
    nri, nro = len(r_in), len(r_out)

    def kern(*refs):
        pre, rest = refs[:npre], refs[npre:]
        cin, rin = rest[:nin], rest[nin:nin + nri]
        o0 = nin + nri
        cout, rout = rest[o0:o0 + nout], rest[o0 + nout:o0 + nout + nro]
        s0 = o0 + nout + nro
        cscr, rscr = rest[s0:s0 + nscr], rest[s0 + nscr:]
        if rider:
            ids = [pl.program_id(d) for d in range(len(grid))]
            rest_zero = functools.reduce(jnp.logical_and, [i == 0 for i in ids[1:]], True)
            start, middle, finish = rider["steps"](rin, rout, rscr)
            pl.when(jnp.logical_and(ids[0] == 0, rest_zero))(start)
            if middle is not None:
                num, den = rider.get("middle_at", (1, 2))
                pl.when(jnp.logical_and(ids[0] == grid[0] * num // den, rest_zero))(middle)
        body(*pre, *cin, *cout, *cscr)
        if rider:
            pl.when(functools.reduce(jnp.logical_and, [i == g - 1 for i, g in zip(ids, grid)]))(finish)

    res = pl.pallas_call(
        kern, name=name,
        grid_spec=pltpu.PrefetchScalarGridSpec(num_scalar_prefetch=npre, grid=grid, in_specs=list(in_specs) + [ANY] * nri,
                                               out_specs=list(out_specs) + [ANY] * nro,
                                               scratch_shapes=list(scratch_shapes) + list(r_scr)),
        out_shape=list(out_shape) + list(r_out),
        compiler_params=_cp(("arbitrary",) * len(grid) if rider else ("parallel",) + ("arbitrary",) * (len(grid) - 1)),
    )(*prefetch, *args, *r_in)
    return list(res[:nout]), list(res[nout:])


def _sibling_send(halves, name):
    n = len(halves)

    def body(*refs):
        ins, outs = refs[:n], refs[n:2 * n]
        send_sems, recv_sems = refs[2 * n:]
        x, y, c = _place()
        sends = [pltpu.make_async_remote_copy(src_ref=ins[a], dst_ref=outs[a], send_sem=send_sems.at[a],
                                              recv_sem=recv_sems.at[a], device_id=(x, y, 1 - c), device_id_type=MESH)
                 for a in range(n)]
        for cp in sends:
            cp.start()
        for cp in sends:
            cp.wait_recv()
        for cp in sends:
            cp.wait_send()

    outs = pl.pallas_call(
        body, name=name, in_specs=[ANY] * n, out_specs=[ANY] * n,
        out_shape=[jax.ShapeDtypeStruct(h.shape, h.dtype) for h in halves],
        scratch_shapes=[pltpu.SemaphoreType.DMA((n,)), pltpu.SemaphoreType.DMA((n,))],
    )(*halves)
    return list(outs)


def _sibling_swap_halves(grads, name):
    n = len(grads)

    def body(*refs):
        ins, outs = refs[:n], refs[n:2 * n]
        send_sems, recv_sems = refs[2 * n:]
        x, y, c = _place()
        sends = [pltpu.make_async_remote_copy(src_ref=ins[a].at[:, 1 - c], dst_ref=outs[a], send_sem=send_sems.at[a],
                                              recv_sem=recv_sems.at[a], device_id=(x, y, 1 - c), device_id_type=MESH)
                 for a in range(n)]
        for cp in sends:
            cp.start()
        for cp in sends:
            cp.wait_recv()
        for cp in sends:
            cp.wait_send()

    outs = pl.pallas_call(
        body, name=name, in_specs=[ANY] * n, out_specs=[ANY] * n,
        out_shape=[jax.ShapeDtypeStruct((4,) + g.shape[2:], g.dtype) for g in grads],
        scratch_shapes=[pltpu.SemaphoreType.DMA((n,)), pltpu.SemaphoreType.DMA((n,))],
    )(*grads)
    return list(outs)


def _chip_scatter(parts, name):
    return _run_rider(_scatter_rider(parts), name)


def _scatter_rider(parts):
    n = len(parts)

    def steps(ins, outs, sems):
        send_sems, recv_sems = sems
        x, y, c = _place()
        chips = [(1 - x, y), (x, 1 - y), (1 - x, 1 - y)]

        def sends():
            return [pltpu.make_async_remote_copy(
                src_ref=ins[a].at[2 * px + py], dst_ref=outs[a].at[j], send_sem=send_sems.at[a * 3 + j],
                recv_sem=recv_sems.at[a * 3 + j], device_id=(px, py, c), device_id_type=MESH)
                for a in range(n) for j, (px, py) in enumerate(chips)]

        def start():
            for cp in sends():
                cp.start()

        def finish():
            for cp in sends():
                cp.wait_recv()
            for cp in sends():
                cp.wait_send()
        return start, None, finish

    return dict(ins=list(parts), out_shapes=[jax.ShapeDtypeStruct((3,) + p.shape[1:], p.dtype) for p in parts],
                sems=[pltpu.SemaphoreType.DMA((3 * n,)), pltpu.SemaphoreType.DMA((3 * n,))], steps=steps)


def _sum_parts(own, got, chip, name):
    _, R, C = own.shape
    tl = _tile(R, 512, HALO)
    tc = C if C % LANES else _tile(C, 1024)

    def body(chip_ref, own_ref, got_ref, o_ref):
        acc = own_ref[...].astype(f32)
        for k in range(3):
            acc = acc + got_ref[k].astype(f32)
        o_ref[...] = acc

    return pl.pallas_call(
        body, name=name,
        grid_spec=pltpu.PrefetchScalarGridSpec(
            num_scalar_prefetch=1, grid=(R // tl, C // tc),
            in_specs=[pl.BlockSpec((None, tl, tc), lambda i, j, ch: (ch[0], i, j)),
                      pl.BlockSpec((3, tl, tc), lambda i, j, ch: (0, i, j))],
            out_specs=pl.BlockSpec((tl, tc), lambda i, j, ch: (i, j))),
        out_shape=jax.ShapeDtypeStruct((R, C), f32), compiler_params=_cp(("parallel", "parallel")),
    )(chip, own, got)


def _adamw_halves(w, m, v, g_mine, g_other, core, name, rider=None):
    NL, R, C = w.shape
    r = R // 2
    tl = _tile(r, 512, HALO)
    tc = C if C % LANES else _tile(C, 1024)
    nh = r // tl
    c1 = 1.0 - ADAM_B1 ** ADAM_STEP
    c2 = 1.0 - ADAM_B2 ** ADAM_STEP

    def body(core_ref, w_ref, m_ref, v_ref, gm_ref, go_ref, g_out, d_out, m_out, v_out):
        i = pl.program_id(1)
        mine = lax.div(i, nh) == core_ref[0]
        gv = jnp.where(mine, gm_ref[...], go_ref[...])
        mn = ADAM_B1 * m_ref[...] + (1.0 - ADAM_B1) * gv
        vn = ADAM_B2 * v_ref[...] + (1.0 - ADAM_B2) * (gv * gv)
        g_out[...] = gv
        d_out[...] = -ADAM_LR * ((mn / c1) / (jnp.sqrt(vn / c2) + ADAM_EPS) + ADAM_WD * w_ref[...])
        m_out[...] = mn
        v_out[...] = vn

    full = pl.BlockSpec((None, tl, tc), lambda l, i, j, co: (l, i, j))
    mine_spec = pl.BlockSpec((None, tl, tc), lambda l, i, j, co: (l, jnp.clip(i - co[0] * nh, 0, nh - 1), j))
    other_spec = pl.BlockSpec((None, tl, tc), lambda l, i, j, co: (l, jnp.clip(i - (1 - co[0]) * nh, 0, nh - 1), j))
    outs, extra = _host_call(
        body, name=name, grid=(NL, R // tl, C // tc), in_specs=[full, full, full, mine_spec, other_spec],
        out_specs=[full] * 4, out_shape=[jax.ShapeDtypeStruct((NL, R, C), f32)] * 4, scratch_shapes=[],
        args=(w, m, v, g_mine, g_other), prefetch=(core,), rider=rider)
    return (*outs, extra)


def _s5_discretize(log_step, a_re, a_im, b_re, b_im):
    step = jnp.exp(log_step)[:, None]
    mag = jnp.exp(a_re * step)
    abar_re = mag * jnp.cos(a_im * step)
    abar_im = mag * jnp.sin(a_im * step)
    den = a_re * a_re + a_im * a_im
    nr = abar_re - 1.0
    fr = (nr * a_re + abar_im * a_im) / den
    fi = (abar_im * a_re - nr * a_im) / den
    bbar_re = fr[..., None] * b_re - fi[..., None] * b_im
    bbar_im = fr[..., None] * b_im + fi[..., None] * b_re
    return abar_re, abar_im, bbar_re, bbar_im


def _s5_prepare(p, cfg):
    abar_re, abar_im, bbar_re, bbar_im = _s5_discretize(p["log_step"], p["a_re"], p["a_im"], p["b_re"], p["b_im"])
    step = jnp.exp(p["log_step"])[:, None]
    arow, tab = _s5_tables(abar_re, abar_im, p["a_re"], p["a_im"], step, cfg)
    bmat, cmat = _s5_mats(bbar_re, bbar_im, p["c_re"], p["c_im"], cfg)
    return dict(arow=arow, tab=tab, bmat=bmat, cmat=cmat, drow=p["d"].reshape(1, cfg.D))


def _s5_param_grads(p, dbmat, dcmat, dabar, dd, cfg):
    J, P = cfg.G // 8, cfg.P
    dbb_re, dbb_im, dc_re, dc_im = _s5_unmats(dbmat, dcmat, cfg)
    da = dabar.reshape(J, 2, 8, P)
    da_re, da_im = da[:, 0].reshape(cfg.G, P), da[:, 1].reshape(cfg.G, P)
    _, vjp = jax.vjp(_s5_discretize, p["log_step"], p["a_re"], p["a_im"], p["b_re"], p["b_im"])
    dls, dare, daim, dbre, dbim = vjp((da_re, da_im, dbb_re, dbb_im))
    return dict(log_step=dls, a_re=dare, a_im=daim, b_re=dbre, b_im=dbim, c_re=dc_re, c_im=dc_im, d=dd.reshape(cfg.G, cfg.H))


def _resid_epi(acc, xv, gv):
    return xv + gv * acc, acc


def _ffn_fwd(x_in, g_norm, sc, sh, gate, W, exch, conv_w, conv_b, cfg, tag):
    h = _norm_mod_fwd(x_in, g_norm, sc, sh, cfg, f"ffn_norm_{tag}")
    rider = exch.rider(f"ffn_up_{tag}")
    a = _mm(h, W[f"ffn_w_up{tag}"], mode="nn", b4=True, tn=1408, out_dtypes=(bf16,), name=f"ffn_up_{tag}", rider=rider)
    if rider:
        a, extra = a
        W.update(exch.done(f"ffn_up_{tag}", extra))
    act = _conv_act_fwd(a, conv_w, conv_b, cfg)
    rider = exch.rider(f"ffn_down_{tag}")
    res = _mm(act, W[f"ffn_w_down{tag}"], mode="nn", extras=[(x_in, "mn"), (gate, "n")], epi=_resid_epi,
              out_dtypes=(f32, bf16), name=f"ffn_down_{tag}", rider=rider)
    if rider:
        res, extra = res
        W.update(exch.done(f"ffn_down_{tag}", extra))
    x_out, out = res
    return x_out, dict(h=h, a=a, act=act, out=out)


def _ffn_bwd(dx, x_in, sv, g_norm, sc, gate, w_up4, w_down, conv_w, conv_b, cfg, tag):
    F = cfg.F
    dout, dgate = _gate_bwd(dx, sv["out"], gate, cfg, f"ffn_gate_bwd_{tag}")
    dact = _mm(dout, w_down, mode="nt", tn=1408, out_dtypes=(bf16,), name=f"ffn_dact_{tag}")
    dw_down = _mm(sv["act"], dout, mode="tn", tm=1408, out_dtypes=(bf16,), name=f"ffn_dwdown_{tag}")
    dcu, dcv, dwu, dwv, dbu, dbv = _conv_act_bwd1(dact, sv["a"], conv_w, conv_b, cfg)
    dau = _conv_bwd2(dcu, conv_w[:, :F], cfg, f"conv_bwd2u_{tag}")
    dav = _conv_bwd2(dcv, conv_w[:, F:], cfg, f"conv_bwd2v_{tag}")
    da = jnp.concatenate([dau, dav], axis=1)
    dh = _mm(da, w_up4, mode="nt", b4=True, tk=1408, out_dtypes=(bf16,), name=f"ffn_dh_{tag}")
    dw_up = _mm(sv["h"], da, mode="tn", out4=True, tn=1408, out_dtypes=(bf16,), name=f"ffn_dwup_{tag}")
    dx_in, A, B = _norm_mod_bwd(dh, x_in, g_norm, sc, dx, cfg, f"ffn_norm_bwd_{tag}")
    small = dict(norm_g=(1.0 + sc) * A, sc=g_norm * A, sh=B, gate=dgate,
                 conv_w=jnp.concatenate([dwu, dwv], axis=1), conv_b=jnp.concatenate([dbu, dbv], axis=1))
    return dx_in, dw_up, dw_down, small


class _NoExchange:
    def rider(self, key, grads=None):
        return None

    def done(self, key, extra):
        return {}


def _local_step(cfg, x, tgt, mod, W, sp, exch=None):
    D, NH = cfg.D, cfg.NH
    exch = exch or _NoExchange()
    W, big = dict(W), {}

    def hand_over(key, grads):
        rider = exch.rider(key, grads)
        if rider is None:
            big.update(grads)
        return rider
    row = lambda v: v.reshape(1, -1)
    nmg0, nmg1 = row(sp["norm_mix_g"][0]), row(sp["norm_mix_g"][1])
    nfg0, nfg1 = row(sp["norm_ffn_g"][0]), row(sp["norm_ffn_g"][1])
    kvg, fng = row(sp["kv_norm_g"]), row(sp["final_norm_g"])
    cw0, cw1 = sp["ffn_conv_w"][0], sp["ffn_conv_w"][1]
    cb0, cb1 = row(sp["ffn_conv_b"][0]), row(sp["ffn_conv_b"][1])
    glu_b = row(sp["ssm_glu_b"])
    fb = jnp.zeros((1, LANES), f32).at[0, :NH].set(sp["forget_b"])
    s5p = {k: sp["ssm_" + k][0] for k in ("log_step", "a_re", "a_im", "b_re", "b_im", "c_re", "c_im", "d")}
    s5 = _s5_prepare(s5p, cfg)
    m0, m1 = mod["l0"], mod["l1"]

    h0 = _norm_mod_fwd(x, nmg0, m0["sc_m"], m0["sh_m"], cfg, "mix_norm_0")
    u = _mm(h0, W["ssm_w_in"], mode="nn", name="ssm_in")
    y, gact, cin, extra = _s5_fwd(u, s5["bmat"], s5["cmat"], s5["drow"], s5["arow"], s5["tab"], cfg, exch.rider("s5_fwd"))
    W.update(exch.done("s5_fwd", extra))

    def glu_epi(acc, bv, gv):
        pre = acc + bv
        return pre, gv.astype(f32) * _sigmoid(pre)
    pre, z = _mm(gact, W["ssm_glu_w"], mode="nn", extras=[(glu_b, "n"), (gact, "mn")], epi=glu_epi,
                 out_dtypes=(f32, bf16), name="ssm_glu")
    x1, out_m0 = _mm(z, W["ssm_w_out"], mode="nn", extras=[(x, "mn"), (m0["g_m"], "n")], epi=_resid_epi,
                     out_dtypes=(f32, bf16), name="ssm_out")
    x2, ffn0 = _ffn_fwd(x1, nfg0, m0["sc_f"], m0["sh_f"], m0["g_f"], W, exch, cw0, cb0, cfg, "0")

    hk = _norm_mod_fwd(x2, kvg, mod["sc_kv"], mod["sh_kv"], cfg, "kv_norm")
    kvb = _mm(hk, W["kv_w"], mode="nn", out_dtypes=(bf16,), name="kv_proj")
    zf = _mm(hk, W["kv_wf"], mode="nn", name="kv_fproj")
    fc = _fgate_fwd(zf, fb, cfg)
    fct = fc[:, :NH].T
    fk = fct[:, None, :]

    h1 = _norm_mod_fwd(x2, nmg1, m1["sc_m"], m1["sh_m"], cfg, "mix_norm_1")
    q = _mm(h1, W["attn_w_q"], mode="nn", epi=lambda acc: (acc * cfg.DH ** -0.5,), out_dtypes=(bf16,), name="attn_q")
    o, lse, extra = _ta_fwd(q, kvb, fk, cfg, exch.rider("attn_fwd"))
    W.update(exch.done("attn_fwd", extra))
    x3, out_m1 = _mm(o, W["attn_w_out"], mode="nn", extras=[(x2, "mn"), (m1["g_m"], "n")], epi=_resid_epi,
                     out_dtypes=(f32, bf16), name="attn_out")
    x4, ffn1 = _ffn_fwd(x3, nfg1, m1["sc_f"], m1["sh_f"], m1["g_f"], W, exch, cw1, cb1, cfg, "1")

    dx, dfng, lcol = _final_loss(x4, fng, tgt, cfg)
    loss = (0.5 / D) * jnp.sum(lcol)

    dx, dw_up1, dw_down1, sf1 = _ffn_bwd(dx, x3, ffn1, nfg1, m1["sc_f"], m1["g_f"], W["ffn_w_up1"], W["ffn_w_down1"], cw1, cb1, cfg, "1")
    dout, dgm1 = _gate_bwd(dx, out_m1, m1["g_m"], cfg, "attn_gate_bwd")
    do = _mm(dout, W["attn_w_out"], mode="nt", out_dtypes=(bf16,), name="attn_do")
    dw_ao = _mm(o, dout, mode="tn", out_dtypes=(bf16,), name="attn_dwout")
    dq, dfq, delta, extra = _ta_bwd_dq(q, kvb, do, o, lse, fk, cfg,
                                       hand_over("attn_bwd", dict(ffn_w_up1=dw_up1, ffn_w_down1=dw_down1)))
    exch.done("attn_bwd", extra)
    dw_q = _mm(h1, dq, mode="tn", out_dtypes=(bf16,), name="attn_dwq")
    dk, dv, dfk, extra = _ta_bwd_dkv(q, kvb, do, delta, lse, fk, cfg,
                                     hand_over("attn_bwd_dkv", dict(attn_w_q=dw_q, attn_w_out=dw_ao)))
    exch.done("attn_bwd_dkv", extra)
    dh1 = _mm(dq, W["attn_w_q"], mode="nt", out_dtypes=(bf16,), name="attn_dh")
    dx, A1, B1 = _norm_mod_bwd(dh1, x2, nmg1, m1["sc_m"], dx, cfg, "mix_norm_bwd_1")

    dfc = jnp.pad((dfq[:, :, 0] + dfk[:, :, 0]).T, ((0, 0), (0, LANES - NH)))
    dzf, dfb = _fgate_bwd(dfc, zf, fb, cfg)
    dkv = jnp.concatenate([dk, dv], axis=1)
    dhk1 = _mm(dkv, W["kv_w"], mode="nt", name="kv_dh1")
    dhk = _mm(dzf, W["kv_wf"], mode="nt", extras=[(dhk1, "mn")], epi=lambda acc, e: (acc + e,), out_dtypes=(bf16,), name="kv_dh2")
    dw_kv = _mm(hk, dkv, mode="tn", out_dtypes=(bf16,), name="kv_dw")
    dw_kf = _mm(hk, dzf, mode="tn", out_dtypes=(bf16,), name="kv_dwf")
    dx, Ak, Bk = _norm_mod_bwd(dhk, x2, kvg, mod["sc_kv"], dx, cfg, "kv_norm_bwd")

    dx, dw_up0, dw_down0, sf0 = _ffn_bwd(dx, x1, ffn0, nfg0, m0["sc_f"], m0["g_f"], W["ffn_w_up0"], W["ffn_w_down0"], cw0, cb0, cfg, "0")
    dout, dgm0 = _gate_bwd(dx, out_m0, m0["g_m"], cfg, "ssm_gate_bwd")
    dz = _mm(dout, W["ssm_w_out"], mode="nt", out_dtypes=(bf16,), name="ssm_dz")
    dw_so = _mm(z, dout, mode="tn", out_dtypes=(bf16,), name="ssm_dwout")
    dpre, dgd, dglub = _glu_bwd(dz, gact, pre, cfg)
    dy = _mm(dpre, W["ssm_glu_w"], mode="nt", extras=[(dgd, "mn"), (y, "mn")],
             epi=lambda acc, e, yv: ((acc + e) * _gelu_grad(yv),), name="ssm_dy")
    dw_glu = _mm(gact, dpre, mode="tn", out_dtypes=(bf16,), name="ssm_dwglu")
    rider = hand_over("s5_bwd", dict(kv_w=jnp.concatenate([dw_kv, dw_kf[:, :NH]], axis=1), ffn_w_up0=dw_up0,
                                     ffn_w_down0=dw_down0, ssm_w_out=dw_so, ssm_glu_w=dw_glu))
    du, dbm, dcm, dab, dd, extra = _s5_bwd(u, dy, cin, s5["bmat"], s5["cmat"], s5["drow"], s5["arow"], s5["tab"], cfg, rider)
    exch.done("s5_bwd", extra)
    dh0 = _mm(du, W["ssm_w_in"], mode="nt", out_dtypes=(bf16,), name="ssm_dh")
    dw_in = _mm(h0, du, mode="tn", out_dtypes=(bf16,), name="ssm_dwin")
    dx, A0, B0 = _norm_mod_bwd(dh0, x, nmg0, m0["sc_m"], dx, cfg, "mix_norm_bwd_0")

    s5g = _s5_param_grads(s5p, dbm, dcm, dab, dd, cfg)
    big["ssm_w_in"] = dw_in
    small = dict(
        norm_mix_g=jnp.concatenate([(1.0 + m0["sc_m"]) * A0, (1.0 + m1["sc_m"]) * A1], axis=0),
        norm_ffn_g=jnp.concatenate([sf0["norm_g"], sf1["norm_g"]], axis=0),
        ssm_glu_b=dglub, kv_norm_g=(1.0 + mod["sc_kv"]) * Ak, forget_b=dfb[0, :NH],
        ffn_conv_w=jnp.stack([sf0["conv_w"], sf1["conv_w"]]), ffn_conv_b=jnp.concatenate([sf0["conv_b"], sf1["conv_b"]], axis=0),
        final_norm_g=dfng, **{"ssm_" + k: v[None] for k, v in s5g.items()})
    dmod = [jnp.concatenate([B0, nmg0 * A0, dgm0, sf0["sh"], sf0["sc"], sf0["gate"]], axis=1),
            jnp.concatenate([B1, nmg1 * A1, dgm1, sf1["sh"], sf1["sc"], sf1["gate"]], axis=1),
            jnp.concatenate([Bk, kvg * Ak], axis=1)]
    return loss, dx, big, small, dmod


WEIGHTS = ["mod_w", "mod_b", "norm_mix_g", "norm_ffn_g", "ssm_w_in", "ssm_log_step", "ssm_a_re", "ssm_a_im", "ssm_b_re",
           "ssm_b_im", "ssm_c_re", "ssm_c_im", "ssm_d", "ssm_glu_w", "ssm_glu_b", "ssm_w_out", "kv_mod_w", "kv_mod_b",
           "kv_norm_g", "kv_w", "forget_b", "attn_w_q", "attn_w_out", "ffn_w_up", "ffn_conv_w", "ffn_conv_b", "ffn_w_down",
           "final_norm_g"]
ARGS = ["x", "c"] + WEIGHTS + ["loss_target"] + ["m_" + n for n in WEIGHTS] + ["v_" + n for n in WEIGHTS]
SMALL = ["mod_b", "norm_mix_g", "norm_ffn_g", "ssm_log_step", "ssm_a_re", "ssm_a_im", "ssm_b_re", "ssm_b_im", "ssm_c_re",
         "ssm_c_im", "ssm_d", "ssm_glu_b", "kv_mod_b", "kv_norm_g", "forget_b", "ffn_conv_w", "ffn_conv_b", "final_norm_g"]
PACK_ROWS = 512


def _pack(arrs):
    flat = jnp.concatenate([a.reshape(-1).astype(f32) for a in arrs])
    unit = PACK_ROWS * LANES
    n = -(-flat.shape[0] // unit) * unit
    return jnp.pad(flat, (0, n - flat.shape[0])).reshape(-1, LANES)


def _unpack(packed, shapes):
    flat, out, off = packed.reshape(-1), [], 0
    for s in shapes:
        n = math.prod(s)
        out.append(flat[off:off + n].reshape(s))
        off += n
    return out


def _silu(v):
    return v * _sigmoid(v)


def _half(w, c, axis):
    r = w.shape[axis] // 2
    return lax.dynamic_slice_in_dim(w, c * r, r, axis=axis)


class _Exchange:
    FIRST = ["ssm_w_in", "ssm_glu_w", "ssm_w_out"]
    FWD = dict(s5_fwd=["ffn_w_up0"], ffn_up_0=["ffn_w_down0", "kv_w"], ffn_down_0=["attn_w_q", "attn_w_out"],
               attn_fwd=["ffn_w_up1", "ffn_w_down1"])
    LATE = (3, 4)

    def __init__(self, cfg, blocks, core):
        self.cfg, self.blocks, self.core = cfg, blocks, core
        self.parts, self.scattered, self.names = {}, {}, {}

    def weights(self, names, gathered):
        D, F, NH = self.cfg.D, self.cfg.F, self.cfg.NH
        W = {}
        for n, g in zip(names, gathered):
            if n.startswith("ffn_w_up"):
                W[n] = g.reshape(4, D, 2 * F // 4)
            elif n == "kv_w":
                full = g.reshape(4, D, -1).transpose(1, 0, 2).reshape(D, -1)
                W["kv_w"] = full[:, :2 * D]
                W["kv_wf"] = jnp.pad(full[:, 2 * D:], ((0, 0), (0, LANES - NH)))
            else:
                W[n] = g.reshape(-1, D)
        return W

    def sibling_sum(self, key, grads):
        D = self.cfg.D

        def blocks_of(n, g):
            if n.startswith("ffn_w_up"):
                return g.reshape(4, 2, D // 2, -1)
            if n == "kv_w":
                return g.reshape(D, 4, -1).transpose(1, 0, 2).reshape(4, 2, D // 2, -1)
            return g.reshape(4, 2, g.shape[0] // 8, g.shape[1])
        names = list(grads)
        gb = [blocks_of(n, grads[n]) for n in names]
        recv = _sibling_swap_halves(gb, f"grad_sibling_swap_{key}")
        for n, g, r in zip(names, gb, recv):
            keep = lax.dynamic_index_in_dim(g, self.core, axis=1, keepdims=False)
            rr, cc = keep.shape[1], keep.shape[2]
            self.parts[n] = _add2(keep.reshape(4 * rr, cc), r.reshape(4 * rr, cc), bf16, f"grad_add_{n}").reshape(4, rr, cc)
        return self.parts

    def rider(self, key, grads=None):
        if key in self.FWD:
            return _gather_rider([self.blocks[n] for n in self.FWD[key]], (1, 2) if key == "attn_fwd" else self.LATE)
        if grads is None:
            return None
        self.names[key] = list(grads)
        parts = self.sibling_sum(key, grads)
        return _scatter_rider([parts[n] for n in self.names[key]])

    def done(self, key, extra):
        if key in self.FWD:
            return self.weights(self.FWD[key], extra)
        self.scattered.update(zip(self.names[key], extra))
        return {}


def kernel(x, c, mod_w, mod_b, norm_mix_g, norm_ffn_g, ssm_w_in, ssm_log_step, ssm_a_re, ssm_a_im, ssm_b_re, ssm_b_im, ssm_c_re, ssm_c_im, ssm_d, ssm_glu_w, ssm_glu_b, ssm_w_out, kv_mod_w, kv_mod_b, kv_norm_g, kv_w, forget_b, attn_w_q, attn_w_out, ffn_w_up, ffn_conv_w, ffn_conv_b, ffn_w_down, final_norm_g, loss_target, m_mod_w, m_mod_b, m_norm_mix_g, m_norm_ffn_g, m_ssm_w_in, m_ssm_log_step, m_ssm_a_re, m_ssm_a_im, m_ssm_b_re, m_ssm_b_im, m_ssm_c_re, m_ssm_c_im, m_ssm_d, m_ssm_glu_w, m_ssm_glu_b, m_ssm_w_out, m_kv_mod_w, m_kv_mod_b, m_kv_norm_g, m_kv_w, m_forget_b, m_attn_w_q, m_attn_w_out, m_ffn_w_up, m_ffn_conv_w, m_ffn_conv_b, m_ffn_w_down, m_final_norm_g, v_mod_w, v_mod_b, v_norm_mix_g, v_norm_ffn_g, v_ssm_w_in, v_ssm_log_step, v_ssm_a_re, v_ssm_a_im, v_ssm_b_re, v_ssm_b_im, v_ssm_c_re, v_ssm_c_im, v_ssm_d, v_ssm_glu_w, v_ssm_glu_b, v_ssm_w_out, v_kv_mod_w, v_kv_mod_b, v_kv_norm_g, v_kv_w, v_forget_b, v_attn_w_q, v_attn_w_out, v_ffn_w_up, v_ffn_conv_w, v_ffn_conv_b, v_ffn_w_down, v_final_norm_g):
    a = dict(locals())
    assert list(a) == ARGS
    return _step(CFG, a)


def _step(cfg, a):
    D, F, NH = cfg.D, cfg.F, cfg.NH
    x_, y_, c_ = _place()
    chip, dev = 2 * x_ + y_, 4 * x_ + 2 * y_ + c_

    big_src = dict(ssm_w_in=a["ssm_w_in"][0], ssm_glu_w=a["ssm_glu_w"][0], ssm_w_out=a["ssm_w_out"][0],
                   attn_w_q=a["attn_w_q"][0], attn_w_out=a["attn_w_out"][0],
                   ffn_w_up0=a["ffn_w_up"][0], ffn_w_up1=a["ffn_w_up"][1],
                   ffn_w_down0=a["ffn_w_down"][0], ffn_w_down1=a["ffn_w_down"][1], kv_w=a["kv_w"])
    big_names = list(big_src)
    exch = _Exchange(cfg, {n: _half(big_src[n], c_, 0).astype(bf16) for n in big_names}, c_)
    first = exch.FIRST
    blocks = [exch.blocks[n] for n in first] + [_half(a["ssm_glu_b"], c_, 1), _half(a["ffn_conv_w"], c_, 2), a["c"]]
    got = _allgather8(blocks, "gather_weights")
    W = exch.weights(first, got)
    glu_b_full = got[-3].reshape(D)
    conv_w_full = got[-2].transpose(1, 2, 0, 3).reshape(2, 3, 2 * F)
    c16 = jnp.pad(got[-1].reshape(N_DEV, D), ((0, 16 - N_DEV), (0, 0)))

    mcols = [_mm(c16, a["mod_w"][l], mode="nn", a_pro=_silu, name=f"mod_fwd_{l}") for l in range(2)]
    mcols.append(_mm(c16, a["kv_mod_w"], mode="nn", a_pro=_silu, name="mod_fwd_kv"))
    widths = [m.shape[1] for m in mcols]
    mall = _allgather8([jnp.concatenate(mcols, axis=1)[:N_DEV]], "gather_mod")[0][0::2]
    offs = [0, widths[0], widths[0] + widths[1]]
    rows = []
    for off, wd, bias in zip(offs, widths, [a["mod_b"][0], a["mod_b"][1], a["kv_mod_b"]]):
        fullm = mall[:, :, off:off + wd].transpose(1, 0, 2).reshape(N_DEV, 4 * wd) + bias
        rows.append(lax.dynamic_slice_in_dim(fullm, dev, 1, axis=0))
    mod = {}
    for l in range(2):
        mod[f"l{l}"] = dict(zip(["sh_m", "sc_m", "g_m", "sh_f", "sc_f", "g_f"], jnp.split(rows[l], 6, axis=1)))
    mod["sh_kv"], mod["sc_kv"] = jnp.split(rows[2], 2, axis=1)

    sp = {n: a[n] for n in ["norm_mix_g", "norm_ffn_g", "kv_norm_g", "final_norm_g", "ffn_conv_b", "forget_b", "ssm_log_step",
                            "ssm_a_re", "ssm_a_im", "ssm_b_re", "ssm_b_im", "ssm_c_re", "ssm_c_im", "ssm_d"]}
    sp["ssm_glu_b"], sp["ffn_conv_w"] = glu_b_full, conv_w_full
    loss, dx, big, small, dmod = _local_step(cfg, a["x"][0], a["loss_target"][0], mod, W, sp, exch)
    loss = lax.psum(loss, ("x", "y", "c"))

    small["mod_b"] = jnp.concatenate([dmod[0], dmod[1]], axis=0)
    small["kv_mod_b"] = dmod[2]
    shapes = [(2, 6 * D) if n == "mod_b" else (1, D) if n == "ssm_glu_b" else (2, 3, 2 * F) if n == "ffn_conv_w"
              else a[n].shape for n in SMALL]
    small_rider = _gather_rider([_pack([small[n] for n in SMALL])])

    last = list(big)
    exch.scattered.update(zip(last, _chip_scatter([exch.sibling_sum("tail", big)[n] for n in last], "grad_chip_scatter")))
    chip1, core1 = jnp.reshape(chip, (1,)).astype(jnp.int32), jnp.reshape(c_, (1,)).astype(jnp.int32)
    mine = {n: _sum_parts(exch.parts[n], exch.scattered[n], chip1, f"grad_sum_{n}") for n in big_names}
    other = dict(zip(big_names, _sibling_send([mine[n] for n in big_names], "grad_sibling_send")))

    grads, delta, new_m, new_v = {}, {}, {}, {}
    members = dict(ffn_w_up=["ffn_w_up0", "ffn_w_up1"], ffn_w_down=["ffn_w_down0", "ffn_w_down1"], kv_w=["kv_w"],
                   ssm_w_in=["ssm_w_in"], ssm_glu_w=["ssm_glu_w"], ssm_w_out=["ssm_w_out"], attn_w_q=["attn_w_q"],
                   attn_w_out=["attn_w_out"])
    for n, parts_ in members.items():
        shp = a[n].shape
        three = lambda t: t.reshape(len(parts_), -1, shp[-1])
        g_, d_, m_, v_, extra = _adamw_halves(three(a[n]), three(a["m_" + n]), three(a["v_" + n]),
                                              jnp.stack([mine[p] for p in parts_]), jnp.stack([other[p] for p in parts_]),
                                              core1, f"adamw_{n}", small_rider if n == "ffn_w_up" else None)
        if extra:
            packs = extra[0]
        grads[n], delta[n], new_m[n], new_v[n] = g_.reshape(shp), d_.reshape(shp), m_.reshape(shp), v_.reshape(shp)

    gsmall = dict(zip(SMALL, _unpack(_sum_lead(packs, f32, "sum_small"), shapes)))
    per_dev = packs.reshape(N_DEV, -1)
    sizes = [math.prod(s) for s in shapes]
    starts = dict(zip(SMALL, [sum(sizes[:i]) for i in range(len(sizes))]))

    def rows_of(name, l, width):
        st = starts[name] + l * 6 * D
        blk = lax.dynamic_slice(per_dev, (0, st + chip * width), (N_DEV, width))
        return jnp.pad(blk, ((0, 16 - N_DEV), (0, 0)))
    g_mod_w = jnp.stack([_mm(c16, rows_of("mod_b", l, 6 * D // 4), mode="tn", a_pro=_silu, name=f"mod_dw_{l}") for l in range(2)])
    g_kv_mod_w = _mm(c16, rows_of("kv_mod_b", 0, 2 * D // 4), mode="tn", a_pro=_silu, name="mod_dw_kv")
    gsmall["ssm_glu_b"] = lax.dynamic_slice_in_dim(gsmall["ssm_glu_b"], chip * (D // 4), D // 4, axis=1)
    gsmall["ffn_conv_w"] = lax.dynamic_slice_in_dim(gsmall["ffn_conv_w"], chip * (2 * F // 4), 2 * F // 4, axis=2)

    grads.update(gsmall)
    grads["mod_w"], grads["kv_mod_w"] = g_mod_w, g_kv_mod_w
    for n in ["mod_w", "kv_mod_w"]:
        shp = a[n].shape
        two = lambda t: t.reshape(-1, shp[-1])
        d_, m_, v_ = _adamw(two(a[n]), two(grads[n]), two(a["m_" + n]), two(a["v_" + n]), f"adamw_{n}")
        delta[n], new_m[n], new_v[n] = d_.reshape(shp), m_.reshape(shp), v_.reshape(shp)
    grads = {n: grads[n].reshape(a[n].shape) for n in WEIGHTS}
    sshapes = [a[n].shape for n in SMALL]
    d_, m_, v_ = _adamw(_pack([a[n] for n in SMALL]), _pack([grads[n] for n in SMALL]), _pack([a["m_" + n] for n in SMALL]),
                        _pack([a["v_" + n] for n in SMALL]), "adamw_small")
    for n, dd_, mm_, vv_ in zip(SMALL, _unpack(d_, sshapes), _unpack(m_, sshapes), _unpack(v_, sshapes)):
        delta[n], new_m[n], new_v[n] = dd_, mm_, vv_

    return (loss, dx[None], *[grads[n] for n in WEIGHTS], *[delta[n] for n in WEIGHTS],
            *[new_m[n] for n in WEIGHTS], *[new_v[n] for n in WEIGHTS])
```

```python
import collections
import functools
import math

import jax
import jax.numpy as jnp
from jax import lax
from jax.experimental import pallas as pl
from jax.experimental.pallas import tpu as pltpu

f32 = jnp.float32
bf16 = jnp.bfloat16
MESH = pl.DeviceIdType.MESH

LANES = 128
SUBLANES = 8
VMEM_BYTES_V7X = 64 * 1024 * 1024
VMEM_LIMIT = 56 * 1024 * 1024

Cfg = collections.namedtuple("Cfg", "L D G P H NH DH F TC BQ")
CFG = Cfg(L=4096, D=2048, G=128, P=64, H=16, NH=16, DH=128, F=5632, TC=512, BQ=512)
NORM_EPS = 1e-6
ADAM_LR, ADAM_B1, ADAM_B2, ADAM_EPS, ADAM_WD, ADAM_STEP = 0.001, 0.9, 0.999, 1e-08, 0.01, 10
N_DEV = 8


def _cp(sem=None):
    return pltpu.CompilerParams(dimension_semantics=sem, vmem_limit_bytes=VMEM_LIMIT)


def _tile(dim, pref, unit=LANES):
    if dim <= pref:
        return dim
    t = (pref // unit) * unit
    while t > unit and dim % t:
        t -= unit
    assert dim % t == 0, (dim, pref)
    return t


_DIMS = {"nn": (((1,), (0,)), ((), ())), "nt": (((1,), (1,)), ((), ())), "tn": (((0,), (0,)), ((), ()))}


def _mm(a, b, *, mode, name, tm=1024, tn=1024, tk=2048, b4=False, out4=False, a_pro=None, extras=(), epi=None,
        out_dtypes=(f32,), rider=None):
    if mode == "tn":
        K, M = a.shape
    else:
        M, K = a.shape
    if b4:
        R, c4 = b.shape[1], b.shape[2]
        N = R if mode == "nt" else 4 * c4
        assert (K == 4 * c4) if mode == "nt" else (K == R)
    else:
        N = b.shape[0] if mode == "nt" else b.shape[1]
        assert K == (b.shape[1] if mode == "nt" else b.shape[0])
    n4 = N // 4
    tm = _tile(M, tm, LANES if mode == "tn" else SUBLANES * 2)
    tn = _tile(n4 if out4 or (b4 and mode != "nt") else N, tn)
    tk = _tile(b.shape[2] if (b4 and mode == "nt") else K, tk)
    nm, nn_, nk = M // tm, N // tn, K // tk

    a_spec = pl.BlockSpec((tk, tm), lambda i, j, k: (k, i)) if mode == "tn" else pl.BlockSpec((tm, tk), lambda i, j, k: (i, k))
    if b4 and mode == "nt":
        q = b.shape[2] // tk
        b_spec = pl.BlockSpec((None, tn, tk), lambda i, j, k: (lax.div(k, q), j, lax.rem(k, q)))
    elif b4:
        q = b.shape[2] // tn
        b_spec = pl.BlockSpec((None, tk, tn), lambda i, j, k: (lax.div(j, q), k, lax.rem(j, q)))
    elif mode == "nt":
        b_spec = pl.BlockSpec((tn, tk), lambda i, j, k: (j, k))
    else:
        b_spec = pl.BlockSpec((tk, tn), lambda i, j, k: (k, j))
    ex_specs = []
    for arr, kind in extras:
        if kind == "mn":
            ex_specs.append(pl.BlockSpec((tm, tn), lambda i, j, k: (i, j)))
        else:
            ex_specs.append(pl.BlockSpec((1, tn), lambda i, j, k: (0, j)))
    if out4:
        qo = n4 // tn
        o_spec = pl.BlockSpec((None, tm, tn), lambda i, j, k: (lax.div(j, qo), i, lax.rem(j, qo)))
        o_shapes = [jax.ShapeDtypeStruct((4, M, n4), dt) for dt in out_dtypes]
    else:
        o_spec = pl.BlockSpec((tm, tn), lambda i, j, k: (i, j))
        o_shapes = [jax.ShapeDtypeStruct((M, N), dt) for dt in out_dtypes]
    ne, no = len(extras), len(out_dtypes)
    dims = _DIMS[mode]

    def body(a_ref, b_ref, *rest):
        ex_refs, o_refs, acc_ref = rest[:ne], rest[ne:ne + no], rest[ne + no]
        k = pl.program_id(2)

        @pl.when(k == 0)
        def _():
            acc_ref[...] = jnp.zeros_like(acc_ref)

        av = a_ref[...]
        if a_pro is not None:
            av = a_pro(av)
        acc_ref[...] += lax.dot_general(av.astype(bf16), b_ref[...].astype(bf16), dims, preferred_element_type=f32)

        @pl.when(k == nk - 1)
        def _():
            acc = acc_ref[...]
            outs = (acc,) if epi is None else epi(acc, *[r[...] for r in ex_refs])
            for o_ref, o in zip(o_refs, outs):
                o_ref[...] = o.astype(o_ref.dtype)

    res, extra = _host_call(
        body, name=name, grid=(nm, nn_, nk), in_specs=[a_spec, b_spec] + ex_specs, out_specs=[o_spec] * no,
        out_shape=o_shapes, scratch_shapes=[pltpu.VMEM((tm, tn), f32)], args=(a, b, *[e[0] for e in extras]), rider=rider)
    res = res[0] if no == 1 else res
    return (res, extra) if rider else res


HALO = 16


def _rowwise(fn, ins, outs, accs, *, L, C, tl, tc, name):
    tl = _tile(L, tl, HALO)
    tc = _tile(C, tc)
    ni, nj = L // tl, C // tc
    hb = tl // HALO
    nh = L // HALO
    in_specs = []
    for spec in ins:
        kind = spec[1]
        off = spec[2] if len(spec) > 2 else 0
        if kind == "rc":
            in_specs.append(pl.BlockSpec((tl, tc), lambda j, i, off=off: (i, j + off)))
        elif kind == "c":
            in_specs.append(pl.BlockSpec((1, tc), lambda j, i, off=off: (0, j + off)))
        elif kind == "c3":
            in_specs.append(pl.BlockSpec((3, tc), lambda j, i, off=off: (0, j + off)))
        elif kind == "prev":
            in_specs.append(pl.BlockSpec((HALO, tc), lambda j, i, off=off: (jnp.maximum(i * hb - 1, 0), j + off)))
        elif kind == "next":
            in_specs.append(pl.BlockSpec((HALO, tc), lambda j, i, off=off: (jnp.minimum((i + 1) * hb, nh - 1), j + off)))
        else:
            raise ValueError(kind)
    out_specs = [pl.BlockSpec((tl, tc), lambda j, i: (i, j)) for _ in outs]
    out_specs += [pl.BlockSpec((r, tc), lambda j, i: (0, j)) for r in accs]
    out_shape = [jax.ShapeDtypeStruct((L, C), dt) for dt in outs] + [jax.ShapeDtypeStruct((r, C), f32) for r in accs]
    nin, nout, nacc = len(ins), len(outs), len(accs)

    def body(*refs):
        i = pl.program_id(1)
        tiles = [r[...] for r in refs[:nin]]
        o_vals, a_vals = fn(i, ni, *tiles)
        for r, v in zip(refs[nin:nin + nout], o_vals):
            r[...] = v.astype(r.dtype)
        if nacc:
            @pl.when(i == 0)
            def _():
                for r in refs[nin + nout:]:
                    r[...] = jnp.zeros_like(r)
            for r, v in zip(refs[nin + nout:], a_vals):
                r[...] += v

    res = pl.pallas_call(
        body, name=name, grid=(nj, ni), in_specs=in_specs, out_specs=out_specs, out_shape=out_shape,
        compiler_params=_cp(("parallel", "arbitrary")),
    )(*[s[0] for s in ins])
    return res


def _colsum(v):
    return jnp.sum(v, axis=0, keepdims=True)


def _sigmoid(x):
    return 1.0 / (1.0 + jnp.exp(-x))


_GELU_C = math.sqrt(2.0 / math.pi)


def _gelu(y):
    t = jnp.tanh(_GELU_C * (y + 0.044715 * y * y * y))
    return 0.5 * y * (1.0 + t)


def _gelu_grad(y):
    y2 = y * y
    t = jnp.tanh(_GELU_C * (y + 0.044715 * y * y2))
    return 0.5 * (1.0 + t) + 0.5 * y * (1.0 - t * t) * _GELU_C * (1.0 + 3.0 * 0.044715 * y2)


def _norm_mod_fwd(x, g, sc, sh, cfg, name):
    def fn(i, ni, xv, gv, scv, shv):
        rstd = lax.rsqrt(jnp.mean(xv * xv, axis=-1, keepdims=True) + NORM_EPS)
        return [xv * rstd * gv * (1.0 + scv) + shv], []
    return _rowwise(fn, [(x, "rc"), (g, "c"), (sc, "c"), (sh, "c")], [bf16], [], L=cfg.L, C=cfg.D, tl=256, tc=cfg.D, name=name)[0]


def _norm_mod_bwd(dh, x, g, sc, dres, cfg, name):
    def fn(i, ni, dhv, xv, gv, scv, *rest):
        dhv = dhv.astype(f32)
        rstd = lax.rsqrt(jnp.mean(xv * xv, axis=-1, keepdims=True) + NORM_EPS)
        xh = xv * rstd
        dxh = dhv * (gv * (1.0 + scv))
        dx = rstd * (dxh - xh * jnp.mean(dxh * xh, axis=-1, keepdims=True))
        if rest:
            dx = dx + rest[0]
        return [dx], [_colsum(dhv * xh), _colsum(dhv)]
    ins = [(dh, "rc"), (x, "rc"), (g, "c"), (sc, "c")] + ([(dres, "rc")] if dres is not None else [])
    return _rowwise(fn, ins, [f32], [1, 1], L=cfg.L, C=cfg.D, tl=256, tc=cfg.D, name=name)


def _final_loss(x, g, tgt, cfg):
    D = cfg.D

    def fn(i, ni, xv, gv, tv):
        rstd = lax.rsqrt(jnp.mean(xv * xv, axis=-1, keepdims=True) + NORM_EPS)
        xh = xv * rstd
        err = xh * gv - tv
        dy = err * (1.0 / D)
        dxh = dy * gv
        dx = rstd * (dxh - xh * jnp.mean(dxh * xh, axis=-1, keepdims=True))
        return [dx], [_colsum(dy * xh), _colsum(err * err)]
    return _rowwise(fn, [(x, "rc"), (g, "c"), (tgt, "rc")], [f32], [1, 1], L=cfg.L, C=D, tl=256, tc=D, name="final_loss")


def _gate_bwd(dx, out, gate, cfg, name):
    def fn(i, ni, dxv, ov, gv):
        return [dxv * gv], [_colsum(dxv * ov.astype(f32))]
    return _rowwise(fn, [(dx, "rc"), (out, "rc"), (gate, "c")], [bf16], [1], L=cfg.L, C=cfg.D, tl=512, tc=cfg.D, name=name)


def _glu_bwd(dz, g, pre, cfg):
    def fn(i, ni, dzv, gv, pv):
        dzv = dzv.astype(f32)
        gv = gv.astype(f32)
        s = _sigmoid(pv)
        dpre = dzv * gv * s * (1.0 - s)
        return [dpre, dzv * s], [_colsum(dpre)]
    return _rowwise(fn, [(dz, "rc"), (g, "rc"), (pre, "rc")], [bf16, f32], [1], L=cfg.L, C=cfg.D, tl=512, tc=cfg.D, name="glu_bwd")


def _shift_rows(av, pv, k, i):
    rows = lax.broadcasted_iota(jnp.int32, av.shape, 0)
    cur = pltpu.roll(av, k, 0)
    prev = pltpu.roll(pv, k, 0)
    prev = jnp.where(i > 0, prev, 0.0)
    prev_full = jnp.concatenate([prev, jnp.zeros((av.shape[0] - pv.shape[0], av.shape[1]), av.dtype)], axis=0) \
        if av.shape[0] > pv.shape[0] else prev
    return jnp.where(rows >= k, cur, prev_full)


def _shift_rows_up(av, nv, k, i, ni):
    n, h = av.shape[0], nv.shape[0]
    rows = lax.broadcasted_iota(jnp.int32, av.shape, 0)
    cur = pltpu.roll(av, n - k, 0)
    nxt = pltpu.roll(nv, h - k, 0)
    nxt = jnp.where(i < ni - 1, nxt, 0.0)
    nxt_full = jnp.concatenate([jnp.zeros((n - h, av.shape[1]), av.dtype), nxt], axis=0) if n > h else nxt
    return jnp.where(rows < n - k, cur, nxt_full)


def _conv3(av, pv, w, i):
    return w[0:1] * _shift_rows(av, pv, 2, i) + w[1:2] * _shift_rows(av, pv, 1, i) + w[2:3] * av


def _conv_act_fwd(a, conv_w, conv_b, cfg):
    F = cfg.F
    tc = _tile(F, 1408)
    nb = F // tc

    def fn(i, ni, au, av, pu, pv, wu, wv, bu, bv):
        cu = _conv3(au.astype(f32), pu.astype(f32), wu, i) + bu
        cv = _conv3(av.astype(f32), pv.astype(f32), wv, i) + bv
        return [cu * _sigmoid(cu) * cv], []
    ins = [(a, "rc"), (a, "rc", nb), (a, "prev"), (a, "prev", nb), (conv_w, "c3"), (conv_w, "c3", nb), (conv_b, "c"), (conv_b, "c", nb)]
    return _rowwise(fn, ins, [bf16], [], L=cfg.L, C=F, tl=512, tc=tc, name="conv_act_fwd")[0]


def _conv_act_bwd1(dact, a, conv_w, conv_b, cfg):
    F = cfg.F
    tc = _tile(F, 1408)
    nb = F // tc

    def fn(i, ni, dav, au, av, pu, pv, wu, wv, bu, bv):
        dav = dav.astype(f32)
        au, av, pu, pv = au.astype(f32), av.astype(f32), pu.astype(f32), pv.astype(f32)
        au1, au2 = _shift_rows(au, pu, 1, i), _shift_rows(au, pu, 2, i)
        av1, av2 = _shift_rows(av, pv, 1, i), _shift_rows(av, pv, 2, i)
        cu = wu[0:1] * au2 + wu[1:2] * au1 + wu[2:3] * au + bu
        cv = wv[0:1] * av2 + wv[1:2] * av1 + wv[2:3] * av + bv
        s = _sigmoid(cu)
        dcu = dav * cv * (s * (1.0 + cu * (1.0 - s)))
        dcv = dav * cu * s
        dwu = jnp.concatenate([_colsum(dcu * au2), _colsum(dcu * au1), _colsum(dcu * au)], axis=0)
        dwv = jnp.concatenate([_colsum(dcv * av2), _colsum(dcv * av1), _colsum(dcv * av)], axis=0)
        return [dcu, dcv], [dwu, dwv, _colsum(dcu), _colsum(dcv)]
    ins = [(dact, "rc"), (a, "rc"), (a, "rc", nb), (a, "prev"), (a, "prev", nb), (conv_w, "c3"), (conv_w, "c3", nb),
           (conv_b, "c"), (conv_b, "c", nb)]
    return _rowwise(fn, ins, [bf16, bf16], [3, 3, 1, 1], L=cfg.L, C=F, tl=512, tc=tc, name="conv_act_bwd1")


def _conv_bwd2(dc, w, cfg, name):
    F = cfg.F
    tc = _tile(F, 1408)

    def fn(i, ni, dcv, nxt, wv):
        dcv, nxt = dcv.astype(f32), nxt.astype(f32)
        return [wv[2:3] * dcv + wv[1:2] * _shift_rows_up(dcv, nxt, 1, i, ni) + wv[0:1] * _shift_rows_up(dcv, nxt, 2, i, ni)], []
    return _rowwise(fn, [(dc, "rc"), (dc, "next"), (w, "c3")], [bf16], [], L=cfg.L, C=F, tl=512, tc=tc, name=name)[0]


NSLAB = 8


def _s5_tables(abar_re, abar_im, lam_re, lam_im, step, cfg):
    J = cfg.G // 8
    expo = jnp.array([r + 1 for r in range(8)] + [8 * 2 ** p for p in range(8)], f32)[:, None, None]
    mag = jnp.exp(lam_re * step * expo)
    ang = lam_im * step * expo
    t_re = (mag * jnp.cos(ang)).reshape(16, J, 8 * cfg.P).transpose(1, 0, 2)
    t_im = (mag * jnp.sin(ang)).reshape(16, J, 8 * cfg.P).transpose(1, 0, 2)
    tab = jnp.concatenate([t_re, t_im], axis=-1)
    arow = jnp.concatenate([abar_re.reshape(J, 1, 8 * cfg.P), abar_im.reshape(J, 1, 8 * cfg.P)], axis=-1)
    return arow, tab


def _s5_mats(bbar_re, bbar_im, c_re, c_im, cfg):
    J, P, H = cfg.G // 8, cfg.P, cfg.H
    eye = jnp.eye(8, dtype=f32)

    def bd_in(bb):
        bb = bb.reshape(J, 8, P, H)
        return jnp.einsum("jgph,gk->jghkp", bb, eye).reshape(J, 8 * H, 8 * P)

    def bd_out(cc):
        cc = cc.reshape(J, 8, H, P)
        return jnp.einsum("jghp,gk->jgpkh", cc, eye).reshape(J, 8 * P, 8 * H)

    bmat = jnp.concatenate([bd_in(bbar_re), bd_in(bbar_im)], axis=2).astype(bf16)
    cmat = jnp.concatenate([bd_out(c_re), -bd_out(c_im)], axis=1).astype(bf16)
    return bmat, cmat


def _s5_unmats(dbmat, dcmat, cfg):
    J, P, H = cfg.G // 8, cfg.P, cfg.H
    eye = jnp.eye(8, dtype=f32)
    db = dbmat.reshape(J, 8, H, 2, 8, P)
    db = jnp.einsum("jghckp,gk->cjgph", db, eye).reshape(2, cfg.G, P, H)
    dc = dcmat.reshape(J, 2, 8, P, 8, H)
    dc = jnp.einsum("jcgpkh,gk->cjghp", dc, eye).reshape(2, cfg.G, H, P)
    return db[0], db[1], dc[0], -dc[1]


def _chunk_scan(x_ref, row0, nt, arow_ref, tab_ref, c0, reverse):
    sg = -1.0 if reverse else 1.0
    rows = lax.broadcasted_iota(jnp.int32, (nt, LANES), 0)
    order = list(range(7, -1, -1)) if reverse else list(range(8))

    def ld(k, r):
        return x_ref[k, pl.ds(row0 + r, nt, stride=8), :]

    def tab(row, k):
        return tab_ref[pl.ds(row, 1), pl.ds(k * LANES, LANES)]

    carries = [None] * NSLAB
    for k in range(4):
        ar = arow_ref[:, pl.ds(k * LANES, LANES)]
        ai = sg * arow_ref[:, pl.ds((4 + k) * LANES, LANES)]
        sr, si = ld(k, order[0]), ld(4 + k, order[0])
        for r in order[1:]:
            sr, si = ar * sr - ai * si + ld(k, r), ar * si + ai * sr + ld(4 + k, r)
        if reverse:
            cr = jnp.where(rows == nt - 1, c0[k], pltpu.roll(sr, nt - 1, 0))
            ci = jnp.where(rows == nt - 1, c0[4 + k], pltpu.roll(si, nt - 1, 0))
        else:
            cr = jnp.where(rows == 0, c0[k], pltpu.roll(sr, 1, 0))
            ci = jnp.where(rows == 0, c0[4 + k], pltpu.roll(si, 1, 0))
        d, p = 1, 0
        while d < nt:
            qr, qi = tab(8 + p, k), sg * tab(8 + p, 4 + k)
            if reverse:
                shr, shi, m = pltpu.roll(cr, nt - d, 0), pltpu.roll(ci, nt - d, 0), rows < nt - d
            else:
                shr, shi, m = pltpu.roll(cr, d, 0), pltpu.roll(ci, d, 0), rows >= d
            cr, ci = cr + jnp.where(m, qr * shr - qi * shi, 0.0), ci + jnp.where(m, qr * shi + qi * shr, 0.0)
            d, p = 2 * d, p + 1
        carries[k], carries[4 + k] = cr, ci
        sr, si = cr, ci
        for r in order:
            sr, si = ar * sr - ai * si + ld(k, r), ar * si + ai * sr + ld(4 + k, r)
            x_ref[k, pl.ds(row0 + r, nt, stride=8), :] = sr
            x_ref[4 + k, pl.ds(row0 + r, nt, stride=8), :] = si
    return carries


def _slabs_to_mat(x_ref, row0, n):
    return jnp.concatenate([x_ref[k, pl.ds(row0, n), :] for k in range(NSLAB)], axis=1)


def _mat_to_slabs(x_ref, row0, n, m):
    for k in range(NSLAB):
        x_ref[k, pl.ds(row0, n), :] = m[:, k * LANES:(k + 1) * LANES]


def _s5_fwd(u, bmat, cmat, drow, arow, tab, cfg, rider=None):
    L, D, Tc = cfg.L, cfg.D, cfg.TC
    J, NC, nt = cfg.G // 8, L // Tc, Tc // 8
    W = NSLAB * LANES

    def body(u_ref, b_ref, c_ref, d_ref, a_ref, t_ref, y_ref, g_ref, cin_ref, x_ref, st_ref):
        c = pl.program_id(1)

        @pl.when(c == 0)
        def _():
            st_ref[...] = jnp.zeros_like(st_ref)

        cin_ref[...] = st_ref[...]
        ub = u_ref[...]
        _mat_to_slabs(x_ref, 0, Tc, jnp.dot(ub.astype(bf16), b_ref[...], preferred_element_type=f32))
        c0 = [st_ref[:, pl.ds(k * LANES, LANES)] for k in range(NSLAB)]
        _chunk_scan(x_ref, 0, nt, a_ref, t_ref, c0, False)
        for k in range(NSLAB):
            st_ref[:, pl.ds(k * LANES, LANES)] = x_ref[k, pl.ds(Tc - 1, 1), :]
        s = _slabs_to_mat(x_ref, 0, Tc).astype(bf16)
        y = jnp.dot(s, c_ref[...], preferred_element_type=f32) + d_ref[...] * ub
        y_ref[...] = y
        g_ref[...] = _gelu(y).astype(bf16)

    outs, extra = _host_call(
        body, name="s5_fwd", grid=(J, NC), rider=rider, args=(u, bmat, cmat, drow, arow, tab),
        in_specs=[pl.BlockSpec((Tc, LANES), lambda j, c: (c, j)),
                  pl.BlockSpec((None, LANES, W), lambda j, c: (j, 0, 0)),
                  pl.BlockSpec((None, W, LANES), lambda j, c: (j, 0, 0)),
                  pl.BlockSpec((1, LANES), lambda j, c: (0, j)),
                  pl.BlockSpec((None, 1, W), lambda j, c: (j, 0, 0)),
                  pl.BlockSpec((None, 16, W), lambda j, c: (j, 0, 0))],
        out_specs=[pl.BlockSpec((Tc, LANES), lambda j, c: (c, j)),
                   pl.BlockSpec((Tc, LANES), lambda j, c: (c, j)),
                   pl.BlockSpec((None, None, 1, W), lambda j, c: (j, c, 0, 0))],
        out_shape=[jax.ShapeDtypeStruct((L, D), f32), jax.ShapeDtypeStruct((L, D), bf16),
                   jax.ShapeDtypeStruct((J, NC, 1, W), f32)],
        scratch_shapes=[pltpu.VMEM((NSLAB, Tc, LANES), f32), pltpu.VMEM((1, W), f32)])
    return (*outs, extra)


def _s5_bwd(u, dy, cin, bmat, cmat, drow, arow, tab, cfg, rider=None):
    L, D, Tc = cfg.L, cfg.D, cfg.TC
    J, NC, nt = cfg.G // 8, L // Tc, Tc // 8
    W = NSLAB * LANES
    PAD = 0

    def body(u_ref, dy_ref, cin_ref, b_ref, c_ref, d_ref, a_ref, t_ref,
             du_ref, db_ref, dc_ref, da_ref, dd_ref, s_ref, g_ref, gst_ref):
        c = pl.program_id(1)

        @pl.when(c == 0)
        def _():
            gst_ref[...] = jnp.zeros_like(gst_ref)
            db_ref[...] = jnp.zeros_like(db_ref)
            dc_ref[...] = jnp.zeros_like(dc_ref)
            da_ref[...] = jnp.zeros_like(da_ref)
            dd_ref[...] = jnp.zeros_like(dd_ref)

        ub, dyb = u_ref[...], dy_ref[...]
        ub16, dy16 = ub.astype(bf16), dyb.astype(bf16)
        _mat_to_slabs(s_ref, PAD, Tc, jnp.dot(ub16, b_ref[...], preferred_element_type=f32))
        c0 = [cin_ref[:, pl.ds(k * LANES, LANES)] for k in range(NSLAB)]
        tile_in = _chunk_scan(s_ref, PAD, nt, a_ref, t_ref, c0, False)
        _mat_to_slabs(g_ref, 0, Tc, lax.dot_general(dy16, c_ref[...], _DIMS["nt"], preferred_element_type=f32))
        g0 = [gst_ref[:, pl.ds(k * LANES, LANES)] for k in range(NSLAB)]
        _chunk_scan(g_ref, 0, nt, a_ref, t_ref, g0, True)
        for k in range(NSLAB):
            gst_ref[:, pl.ds(k * LANES, LANES)] = g_ref[k, pl.ds(0, 1), :]
        for k in range(4):
            acc_r = jnp.zeros((nt, LANES), f32)
            acc_i = jnp.zeros((nt, LANES), f32)
            for r in range(8):
                gr = g_ref[k, pl.ds(r, nt, stride=8), :]
                gi = g_ref[4 + k, pl.ds(r, nt, stride=8), :]
                if r == 0:
                    pr, pi = tile_in[k], tile_in[4 + k]
                else:
                    pr = s_ref[k, pl.ds(PAD + r - 1, nt, stride=8), :]
                    pi = s_ref[4 + k, pl.ds(PAD + r - 1, nt, stride=8), :]
                acc_r += gr * pr + gi * pi
                acc_i += gi * pr - gr * pi
            da_ref[:, pl.ds(k * LANES, LANES)] += _colsum(acc_r)
            da_ref[:, pl.ds((4 + k) * LANES, LANES)] += _colsum(acc_i)
        gm = _slabs_to_mat(g_ref, 0, Tc).astype(bf16)
        sm = _slabs_to_mat(s_ref, PAD, Tc).astype(bf16)
        du = lax.dot_general(gm, b_ref[...], _DIMS["nt"], preferred_element_type=f32) + d_ref[...] * dyb
        du_ref[...] = du.astype(bf16)
        db_ref[...] += lax.dot_general(ub16, gm, _DIMS["tn"], preferred_element_type=f32)
        dc_ref[...] += lax.dot_general(sm, dy16, _DIMS["tn"], preferred_element_type=f32)
        dd_ref[...] += _colsum(dyb * ub)

    rc = lambda j, c: (NC - 1 - c, j)
    outs, extra = _host_call(
        body, name="s5_bwd", grid=(J, NC), rider=rider, args=(u, dy, cin, bmat, cmat, drow, arow, tab),
        in_specs=[pl.BlockSpec((Tc, LANES), rc), pl.BlockSpec((Tc, LANES), rc),
                  pl.BlockSpec((None, None, 1, W), lambda j, c: (j, NC - 1 - c, 0, 0)),
                  pl.BlockSpec((None, LANES, W), lambda j, c: (j, 0, 0)),
                  pl.BlockSpec((None, W, LANES), lambda j, c: (j, 0, 0)),
                  pl.BlockSpec((1, LANES), lambda j, c: (0, j)),
                  pl.BlockSpec((None, 1, W), lambda j, c: (j, 0, 0)),
                  pl.BlockSpec((None, 16, W), lambda j, c: (j, 0, 0))],
        out_specs=[pl.BlockSpec((Tc, LANES), rc),
                   pl.BlockSpec((None, LANES, W), lambda j, c: (j, 0, 0)),
                   pl.BlockSpec((None, W, LANES), lambda j, c: (j, 0, 0)),
                   pl.BlockSpec((None, 1, W), lambda j, c: (j, 0, 0)),
                   pl.BlockSpec((1, LANES), lambda j, c: (0, j))],
        out_shape=[jax.ShapeDtypeStruct((L, D), bf16), jax.ShapeDtypeStruct((J, LANES, W), f32),
                   jax.ShapeDtypeStruct((J, W, LANES), f32), jax.ShapeDtypeStruct((J, 1, W), f32),
                   jax.ShapeDtypeStruct((1, D), f32)],
        scratch_shapes=[pltpu.VMEM((NSLAB, Tc + PAD, LANES), f32), pltpu.VMEM((NSLAB, Tc, LANES), f32),
                        pltpu.VMEM((1, W), f32)])
    return (*outs, extra)


NEG = -1e30


def _tri_tables(nq, by_key):
    pairs = [(qi, ki) for ki in range(nq) for qi in range(ki, nq)] if by_key else \
            [(qi, ki) for qi in range(nq) for ki in range(qi + 1)]
    return jnp.array([p[0] for p in pairs], jnp.int32), jnp.array([p[1] for p in pairs], jnp.int32)


def _tri_call(body, name, cfg, by_key, in_specs, out_specs, out_shape, scratch_shapes, args, rider=None):
    nq = cfg.L // cfg.BQ
    outs, extra = _host_call(body, name=name, grid=(cfg.NH, nq * (nq + 1) // 2), in_specs=in_specs, out_specs=out_specs,
                             out_shape=out_shape, scratch_shapes=scratch_shapes, args=args,
                             prefetch=_tri_tables(nq, by_key), rider=rider)
    return (*outs, extra)


def _ta_fwd(q, kv, fk, cfg, rider=None):
    L, D, NH, DH, B = cfg.L, cfg.D, cfg.NH, cfg.DH, cfg.BQ
    scale = DH ** -0.5

    def body(qt_ref, kt_ref, q_ref, k_ref, v_ref, fk_ref, o_ref, lse_ref, m_ref, acc_ref, a_ref, s_ref, p_ref):
        pid = pl.program_id(1)
        qi, ki = qt_ref[pid], kt_ref[pid]

        @pl.when(ki == 0)
        def _():
            m_ref[...] = jnp.full_like(m_ref, NEG)
            acc_ref[...] = jnp.zeros_like(acc_ref)

        def compute(masked):
            s_ref[...] = lax.dot_general(q_ref[...], k_ref[...], _DIMS["nt"], preferred_element_type=f32)
            fkv = fk_ref[...]

            def strip(rows, row0, c):
                t = s_ref[rows, :] - fkv
                if masked:
                    t = jnp.where(_fa_mask(row0, t.shape), t, NEG)
                m_prev = m_ref[rows, :]
                m_new = jnp.maximum(m_prev, jnp.max(t, axis=1, keepdims=True))
                m_ref[rows, :] = m_new
                a_ref[rows, :] = jnp.exp(m_prev - m_new)
                p_ref[rows, :] = jnp.exp(t - m_new).astype(bf16)
                return c
            _fa_strips(B, strip, 0)
            v1 = jnp.concatenate([v_ref[...], jnp.ones((B, DH), bf16)], axis=1)
            acc_ref[...] = a_ref[...] * acc_ref[...] + jnp.dot(p_ref[...], v1, preferred_element_type=f32)

        @pl.when(ki < qi)
        def _():
            compute(False)

        @pl.when(ki == qi)
        def _():
            compute(True)
            l = acc_ref[:, DH:]
            o_ref[...] = (acc_ref[:, :DH] / l).astype(o_ref.dtype)
            lse_ref[...] = m_ref[...] + jnp.log(l[:, :1])

    col = pltpu.VMEM((B, 1), f32)
    return _tri_call(
        body, "attn_fwd", cfg, False,
        [pl.BlockSpec((B, DH), lambda h, p, qt, kt: (qt[p], h)),
         pl.BlockSpec((B, DH), lambda h, p, qt, kt: (kt[p], h)),
         pl.BlockSpec((B, DH), lambda h, p, qt, kt: (kt[p], NH + h)),
         pl.BlockSpec((None, 1, B), lambda h, p, qt, kt: (h, 0, kt[p]))],
        [pl.BlockSpec((B, DH), lambda h, p, qt, kt: (qt[p], h)),
         pl.BlockSpec((None, B, 1), lambda h, p, qt, kt: (h, qt[p], 0))],
        [jax.ShapeDtypeStruct((L, D), bf16), jax.ShapeDtypeStruct((NH, L, 1), f32)],
        [col, pltpu.VMEM((B, 2 * DH), f32), col, pltpu.VMEM((B, B), f32), pltpu.VMEM((B, B), bf16)],
        (q, kv, kv, fk), rider)


def _ta_bwd_dq(q, kv, do, o, lse, fk, cfg, rider=None):
    L, D, NH, DH, B = cfg.L, cfg.D, cfg.NH, cfg.DH, cfg.BQ
    scale = DH ** -0.5

    def body(qt_ref, kt_ref, q_ref, k_ref, v_ref, do_ref, o_ref, lse_ref, fk_ref, dq_ref, dfq_ref, dl_ref,
             acc_ref, s_ref, dp_ref, ds_ref):
        pid = pl.program_id(1)
        qi, ki = qt_ref[pid], kt_ref[pid]

        @pl.when(ki == 0)
        def _():
            dl_ref[...] = jnp.sum(do_ref[...].astype(f32) * o_ref[...].astype(f32), axis=1, keepdims=True)
            acc_ref[...] = jnp.zeros_like(acc_ref)

        def compute(masked):
            s_ref[...] = lax.dot_general(q_ref[...], k_ref[...], _DIMS["nt"], preferred_element_type=f32)
            dp_ref[...] = lax.dot_general(do_ref[...], v_ref[...], _DIMS["nt"], preferred_element_type=f32)
            fkv = fk_ref[...]

            def strip(rows, row0, c):
                p = jnp.exp(s_ref[rows, :] - fkv - lse_ref[rows, :])
                if masked:
                    p = jnp.where(_fa_mask(row0, p.shape), p, 0.0)
                ds_ref[rows, :] = (p * (dp_ref[rows, :] - dl_ref[rows, :])).astype(bf16)
                return c
            _fa_strips(B, strip, 0)
            k1 = jnp.concatenate([k_ref[...], jnp.ones((B, DH), bf16)], axis=1)
            acc_ref[...] += jnp.dot(ds_ref[...], k1, preferred_element_type=f32)

        @pl.when(ki < qi)
        def _():
            compute(False)

        @pl.when(ki == qi)
        def _():
            compute(True)
            dq_ref[...] = (acc_ref[:, :DH] * scale).astype(dq_ref.dtype)
            dfq_ref[...] = acc_ref[:, DH:DH + 1]

    qmap = lambda h, p, qt, kt: (qt[p], h)
    cmap = lambda h, p, qt, kt: (h, qt[p], 0)
    return _tri_call(
        body, "attn_bwd_dq", cfg, False,
        [pl.BlockSpec((B, DH), qmap),
         pl.BlockSpec((B, DH), lambda h, p, qt, kt: (kt[p], h)),
         pl.BlockSpec((B, DH), lambda h, p, qt, kt: (kt[p], NH + h)),
         pl.BlockSpec((B, DH), qmap), pl.BlockSpec((B, DH), qmap),
         pl.BlockSpec((None, B, 1), cmap),
         pl.BlockSpec((None, 1, B), lambda h, p, qt, kt: (h, 0, kt[p]))],
        [pl.BlockSpec((B, DH), qmap), pl.BlockSpec((None, B, 1), cmap), pl.BlockSpec((None, B, 1), cmap)],
        [jax.ShapeDtypeStruct((L, D), bf16), jax.ShapeDtypeStruct((NH, L, 1), f32), jax.ShapeDtypeStruct((NH, L, 1), f32)],
        [pltpu.VMEM((B, 2 * DH), f32), pltpu.VMEM((B, B), f32), pltpu.VMEM((B, B), f32), pltpu.VMEM((B, B), bf16)],
        (q, kv, kv, do, o, lse, fk), rider)


def _ta_bwd_dkv(q, kv, do, delta, lse, fk, cfg, rider=None):
    L, D, NH, DH, B = cfg.L, cfg.D, cfg.NH, cfg.DH, cfg.BQ
    nq = L // B
    scale = DH ** -0.5

    def body(qt_ref, kt_ref, q_ref, k_ref, v_ref, do_ref, dl_ref, lse_ref, fk_ref, dk_ref, dv_ref, dfk_ref,
             dka_ref, dva_ref, s_ref, dp_ref, p_ref, ds_ref):
        pid = pl.program_id(1)
        qi, ki = qt_ref[pid], kt_ref[pid]

        @pl.when(qi == ki)
        def _():
            dka_ref[...] = jnp.zeros_like(dka_ref)
            dva_ref[...] = jnp.zeros_like(dva_ref)

        def compute(masked):
            s_ref[...] = lax.dot_general(q_ref[...], k_ref[...], _DIMS["nt"], preferred_element_type=f32)
            dp_ref[...] = lax.dot_general(do_ref[...], v_ref[...], _DIMS["nt"], preferred_element_type=f32)
            fkv = fk_ref[...]

            def strip(rows, row0, c):
                p = jnp.exp(s_ref[rows, :] - fkv - lse_ref[rows, :])
                if masked:
                    p = jnp.where(_fa_mask(row0, p.shape), p, 0.0)
                p_ref[rows, :] = p.astype(bf16)
                ds_ref[rows, :] = (p * (dp_ref[rows, :] - dl_ref[rows, :])).astype(bf16)
                return c
            _fa_strips(B, strip, 0)
            q1 = jnp.concatenate([q_ref[...], jnp.ones((B, DH), bf16)], axis=1)
            dva_ref[...] += lax.dot_general(p_ref[...], do_ref[...], _DIMS["tn"], preferred_element_type=f32)
            dka_ref[...] += lax.dot_general(ds_ref[...], q1, _DIMS["tn"], preferred_element_type=f32)

        @pl.when(qi == ki)
        def _():
            compute(True)

        @pl.when(qi > ki)
        def _():
            compute(False)

        @pl.when(qi == nq - 1)
        def _():
            dk_ref[...] = dka_ref[:, :DH].astype(dk_ref.dtype)
            dv_ref[...] = dva_ref[...].astype(dv_ref.dtype)
            dfk_ref[...] = -dka_ref[:, DH:DH + 1]

    qmap = lambda h, p, qt, kt: (qt[p], h)
    cmap = lambda h, p, qt, kt: (h, qt[p], 0)
    kmap = lambda h, p, qt, kt: (kt[p], h)
    return _tri_call(
        body, "attn_bwd_dkv", cfg, True,
        [pl.BlockSpec((B, DH), qmap), pl.BlockSpec((B, DH), kmap),
         pl.BlockSpec((B, DH), lambda h, p, qt, kt: (kt[p], NH + h)),
         pl.BlockSpec((B, DH), qmap), pl.BlockSpec((None, B, 1), cmap), pl.BlockSpec((None, B, 1), cmap),
         pl.BlockSpec((None, 1, B), lambda h, p, qt, kt: (h, 0, kt[p]))],
        [pl.BlockSpec((B, DH), kmap), pl.BlockSpec((B, DH), kmap),
         pl.BlockSpec((None, B, 1), lambda h, p, qt, kt: (h, kt[p], 0))],
        [jax.ShapeDtypeStruct((L, D), bf16), jax.ShapeDtypeStruct((L, D), bf16), jax.ShapeDtypeStruct((NH, L, 1), f32)],
        [pltpu.VMEM((B, 2 * DH), f32), pltpu.VMEM((B, DH), f32), pltpu.VMEM((B, B), f32), pltpu.VMEM((B, B), f32),
         pltpu.VMEM((B, B), bf16), pltpu.VMEM((B, B), bf16)],
        (q, kv, kv, do, delta, lse, fk), rider)


STRIP = 32


def _fa_strips(nrows, fn, init):
    return lax.fori_loop(0, nrows // STRIP, lambda r, c: fn(pl.ds(pl.multiple_of(r * STRIP, STRIP), STRIP), r * STRIP, c),
                         init, unroll=True)


def _fa_mask(row0, shape):
    rows = row0 + lax.broadcasted_iota(jnp.int32, shape, 0)
    cols = lax.broadcasted_iota(jnp.int32, shape, 1)
    return cols <= rows


def _fa_fwd(q, kv, fk, cfg):
    L, D, NH, DH, B = cfg.L, cfg.D, cfg.NH, cfg.DH, cfg.BQ
    nq = L // B
    scale = DH ** -0.5

    def body(q_ref, k_ref, v_ref, fk_ref, o_ref, lse_ref, m_ref, l_ref, acc_ref, a_ref, s_ref, p_ref):
        qi, ki = pl.program_id(1), pl.program_id(2)

        @pl.when(ki == 0)
        def _():
            m_ref[...] = jnp.full_like(m_ref, NEG)
            l_ref[...] = jnp.zeros_like(l_ref)
            acc_ref[...] = jnp.zeros_like(acc_ref)

        def compute(masked):
            s_ref[...] = lax.dot_general(q_ref[...], k_ref[...], _DIMS["nt"], preferred_element_type=f32)
            fkv = fk_ref[...]

            def strip(rows, row0, c):
                t = s_ref[rows, :] - fkv
                if masked:
                    t = jnp.where(_fa_mask(row0, t.shape), t, NEG)
                m_prev = m_ref[rows, :]
                m_new = jnp.maximum(m_prev, jnp.max(t, axis=1, keepdims=True))
                p = jnp.exp(t - m_new)
                alpha = jnp.exp(m_prev - m_new)
                l_ref[rows, :] = alpha * l_ref[rows, :] + jnp.sum(p, axis=1, keepdims=True)
                m_ref[rows, :] = m_new
                a_ref[rows, :] = alpha
                p_ref[rows, :] = p.astype(bf16)
                return c
            _fa_strips(B, strip, 0)
            acc_ref[...] = a_ref[...] * acc_ref[...] + jnp.dot(p_ref[...], v_ref[...], preferred_element_type=f32)

        @pl.when(ki < qi)
        def _():
            compute(False)

        @pl.when(ki == qi)
        def _():
            compute(True)
            o_ref[...] = (acc_ref[...] / l_ref[...]).astype(o_ref.dtype)
            lse_ref[...] = m_ref[...] + jnp.log(l_ref[...])

    col = pltpu.VMEM((B, 1), f32)
    return pl.pallas_call(
        body, name="attn_fwd", grid=(NH, nq, nq),
        in_specs=[pl.BlockSpec((B, DH), lambda h, qi, ki: (qi, h)),
                  pl.BlockSpec((B, DH), lambda h, qi, ki: (jnp.minimum(ki, qi), h)),
                  pl.BlockSpec((B, DH), lambda h, qi, ki: (jnp.minimum(ki, qi), NH + h)),
                  pl.BlockSpec((None, 1, B), lambda h, qi, ki: (h, 0, jnp.minimum(ki, qi)))],
        out_specs=[pl.BlockSpec((B, DH), lambda h, qi, ki: (qi, h)),
                   pl.BlockSpec((None, B, 1), lambda h, qi, ki: (h, qi, 0))],
        out_shape=[jax.ShapeDtypeStruct((L, D), bf16), jax.ShapeDtypeStruct((NH, L, 1), f32)],
        scratch_shapes=[col, col, pltpu.VMEM((B, DH), f32), col, pltpu.VMEM((B, B), f32), pltpu.VMEM((B, B), bf16)],
        compiler_params=_cp(("parallel", "parallel", "arbitrary")),
    )(q, kv, kv, fk)


def _fa_bwd_dq(q, kv, do, o, lse, fk, cfg):
    L, D, NH, DH, B = cfg.L, cfg.D, cfg.NH, cfg.DH, cfg.BQ
    nq = L // B
    scale = DH ** -0.5

    def body(q_ref, k_ref, v_ref, do_ref, o_ref, lse_ref, fk_ref, dq_ref, dfq_ref, dl_ref, acc_ref, df_ref, s_ref, dp_ref, ds_ref):
        qi, ki = pl.program_id(1), pl.program_id(2)

        @pl.when(ki == 0)
        def _():
            dl_ref[...] = jnp.sum(do_ref[...].astype(f32) * o_ref[...].astype(f32), axis=1, keepdims=True)
            acc_ref[...] = jnp.zeros_like(acc_ref)
            df_ref[...] = jnp.zeros_like(df_ref)

        def compute(masked):
            s_ref[...] = lax.dot_general(q_ref[...], k_ref[...], _DIMS["nt"], preferred_element_type=f32)
            dp_ref[...] = lax.dot_general(do_ref[...], v_ref[...], _DIMS["nt"], preferred_element_type=f32)
            fkv = fk_ref[...]

            def strip(rows, row0, c):
                p = jnp.exp(s_ref[rows, :] - fkv - lse_ref[rows, :])
                if masked:
                    p = jnp.where(_fa_mask(row0, p.shape), p, 0.0)
                ds = p * (dp_ref[rows, :] - dl_ref[rows, :])
                df_ref[rows, :] += jnp.sum(ds, axis=1, keepdims=True)
                ds_ref[rows, :] = ds.astype(bf16)
                return c
            _fa_strips(B, strip, 0)
            acc_ref[...] += jnp.dot(ds_ref[...], k_ref[...], preferred_element_type=f32)

        @pl.when(ki < qi)
        def _():
            compute(False)

        @pl.when(ki == qi)
        def _():
            compute(True)
            dq_ref[...] = (acc_ref[...] * scale).astype(dq_ref.dtype)
            dfq_ref[...] = df_ref[...]

    qmap = lambda h, qi, ki: (qi, h)
    cmap = lambda h, qi, ki: (h, qi, 0)
    return pl.pallas_call(
        body, name="attn_bwd_dq", grid=(NH, nq, nq),
        in_specs=[pl.BlockSpec((B, DH), qmap),
                  pl.BlockSpec((B, DH), lambda h, qi, ki: (jnp.minimum(ki, qi), h)),
                  pl.BlockSpec((B, DH), lambda h, qi, ki: (jnp.minimum(ki, qi), NH + h)),
                  pl.BlockSpec((B, DH), qmap), pl.BlockSpec((B, DH), qmap),
                  pl.BlockSpec((None, B, 1), cmap),
                  pl.BlockSpec((None, 1, B), lambda h, qi, ki: (h, 0, jnp.minimum(ki, qi)))],
        out_specs=[pl.BlockSpec((B, DH), qmap), pl.BlockSpec((None, B, 1), cmap), pl.BlockSpec((None, B, 1), cmap)],
        out_shape=[jax.ShapeDtypeStruct((L, D), bf16), jax.ShapeDtypeStruct((NH, L, 1), f32),
                   jax.ShapeDtypeStruct((NH, L, 1), f32)],
        scratch_shapes=[pltpu.VMEM((B, DH), f32), pltpu.VMEM((B, 1), f32), pltpu.VMEM((B, B), f32),
                        pltpu.VMEM((B, B), f32), pltpu.VMEM((B, B), bf16)],
        compiler_params=_cp(("parallel", "parallel", "arbitrary")),
    )(q, kv, kv, do, o, lse, fk)


def _fa_bwd_dkv(q, kv, do, delta, lse, fk, cfg):
    L, D, NH, DH, B = cfg.L, cfg.D, cfg.NH, cfg.DH, cfg.BQ
    nq = L // B
    scale = DH ** -0.5

    def body(q_ref, k_ref, v_ref, do_ref, dl_ref, lse_ref, fk_ref, dk_ref, dv_ref, dfk_ref,
             dka_ref, dva_ref, dfa_ref, s_ref, dp_ref, p_ref, ds_ref):
        ki, qi = pl.program_id(1), pl.program_id(2)

        @pl.when(qi == 0)
        def _():
            dka_ref[...] = jnp.zeros_like(dka_ref)
            dva_ref[...] = jnp.zeros_like(dva_ref)
            dfa_ref[...] = jnp.zeros_like(dfa_ref)

        def compute(masked):
            s_ref[...] = lax.dot_general(q_ref[...], k_ref[...], _DIMS["nt"], preferred_element_type=f32)
            dp_ref[...] = lax.dot_general(do_ref[...], v_ref[...], _DIMS["nt"], preferred_element_type=f32)
            fkv = fk_ref[...]

            def strip(rows, row0, cs):
                p = jnp.exp(s_ref[rows, :] - fkv - lse_ref[rows, :])
                if masked:
                    p = jnp.where(_fa_mask(row0, p.shape), p, 0.0)
                ds = p * (dp_ref[rows, :] - dl_ref[rows, :])
                p_ref[rows, :] = p.astype(bf16)
                ds_ref[rows, :] = ds.astype(bf16)
                return cs + ds
            cs = _fa_strips(B, strip, jnp.zeros((STRIP, B), f32))
            dva_ref[...] += lax.dot_general(p_ref[...], do_ref[...], _DIMS["tn"], preferred_element_type=f32)
            dka_ref[...] += lax.dot_general(ds_ref[...], q_ref[...], _DIMS["tn"], preferred_element_type=f32)
            dfa_ref[...] -= jnp.sum(cs, axis=0, keepdims=True)

        @pl.when(qi == ki)
        def _():
            compute(True)

        @pl.when(qi > ki)
        def _():
            compute(False)

        @pl.when(qi == nq - 1)
        def _():
            dk_ref[...] = (dka_ref[...] * scale).astype(dk_ref.dtype)
            dv_ref[...] = dva_ref[...].astype(dv_ref.dtype)
            dfk_ref[...] = dfa_ref[...]

    qmap = lambda h, ki, qi: (jnp.maximum(qi, ki), h)
    cmap = lambda h, ki, qi: (h, jnp.maximum(qi, ki), 0)
    return pl.pallas_call(
        body, name="attn_bwd_dkv", grid=(NH, nq, nq),
        in_specs=[pl.BlockSpec((B, DH), qmap),
                  pl.BlockSpec((B, DH), lambda h, ki, qi: (ki, h)),
                  pl.BlockSpec((B, DH), lambda h, ki, qi: (ki, NH + h)),
                  pl.BlockSpec((B, DH), qmap),
                  pl.BlockSpec((None, B, 1), cmap), pl.BlockSpec((None, B, 1), cmap),
                  pl.BlockSpec((None, 1, B), lambda h, ki, qi: (h, 0, ki))],
        out_specs=[pl.BlockSpec((B, DH), lambda h, ki, qi: (ki, h)), pl.BlockSpec((B, DH), lambda h, ki, qi: (ki, h)),
                   pl.BlockSpec((None, 1, B), lambda h, ki, qi: (h, 0, ki))],
        out_shape=[jax.ShapeDtypeStruct((L, D), bf16), jax.ShapeDtypeStruct((L, D), bf16),
                   jax.ShapeDtypeStruct((NH, 1, L), f32)],
        scratch_shapes=[pltpu.VMEM((B, DH), f32), pltpu.VMEM((B, DH), f32), pltpu.VMEM((1, B), f32),
                        pltpu.VMEM((B, B), f32), pltpu.VMEM((B, B), f32), pltpu.VMEM((B, B), bf16), pltpu.VMEM((B, B), bf16)],
        compiler_params=_cp(("parallel", "parallel", "arbitrary")),
    )(q, kv, kv, do, delta, lse, fk)


def _fox_logits(q, k, fqv, fkv, scale, masked):
    s = lax.dot_general(q, k, _DIMS["nt"], preferred_element_type=f32) * scale + fqv - fkv
    if masked:
        rows = lax.broadcasted_iota(jnp.int32, s.shape, 0)
        cols = lax.broadcasted_iota(jnp.int32, s.shape, 1)
        return s, cols <= rows
    return s, None


def _fox_fwd(q, kv, fq, fk, cfg):
    L, D, NH, DH, B = cfg.L, cfg.D, cfg.NH, cfg.DH, cfg.BQ
    nq = L // B
    scale = DH ** -0.5

    def body(q_ref, k_ref, v_ref, fq_ref, fk_ref, o_ref, lse_ref):
        qi = pl.program_id(1)
        qv, fqv = q_ref[...], fq_ref[...]

        def chunk(kj, carry, masked):
            m, l, acc = carry
            rows = pl.ds(pl.multiple_of(kj * B, B), B)
            s, mask = _fox_logits(qv, k_ref[rows, :], fqv, fk_ref[kj], scale, masked)
            if masked:
                s = jnp.where(mask, s, NEG)
            m_new = jnp.maximum(m, jnp.max(s, axis=1, keepdims=True))
            alpha = jnp.exp(m - m_new)
            p = jnp.exp(s - m_new)
            l = alpha * l + jnp.sum(p, axis=1, keepdims=True)
            acc = alpha * acc + jnp.dot(p.astype(bf16), v_ref[rows, :], preferred_element_type=f32)
            return m_new, l, acc

        init = (jnp.full((B, 1), NEG, f32), jnp.zeros((B, 1), f32), jnp.zeros((B, DH), f32))
        carry = lax.fori_loop(0, qi, lambda kj, c: chunk(kj, c, False), init)
        m, l, acc = chunk(qi, carry, True)
        o_ref[...] = (acc / l).astype(o_ref.dtype)
        lse_ref[...] = m + jnp.log(l)

    return pl.pallas_call(
        body, name="attn_fwd", grid=(NH, nq),
        in_specs=[pl.BlockSpec((B, DH), lambda h, qi: (qi, h)),
                  pl.BlockSpec((L, DH), lambda h, qi: (0, h)), pl.BlockSpec((L, DH), lambda h, qi: (0, NH + h)),
                  pl.BlockSpec((None, B, 1), lambda h, qi: (h, qi, 0)),
                  pl.BlockSpec((None, nq, 1, B), lambda h, qi: (h, 0, 0, 0))],
        out_specs=[pl.BlockSpec((B, DH), lambda h, qi: (qi, h)), pl.BlockSpec((None, B, 1), lambda h, qi: (h, qi, 0))],
        out_shape=[jax.ShapeDtypeStruct((L, D), bf16), jax.ShapeDtypeStruct((NH, L, 1), f32)],
        compiler_params=_cp(("parallel", "arbitrary")),
    )(q, kv, kv, fq, fk)


def _fox_bwd_dq(q, kv, do, o, lse, fq, fk, cfg):
    L, D, NH, DH, B = cfg.L, cfg.D, cfg.NH, cfg.DH, cfg.BQ
    nq = L // B
    scale = DH ** -0.5

    def body(q_ref, k_ref, v_ref, do_ref, o_ref, lse_ref, fq_ref, fk_ref, dq_ref, dfq_ref, dl_ref):
        qi = pl.program_id(1)
        qv, fqv, dov, lsev = q_ref[...], fq_ref[...], do_ref[...], lse_ref[...]
        delta = jnp.sum(dov.astype(f32) * o_ref[...].astype(f32), axis=1, keepdims=True)

        def chunk(kj, carry, masked):
            acc, df = carry
            rows = pl.ds(pl.multiple_of(kj * B, B), B)
            kv_ = k_ref[rows, :]
            s, mask = _fox_logits(qv, kv_, fqv, fk_ref[kj], scale, masked)
            p = jnp.exp(s - lsev)
            if masked:
                p = jnp.where(mask, p, 0.0)
            dp = lax.dot_general(dov, v_ref[rows, :], _DIMS["nt"], preferred_element_type=f32)
            ds = p * (dp - delta)
            return acc + jnp.dot(ds.astype(bf16), kv_, preferred_element_type=f32), df + jnp.sum(ds, axis=1, keepdims=True)

        carry = lax.fori_loop(0, qi, lambda kj, c: chunk(kj, c, False), (jnp.zeros((B, DH), f32), jnp.zeros((B, 1), f32)))
        acc, df = chunk(qi, carry, True)
        dq_ref[...] = (acc * scale).astype(dq_ref.dtype)
        dfq_ref[...] = df
        dl_ref[...] = delta

    qmap = lambda h, qi: (qi, h)
    cmap = lambda h, qi: (h, qi, 0)
    return pl.pallas_call(
        body, name="attn_bwd_dq", grid=(NH, nq),
        in_specs=[pl.BlockSpec((B, DH), qmap),
                  pl.BlockSpec((L, DH), lambda h, qi: (0, h)), pl.BlockSpec((L, DH), lambda h, qi: (0, NH + h)),
                  pl.BlockSpec((B, DH), qmap), pl.BlockSpec((B, DH), qmap),
                  pl.BlockSpec((None, B, 1), cmap), pl.BlockSpec((None, B, 1), cmap),
                  pl.BlockSpec((None, nq, 1, B), lambda h, qi: (h, 0, 0, 0))],
        out_specs=[pl.BlockSpec((B, DH), qmap), pl.BlockSpec((None, B, 1), cmap), pl.BlockSpec((None, B, 1), cmap)],
        out_shape=[jax.ShapeDtypeStruct((L, D), bf16), jax.ShapeDtypeStruct((NH, L, 1), f32),
                   jax.ShapeDtypeStruct((NH, L, 1), f32)],
        compiler_params=_cp(("parallel", "arbitrary")),
    )(q, kv, kv, do, o, lse, fq, fk)


def _fox_bwd_dkv(q, kv, do, delta, lse, fq, fk, cfg):
    L, D, NH, DH, B = cfg.L, cfg.D, cfg.NH, cfg.DH, cfg.BQ
    nq = L // B
    scale = DH ** -0.5

    def body(q_ref, k_ref, v_ref, do_ref, dl_ref, lse_ref, fq_ref, fk_ref, dk_ref, dv_ref, dfk_ref):
        ki = pl.program_id(1)
        kv_, vv, fkv = k_ref[...], v_ref[...], fk_ref[...]

        def block(qj, carry, masked):
            dk, dv, df = carry
            rows = pl.ds(pl.multiple_of(qj * B, B), B)
            qv, dov = q_ref[rows, :], do_ref[rows, :]
            s, mask = _fox_logits(qv, kv_, fq_ref[rows, :], fkv, scale, masked)
            p = jnp.exp(s - lse_ref[rows, :])
            if masked:
                p = jnp.where(mask, p, 0.0)
            dv = dv + lax.dot_general(p.astype(bf16), dov, _DIMS["tn"], preferred_element_type=f32)
            dp = lax.dot_general(dov, vv, _DIMS["nt"], preferred_element_type=f32)
            ds = p * (dp - dl_ref[rows, :])
            dk = dk + lax.dot_general(ds.astype(bf16), qv, _DIMS["tn"], preferred_element_type=f32)
            return dk, dv, df - jnp.sum(ds, axis=0, keepdims=True)

        init = (jnp.zeros((B, DH), f32), jnp.zeros((B, DH), f32), jnp.zeros((1, B), f32))
        carry = block(ki, init, True)
        dk, dv, df = lax.fori_loop(ki + 1, nq, lambda qj, c: block(qj, c, False), carry)
        dk_ref[...] = (dk * scale).astype(dk_ref.dtype)
        dv_ref[...] = dv.astype(dv_ref.dtype)
        dfk_ref[...] = df

    whole = lambda h, ki: (0, h)
    col = lambda h, ki: (h, 0, 0)
    return pl.pallas_call(
        body, name="attn_bwd_dkv", grid=(NH, nq),
        in_specs=[pl.BlockSpec((L, DH), whole),
                  pl.BlockSpec((B, DH), lambda h, ki: (ki, h)), pl.BlockSpec((B, DH), lambda h, ki: (ki, NH + h)),
                  pl.BlockSpec((L, DH), whole),
                  pl.BlockSpec((None, L, 1), col), pl.BlockSpec((None, L, 1), col), pl.BlockSpec((None, L, 1), col),
                  pl.BlockSpec((None, None, 1, B), lambda h, ki: (h, ki, 0, 0))],
        out_specs=[pl.BlockSpec((B, DH), lambda h, ki: (ki, h)), pl.BlockSpec((B, DH), lambda h, ki: (ki, h)),
                   pl.BlockSpec((None, None, 1, B), lambda h, ki: (h, ki, 0, 0))],
        out_shape=[jax.ShapeDtypeStruct((L, D), bf16), jax.ShapeDtypeStruct((L, D), bf16),
                   jax.ShapeDtypeStruct((NH, nq, 1, B), f32)],
        compiler_params=_cp(("parallel", "arbitrary")),
    )(q, kv, kv, do, delta, lse, fq, fk)


FCH = 256


def _split3(x):
    hi = x.astype(bf16)
    r1 = x - hi.astype(f32)
    mid = r1.astype(bf16)
    lo = (r1 - mid.astype(f32)).astype(bf16)
    return hi, mid, lo


def _tri_sum(tri, x):
    hi, mid, lo = _split3(x)
    return (jnp.dot(tri, hi, preferred_element_type=f32) + jnp.dot(tri, mid, preferred_element_type=f32)
            + jnp.dot(tri, lo, preferred_element_type=f32))


def _fgate_fwd(z, fb, cfg):
    L = cfg.L

    def body(z_ref, fb_ref, f_ref):
        r = lax.broadcasted_iota(jnp.int32, (FCH, FCH), 0)
        c = lax.broadcasted_iota(jnp.int32, (FCH, FCH), 1)
        tri = (c <= r).astype(bf16)
        carry = jnp.zeros((1, LANES), f32)
        for ch in range(L // FCH):
            x = z_ref[pl.ds(ch * FCH, FCH), :] + fb_ref[...]
            lf = jnp.minimum(x, 0.0) - jnp.log(1.0 + jnp.exp(-jnp.abs(x)))
            f_ref[pl.ds(ch * FCH, FCH), :] = _tri_sum(tri, lf) + carry
            carry = f_ref[pl.ds(ch * FCH + FCH - 1, 1), :]

    vm = pl.BlockSpec(memory_space=pltpu.VMEM)
    return pl.pallas_call(body, name="fgate_fwd", in_specs=[vm, vm], out_specs=vm,
                          out_shape=jax.ShapeDtypeStruct((L, LANES), f32), compiler_params=_cp())(z, fb)


def _fgate_bwd(df, z, fb, cfg):
    L = cfg.L

    def body(df_ref, z_ref, fb_ref, dz_ref, db_ref):
        r = lax.broadcasted_iota(jnp.int32, (FCH, FCH), 0)
        c = lax.broadcasted_iota(jnp.int32, (FCH, FCH), 1)
        tri = (c >= r).astype(bf16)
        carry = jnp.zeros((1, LANES), f32)
        dbs = jnp.zeros((1, LANES), f32)
        for ch in range(L // FCH - 1, -1, -1):
            suf = _tri_sum(tri, df_ref[pl.ds(ch * FCH, FCH), :]) + carry
            x = z_ref[pl.ds(ch * FCH, FCH), :] + fb_ref[...]
            dz = suf * _sigmoid(-x)
            dz_ref[pl.ds(ch * FCH, FCH), :] = dz
            dbs = dbs + _colsum(dz)
            carry = carry + _colsum(df_ref[pl.ds(ch * FCH, FCH), :])
        db_ref[...] = dbs

    vm = pl.BlockSpec(memory_space=pltpu.VMEM)
    return pl.pallas_call(body, name="fgate_bwd", in_specs=[vm, vm, vm], out_specs=[vm, vm],
                          out_shape=[jax.ShapeDtypeStruct((L, LANES), f32), jax.ShapeDtypeStruct((1, LANES), f32)],
                          compiler_params=_cp())(df, z, fb)


def _adamw(w, g, m, v, name):
    R, C = w.shape
    c1 = 1.0 - ADAM_B1 ** ADAM_STEP
    c2 = 1.0 - ADAM_B2 ** ADAM_STEP

    def fn(i, ni, wv, gv, mv, vv):
        mn = ADAM_B1 * mv + (1.0 - ADAM_B1) * gv
        vn = ADAM_B2 * vv + (1.0 - ADAM_B2) * (gv * gv)
        delta = -ADAM_LR * ((mn / c1) / (jnp.sqrt(vn / c2) + ADAM_EPS) + ADAM_WD * wv)
        return [delta, mn, vn], []
    tc = C if C % LANES else _tile(C, 1024)
    return _rowwise(fn, [(w, "rc"), (g, "rc"), (m, "rc"), (v, "rc")], [f32, f32, f32], [], L=R, C=C, tl=512, tc=tc, name=name)


def _sum_lead(x, out_dtype, name):
    n, R, C = x.shape
    tl = _tile(R, 512, HALO)
    tc = C if C % LANES else _tile(C, 1024)

    def body(x_ref, o_ref):
        acc = x_ref[0].astype(f32)
        for k in range(1, n):
            acc = acc + x_ref[k].astype(f32)
        o_ref[...] = acc.astype(o_ref.dtype)

    return pl.pallas_call(
        body, name=name, grid=(R // tl, C // tc),
        in_specs=[pl.BlockSpec((n, tl, tc), lambda i, j: (0, i, j))], out_specs=pl.BlockSpec((tl, tc), lambda i, j: (i, j)),
        out_shape=jax.ShapeDtypeStruct((R, C), out_dtype), compiler_params=_cp(("parallel", "parallel")),
    )(x)


def _add_own_half(g, got, core, name):
    _, _, R, C = g.shape
    tl = _tile(R, 512, HALO)
    tc = C if C % LANES else _tile(C, 1024)

    def body(core_ref, g_ref, got_ref, o_ref):
        o_ref[...] = (g_ref[...].astype(f32) + got_ref[...].astype(f32)).astype(o_ref.dtype)

    blk = pl.BlockSpec((None, tl, tc), lambda k, i, j, co: (k, i, j))
    return pl.pallas_call(
        body, name=name,
        grid_spec=pltpu.PrefetchScalarGridSpec(
            num_scalar_prefetch=1, grid=(4, R // tl, C // tc),
            in_specs=[pl.BlockSpec((None, None, tl, tc), lambda k, i, j, co: (k, co[0], i, j)), blk], out_specs=blk),
        out_shape=jax.ShapeDtypeStruct((4, R, C), bf16), compiler_params=_cp(("parallel", "parallel", "parallel")),
    )(core, g, got)


ANY = pl.BlockSpec(memory_space=pl.ANY)
LOCAL_CHUNKS = 4


def _place():
    x, y, c = lax.axis_index("x"), lax.axis_index("y"), lax.axis_index("c")
    return x, y, c


def _allgather8(blocks, name):
    return _run_rider(_gather_rider(blocks), name)


def _gather_rider(blocks, middle_at=(1, 2)):
    n = len(blocks)

    def steps(ins, outs, sems):
        send_sems, recv_sems, local_sems = sems
        x, y, c = _place()
        me, sibling = (x, y, c), (x, y, 1 - c)
        chips = [(1 - x, y), (x, 1 - y), (1 - x, 1 - y)]

        def slot(a, dev):
            return outs[a].at[4 * dev[0] + 2 * dev[1] + dev[2]]

        def copy(a, k, block, to, src=None):
            return pltpu.make_async_remote_copy(
                src_ref=slot(a, block) if src is None else src, dst_ref=slot(a, block),
                send_sem=send_sems.at[a * 7 + k], recv_sem=recv_sems.at[a * 7 + k], device_id=to, device_id_type=MESH)

        def mine():
            out = []
            for a in range(n):
                rows = blocks[a].shape[0]
                k = LOCAL_CHUNKS if rows % (LOCAL_CHUNKS * HALO) == 0 else 1
                for i in range(k):
                    piece = pl.ds(i * (rows // k), rows // k)
                    out.append(pltpu.make_async_copy(ins[a].at[piece], slot(a, me).at[piece], local_sems.at[a * LOCAL_CHUNKS + i]))
            return out

        def first():
            out = []
            for a in range(n):
                out.append(copy(a, 0, me, sibling, src=ins[a]))
                out += [copy(a, 1 + j, me, (*chip, c), src=ins[a]) for j, chip in enumerate(chips)]
            return out

        def passed():
            return [copy(a, 4 + j, (*chip, c), sibling) for j, chip in enumerate(chips) for a in range(n)]

        def start():
            for cp in mine() + first():
                cp.start()

        def middle():
            for j, chip in enumerate(chips):
                for a in range(n):
                    copy(a, 1 + j, (*chip, c), me).wait_recv()
                    copy(a, 4 + j, (*chip, c), sibling).start()

        def finish():
            for a in range(n):
                copy(a, 0, sibling, me).wait_recv()
            for j, chip in enumerate(chips):
                for a in range(n):
                    copy(a, 4 + j, (*chip, 1 - c), me).wait_recv()
            for cp in first() + passed():
                cp.wait_send()
            for cp in mine():
                cp.wait()
        return start, middle, finish

    return dict(ins=list(blocks), out_shapes=[jax.ShapeDtypeStruct((N_DEV,) + b.shape, b.dtype) for b in blocks],
                sems=[pltpu.SemaphoreType.DMA((7 * n,)), pltpu.SemaphoreType.DMA((7 * n,)),
                      pltpu.SemaphoreType.DMA((LOCAL_CHUNKS * n,))],
                steps=steps, middle_at=middle_at)


def _run_rider(rider, name):
    ni, no = len(rider["ins"]), len(rider["out_shapes"])

    def body(*refs):
        start, middle, finish = rider["steps"](refs[:ni], refs[ni:ni + no], refs[ni + no:])
        start()
        if middle is not None:
            middle()
        finish()

    outs = pl.pallas_call(body, name=name, in_specs=[ANY] * ni, out_specs=[ANY] * no, out_shape=rider["out_shapes"],
                          scratch_shapes=rider["sems"])(*rider["ins"])
    return list(outs)


def _host_call(body, *, name, grid, in_specs, out_specs, out_shape, scratch_shapes, args, prefetch=(), rider=None):
    npre, nin, nout, nscr = len(prefetch), len(in_specs), len(out_specs), len(scratch_shapes)
    r_in, r_out, r_scr = (rider["ins"], rider["out_shapes"], rider["sems"]) if rider else ([], [], [])
    nri, nro = len(r_in), len(r_out)

    def kern(*refs):
        pre, rest = refs[:npre], refs[npre:]
        cin, rin = rest[:nin], rest[nin:nin + nri]
        o0 = nin + nri
        cout, rout = rest[o0:o0 + nout], rest[o0 + nout:o0 + nout + nro]
        s0 = o0 + nout + nro
        cscr, rscr = rest[s0:s0 + nscr], rest[s0 + nscr:]
        if rider:
            ids = [pl.program_id(d) for d in range(len(grid))]
            rest_zero = functools.reduce(jnp.logical_and, [i == 0 for i in ids[1:]], True)
            start, middle, finish = rider["steps"](rin, rout, rscr)
            pl.when(jnp.logical_and(ids[0] == 0, rest_zero))(start)
            if middle is not None:
                num, den = rider.get("middle_at", (1, 2))
                pl.when(jnp.logical_and(ids[0] == grid[0] * num // den, rest_zero))(middle)
        body(*pre, *cin, *cout, *cscr)
        if rider:
            pl.when(functools.reduce(jnp.logical_and, [i == g - 1 for i, g in zip(ids, grid)]))(finish)

    res = pl.pallas_call(
        kern, name=name,
        grid_spec=pltpu.PrefetchScalarGridSpec(num_scalar_prefetch=npre, grid=grid, in_specs=list(in_specs) + [ANY] * nri,
                                               out_specs=list(out_specs) + [ANY] * nro,
                                               scratch_shapes=list(scratch_shapes) + list(r_scr)),
        out_shape=list(out_shape) + list(r_out),
        compiler_params=_cp(("arbitrary",) * len(grid) if rider else ("parallel",) + ("arbitrary",) * (len(grid) - 1)),
    )(*prefetch, *args, *r_in)
    return list(res[:nout]), list(res[nout:])


def _sibling_send(halves, name):
    n = len(halves)

    def body(*refs):
        ins, outs = refs[:n], refs[n:2 * n]
        send_sems, recv_sems = refs[2 * n:]
        x, y, c = _place()
        sends = [pltpu.make_async_remote_copy(src_ref=ins[a], dst_ref=outs[a], send_sem=send_sems.at[a],
                                              recv_sem=recv_sems.at[a], device_id=(x, y, 1 - c), device_id_type=MESH)
                 for a in range(n)]
        for cp in sends:
            cp.start()
        for cp in sends:
            cp.wait_recv()
        for cp in sends:
            cp.wait_send()

    outs = pl.pallas_call(
        body, name=name, in_specs=[ANY] * n, out_specs=[ANY] * n,
        out_shape=[jax.ShapeDtypeStruct(h.shape, h.dtype) for h in halves],
        scratch_shapes=[pltpu.SemaphoreType.DMA((n,)), pltpu.SemaphoreType.DMA((n,))],
    )(*halves)
    return list(outs)


def _sibling_swap_halves(grads, name):
    n = len(grads)

    def body(*refs):
        ins, outs = refs[:n], refs[n:2 * n]
        send_sems, recv_sems = refs[2 * n:]
        x, y, c = _place()
        sends = [pltpu.make_async_remote_copy(src_ref=ins[a].at[:, 1 - c], dst_ref=outs[a], send_sem=send_sems.at[a],
                                              recv_sem=recv_sems.at[a], device_id=(x, y, 1 - c), device_id_type=MESH)
                 for a in range(n)]
        for cp in sends:
            cp.start()
        for cp in sends:
            cp.wait_recv()
        for cp in sends:
            cp.wait_send()

    outs = pl.pallas_call(
        body, name=name, in_specs=[ANY] * n, out_specs=[ANY] * n,
        out_shape=[jax.ShapeDtypeStruct((4,) + g.shape[2:], g.dtype) for g in grads],
        scratch_shapes=[pltpu.SemaphoreType.DMA((n,)), pltpu.SemaphoreType.DMA((n,))],
    )(*grads)
    return list(outs)


def _chip_scatter(parts, name):
    return _run_rider(_scatter_rider(parts), name)


def _scatter_rider(parts):
    n = len(parts)

    def steps(ins, outs, sems):
        send_sems, recv_sems = sems
        x, y, c = _place()
        chips = [(1 - x, y), (x, 1 - y), (1 - x, 1 - y)]

        def sends():
            return [pltpu.make_async_remote_copy(
                src_ref=ins[a].at[2 * px + py], dst_ref=outs[a].at[j], send_sem=send_sems.at[a * 3 + j],
                recv_sem=recv_sems.at[a * 3 + j], device_id=(px, py, c), device_id_type=MESH)
                for a in range(n) for j, (px, py) in enumerate(chips)]

        def start():
            for cp in sends():
                cp.start()

        def finish():
            for cp in sends():
                cp.wait_recv()
            for cp in sends():
                cp.wait_send()
        return start, None, finish

    return dict(ins=list(parts), out_shapes=[jax.ShapeDtypeStruct((3,) + p.shape[1:], p.dtype) for p in parts],
                sems=[pltpu.SemaphoreType.DMA((3 * n,)), pltpu.SemaphoreType.DMA((3 * n,))], steps=steps)


def _sum_parts(own, got, chip, name):
    _, R, C = own.shape
    tl = _tile(R, 512, HALO)
    tc = C if C % LANES else _tile(C, 1024)

    def body(chip_ref, own_ref, got_ref, o_ref):
        acc = own_ref[...].astype(f32)
        for k in range(3):
            acc = acc + got_ref[k].astype(f32)
        o_ref[...] = acc

    return pl.pallas_call(
        body, name=name,
        grid_spec=pltpu.PrefetchScalarGridSpec(
            num_scalar_prefetch=1, grid=(R // tl, C // tc),
            in_specs=[pl.BlockSpec((None, tl, tc), lambda i, j, ch: (ch[0], i, j)),
                      pl.BlockSpec((3, tl, tc), lambda i, j, ch: (0, i, j))],
            out_specs=pl.BlockSpec((tl, tc), lambda i, j, ch: (i, j))),
        out_shape=jax.ShapeDtypeStruct((R, C), f32), compiler_params=_cp(("parallel", "parallel")),
    )(chip, own, got)


def _adamw_halves(w, m, v, g_mine, g_other, core, name, rider=None):
    NL, R, C = w.shape
    r = R // 2
    tl = _tile(r, 512, HALO)
    tc = C if C % LANES else _tile(C, 1024)
    nh = r // tl
    c1 = 1.0 - ADAM_B1 ** ADAM_STEP
    c2 = 1.0 - ADAM_B2 ** ADAM_STEP

    def body(core_ref, w_ref, m_ref, v_ref, gm_ref, go_ref, g_out, d_out, m_out, v_out):
        i = pl.program_id(1)
        mine = lax.div(i, nh) == core_ref[0]
        gv = jnp.where(mine, gm_ref[...], go_ref[...])
        mn = ADAM_B1 * m_ref[...] + (1.0 - ADAM_B1) * gv
        vn = ADAM_B2 * v_ref[...] + (1.0 - ADAM_B2) * (gv * gv)
        g_out[...] = gv
        d_out[...] = -ADAM_LR * ((mn / c1) / (jnp.sqrt(vn / c2) + ADAM_EPS) + ADAM_WD * w_ref[...])
        m_out[...] = mn
        v_out[...] = vn

    full = pl.BlockSpec((None, tl, tc), lambda l, i, j, co: (l, i, j))
    mine_spec = pl.BlockSpec((None, tl, tc), lambda l, i, j, co: (l, jnp.clip(i - co[0] * nh, 0, nh - 1), j))
    other_spec = pl.BlockSpec((None, tl, tc), lambda l, i, j, co: (l, jnp.clip(i - (1 - co[0]) * nh, 0, nh - 1), j))
    outs, extra = _host_call(
        body, name=name, grid=(NL, R // tl, C // tc), in_specs=[full, full, full, mine_spec, other_spec],
        out_specs=[full] * 4, out_shape=[jax.ShapeDtypeStruct((NL, R, C), f32)] * 4, scratch_shapes=[],
        args=(w, m, v, g_mine, g_other), prefetch=(core,), rider=rider)
    return (*outs, extra)


def _s5_discretize(log_step, a_re, a_im, b_re, b_im):
    step = jnp.exp(log_step)[:, None]
    mag = jnp.exp(a_re * step)
    abar_re = mag * jnp.cos(a_im * step)
    abar_im = mag * jnp.sin(a_im * step)
    den = a_re * a_re + a_im * a_im
    nr = abar_re - 1.0
    fr = (nr * a_re + abar_im * a_im) / den
    fi = (abar_im * a_re - nr * a_im) / den
    bbar_re = fr[..., None] * b_re - fi[..., None] * b_im
    bbar_im = fr[..., None] * b_im + fi[..., None] * b_re
    return abar_re, abar_im, bbar_re, bbar_im


def _s5_prepare(p, cfg):
    abar_re, abar_im, bbar_re, bbar_im = _s5_discretize(p["log_step"], p["a_re"], p["a_im"], p["b_re"], p["b_im"])
    step = jnp.exp(p["log_step"])[:, None]
    arow, tab = _s5_tables(abar_re, abar_im, p["a_re"], p["a_im"], step, cfg)
    bmat, cmat = _s5_mats(bbar_re, bbar_im, p["c_re"], p["c_im"], cfg)
    return dict(arow=arow, tab=tab, bmat=bmat, cmat=cmat, drow=p["d"].reshape(1, cfg.D))


def _s5_param_grads(p, dbmat, dcmat, dabar, dd, cfg):
    J, P = cfg.G // 8, cfg.P
    dbb_re, dbb_im, dc_re, dc_im = _s5_unmats(dbmat, dcmat, cfg)
    da = dabar.reshape(J, 2, 8, P)
    da_re, da_im = da[:, 0].reshape(cfg.G, P), da[:, 1].reshape(cfg.G, P)
    _, vjp = jax.vjp(_s5_discretize, p["log_step"], p["a_re"], p["a_im"], p["b_re"], p["b_im"])
    dls, dare, daim, dbre, dbim = vjp((da_re, da_im, dbb_re, dbb_im))
    return dict(log_step=dls, a_re=dare, a_im=daim, b_re=dbre, b_im=dbim, c_re=dc_re, c_im=dc_im, d=dd.reshape(cfg.G, cfg.H))


def _resid_epi(acc, xv, gv):
    return xv + gv * acc, acc


def _ffn_fwd(x_in, g_norm, sc, sh, gate, W, exch, conv_w, conv_b, cfg, tag):
    h = _norm_mod_fwd(x_in, g_norm, sc, sh, cfg, f"ffn_norm_{tag}")
    rider = exch.rider(f"ffn_up_{tag}")
    a = _mm(h, W[f"ffn_w_up{tag}"], mode="nn", b4=True, tn=1408, out_dtypes=(bf16,), name=f"ffn_up_{tag}", rider=rider)
    if rider:
        a, extra = a
        W.update(exch.done(f"ffn_up_{tag}", extra))
    act = _conv_act_fwd(a, conv_w, conv_b, cfg)
    rider = exch.rider(f"ffn_down_{tag}")
    res = _mm(act, W[f"ffn_w_down{tag}"], mode="nn", extras=[(x_in, "mn"), (gate, "n")], epi=_resid_epi,
              out_dtypes=(f32, bf16), name=f"ffn_down_{tag}", rider=rider)
    if rider:
        res, extra = res
        W.update(exch.done(f"ffn_down_{tag}", extra))
    x_out, out = res
    return x_out, dict(h=h, a=a, act=act, out=out)


def _ffn_bwd(dx, x_in, sv, g_norm, sc, gate, w_up4, w_down, conv_w, conv_b, cfg, tag):
    F = cfg.F
    dout, dgate = _gate_bwd(dx, sv["out"], gate, cfg, f"ffn_gate_bwd_{tag}")
    dact = _mm(dout, w_down, mode="nt", tn=1408, out_dtypes=(bf16,), name=f"ffn_dact_{tag}")
    dw_down = _mm(sv["act"], dout, mode="tn", tm=1408, out_dtypes=(bf16,), name=f"ffn_dwdown_{tag}")
    dcu, dcv, dwu, dwv, dbu, dbv = _conv_act_bwd1(dact, sv["a"], conv_w, conv_b, cfg)
    dau = _conv_bwd2(dcu, conv_w[:, :F], cfg, f"conv_bwd2u_{tag}")
    dav = _conv_bwd2(dcv, conv_w[:, F:], cfg, f"conv_bwd2v_{tag}")
    da = jnp.concatenate([dau, dav], axis=1)
    dh = _mm(da, w_up4, mode="nt", b4=True, tk=1408, out_dtypes=(bf16,), name=f"ffn_dh_{tag}")
    dw_up = _mm(sv["h"], da, mode="tn", out4=True, tn=1408, out_dtypes=(bf16,), name=f"ffn_dwup_{tag}")
    dx_in, A, B = _norm_mod_bwd(dh, x_in, g_norm, sc, dx, cfg, f"ffn_norm_bwd_{tag}")
    small = dict(norm_g=(1.0 + sc) * A, sc=g_norm * A, sh=B, gate=dgate,
                 conv_w=jnp.concatenate([dwu, dwv], axis=1), conv_b=jnp.concatenate([dbu, dbv], axis=1))
    return dx_in, dw_up, dw_down, small


class _NoExchange:
    def rider(self, key, grads=None):
        return None

    def done(self, key, extra):
        return {}


def _local_step(cfg, x, tgt, mod, W, sp, exch=None):
    D, NH = cfg.D, cfg.NH
    exch = exch or _NoExchange()
    W, big = dict(W), {}

    def hand_over(key, grads):
        rider = exch.rider(key, grads)
        if rider is None:
            big.update(grads)
        return rider
    row = lambda v: v.reshape(1, -1)
    nmg0, nmg1 = row(sp["norm_mix_g"][0]), row(sp["norm_mix_g"][1])
    nfg0, nfg1 = row(sp["norm_ffn_g"][0]), row(sp["norm_ffn_g"][1])
    kvg, fng = row(sp["kv_norm_g"]), row(sp["final_norm_g"])
    cw0, cw1 = sp["ffn_conv_w"][0], sp["ffn_conv_w"][1]
    cb0, cb1 = row(sp["ffn_conv_b"][0]), row(sp["ffn_conv_b"][1])
    glu_b = row(sp["ssm_glu_b"])
    fb = jnp.zeros((1, LANES), f32).at[0, :NH].set(sp["forget_b"])
    s5p = {k: sp["ssm_" + k][0] for k in ("log_step", "a_re", "a_im", "b_re", "b_im", "c_re", "c_im", "d")}
    s5 = _s5_prepare(s5p, cfg)
    m0, m1 = mod["l0"], mod["l1"]

    h0 = _norm_mod_fwd(x, nmg0, m0["sc_m"], m0["sh_m"], cfg, "mix_norm_0")
    u = _mm(h0, W["ssm_w_in"], mode="nn", name="ssm_in")
    y, gact, cin, extra = _s5_fwd(u, s5["bmat"], s5["cmat"], s5["drow"], s5["arow"], s5["tab"], cfg, exch.rider("s5_fwd"))
    W.update(exch.done("s5_fwd", extra))

    def glu_epi(acc, bv, gv):
        pre = acc + bv
        return pre, gv.astype(f32) * _sigmoid(pre)
    pre, z = _mm(gact, W["ssm_glu_w"], mode="nn", extras=[(glu_b, "n"), (gact, "mn")], epi=glu_epi,
                 out_dtypes=(f32, bf16), name="ssm_glu")
    x1, out_m0 = _mm(z, W["ssm_w_out"], mode="nn", extras=[(x, "mn"), (m0["g_m"], "n")], epi=_resid_epi,
                     out_dtypes=(f32, bf16), name="ssm_out")
    x2, ffn0 = _ffn_fwd(x1, nfg0, m0["sc_f"], m0["sh_f"], m0["g_f"], W, exch, cw0, cb0, cfg, "0")

    hk = _norm_mod_fwd(x2, kvg, mod["sc_kv"], mod["sh_kv"], cfg, "kv_norm")
    kvb = _mm(hk, W["kv_w"], mode="nn", out_dtypes=(bf16,), name="kv_proj")
    zf = _mm(hk, W["kv_wf"], mode="nn", name="kv_fproj")
    fc = _fgate_fwd(zf, fb, cfg)
    fct = fc[:, :NH].T
    fk = fct[:, None, :]

    h1 = _norm_mod_fwd(x2, nmg1, m1["sc_m"], m1["sh_m"], cfg, "mix_norm_1")
    q = _mm(h1, W["attn_w_q"], mode="nn", epi=lambda acc: (acc * cfg.DH ** -0.5,), out_dtypes=(bf16,), name="attn_q")
    o, lse, extra = _ta_fwd(q, kvb, fk, cfg, exch.rider("attn_fwd"))
    W.update(exch.done("attn_fwd", extra))
    x3, out_m1 = _mm(o, W["attn_w_out"], mode="nn", extras=[(x2, "mn"), (m1["g_m"], "n")], epi=_resid_epi,
                     out_dtypes=(f32, bf16), name="attn_out")
    x4, ffn1 = _ffn_fwd(x3, nfg1, m1["sc_f"], m1["sh_f"], m1["g_f"], W, exch, cw1, cb1, cfg, "1")

    dx, dfng, lcol = _final_loss(x4, fng, tgt, cfg)
    loss = (0.5 / D) * jnp.sum(lcol)

    dx, dw_up1, dw_down1, sf1 = _ffn_bwd(dx, x3, ffn1, nfg1, m1["sc_f"], m1["g_f"], W["ffn_w_up1"], W["ffn_w_down1"], cw1, cb1, cfg, "1")
    dout, dgm1 = _gate_bwd(dx, out_m1, m1["g_m"], cfg, "attn_gate_bwd")
    do = _mm(dout, W["attn_w_out"], mode="nt", out_dtypes=(bf16,), name="attn_do")
    dw_ao = _mm(o, dout, mode="tn", out_dtypes=(bf16,), name="attn_dwout")
    dq, dfq, delta, extra = _ta_bwd_dq(q, kvb, do, o, lse, fk, cfg,
                                       hand_over("attn_bwd", dict(ffn_w_up1=dw_up1, ffn_w_down1=dw_down1)))
    exch.done("attn_bwd", extra)
    dw_q = _mm(h1, dq, mode="tn", out_dtypes=(bf16,), name="attn_dwq")
    dk, dv, dfk, extra = _ta_bwd_dkv(q, kvb, do, delta, lse, fk, cfg,
                                     hand_over("attn_bwd_dkv", dict(attn_w_q=dw_q, attn_w_out=dw_ao)))
    exch.done("attn_bwd_dkv", extra)
    dh1 = _mm(dq, W["attn_w_q"], mode="nt", out_dtypes=(bf16,), name="attn_dh")
    dx, A1, B1 = _norm_mod_bwd(dh1, x2, nmg1, m1["sc_m"], dx, cfg, "mix_norm_bwd_1")

    dfc = jnp.pad((dfq[:, :, 0] + dfk[:, :, 0]).T, ((0, 0), (0, LANES - NH)))
    dzf, dfb = _fgate_bwd(dfc, zf, fb, cfg)
    dkv = jnp.concatenate([dk, dv], axis=1)
    dhk1 = _mm(dkv, W["kv_w"], mode="nt", name="kv_dh1")
    dhk = _mm(dzf, W["kv_wf"], mode="nt", extras=[(dhk1, "mn")], epi=lambda acc, e: (acc + e,), out_dtypes=(bf16,), name="kv_dh2")
    dw_kv = _mm(hk, dkv, mode="tn", out_dtypes=(bf16,), name="kv_dw")
    dw_kf = _mm(hk, dzf, mode="tn", out_dtypes=(bf16,), name="kv_dwf")
    dx, Ak, Bk = _norm_mod_bwd(dhk, x2, kvg, mod["sc_kv"], dx, cfg, "kv_norm_bwd")

    dx, dw_up0, dw_down0, sf0 = _ffn_bwd(dx, x1, ffn0, nfg0, m0["sc_f"], m0["g_f"], W["ffn_w_up0"], W["ffn_w_down0"], cw0, cb0, cfg, "0")
    dout, dgm0 = _gate_bwd(dx, out_m0, m0["g_m"], cfg, "ssm_gate_bwd")
    dz = _mm(dout, W["ssm_w_out"], mode="nt", out_dtypes=(bf16,), name="ssm_dz")
    dw_so = _mm(z, dout, mode="tn", out_dtypes=(bf16,), name="ssm_dwout")
    dpre, dgd, dglub = _glu_bwd(dz, gact, pre, cfg)
    dy = _mm(dpre, W["ssm_glu_w"], mode="nt", extras=[(dgd, "mn"), (y, "mn")],
             epi=lambda acc, e, yv: ((acc + e) * _gelu_grad(yv),), name="ssm_dy")
    dw_glu = _mm(gact, dpre, mode="tn", out_dtypes=(bf16,), name="ssm_dwglu")
    rider = hand_over("s5_bwd", dict(kv_w=jnp.concatenate([dw_kv, dw_kf[:, :NH]], axis=1), ffn_w_up0=dw_up0,
                                     ffn_w_down0=dw_down0, ssm_w_out=dw_so, ssm_glu_w=dw_glu))
    du, dbm, dcm, dab, dd, extra = _s5_bwd(u, dy, cin, s5["bmat"], s5["cmat"], s5["drow"], s5["arow"], s5["tab"], cfg, rider)
    exch.done("s5_bwd", extra)
    dh0 = _mm(du, W["ssm_w_in"], mode="nt", out_dtypes=(bf16,), name="ssm_dh")
    dw_in = _mm(h0, du, mode="tn", out_dtypes=(bf16,), name="ssm_dwin")
    dx, A0, B0 = _norm_mod_bwd(dh0, x, nmg0, m0["sc_m"], dx, cfg, "mix_norm_bwd_0")

    s5g = _s5_param_grads(s5p, dbm, dcm, dab, dd, cfg)
    big["ssm_w_in"] = dw_in
    small = dict(
        norm_mix_g=jnp.concatenate([(1.0 + m0["sc_m"]) * A0, (1.0 + m1["sc_m"]) * A1], axis=0),
        norm_ffn_g=jnp.concatenate([sf0["norm_g"], sf1["norm_g"]], axis=0),
        ssm_glu_b=dglub, kv_norm_g=(1.0 + mod["sc_kv"]) * Ak, forget_b=dfb[0, :NH],
        ffn_conv_w=jnp.stack([sf0["conv_w"], sf1["conv_w"]]), ffn_conv_b=jnp.concatenate([sf0["conv_b"], sf1["conv_b"]], axis=0),
        final_norm_g=dfng, **{"ssm_" + k: v[None] for k, v in s5g.items()})
    dmod = [jnp.concatenate([B0, nmg0 * A0, dgm0, sf0["sh"], sf0["sc"], sf0["gate"]], axis=1),
            jnp.concatenate([B1, nmg1 * A1, dgm1, sf1["sh"], sf1["sc"], sf1["gate"]], axis=1),
            jnp.concatenate([Bk, kvg * Ak], axis=1)]
    return loss, dx, big, small, dmod


WEIGHTS = ["mod_w", "mod_b", "norm_mix_g", "norm_ffn_g", "ssm_w_in", "ssm_log_step", "ssm_a_re", "ssm_a_im", "ssm_b_re",
           "ssm_b_im", "ssm_c_re", "ssm_c_im", "ssm_d", "ssm_glu_w", "ssm_glu_b", "ssm_w_out", "kv_mod_w", "kv_mod_b",
           "kv_norm_g", "kv_w", "forget_b", "attn_w_q", "attn_w_out", "ffn_w_up", "ffn_conv_w", "ffn_conv_b", "ffn_w_down",
           "final_norm_g"]
ARGS = ["x", "c"] + WEIGHTS + ["loss_target"] + ["m_" + n for n in WEIGHTS] + ["v_" + n for n in WEIGHTS]
SMALL = ["mod_b", "norm_mix_g", "norm_ffn_g", "ssm_log_step", "ssm_a_re", "ssm_a_im", "ssm_b_re", "ssm_b_im", "ssm_c_re",
         "ssm_c_im", "ssm_d", "ssm_glu_b", "kv_mod_b", "kv_norm_g", "forget_b", "ffn_conv_w", "ffn_conv_b", "final_norm_g"]
PACK_ROWS = 512


def _pack(arrs):
    flat = jnp.concatenate([a.reshape(-1).astype(f32) for a in arrs])
    unit = PACK_ROWS * LANES
    n = -(-flat.shape[0] // unit) * unit
    return jnp.pad(flat, (0, n - flat.shape[0])).reshape(-1, LANES)


def _unpack(packed, shapes):
    flat, out, off = packed.reshape(-1), [], 0
    for s in shapes:
        n = math.prod(s)
        out.append(flat[off:off + n].reshape(s))
        off += n
    return out


def _silu(v):
    return v * _sigmoid(v)


def _half(w, c, axis):
    r = w.shape[axis] // 2
    return lax.dynamic_slice_in_dim(w, c * r, r, axis=axis)


class _Exchange:
    FIRST = ["ssm_w_in", "ssm_glu_w", "ssm_w_out"]
    FWD = dict(s5_fwd=["ffn_w_up0"], ffn_up_0=["ffn_w_down0", "kv_w"], ffn_down_0=["attn_w_q", "attn_w_out"],
               attn_fwd=["ffn_w_up1", "ffn_w_down1"])
    LATE = (3, 4)

    def __init__(self, cfg, blocks, core):
        self.cfg, self.blocks, self.core = cfg, blocks, core
        self.parts, self.scattered, self.names = {}, {}, {}

    def weights(self, names, gathered):
        D, F, NH = self.cfg.D, self.cfg.F, self.cfg.NH
        W = {}
        for n, g in zip(names, gathered):
            if n.startswith("ffn_w_up"):
                W[n] = g.reshape(4, D, 2 * F // 4)
            elif n == "kv_w":
                full = g.reshape(4, D, -1).transpose(1, 0, 2).reshape(D, -1)
                W["kv_w"] = full[:, :2 * D]
                W["kv_wf"] = jnp.pad(full[:, 2 * D:], ((0, 0), (0, LANES - NH)))
            else:
                W[n] = g.reshape(-1, D)
        return W

    def sibling_sum(self, key, grads):
        D = self.cfg.D

        def blocks_of(n, g):
            if n.startswith("ffn_w_up"):
                return g.reshape(4, 2, D // 2, -1)
            if n == "kv_w":
                return g.reshape(D, 4, -1).transpose(1, 0, 2).reshape(4, 2, D // 2, -1)
            return g.reshape(4, 2, g.shape[0] // 8, g.shape[1])
        names = list(grads)
        gb = [blocks_of(n, grads[n]) for n in names]
        recv = _sibling_swap_halves(gb, f"grad_sibling_swap_{key}")
        core1 = jnp.reshape(self.core, (1,)).astype(jnp.int32)
        for n, g, r in zip(names, gb, recv):
            self.parts[n] = _add_own_half(g, r, core1, f"grad_add_{n}")
        return self.parts

    def rider(self, key, grads=None):
        if key in self.FWD:
            return _gather_rider([self.blocks[n] for n in self.FWD[key]], (1, 2) if key == "attn_fwd" else self.LATE)
        if grads is None:
            return None
        self.names[key] = list(grads)
        parts = self.sibling_sum(key, grads)
        return _scatter_rider([parts[n] for n in self.names[key]])

    def done(self, key, extra):
        if key in self.FWD:
            return self.weights(self.FWD[key], extra)
        self.scattered.update(zip(self.names[key], extra))
        return {}


def kernel(x, c, mod_w, mod_b, norm_mix_g, norm_ffn_g, ssm_w_in, ssm_log_step, ssm_a_re, ssm_a_im, ssm_b_re, ssm_b_im, ssm_c_re, ssm_c_im, ssm_d, ssm_glu_w, ssm_glu_b, ssm_w_out, kv_mod_w, kv_mod_b, kv_norm_g, kv_w, forget_b, attn_w_q, attn_w_out, ffn_w_up, ffn_conv_w, ffn_conv_b, ffn_w_down, final_norm_g, loss_target, m_mod_w, m_mod_b, m_norm_mix_g, m_norm_ffn_g, m_ssm_w_in, m_ssm_log_step, m_ssm_a_re, m_ssm_a_im, m_ssm_b_re, m_ssm_b_im, m_ssm_c_re, m_ssm_c_im, m_ssm_d, m_ssm_glu_w, m_ssm_glu_b, m_ssm_w_out, m_kv_mod_w, m_kv_mod_b, m_kv_norm_g, m_kv_w, m_forget_b, m_attn_w_q, m_attn_w_out, m_ffn_w_up, m_ffn_conv_w, m_ffn_conv_b, m_ffn_w_down, m_final_norm_g, v_mod_w, v_mod_b, v_norm_mix_g, v_norm_ffn_g, v_ssm_w_in, v_ssm_log_step, v_ssm_a_re, v_ssm_a_im, v_ssm_b_re, v_ssm_b_im, v_ssm_c_re, v_ssm_c_im, v_ssm_d, v_ssm_glu_w, v_ssm_glu_b, v_ssm_w_out, v_kv_mod_w, v_kv_mod_b, v_kv_norm_g, v_kv_w, v_forget_b, v_attn_w_q, v_attn_w_out, v_ffn_w_up, v_ffn_conv_w, v_ffn_conv_b, v_ffn_w_down, v_final_norm_g):
    a = dict(locals())
    assert list(a) == ARGS
    return _step(CFG, a)


def _step(cfg, a):
    D, F, NH = cfg.D, cfg.F, cfg.NH
    x_, y_, c_ = _place()
    chip, dev = 2 * x_ + y_, 4 * x_ + 2 * y_ + c_

    big_src = dict(ssm_w_in=a["ssm_w_in"][0], ssm_glu_w=a["ssm_glu_w"][0], ssm_w_out=a["ssm_w_out"][0],
                   attn_w_q=a["attn_w_q"][0], attn_w_out=a["attn_w_out"][0],
                   ffn_w_up0=a["ffn_w_up"][0], ffn_w_up1=a["ffn_w_up"][1],
                   ffn_w_down0=a["ffn_w_down"][0], ffn_w_down1=a["ffn_w_down"][1], kv_w=a["kv_w"])
    big_names = list(big_src)
    exch = _Exchange(cfg, {n: _half(big_src[n], c_, 0).astype(bf16) for n in big_names}, c_)
    first = exch.FIRST
    blocks = [exch.blocks[n] for n in first] + [_half(a["ssm_glu_b"], c_, 1), _half(a["ffn_conv_w"], c_, 2), a["c"]]
    got = _allgather8(blocks, "gather_weights")
    W = exch.weights(first, got)
    glu_b_full = got[-3].reshape(D)
    conv_w_full = got[-2].transpose(1, 2, 0, 3).reshape(2, 3, 2 * F)
    c16 = jnp.pad(got[-1].reshape(N_DEV, D), ((0, 16 - N_DEV), (0, 0)))

    mcols = [_mm(c16, a["mod_w"][l], mode="nn", a_pro=_silu, name=f"mod_fwd_{l}") for l in range(2)]
    mcols.append(_mm(c16, a["kv_mod_w"], mode="nn", a_pro=_silu, name="mod_fwd_kv"))
    widths = [m.shape[1] for m in mcols]
    mall = _allgather8([jnp.concatenate(mcols, axis=1)[:N_DEV]], "gather_mod")[0][0::2]
    offs = [0, widths[0], widths[0] + widths[1]]
    rows = []
    for off, wd, bias in zip(offs, widths, [a["mod_b"][0], a["mod_b"][1], a["kv_mod_b"]]):
        fullm = mall[:, :, off:off + wd].transpose(1, 0, 2).reshape(N_DEV, 4 * wd) + bias
        rows.append(lax.dynamic_slice_in_dim(fullm, dev, 1, axis=0))
    mod = {}
    for l in range(2):
        mod[f"l{l}"] = dict(zip(["sh_m", "sc_m", "g_m", "sh_f", "sc_f", "g_f"], jnp.split(rows[l], 6, axis=1)))
    mod["sh_kv"], mod["sc_kv"] = jnp.split(rows[2], 2, axis=1)

    sp = {n: a[n] for n in ["norm_mix_g", "norm_ffn_g", "kv_norm_g", "final_norm_g", "ffn_conv_b", "forget_b", "ssm_log_step",
                            "ssm_a_re", "ssm_a_im", "ssm_b_re", "ssm_b_im", "ssm_c_re", "ssm_c_im", "ssm_d"]}
    sp["ssm_glu_b"], sp["ffn_conv_w"] = glu_b_full, conv_w_full
    loss, dx, big, small, dmod = _local_step(cfg, a["x"][0], a["loss_target"][0], mod, W, sp, exch)
    loss = lax.psum(loss, ("x", "y", "c"))

    small["mod_b"] = jnp.concatenate([dmod[0], dmod[1]], axis=0)
    small["kv_mod_b"] = dmod[2]
    shapes = [(2, 6 * D) if n == "mod_b" else (1, D) if n == "ssm_glu_b" else (2, 3, 2 * F) if n == "ffn_conv_w"
              else a[n].shape for n in SMALL]
    small_rider = _gather_rider([_pack([small[n] for n in SMALL])])

    last = list(big)
    exch.scattered.update(zip(last, _chip_scatter([exch.sibling_sum("tail", big)[n] for n in last], "grad_chip_scatter")))
    chip1, core1 = jnp.reshape(chip, (1,)).astype(jnp.int32), jnp.reshape(c_, (1,)).astype(jnp.int32)
    mine = {n: _sum_parts(exch.parts[n], exch.scattered[n], chip1, f"grad_sum_{n}") for n in big_names}
    other = dict(zip(big_names, _sibling_send([mine[n] for n in big_names], "grad_sibling_send")))

    grads, delta, new_m, new_v = {}, {}, {}, {}
    members = dict(ffn_w_up=["ffn_w_up0", "ffn_w_up1"], ffn_w_down=["ffn_w_down0", "ffn_w_down1"], kv_w=["kv_w"],
                   ssm_w_in=["ssm_w_in"], ssm_glu_w=["ssm_glu_w"], ssm_w_out=["ssm_w_out"], attn_w_q=["attn_w_q"],
                   attn_w_out=["attn_w_out"])
    for n, parts_ in members.items():
        shp = a[n].shape
        three = lambda t: t.reshape(len(parts_), -1, shp[-1])
        g_, d_, m_, v_, extra = _adamw_halves(three(a[n]), three(a["m_" + n]), three(a["v_" + n]),
                                              jnp.stack([mine[p] for p in parts_]), jnp.stack([other[p] for p in parts_]),
                                              core1, f"adamw_{n}", small_rider if n == "ffn_w_up" else None)
        if extra:
            packs = extra[0]
        grads[n], delta[n], new_m[n], new_v[n] = g_.reshape(shp), d_.reshape(shp), m_.reshape(shp), v_.reshape(shp)

    gsmall = dict(zip(SMALL, _unpack(_sum_lead(packs, f32, "sum_small"), shapes)))
    per_dev = packs.reshape(N_DEV, -1)
    sizes = [math.prod(s) for s in shapes]
    starts = dict(zip(SMALL, [sum(sizes[:i]) for i in range(len(sizes))]))

    def rows_of(name, l, width):
        st = starts[name] + l * 6 * D
        blk = lax.dynamic_slice(per_dev, (0, st + chip * width), (N_DEV, width))
        return jnp.pad(blk, ((0, 16 - N_DEV), (0, 0)))
    g_mod_w = jnp.stack([_mm(c16, rows_of("mod_b", l, 6 * D // 4), mode="tn", a_pro=_silu, name=f"mod_dw_{l}") for l in range(2)])
    g_kv_mod_w = _mm(c16, rows_of("kv_mod_b", 0, 2 * D // 4), mode="tn", a_pro=_silu, name="mod_dw_kv")
    gsmall["ssm_glu_b"] = lax.dynamic_slice_in_dim(gsmall["ssm_glu_b"], chip * (D // 4), D // 4, axis=1)
    gsmall["ffn_conv_w"] = lax.dynamic_slice_in_dim(gsmall["ffn_conv_w"], chip * (2 * F // 4), 2 * F // 4, axis=2)

    grads.update(gsmall)
    grads["mod_w"], grads["kv_mod_w"] = g_mod_w, g_kv_mod_w
    for n in ["mod_w", "kv_mod_w"]:
        shp = a[n].shape
        two = lambda t: t.reshape(-1, shp[-1])
        d_, m_, v_ = _adamw(two(a[n]), two(grads[n]), two(a["m_" + n]), two(a["v_" + n]), f"adamw_{n}")
        delta[n], new_m[n], new_v[n] = d_.reshape(shp), m_.reshape(shp), v_.reshape(shp)
    grads = {n: grads[n].reshape(a[n].shape) for n in WEIGHTS}
    sshapes = [a[n].shape for n in SMALL]
    d_, m_, v_ = _adamw(_pack([a[n] for n in SMALL]), _pack([grads[n] for n in SMALL]), _pack([a["m_" + n] for n in SMALL]),
                        _pack([a["v_" + n] for n in SMALL]), "adamw_small")
    for n, dd_, mm_, vv_ in zip(SMALL, _unpack(d_, sshapes), _unpack(m_, sshapes), _unpack(v_, sshapes)):
        delta[n], new_m[n], new_v[n] = dd_, mm_, vv_

    return (loss, dx[None], *[grads[n] for n in WEIGHTS], *[delta[n] for n in WEIGHTS],
            *[new_m[n] for n in WEIGHTS], *[new_v[n] for n in WEIGHTS])
```

```python
import collections
import functools
import math

import jax
import jax.numpy as jnp
from jax import lax
from jax.experimental import pallas as pl
from jax.experimental.pallas import tpu as pltpu

f32 = jnp.float32
bf16 = jnp.bfloat16
MESH = pl.DeviceIdType.MESH

LANES = 128
SUBLANES = 8
VMEM_BYTES_V7X = 64 * 1024 * 1024
VMEM_LIMIT = 56 * 1024 * 1024

Cfg = collections.namedtuple("Cfg", "L D G P H NH DH F TC BQ")
CFG = Cfg(L=4096, D=2048, G=128, P=64, H=16, NH=16, DH=128, F=5632, TC=512, BQ=1024)
NORM_EPS = 1e-6
ADAM_LR, ADAM_B1, ADAM_B2, ADAM_EPS, ADAM_WD, ADAM_STEP = 0.001, 0.9, 0.999, 1e-08, 0.01, 10
N_DEV = 8


def _cp(sem=None):
    return pltpu.CompilerParams(dimension_semantics=sem, vmem_limit_bytes=VMEM_LIMIT)


def _tile(dim, pref, unit=LANES):
    if dim <= pref:
        return dim
    t = (pref // unit) * unit
    while t > unit and dim % t:
        t -= unit
    assert dim % t == 0, (dim, pref)
    return t


_DIMS = {"nn": (((1,), (0,)), ((), ())), "nt": (((1,), (1,)), ((), ())), "tn": (((0,), (0,)), ((), ()))}


def _mm(a, b, *, mode, name, tm=1024, tn=1024, tk=2048, b4=False, out4=False, a_pro=None, extras=(), epi=None,
        out_dtypes=(f32,), rider=None):
    if mode == "tn":
        K, M = a.shape
    else:
        M, K = a.shape
    if b4:
        R, c4 = b.shape[1], b.shape[2]
        N = R if mode == "nt" else 4 * c4
        assert (K == 4 * c4) if mode == "nt" else (K == R)
    else:
        N = b.shape[0] if mode == "nt" else b.shape[1]
        assert K == (b.shape[1] if mode == "nt" else b.shape[0])
    n4 = N // 4
    tm = _tile(M, tm, LANES if mode == "tn" else SUBLANES * 2)
    tn = _tile(n4 if out4 or (b4 and mode != "nt") else N, tn)
    tk = _tile(b.shape[2] if (b4 and mode == "nt") else K, tk)
    nm, nn_, nk = M // tm, N // tn, K // tk

    a_spec = pl.BlockSpec((tk, tm), lambda i, j, k: (k, i)) if mode == "tn" else pl.BlockSpec((tm, tk), lambda i, j, k: (i, k))
    if b4 and mode == "nt":
        q = b.shape[2] // tk
        b_spec = pl.BlockSpec((None, tn, tk), lambda i, j, k: (lax.div(k, q), j, lax.rem(k, q)))
    elif b4:
        q = b.shape[2] // tn
        b_spec = pl.BlockSpec((None, tk, tn), lambda i, j, k: (lax.div(j, q), k, lax.rem(j, q)))
    elif mode == "nt":
        b_spec = pl.BlockSpec((tn, tk), lambda i, j, k: (j, k))
    else:
        b_spec = pl.BlockSpec((tk, tn), lambda i, j, k: (k, j))
    ex_specs = []
    for arr, kind in extras:
        if kind == "mn":
            ex_specs.append(pl.BlockSpec((tm, tn), lambda i, j, k: (i, j)))
        else:
            ex_specs.append(pl.BlockSpec((1, tn), lambda i, j, k: (0, j)))
    if out4:
        qo = n4 // tn
        o_spec = pl.BlockSpec((None, tm, tn), lambda i, j, k: (lax.div(j, qo), i, lax.rem(j, qo)))
        o_shapes = [jax.ShapeDtypeStruct((4, M, n4), dt) for dt in out_dtypes]
    else:
        o_spec = pl.BlockSpec((tm, tn), lambda i, j, k: (i, j))
        o_shapes = [jax.ShapeDtypeStruct((M, N), dt) for dt in out_dtypes]
    ne, no = len(extras), len(out_dtypes)
    dims = _DIMS[mode]

    def body(a_ref, b_ref, *rest):
        ex_refs, o_refs, acc_ref = rest[:ne], rest[ne:ne + no], rest[ne + no]
        k = pl.program_id(2)

        @pl.when(k == 0)
        def _():
            acc_ref[...] = jnp.zeros_like(acc_ref)

        av = a_ref[...]
        if a_pro is not None:
            av = a_pro(av)
        acc_ref[...] += lax.dot_general(av.astype(bf16), b_ref[...].astype(bf16), dims, preferred_element_type=f32)

        @pl.when(k == nk - 1)
        def _():
            acc = acc_ref[...]
            outs = (acc,) if epi is None else epi(acc, *[r[...] for r in ex_refs])
            for o_ref, o in zip(o_refs, outs):
                o_ref[...] = o.astype(o_ref.dtype)

    res, extra = _host_call(
        body, name=name, grid=(nm, nn_, nk), in_specs=[a_spec, b_spec] + ex_specs, out_specs=[o_spec] * no,
        out_shape=o_shapes, scratch_shapes=[pltpu.VMEM((tm, tn), f32)], args=(a, b, *[e[0] for e in extras]), rider=rider)
    res = res[0] if no == 1 else res
    return (res, extra) if rider else res


HALO = 16


def _rowwise(fn, ins, outs, accs, *, L, C, tl, tc, name):
    tl = _tile(L, tl, HALO)
    tc = _tile(C, tc)
    ni, nj = L // tl, C // tc
    hb = tl // HALO
    nh = L // HALO
    in_specs = []
    for spec in ins:
        kind = spec[1]
        off = spec[2] if len(spec) > 2 else 0
        if kind == "rc":
            in_specs.append(pl.BlockSpec((tl, tc), lambda j, i, off=off: (i, j + off)))
        elif kind == "c":
            in_specs.append(pl.BlockSpec((1, tc), lambda j, i, off=off: (0, j + off)))
        elif kind == "c3":
            in_specs.append(pl.BlockSpec((3, tc), lambda j, i, off=off: (0, j + off)))
        elif kind == "prev":
            in_specs.append(pl.BlockSpec((HALO, tc), lambda j, i, off=off: (jnp.maximum(i * hb - 1, 0), j + off)))
        elif kind == "next":
            in_specs.append(pl.BlockSpec((HALO, tc), lambda j, i, off=off: (jnp.minimum((i + 1) * hb, nh - 1), j + off)))
        else:
            raise ValueError(kind)
    out_specs = [pl.BlockSpec((tl, tc), lambda j, i: (i, j)) for _ in outs]
    out_specs += [pl.BlockSpec((r, tc), lambda j, i: (0, j)) for r in accs]
    out_shape = [jax.ShapeDtypeStruct((L, C), dt) for dt in outs] + [jax.ShapeDtypeStruct((r, C), f32) for r in accs]
    nin, nout, nacc = len(ins), len(outs), len(accs)

    def body(*refs):
        i = pl.program_id(1)
        tiles = [r[...] for r in refs[:nin]]
        o_vals, a_vals = fn(i, ni, *tiles)
        for r, v in zip(refs[nin:nin + nout], o_vals):
            r[...] = v.astype(r.dtype)
        if nacc:
            @pl.when(i == 0)
            def _():
                for r in refs[nin + nout:]:
                    r[...] = jnp.zeros_like(r)
            for r, v in zip(refs[nin + nout:], a_vals):
                r[...] += v

    res = pl.pallas_call(
        body, name=name, grid=(nj, ni), in_specs=in_specs, out_specs=out_specs, out_shape=out_shape,
        compiler_params=_cp(("parallel", "arbitrary")),
    )(*[s[0] for s in ins])
    return res


def _colsum(v):
    return jnp.sum(v, axis=0, keepdims=True)


def _sigmoid(x):
    return 1.0 / (1.0 + jnp.exp(-x))


_GELU_C = math.sqrt(2.0 / math.pi)


def _gelu(y):
    t = jnp.tanh(_GELU_C * (y + 0.044715 * y * y * y))
    return 0.5 * y * (1.0 + t)


def _gelu_grad(y):
    y2 = y * y
    t = jnp.tanh(_GELU_C * (y + 0.044715 * y * y2))
    return 0.5 * (1.0 + t) + 0.5 * y * (1.0 - t * t) * _GELU_C * (1.0 + 3.0 * 0.044715 * y2)


def _norm_mod_fwd(x, g, sc, sh, cfg, name):
    def fn(i, ni, xv, gv, scv, shv):
        rstd = lax.rsqrt(jnp.mean(xv * xv, axis=-1, keepdims=True) + NORM_EPS)
        return [xv * rstd * gv * (1.0 + scv) + shv], []
    return _rowwise(fn, [(x, "rc"), (g, "c"), (sc, "c"), (sh, "c")], [bf16], [], L=cfg.L, C=cfg.D, tl=256, tc=cfg.D, name=name)[0]


def _norm_mod_bwd(dh, x, g, sc, dres, cfg, name):
    def fn(i, ni, dhv, xv, gv, scv, *rest):
        dhv = dhv.astype(f32)
        rstd = lax.rsqrt(jnp.mean(xv * xv, axis=-1, keepdims=True) + NORM_EPS)
        xh = xv * rstd
        dxh = dhv * (gv * (1.0 + scv))
        dx = rstd * (dxh - xh * jnp.mean(dxh * xh, axis=-1, keepdims=True))
        if rest:
            dx = dx + rest[0]
        return [dx], [_colsum(dhv * xh), _colsum(dhv)]
    ins = [(dh, "rc"), (x, "rc"), (g, "c"), (sc, "c")] + ([(dres, "rc")] if dres is not None else [])
    return _rowwise(fn, ins, [f32], [1, 1], L=cfg.L, C=cfg.D, tl=256, tc=cfg.D, name=name)


def _final_loss(x, g, tgt, cfg):
    D = cfg.D

    def fn(i, ni, xv, gv, tv):
        rstd = lax.rsqrt(jnp.mean(xv * xv, axis=-1, keepdims=True) + NORM_EPS)
        xh = xv * rstd
        err = xh * gv - tv
        dy = err * (1.0 / D)
        dxh = dy * gv
        dx = rstd * (dxh - xh * jnp.mean(dxh * xh, axis=-1, keepdims=True))
        return [dx], [_colsum(dy * xh), _colsum(err * err)]
    return _rowwise(fn, [(x, "rc"), (g, "c"), (tgt, "rc")], [f32], [1, 1], L=cfg.L, C=D, tl=256, tc=D, name="final_loss")


def _gate_bwd(dx, out, gate, cfg, name):
    def fn(i, ni, dxv, ov, gv):
        return [dxv * gv], [_colsum(dxv * ov.astype(f32))]
    return _rowwise(fn, [(dx, "rc"), (out, "rc"), (gate, "c")], [bf16], [1], L=cfg.L, C=cfg.D, tl=512, tc=cfg.D, name=name)


def _glu_bwd(dz, g, pre, cfg):
    def fn(i, ni, dzv, gv, pv):
        dzv = dzv.astype(f32)
        gv = gv.astype(f32)
        s = _sigmoid(pv)
        dpre = dzv * gv * s * (1.0 - s)
        return [dpre, dzv * s], [_colsum(dpre)]
    return _rowwise(fn, [(dz, "rc"), (g, "rc"), (pre, "rc")], [bf16, f32], [1], L=cfg.L, C=cfg.D, tl=512, tc=cfg.D, name="glu_bwd")


def _shift_rows(av, pv, k, i):
    rows = lax.broadcasted_iota(jnp.int32, av.shape, 0)
    cur = pltpu.roll(av, k, 0)
    prev = pltpu.roll(pv, k, 0)
    prev = jnp.where(i > 0, prev, 0.0)
    prev_full = jnp.concatenate([prev, jnp.zeros((av.shape[0] - pv.shape[0], av.shape[1]), av.dtype)], axis=0) \
        if av.shape[0] > pv.shape[0] else prev
    return jnp.where(rows >= k, cur, prev_full)


def _shift_rows_up(av, nv, k, i, ni):
    n, h = av.shape[0], nv.shape[0]
    rows = lax.broadcasted_iota(jnp.int32, av.shape, 0)
    cur = pltpu.roll(av, n - k, 0)
    nxt = pltpu.roll(nv, h - k, 0)
    nxt = jnp.where(i < ni - 1, nxt, 0.0)
    nxt_full = jnp.concatenate([jnp.zeros((n - h, av.shape[1]), av.dtype), nxt], axis=0) if n > h else nxt
    return jnp.where(rows < n - k, cur, nxt_full)


def _conv3(av, pv, w, i):
    return w[0:1] * _shift_rows(av, pv, 2, i) + w[1:2] * _shift_rows(av, pv, 1, i) + w[2:3] * av


def _conv_act_fwd(a, conv_w, conv_b, cfg):
    F = cfg.F
    tc = _tile(F, 1408)
    nb = F // tc

    def fn(i, ni, au, av, pu, pv, wu, wv, bu, bv):
        cu = _conv3(au.astype(f32), pu.astype(f32), wu, i) + bu
        cv = _conv3(av.astype(f32), pv.astype(f32), wv, i) + bv
        return [cu * _sigmoid(cu) * cv], []
    ins = [(a, "rc"), (a, "rc", nb), (a, "prev"), (a, "prev", nb), (conv_w, "c3"), (conv_w, "c3", nb), (conv_b, "c"), (conv_b, "c", nb)]
    return _rowwise(fn, ins, [bf16], [], L=cfg.L, C=F, tl=512, tc=tc, name="conv_act_fwd")[0]


def _conv_act_bwd1(dact, a, conv_w, conv_b, cfg):
    F = cfg.F
    tc = _tile(F, 1408)
    nb = F // tc

    def fn(i, ni, dav, au, av, pu, pv, wu, wv, bu, bv):
        dav = dav.astype(f32)
        au, av, pu, pv = au.astype(f32), av.astype(f32), pu.astype(f32), pv.astype(f32)
        au1, au2 = _shift_rows(au, pu, 1, i), _shift_rows(au, pu, 2, i)
        av1, av2 = _shift_rows(av, pv, 1, i), _shift_rows(av, pv, 2, i)
        cu = wu[0:1] * au2 + wu[1:2] * au1 + wu[2:3] * au + bu
        cv = wv[0:1] * av2 + wv[1:2] * av1 + wv[2:3] * av + bv
        s = _sigmoid(cu)
        dcu = dav * cv * (s * (1.0 + cu * (1.0 - s)))
        dcv = dav * cu * s
        dwu = jnp.concatenate([_colsum(dcu * au2), _colsum(dcu * au1), _colsum(dcu * au)], axis=0)
        dwv = jnp.concatenate([_colsum(dcv * av2), _colsum(dcv * av1), _colsum(dcv * av)], axis=0)
        return [dcu, dcv], [dwu, dwv, _colsum(dcu), _colsum(dcv)]
    ins = [(dact, "rc"), (a, "rc"), (a, "rc", nb), (a, "prev"), (a, "prev", nb), (conv_w, "c3"), (conv_w, "c3", nb),
           (conv_b, "c"), (conv_b, "c", nb)]
    return _rowwise(fn, ins, [bf16, bf16], [3, 3, 1, 1], L=cfg.L, C=F, tl=512, tc=tc, name="conv_act_bwd1")


def _conv_bwd2(dc, w, cfg, name):
    F = cfg.F
    tc = _tile(F, 1408)

    def fn(i, ni, dcv, nxt, wv):
        dcv, nxt = dcv.astype(f32), nxt.astype(f32)
        return [wv[2:3] * dcv + wv[1:2] * _shift_rows_up(dcv, nxt, 1, i, ni) + wv[0:1] * _shift_rows_up(dcv, nxt, 2, i, ni)], []
    return _rowwise(fn, [(dc, "rc"), (dc, "next"), (w, "c3")], [bf16], [], L=cfg.L, C=F, tl=512, tc=tc, name=name)[0]


NSLAB = 8


def _s5_tables(abar_re, abar_im, lam_re, lam_im, step, cfg):
    J = cfg.G // 8
    expo = jnp.array([r + 1 for r in range(8)] + [8 * 2 ** p for p in range(8)], f32)[:, None, None]
    mag = jnp.exp(lam_re * step * expo)
    ang = lam_im * step * expo
    t_re = (mag * jnp.cos(ang)).reshape(16, J, 8 * cfg.P).transpose(1, 0, 2)
    t_im = (mag * jnp.sin(ang)).reshape(16, J, 8 * cfg.P).transpose(1, 0, 2)
    tab = jnp.concatenate([t_re, t_im], axis=-1)
    arow = jnp.concatenate([abar_re.reshape(J, 1, 8 * cfg.P), abar_im.reshape(J, 1, 8 * cfg.P)], axis=-1)
    return arow, tab


def _s5_mats(bbar_re, bbar_im, c_re, c_im, cfg):
    J, P, H = cfg.G // 8, cfg.P, cfg.H
    eye = jnp.eye(8, dtype=f32)

    def bd_in(bb):
        bb = bb.reshape(J, 8, P, H)
        return jnp.einsum("jgph,gk->jghkp", bb, eye).reshape(J, 8 * H, 8 * P)

    def bd_out(cc):
        cc = cc.reshape(J, 8, H, P)
        return jnp.einsum("jghp,gk->jgpkh", cc, eye).reshape(J, 8 * P, 8 * H)

    bmat = jnp.concatenate([bd_in(bbar_re), bd_in(bbar_im)], axis=2).astype(bf16)
    cmat = jnp.concatenate([bd_out(c_re), -bd_out(c_im)], axis=1).astype(bf16)
    return bmat, cmat


def _s5_unmats(dbmat, dcmat, cfg):
    J, P, H = cfg.G // 8, cfg.P, cfg.H
    eye = jnp.eye(8, dtype=f32)
    db = dbmat.reshape(J, 8, H, 2, 8, P)
    db = jnp.einsum("jghckp,gk->cjgph", db, eye).reshape(2, cfg.G, P, H)
    dc = dcmat.reshape(J, 2, 8, P, 8, H)
    dc = jnp.einsum("jcgpkh,gk->cjghp", dc, eye).reshape(2, cfg.G, H, P)
    return db[0], db[1], dc[0], -dc[1]


def _chunk_scan(x_ref, row0, nt, arow_ref, tab_ref, c0, reverse):
    sg = -1.0 if reverse else 1.0
    rows = lax.broadcasted_iota(jnp.int32, (nt, LANES), 0)
    order = list(range(7, -1, -1)) if reverse else list(range(8))

    def ld(k, r):
        return x_ref[k, pl.ds(row0 + r, nt, stride=8), :]

    def tab(row, k):
        return tab_ref[pl.ds(row, 1), pl.ds(k * LANES, LANES)]

    carries = [None] * NSLAB
    for k in range(4):
        ar = arow_ref[:, pl.ds(k * LANES, LANES)]
        ai = sg * arow_ref[:, pl.ds((4 + k) * LANES, LANES)]
        sr, si = ld(k, order[0]), ld(4 + k, order[0])
        for r in order[1:]:
            sr, si = ar * sr - ai * si + ld(k, r), ar * si + ai * sr + ld(4 + k, r)
        if reverse:
            cr = jnp.where(rows == nt - 1, c0[k], pltpu.roll(sr, nt - 1, 0))
            ci = jnp.where(rows == nt - 1, c0[4 + k], pltpu.roll(si, nt - 1, 0))
        else:
            cr = jnp.where(rows == 0, c0[k], pltpu.roll(sr, 1, 0))
            ci = jnp.where(rows == 0, c0[4 + k], pltpu.roll(si, 1, 0))
        d, p = 1, 0
        while d < nt:
            qr, qi = tab(8 + p, k), sg * tab(8 + p, 4 + k)
            if reverse:
                shr, shi, m = pltpu.roll(cr, nt - d, 0), pltpu.roll(ci, nt - d, 0), rows < nt - d
            else:
                shr, shi, m = pltpu.roll(cr, d, 0), pltpu.roll(ci, d, 0), rows >= d
            cr, ci = cr + jnp.where(m, qr * shr - qi * shi, 0.0), ci + jnp.where(m, qr * shi + qi * shr, 0.0)
            d, p = 2 * d, p + 1
        carries[k], carries[4 + k] = cr, ci
        sr, si = cr, ci
        for r in order:
            sr, si = ar * sr - ai * si + ld(k, r), ar * si + ai * sr + ld(4 + k, r)
            x_ref[k, pl.ds(row0 + r, nt, stride=8), :] = sr
            x_ref[4 + k, pl.ds(row0 + r, nt, stride=8), :] = si
    return carries


def _slabs_to_mat(x_ref, row0, n):
    return jnp.concatenate([x_ref[k, pl.ds(row0, n), :] for k in range(NSLAB)], axis=1)


def _mat_to_slabs(x_ref, row0, n, m):
    for k in range(NSLAB):
        x_ref[k, pl.ds(row0, n), :] = m[:, k * LANES:(k + 1) * LANES]


def _s5_fwd(u, bmat, cmat, drow, arow, tab, cfg, rider=None):
    L, D, Tc = cfg.L, cfg.D, cfg.TC
    J, NC, nt = cfg.G // 8, L // Tc, Tc // 8
    W = NSLAB * LANES

    def body(u_ref, b_ref, c_ref, d_ref, a_ref, t_ref, y_ref, g_ref, cin_ref, x_ref, st_ref):
        c = pl.program_id(1)

        @pl.when(c == 0)
        def _():
            st_ref[...] = jnp.zeros_like(st_ref)

        cin_ref[...] = st_ref[...]
        ub = u_ref[...]
        _mat_to_slabs(x_ref, 0, Tc, jnp.dot(ub.astype(bf16), b_ref[...], preferred_element_type=f32))
        c0 = [st_ref[:, pl.ds(k * LANES, LANES)] for k in range(NSLAB)]
        _chunk_scan(x_ref, 0, nt, a_ref, t_ref, c0, False)
        for k in range(NSLAB):
            st_ref[:, pl.ds(k * LANES, LANES)] = x_ref[k, pl.ds(Tc - 1, 1), :]
        s = _slabs_to_mat(x_ref, 0, Tc).astype(bf16)
        y = jnp.dot(s, c_ref[...], preferred_element_type=f32) + d_ref[...] * ub
        y_ref[...] = y
        g_ref[...] = _gelu(y).astype(bf16)

    outs, extra = _host_call(
        body, name="s5_fwd", grid=(J, NC), rider=rider, args=(u, bmat, cmat, drow, arow, tab),
        in_specs=[pl.BlockSpec((Tc, LANES), lambda j, c: (c, j)),
                  pl.BlockSpec((None, LANES, W), lambda j, c: (j, 0, 0)),
                  pl.BlockSpec((None, W, LANES), lambda j, c: (j, 0, 0)),
                  pl.BlockSpec((1, LANES), lambda j, c: (0, j)),
                  pl.BlockSpec((None, 1, W), lambda j, c: (j, 0, 0)),
                  pl.BlockSpec((None, 16, W), lambda j, c: (j, 0, 0))],
        out_specs=[pl.BlockSpec((Tc, LANES), lambda j, c: (c, j)),
                   pl.BlockSpec((Tc, LANES), lambda j, c: (c, j)),
                   pl.BlockSpec((None, None, 1, W), lambda j, c: (j, c, 0, 0))],
        out_shape=[jax.ShapeDtypeStruct((L, D), f32), jax.ShapeDtypeStruct((L, D), bf16),
                   jax.ShapeDtypeStruct((J, NC, 1, W), f32)],
        scratch_shapes=[pltpu.VMEM((NSLAB, Tc, LANES), f32), pltpu.VMEM((1, W), f32)])
    return (*outs, extra)


def _s5_bwd(u, dy, cin, bmat, cmat, drow, arow, tab, cfg, rider=None):
    L, D, Tc = cfg.L, cfg.D, cfg.TC
    J, NC, nt = cfg.G // 8, L // Tc, Tc // 8
    W = NSLAB * LANES
    PAD = 0

    def body(u_ref, dy_ref, cin_ref, b_ref, c_ref, d_ref, a_ref, t_ref,
             du_ref, db_ref, dc_ref, da_ref, dd_ref, s_ref, g_ref, gst_ref):
        c = pl.program_id(1)

        @pl.when(c == 0)
        def _():
            gst_ref[...] = jnp.zeros_like(gst_ref)
            db_ref[...] = jnp.zeros_like(db_ref)
            dc_ref[...] = jnp.zeros_like(dc_ref)
            da_ref[...] = jnp.zeros_like(da_ref)
            dd_ref[...] = jnp.zeros_like(dd_ref)

        ub, dyb = u_ref[...], dy_ref[...]
        ub16, dy16 = ub.astype(bf16), dyb.astype(bf16)
        _mat_to_slabs(s_ref, PAD, Tc, jnp.dot(ub16, b_ref[...], preferred_element_type=f32))
        c0 = [cin_ref[:, pl.ds(k * LANES, LANES)] for k in range(NSLAB)]
        tile_in = _chunk_scan(s_ref, PAD, nt, a_ref, t_ref, c0, False)
        _mat_to_slabs(g_ref, 0, Tc, lax.dot_general(dy16, c_ref[...], _DIMS["nt"], preferred_element_type=f32))
        g0 = [gst_ref[:, pl.ds(k * LANES, LANES)] for k in range(NSLAB)]
        _chunk_scan(g_ref, 0, nt, a_ref, t_ref, g0, True)
        for k in range(NSLAB):
            gst_ref[:, pl.ds(k * LANES, LANES)] = g_ref[k, pl.ds(0, 1), :]
        for k in range(4):
            acc_r = jnp.zeros((nt, LANES), f32)
            acc_i = jnp.zeros((nt, LANES), f32)
            for r in range(8):
                gr = g_ref[k, pl.ds(r, nt, stride=8), :]
                gi = g_ref[4 + k, pl.ds(r, nt, stride=8), :]
                if r == 0:
                    pr, pi = tile_in[k], tile_in[4 + k]
                else:
                    pr = s_ref[k, pl.ds(PAD + r - 1, nt, stride=8), :]
                    pi = s_ref[4 + k, pl.ds(PAD + r - 1, nt, stride=8), :]
                acc_r += gr * pr + gi * pi
                acc_i += gi * pr - gr * pi
            da_ref[:, pl.ds(k * LANES, LANES)] += _colsum(acc_r)
            da_ref[:, pl.ds((4 + k) * LANES, LANES)] += _colsum(acc_i)
        gm = _slabs_to_mat(g_ref, 0, Tc).astype(bf16)
        sm = _slabs_to_mat(s_ref, PAD, Tc).astype(bf16)
        du = lax.dot_general(gm, b_ref[...], _DIMS["nt"], preferred_element_type=f32) + d_ref[...] * dyb
        du_ref[...] = du.astype(bf16)
        db_ref[...] += lax.dot_general(ub16, gm, _DIMS["tn"], preferred_element_type=f32)
        dc_ref[...] += lax.dot_general(sm, dy16, _DIMS["tn"], preferred_element_type=f32)
        dd_ref[...] += _colsum(dyb * ub)

    rc = lambda j, c: (NC - 1 - c, j)
    outs, extra = _host_call(
        body, name="s5_bwd", grid=(J, NC), rider=rider, args=(u, dy, cin, bmat, cmat, drow, arow, tab),
        in_specs=[pl.BlockSpec((Tc, LANES), rc), pl.BlockSpec((Tc, LANES), rc),
                  pl.BlockSpec((None, None, 1, W), lambda j, c: (j, NC - 1 - c, 0, 0)),
                  pl.BlockSpec((None, LANES, W), lambda j, c: (j, 0, 0)),
                  pl.BlockSpec((None, W, LANES), lambda j, c: (j, 0, 0)),
                  pl.BlockSpec((1, LANES), lambda j, c: (0, j)),
                  pl.BlockSpec((None, 1, W), lambda j, c: (j, 0, 0)),
                  pl.BlockSpec((None, 16, W), lambda j, c: (j, 0, 0))],
        out_specs=[pl.BlockSpec((Tc, LANES), rc),
                   pl.BlockSpec((None, LANES, W), lambda j, c: (j, 0, 0)),
                   pl.BlockSpec((None, W, LANES), lambda j, c: (j, 0, 0)),
                   pl.BlockSpec((None, 1, W), lambda j, c: (j, 0, 0)),
                   pl.BlockSpec((1, LANES), lambda j, c: (0, j))],
        out_shape=[jax.ShapeDtypeStruct((L, D), bf16), jax.ShapeDtypeStruct((J, LANES, W), f32),
                   jax.ShapeDtypeStruct((J, W, LANES), f32), jax.ShapeDtypeStruct((J, 1, W), f32),
                   jax.ShapeDtypeStruct((1, D), f32)],
        scratch_shapes=[pltpu.VMEM((NSLAB, Tc + PAD, LANES), f32), pltpu.VMEM((NSLAB, Tc, LANES), f32),
                        pltpu.VMEM((1, W), f32)])
    return (*outs, extra)


NEG = -1e30


def _tri_tables(nq, by_key):
    pairs = [(qi, ki) for ki in range(nq) for qi in range(ki, nq)] if by_key else \
            [(qi, ki) for qi in range(nq) for ki in range(qi + 1)]
    return jnp.array([p[0] for p in pairs], jnp.int32), jnp.array([p[1] for p in pairs], jnp.int32)


def _tri_call(body, name, cfg, by_key, in_specs, out_specs, out_shape, scratch_shapes, args, rider=None):
    nq = cfg.L // cfg.BQ
    outs, extra = _host_call(body, name=name, grid=(cfg.NH, nq * (nq + 1) // 2), in_specs=in_specs, out_specs=out_specs,
                             out_shape=out_shape, scratch_shapes=scratch_shapes, args=args,
                             prefetch=_tri_tables(nq, by_key), rider=rider)
    return (*outs, extra)


def _ta_fwd(q, kv, fk, cfg, rider=None):
    L, D, NH, DH, B = cfg.L, cfg.D, cfg.NH, cfg.DH, cfg.BQ
    scale = DH ** -0.5

    def body(qt_ref, kt_ref, q_ref, k_ref, v_ref, fk_ref, o_ref, lse_ref, m_ref, acc_ref, a_ref, s_ref, p_ref):
        pid = pl.program_id(1)
        qi, ki = qt_ref[pid], kt_ref[pid]

        @pl.when(ki == 0)
        def _():
            m_ref[...] = jnp.full_like(m_ref, NEG)
            acc_ref[...] = jnp.zeros_like(acc_ref)

        def compute(masked):
            s_ref[...] = lax.dot_general(q_ref[...], k_ref[...], _DIMS["nt"], preferred_element_type=f32)
            fkv = fk_ref[...]

            def strip(rows, row0, c):
                t = s_ref[rows, :] - fkv
                if masked:
                    t = jnp.where(_fa_mask(row0, t.shape), t, NEG)
                m_prev = m_ref[rows, :]
                m_new = jnp.maximum(m_prev, jnp.max(t, axis=1, keepdims=True))
                m_ref[rows, :] = m_new
                a_ref[rows, :] = jnp.exp(m_prev - m_new)
                p_ref[rows, :] = jnp.exp(t - m_new).astype(bf16)
                return c
            _fa_strips(B, strip, 0)
            v1 = jnp.concatenate([v_ref[...], jnp.ones((B, DH), bf16)], axis=1)
            acc_ref[...] = a_ref[...] * acc_ref[...] + jnp.dot(p_ref[...], v1, preferred_element_type=f32)

        @pl.when(ki < qi)
        def _():
            compute(False)

        @pl.when(ki == qi)
        def _():
            compute(True)
            l = acc_ref[:, DH:]
            o_ref[...] = (acc_ref[:, :DH] / l).astype(o_ref.dtype)
            lse_ref[...] = m_ref[...] + jnp.log(l[:, :1])

    col = pltpu.VMEM((B, 1), f32)
    return _tri_call(
        body, "attn_fwd", cfg, False,
        [pl.BlockSpec((B, DH), lambda h, p, qt, kt: (qt[p], h)),
         pl.BlockSpec((B, DH), lambda h, p, qt, kt: (kt[p], h)),
         pl.BlockSpec((B, DH), lambda h, p, qt, kt: (kt[p], NH + h)),
         pl.BlockSpec((None, 1, B), lambda h, p, qt, kt: (h, 0, kt[p]))],
        [pl.BlockSpec((B, DH), lambda h, p, qt, kt: (qt[p], h)),
         pl.BlockSpec((None, B, 1), lambda h, p, qt, kt: (h, qt[p], 0))],
        [jax.ShapeDtypeStruct((L, D), bf16), jax.ShapeDtypeStruct((NH, L, 1), f32)],
        [col, pltpu.VMEM((B, 2 * DH), f32), col, pltpu.VMEM((B, B), f32), pltpu.VMEM((B, B), bf16)],
        (q, kv, kv, fk), rider)


def _ta_bwd_dq(q, kv, do, o, lse, fk, cfg, rider=None):
    L, D, NH, DH, B = cfg.L, cfg.D, cfg.NH, cfg.DH, cfg.BQ
    scale = DH ** -0.5

    def body(qt_ref, kt_ref, q_ref, k_ref, v_ref, do_ref, o_ref, lse_ref, fk_ref, dq_ref, dfq_ref, dl_ref,
             acc_ref, s_ref, dp_ref, ds_ref):
        pid = pl.program_id(1)
        qi, ki = qt_ref[pid], kt_ref[pid]

        @pl.when(ki == 0)
        def _():
            dl_ref[...] = jnp.sum(do_ref[...].astype(f32) * o_ref[...].astype(f32), axis=1, keepdims=True)
            acc_ref[...] = jnp.zeros_like(acc_ref)

        def compute(masked):
            s_ref[...] = lax.dot_general(q_ref[...], k_ref[...], _DIMS["nt"], preferred_element_type=f32)
            dp_ref[...] = lax.dot_general(do_ref[...], v_ref[...], _DIMS["nt"], preferred_element_type=f32)
            fkv = fk_ref[...]

            def strip(rows, row0, c):
                p = jnp.exp(s_ref[rows, :] - fkv - lse_ref[rows, :])
                if masked:
                    p = jnp.where(_fa_mask(row0, p.shape), p, 0.0)
                ds_ref[rows, :] = (p * (dp_ref[rows, :] - dl_ref[rows, :])).astype(bf16)
                return c
            _fa_strips(B, strip, 0)
            k1 = jnp.concatenate([k_ref[...], jnp.ones((B, DH), bf16)], axis=1)
            acc_ref[...] += jnp.dot(ds_ref[...], k1, preferred_element_type=f32)

        @pl.when(ki < qi)
        def _():
            compute(False)

        @pl.when(ki == qi)
        def _():
            compute(True)
            dq_ref[...] = (acc_ref[:, :DH] * scale).astype(dq_ref.dtype)
            dfq_ref[...] = acc_ref[:, DH:DH + 1]

    qmap = lambda h, p, qt, kt: (qt[p], h)
    cmap = lambda h, p, qt, kt: (h, qt[p], 0)
    return _tri_call(
        body, "attn_bwd_dq", cfg, False,
        [pl.BlockSpec((B, DH), qmap),
         pl.BlockSpec((B, DH), lambda h, p, qt, kt: (kt[p], h)),
         pl.BlockSpec((B, DH), lambda h, p, qt, kt: (kt[p], NH + h)),
         pl.BlockSpec((B, DH), qmap), pl.BlockSpec((B, DH), qmap),
         pl.BlockSpec((None, B, 1), cmap),
         pl.BlockSpec((None, 1, B), lambda h, p, qt, kt: (h, 0, kt[p]))],
        [pl.BlockSpec((B, DH), qmap), pl.BlockSpec((None, B, 1), cmap), pl.BlockSpec((None, B, 1), cmap)],
        [jax.ShapeDtypeStruct((L, D), bf16), jax.ShapeDtypeStruct((NH, L, 1), f32), jax.ShapeDtypeStruct((NH, L, 1), f32)],
        [pltpu.VMEM((B, 2 * DH), f32), pltpu.VMEM((B, B), f32), pltpu.VMEM((B, B), f32), pltpu.VMEM((B, B), bf16)],
        (q, kv, kv, do, o, lse, fk), rider)


def _ta_bwd_dkv(q, kv, do, delta, lse, fk, cfg, rider=None):
    L, D, NH, DH, B = cfg.L, cfg.D, cfg.NH, cfg.DH, cfg.BQ
    nq = L // B
    scale = DH ** -0.5

    def body(qt_ref, kt_ref, q_ref, k_ref, v_ref, do_ref, dl_ref, lse_ref, fk_ref, dk_ref, dv_ref, dfk_ref,
             dka_ref, dva_ref, s_ref, dp_ref, p_ref, ds_ref):
        pid = pl.program_id(1)
        qi, ki = qt_ref[pid], kt_ref[pid]

        @pl.when(qi == ki)
        def _():
            dka_ref[...] = jnp.zeros_like(dka_ref)
            dva_ref[...] = jnp.zeros_like(dva_ref)

        def compute(masked):
            s_ref[...] = lax.dot_general(q_ref[...], k_ref[...], _DIMS["nt"], preferred_element_type=f32)
            dp_ref[...] = lax.dot_general(do_ref[...], v_ref[...], _DIMS["nt"], preferred_element_type=f32)
            fkv = fk_ref[...]

            def strip(rows, row0, c):
                p = jnp.exp(s_ref[rows, :] - fkv - lse_ref[rows, :])
                if masked:
                    p = jnp.where(_fa_mask(row0, p.shape), p, 0.0)
                p_ref[rows, :] = p.astype(bf16)
                ds_ref[rows, :] = (p * (dp_ref[rows, :] - dl_ref[rows, :])).astype(bf16)
                return c
            _fa_strips(B, strip, 0)
            q1 = jnp.concatenate([q_ref[...], jnp.ones((B, DH), bf16)], axis=1)
            dva_ref[...] += lax.dot_general(p_ref[...], do_ref[...], _DIMS["tn"], preferred_element_type=f32)
            dka_ref[...] += lax.dot_general(ds_ref[...], q1, _DIMS["tn"], preferred_element_type=f32)

        @pl.when(qi == ki)
        def _():
            compute(True)

        @pl.when(qi > ki)
        def _():
            compute(False)

        @pl.when(qi == nq - 1)
        def _():
            dk_ref[...] = dka_ref[:, :DH].astype(dk_ref.dtype)
            dv_ref[...] = dva_ref[...].astype(dv_ref.dtype)
            dfk_ref[...] = -dka_ref[:, DH:DH + 1]

    qmap = lambda h, p, qt, kt: (qt[p], h)
    cmap = lambda h, p, qt, kt: (h, qt[p], 0)
    kmap = lambda h, p, qt, kt: (kt[p], h)
    return _tri_call(
        body, "attn_bwd_dkv", cfg, True,
        [pl.BlockSpec((B, DH), qmap), pl.BlockSpec((B, DH), kmap),
         pl.BlockSpec((B, DH), lambda h, p, qt, kt: (kt[p], NH + h)),
         pl.BlockSpec((B, DH), qmap), pl.BlockSpec((None, B, 1), cmap), pl.BlockSpec((None, B, 1), cmap),
         pl.BlockSpec((None, 1, B), lambda h, p, qt, kt: (h, 0, kt[p]))],
        [pl.BlockSpec((B, DH), kmap), pl.BlockSpec((B, DH), kmap),
         pl.BlockSpec((None, B, 1), lambda h, p, qt, kt: (h, kt[p], 0))],
        [jax.ShapeDtypeStruct((L, D), bf16), jax.ShapeDtypeStruct((L, D), bf16), jax.ShapeDtypeStruct((NH, L, 1), f32)],
        [pltpu.VMEM((B, 2 * DH), f32), pltpu.VMEM((B, DH), f32), pltpu.VMEM((B, B), f32), pltpu.VMEM((B, B), f32),
         pltpu.VMEM((B, B), bf16), pltpu.VMEM((B, B), bf16)],
        (q, kv, kv, do, delta, lse, fk), rider)


STRIP = 32


def _fa_strips(nrows, fn, init):
    return lax.fori_loop(0, nrows // STRIP, lambda r, c: fn(pl.ds(pl.multiple_of(r * STRIP, STRIP), STRIP), r * STRIP, c),
                         init, unroll=True)


def _fa_mask(row0, shape):
    rows = row0 + lax.broadcasted_iota(jnp.int32, shape, 0)
    cols = lax.broadcasted_iota(jnp.int32, shape, 1)
    return cols <= rows


def _fa_fwd(q, kv, fk, cfg):
    L, D, NH, DH, B = cfg.L, cfg.D, cfg.NH, cfg.DH, cfg.BQ
    nq = L // B
    scale = DH ** -0.5

    def body(q_ref, k_ref, v_ref, fk_ref, o_ref, lse_ref, m_ref, l_ref, acc_ref, a_ref, s_ref, p_ref):
        qi, ki = pl.program_id(1), pl.program_id(2)

        @pl.when(ki == 0)
        def _():
            m_ref[...] = jnp.full_like(m_ref, NEG)
            l_ref[...] = jnp.zeros_like(l_ref)
            acc_ref[...] = jnp.zeros_like(acc_ref)

        def compute(masked):
            s_ref[...] = lax.dot_general(q_ref[...], k_ref[...], _DIMS["nt"], preferred_element_type=f32)
            fkv = fk_ref[...]

            def strip(rows, row0, c):
                t = s_ref[rows, :] - fkv
                if masked:
                    t = jnp.where(_fa_mask(row0, t.shape), t, NEG)
                m_prev = m_ref[rows, :]
                m_new = jnp.maximum(m_prev, jnp.max(t, axis=1, keepdims=True))
                p = jnp.exp(t - m_new)
                alpha = jnp.exp(m_prev - m_new)
                l_ref[rows, :] = alpha * l_ref[rows, :] + jnp.sum(p, axis=1, keepdims=True)
                m_ref[rows, :] = m_new
                a_ref[rows, :] = alpha
                p_ref[rows, :] = p.astype(bf16)
                return c
            _fa_strips(B, strip, 0)
            acc_ref[...] = a_ref[...] * acc_ref[...] + jnp.dot(p_ref[...], v_ref[...], preferred_element_type=f32)

        @pl.when(ki < qi)
        def _():
            compute(False)

        @pl.when(ki == qi)
        def _():
            compute(True)
            o_ref[...] = (acc_ref[...] / l_ref[...]).astype(o_ref.dtype)
            lse_ref[...] = m_ref[...] + jnp.log(l_ref[...])

    col = pltpu.VMEM((B, 1), f32)
    return pl.pallas_call(
        body, name="attn_fwd", grid=(NH, nq, nq),
        in_specs=[pl.BlockSpec((B, DH), lambda h, qi, ki: (qi, h)),
                  pl.BlockSpec((B, DH), lambda h, qi, ki: (jnp.minimum(ki, qi), h)),
                  pl.BlockSpec((B, DH), lambda h, qi, ki: (jnp.minimum(ki, qi), NH + h)),
                  pl.BlockSpec((None, 1, B), lambda h, qi, ki: (h, 0, jnp.minimum(ki, qi)))],
        out_specs=[pl.BlockSpec((B, DH), lambda h, qi, ki: (qi, h)),
                   pl.BlockSpec((None, B, 1), lambda h, qi, ki: (h, qi, 0))],
        out_shape=[jax.ShapeDtypeStruct((L, D), bf16), jax.ShapeDtypeStruct((NH, L, 1), f32)],
        scratch_shapes=[col, col, pltpu.VMEM((B, DH), f32), col, pltpu.VMEM((B, B), f32), pltpu.VMEM((B, B), bf16)],
        compiler_params=_cp(("parallel", "parallel", "arbitrary")),
    )(q, kv, kv, fk)


def _fa_bwd_dq(q, kv, do, o, lse, fk, cfg):
    L, D, NH, DH, B = cfg.L, cfg.D, cfg.NH, cfg.DH, cfg.BQ
    nq = L // B
    scale = DH ** -0.5

    def body(q_ref, k_ref, v_ref, do_ref, o_ref, lse_ref, fk_ref, dq_ref, dfq_ref, dl_ref, acc_ref, df_ref, s_ref, dp_ref, ds_ref):
        qi, ki = pl.program_id(1), pl.program_id(2)

        @pl.when(ki == 0)
        def _():
            dl_ref[...] = jnp.sum(do_ref[...].astype(f32) * o_ref[...].astype(f32), axis=1, keepdims=True)
            acc_ref[...] = jnp.zeros_like(acc_ref)
            df_ref[...] = jnp.zeros_like(df_ref)

        def compute(masked):
            s_ref[...] = lax.dot_general(q_ref[...], k_ref[...], _DIMS["nt"], preferred_element_type=f32)
            dp_ref[...] = lax.dot_general(do_ref[...], v_ref[...], _DIMS["nt"], preferred_element_type=f32)
            fkv = fk_ref[...]

            def strip(rows, row0, c):
                p = jnp.exp(s_ref[rows, :] - fkv - lse_ref[rows, :])
                if masked:
                    p = jnp.where(_fa_mask(row0, p.shape), p, 0.0)
                ds = p * (dp_ref[rows, :] - dl_ref[rows, :])
                df_ref[rows, :] += jnp.sum(ds, axis=1, keepdims=True)
                ds_ref[rows, :] = ds.astype(bf16)
                return c
            _fa_strips(B, strip, 0)
            acc_ref[...] += jnp.dot(ds_ref[...], k_ref[...], preferred_element_type=f32)

        @pl.when(ki < qi)
        def _():
            compute(False)

        @pl.when(ki == qi)
        def _():
            compute(True)
            dq_ref[...] = (acc_ref[...] * scale).astype(dq_ref.dtype)
            dfq_ref[...] = df_ref[...]

    qmap = lambda h, qi, ki: (qi, h)
    cmap = lambda h, qi, ki: (h, qi, 0)
    return pl.pallas_call(
        body, name="attn_bwd_dq", grid=(NH, nq, nq),
        in_specs=[pl.BlockSpec((B, DH), qmap),
                  pl.BlockSpec((B, DH), lambda h, qi, ki: (jnp.minimum(ki, qi), h)),
                  pl.BlockSpec((B, DH), lambda h, qi, ki: (jnp.minimum(ki, qi), NH + h)),
                  pl.BlockSpec((B, DH), qmap), pl.BlockSpec((B, DH), qmap),
                  pl.BlockSpec((None, B, 1), cmap),
                  pl.BlockSpec((None, 1, B), lambda h, qi, ki: (h, 0, jnp.minimum(ki, qi)))],
        out_specs=[pl.BlockSpec((B, DH), qmap), pl.BlockSpec((None, B, 1), cmap), pl.BlockSpec((None, B, 1), cmap)],
        out_shape=[jax.ShapeDtypeStruct((L, D), bf16), jax.ShapeDtypeStruct((NH, L, 1), f32),
                   jax.ShapeDtypeStruct((NH, L, 1), f32)],
        scratch_shapes=[pltpu.VMEM((B, DH), f32), pltpu.VMEM((B, 1), f32), pltpu.VMEM((B, B), f32),
                        pltpu.VMEM((B, B), f32), pltpu.VMEM((B, B), bf16)],
        compiler_params=_cp(("parallel", "parallel", "arbitrary")),
    )(q, kv, kv, do, o, lse, fk)


def _fa_bwd_dkv(q, kv, do, delta, lse, fk, cfg):
    L, D, NH, DH, B = cfg.L, cfg.D, cfg.NH, cfg.DH, cfg.BQ
    nq = L // B
    scale = DH ** -0.5

    def body(q_ref, k_ref, v_ref, do_ref, dl_ref, lse_ref, fk_ref, dk_ref, dv_ref, dfk_ref,
             dka_ref, dva_ref, dfa_ref, s_ref, dp_ref, p_ref, ds_ref):
        ki, qi = pl.program_id(1), pl.program_id(2)

        @pl.when(qi == 0)
        def _():
            dka_ref[...] = jnp.zeros_like(dka_ref)
            dva_ref[...] = jnp.zeros_like(dva_ref)
            dfa_ref[...] = jnp.zeros_like(dfa_ref)

        def compute(masked):
            s_ref[...] = lax.dot_general(q_ref[...], k_ref[...], _DIMS["nt"], preferred_element_type=f32)
            dp_ref[...] = lax.dot_general(do_ref[...], v_ref[...], _DIMS["nt"], preferred_element_type=f32)
            fkv = fk_ref[...]

            def strip(rows, row0, cs):
                p = jnp.exp(s_ref[rows, :] - fkv - lse_ref[rows, :])
                if masked:
                    p = jnp.where(_fa_mask(row0, p.shape), p, 0.0)
                ds = p * (dp_ref[rows, :] - dl_ref[rows, :])
                p_ref[rows, :] = p.astype(bf16)
                ds_ref[rows, :] = ds.astype(bf16)
                return cs + ds
            cs = _fa_strips(B, strip, jnp.zeros((STRIP, B), f32))
            dva_ref[...] += lax.dot_general(p_ref[...], do_ref[...], _DIMS["tn"], preferred_element_type=f32)
            dka_ref[...] += lax.dot_general(ds_ref[...], q_ref[...], _DIMS["tn"], preferred_element_type=f32)
            dfa_ref[...] -= jnp.sum(cs, axis=0, keepdims=True)

        @pl.when(qi == ki)
        def _():
            compute(True)

        @pl.when(qi > ki)
        def _():
            compute(False)

        @pl.when(qi == nq - 1)
        def _():
            dk_ref[...] = (dka_ref[...] * scale).astype(dk_ref.dtype)
            dv_ref[...] = dva_ref[...].astype(dv_ref.dtype)
            dfk_ref[...] = dfa_ref[...]

    qmap = lambda h, ki, qi: (jnp.maximum(qi, ki), h)
    cmap = lambda h, ki, qi: (h, jnp.maximum(qi, ki), 0)
    return pl.pallas_call(
        body, name="attn_bwd_dkv", grid=(NH, nq, nq),
        in_specs=[pl.BlockSpec((B, DH), qmap),
                  pl.BlockSpec((B, DH), lambda h, ki, qi: (ki, h)),
                  pl.BlockSpec((B, DH), lambda h, ki, qi: (ki, NH + h)),
                  pl.BlockSpec((B, DH), qmap),
                  pl.BlockSpec((None, B, 1), cmap), pl.BlockSpec((None, B, 1), cmap),
                  pl.BlockSpec((None, 1, B), lambda h, ki, qi: (h, 0, ki))],
        out_specs=[pl.BlockSpec((B, DH), lambda h, ki, qi: (ki, h)), pl.BlockSpec((B, DH), lambda h, ki, qi: (ki, h)),
                   pl.BlockSpec((None, 1, B), lambda h, ki, qi: (h, 0, ki))],
        out_shape=[jax.ShapeDtypeStruct((L, D), bf16), jax.ShapeDtypeStruct((L, D), bf16),
                   jax.ShapeDtypeStruct((NH, 1, L), f32)],
        scratch_shapes=[pltpu.VMEM((B, DH), f32), pltpu.VMEM((B, DH), f32), pltpu.VMEM((1, B), f32),
                        pltpu.VMEM((B, B), f32), pltpu.VMEM((B, B), f32), pltpu.VMEM((B, B), bf16), pltpu.VMEM((B, B), bf16)],
        compiler_params=_cp(("parallel", "parallel", "arbitrary")),
    )(q, kv, kv, do, delta, lse, fk)


def _fox_logits(q, k, fqv, fkv, scale, masked):
    s = lax.dot_general(q, k, _DIMS["nt"], preferred_element_type=f32) * scale + fqv - fkv
    if masked:
        rows = lax.broadcasted_iota(jnp.int32, s.shape, 0)
        cols = lax.broadcasted_iota(jnp.int32, s.shape, 1)
        return s, cols <= rows
    return s, None


def _fox_fwd(q, kv, fq, fk, cfg):
    L, D, NH, DH, B = cfg.L, cfg.D, cfg.NH, cfg.DH, cfg.BQ
    nq = L // B
    scale = DH ** -0.5

    def body(q_ref, k_ref, v_ref, fq_ref, fk_ref, o_ref, lse_ref):
        qi = pl.program_id(1)
        qv, fqv = q_ref[...], fq_ref[...]

        def chunk(kj, carry, masked):
            m, l, acc = carry
            rows = pl.ds(pl.multiple_of(kj * B, B), B)
            s, mask = _fox_logits(qv, k_ref[rows, :], fqv, fk_ref[kj], scale, masked)
            if masked:
                s = jnp.where(mask, s, NEG)
            m_new = jnp.maximum(m, jnp.max(s, axis=1, keepdims=True))
            alpha = jnp.exp(m - m_new)
            p = jnp.exp(s - m_new)
            l = alpha * l + jnp.sum(p, axis=1, keepdims=True)
            acc = alpha * acc + jnp.dot(p.astype(bf16), v_ref[rows, :], preferred_element_type=f32)
            return m_new, l, acc

        init = (jnp.full((B, 1), NEG, f32), jnp.zeros((B, 1), f32), jnp.zeros((B, DH), f32))
        carry = lax.fori_loop(0, qi, lambda kj, c: chunk(kj, c, False), init)
        m, l, acc = chunk(qi, carry, True)
        o_ref[...] = (acc / l).astype(o_ref.dtype)
        lse_ref[...] = m + jnp.log(l)

    return pl.pallas_call(
        body, name="attn_fwd", grid=(NH, nq),
        in_specs=[pl.BlockSpec((B, DH), lambda h, qi: (qi, h)),
                  pl.BlockSpec((L, DH), lambda h, qi: (0, h)), pl.BlockSpec((L, DH), lambda h, qi: (0, NH + h)),
                  pl.BlockSpec((None, B, 1), lambda h, qi: (h, qi, 0)),
                  pl.BlockSpec((None, nq, 1, B), lambda h, qi: (h, 0, 0, 0))],
        out_specs=[pl.BlockSpec((B, DH), lambda h, qi: (qi, h)), pl.BlockSpec((None, B, 1), lambda h, qi: (h, qi, 0))],
        out_shape=[jax.ShapeDtypeStruct((L, D), bf16), jax.ShapeDtypeStruct((NH, L, 1), f32)],
        compiler_params=_cp(("parallel", "arbitrary")),
    )(q, kv, kv, fq, fk)


def _fox_bwd_dq(q, kv, do, o, lse, fq, fk, cfg):
    L, D, NH, DH, B = cfg.L, cfg.D, cfg.NH, cfg.DH, cfg.BQ
    nq = L // B
    scale = DH ** -0.5

    def body(q_ref, k_ref, v_ref, do_ref, o_ref, lse_ref, fq_ref, fk_ref, dq_ref, dfq_ref, dl_ref):
        qi = pl.program_id(1)
        qv, fqv, dov, lsev = q_ref[...], fq_ref[...], do_ref[...], lse_ref[...]
        delta = jnp.sum(dov.astype(f32) * o_ref[...].astype(f32), axis=1, keepdims=True)

        def chunk(kj, carry, masked):
            acc, df = carry
            rows = pl.ds(pl.multiple_of(kj * B, B), B)
            kv_ = k_ref[rows, :]
            s, mask = _fox_logits(qv, kv_, fqv, fk_ref[kj], scale, masked)
            p = jnp.exp(s - lsev)
            if masked:
                p = jnp.where(mask, p, 0.0)
            dp = lax.dot_general(dov, v_ref[rows, :], _DIMS["nt"], preferred_element_type=f32)
            ds = p * (dp - delta)
            return acc + jnp.dot(ds.astype(bf16), kv_, preferred_element_type=f32), df + jnp.sum(ds, axis=1, keepdims=True)

        carry = lax.fori_loop(0, qi, lambda kj, c: chunk(kj, c, False), (jnp.zeros((B, DH), f32), jnp.zeros((B, 1), f32)))
        acc, df = chunk(qi, carry, True)
        dq_ref[...] = (acc * scale).astype(dq_ref.dtype)
        dfq_ref[...] = df
        dl_ref[...] = delta

    qmap = lambda h, qi: (qi, h)
    cmap = lambda h, qi: (h, qi, 0)
    return pl.pallas_call(
        body, name="attn_bwd_dq", grid=(NH, nq),
        in_specs=[pl.BlockSpec((B, DH), qmap),
                  pl.BlockSpec((L, DH), lambda h, qi: (0, h)), pl.BlockSpec((L, DH), lambda h, qi: (0, NH + h)),
                  pl.BlockSpec((B, DH), qmap), pl.BlockSpec((B, DH), qmap),
                  pl.BlockSpec((None, B, 1), cmap), pl.BlockSpec((None, B, 1), cmap),
                  pl.BlockSpec((None, nq, 1, B), lambda h, qi: (h, 0, 0, 0))],
        out_specs=[pl.BlockSpec((B, DH), qmap), pl.BlockSpec((None, B, 1), cmap), pl.BlockSpec((None, B, 1), cmap)],
        out_shape=[jax.ShapeDtypeStruct((L, D), bf16), jax.ShapeDtypeStruct((NH, L, 1), f32),
                   jax.ShapeDtypeStruct((NH, L, 1), f32)],
        compiler_params=_cp(("parallel", "arbitrary")),
    )(q, kv, kv, do, o, lse, fq, fk)


def _fox_bwd_dkv(q, kv, do, delta, lse, fq, fk, cfg):
    L, D, NH, DH, B = cfg.L, cfg.D, cfg.NH, cfg.DH, cfg.BQ
    nq = L // B
    scale = DH ** -0.5

    def body(q_ref, k_ref, v_ref, do_ref, dl_ref, lse_ref, fq_ref, fk_ref, dk_ref, dv_ref, dfk_ref):
        ki = pl.program_id(1)
        kv_, vv, fkv = k_ref[...], v_ref[...], fk_ref[...]

        def block(qj, carry, masked):
            dk, dv, df = carry
            rows = pl.ds(pl.multiple_of(qj * B, B), B)
            qv, dov = q_ref[rows, :], do_ref[rows, :]
            s, mask = _fox_logits(qv, kv_, fq_ref[rows, :], fkv, scale, masked)
            p = jnp.exp(s - lse_ref[rows, :])
            if masked:
                p = jnp.where(mask, p, 0.0)
            dv = dv + lax.dot_general(p.astype(bf16), dov, _DIMS["tn"], preferred_element_type=f32)
            dp = lax.dot_general(dov, vv, _DIMS["nt"], preferred_element_type=f32)
            ds = p * (dp - dl_ref[rows, :])
            dk = dk + lax.dot_general(ds.astype(bf16), qv, _DIMS["tn"], preferred_element_type=f32)
            return dk, dv, df - jnp.sum(ds, axis=0, keepdims=True)

        init = (jnp.zeros((B, DH), f32), jnp.zeros((B, DH), f32), jnp.zeros((1, B), f32))
        carry = block(ki, init, True)
        dk, dv, df = lax.fori_loop(ki + 1, nq, lambda qj, c: block(qj, c, False), carry)
        dk_ref[...] = (dk * scale).astype(dk_ref.dtype)
        dv_ref[...] = dv.astype(dv_ref.dtype)
        dfk_ref[...] = df

    whole = lambda h, ki: (0, h)
    col = lambda h, ki: (h, 0, 0)
    return pl.pallas_call(
        body, name="attn_bwd_dkv", grid=(NH, nq),
        in_specs=[pl.BlockSpec((L, DH), whole),
                  pl.BlockSpec((B, DH), lambda h, ki: (ki, h)), pl.BlockSpec((B, DH), lambda h, ki: (ki, NH + h)),
                  pl.BlockSpec((L, DH), whole),
                  pl.BlockSpec((None, L, 1), col), pl.BlockSpec((None, L, 1), col), pl.BlockSpec((None, L, 1), col),
                  pl.BlockSpec((None, None, 1, B), lambda h, ki: (h, ki, 0, 0))],
        out_specs=[pl.BlockSpec((B, DH), lambda h, ki: (ki, h)), pl.BlockSpec((B, DH), lambda h, ki: (ki, h)),
                   pl.BlockSpec((None, None, 1, B), lambda h, ki: (h, ki, 0, 0))],
        out_shape=[jax.ShapeDtypeStruct((L, D), bf16), jax.ShapeDtypeStruct((L, D), bf16),
                   jax.ShapeDtypeStruct((NH, nq, 1, B), f32)],
        compiler_params=_cp(("parallel", "arbitrary")),
    )(q, kv, kv, do, delta, lse, fq, fk)


FCH = 256


def _split3(x):
    hi = x.astype(bf16)
    r1 = x - hi.astype(f32)
    mid = r1.astype(bf16)
    lo = (r1 - mid.astype(f32)).astype(bf16)
    return hi, mid, lo


def _tri_sum(tri, x):
    hi, mid, lo = _split3(x)
    return (jnp.dot(tri, hi, preferred_element_type=f32) + jnp.dot(tri, mid, preferred_element_type=f32)
            + jnp.dot(tri, lo, preferred_element_type=f32))


def _fgate_fwd(z, fb, cfg):
    L = cfg.L

    def body(z_ref, fb_ref, f_ref):
        r = lax.broadcasted_iota(jnp.int32, (FCH, FCH), 0)
        c = lax.broadcasted_iota(jnp.int32, (FCH, FCH), 1)
        tri = (c <= r).astype(bf16)
        carry = jnp.zeros((1, LANES), f32)
        for ch in range(L // FCH):
            x = z_ref[pl.ds(ch * FCH, FCH), :] + fb_ref[...]
            lf = jnp.minimum(x, 0.0) - jnp.log(1.0 + jnp.exp(-jnp.abs(x)))
            f_ref[pl.ds(ch * FCH, FCH), :] = _tri_sum(tri, lf) + carry
            carry = f_ref[pl.ds(ch * FCH + FCH - 1, 1), :]

    vm = pl.BlockSpec(memory_space=pltpu.VMEM)
    return pl.pallas_call(body, name="fgate_fwd", in_specs=[vm, vm], out_specs=vm,
                          out_shape=jax.ShapeDtypeStruct((L, LANES), f32), compiler_params=_cp())(z, fb)


def _fgate_bwd(df, z, fb, cfg):
    L = cfg.L

    def body(df_ref, z_ref, fb_ref, dz_ref, db_ref):
        r = lax.broadcasted_iota(jnp.int32, (FCH, FCH), 0)
        c = lax.broadcasted_iota(jnp.int32, (FCH, FCH), 1)
        tri = (c >= r).astype(bf16)
        carry = jnp.zeros((1, LANES), f32)
        dbs = jnp.zeros((1, LANES), f32)
        for ch in range(L // FCH - 1, -1, -1):
            suf = _tri_sum(tri, df_ref[pl.ds(ch * FCH, FCH), :]) + carry
            x = z_ref[pl.ds(ch * FCH, FCH), :] + fb_ref[...]
            dz = suf * _sigmoid(-x)
            dz_ref[pl.ds(ch * FCH, FCH), :] = dz
            dbs = dbs + _colsum(dz)
            carry = carry + _colsum(df_ref[pl.ds(ch * FCH, FCH), :])
        db_ref[...] = dbs

    vm = pl.BlockSpec(memory_space=pltpu.VMEM)
    return pl.pallas_call(body, name="fgate_bwd", in_specs=[vm, vm, vm], out_specs=[vm, vm],
                          out_shape=[jax.ShapeDtypeStruct((L, LANES), f32), jax.ShapeDtypeStruct((1, LANES), f32)],
                          compiler_params=_cp())(df, z, fb)


def _adamw(w, g, m, v, name):
    R, C = w.shape
    c1 = 1.0 - ADAM_B1 ** ADAM_STEP
    c2 = 1.0 - ADAM_B2 ** ADAM_STEP

    def fn(i, ni, wv, gv, mv, vv):
        mn = ADAM_B1 * mv + (1.0 - ADAM_B1) * gv
        vn = ADAM_B2 * vv + (1.0 - ADAM_B2) * (gv * gv)
        delta = -ADAM_LR * ((mn / c1) / (jnp.sqrt(vn / c2) + ADAM_EPS) + ADAM_WD * wv)
        return [delta, mn, vn], []
    tc = C if C % LANES else _tile(C, 1024)
    return _rowwise(fn, [(w, "rc"), (g, "rc"), (m, "rc"), (v, "rc")], [f32, f32, f32], [], L=R, C=C, tl=512, tc=tc, name=name)


def _sum_lead(x, out_dtype, name):
    n, R, C = x.shape
    tl = _tile(R, 512, HALO)
    tc = C if C % LANES else _tile(C, 1024)

    def body(x_ref, o_ref):
        acc = x_ref[0].astype(f32)
        for k in range(1, n):
            acc = acc + x_ref[k].astype(f32)
        o_ref[...] = acc.astype(o_ref.dtype)

    return pl.pallas_call(
        body, name=name, grid=(R // tl, C // tc),
        in_specs=[pl.BlockSpec((n, tl, tc), lambda i, j: (0, i, j))], out_specs=pl.BlockSpec((tl, tc), lambda i, j: (i, j)),
        out_shape=jax.ShapeDtypeStruct((R, C), out_dtype), compiler_params=_cp(("parallel", "parallel")),
    )(x)


def _add_own_half(g, got, core, name):
    _, _, R, C = g.shape
    tl = _tile(R, 512, HALO)
    tc = C if C % LANES else _tile(C, 1024)

    def body(core_ref, g_ref, got_ref, o_ref):
        o_ref[...] = (g_ref[...].astype(f32) + got_ref[...].astype(f32)).astype(o_ref.dtype)

    blk = pl.BlockSpec((None, tl, tc), lambda k, i, j, co: (k, i, j))
    return pl.pallas_call(
        body, name=name,
        grid_spec=pltpu.PrefetchScalarGridSpec(
            num_scalar_prefetch=1, grid=(4, R // tl, C // tc),
            in_specs=[pl.BlockSpec((None, None, tl, tc), lambda k, i, j, co: (k, co[0], i, j)), blk], out_specs=blk),
        out_shape=jax.ShapeDtypeStruct((4, R, C), bf16), compiler_params=_cp(("parallel", "parallel", "parallel")),
    )(core, g, got)


ANY = pl.BlockSpec(memory_space=pl.ANY)
LOCAL_CHUNKS = 4


def _place():
    x, y, c = lax.axis_index("x"), lax.axis_index("y"), lax.axis_index("c")
    return x, y, c


def _allgather8(blocks, name):
    return _run_rider(_gather_rider(blocks), name)


def _gather_rider(blocks, middle_at=(1, 2)):
    n = len(blocks)

    def steps(ins, outs, sems):
        send_sems, recv_sems, local_sems = sems
        x, y, c = _place()
        me, sibling = (x, y, c), (x, y, 1 - c)
        chips = [(1 - x, y), (x, 1 - y), (1 - x, 1 - y)]

        def slot(a, dev):
            return outs[a].at[4 * dev[0] + 2 * dev[1] + dev[2]]

        def copy(a, k, block, to, src=None):
            return pltpu.make_async_remote_copy(
                src_ref=slot(a, block) if src is None else src, dst_ref=slot(a, block),
                send_sem=send_sems.at[a * 7 + k], recv_sem=recv_sems.at[a * 7 + k], device_id=to, device_id_type=MESH)

        def mine():
            out = []
            for a in range(n):
                rows = blocks[a].shape[0]
                k = LOCAL_CHUNKS if rows % (LOCAL_CHUNKS * HALO) == 0 else 1
                for i in range(k):
                    piece = pl.ds(i * (rows // k), rows // k)
                    out.append(pltpu.make_async_copy(ins[a].at[piece], slot(a, me).at[piece], local_sems.at[a * LOCAL_CHUNKS + i]))
            return out

        def first():
            out = []
            for a in range(n):
                out.append(copy(a, 0, me, sibling, src=ins[a]))
                out += [copy(a, 1 + j, me, (*chip, c), src=ins[a]) for j, chip in enumerate(chips)]
            return out

        def passed():
            return [copy(a, 4 + j, (*chip, c), sibling) for j, chip in enumerate(chips) for a in range(n)]

        def start():
            for cp in mine() + first():
                cp.start()

        def middle():
            for j, chip in enumerate(chips):
                for a in range(n):
                    copy(a, 1 + j, (*chip, c), me).wait_recv()
                    copy(a, 4 + j, (*chip, c), sibling).start()

        def finish():
            for a in range(n):
                copy(a, 0, sibling, me).wait_recv()
            for j, chip in enumerate(chips):
                for a in range(n):
                    copy(a, 4 + j, (*chip, 1 - c), me).wait_recv()
            for cp in first() + passed():
                cp.wait_send()
            for cp in mine():
                cp.wait()
        return start, middle, finish

    return dict(ins=list(blocks), out_shapes=[jax.ShapeDtypeStruct((N_DEV,) + b.shape, b.dtype) for b in blocks],
                sems=[pltpu.SemaphoreType.DMA((7 * n,)), pltpu.SemaphoreType.DMA((7 * n,)),
                      pltpu.SemaphoreType.DMA((LOCAL_CHUNKS * n,))],
                steps=steps, middle_at=middle_at)


def _run_rider(rider, name):
    ni, no = len(rider["ins"]), len(rider["out_shapes"])

    def body(*refs):
        start, middle, finish = rider["steps"](refs[:ni], refs[ni:ni + no], refs[ni + no:])
        start()
        if middle is not None:
            middle()
        finish()

    outs = pl.pallas_call(body, name=name, in_specs=[ANY] * ni, out_specs=[ANY] * no, out_shape=rider["out_shapes"],
                          scratch_shapes=rider["sems"])(*rider["ins"])
    return list(outs)


def _host_call(body, *, name, grid, in_specs, out_specs, out_shape, scratch_shapes, args, prefetch=(), rider=None):
    npre, nin, nout, nscr = len(prefetch), len(in_specs), len(out_specs), len(scratch_shapes)
    r_in, r_out, r_scr = (rider["ins"], rider["out_shapes"], rider["sems"]) if rider else ([], [], [])
    nri, nro = len(r_in), len(r_out)

    def kern(*refs):
        pre, rest = refs[:npre], refs[npre:]
        cin, rin = rest[:nin], rest[nin:nin + nri]
        o0 = nin + nri
        cout, rout = rest[o0:o0 + nout], rest[o0 + nout:o0 + nout + nro]
        s0 = o0 + nout + nro
        cscr, rscr = rest[s0:s0 + nscr], rest[s0 + nscr:]
        if rider:
            ids = [pl.program_id(d) for d in range(len(grid))]
            rest_zero = functools.reduce(jnp.logical_and, [i == 0 for i in ids[1:]], True)
            start, middle, finish = rider["steps"](rin, rout, rscr)
            pl.when(jnp.logical_and(ids[0] == 0, rest_zero))(start)
            if middle is not None:
                num, den = rider.get("middle_at", (1, 2))
                pl.when(jnp.logical_and(ids[0] == grid[0] * num // den, rest_zero))(middle)
        body(*pre, *cin, *cout, *cscr)
        if rider:
            pl.when(functools.reduce(jnp.logical_and, [i == g - 1 for i, g in zip(ids, grid)]))(finish)

    res = pl.pallas_call(
        kern, name=name,
        grid_spec=pltpu.PrefetchScalarGridSpec(num_scalar_prefetch=npre, grid=grid, in_specs=list(in_specs) + [ANY] * nri,
                                               out_specs=list(out_specs) + [ANY] * nro,
                                               scratch_shapes=list(scratch_shapes) + list(r_scr)),
        out_shape=list(out_shape) + list(r_out),
        compiler_params=_cp(("arbitrary",) * len(grid) if rider else ("parallel",) + ("arbitrary",) * (len(grid) - 1)),
    )(*prefetch, *args, *r_in)
    return list(res[:nout]), list(res[nout:])


def _sibling_send(halves, name):
    n = len(halves)

    def body(*refs):
        ins, outs = refs[:n], refs[n:2 * n]
        send_sems, recv_sems = refs[2 * n:]
        x, y, c = _place()
        sends = [pltpu.make_async_remote_copy(src_ref=ins[a], dst_ref=outs[a], send_sem=send_sems.at[a],
                                              recv_sem=recv_sems.at[a], device_id=(x, y, 1 - c), device_id_type=MESH)
                 for a in range(n)]
        for cp in sends:
            cp.start()
        for cp in sends:
            cp.wait_recv()
        for cp in sends:
            cp.wait_send()

    outs = pl.pallas_call(
        body, name=name, in_specs=[ANY] * n, out_specs=[ANY] * n,
        out_shape=[jax.ShapeDtypeStruct(h.shape, h.dtype) for h in halves],
        scratch_shapes=[pltpu.SemaphoreType.DMA((n,)), pltpu.SemaphoreType.DMA((n,))],
    )(*halves)
    return list(outs)


def _sibling_swap_halves(grads, name):
    n = len(grads)

    def body(*refs):
        ins, outs = refs[:n], refs[n:2 * n]
        send_sems, recv_sems = refs[2 * n:]
        x, y, c = _place()
        sends = [pltpu.make_async_remote_copy(src_ref=ins[a].at[:, 1 - c], dst_ref=outs[a], send_sem=send_sems.at[a],
                                              recv_sem=recv_sems.at[a], device_id=(x, y, 1 - c), device_id_type=MESH)
                 for a in range(n)]
        for cp in sends:
            cp.start()
        for cp in sends:
            cp.wait_recv()
        for cp in sends:
            cp.wait_send()

    outs = pl.pallas_call(
        body, name=name, in_specs=[ANY] * n, out_specs=[ANY] * n,
        out_shape=[jax.ShapeDtypeStruct((4,) + g.shape[2:], g.dtype) for g in grads],
        scratch_shapes=[pltpu.SemaphoreType.DMA((n,)), pltpu.SemaphoreType.DMA((n,))],
    )(*grads)
    return list(outs)


def _chip_scatter(parts, name):
    return _run_rider(_scatter_rider(parts), name)


def _scatter_rider(parts):
    n = len(parts)

    def steps(ins, outs, sems):
        send_sems, recv_sems = sems
        x, y, c = _place()
        chips = [(1 - x, y), (x, 1 - y), (1 - x, 1 - y)]

        def sends():
            return [pltpu.make_async_remote_copy(
                src_ref=ins[a].at[2 * px + py], dst_ref=outs[a].at[j], send_sem=send_sems.at[a * 3 + j],
                recv_sem=recv_sems.at[a * 3 + j], device_id=(px, py, c), device_id_type=MESH)
                for a in range(n) for j, (px, py) in enumerate(chips)]

        def start():
            for cp in sends():
                cp.start()

        def finish():
            for cp in sends():
                cp.wait_recv()
            for cp in sends():
                cp.wait_send()
        return start, None, finish

    return dict(ins=list(parts), out_shapes=[jax.ShapeDtypeStruct((3,) + p.shape[1:], p.dtype) for p in parts],
                sems=[pltpu.SemaphoreType.DMA((3 * n,)), pltpu.SemaphoreType.DMA((3 * n,))], steps=steps)


def _sum_parts(own, got, chip, name):
    _, R, C = own.shape
    tl = _tile(R, 512, HALO)
    tc = C if C % LANES else _tile(C, 1024)

    def body(chip_ref, own_ref, got_ref, o_ref):
        acc = own_ref[...].astype(f32)
        for k in range(3):
            acc = acc + got_ref[k].astype(f32)
        o_ref[...] = acc

    return pl.pallas_call(
        body, name=name,
        grid_spec=pltpu.PrefetchScalarGridSpec(
            num_scalar_prefetch=1, grid=(R // tl, C // tc),
            in_specs=[pl.BlockSpec((None, tl, tc), lambda i, j, ch: (ch[0], i, j)),
                      pl.BlockSpec((3, tl, tc), lambda i, j, ch: (0, i, j))],
            out_specs=pl.BlockSpec((tl, tc), lambda i, j, ch: (i, j))),
        out_shape=jax.ShapeDtypeStruct((R, C), f32), compiler_params=_cp(("parallel", "parallel")),
    )(chip, own, got)


def _adamw_halves(w, m, v, g_mine, g_other, core, name, rider=None):
    NL, R, C = w.shape
    r = R // 2
    tl = _tile(r, 512, HALO)
    tc = C if C % LANES else _tile(C, 1024)
    nh = r // tl
    c1 = 1.0 - ADAM_B1 ** ADAM_STEP
    c2 = 1.0 - ADAM_B2 ** ADAM_STEP

    def body(core_ref, w_ref, m_ref, v_ref, gm_ref, go_ref, g_out, d_out, m_out, v_out):
        i = pl.program_id(1)
        mine = lax.div(i, nh) == core_ref[0]
        gv = jnp.where(mine, gm_ref[...], go_ref[...])
        mn = ADAM_B1 * m_ref[...] + (1.0 - ADAM_B1) * gv
        vn = ADAM_B2 * v_ref[...] + (1.0 - ADAM_B2) * (gv * gv)
        g_out[...] = gv
        d_out[...] = -ADAM_LR * ((mn / c1) / (jnp.sqrt(vn / c2) + ADAM_EPS) + ADAM_WD * w_ref[...])
        m_out[...] = mn
        v_out[...] = vn

    full = pl.BlockSpec((None, tl, tc), lambda l, i, j, co: (l, i, j))
    mine_spec = pl.BlockSpec((None, tl, tc), lambda l, i, j, co: (l, jnp.clip(i - co[0] * nh, 0, nh - 1), j))
    other_spec = pl.BlockSpec((None, tl, tc), lambda l, i, j, co: (l, jnp.clip(i - (1 - co[0]) * nh, 0, nh - 1), j))
    outs, extra = _host_call(
        body, name=name, grid=(NL, R // tl, C // tc), in_specs=[full, full, full, mine_spec, other_spec],
        out_specs=[full] * 4, out_shape=[jax.ShapeDtypeStruct((NL, R, C), f32)] * 4, scratch_shapes=[],
        args=(w, m, v, g_mine, g_other), prefetch=(core,), rider=rider)
    return (*outs, extra)


def _s5_discretize(log_step, a_re, a_im, b_re, b_im):
    step = jnp.exp(log_step)[:, None]
    mag = jnp.exp(a_re * step)
    abar_re = mag * jnp.cos(a_im * step)
    abar_im = mag * jnp.sin(a_im * step)
    den = a_re * a_re + a_im * a_im
    nr = abar_re - 1.0
    fr = (nr * a_re + abar_im * a_im) / den
    fi = (abar_im * a_re - nr * a_im) / den
    bbar_re = fr[..., None] * b_re - fi[..., None] * b_im
    bbar_im = fr[..., None] * b_im + fi[..., None] * b_re
    return abar_re, abar_im, bbar_re, bbar_im


def _s5_prepare(p, cfg):
    abar_re, abar_im, bbar_re, bbar_im = _s5_discretize(p["log_step"], p["a_re"], p["a_im"], p["b_re"], p["b_im"])
    step = jnp.exp(p["log_step"])[:, None]
    arow, tab = _s5_tables(abar_re, abar_im, p["a_re"], p["a_im"], step, cfg)
    bmat, cmat = _s5_mats(bbar_re, bbar_im, p["c_re"], p["c_im"], cfg)
    return dict(arow=arow, tab=tab, bmat=bmat, cmat=cmat, drow=p["d"].reshape(1, cfg.D))


def _s5_param_grads(p, dbmat, dcmat, dabar, dd, cfg):
    J, P = cfg.G // 8, cfg.P
    dbb_re, dbb_im, dc_re, dc_im = _s5_unmats(dbmat, dcmat, cfg)
    da = dabar.reshape(J, 2, 8, P)
    da_re, da_im = da[:, 0].reshape(cfg.G, P), da[:, 1].reshape(cfg.G, P)
    _, vjp = jax.vjp(_s5_discretize, p["log_step"], p["a_re"], p["a_im"], p["b_re"], p["b_im"])
    dls, dare, daim, dbre, dbim = vjp((da_re, da_im, dbb_re, dbb_im))
    return dict(log_step=dls, a_re=dare, a_im=daim, b_re=dbre, b_im=dbim, c_re=dc_re, c_im=dc_im, d=dd.reshape(cfg.G, cfg.H))


def _resid_epi(acc, xv, gv):
    return xv + gv * acc, acc


def _ffn_fwd(x_in, g_norm, sc, sh, gate, W, exch, conv_w, conv_b, cfg, tag):
    h = _norm_mod_fwd(x_in, g_norm, sc, sh, cfg, f"ffn_norm_{tag}")
    rider = exch.rider(f"ffn_up_{tag}")
    a = _mm(h, W[f"ffn_w_up{tag}"], mode="nn", b4=True, tn=1408, out_dtypes=(bf16,), name=f"ffn_up_{tag}", rider=rider)
    if rider:
        a, extra = a
        W.update(exch.done(f"ffn_up_{tag}", extra))
    act = _conv_act_fwd(a, conv_w, conv_b, cfg)
    rider = exch.rider(f"ffn_down_{tag}")
    res = _mm(act, W[f"ffn_w_down{tag}"], mode="nn", extras=[(x_in, "mn"), (gate, "n")], epi=_resid_epi,
              out_dtypes=(f32, bf16), name=f"ffn_down_{tag}", rider=rider)
    if rider:
        res, extra = res
        W.update(exch.done(f"ffn_down_{tag}", extra))
    x_out, out = res
    return x_out, dict(h=h, a=a, act=act, out=out)


def _ffn_bwd(dx, x_in, sv, g_norm, sc, gate, w_up4, w_down, conv_w, conv_b, cfg, tag):
    F = cfg.F
    dout, dgate = _gate_bwd(dx, sv["out"], gate, cfg, f"ffn_gate_bwd_{tag}")
    dact = _mm(dout, w_down, mode="nt", tn=1408, out_dtypes=(bf16,), name=f"ffn_dact_{tag}")
    dw_down = _mm(sv["act"], dout, mode="tn", tm=1408, out_dtypes=(bf16,), name=f"ffn_dwdown_{tag}")
    dcu, dcv, dwu, dwv, dbu, dbv = _conv_act_bwd1(dact, sv["a"], conv_w, conv_b, cfg)
    dau = _conv_bwd2(dcu, conv_w[:, :F], cfg, f"conv_bwd2u_{tag}")
    dav = _conv_bwd2(dcv, conv_w[:, F:], cfg, f"conv_bwd2v_{tag}")
    da = jnp.concatenate([dau, dav], axis=1)
    dh = _mm(da, w_up4, mode="nt", b4=True, tk=1408, out_dtypes=(bf16,), name=f"ffn_dh_{tag}")
    dw_up = _mm(sv["h"], da, mode="tn", out4=True, tn=1408, out_dtypes=(bf16,), name=f"ffn_dwup_{tag}")
    dx_in, A, B = _norm_mod_bwd(dh, x_in, g_norm, sc, dx, cfg, f"ffn_norm_bwd_{tag}")
    small = dict(norm_g=(1.0 + sc) * A, sc=g_norm * A, sh=B, gate=dgate,
                 conv_w=jnp.concatenate([dwu, dwv], axis=1), conv_b=jnp.concatenate([dbu, dbv], axis=1))
    return dx_in, dw_up, dw_down, small


class _NoExchange:
    def rider(self, key, grads=None):
        return None

    def done(self, key, extra):
        return {}


def _local_step(cfg, x, tgt, mod, W, sp, exch=None):
    D, NH = cfg.D, cfg.NH
    exch = exch or _NoExchange()
    W, big = dict(W), {}

    def hand_over(key, grads):
        rider = exch.rider(key, grads)
        if rider is None:
            big.update(grads)
        return rider
    row = lambda v: v.reshape(1, -1)
    nmg0, nmg1 = row(sp["norm_mix_g"][0]), row(sp["norm_mix_g"][1])
    nfg0, nfg1 = row(sp["norm_ffn_g"][0]), row(sp["norm_ffn_g"][1])
    kvg, fng = row(sp["kv_norm_g"]), row(sp["final_norm_g"])
    cw0, cw1 = sp["ffn_conv_w"][0], sp["ffn_conv_w"][1]
    cb0, cb1 = row(sp["ffn_conv_b"][0]), row(sp["ffn_conv_b"][1])
    glu_b = row(sp["ssm_glu_b"])
    fb = jnp.zeros((1, LANES), f32).at[0, :NH].set(sp["forget_b"])
    s5p = {k: sp["ssm_" + k][0] for k in ("log_step", "a_re", "a_im", "b_re", "b_im", "c_re", "c_im", "d")}
    s5 = _s5_prepare(s5p, cfg)
    m0, m1 = mod["l0"], mod["l1"]

    h0 = _norm_mod_fwd(x, nmg0, m0["sc_m"], m0["sh_m"], cfg, "mix_norm_0")
    u = _mm(h0, W["ssm_w_in"], mode="nn", name="ssm_in")
    y, gact, cin, extra = _s5_fwd(u, s5["bmat"], s5["cmat"], s5["drow"], s5["arow"], s5["tab"], cfg, exch.rider("s5_fwd"))
    W.update(exch.done("s5_fwd", extra))

    def glu_epi(acc, bv, gv):
        pre = acc + bv
        return pre, gv.astype(f32) * _sigmoid(pre)
    pre, z = _mm(gact, W["ssm_glu_w"], mode="nn", extras=[(glu_b, "n"), (gact, "mn")], epi=glu_epi,
                 out_dtypes=(f32, bf16), name="ssm_glu")
    x1, out_m0 = _mm(z, W["ssm_w_out"], mode="nn", extras=[(x, "mn"), (m0["g_m"], "n")], epi=_resid_epi,
                     out_dtypes=(f32, bf16), name="ssm_out")
    x2, ffn0 = _ffn_fwd(x1, nfg0, m0["sc_f"], m0["sh_f"], m0["g_f"], W, exch, cw0, cb0, cfg, "0")

    hk = _norm_mod_fwd(x2, kvg, mod["sc_kv"], mod["sh_kv"], cfg, "kv_norm")
    kvb = _mm(hk, W["kv_w"], mode="nn", out_dtypes=(bf16,), name="kv_proj")
    zf = _mm(hk, W["kv_wf"], mode="nn", name="kv_fproj")
    fc = _fgate_fwd(zf, fb, cfg)
    fct = fc[:, :NH].T
    fk = fct[:, None, :]

    h1 = _norm_mod_fwd(x2, nmg1, m1["sc_m"], m1["sh_m"], cfg, "mix_norm_1")
    q = _mm(h1, W["attn_w_q"], mode="nn", epi=lambda acc: (acc * cfg.DH ** -0.5,), out_dtypes=(bf16,), name="attn_q")
    o, lse, extra = _ta_fwd(q, kvb, fk, cfg, exch.rider("attn_fwd"))
    W.update(exch.done("attn_fwd", extra))
    x3, out_m1 = _mm(o, W["attn_w_out"], mode="nn", extras=[(x2, "mn"), (m1["g_m"], "n")], epi=_resid_epi,
                     out_dtypes=(f32, bf16), name="attn_out")
    x4, ffn1 = _ffn_fwd(x3, nfg1, m1["sc_f"], m1["sh_f"], m1["g_f"], W, exch, cw1, cb1, cfg, "1")

    dx, dfng, lcol = _final_loss(x4, fng, tgt, cfg)
    loss = (0.5 / D) * jnp.sum(lcol)

    dx, dw_up1, dw_down1, sf1 = _ffn_bwd(dx, x3, ffn1, nfg1, m1["sc_f"], m1["g_f"], W["ffn_w_up1"], W["ffn_w_down1"], cw1, cb1, cfg, "1")
    dout, dgm1 = _gate_bwd(dx, out_m1, m1["g_m"], cfg, "attn_gate_bwd")
    do = _mm(dout, W["attn_w_out"], mode="nt", out_dtypes=(bf16,), name="attn_do")
    dw_ao = _mm(o, dout, mode="tn", out_dtypes=(bf16,), name="attn_dwout")
    dq, dfq, delta, extra = _ta_bwd_dq(q, kvb, do, o, lse, fk, cfg,
                                       hand_over("attn_bwd", dict(ffn_w_up1=dw_up1, ffn_w_down1=dw_down1)))
    exch.done("attn_bwd", extra)
    dw_q = _mm(h1, dq, mode="tn", out_dtypes=(bf16,), name="attn_dwq")
    dk, dv, dfk, extra = _ta_bwd_dkv(q, kvb, do, delta, lse, fk, cfg,
                                     hand_over("attn_bwd_dkv", dict(attn_w_q=dw_q, attn_w_out=dw_ao)))
    exch.done("attn_bwd_dkv", extra)
    dh1 = _mm(dq, W["attn_w_q"], mode="nt", out_dtypes=(bf16,), name="attn_dh")
    dx, A1, B1 = _norm_mod_bwd(dh1, x2, nmg1, m1["sc_m"], dx, cfg, "mix_norm_bwd_1")

    dfc = jnp.pad((dfq[:, :, 0] + dfk[:, :, 0]).T, ((0, 0), (0, LANES - NH)))
    dzf, dfb = _fgate_bwd(dfc, zf, fb, cfg)
    dkv = jnp.concatenate([dk, dv], axis=1)
    dhk1 = _mm(dkv, W["kv_w"], mode="nt", name="kv_dh1")
    dhk = _mm(dzf, W["kv_wf"], mode="nt", extras=[(dhk1, "mn")], epi=lambda acc, e: (acc + e,), out_dtypes=(bf16,), name="kv_dh2")
    dw_kv = _mm(hk, dkv, mode="tn", out_dtypes=(bf16,), name="kv_dw")
    dw_kf = _mm(hk, dzf, mode="tn", out_dtypes=(bf16,), name="kv_dwf")
    dx, Ak, Bk = _norm_mod_bwd(dhk, x2, kvg, mod["sc_kv"], dx, cfg, "kv_norm_bwd")

    dx, dw_up0, dw_down0, sf0 = _ffn_bwd(dx, x1, ffn0, nfg0, m0["sc_f"], m0["g_f"], W["ffn_w_up0"], W["ffn_w_down0"], cw0, cb0, cfg, "0")
    dout, dgm0 = _gate_bwd(dx, out_m0, m0["g_m"], cfg, "ssm_gate_bwd")
    dz = _mm(dout, W["ssm_w_out"], mode="nt", out_dtypes=(bf16,), name="ssm_dz")
    dw_so = _mm(z, dout, mode="tn", out_dtypes=(bf16,), name="ssm_dwout")
    dpre, dgd, dglub = _glu_bwd(dz, gact, pre, cfg)
    dy = _mm(dpre, W["ssm_glu_w"], mode="nt", extras=[(dgd, "mn"), (y, "mn")],
             epi=lambda acc, e, yv: ((acc + e) * _gelu_grad(yv),), name="ssm_dy")
    dw_glu = _mm(gact, dpre, mode="tn", out_dtypes=(bf16,), name="ssm_dwglu")
    rider = hand_over("s5_bwd", dict(kv_w=jnp.concatenate([dw_kv, dw_kf[:, :NH]], axis=1), ffn_w_up0=dw_up0,
                                     ffn_w_down0=dw_down0, ssm_w_out=dw_so, ssm_glu_w=dw_glu))
    du, dbm, dcm, dab, dd, extra = _s5_bwd(u, dy, cin, s5["bmat"], s5["cmat"], s5["drow"], s5["arow"], s5["tab"], cfg, rider)
    exch.done("s5_bwd", extra)
    dh0 = _mm(du, W["ssm_w_in"], mode="nt", out_dtypes=(bf16,), name="ssm_dh")
    dw_in = _mm(h0, du, mode="tn", out_dtypes=(bf16,), name="ssm_dwin")
    dx, A0, B0 = _norm_mod_bwd(dh0, x, nmg0, m0["sc_m"], dx, cfg, "mix_norm_bwd_0")

    s5g = _s5_param_grads(s5p, dbm, dcm, dab, dd, cfg)
    big["ssm_w_in"] = dw_in
    small = dict(
        norm_mix_g=jnp.concatenate([(1.0 + m0["sc_m"]) * A0, (1.0 + m1["sc_m"]) * A1], axis=0),
        norm_ffn_g=jnp.concatenate([sf0["norm_g"], sf1["norm_g"]], axis=0),
        ssm_glu_b=dglub, kv_norm_g=(1.0 + mod["sc_kv"]) * Ak, forget_b=dfb[0, :NH],
        ffn_conv_w=jnp.stack([sf0["conv_w"], sf1["conv_w"]]), ffn_conv_b=jnp.concatenate([sf0["conv_b"], sf1["conv_b"]], axis=0),
        final_norm_g=dfng, **{"ssm_" + k: v[None] for k, v in s5g.items()})
    dmod = [jnp.concatenate([B0, nmg0 * A0, dgm0, sf0["sh"], sf0["sc"], sf0["gate"]], axis=1),
            jnp.concatenate([B1, nmg1 * A1, dgm1, sf1["sh"], sf1["sc"], sf1["gate"]], axis=1),
            jnp.concatenate([Bk, kvg * Ak], axis=1)]
    return loss, dx, big, small, dmod


WEIGHTS = ["mod_w", "mod_b", "norm_mix_g", "norm_ffn_g", "ssm_w_in", "ssm_log_step", "ssm_a_re", "ssm_a_im", "ssm_b_re",
           "ssm_b_im", "ssm_c_re", "ssm_c_im", "ssm_d", "ssm_glu_w", "ssm_glu_b", "ssm_w_out", "kv_mod_w", "kv_mod_b",
           "kv_norm_g", "kv_w", "forget_b", "attn_w_q", "attn_w_out", "ffn_w_up", "ffn_conv_w", "ffn_conv_b", "ffn_w_down",
           "final_norm_g"]
ARGS = ["x", "c"] + WEIGHTS + ["loss_target"] + ["m_" + n for n in WEIGHTS] + ["v_" + n for n in WEIGHTS]
SMALL = ["mod_b", "norm_mix_g", "norm_ffn_g", "ssm_log_step", "ssm_a_re", "ssm_a_im", "ssm_b_re", "ssm_b_im", "ssm_c_re",
         "ssm_c_im", "ssm_d", "ssm_glu_b", "kv_mod_b", "kv_norm_g", "forget_b", "ffn_conv_w", "ffn_conv_b", "final_norm_g"]
PACK_ROWS = 512


def _pack(arrs):
    flat = jnp.concatenate([a.reshape(-1).astype(f32) for a in arrs])
    unit = PACK_ROWS * LANES
    n = -(-flat.shape[0] // unit) * unit
    return jnp.pad(flat, (0, n - flat.shape[0])).reshape(-1, LANES)


def _unpack(packed, shapes):
    flat, out, off = packed.reshape(-1), [], 0
    for s in shapes:
        n = math.prod(s)
        out.append(flat[off:off + n].reshape(s))
        off += n
    return out


def _silu(v):
    return v * _sigmoid(v)


def _half(w, c, axis):
    r = w.shape[axis] // 2
    return lax.dynamic_slice_in_dim(w, c * r, r, axis=axis)


class _Exchange:
    FIRST = ["ssm_w_in", "ssm_glu_w", "ssm_w_out"]
    FWD = dict(s5_fwd=["ffn_w_up0"], ffn_up_0=["ffn_w_down0", "kv_w"], ffn_down_0=["attn_w_q", "attn_w_out"],
               attn_fwd=["ffn_w_up1", "ffn_w_down1"])
    LATE = (3, 4)

    def __init__(self, cfg, blocks, core):
        self.cfg, self.blocks, self.core = cfg, blocks, core
        self.parts, self.scattered, self.names = {}, {}, {}

    def weights(self, names, gathered):
        D, F, NH = self.cfg.D, self.cfg.F, self.cfg.NH
        W = {}
        for n, g in zip(names, gathered):
            if n.startswith("ffn_w_up"):
                W[n] = g.reshape(4, D, 2 * F // 4)
            elif n == "kv_w":
                full = g.reshape(4, D, -1).transpose(1, 0, 2).reshape(D, -1)
                W["kv_w"] = full[:, :2 * D]
                W["kv_wf"] = jnp.pad(full[:, 2 * D:], ((0, 0), (0, LANES - NH)))
            else:
                W[n] = g.reshape(-1, D)
        return W

    def sibling_sum(self, key, grads):
        D = self.cfg.D

        def blocks_of(n, g):
            if n.startswith("ffn_w_up"):
                return g.reshape(4, 2, D // 2, -1)
            if n == "kv_w":
                return g.reshape(D, 4, -1).transpose(1, 0, 2).reshape(4, 2, D // 2, -1)
            return g.reshape(4, 2, g.shape[0] // 8, g.shape[1])
        names = list(grads)
        gb = [blocks_of(n, grads[n]) for n in names]
        recv = _sibling_swap_halves(gb, f"grad_sibling_swap_{key}")
        core1 = jnp.reshape(self.core, (1,)).astype(jnp.int32)
        for n, g, r in zip(names, gb, recv):
            self.parts[n] = _add_own_half(g, r, core1, f"grad_add_{n}")
        return self.parts

    def rider(self, key, grads=None):
        if key in self.FWD:
            return _gather_rider([self.blocks[n] for n in self.FWD[key]], (1, 2) if key == "attn_fwd" else self.LATE)
        if grads is None:
            return None
        self.names[key] = list(grads)
        parts = self.sibling_sum(key, grads)
        return _scatter_rider([parts[n] for n in self.names[key]])

    def done(self, key, extra):
        if key in self.FWD:
            return self.weights(self.FWD[key], extra)
        self.scattered.update(zip(self.names[key], extra))
        return {}


def kernel(x, c, mod_w, mod_b, norm_mix_g, norm_ffn_g, ssm_w_in, ssm_log_step, ssm_a_re, ssm_a_im, ssm_b_re, ssm_b_im, ssm_c_re, ssm_c_im, ssm_d, ssm_glu_w, ssm_glu_b, ssm_w_out, kv_mod_w, kv_mod_b, kv_norm_g, kv_w, forget_b, attn_w_q, attn_w_out, ffn_w_up, ffn_conv_w, ffn_conv_b, ffn_w_down, final_norm_g, loss_target, m_mod_w, m_mod_b, m_norm_mix_g, m_norm_ffn_g, m_ssm_w_in, m_ssm_log_step, m_ssm_a_re, m_ssm_a_im, m_ssm_b_re, m_ssm_b_im, m_ssm_c_re, m_ssm_c_im, m_ssm_d, m_ssm_glu_w, m_ssm_glu_b, m_ssm_w_out, m_kv_mod_w, m_kv_mod_b, m_kv_norm_g, m_kv_w, m_forget_b, m_attn_w_q, m_attn_w_out, m_ffn_w_up, m_ffn_conv_w, m_ffn_conv_b, m_ffn_w_down, m_final_norm_g, v_mod_w, v_mod_b, v_norm_mix_g, v_norm_ffn_g, v_ssm_w_in, v_ssm_log_step, v_ssm_a_re, v_ssm_a_im, v_ssm_b_re, v_ssm_b_im, v_ssm_c_re, v_ssm_c_im, v_ssm_d, v_ssm_glu_w, v_ssm_glu_b, v_ssm_w_out, v_kv_mod_w, v_kv_mod_b, v_kv_norm_g, v_kv_w, v_forget_b, v_attn_w_q, v_attn_w_out, v_ffn_w_up, v_ffn_conv_w, v_ffn_conv_b, v_ffn_w_down, v_final_norm_g):
    a = dict(locals())
    assert list(a) == ARGS
    return _step(CFG, a)


def _step(cfg, a):
    D, F, NH = cfg.D, cfg.F, cfg.NH
    x_, y_, c_ = _place()
    chip, dev = 2 * x_ + y_, 4 * x_ + 2 * y_ + c_

    big_src = dict(ssm_w_in=a["ssm_w_in"][0], ssm_glu_w=a["ssm_glu_w"][0], ssm_w_out=a["ssm_w_out"][0],
                   attn_w_q=a["attn_w_q"][0], attn_w_out=a["attn_w_out"][0],
                   ffn_w_up0=a["ffn_w_up"][0], ffn_w_up1=a["ffn_w_up"][1],
                   ffn_w_down0=a["ffn_w_down"][0], ffn_w_down1=a["ffn_w_down"][1], kv_w=a["kv_w"])
    big_names = list(big_src)
    exch = _Exchange(cfg, {n: _half(big_src[n], c_, 0).astype(bf16) for n in big_names}, c_)
    first = exch.FIRST
    blocks = [exch.blocks[n] for n in first] + [_half(a["ssm_glu_b"], c_, 1), _half(a["ffn_conv_w"], c_, 2), a["c"]]
    got = _allgather8(blocks, "gather_weights")
    W = exch.weights(first, got)
    glu_b_full = got[-3].reshape(D)
    conv_w_full = got[-2].transpose(1, 2, 0, 3).reshape(2, 3, 2 * F)
    c16 = jnp.pad(got[-1].reshape(N_DEV, D), ((0, 16 - N_DEV), (0, 0)))

    mcols = [_mm(c16, a["mod_w"][l], mode="nn", a_pro=_silu, name=f"mod_fwd_{l}") for l in range(2)]
    mcols.append(_mm(c16, a["kv_mod_w"], mode="nn", a_pro=_silu, name="mod_fwd_kv"))
    widths = [m.shape[1] for m in mcols]
    mall = _allgather8([jnp.concatenate(mcols, axis=1)[:N_DEV]], "gather_mod")[0][0::2]
    offs = [0, widths[0], widths[0] + widths[1]]
    rows = []
    for off, wd, bias in zip(offs, widths, [a["mod_b"][0], a["mod_b"][1], a["kv_mod_b"]]):
        fullm = mall[:, :, off:off + wd].transpose(1, 0, 2).reshape(N_DEV, 4 * wd) + bias
        rows.append(lax.dynamic_slice_in_dim(fullm, dev, 1, axis=0))
    mod = {}
    for l in range(2):
        mod[f"l{l}"] = dict(zip(["sh_m", "sc_m", "g_m", "sh_f", "sc_f", "g_f"], jnp.split(rows[l], 6, axis=1)))
    mod["sh_kv"], mod["sc_kv"] = jnp.split(rows[2], 2, axis=1)

    sp = {n: a[n] for n in ["norm_mix_g", "norm_ffn_g", "kv_norm_g", "final_norm_g", "ffn_conv_b", "forget_b", "ssm_log_step",
                            "ssm_a_re", "ssm_a_im", "ssm_b_re", "ssm_b_im", "ssm_c_re", "ssm_c_im", "ssm_d"]}
    sp["ssm_glu_b"], sp["ffn_conv_w"] = glu_b_full, conv_w_full
    loss, dx, big, small, dmod = _local_step(cfg, a["x"][0], a["loss_target"][0], mod, W, sp, exch)
    loss = lax.psum(loss, ("x", "y", "c"))

    small["mod_b"] = jnp.concatenate([dmod[0], dmod[1]], axis=0)
    small["kv_mod_b"] = dmod[2]
    shapes = [(2, 6 * D) if n == "mod_b" else (1, D) if n == "ssm_glu_b" else (2, 3, 2 * F) if n == "ffn_conv_w"
              else a[n].shape for n in SMALL]
    small_rider = _gather_rider([_pack([small[n] for n in SMALL])])

    last = list(big)
    exch.scattered.update(zip(last, _chip_scatter([exch.sibling_sum("tail", big)[n] for n in last], "grad_chip_scatter")))
    chip1, core1 = jnp.reshape(chip, (1,)).astype(jnp.int32), jnp.reshape(c_, (1,)).astype(jnp.int32)
    mine = {n: _sum_parts(exch.parts[n], exch.scattered[n], chip1, f"grad_sum_{n}") for n in big_names}
    other = dict(zip(big_names, _sibling_send([mine[n] for n in big_names], "grad_sibling_send")))

    grads, delta, new_m, new_v = {}, {}, {}, {}
    members = dict(ffn_w_up=["ffn_w_up0", "ffn_w_up1"], ffn_w_down=["ffn_w_down0", "ffn_w_down1"], kv_w=["kv_w"],
                   ssm_w_in=["ssm_w_in"], ssm_glu_w=["ssm_glu_w"], ssm_w_out=["ssm_w_out"], attn_w_q=["attn_w_q"],
                   attn_w_out=["attn_w_out"])
    for n, parts_ in members.items():
        shp = a[n].shape
        three = lambda t: t.reshape(len(parts_), -1, shp[-1])
        g_, d_, m_, v_, extra = _adamw_halves(three(a[n]), three(a["m_" + n]), three(a["v_" + n]),
                                              jnp.stack([mine[p] for p in parts_]), jnp.stack([other[p] for p in parts_]),
                                              core1, f"adamw_{n}", small_rider if n == "ffn_w_up" else None)
        if extra:
            packs = extra[0]
        grads[n], delta[n], new_m[n], new_v[n] = g_.reshape(shp), d_.reshape(shp), m_.reshape(shp), v_.reshape(shp)

    gsmall = dict(zip(SMALL, _unpack(_sum_lead(packs, f32, "sum_small"), shapes)))
    per_dev = packs.reshape(N_DEV, -1)
    sizes = [math.prod(s) for s in shapes]
    starts = dict(zip(SMALL, [sum(sizes[:i]) for i in range(len(sizes))]))

    def rows_of(name, l, width):
        st = starts[name] + l * 6 * D
        blk = lax.dynamic_slice(per_dev, (0, st + chip * width), (N_DEV, width))
        return jnp.pad(blk, ((0, 16 - N_DEV), (0, 0)))
    g_mod_w = jnp.stack([_mm(c16, rows_of("mod_b", l, 6 * D // 4), mode="tn", a_pro=_silu, name=f"mod_dw_{l}") for l in range(2)])
    g_kv_mod_w = _mm(c16, rows_of("kv_mod_b", 0, 2 * D // 4), mode="tn", a_pro=_silu, name="mod_dw_kv")
    gsmall["ssm_glu_b"] = lax.dynamic_slice_in_dim(gsmall["ssm_glu_b"], chip * (D // 4), D // 4, axis=1)
    gsmall["ffn_conv_w"] = lax.dynamic_slice_in_dim(gsmall["ffn_conv_w"], chip * (2 * F // 4), 2 * F // 4, axis=2)

    grads.update(gsmall)
    grads["mod_w"], grads["kv_mod_w"] = g_mod_w, g_kv_mod_w
    for n in ["mod_w", "kv_mod_w"]:
        shp = a[n].shape
        two = lambda t: t.reshape(-1, shp[-1])
        d_, m_, v_ = _adamw(two(a[n]), two(grads[n]), two(a["m_" + n]), two(a["v_" + n]), f"adamw_{n}")
        delta[n], new_m[n], new_v[n] = d_.reshape(shp), m_.reshape(shp), v_.reshape(shp)
    grads = {n: grads[n].reshape(a[n].shape) for n in WEIGHTS}
    sshapes = [a[n].shape for n in SMALL]
    d_, m_, v_ = _adamw(_pack([a[n] for n in SMALL]), _pack([grads[n] for n in SMALL]), _pack([a["m_" + n] for n in SMALL]),
                        _pack([a["v_" + n] for n in SMALL]), "adamw_small")
    for n, dd_, mm_, vv_ in zip(SMALL, _unpack(d_, sshapes), _unpack(m_, sshapes), _unpack(v_, sshapes)):
        delta[n], new_m[n], new_v[n] = dd_, mm_, vv_

    return (loss, dx[None], *[grads[n] for n in WEIGHTS], *[delta[n] for n in WEIGHTS],
            *[new_m[n] for n in WEIGHTS], *[new_v[n] for n in WEIGHTS])
```

```python
import collections
import functools
import math

import jax
import jax.numpy as jnp
from jax import lax
from jax.experimental import pallas as pl
from jax.experimental.pallas import tpu as pltpu

f32 = jnp.float32
bf16 = jnp.bfloat16
MESH = pl.DeviceIdType.MESH

LANES = 128
SUBLANES = 8
VMEM_BYTES_V7X = 64 * 1024 * 1024
VMEM_LIMIT = 56 * 1024 * 1024

Cfg = collections.namedtuple("Cfg", "L D G P H NH DH F TC BQ")
CFG = Cfg(L=4096, D=2048, G=128, P=64, H=16, NH=16, DH=128, F=5632, TC=1024, BQ=1024)
NORM_EPS = 1e-6
ADAM_LR, ADAM_B1, ADAM_B2, ADAM_EPS, ADAM_WD, ADAM_STEP = 0.001, 0.9, 0.999, 1e-08, 0.01, 10
N_DEV = 8


def _cp(sem=None):
    return pltpu.CompilerParams(dimension_semantics=sem, vmem_limit_bytes=VMEM_LIMIT)


def _tile(dim, pref, unit=LANES):
    if dim <= pref:
        return dim
    t = (pref // unit) * unit
    while t > unit and dim % t:
        t -= unit
    assert dim % t == 0, (dim, pref)
    return t


_DIMS = {"nn": (((1,), (0,)), ((), ())), "nt": (((1,), (1,)), ((), ())), "tn": (((0,), (0,)), ((), ()))}


def _mm(a, b, *, mode, name, tm=1024, tn=1024, tk=2048, b4=False, out4=False, a_pro=None, extras=(), epi=None,
        out_dtypes=(f32,), rider=None):
    if mode == "tn":
        K, M = a.shape
    else:
        M, K = a.shape
    if b4:
        R, c4 = b.shape[1], b.shape[2]
        N = R if mode == "nt" else 4 * c4
        assert (K == 4 * c4) if mode == "nt" else (K == R)
    else:
        N = b.shape[0] if mode == "nt" else b.shape[1]
        assert K == (b.shape[1] if mode == "nt" else b.shape[0])
    n4 = N // 4
    tm = _tile(M, tm, LANES if mode == "tn" else SUBLANES * 2)
    tn = _tile(n4 if out4 or (b4 and mode != "nt") else N, tn)
    tk = _tile(b.shape[2] if (b4 and mode == "nt") else K, tk)
    nm, nn_, nk = M // tm, N // tn, K // tk

    a_spec = pl.BlockSpec((tk, tm), lambda i, j, k: (k, i)) if mode == "tn" else pl.BlockSpec((tm, tk), lambda i, j, k: (i, k))
    if b4 and mode == "nt":
        q = b.shape[2] // tk
        b_spec = pl.BlockSpec((None, tn, tk), lambda i, j, k: (lax.div(k, q), j, lax.rem(k, q)))
    elif b4:
        q = b.shape[2] // tn
        b_spec = pl.BlockSpec((None, tk, tn), lambda i, j, k: (lax.div(j, q), k, lax.rem(j, q)))
    elif mode == "nt":
        b_spec = pl.BlockSpec((tn, tk), lambda i, j, k: (j, k))
    else:
        b_spec = pl.BlockSpec((tk, tn), lambda i, j, k: (k, j))
    ex_specs = []
    for arr, kind in extras:
        if kind == "mn":
            ex_specs.append(pl.BlockSpec((tm, tn), lambda i, j, k: (i, j)))
        else:
            ex_specs.append(pl.BlockSpec((1, tn), lambda i, j, k: (0, j)))
    if out4:
        qo = n4 // tn
        o_spec = pl.BlockSpec((None, tm, tn), lambda i, j, k: (lax.div(j, qo), i, lax.rem(j, qo)))
        o_shapes = [jax.ShapeDtypeStruct((4, M, n4), dt) for dt in out_dtypes]
    else:
        o_spec = pl.BlockSpec((tm, tn), lambda i, j, k: (i, j))
        o_shapes = [jax.ShapeDtypeStruct((M, N), dt) for dt in out_dtypes]
    ne, no = len(extras), len(out_dtypes)
    dims = _DIMS[mode]

    def body(a_ref, b_ref, *rest):
        ex_refs, o_refs, acc_ref = rest[:ne], rest[ne:ne + no], rest[ne + no]
        k = pl.program_id(2)

        @pl.when(k == 0)
        def _():
            acc_ref[...] = jnp.zeros_like(acc_ref)

        av = a_ref[...]
        if a_pro is not None:
            av = a_pro(av)
        acc_ref[...] += lax.dot_general(av.astype(bf16), b_ref[...].astype(bf16), dims, preferred_element_type=f32)

        @pl.when(k == nk - 1)
        def _():
            acc = acc_ref[...]
            outs = (acc,) if epi is None else epi(acc, *[r[...] for r in ex_refs])
            for o_ref, o in zip(o_refs, outs):
                o_ref[...] = o.astype(o_ref.dtype)

    res, extra = _host_call(
        body, name=name, grid=(nm, nn_, nk), in_specs=[a_spec, b_spec] + ex_specs, out_specs=[o_spec] * no,
        out_shape=o_shapes, scratch_shapes=[pltpu.VMEM((tm, tn), f32)], args=(a, b, *[e[0] for e in extras]), rider=rider)
    res = res[0] if no == 1 else res
    return (res, extra) if rider else res


HALO = 16


def _rowwise(fn, ins, outs, accs, *, L, C, tl, tc, name):
    tl = _tile(L, tl, HALO)
    tc = _tile(C, tc)
    ni, nj = L // tl, C // tc
    hb = tl // HALO
    nh = L // HALO
    in_specs = []
    for spec in ins:
        kind = spec[1]
        off = spec[2] if len(spec) > 2 else 0
        if kind == "rc":
            in_specs.append(pl.BlockSpec((tl, tc), lambda j, i, off=off: (i, j + off)))
        elif kind == "c":
            in_specs.append(pl.BlockSpec((1, tc), lambda j, i, off=off: (0, j + off)))
        elif kind == "c3":
            in_specs.append(pl.BlockSpec((3, tc), lambda j, i, off=off: (0, j + off)))
        elif kind == "prev":
            in_specs.append(pl.BlockSpec((HALO, tc), lambda j, i, off=off: (jnp.maximum(i * hb - 1, 0), j + off)))
        elif kind == "next":
            in_specs.append(pl.BlockSpec((HALO, tc), lambda j, i, off=off: (jnp.minimum((i + 1) * hb, nh - 1), j + off)))
        else:
            raise ValueError(kind)
    out_specs = [pl.BlockSpec((tl, tc), lambda j, i: (i, j)) for _ in outs]
    out_specs += [pl.BlockSpec((r, tc), lambda j, i: (0, j)) for r in accs]
    out_shape = [jax.ShapeDtypeStruct((L, C), dt) for dt in outs] + [jax.ShapeDtypeStruct((r, C), f32) for r in accs]
    nin, nout, nacc = len(ins), len(outs), len(accs)

    def body(*refs):
        i = pl.program_id(1)
        tiles = [r[...] for r in refs[:nin]]
        o_vals, a_vals = fn(i, ni, *tiles)
        for r, v in zip(refs[nin:nin + nout], o_vals):
            r[...] = v.astype(r.dtype)
        if nacc:
            @pl.when(i == 0)
            def _():
                for r in refs[nin + nout:]:
                    r[...] = jnp.zeros_like(r)
            for r, v in zip(refs[nin + nout:], a_vals):
                r[...] += v

    res = pl.pallas_call(
        body, name=name, grid=(nj, ni), in_specs=in_specs, out_specs=out_specs, out_shape=out_shape,
        compiler_params=_cp(("parallel", "arbitrary")),
    )(*[s[0] for s in ins])
    return res


def _colsum(v):
    return jnp.sum(v, axis=0, keepdims=True)


def _sigmoid(x):
    return 1.0 / (1.0 + jnp.exp(-x))


_GELU_C = math.sqrt(2.0 / math.pi)


def _gelu(y):
    t = jnp.tanh(_GELU_C * (y + 0.044715 * y * y * y))
    return 0.5 * y * (1.0 + t)


def _gelu_grad(y):
    y2 = y * y
    t = jnp.tanh(_GELU_C * (y + 0.044715 * y * y2))
    return 0.5 * (1.0 + t) + 0.5 * y * (1.0 - t * t) * _GELU_C * (1.0 + 3.0 * 0.044715 * y2)


def _norm_mod_fwd(x, g, sc, sh, cfg, name):
    def fn(i, ni, xv, gv, scv, shv):
        rstd = lax.rsqrt(jnp.mean(xv * xv, axis=-1, keepdims=True) + NORM_EPS)
        return [xv * rstd * gv * (1.0 + scv) + shv], []
    return _rowwise(fn, [(x, "rc"), (g, "c"), (sc, "c"), (sh, "c")], [bf16], [], L=cfg.L, C=cfg.D, tl=256, tc=cfg.D, name=name)[0]


def _norm_mod_bwd(dh, x, g, sc, dres, cfg, name):
    def fn(i, ni, dhv, xv, gv, scv, *rest):
        dhv = dhv.astype(f32)
        rstd = lax.rsqrt(jnp.mean(xv * xv, axis=-1, keepdims=True) + NORM_EPS)
        xh = xv * rstd
        dxh = dhv * (gv * (1.0 + scv))
        dx = rstd * (dxh - xh * jnp.mean(dxh * xh, axis=-1, keepdims=True))
        if rest:
            dx = dx + rest[0]
        return [dx], [_colsum(dhv * xh), _colsum(dhv)]
    ins = [(dh, "rc"), (x, "rc"), (g, "c"), (sc, "c")] + ([(dres, "rc")] if dres is not None else [])
    return _rowwise(fn, ins, [f32], [1, 1], L=cfg.L, C=cfg.D, tl=256, tc=cfg.D, name=name)


def _final_loss(x, g, tgt, cfg):
    D = cfg.D

    def fn(i, ni, xv, gv, tv):
        rstd = lax.rsqrt(jnp.mean(xv * xv, axis=-1, keepdims=True) + NORM_EPS)
        xh = xv * rstd
        err = xh * gv - tv
        dy = err * (1.0 / D)
        dxh = dy * gv
        dx = rstd * (dxh - xh * jnp.mean(dxh * xh, axis=-1, keepdims=True))
        return [dx], [_colsum(dy * xh), _colsum(err * err)]
    return _rowwise(fn, [(x, "rc"), (g, "c"), (tgt, "rc")], [f32], [1, 1], L=cfg.L, C=D, tl=256, tc=D, name="final_loss")


def _gate_bwd(dx, out, gate, cfg, name):
    def fn(i, ni, dxv, ov, gv):
        return [dxv * gv], [_colsum(dxv * ov.astype(f32))]
    return _rowwise(fn, [(dx, "rc"), (out, "rc"), (gate, "c")], [bf16], [1], L=cfg.L, C=cfg.D, tl=512, tc=cfg.D, name=name)


def _glu_bwd(dz, g, pre, cfg):
    def fn(i, ni, dzv, gv, pv):
        dzv = dzv.astype(f32)
        gv = gv.astype(f32)
        s = _sigmoid(pv)
        dpre = dzv * gv * s * (1.0 - s)
        return [dpre, dzv * s], [_colsum(dpre)]
    return _rowwise(fn, [(dz, "rc"), (g, "rc"), (pre, "rc")], [bf16, f32], [1], L=cfg.L, C=cfg.D, tl=512, tc=cfg.D, name="glu_bwd")


def _shift_rows(av, pv, k, i):
    rows = lax.broadcasted_iota(jnp.int32, av.shape, 0)
    cur = pltpu.roll(av, k, 0)
    prev = pltpu.roll(pv, k, 0)
    prev = jnp.where(i > 0, prev, 0.0)
    prev_full = jnp.concatenate([prev, jnp.zeros((av.shape[0] - pv.shape[0], av.shape[1]), av.dtype)], axis=0) \
        if av.shape[0] > pv.shape[0] else prev
    return jnp.where(rows >= k, cur, prev_full)


def _shift_rows_up(av, nv, k, i, ni):
    n, h = av.shape[0], nv.shape[0]
    rows = lax.broadcasted_iota(jnp.int32, av.shape, 0)
    cur = pltpu.roll(av, n - k, 0)
    nxt = pltpu.roll(nv, h - k, 0)
    nxt = jnp.where(i < ni - 1, nxt, 0.0)
    nxt_full = jnp.concatenate([jnp.zeros((n - h, av.shape[1]), av.dtype), nxt], axis=0) if n > h else nxt
    return jnp.where(rows < n - k, cur, nxt_full)


def _conv3(av, pv, w, i):
    return w[0:1] * _shift_rows(av, pv, 2, i) + w[1:2] * _shift_rows(av, pv, 1, i) + w[2:3] * av


def _conv_act_fwd(a, conv_w, conv_b, cfg):
    F = cfg.F
    tc = _tile(F, 1408)
    nb = F // tc

    def fn(i, ni, au, av, pu, pv, wu, wv, bu, bv):
        cu = _conv3(au.astype(f32), pu.astype(f32), wu, i) + bu
        cv = _conv3(av.astype(f32), pv.astype(f32), wv, i) + bv
        return [cu * _sigmoid(cu) * cv], []
    ins = [(a, "rc"), (a, "rc", nb), (a, "prev"), (a, "prev", nb), (conv_w, "c3"), (conv_w, "c3", nb), (conv_b, "c"), (conv_b, "c", nb)]
    return _rowwise(fn, ins, [bf16], [], L=cfg.L, C=F, tl=512, tc=tc, name="conv_act_fwd")[0]


def _conv_act_bwd1(dact, a, conv_w, conv_b, cfg):
    F = cfg.F
    tc = _tile(F, 1408)
    nb = F // tc

    def fn(i, ni, dav, au, av, pu, pv, wu, wv, bu, bv):
        dav = dav.astype(f32)
        au, av, pu, pv = au.astype(f32), av.astype(f32), pu.astype(f32), pv.astype(f32)
        au1, au2 = _shift_rows(au, pu, 1, i), _shift_rows(au, pu, 2, i)
        av1, av2 = _shift_rows(av, pv, 1, i), _shift_rows(av, pv, 2, i)
        cu = wu[0:1] * au2 + wu[1:2] * au1 + wu[2:3] * au + bu
        cv = wv[0:1] * av2 + wv[1:2] * av1 + wv[2:3] * av + bv
        s = _sigmoid(cu)
        dcu = dav * cv * (s * (1.0 + cu * (1.0 - s)))
        dcv = dav * cu * s
        dwu = jnp.concatenate([_colsum(dcu * au2), _colsum(dcu * au1), _colsum(dcu * au)], axis=0)
        dwv = jnp.concatenate([_colsum(dcv * av2), _colsum(dcv * av1), _colsum(dcv * av)], axis=0)
        return [dcu, dcv], [dwu, dwv, _colsum(dcu), _colsum(dcv)]
    ins = [(dact, "rc"), (a, "rc"), (a, "rc", nb), (a, "prev"), (a, "prev", nb), (conv_w, "c3"), (conv_w, "c3", nb),
           (conv_b, "c"), (conv_b, "c", nb)]
    return _rowwise(fn, ins, [bf16, bf16], [3, 3, 1, 1], L=cfg.L, C=F, tl=512, tc=tc, name="conv_act_bwd1")


def _conv_bwd2(dc, w, cfg, name):
    F = cfg.F
    tc = _tile(F, 1408)

    def fn(i, ni, dcv, nxt, wv):
        dcv, nxt = dcv.astype(f32), nxt.astype(f32)
        return [wv[2:3] * dcv + wv[1:2] * _shift_rows_up(dcv, nxt, 1, i, ni) + wv[0:1] * _shift_rows_up(dcv, nxt, 2, i, ni)], []
    return _rowwise(fn, [(dc, "rc"), (dc, "next"), (w, "c3")], [bf16], [], L=cfg.L, C=F, tl=512, tc=tc, name=name)[0]


NSLAB = 8


def _s5_tables(abar_re, abar_im, lam_re, lam_im, step, cfg):
    J = cfg.G // 8
    expo = jnp.array([r + 1 for r in range(8)] + [8 * 2 ** p for p in range(8)], f32)[:, None, None]
    mag = jnp.exp(lam_re * step * expo)
    ang = lam_im * step * expo
    t_re = (mag * jnp.cos(ang)).reshape(16, J, 8 * cfg.P).transpose(1, 0, 2)
    t_im = (mag * jnp.sin(ang)).reshape(16, J, 8 * cfg.P).transpose(1, 0, 2)
    tab = jnp.concatenate([t_re, t_im], axis=-1)
    arow = jnp.concatenate([abar_re.reshape(J, 1, 8 * cfg.P), abar_im.reshape(J, 1, 8 * cfg.P)], axis=-1)
    return arow, tab


def _s5_mats(bbar_re, bbar_im, c_re, c_im, cfg):
    J, P, H = cfg.G // 8, cfg.P, cfg.H
    eye = jnp.eye(8, dtype=f32)

    def bd_in(bb):
        bb = bb.reshape(J, 8, P, H)
        return jnp.einsum("jgph,gk->jghkp", bb, eye).reshape(J, 8 * H, 8 * P)

    def bd_out(cc):
        cc = cc.reshape(J, 8, H, P)
        return jnp.einsum("jghp,gk->jgpkh", cc, eye).reshape(J, 8 * P, 8 * H)

    bmat = jnp.concatenate([bd_in(bbar_re), bd_in(bbar_im)], axis=2).astype(bf16)
    cmat = jnp.concatenate([bd_out(c_re), -bd_out(c_im)], axis=1).astype(bf16)
    return bmat, cmat


def _s5_unmats(dbmat, dcmat, cfg):
    J, P, H = cfg.G // 8, cfg.P, cfg.H
    eye = jnp.eye(8, dtype=f32)
    db = dbmat.reshape(J, 8, H, 2, 8, P)
    db = jnp.einsum("jghckp,gk->cjgph", db, eye).reshape(2, cfg.G, P, H)
    dc = dcmat.reshape(J, 2, 8, P, 8, H)
    dc = jnp.einsum("jcgpkh,gk->cjghp", dc, eye).reshape(2, cfg.G, H, P)
    return db[0], db[1], dc[0], -dc[1]


def _chunk_scan(x_ref, row0, nt, arow_ref, tab_ref, c0, reverse):
    sg = -1.0 if reverse else 1.0
    rows = lax.broadcasted_iota(jnp.int32, (nt, LANES), 0)
    order = list(range(7, -1, -1)) if reverse else list(range(8))

    def ld(k, r):
        return x_ref[k, pl.ds(row0 + r, nt, stride=8), :]

    def tab(row, k):
        return tab_ref[pl.ds(row, 1), pl.ds(k * LANES, LANES)]

    carries = [None] * NSLAB
    for k in range(4):
        ar = arow_ref[:, pl.ds(k * LANES, LANES)]
        ai = sg * arow_ref[:, pl.ds((4 + k) * LANES, LANES)]
        sr, si = ld(k, order[0]), ld(4 + k, order[0])
        for r in order[1:]:
            sr, si = ar * sr - ai * si + ld(k, r), ar * si + ai * sr + ld(4 + k, r)
        if reverse:
            cr = jnp.where(rows == nt - 1, c0[k], pltpu.roll(sr, nt - 1, 0))
            ci = jnp.where(rows == nt - 1, c0[4 + k], pltpu.roll(si, nt - 1, 0))
        else:
            cr = jnp.where(rows == 0, c0[k], pltpu.roll(sr, 1, 0))
            ci = jnp.where(rows == 0, c0[4 + k], pltpu.roll(si, 1, 0))
        d, p = 1, 0
        while d < nt:
            qr, qi = tab(8 + p, k), sg * tab(8 + p, 4 + k)
            if reverse:
                shr, shi, m = pltpu.roll(cr, nt - d, 0), pltpu.roll(ci, nt - d, 0), rows < nt - d
            else:
                shr, shi, m = pltpu.roll(cr, d, 0), pltpu.roll(ci, d, 0), rows >= d
            cr, ci = cr + jnp.where(m, qr * shr - qi * shi, 0.0), ci + jnp.where(m, qr * shi + qi * shr, 0.0)
            d, p = 2 * d, p + 1
        carries[k], carries[4 + k] = cr, ci
        sr, si = cr, ci
        for r in order:
            sr, si = ar * sr - ai * si + ld(k, r), ar * si + ai * sr + ld(4 + k, r)
            x_ref[k, pl.ds(row0 + r, nt, stride=8), :] = sr
            x_ref[4 + k, pl.ds(row0 + r, nt, stride=8), :] = si
    return carries


def _slabs_to_mat(x_ref, row0, n):
    return jnp.concatenate([x_ref[k, pl.ds(row0, n), :] for k in range(NSLAB)], axis=1)


def _mat_to_slabs(x_ref, row0, n, m):
    for k in range(NSLAB):
        x_ref[k, pl.ds(row0, n), :] = m[:, k * LANES:(k + 1) * LANES]


def _s5_fwd(u, bmat, cmat, drow, arow, tab, cfg, rider=None):
    L, D, Tc = cfg.L, cfg.D, cfg.TC
    J, NC, nt = cfg.G // 8, L // Tc, Tc // 8
    W = NSLAB * LANES

    def body(u_ref, b_ref, c_ref, d_ref, a_ref, t_ref, y_ref, g_ref, cin_ref, x_ref, st_ref):
        c = pl.program_id(1)

        @pl.when(c == 0)
        def _():
            st_ref[...] = jnp.zeros_like(st_ref)

        cin_ref[...] = st_ref[...]
        ub = u_ref[...]
        _mat_to_slabs(x_ref, 0, Tc, jnp.dot(ub.astype(bf16), b_ref[...], preferred_element_type=f32))
        c0 = [st_ref[:, pl.ds(k * LANES, LANES)] for k in range(NSLAB)]
        _chunk_scan(x_ref, 0, nt, a_ref, t_ref, c0, False)
        for k in range(NSLAB):
            st_ref[:, pl.ds(k * LANES, LANES)] = x_ref[k, pl.ds(Tc - 1, 1), :]
        s = _slabs_to_mat(x_ref, 0, Tc).astype(bf16)
        y = jnp.dot(s, c_ref[...], preferred_element_type=f32) + d_ref[...] * ub
        y_ref[...] = y
        g_ref[...] = _gelu(y).astype(bf16)

    outs, extra = _host_call(
        body, name="s5_fwd", grid=(J, NC), rider=rider, args=(u, bmat, cmat, drow, arow, tab),
        in_specs=[pl.BlockSpec((Tc, LANES), lambda j, c: (c, j)),
                  pl.BlockSpec((None, LANES, W), lambda j, c: (j, 0, 0)),
                  pl.BlockSpec((None, W, LANES), lambda j, c: (j, 0, 0)),
                  pl.BlockSpec((1, LANES), lambda j, c: (0, j)),
                  pl.BlockSpec((None, 1, W), lambda j, c: (j, 0, 0)),
                  pl.BlockSpec((None, 16, W), lambda j, c: (j, 0, 0))],
        out_specs=[pl.BlockSpec((Tc, LANES), lambda j, c: (c, j)),
                   pl.BlockSpec((Tc, LANES), lambda j, c: (c, j)),
                   pl.BlockSpec((None, None, 1, W), lambda j, c: (j, c, 0, 0))],
        out_shape=[jax.ShapeDtypeStruct((L, D), f32), jax.ShapeDtypeStruct((L, D), bf16),
                   jax.ShapeDtypeStruct((J, NC, 1, W), f32)],
        scratch_shapes=[pltpu.VMEM((NSLAB, Tc, LANES), f32), pltpu.VMEM((1, W), f32)])
    return (*outs, extra)


def _s5_bwd(u, dy, cin, bmat, cmat, drow, arow, tab, cfg, rider=None):
    L, D, Tc = cfg.L, cfg.D, cfg.TC
    J, NC, nt = cfg.G // 8, L // Tc, Tc // 8
    W = NSLAB * LANES
    PAD = 0

    def body(u_ref, dy_ref, cin_ref, b_ref, c_ref, d_ref, a_ref, t_ref,
             du_ref, db_ref, dc_ref, da_ref, dd_ref, s_ref, g_ref, gst_ref):
        c = pl.program_id(1)

        @pl.when(c == 0)
        def _():
            gst_ref[...] = jnp.zeros_like(gst_ref)
            db_ref[...] = jnp.zeros_like(db_ref)
            dc_ref[...] = jnp.zeros_like(dc_ref)
            da_ref[...] = jnp.zeros_like(da_ref)
            dd_ref[...] = jnp.zeros_like(dd_ref)

        ub, dyb = u_ref[...], dy_ref[...]
        ub16, dy16 = ub.astype(bf16), dyb.astype(bf16)
        _mat_to_slabs(s_ref, PAD, Tc, jnp.dot(ub16, b_ref[...], preferred_element_type=f32))
        c0 = [cin_ref[:, pl.ds(k * LANES, LANES)] for k in range(NSLAB)]
        tile_in = _chunk_scan(s_ref, PAD, nt, a_ref, t_ref, c0, False)
        _mat_to_slabs(g_ref, 0, Tc, lax.dot_general(dy16, c_ref[...], _DIMS["nt"], preferred_element_type=f32))
        g0 = [gst_ref[:, pl.ds(k * LANES, LANES)] for k in range(NSLAB)]
        _chunk_scan(g_ref, 0, nt, a_ref, t_ref, g0, True)
        for k in range(NSLAB):
            gst_ref[:, pl.ds(k * LANES, LANES)] = g_ref[k, pl.ds(0, 1), :]
        for k in range(4):
            acc_r = jnp.zeros((nt, LANES), f32)
            acc_i = jnp.zeros((nt, LANES), f32)
            for r in range(8):
                gr = g_ref[k, pl.ds(r, nt, stride=8), :]
                gi = g_ref[4 + k, pl.ds(r, nt, stride=8), :]
                if r == 0:
                    pr, pi = tile_in[k], tile_in[4 + k]
                else:
                    pr = s_ref[k, pl.ds(PAD + r - 1, nt, stride=8), :]
                    pi = s_ref[4 + k, pl.ds(PAD + r - 1, nt, stride=8), :]
                acc_r += gr * pr + gi * pi
                acc_i += gi * pr - gr * pi
            da_ref[:, pl.ds(k * LANES, LANES)] += _colsum(acc_r)
            da_ref[:, pl.ds((4 + k) * LANES, LANES)] += _colsum(acc_i)
        gm = _slabs_to_mat(g_ref, 0, Tc).astype(bf16)
        sm = _slabs_to_mat(s_ref, PAD, Tc).astype(bf16)
        du = lax.dot_general(gm, b_ref[...], _DIMS["nt"], preferred_element_type=f32) + d_ref[...] * dyb
        du_ref[...] = du.astype(bf16)
        db_ref[...] += lax.dot_general(ub16, gm, _DIMS["tn"], preferred_element_type=f32)
        dc_ref[...] += lax.dot_general(sm, dy16, _DIMS["tn"], preferred_element_type=f32)
        dd_ref[...] += _colsum(dyb * ub)

    rc = lambda j, c: (NC - 1 - c, j)
    outs, extra = _host_call(
        body, name="s5_bwd", grid=(J, NC), rider=rider, args=(u, dy, cin, bmat, cmat, drow, arow, tab),
        in_specs=[pl.BlockSpec((Tc, LANES), rc), pl.BlockSpec((Tc, LANES), rc),
                  pl.BlockSpec((None, None, 1, W), lambda j, c: (j, NC - 1 - c, 0, 0)),
                  pl.BlockSpec((None, LANES, W), lambda j, c: (j, 0, 0)),
                  pl.BlockSpec((None, W, LANES), lambda j, c: (j, 0, 0)),
                  pl.BlockSpec((1, LANES), lambda j, c: (0, j)),
                  pl.BlockSpec((None, 1, W), lambda j, c: (j, 0, 0)),
                  pl.BlockSpec((None, 16, W), lambda j, c: (j, 0, 0))],
        out_specs=[pl.BlockSpec((Tc, LANES), rc),
                   pl.BlockSpec((None, LANES, W), lambda j, c: (j, 0, 0)),
                   pl.BlockSpec((None, W, LANES), lambda j, c: (j, 0, 0)),
                   pl.BlockSpec((None, 1, W), lambda j, c: (j, 0, 0)),
                   pl.BlockSpec((1, LANES), lambda j, c: (0, j))],
        out_shape=[jax.ShapeDtypeStruct((L, D), bf16), jax.ShapeDtypeStruct((J, LANES, W), f32),
                   jax.ShapeDtypeStruct((J, W, LANES), f32), jax.ShapeDtypeStruct((J, 1, W), f32),
                   jax.ShapeDtypeStruct((1, D), f32)],
        scratch_shapes=[pltpu.VMEM((NSLAB, Tc + PAD, LANES), f32), pltpu.VMEM((NSLAB, Tc, LANES), f32),
                        pltpu.VMEM((1, W), f32)])
    return (*outs, extra)


NEG = -1e30


def _tri_tables(nq, by_key):
    pairs = [(qi, ki) for ki in range(nq) for qi in range(ki, nq)] if by_key else \
            [(qi, ki) for qi in range(nq) for ki in range(qi + 1)]
    return jnp.array([p[0] for p in pairs], jnp.int32), jnp.array([p[1] for p in pairs], jnp.int32)


def _tri_call(body, name, cfg, by_key, in_specs, out_specs, out_shape, scratch_shapes, args, rider=None):
    nq = cfg.L // cfg.BQ
    outs, extra = _host_call(body, name=name, grid=(cfg.NH, nq * (nq + 1) // 2), in_specs=in_specs, out_specs=out_specs,
                             out_shape=out_shape, scratch_shapes=scratch_shapes, args=args,
                             prefetch=_tri_tables(nq, by_key), rider=rider)
    return (*outs, extra)


def _ta_fwd(q, kv, fk, cfg, rider=None):
    L, D, NH, DH, B = cfg.L, cfg.D, cfg.NH, cfg.DH, cfg.BQ
    scale = DH ** -0.5

    def body(qt_ref, kt_ref, q_ref, k_ref, v_ref, fk_ref, o_ref, lse_ref, m_ref, acc_ref, a_ref, s_ref, p_ref):
        pid = pl.program_id(1)
        qi, ki = qt_ref[pid], kt_ref[pid]

        @pl.when(ki == 0)
        def _():
            m_ref[...] = jnp.full_like(m_ref, NEG)
            acc_ref[...] = jnp.zeros_like(acc_ref)

        def compute(masked):
            s_ref[...] = lax.dot_general(q_ref[...], k_ref[...], _DIMS["nt"], preferred_element_type=f32)
            fkv = fk_ref[...]

            def strip(rows, row0, c):
                t = s_ref[rows, :] - fkv
                if masked:
                    t = jnp.where(_fa_mask(row0, t.shape), t, NEG)
                m_prev = m_ref[rows, :]
                m_new = jnp.maximum(m_prev, jnp.max(t, axis=1, keepdims=True))
                m_ref[rows, :] = m_new
                a_ref[rows, :] = jnp.exp(m_prev - m_new)
                p_ref[rows, :] = jnp.exp(t - m_new).astype(bf16)
                return c
            _fa_strips(B, strip, 0)
            v1 = jnp.concatenate([v_ref[...], jnp.ones((B, DH), bf16)], axis=1)
            acc_ref[...] = a_ref[...] * acc_ref[...] + jnp.dot(p_ref[...], v1, preferred_element_type=f32)

        @pl.when(ki < qi)
        def _():
            compute(False)

        @pl.when(ki == qi)
        def _():
            compute(True)
            l = acc_ref[:, DH:]
            o_ref[...] = (acc_ref[:, :DH] / l).astype(o_ref.dtype)
            lse_ref[...] = m_ref[...] + jnp.log(l[:, :1])

    col = pltpu.VMEM((B, 1), f32)
    return _tri_call(
        body, "attn_fwd", cfg, False,
        [pl.BlockSpec((B, DH), lambda h, p, qt, kt: (qt[p], h)),
         pl.BlockSpec((B, DH), lambda h, p, qt, kt: (kt[p], h)),
         pl.BlockSpec((B, DH), lambda h, p, qt, kt: (kt[p], NH + h)),
         pl.BlockSpec((None, 1, B), lambda h, p, qt, kt: (h, 0, kt[p]))],
        [pl.BlockSpec((B, DH), lambda h, p, qt, kt: (qt[p], h)),
         pl.BlockSpec((None, B, 1), lambda h, p, qt, kt: (h, qt[p], 0))],
        [jax.ShapeDtypeStruct((L, D), bf16), jax.ShapeDtypeStruct((NH, L, 1), f32)],
        [col, pltpu.VMEM((B, 2 * DH), f32), col, pltpu.VMEM((B, B), f32), pltpu.VMEM((B, B), bf16)],
        (q, kv, kv, fk), rider)


def _ta_bwd_dq(q, kv, do, o, lse, fk, cfg, rider=None):
    L, D, NH, DH, B = cfg.L, cfg.D, cfg.NH, cfg.DH, cfg.BQ
    scale = DH ** -0.5

    def body(qt_ref, kt_ref, q_ref, k_ref, v_ref, do_ref, o_ref, lse_ref, fk_ref, dq_ref, dfq_ref, dl_ref,
             acc_ref, s_ref, dp_ref, ds_ref):
        pid = pl.program_id(1)
        qi, ki = qt_ref[pid], kt_ref[pid]

        @pl.when(ki == 0)
        def _():
            dl_ref[...] = jnp.sum(do_ref[...].astype(f32) * o_ref[...].astype(f32), axis=1, keepdims=True)
            acc_ref[...] = jnp.zeros_like(acc_ref)

        def compute(masked):
            s_ref[...] = lax.dot_general(q_ref[...], k_ref[...], _DIMS["nt"], preferred_element_type=f32)
            dp_ref[...] = lax.dot_general(do_ref[...], v_ref[...], _DIMS["nt"], preferred_element_type=f32)
            fkv = fk_ref[...]

            def strip(rows, row0, c):
                p = jnp.exp(s_ref[rows, :] - fkv - lse_ref[rows, :])
                if masked:
                    p = jnp.where(_fa_mask(row0, p.shape), p, 0.0)
                ds_ref[rows, :] = (p * (dp_ref[rows, :] - dl_ref[rows, :])).astype(bf16)
                return c
            _fa_strips(B, strip, 0)
            k1 = jnp.concatenate([k_ref[...], jnp.ones((B, DH), bf16)], axis=1)
            acc_ref[...] += jnp.dot(ds_ref[...], k1, preferred_element_type=f32)

        @pl.when(ki < qi)
        def _():
            compute(False)

        @pl.when(ki == qi)
        def _():
            compute(True)
            dq_ref[...] = (acc_ref[:, :DH] * scale).astype(dq_ref.dtype)
            dfq_ref[...] = acc_ref[:, DH:DH + 1]

    qmap = lambda h, p, qt, kt: (qt[p], h)
    cmap = lambda h, p, qt, kt: (h, qt[p], 0)
    return _tri_call(
        body, "attn_bwd_dq", cfg, False,
        [pl.BlockSpec((B, DH), qmap),
         pl.BlockSpec((B, DH), lambda h, p, qt, kt: (kt[p], h)),
         pl.BlockSpec((B, DH), lambda h, p, qt, kt: (kt[p], NH + h)),
         pl.BlockSpec((B, DH), qmap), pl.BlockSpec((B, DH), qmap),
         pl.BlockSpec((None, B, 1), cmap),
         pl.BlockSpec((None, 1, B), lambda h, p, qt, kt: (h, 0, kt[p]))],
        [pl.BlockSpec((B, DH), qmap), pl.BlockSpec((None, B, 1), cmap), pl.BlockSpec((None, B, 1), cmap)],
        [jax.ShapeDtypeStruct((L, D), bf16), jax.ShapeDtypeStruct((NH, L, 1), f32), jax.ShapeDtypeStruct((NH, L, 1), f32)],
        [pltpu.VMEM((B, 2 * DH), f32), pltpu.VMEM((B, B), f32), pltpu.VMEM((B, B), f32), pltpu.VMEM((B, B), bf16)],
        (q, kv, kv, do, o, lse, fk), rider)


def _ta_bwd_dkv(q, kv, do, delta, lse, fk, cfg, rider=None):
    L, D, NH, DH, B = cfg.L, cfg.D, cfg.NH, cfg.DH, cfg.BQ
    nq = L // B
    scale = DH ** -0.5

    def body(qt_ref, kt_ref, q_ref, k_ref, v_ref, do_ref, dl_ref, lse_ref, fk_ref, dk_ref, dv_ref, dfk_ref,
             dka_ref, dva_ref, s_ref, dp_ref, p_ref, ds_ref):
        pid = pl.program_id(1)
        qi, ki = qt_ref[pid], kt_ref[pid]

        @pl.when(qi == ki)
        def _():
            dka_ref[...] = jnp.zeros_like(dka_ref)
            dva_ref[...] = jnp.zeros_like(dva_ref)

        def compute(masked):
            s_ref[...] = lax.dot_general(q_ref[...], k_ref[...], _DIMS["nt"], preferred_element_type=f32)
            dp_ref[...] = lax.dot_general(do_ref[...], v_ref[...], _DIMS["nt"], preferred_element_type=f32)
            fkv = fk_ref[...]

            def strip(rows, row0, c):
                p = jnp.exp(s_ref[rows, :] - fkv - lse_ref[rows, :])
                if masked:
                    p = jnp.where(_fa_mask(row0, p.shape), p, 0.0)
                p_ref[rows, :] = p.astype(bf16)
                ds_ref[rows, :] = (p * (dp_ref[rows, :] - dl_ref[rows, :])).astype(bf16)
                return c
            _fa_strips(B, strip, 0)
            q1 = jnp.concatenate([q_ref[...], jnp.ones((B, DH), bf16)], axis=1)
            dva_ref[...] += lax.dot_general(p_ref[...], do_ref[...], _DIMS["tn"], preferred_element_type=f32)
            dka_ref[...] += lax.dot_general(ds_ref[...], q1, _DIMS["tn"], preferred_element_type=f32)

        @pl.when(qi == ki)
        def _():
            compute(True)

        @pl.when(qi > ki)
        def _():
            compute(False)

        @pl.when(qi == nq - 1)
        def _():
            dk_ref[...] = dka_ref[:, :DH].astype(dk_ref.dtype)
            dv_ref[...] = dva_ref[...].astype(dv_ref.dtype)
            dfk_ref[...] = -dka_ref[:, DH:DH + 1]

    qmap = lambda h, p, qt, kt: (qt[p], h)
    cmap = lambda h, p, qt, kt: (h, qt[p], 0)
    kmap = lambda h, p, qt, kt: (kt[p], h)
    return _tri_call(
        body, "attn_bwd_dkv", cfg, True,
        [pl.BlockSpec((B, DH), qmap), pl.BlockSpec((B, DH), kmap),
         pl.BlockSpec((B, DH), lambda h, p, qt, kt: (kt[p], NH + h)),
         pl.BlockSpec((B, DH), qmap), pl.BlockSpec((None, B, 1), cmap), pl.BlockSpec((None, B, 1), cmap),
         pl.BlockSpec((None, 1, B), lambda h, p, qt, kt: (h, 0, kt[p]))],
        [pl.BlockSpec((B, DH), kmap), pl.BlockSpec((B, DH), kmap),
         pl.BlockSpec((None, B, 1), lambda h, p, qt, kt: (h, kt[p], 0))],
        [jax.ShapeDtypeStruct((L, D), bf16), jax.ShapeDtypeStruct((L, D), bf16), jax.ShapeDtypeStruct((NH, L, 1), f32)],
        [pltpu.VMEM((B, 2 * DH), f32), pltpu.VMEM((B, DH), f32), pltpu.VMEM((B, B), f32), pltpu.VMEM((B, B), f32),
         pltpu.VMEM((B, B), bf16), pltpu.VMEM((B, B), bf16)],
        (q, kv, kv, do, delta, lse, fk), rider)


STRIP = 32


def _fa_strips(nrows, fn, init):
    return lax.fori_loop(0, nrows // STRIP, lambda r, c: fn(pl.ds(pl.multiple_of(r * STRIP, STRIP), STRIP), r * STRIP, c),
                         init, unroll=True)


def _fa_mask(row0, shape):
    rows = row0 + lax.broadcasted_iota(jnp.int32, shape, 0)
    cols = lax.broadcasted_iota(jnp.int32, shape, 1)
    return cols <= rows


def _fa_fwd(q, kv, fk, cfg):
    L, D, NH, DH, B = cfg.L, cfg.D, cfg.NH, cfg.DH, cfg.BQ
    nq = L // B
    scale = DH ** -0.5

    def body(q_ref, k_ref, v_ref, fk_ref, o_ref, lse_ref, m_ref, l_ref, acc_ref, a_ref, s_ref, p_ref):
        qi, ki = pl.program_id(1), pl.program_id(2)

        @pl.when(ki == 0)
        def _():
            m_ref[...] = jnp.full_like(m_ref, NEG)
            l_ref[...] = jnp.zeros_like(l_ref)
            acc_ref[...] = jnp.zeros_like(acc_ref)

        def compute(masked):
            s_ref[...] = lax.dot_general(q_ref[...], k_ref[...], _DIMS["nt"], preferred_element_type=f32)
            fkv = fk_ref[...]

            def strip(rows, row0, c):
                t = s_ref[rows, :] - fkv
                if masked:
                    t = jnp.where(_fa_mask(row0, t.shape), t, NEG)
                m_prev = m_ref[rows, :]
                m_new = jnp.maximum(m_prev, jnp.max(t, axis=1, keepdims=True))
                p = jnp.exp(t - m_new)
                alpha = jnp.exp(m_prev - m_new)
                l_ref[rows, :] = alpha * l_ref[rows, :] + jnp.sum(p, axis=1, keepdims=True)
                m_ref[rows, :] = m_new
                a_ref[rows, :] = alpha
                p_ref[rows, :] = p.astype(bf16)
                return c
            _fa_strips(B, strip, 0)
            acc_ref[...] = a_ref[...] * acc_ref[...] + jnp.dot(p_ref[...], v_ref[...], preferred_element_type=f32)

        @pl.when(ki < qi)
        def _():
            compute(False)

        @pl.when(ki == qi)
        def _():
            compute(True)
            o_ref[...] = (acc_ref[...] / l_ref[...]).astype(o_ref.dtype)
            lse_ref[...] = m_ref[...] + jnp.log(l_ref[...])

    col = pltpu.VMEM((B, 1), f32)
    return pl.pallas_call(
        body, name="attn_fwd", grid=(NH, nq, nq),
        in_specs=[pl.BlockSpec((B, DH), lambda h, qi, ki: (qi, h)),
                  pl.BlockSpec((B, DH), lambda h, qi, ki: (jnp.minimum(ki, qi), h)),
                  pl.BlockSpec((B, DH), lambda h, qi, ki: (jnp.minimum(ki, qi), NH + h)),
                  pl.BlockSpec((None, 1, B), lambda h, qi, ki: (h, 0, jnp.minimum(ki, qi)))],
        out_specs=[pl.BlockSpec((B, DH), lambda h, qi, ki: (qi, h)),
                   pl.BlockSpec((None, B, 1), lambda h, qi, ki: (h, qi, 0))],
        out_shape=[jax.ShapeDtypeStruct((L, D), bf16), jax.ShapeDtypeStruct((NH, L, 1), f32)],
        scratch_shapes=[col, col, pltpu.VMEM((B, DH), f32), col, pltpu.VMEM((B, B), f32), pltpu.VMEM((B, B), bf16)],
        compiler_params=_cp(("parallel", "parallel", "arbitrary")),
    )(q, kv, kv, fk)


def _fa_bwd_dq(q, kv, do, o, lse, fk, cfg):
    L, D, NH, DH, B = cfg.L, cfg.D, cfg.NH, cfg.DH, cfg.BQ
    nq = L // B
    scale = DH ** -0.5

    def body(q_ref, k_ref, v_ref, do_ref, o_ref, lse_ref, fk_ref, dq_ref, dfq_ref, dl_ref, acc_ref, df_ref, s_ref, dp_ref, ds_ref):
        qi, ki = pl.program_id(1), pl.program_id(2)

        @pl.when(ki == 0)
        def _():
            dl_ref[...] = jnp.sum(do_ref[...].astype(f32) * o_ref[...].astype(f32), axis=1, keepdims=True)
            acc_ref[...] = jnp.zeros_like(acc_ref)
            df_ref[...] = jnp.zeros_like(df_ref)

        def compute(masked):
            s_ref[...] = lax.dot_general(q_ref[...], k_ref[...], _DIMS["nt"], preferred_element_type=f32)
            dp_ref[...] = lax.dot_general(do_ref[...], v_ref[...], _DIMS["nt"], preferred_element_type=f32)
            fkv = fk_ref[...]

            def strip(rows, row0, c):
                p = jnp.exp(s_ref[rows, :] - fkv - lse_ref[rows, :])
                if masked:
                    p = jnp.where(_fa_mask(row0, p.shape), p, 0.0)
                ds = p * (dp_ref[rows, :] - dl_ref[rows, :])
                df_ref[rows, :] += jnp.sum(ds, axis=1, keepdims=True)
                ds_ref[rows, :] = ds.astype(bf16)
                return c
            _fa_strips(B, strip, 0)
            acc_ref[...] += jnp.dot(ds_ref[...], k_ref[...], preferred_element_type=f32)

        @pl.when(ki < qi)
        def _():
            compute(False)

        @pl.when(ki == qi)
        def _():
            compute(True)
            dq_ref[...] = (acc_ref[...] * scale).astype(dq_ref.dtype)
            dfq_ref[...] = df_ref[...]

    qmap = lambda h, qi, ki: (qi, h)
    cmap = lambda h, qi, ki: (h, qi, 0)
    return pl.pallas_call(
        body, name="attn_bwd_dq", grid=(NH, nq, nq),
        in_specs=[pl.BlockSpec((B, DH), qmap),
                  pl.BlockSpec((B, DH), lambda h, qi, ki: (jnp.minimum(ki, qi), h)),
                  pl.BlockSpec((B, DH), lambda h, qi, ki: (jnp.minimum(ki, qi), NH + h)),
                  pl.BlockSpec((B, DH), qmap), pl.BlockSpec((B, DH), qmap),
                  pl.BlockSpec((None, B, 1), cmap),
                  pl.BlockSpec((None, 1, B), lambda h, qi, ki: (h, 0, jnp.minimum(ki, qi)))],
        out_specs=[pl.BlockSpec((B, DH), qmap), pl.BlockSpec((None, B, 1), cmap), pl.BlockSpec((None, B, 1), cmap)],
        out_shape=[jax.ShapeDtypeStruct((L, D), bf16), jax.ShapeDtypeStruct((NH, L, 1), f32),
                   jax.ShapeDtypeStruct((NH, L, 1), f32)],
        scratch_shapes=[pltpu.VMEM((B, DH), f32), pltpu.VMEM((B, 1), f32), pltpu.VMEM((B, B), f32),
                        pltpu.VMEM((B, B), f32), pltpu.VMEM((B, B), bf16)],
        compiler_params=_cp(("parallel", "parallel", "arbitrary")),
    )(q, kv, kv, do, o, lse, fk)


def _fa_bwd_dkv(q, kv, do, delta, lse, fk, cfg):
    L, D, NH, DH, B = cfg.L, cfg.D, cfg.NH, cfg.DH, cfg.BQ
    nq = L // B
    scale = DH ** -0.5

    def body(q_ref, k_ref, v_ref, do_ref, dl_ref, lse_ref, fk_ref, dk_ref, dv_ref, dfk_ref,
             dka_ref, dva_ref, dfa_ref, s_ref, dp_ref, p_ref, ds_ref):
        ki, qi = pl.program_id(1), pl.program_id(2)

        @pl.when(qi == 0)
        def _():
            dka_ref[...] = jnp.zeros_like(dka_ref)
            dva_ref[...] = jnp.zeros_like(dva_ref)
            dfa_ref[...] = jnp.zeros_like(dfa_ref)

        def compute(masked):
            s_ref[...] = lax.dot_general(q_ref[...], k_ref[...], _DIMS["nt"], preferred_element_type=f32)
            dp_ref[...] = lax.dot_general(do_ref[...], v_ref[...], _DIMS["nt"], preferred_element_type=f32)
            fkv = fk_ref[...]

            def strip(rows, row0, cs):
                p = jnp.exp(s_ref[rows, :] - fkv - lse_ref[rows, :])
                if masked:
                    p = jnp.where(_fa_mask(row0, p.shape), p, 0.0)
                ds = p * (dp_ref[rows, :] - dl_ref[rows, :])
                p_ref[rows, :] = p.astype(bf16)
                ds_ref[rows, :] = ds.astype(bf16)
                return cs + ds
            cs = _fa_strips(B, strip, jnp.zeros((STRIP, B), f32))
            dva_ref[...] += lax.dot_general(p_ref[...], do_ref[...], _DIMS["tn"], preferred_element_type=f32)
            dka_ref[...] += lax.dot_general(ds_ref[...], q_ref[...], _DIMS["tn"], preferred_element_type=f32)
            dfa_ref[...] -= jnp.sum(cs, axis=0, keepdims=True)

        @pl.when(qi == ki)
        def _():
            compute(True)

        @pl.when(qi > ki)
        def _():
            compute(False)

        @pl.when(qi == nq - 1)
        def _():
            dk_ref[...] = (dka_ref[...] * scale).astype(dk_ref.dtype)
            dv_ref[...] = dva_ref[...].astype(dv_ref.dtype)
            dfk_ref[...] = dfa_ref[...]

    qmap = lambda h, ki, qi: (jnp.maximum(qi, ki), h)
    cmap = lambda h, ki, qi: (h, jnp.maximum(qi, ki), 0)
    return pl.pallas_call(
        body, name="attn_bwd_dkv", grid=(NH, nq, nq),
        in_specs=[pl.BlockSpec((B, DH), qmap),
                  pl.BlockSpec((B, DH), lambda h, ki, qi: (ki, h)),
                  pl.BlockSpec((B, DH), lambda h, ki, qi: (ki, NH + h)),
                  pl.BlockSpec((B, DH), qmap),
                  pl.BlockSpec((None, B, 1), cmap), pl.BlockSpec((None, B, 1), cmap),
                  pl.BlockSpec((None, 1, B), lambda h, ki, qi: (h, 0, ki))],
        out_specs=[pl.BlockSpec((B, DH), lambda h, ki, qi: (ki, h)), pl.BlockSpec((B, DH), lambda h, ki, qi: (ki, h)),
                   pl.BlockSpec((None, 1, B), lambda h, ki, qi: (h, 0, ki))],
        out_shape=[jax.ShapeDtypeStruct((L, D), bf16), jax.ShapeDtypeStruct((L, D), bf16),
                   jax.ShapeDtypeStruct((NH, 1, L), f32)],
        scratch_shapes=[pltpu.VMEM((B, DH), f32), pltpu.VMEM((B, DH), f32), pltpu.VMEM((1, B), f32),
                        pltpu.VMEM((B, B), f32), pltpu.VMEM((B, B), f32), pltpu.VMEM((B, B), bf16), pltpu.VMEM((B, B), bf16)],
        compiler_params=_cp(("parallel", "parallel", "arbitrary")),
    )(q, kv, kv, do, delta, lse, fk)


def _fox_logits(q, k, fqv, fkv, scale, masked):
    s = lax.dot_general(q, k, _DIMS["nt"], preferred_element_type=f32) * scale + fqv - fkv
    if masked:
        rows = lax.broadcasted_iota(jnp.int32, s.shape, 0)
        cols = lax.broadcasted_iota(jnp.int32, s.shape, 1)
        return s, cols <= rows
    return s, None


def _fox_fwd(q, kv, fq, fk, cfg):
    L, D, NH, DH, B = cfg.L, cfg.D, cfg.NH, cfg.DH, cfg.BQ
    nq = L // B
    scale = DH ** -0.5

    def body(q_ref, k_ref, v_ref, fq_ref, fk_ref, o_ref, lse_ref):
        qi = pl.program_id(1)
        qv, fqv = q_ref[...], fq_ref[...]

        def chunk(kj, carry, masked):
            m, l, acc = carry
            rows = pl.ds(pl.multiple_of(kj * B, B), B)
            s, mask = _fox_logits(qv, k_ref[rows, :], fqv, fk_ref[kj], scale, masked)
            if masked:
                s = jnp.where(mask, s, NEG)
            m_new = jnp.maximum(m, jnp.max(s, axis=1, keepdims=True))
            alpha = jnp.exp(m - m_new)
            p = jnp.exp(s - m_new)
            l = alpha * l + jnp.sum(p, axis=1, keepdims=True)
            acc = alpha * acc + jnp.dot(p.astype(bf16), v_ref[rows, :], preferred_element_type=f32)
            return m_new, l, acc

        init = (jnp.full((B, 1), NEG, f32), jnp.zeros((B, 1), f32), jnp.zeros((B, DH), f32))
        carry = lax.fori_loop(0, qi, lambda kj, c: chunk(kj, c, False), init)
        m, l, acc = chunk(qi, carry, True)
        o_ref[...] = (acc / l).astype(o_ref.dtype)
        lse_ref[...] = m + jnp.log(l)

    return pl.pallas_call(
        body, name="attn_fwd", grid=(NH, nq),
        in_specs=[pl.BlockSpec((B, DH), lambda h, qi: (qi, h)),
                  pl.BlockSpec((L, DH), lambda h, qi: (0, h)), pl.BlockSpec((L, DH), lambda h, qi: (0, NH + h)),
                  pl.BlockSpec((None, B, 1), lambda h, qi: (h, qi, 0)),
                  pl.BlockSpec((None, nq, 1, B), lambda h, qi: (h, 0, 0, 0))],
        out_specs=[pl.BlockSpec((B, DH), lambda h, qi: (qi, h)), pl.BlockSpec((None, B, 1), lambda h, qi: (h, qi, 0))],
        out_shape=[jax.ShapeDtypeStruct((L, D), bf16), jax.ShapeDtypeStruct((NH, L, 1), f32)],
        compiler_params=_cp(("parallel", "arbitrary")),
    )(q, kv, kv, fq, fk)


def _fox_bwd_dq(q, kv, do, o, lse, fq, fk, cfg):
    L, D, NH, DH, B = cfg.L, cfg.D, cfg.NH, cfg.DH, cfg.BQ
    nq = L // B
    scale = DH ** -0.5

    def body(q_ref, k_ref, v_ref, do_ref, o_ref, lse_ref, fq_ref, fk_ref, dq_ref, dfq_ref, dl_ref):
        qi = pl.program_id(1)
        qv, fqv, dov, lsev = q_ref[...], fq_ref[...], do_ref[...], lse_ref[...]
        delta = jnp.sum(dov.astype(f32) * o_ref[...].astype(f32), axis=1, keepdims=True)

        def chunk(kj, carry, masked):
            acc, df = carry
            rows = pl.ds(pl.multiple_of(kj * B, B), B)
            kv_ = k_ref[rows, :]
            s, mask = _fox_logits(qv, kv_, fqv, fk_ref[kj], scale, masked)
            p = jnp.exp(s - lsev)
            if masked:
                p = jnp.where(mask, p, 0.0)
            dp = lax.dot_general(dov, v_ref[rows, :], _DIMS["nt"], preferred_element_type=f32)
            ds = p * (dp - delta)
            return acc + jnp.dot(ds.astype(bf16), kv_, preferred_element_type=f32), df + jnp.sum(ds, axis=1, keepdims=True)

        carry = lax.fori_loop(0, qi, lambda kj, c: chunk(kj, c, False), (jnp.zeros((B, DH), f32), jnp.zeros((B, 1), f32)))
        acc, df = chunk(qi, carry, True)
        dq_ref[...] = (acc * scale).astype(dq_ref.dtype)
        dfq_ref[...] = df
        dl_ref[...] = delta

    qmap = lambda h, qi: (qi, h)
    cmap = lambda h, qi: (h, qi, 0)
    return pl.pallas_call(
        body, name="attn_bwd_dq", grid=(NH, nq),
        in_specs=[pl.BlockSpec((B, DH), qmap),
                  pl.BlockSpec((L, DH), lambda h, qi: (0, h)), pl.BlockSpec((L, DH), lambda h, qi: (0, NH + h)),
                  pl.BlockSpec((B, DH), qmap), pl.BlockSpec((B, DH), qmap),
                  pl.BlockSpec((None, B, 1), cmap), pl.BlockSpec((None, B, 1), cmap),
                  pl.BlockSpec((None, nq, 1, B), lambda h, qi: (h, 0, 0, 0))],
        out_specs=[pl.BlockSpec((B, DH), qmap), pl.BlockSpec((None, B, 1), cmap), pl.BlockSpec((None, B, 1), cmap)],
        out_shape=[jax.ShapeDtypeStruct((L, D), bf16), jax.ShapeDtypeStruct((NH, L, 1), f32),
                   jax.ShapeDtypeStruct((NH, L, 1), f32)],
        compiler_params=_cp(("parallel", "arbitrary")),
    )(q, kv, kv, do, o, lse, fq, fk)


def _fox_bwd_dkv(q, kv, do, delta, lse, fq, fk, cfg):
    L, D, NH, DH, B = cfg.L, cfg.D, cfg.NH, cfg.DH, cfg.BQ
    nq = L // B
    scale = DH ** -0.5

    def body(q_ref, k_ref, v_ref, do_ref, dl_ref, lse_ref, fq_ref, fk_ref, dk_ref, dv_ref, dfk_ref):
        ki = pl.program_id(1)
        kv_, vv, fkv = k_ref[...], v_ref[...], fk_ref[...]

        def block(qj, carry, masked):
            dk, dv, df = carry
            rows = pl.ds(pl.multiple_of(qj * B, B), B)
            qv, dov = q_ref[rows, :], do_ref[rows, :]
            s, mask = _fox_logits(qv, kv_, fq_ref[rows, :], fkv, scale, masked)
            p = jnp.exp(s - lse_ref[rows, :])
            if masked:
                p = jnp.where(mask, p, 0.0)
            dv = dv + lax.dot_general(p.astype(bf16), dov, _DIMS["tn"], preferred_element_type=f32)
            dp = lax.dot_general(dov, vv, _DIMS["nt"], preferred_element_type=f32)
            ds = p * (dp - dl_ref[rows, :])
            dk = dk + lax.dot_general(ds.astype(bf16), qv, _DIMS["tn"], preferred_element_type=f32)
            return dk, dv, df - jnp.sum(ds, axis=0, keepdims=True)

        init = (jnp.zeros((B, DH), f32), jnp.zeros((B, DH), f32), jnp.zeros((1, B), f32))
        carry = block(ki, init, True)
        dk, dv, df = lax.fori_loop(ki + 1, nq, lambda qj, c: block(qj, c, False), carry)
        dk_ref[...] = (dk * scale).astype(dk_ref.dtype)
        dv_ref[...] = dv.astype(dv_ref.dtype)
        dfk_ref[...] = df

    whole = lambda h, ki: (0, h)
    col = lambda h, ki: (h, 0, 0)
    return pl.pallas_call(
        body, name="attn_bwd_dkv", grid=(NH, nq),
        in_specs=[pl.BlockSpec((L, DH), whole),
                  pl.BlockSpec((B, DH), lambda h, ki: (ki, h)), pl.BlockSpec((B, DH), lambda h, ki: (ki, NH + h)),
                  pl.BlockSpec((L, DH), whole),
                  pl.BlockSpec((None, L, 1), col), pl.BlockSpec((None, L, 1), col), pl.BlockSpec((None, L, 1), col),
                  pl.BlockSpec((None, None, 1, B), lambda h, ki: (h, ki, 0, 0))],
        out_specs=[pl.BlockSpec((B, DH), lambda h, ki: (ki, h)), pl.BlockSpec((B, DH), lambda h, ki: (ki, h)),
                   pl.BlockSpec((None, None, 1, B), lambda h, ki: (h, ki, 0, 0))],
        out_shape=[jax.ShapeDtypeStruct((L, D), bf16), jax.ShapeDtypeStruct((L, D), bf16),
                   jax.ShapeDtypeStruct((NH, nq, 1, B), f32)],
        compiler_params=_cp(("parallel", "arbitrary")),
    )(q, kv, kv, do, delta, lse, fq, fk)


FCH = 256


def _split3(x):
    hi = x.astype(bf16)
    r1 = x - hi.astype(f32)
    mid = r1.astype(bf16)
    lo = (r1 - mid.astype(f32)).astype(bf16)
    return hi, mid, lo


def _tri_sum(tri, x):
    hi, mid, lo = _split3(x)
    return (jnp.dot(tri, hi, preferred_element_type=f32) + jnp.dot(tri, mid, preferred_element_type=f32)
            + jnp.dot(tri, lo, preferred_element_type=f32))


def _fgate_fwd(z, fb, cfg):
    L = cfg.L

    def body(z_ref, fb_ref, f_ref):
        r = lax.broadcasted_iota(jnp.int32, (FCH, FCH), 0)
        c = lax.broadcasted_iota(jnp.int32, (FCH, FCH), 1)
        tri = (c <= r).astype(bf16)
        carry = jnp.zeros((1, LANES), f32)
        for ch in range(L // FCH):
            x = z_ref[pl.ds(ch * FCH, FCH), :] + fb_ref[...]
            lf = jnp.minimum(x, 0.0) - jnp.log(1.0 + jnp.exp(-jnp.abs(x)))
            f_ref[pl.ds(ch * FCH, FCH), :] = _tri_sum(tri, lf) + carry
            carry = f_ref[pl.ds(ch * FCH + FCH - 1, 1), :]

    vm = pl.BlockSpec(memory_space=pltpu.VMEM)
    return pl.pallas_call(body, name="fgate_fwd", in_specs=[vm, vm], out_specs=vm,
                          out_shape=jax.ShapeDtypeStruct((L, LANES), f32), compiler_params=_cp())(z, fb)


def _fgate_bwd(df, z, fb, cfg):
    L = cfg.L

    def body(df_ref, z_ref, fb_ref, dz_ref, db_ref):
        r = lax.broadcasted_iota(jnp.int32, (FCH, FCH), 0)
        c = lax.broadcasted_iota(jnp.int32, (FCH, FCH), 1)
        tri = (c >= r).astype(bf16)
        carry = jnp.zeros((1, LANES), f32)
        dbs = jnp.zeros((1, LANES), f32)
        for ch in range(L // FCH - 1, -1, -1):
            suf = _tri_sum(tri, df_ref[pl.ds(ch * FCH, FCH), :]) + carry
            x = z_ref[pl.ds(ch * FCH, FCH), :] + fb_ref[...]
            dz = suf * _sigmoid(-x)
            dz_ref[pl.ds(ch * FCH, FCH), :] = dz
            dbs = dbs + _colsum(dz)
            carry = carry + _colsum(df_ref[pl.ds(ch * FCH, FCH), :])
        db_ref[...] = dbs

    vm = pl.BlockSpec(memory_space=pltpu.VMEM)
    return pl.pallas_call(body, name="fgate_bwd", in_specs=[vm, vm, vm], out_specs=[vm, vm],
                          out_shape=[jax.ShapeDtypeStruct((L, LANES), f32), jax.ShapeDtypeStruct((1, LANES), f32)],
                          compiler_params=_cp())(df, z, fb)


def _adamw(w, g, m, v, name):
    R, C = w.shape
    c1 = 1.0 - ADAM_B1 ** ADAM_STEP
    c2 = 1.0 - ADAM_B2 ** ADAM_STEP

    def fn(i, ni, wv, gv, mv, vv):
        mn = ADAM_B1 * mv + (1.0 - ADAM_B1) * gv
        vn = ADAM_B2 * vv + (1.0 - ADAM_B2) * (gv * gv)
        delta = -ADAM_LR * ((mn / c1) / (jnp.sqrt(vn / c2) + ADAM_EPS) + ADAM_WD * wv)
        return [delta, mn, vn], []
    tc = C if C % LANES else _tile(C, 1024)
    return _rowwise(fn, [(w, "rc"), (g, "rc"), (m, "rc"), (v, "rc")], [f32, f32, f32], [], L=R, C=C, tl=512, tc=tc, name=name)


def _sum_lead(x, out_dtype, name):
    n, R, C = x.shape
    tl = _tile(R, 512, HALO)
    tc = C if C % LANES else _tile(C, 1024)

    def body(x_ref, o_ref):
        acc = x_ref[0].astype(f32)
        for k in range(1, n):
            acc = acc + x_ref[k].astype(f32)
        o_ref[...] = acc.astype(o_ref.dtype)

    return pl.pallas_call(
        body, name=name, grid=(R // tl, C // tc),
        in_specs=[pl.BlockSpec((n, tl, tc), lambda i, j: (0, i, j))], out_specs=pl.BlockSpec((tl, tc), lambda i, j: (i, j)),
        out_shape=jax.ShapeDtypeStruct((R, C), out_dtype), compiler_params=_cp(("parallel", "parallel")),
    )(x)


def _add_own_half(g, got, core, name):
    _, _, R, C = g.shape
    tl = _tile(R, 512, HALO)
    tc = C if C % LANES else _tile(C, 1024)

    def body(core_ref, g_ref, got_ref, o_ref):
        o_ref[...] = (g_ref[...].astype(f32) + got_ref[...].astype(f32)).astype(o_ref.dtype)

    blk = pl.BlockSpec((None, tl, tc), lambda k, i, j, co: (k, i, j))
    return pl.pallas_call(
        body, name=name,
        grid_spec=pltpu.PrefetchScalarGridSpec(
            num_scalar_prefetch=1, grid=(4, R // tl, C // tc),
            in_specs=[pl.BlockSpec((None, None, tl, tc), lambda k, i, j, co: (k, co[0], i, j)), blk], out_specs=blk),
        out_shape=jax.ShapeDtypeStruct((4, R, C), bf16), compiler_params=_cp(("parallel", "parallel", "parallel")),
    )(core, g, got)


ANY = pl.BlockSpec(memory_space=pl.ANY)
LOCAL_CHUNKS = 4


def _place():
    x, y, c = lax.axis_index("x"), lax.axis_index("y"), lax.axis_index("c")
    return x, y, c


def _allgather8(blocks, name):
    return _run_rider(_gather_rider(blocks), name)


def _gather_rider(blocks, middle_at=(1, 2)):
    n = len(blocks)

    def steps(ins, outs, sems):
        send_sems, recv_sems, local_sems = sems
        x, y, c = _place()
        me, sibling = (x, y, c), (x, y, 1 - c)
        chips = [(1 - x, y), (x, 1 - y), (1 - x, 1 - y)]

        def slot(a, dev):
            return outs[a].at[4 * dev[0] + 2 * dev[1] + dev[2]]

        def copy(a, k, block, to, src=None):
            return pltpu.make_async_remote_copy(
                src_ref=slot(a, block) if src is None else src, dst_ref=slot(a, block),
                send_sem=send_sems.at[a * 7 + k], recv_sem=recv_sems.at[a * 7 + k], device_id=to, device_id_type=MESH)

        def mine():
            out = []
            for a in range(n):
                rows = blocks[a].shape[0]
                k = LOCAL_CHUNKS if rows % (LOCAL_CHUNKS * HALO) == 0 else 1
                for i in range(k):
                    piece = pl.ds(i * (rows // k), rows // k)
                    out.append(pltpu.make_async_copy(ins[a].at[piece], slot(a, me).at[piece], local_sems.at[a * LOCAL_CHUNKS + i]))
            return out

        def first():
            out = []
            for a in range(n):
                out.append(copy(a, 0, me, sibling, src=ins[a]))
                out += [copy(a, 1 + j, me, (*chip, c), src=ins[a]) for j, chip in enumerate(chips)]
            return out

        def passed():
            return [copy(a, 4 + j, (*chip, c), sibling) for j, chip in enumerate(chips) for a in range(n)]

        def start():
            for cp in mine() + first():
                cp.start()

        def middle():
            for j, chip in enumerate(chips):
                for a in range(n):
                    copy(a, 1 + j, (*chip, c), me).wait_recv()
                    copy(a, 4 + j, (*chip, c), sibling).start()

        def finish():
            for a in range(n):
                copy(a, 0, sibling, me).wait_recv()
            for j, chip in enumerate(chips):
                for a in range(n):
                    copy(a, 4 + j, (*chip, 1 - c), me).wait_recv()
            for cp in first() + passed():
                cp.wait_send()
            for cp in mine():
                cp.wait()
        return start, middle, finish

    return dict(ins=list(blocks), out_shapes=[jax.ShapeDtypeStruct((N_DEV,) + b.shape, b.dtype) for b in blocks],
                sems=[pltpu.SemaphoreType.DMA((7 * n,)), pltpu.SemaphoreType.DMA((7 * n,)),
                      pltpu.SemaphoreType.DMA((LOCAL_CHUNKS * n,))],
                steps=steps, middle_at=middle_at)


def _run_rider(rider, name):
    ni, no = len(rider["ins"]), len(rider["out_shapes"])

    def body(*refs):
        start, middle, finish = rider["steps"](refs[:ni], refs[ni:ni + no], refs[ni + no:])
        start()
        if middle is not None:
            middle()
        finish()

    outs = pl.pallas_call(body, name=name, in_specs=[ANY] * ni, out_specs=[ANY] * no, out_shape=rider["out_shapes"],
                          scratch_shapes=rider["sems"])(*rider["ins"])
    return list(outs)


def _host_call(body, *, name, grid, in_specs, out_specs, out_shape, scratch_shapes, args, prefetch=(), rider=None):
    npre, nin, nout, nscr = len(prefetch), len(in_specs), len(out_specs), len(scratch_shapes)
    r_in, r_out, r_scr = (rider["ins"], rider["out_shapes"], rider["sems"]) if rider else ([], [], [])
    nri, nro = len(r_in), len(r_out)

    def kern(*refs):
        pre, rest = refs[:npre], refs[npre:]
        cin, rin = rest[:nin], rest[nin:nin + nri]
        o0 = nin + nri
        cout, rout = rest[o0:o0 + nout], rest[o0 + nout:o0 + nout + nro]
        s0 = o0 + nout + nro
        cscr, rscr = rest[s0:s0 + nscr], rest[s0 + nscr:]
        if rider:
            ids = [pl.program_id(d) for d in range(len(grid))]
            rest_zero = functools.reduce(jnp.logical_and, [i == 0 for i in ids[1:]], True)
            start, middle, finish = rider["steps"](rin, rout, rscr)
            pl.when(jnp.logical_and(ids[0] == 0, rest_zero))(start)
            if middle is not None:
                num, den = rider.get("middle_at", (1, 2))
                pl.when(jnp.logical_and(ids[0] == grid[0] * num // den, rest_zero))(middle)
        body(*pre, *cin, *cout, *cscr)
        if rider:
            pl.when(functools.reduce(jnp.logical_and, [i == g - 1 for i, g in zip(ids, grid)]))(finish)

    res = pl.pallas_call(
        kern, name=name,
        grid_spec=pltpu.PrefetchScalarGridSpec(num_scalar_prefetch=npre, grid=grid, in_specs=list(in_specs) + [ANY] * nri,
                                               out_specs=list(out_specs) + [ANY] * nro,
                                               scratch_shapes=list(scratch_shapes) + list(r_scr)),
        out_shape=list(out_shape) + list(r_out),
        compiler_params=_cp(("arbitrary",) * len(grid) if rider else ("parallel",) + ("arbitrary",) * (len(grid) - 1)),
    )(*prefetch, *args, *r_in)
    return list(res[:nout]), list(res[nout:])


def _sibling_send(halves, name):
    n = len(halves)

    def body(*refs):
        ins, outs = refs[:n], refs[n:2 * n]
        send_sems, recv_sems = refs[2 * n:]
        x, y, c = _place()
        sends = [pltpu.make_async_remote_copy(src_ref=ins[a], dst_ref=outs[a], send_sem=send_sems.at[a],
                                              recv_sem=recv_sems.at[a], device_id=(x, y, 1 - c), device_id_type=MESH)
                 for a in range(n)]
        for cp in sends:
            cp.start()
        for cp in sends:
            cp.wait_recv()
        for cp in sends:
            cp.wait_send()

    outs = pl.pallas_call(
        body, name=name, in_specs=[ANY] * n, out_specs=[ANY] * n,
        out_shape=[jax.ShapeDtypeStruct(h.shape, h.dtype) for h in halves],
        scratch_shapes=[pltpu.SemaphoreType.DMA((n,)), pltpu.SemaphoreType.DMA((n,))],
    )(*halves)
    return list(outs)


def _sibling_swap_halves(grads, name):
    n = len(grads)

    def body(*refs):
        ins, outs = refs[:n], refs[n:2 * n]
        send_sems, recv_sems = refs[2 * n:]
        x, y, c = _place()
        sends = [pltpu.make_async_remote_copy(src_ref=ins[a].at[:, 1 - c], dst_ref=outs[a], send_sem=send_sems.at[a],
                                              recv_sem=recv_sems.at[a], device_id=(x, y, 1 - c), device_id_type=MESH)
                 for a in range(n)]
        for cp in sends:
            cp.start()
        for cp in sends:
            cp.wait_recv()
        for cp in sends:
            cp.wait_send()

    outs = pl.pallas_call(
        body, name=name, in_specs=[ANY] * n, out_specs=[ANY] * n,
        out_shape=[jax.ShapeDtypeStruct((4,) + g.shape[2:], g.dtype) for g in grads],
        scratch_shapes=[pltpu.SemaphoreType.DMA((n,)), pltpu.SemaphoreType.DMA((n,))],
    )(*grads)
    return list(outs)


def _chip_scatter(parts, name):
    return _run_rider(_scatter_rider(parts), name)


def _scatter_rider(parts):
    n = len(parts)

    def steps(ins, outs, sems):
        send_sems, recv_sems = sems
        x, y, c = _place()
        chips = [(1 - x, y), (x, 1 - y), (1 - x, 1 - y)]

        def sends():
            return [pltpu.make_async_remote_copy(
                src_ref=ins[a].at[2 * px + py], dst_ref=outs[a].at[j], send_sem=send_sems.at[a * 3 + j],
                recv_sem=recv_sems.at[a * 3 + j], device_id=(px, py, c), device_id_type=MESH)
                for a in range(n) for j, (px, py) in enumerate(chips)]

        def start():
            for cp in sends():
                cp.start()

        def finish():
            for cp in sends():
                cp.wait_recv()
            for cp in sends():
                cp.wait_send()
        return start, None, finish

    return dict(ins=list(parts), out_shapes=[jax.ShapeDtypeStruct((3,) + p.shape[1:], p.dtype) for p in parts],
                sems=[pltpu.SemaphoreType.DMA((3 * n,)), pltpu.SemaphoreType.DMA((3 * n,))], steps=steps)


def _sum_parts(own, got, chip, name):
    _, R, C = own.shape
    tl = _tile(R, 512, HALO)
    tc = C if C % LANES else _tile(C, 1024)

    def body(chip_ref, own_ref, got_ref, o_ref):
        acc = own_ref[...].astype(f32)
        for k in range(3):
            acc = acc + got_ref[k].astype(f32)
        o_ref[...] = acc

    return pl.pallas_call(
        body, name=name,
        grid_spec=pltpu.PrefetchScalarGridSpec(
            num_scalar_prefetch=1, grid=(R // tl, C // tc),
            in_specs=[pl.BlockSpec((None, tl, tc), lambda i, j, ch: (ch[0], i, j)),
                      pl.BlockSpec((3, tl, tc), lambda i, j, ch: (0, i, j))],
            out_specs=pl.BlockSpec((tl, tc), lambda i, j, ch: (i, j))),
        out_shape=jax.ShapeDtypeStruct((R, C), f32), compiler_params=_cp(("parallel", "parallel")),
    )(chip, own, got)


def _adamw_halves(w, m, v, g_mine, g_other, core, name, rider=None):
    NL, R, C = w.shape
    r = R // 2
    tl = _tile(r, 512, HALO)
    tc = C if C % LANES else _tile(C, 1024)
    nh = r // tl
    c1 = 1.0 - ADAM_B1 ** ADAM_STEP
    c2 = 1.0 - ADAM_B2 ** ADAM_STEP

    def body(core_ref, w_ref, m_ref, v_ref, gm_ref, go_ref, g_out, d_out, m_out, v_out):
        i = pl.program_id(1)
        mine = lax.div(i, nh) == core_ref[0]
        gv = jnp.where(mine, gm_ref[...], go_ref[...])
        mn = ADAM_B1 * m_ref[...] + (1.0 - ADAM_B1) * gv
        vn = ADAM_B2 * v_ref[...] + (1.0 - ADAM_B2) * (gv * gv)
        g_out[...] = gv
        d_out[...] = -ADAM_LR * ((mn / c1) / (jnp.sqrt(vn / c2) + ADAM_EPS) + ADAM_WD * w_ref[...])
        m_out[...] = mn
        v_out[...] = vn

    full = pl.BlockSpec((None, tl, tc), lambda l, i, j, co: (l, i, j))
    mine_spec = pl.BlockSpec((None, tl, tc), lambda l, i, j, co: (l, jnp.clip(i - co[0] * nh, 0, nh - 1), j))
    other_spec = pl.BlockSpec((None, tl, tc), lambda l, i, j, co: (l, jnp.clip(i - (1 - co[0]) * nh, 0, nh - 1), j))
    outs, extra = _host_call(
        body, name=name, grid=(NL, R // tl, C // tc), in_specs=[full, full, full, mine_spec, other_spec],
        out_specs=[full] * 4, out_shape=[jax.ShapeDtypeStruct((NL, R, C), f32)] * 4, scratch_shapes=[],
        args=(w, m, v, g_mine, g_other), prefetch=(core,), rider=rider)
    return (*outs, extra)


def _s5_discretize(log_step, a_re, a_im, b_re, b_im):
    step = jnp.exp(log_step)[:, None]
    mag = jnp.exp(a_re * step)
    abar_re = mag * jnp.cos(a_im * step)
    abar_im = mag * jnp.sin(a_im * step)
    den = a_re * a_re + a_im * a_im
    nr = abar_re - 1.0
    fr = (nr * a_re + abar_im * a_im) / den
    fi = (abar_im * a_re - nr * a_im) / den
    bbar_re = fr[..., None] * b_re - fi[..., None] * b_im
    bbar_im = fr[..., None] * b_im + fi[..., None] * b_re
    return abar_re, abar_im, bbar_re, bbar_im


def _s5_prepare(p, cfg):
    abar_re, abar_im, bbar_re, bbar_im = _s5_discretize(p["log_step"], p["a_re"], p["a_im"], p["b_re"], p["b_im"])
    step = jnp.exp(p["log_step"])[:, None]
    arow, tab = _s5_tables(abar_re, abar_im, p["a_re"], p["a_im"], step, cfg)
    bmat, cmat = _s5_mats(bbar_re, bbar_im, p["c_re"], p["c_im"], cfg)
    return dict(arow=arow, tab=tab, bmat=bmat, cmat=cmat, drow=p["d"].reshape(1, cfg.D))


def _s5_param_grads(p, dbmat, dcmat, dabar, dd, cfg):
    J, P = cfg.G // 8, cfg.P
    dbb_re, dbb_im, dc_re, dc_im = _s5_unmats(dbmat, dcmat, cfg)
    da = dabar.reshape(J, 2, 8, P)
    da_re, da_im = da[:, 0].reshape(cfg.G, P), da[:, 1].reshape(cfg.G, P)
    _, vjp = jax.vjp(_s5_discretize, p["log_step"], p["a_re"], p["a_im"], p["b_re"], p["b_im"])
    dls, dare, daim, dbre, dbim = vjp((da_re, da_im, dbb_re, dbb_im))
    return dict(log_step=dls, a_re=dare, a_im=daim, b_re=dbre, b_im=dbim, c_re=dc_re, c_im=dc_im, d=dd.reshape(cfg.G, cfg.H))


def _resid_epi(acc, xv, gv):
    return xv + gv * acc, acc


def _ffn_fwd(x_in, g_norm, sc, sh, gate, W, exch, conv_w, conv_b, cfg, tag):
    h = _norm_mod_fwd(x_in, g_norm, sc, sh, cfg, f"ffn_norm_{tag}")
    rider = exch.rider(f"ffn_up_{tag}")
    a = _mm(h, W[f"ffn_w_up{tag}"], mode="nn", b4=True, tn=1408, out_dtypes=(bf16,), name=f"ffn_up_{tag}", rider=rider)
    if rider:
        a, extra = a
        W.update(exch.done(f"ffn_up_{tag}", extra))
    act = _conv_act_fwd(a, conv_w, conv_b, cfg)
    rider = exch.rider(f"ffn_down_{tag}")
    res = _mm(act, W[f"ffn_w_down{tag}"], mode="nn", extras=[(x_in, "mn"), (gate, "n")], epi=_resid_epi,
              out_dtypes=(f32, bf16), name=f"ffn_down_{tag}", rider=rider)
    if rider:
        res, extra = res
        W.update(exch.done(f"ffn_down_{tag}", extra))
    x_out, out = res
    return x_out, dict(h=h, a=a, act=act, out=out)


def _ffn_bwd(dx, x_in, sv, g_norm, sc, gate, w_up4, w_down, conv_w, conv_b, cfg, tag):
    F = cfg.F
    dout, dgate = _gate_bwd(dx, sv["out"], gate, cfg, f"ffn_gate_bwd_{tag}")
    dact = _mm(dout, w_down, mode="nt", tn=1408, out_dtypes=(bf16,), name=f"ffn_dact_{tag}")
    dw_down = _mm(sv["act"], dout, mode="tn", tm=1408, out_dtypes=(bf16,), name=f"ffn_dwdown_{tag}")
    dcu, dcv, dwu, dwv, dbu, dbv = _conv_act_bwd1(dact, sv["a"], conv_w, conv_b, cfg)
    dau = _conv_bwd2(dcu, conv_w[:, :F], cfg, f"conv_bwd2u_{tag}")
    dav = _conv_bwd2(dcv, conv_w[:, F:], cfg, f"conv_bwd2v_{tag}")
    da = jnp.concatenate([dau, dav], axis=1)
    dh = _mm(da, w_up4, mode="nt", b4=True, tk=1408, out_dtypes=(bf16,), name=f"ffn_dh_{tag}")
    dw_up = _mm(sv["h"], da, mode="tn", out4=True, tn=1408, out_dtypes=(bf16,), name=f"ffn_dwup_{tag}")
    dx_in, A, B = _norm_mod_bwd(dh, x_in, g_norm, sc, dx, cfg, f"ffn_norm_bwd_{tag}")
    small = dict(norm_g=(1.0 + sc) * A, sc=g_norm * A, sh=B, gate=dgate,
                 conv_w=jnp.concatenate([dwu, dwv], axis=1), conv_b=jnp.concatenate([dbu, dbv], axis=1))
    return dx_in, dw_up, dw_down, small


class _NoExchange:
    def rider(self, key, grads=None):
        return None

    def done(self, key, extra):
        return {}


def _local_step(cfg, x, tgt, mod, W, sp, exch=None):
    D, NH = cfg.D, cfg.NH
    exch = exch or _NoExchange()
    W, big = dict(W), {}

    def hand_over(key, grads):
        rider = exch.rider(key, grads)
        if rider is None:
            big.update(grads)
        return rider
    row = lambda v: v.reshape(1, -1)
    nmg0, nmg1 = row(sp["norm_mix_g"][0]), row(sp["norm_mix_g"][1])
    nfg0, nfg1 = row(sp["norm_ffn_g"][0]), row(sp["norm_ffn_g"][1])
    kvg, fng = row(sp["kv_norm_g"]), row(sp["final_norm_g"])
    cw0, cw1 = sp["ffn_conv_w"][0], sp["ffn_conv_w"][1]
    cb0, cb1 = row(sp["ffn_conv_b"][0]), row(sp["ffn_conv_b"][1])
    glu_b = row(sp["ssm_glu_b"])
    fb = jnp.zeros((1, LANES), f32).at[0, :NH].set(sp["forget_b"])
    s5p = {k: sp["ssm_" + k][0] for k in ("log_step", "a_re", "a_im", "b_re", "b_im", "c_re", "c_im", "d")}
    s5 = _s5_prepare(s5p, cfg)
    m0, m1 = mod["l0"], mod["l1"]

    h0 = _norm_mod_fwd(x, nmg0, m0["sc_m"], m0["sh_m"], cfg, "mix_norm_0")
    u = _mm(h0, W["ssm_w_in"], mode="nn", name="ssm_in")
    y, gact, cin, extra = _s5_fwd(u, s5["bmat"], s5["cmat"], s5["drow"], s5["arow"], s5["tab"], cfg, exch.rider("s5_fwd"))
    W.update(exch.done("s5_fwd", extra))

    def glu_epi(acc, bv, gv):
        pre = acc + bv
        return pre, gv.astype(f32) * _sigmoid(pre)
    pre, z = _mm(gact, W["ssm_glu_w"], mode="nn", extras=[(glu_b, "n"), (gact, "mn")], epi=glu_epi,
                 out_dtypes=(f32, bf16), name="ssm_glu")
    x1, out_m0 = _mm(z, W["ssm_w_out"], mode="nn", extras=[(x, "mn"), (m0["g_m"], "n")], epi=_resid_epi,
                     out_dtypes=(f32, bf16), name="ssm_out")
    x2, ffn0 = _ffn_fwd(x1, nfg0, m0["sc_f"], m0["sh_f"], m0["g_f"], W, exch, cw0, cb0, cfg, "0")

    hk = _norm_mod_fwd(x2, kvg, mod["sc_kv"], mod["sh_kv"], cfg, "kv_norm")
    kvb = _mm(hk, W["kv_w"], mode="nn", out_dtypes=(bf16,), name="kv_proj")
    zf = _mm(hk, W["kv_wf"], mode="nn", name="kv_fproj")
    fc = _fgate_fwd(zf, fb, cfg)
    fct = fc[:, :NH].T
    fk = fct[:, None, :]

    h1 = _norm_mod_fwd(x2, nmg1, m1["sc_m"], m1["sh_m"], cfg, "mix_norm_1")
    q = _mm(h1, W["attn_w_q"], mode="nn", epi=lambda acc: (acc * cfg.DH ** -0.5,), out_dtypes=(bf16,), name="attn_q")
    o, lse, extra = _ta_fwd(q, kvb, fk, cfg, exch.rider("attn_fwd"))
    W.update(exch.done("attn_fwd", extra))
    x3, out_m1 = _mm(o, W["attn_w_out"], mode="nn", extras=[(x2, "mn"), (m1["g_m"], "n")], epi=_resid_epi,
                     out_dtypes=(f32, bf16), name="attn_out")
    x4, ffn1 = _ffn_fwd(x3, nfg1, m1["sc_f"], m1["sh_f"], m1["g_f"], W, exch, cw1, cb1, cfg, "1")

    dx, dfng, lcol = _final_loss(x4, fng, tgt, cfg)
    loss = (0.5 / D) * jnp.sum(lcol)

    dx, dw_up1, dw_down1, sf1 = _ffn_bwd(dx, x3, ffn1, nfg1, m1["sc_f"], m1["g_f"], W["ffn_w_up1"], W["ffn_w_down1"], cw1, cb1, cfg, "1")
    dout, dgm1 = _gate_bwd(dx, out_m1, m1["g_m"], cfg, "attn_gate_bwd")
    do = _mm(dout, W["attn_w_out"], mode="nt", out_dtypes=(bf16,), name="attn_do")
    dw_ao = _mm(o, dout, mode="tn", out_dtypes=(bf16,), name="attn_dwout")
    dq, dfq, delta, extra = _ta_bwd_dq(q, kvb, do, o, lse, fk, cfg,
                                       hand_over("attn_bwd", dict(ffn_w_up1=dw_up1, ffn_w_down1=dw_down1)))
    exch.done("attn_bwd", extra)
    dw_q = _mm(h1, dq, mode="tn", out_dtypes=(bf16,), name="attn_dwq")
    dk, dv, dfk, extra = _ta_bwd_dkv(q, kvb, do, delta, lse, fk, cfg,
                                     hand_over("attn_bwd_dkv", dict(attn_w_q=dw_q, attn_w_out=dw_ao)))
    exch.done("attn_bwd_dkv", extra)
    dh1 = _mm(dq, W["attn_w_q"], mode="nt", out_dtypes=(bf16,), name="attn_dh")
    dx, A1, B1 = _norm_mod_bwd(dh1, x2, nmg1, m1["sc_m"], dx, cfg, "mix_norm_bwd_1")

    dfc = jnp.pad((dfq[:, :, 0] + dfk[:, :, 0]).T, ((0, 0), (0, LANES - NH)))
    dzf, dfb = _fgate_bwd(dfc, zf, fb, cfg)
    dkv = jnp.concatenate([dk, dv], axis=1)
    dhk1 = _mm(dkv, W["kv_w"], mode="nt", name="kv_dh1")
    dhk = _mm(dzf, W["kv_wf"], mode="nt", extras=[(dhk1, "mn")], epi=lambda acc, e: (acc + e,), out_dtypes=(bf16,), name="kv_dh2")
    dw_kv = _mm(hk, dkv, mode="tn", out_dtypes=(bf16,), name="kv_dw")
    dw_kf = _mm(hk, dzf, mode="tn", out_dtypes=(bf16,), name="kv_dwf")
    dx, Ak, Bk = _norm_mod_bwd(dhk, x2, kvg, mod["sc_kv"], dx, cfg, "kv_norm_bwd")

    dx, dw_up0, dw_down0, sf0 = _ffn_bwd(dx, x1, ffn0, nfg0, m0["sc_f"], m0["g_f"], W["ffn_w_up0"], W["ffn_w_down0"], cw0, cb0, cfg, "0")
    dout, dgm0 = _gate_bwd(dx, out_m0, m0["g_m"], cfg, "ssm_gate_bwd")
    dz = _mm(dout, W["ssm_w_out"], mode="nt", out_dtypes=(bf16,), name="ssm_dz")
    dw_so = _mm(z, dout, mode="tn", out_dtypes=(bf16,), name="ssm_dwout")
    dpre, dgd, dglub = _glu_bwd(dz, gact, pre, cfg)
    dy = _mm(dpre, W["ssm_glu_w"], mode="nt", extras=[(dgd, "mn"), (y, "mn")],
             epi=lambda acc, e, yv: ((acc + e) * _gelu_grad(yv),), name="ssm_dy")
    dw_glu = _mm(gact, dpre, mode="tn", out_dtypes=(bf16,), name="ssm_dwglu")
    rider = hand_over("s5_bwd", dict(kv_w=jnp.concatenate([dw_kv, dw_kf[:, :NH]], axis=1), ffn_w_up0=dw_up0,
                                     ffn_w_down0=dw_down0, ssm_w_out=dw_so, ssm_glu_w=dw_glu))
    du, dbm, dcm, dab, dd, extra = _s5_bwd(u, dy, cin, s5["bmat"], s5["cmat"], s5["drow"], s5["arow"], s5["tab"], cfg, rider)
    exch.done("s5_bwd", extra)
    dh0 = _mm(du, W["ssm_w_in"], mode="nt", out_dtypes=(bf16,), name="ssm_dh")
    dw_in = _mm(h0, du, mode="tn", out_dtypes=(bf16,), name="ssm_dwin")
    dx, A0, B0 = _norm_mod_bwd(dh0, x, nmg0, m0["sc_m"], dx, cfg, "mix_norm_bwd_0")

    s5g = _s5_param_grads(s5p, dbm, dcm, dab, dd, cfg)
    big["ssm_w_in"] = dw_in
    small = dict(
        norm_mix_g=jnp.concatenate([(1.0 + m0["sc_m"]) * A0, (1.0 + m1["sc_m"]) * A1], axis=0),
        norm_ffn_g=jnp.concatenate([sf0["norm_g"], sf1["norm_g"]], axis=0),
        ssm_glu_b=dglub, kv_norm_g=(1.0 + mod["sc_kv"]) * Ak, forget_b=dfb[0, :NH],
        ffn_conv_w=jnp.stack([sf0["conv_w"], sf1["conv_w"]]), ffn_conv_b=jnp.concatenate([sf0["conv_b"], sf1["conv_b"]], axis=0),
        final_norm_g=dfng, **{"ssm_" + k: v[None] for k, v in s5g.items()})
    dmod = [jnp.concatenate([B0, nmg0 * A0, dgm0, sf0["sh"], sf0["sc"], sf0["gate"]], axis=1),
            jnp.concatenate([B1, nmg1 * A1, dgm1, sf1["sh"], sf1["sc"], sf1["gate"]], axis=1),
            jnp.concatenate([Bk, kvg * Ak], axis=1)]
    return loss, dx, big, small, dmod


WEIGHTS = ["mod_w", "mod_b", "norm_mix_g", "norm_ffn_g", "ssm_w_in", "ssm_log_step", "ssm_a_re", "ssm_a_im", "ssm_b_re",
           "ssm_b_im", "ssm_c_re", "ssm_c_im", "ssm_d", "ssm_glu_w", "ssm_glu_b", "ssm_w_out", "kv_mod_w", "kv_mod_b",
           "kv_norm_g", "kv_w", "forget_b", "attn_w_q", "attn_w_out", "ffn_w_up", "ffn_conv_w", "ffn_conv_b", "ffn_w_down",
           "final_norm_g"]
ARGS = ["x", "c"] + WEIGHTS + ["loss_target"] + ["m_" + n for n in WEIGHTS] + ["v_" + n for n in WEIGHTS]
SMALL = ["mod_b", "norm_mix_g", "norm_ffn_g", "ssm_log_step", "ssm_a_re", "ssm_a_im", "ssm_b_re", "ssm_b_im", "ssm_c_re",
         "ssm_c_im", "ssm_d", "ssm_glu_b", "kv_mod_b", "kv_norm_g", "forget_b", "ffn_conv_w", "ffn_conv_b", "final_norm_g"]
PACK_ROWS = 512


def _pack(arrs):
    flat = jnp.concatenate([a.reshape(-1).astype(f32) for a in arrs])
    unit = PACK_ROWS * LANES
    n = -(-flat.shape[0] // unit) * unit
    return jnp.pad(flat, (0, n - flat.shape[0])).reshape(-1, LANES)


def _unpack(packed, shapes):
    flat, out, off = packed.reshape(-1), [], 0
    for s in shapes:
        n = math.prod(s)
        out.append(flat[off:off + n].reshape(s))
        off += n
    return out


def _silu(v):
    return v * _sigmoid(v)


def _half(w, c, axis):
    r = w.shape[axis] // 2
    return lax.dynamic_slice_in_dim(w, c * r, r, axis=axis)


class _Exchange:
    FIRST = ["ssm_w_in", "ssm_glu_w", "ssm_w_out"]
    FWD = dict(s5_fwd=["ffn_w_up0"], ffn_up_0=["ffn_w_down0", "kv_w"], ffn_down_0=["attn_w_q", "attn_w_out"],
               attn_fwd=["ffn_w_up1", "ffn_w_down1"])
    LATE = (3, 4)

    def __init__(self, cfg, blocks, core):
        self.cfg, self.blocks, self.core = cfg, blocks, core
        self.parts, self.scattered, self.names = {}, {}, {}

    def weights(self, names, gathered):
        D, F, NH = self.cfg.D, self.cfg.F, self.cfg.NH
        W = {}
        for n, g in zip(names, gathered):
            if n.startswith("ffn_w_up"):
                W[n] = g.reshape(4, D, 2 * F // 4)
            elif n == "kv_w":
                full = g.reshape(4, D, -1).transpose(1, 0, 2).reshape(D, -1)
                W["kv_w"] = full[:, :2 * D]
                W["kv_wf"] = jnp.pad(full[:, 2 * D:], ((0, 0), (0, LANES - NH)))
            else:
                W[n] = g.reshape(-1, D)
        return W

    def sibling_sum(self, key, grads):
        D = self.cfg.D

        def blocks_of(n, g):
            if n.startswith("ffn_w_up"):
                return g.reshape(4, 2, D // 2, -1)
            if n == "kv_w":
                return g.reshape(D, 4, -1).transpose(1, 0, 2).reshape(4, 2, D // 2, -1)
            return g.reshape(4, 2, g.shape[0] // 8, g.shape[1])
        names = list(grads)
        gb = [blocks_of(n, grads[n]) for n in names]
        recv = _sibling_swap_halves(gb, f"grad_sibling_swap_{key}")
        core1 = jnp.reshape(self.core, (1,)).astype(jnp.int32)
        for n, g, r in zip(names, gb, recv):
            self.parts[n] = _add_own_half(g, r, core1, f"grad_add_{n}")
        return self.parts

    def rider(self, key, grads=None):
        if key in self.FWD:
            return _gather_rider([self.blocks[n] for n in self.FWD[key]], (1, 2) if key == "attn_fwd" else self.LATE)
        if grads is None:
            return None
        self.names[key] = list(grads)
        parts = self.sibling_sum(key, grads)
        return _scatter_rider([parts[n] for n in self.names[key]])

    def done(self, key, extra):
        if key in self.FWD:
            return self.weights(self.FWD[key], extra)
        self.scattered.update(zip(self.names[key], extra))
        return {}


def kernel(x, c, mod_w, mod_b, norm_mix_g, norm_ffn_g, ssm_w_in, ssm_log_step, ssm_a_re, ssm_a_im, ssm_b_re, ssm_b_im, ssm_c_re, ssm_c_im, ssm_d, ssm_glu_w, ssm_glu_b, ssm_w_out, kv_mod_w, kv_mod_b, kv_norm_g, kv_w, forget_b, attn_w_q, attn_w_out, ffn_w_up, ffn_conv_w, ffn_conv_b, ffn_w_down, final_norm_g, loss_target, m_mod_w, m_mod_b, m_norm_mix_g, m_norm_ffn_g, m_ssm_w_in, m_ssm_log_step, m_ssm_a_re, m_ssm_a_im, m_ssm_b_re, m_ssm_b_im, m_ssm_c_re, m_ssm_c_im, m_ssm_d, m_ssm_glu_w, m_ssm_glu_b, m_ssm_w_out, m_kv_mod_w, m_kv_mod_b, m_kv_norm_g, m_kv_w, m_forget_b, m_attn_w_q, m_attn_w_out, m_ffn_w_up, m_ffn_conv_w, m_ffn_conv_b, m_ffn_w_down, m_final_norm_g, v_mod_w, v_mod_b, v_norm_mix_g, v_norm_ffn_g, v_ssm_w_in, v_ssm_log_step, v_ssm_a_re, v_ssm_a_im, v_ssm_b_re, v_ssm_b_im, v_ssm_c_re, v_ssm_c_im, v_ssm_d, v_ssm_glu_w, v_ssm_glu_b, v_ssm_w_out, v_kv_mod_w, v_kv_mod_b, v_kv_norm_g, v_kv_w, v_forget_b, v_attn_w_q, v_attn_w_out, v_ffn_w_up, v_ffn_conv_w, v_ffn_conv_b, v_ffn_w_down, v_final_norm_g):
    a = dict(locals())
    assert list(a) == ARGS
    return _step(CFG, a)


def _step(cfg, a):
    D, F, NH = cfg.D, cfg.F, cfg.NH
    x_, y_, c_ = _place()
    chip, dev = 2 * x_ + y_, 4 * x_ + 2 * y_ + c_

    big_src = dict(ssm_w_in=a["ssm_w_in"][0], ssm_glu_w=a["ssm_glu_w"][0], ssm_w_out=a["ssm_w_out"][0],
                   attn_w_q=a["attn_w_q"][0], attn_w_out=a["attn_w_out"][0],
                   ffn_w_up0=a["ffn_w_up"][0], ffn_w_up1=a["ffn_w_up"][1],
                   ffn_w_down0=a["ffn_w_down"][0], ffn_w_down1=a["ffn_w_down"][1], kv_w=a["kv_w"])
    big_names = list(big_src)
    exch = _Exchange(cfg, {n: _half(big_src[n], c_, 0).astype(bf16) for n in big_names}, c_)
    first = exch.FIRST
    blocks = [exch.blocks[n] for n in first] + [_half(a["ssm_glu_b"], c_, 1), _half(a["ffn_conv_w"], c_, 2), a["c"]]
    got = _allgather8(blocks, "gather_weights")
    W = exch.weights(first, got)
    glu_b_full = got[-3].reshape(D)
    conv_w_full = got[-2].transpose(1, 2, 0, 3).reshape(2, 3, 2 * F)
    c16 = jnp.pad(got[-1].reshape(N_DEV, D), ((0, 16 - N_DEV), (0, 0)))

    mcols = [_mm(c16, a["mod_w"][l], mode="nn", a_pro=_silu, name=f"mod_fwd_{l}") for l in range(2)]
    mcols.append(_mm(c16, a["kv_mod_w"], mode="nn", a_pro=_silu, name="mod_fwd_kv"))
    widths = [m.shape[1] for m in mcols]
    mall = _allgather8([jnp.concatenate(mcols, axis=1)[:N_DEV]], "gather_mod")[0][0::2]
    offs = [0, widths[0], widths[0] + widths[1]]
    rows = []
    for off, wd, bias in zip(offs, widths, [a["mod_b"][0], a["mod_b"][1], a["kv_mod_b"]]):
        fullm = mall[:, :, off:off + wd].transpose(1, 0, 2).reshape(N_DEV, 4 * wd) + bias
        rows.append(lax.dynamic_slice_in_dim(fullm, dev, 1, axis=0))
    mod = {}
    for l in range(2):
        mod[f"l{l}"] = dict(zip(["sh_m", "sc_m", "g_m", "sh_f", "sc_f", "g_f"], jnp.split(rows[l], 6, axis=1)))
    mod["sh_kv"], mod["sc_kv"] = jnp.split(rows[2], 2, axis=1)

    sp = {n: a[n] for n in ["norm_mix_g", "norm_ffn_g", "kv_norm_g", "final_norm_g", "ffn_conv_b", "forget_b", "ssm_log_step",
                            "ssm_a_re", "ssm_a_im", "ssm_b_re", "ssm_b_im", "ssm_c_re", "ssm_c_im", "ssm_d"]}
    sp["ssm_glu_b"], sp["ffn_conv_w"] = glu_b_full, conv_w_full
    loss, dx, big, small, dmod = _local_step(cfg, a["x"][0], a["loss_target"][0], mod, W, sp, exch)
    loss = lax.psum(loss, ("x", "y", "c"))

    small["mod_b"] = jnp.concatenate([dmod[0], dmod[1]], axis=0)
    small["kv_mod_b"] = dmod[2]
    shapes = [(2, 6 * D) if n == "mod_b" else (1, D) if n == "ssm_glu_b" else (2, 3, 2 * F) if n == "ffn_conv_w"
              else a[n].shape for n in SMALL]
    small_rider = _gather_rider([_pack([small[n] for n in SMALL])])

    last = list(big)
    exch.scattered.update(zip(last, _chip_scatter([exch.sibling_sum("tail", big)[n] for n in last], "grad_chip_scatter")))
    chip1, core1 = jnp.reshape(chip, (1,)).astype(jnp.int32), jnp.reshape(c_, (1,)).astype(jnp.int32)
    mine = {n: _sum_parts(exch.parts[n], exch.scattered[n], chip1, f"grad_sum_{n}") for n in big_names}
    other = dict(zip(big_names, _sibling_send([mine[n] for n in big_names], "grad_sibling_send")))

    grads, delta, new_m, new_v = {}, {}, {}, {}
    members = dict(ffn_w_up=["ffn_w_up0", "ffn_w_up1"], ffn_w_down=["ffn_w_down0", "ffn_w_down1"], kv_w=["kv_w"],
                   ssm_w_in=["ssm_w_in"], ssm_glu_w=["ssm_glu_w"], ssm_w_out=["ssm_w_out"], attn_w_q=["attn_w_q"],
                   attn_w_out=["attn_w_out"])
    for n, parts_ in members.items():
        shp = a[n].shape
        three = lambda t: t.reshape(len(parts_), -1, shp[-1])
        g_, d_, m_, v_, extra = _adamw_halves(three(a[n]), three(a["m_" + n]), three(a["v_" + n]),
                                              jnp.stack([mine[p] for p in parts_]), jnp.stack([other[p] for p in parts_]),
                                              core1, f"adamw_{n}", small_rider if n == "ffn_w_up" else None)
        if extra:
            packs = extra[0]
        grads[n], delta[n], new_m[n], new_v[n] = g_.reshape(shp), d_.reshape(shp), m_.reshape(shp), v_.reshape(shp)

    gsmall = dict(zip(SMALL, _unpack(_sum_lead(packs, f32, "sum_small"), shapes)))
    per_dev = packs.reshape(N_DEV, -1)
    sizes = [math.prod(s) for s in shapes]
    starts = dict(zip(SMALL, [sum(sizes[:i]) for i in range(len(sizes))]))

    def rows_of(name, l, width):
        st = starts[name] + l * 6 * D
        blk = lax.dynamic_slice(per_dev, (0, st + chip * width), (N_DEV, width))
        return jnp.pad(blk, ((0, 16 - N_DEV), (0, 0)))
    g_mod_w = jnp.stack([_mm(c16, rows_of("mod_b", l, 6 * D // 4), mode="tn", a_pro=_silu, name=f"mod_dw_{l}") for l in range(2)])
    g_kv_mod_w = _mm(c16, rows_of("kv_mod_b", 0, 2 * D // 4), mode="tn", a_pro=_silu, name="mod_dw_kv")
    gsmall["ssm_glu_b"] = lax.dynamic_slice_in_dim(gsmall["ssm_glu_b"], chip * (D // 4), D // 4, axis=1)
    gsmall["ffn_conv_w"] = lax.dynamic_slice_in_dim(gsmall["ffn_conv_w"], chip * (2 * F // 4), 2 * F // 4, axis=2)

    grads.update(gsmall)
    grads["mod_w"], grads["kv_mod_w"] = g_mod_w, g_kv_mod_w
    for n in ["mod_w", "kv_mod_w"]:
        shp = a[n].shape
        two = lambda t: t.reshape(-1, shp[-1])
        d_, m_, v_ = _adamw(two(a[n]), two(grads[n]), two(a["m_" + n]), two(a["v_" + n]), f"adamw_{n}")
        delta[n], new_m[n], new_v[n] = d_.reshape(shp), m_.reshape(shp), v_.reshape(shp)
    grads = {n: grads[n].reshape(a[n].shape) for n in WEIGHTS}
    sshapes = [a[n].shape for n in SMALL]
    d_, m_, v_ = _adamw(_pack([a[n] for n in SMALL]), _pack([grads[n] for n in SMALL]), _pack([a["m_" + n] for n in SMALL]),
                        _pack([a["v_" + n] for n in SMALL]), "adamw_small")
    for n, dd_, mm_, vv_ in zip(SMALL, _unpack(d_, sshapes), _unpack(m_, sshapes), _unpack(v_, sshapes)):
        delta[n], new_m[n], new_v[n] = dd_, mm_, vv_

    return (loss, dx[None], *[grads[n] for n in WEIGHTS], *[delta[n] for n in WEIGHTS],
            *[new_m[n] for n in WEIGHTS], *[new_v[n] for n in WEIGHTS])
```

```python
import collections
import functools
import math

import jax
import jax.numpy as jnp
from jax import lax
from jax.experimental import pallas as pl
from jax.experimental.pallas import tpu as pltpu

f32 = jnp.float32
bf16 = jnp.bfloat16
MESH = pl.DeviceIdType.MESH

LANES = 128
SUBLANES = 8
VMEM_BYTES_V7X = 64 * 1024 * 1024
VMEM_LIMIT = 56 * 1024 * 1024

Cfg = collections.namedtuple("Cfg", "L D G P H NH DH F TC BQ")
CFG = Cfg(L=4096, D=2048, G=128, P=64, H=16, NH=16, DH=128, F=5632, TC=1024, BQ=1024)
NORM_EPS = 1e-6
ADAM_LR, ADAM_B1, ADAM_B2, ADAM_EPS, ADAM_WD, ADAM_STEP = 0.001, 0.9, 0.999, 1e-08, 0.01, 10
N_DEV = 8


def _cp(sem=None):
    return pltpu.CompilerParams(dimension_semantics=sem, vmem_limit_bytes=VMEM_LIMIT)


def _tile(dim, pref, unit=LANES):
    if dim <= pref:
        return dim
    t = (pref // unit) * unit
    while t > unit and dim % t:
        t -= unit
    assert dim % t == 0, (dim, pref)
    return t


_DIMS = {"nn": (((1,), (0,)), ((), ())), "nt": (((1,), (1,)), ((), ())), "tn": (((0,), (0,)), ((), ()))}


def _mm(a, b, *, mode, name, tm=1024, tn=1024, tk=2048, b4=False, out4=False, a_pro=None, extras=(), epi=None,
        out_dtypes=(f32,), rider=None):
    if mode == "tn":
        K, M = a.shape
    else:
        M, K = a.shape
    if b4:
        R, c4 = b.shape[1], b.shape[2]
        N = R if mode == "nt" else 4 * c4
        assert (K == 4 * c4) if mode == "nt" else (K == R)
    else:
        N = b.shape[0] if mode == "nt" else b.shape[1]
        assert K == (b.shape[1] if mode == "nt" else b.shape[0])
    n4 = N // 4
    tm = _tile(M, tm, LANES if mode == "tn" else SUBLANES * 2)
    tn = _tile(n4 if out4 or (b4 and mode != "nt") else N, tn)
    tk = _tile(b.shape[2] if (b4 and mode == "nt") else K, tk)
    nm, nn_, nk = M // tm, N // tn, K // tk

    a_spec = pl.BlockSpec((tk, tm), lambda i, j, k: (k, i)) if mode == "tn" else pl.BlockSpec((tm, tk), lambda i, j, k: (i, k))
    if b4 and mode == "nt":
        q = b.shape[2] // tk
        b_spec = pl.BlockSpec((None, tn, tk), lambda i, j, k: (lax.div(k, q), j, lax.rem(k, q)))
    elif b4:
        q = b.shape[2] // tn
        b_spec = pl.BlockSpec((None, tk, tn), lambda i, j, k: (lax.div(j, q), k, lax.rem(j, q)))
    elif mode == "nt":
        b_spec = pl.BlockSpec((tn, tk), lambda i, j, k: (j, k))
    else:
        b_spec = pl.BlockSpec((tk, tn), lambda i, j, k: (k, j))
    ex_specs = []
    for arr, kind in extras:
        if kind == "mn":
            ex_specs.append(pl.BlockSpec((tm, tn), lambda i, j, k: (i, j)))
        else:
            ex_specs.append(pl.BlockSpec((1, tn), lambda i, j, k: (0, j)))
    if out4:
        qo = n4 // tn
        o_spec = pl.BlockSpec((None, tm, tn), lambda i, j, k: (lax.div(j, qo), i, lax.rem(j, qo)))
        o_shapes = [jax.ShapeDtypeStruct((4, M, n4), dt) for dt in out_dtypes]
    else:
        o_spec = pl.BlockSpec((tm, tn), lambda i, j, k: (i, j))
        o_shapes = [jax.ShapeDtypeStruct((M, N), dt) for dt in out_dtypes]
    ne, no = len(extras), len(out_dtypes)
    dims = _DIMS[mode]

    def body(a_ref, b_ref, *rest):
        ex_refs, o_refs, acc_ref = rest[:ne], rest[ne:ne + no], rest[ne + no]
        k = pl.program_id(2)

        @pl.when(k == 0)
        def _():
            acc_ref[...] = jnp.zeros_like(acc_ref)

        av = a_ref[...]
        if a_pro is not None:
            av = a_pro(av)
        acc_ref[...] += lax.dot_general(av.astype(bf16), b_ref[...].astype(bf16), dims, preferred_element_type=f32)

        @pl.when(k == nk - 1)
        def _():
            acc = acc_ref[...]
            outs = (acc,) if epi is None else epi(acc, *[r[...] for r in ex_refs])
            for o_ref, o in zip(o_refs, outs):
                o_ref[...] = o.astype(o_ref.dtype)

    res, extra = _host_call(
        body, name=name, grid=(nm, nn_, nk), in_specs=[a_spec, b_spec] + ex_specs, out_specs=[o_spec] * no,
        out_shape=o_shapes, scratch_shapes=[pltpu.VMEM((tm, tn), f32)], args=(a, b, *[e[0] for e in extras]), rider=rider)
    res = res[0] if no == 1 else res
    return (res, extra) if rider else res


HALO = 16


def _rowwise(fn, ins, outs, accs, *, L, C, tl, tc, name):
    tl = _tile(L, tl, HALO)
    tc = _tile(C, tc)
    ni, nj = L // tl, C // tc
    hb = tl // HALO
    nh = L // HALO
    in_specs = []
    for spec in ins:
        kind = spec[1]
        off = spec[2] if len(spec) > 2 else 0
        if kind == "rc":
            in_specs.append(pl.BlockSpec((tl, tc), lambda j, i, off=off: (i, j + off)))
        elif kind == "c":
            in_specs.append(pl.BlockSpec((1, tc), lambda j, i, off=off: (0, j + off)))
        elif kind == "c3":
            in_specs.append(pl.BlockSpec((3, tc), lambda j, i, off=off: (0, j + off)))
        elif kind == "prev":
            in_specs.append(pl.BlockSpec((HALO, tc), lambda j, i, off=off: (jnp.maximum(i * hb - 1, 0), j + off)))
        elif kind == "next":
            in_specs.append(pl.BlockSpec((HALO, tc), lambda j, i, off=off: (jnp.minimum((i + 1) * hb, nh - 1), j + off)))
        else:
            raise ValueError(kind)
    out_specs = [pl.BlockSpec((tl, tc), lambda j, i: (i, j)) for _ in outs]
    out_specs += [pl.BlockSpec((r, tc), lambda j, i: (0, j)) for r in accs]
    out_shape = [jax.ShapeDtypeStruct((L, C), dt) for dt in outs] + [jax.ShapeDtypeStruct((r, C), f32) for r in accs]
    nin, nout, nacc = len(ins), len(outs), len(accs)

    def body(*refs):
        i = pl.program_id(1)
        tiles = [r[...] for r in refs[:nin]]
        o_vals, a_vals = fn(i, ni, *tiles)
        for r, v in zip(refs[nin:nin + nout], o_vals):
            r[...] = v.astype(r.dtype)
        if nacc:
            @pl.when(i == 0)
            def _():
                for r in refs[nin + nout:]:
                    r[...] = jnp.zeros_like(r)
            for r, v in zip(refs[nin + nout:], a_vals):
                r[...] += v

    res = pl.pallas_call(
        body, name=name, grid=(nj, ni), in_specs=in_specs, out_specs=out_specs, out_shape=out_shape,
        compiler_params=_cp(("parallel", "arbitrary")),
    )(*[s[0] for s in ins])
    return res


def _colsum(v):
    return jnp.sum(v, axis=0, keepdims=True)


def _sigmoid(x):
    return 1.0 / (1.0 + jnp.exp(-x))


_GELU_C = math.sqrt(2.0 / math.pi)


def _gelu(y):
    t = jnp.tanh(_GELU_C * (y + 0.044715 * y * y * y))
    return 0.5 * y * (1.0 + t)


def _gelu_grad(y):
    y2 = y * y
    t = jnp.tanh(_GELU_C * (y + 0.044715 * y * y2))
    return 0.5 * (1.0 + t) + 0.5 * y * (1.0 - t * t) * _GELU_C * (1.0 + 3.0 * 0.044715 * y2)


def _norm_mod_fwd(x, g, sc, sh, cfg, name):
    def fn(i, ni, xv, gv, scv, shv):
        rstd = lax.rsqrt(jnp.mean(xv * xv, axis=-1, keepdims=True) + NORM_EPS)
        return [xv * rstd * gv * (1.0 + scv) + shv], []
    return _rowwise(fn, [(x, "rc"), (g, "c"), (sc, "c"), (sh, "c")], [bf16], [], L=cfg.L, C=cfg.D, tl=256, tc=cfg.D, name=name)[0]


def _norm_mod_bwd(dh, x, g, sc, dres, cfg, name):
    def fn(i, ni, dhv, xv, gv, scv, *rest):
        dhv = dhv.astype(f32)
        rstd = lax.rsqrt(jnp.mean(xv * xv, axis=-1, keepdims=True) + NORM_EPS)
        xh = xv * rstd
        dxh = dhv * (gv * (1.0 + scv))
        dx = rstd * (dxh - xh * jnp.mean(dxh * xh, axis=-1, keepdims=True))
        if rest:
            dx = dx + rest[0]
        return [dx], [_colsum(dhv * xh), _colsum(dhv)]
    ins = [(dh, "rc"), (x, "rc"), (g, "c"), (sc, "c")] + ([(dres, "rc")] if dres is not None else [])
    return _rowwise(fn, ins, [f32], [1, 1], L=cfg.L, C=cfg.D, tl=256, tc=cfg.D, name=name)


def _final_loss(x, g, tgt, cfg):
    D = cfg.D

    def fn(i, ni, xv, gv, tv):
        rstd = lax.rsqrt(jnp.mean(xv * xv, axis=-1, keepdims=True) + NORM_EPS)
        xh = xv * rstd
        err = xh * gv - tv
        dy = err * (1.0 / D)
        dxh = dy * gv
        dx = rstd * (dxh - xh * jnp.mean(dxh * xh, axis=-1, keepdims=True))
        return [dx], [_colsum(dy * xh), _colsum(err * err)]
    return _rowwise(fn, [(x, "rc"), (g, "c"), (tgt, "rc")], [f32], [1, 1], L=cfg.L, C=D, tl=256, tc=D, name="final_loss")


def _gate_bwd(dx, out, gate, cfg, name):
    def fn(i, ni, dxv, ov, gv):
        return [dxv * gv], [_colsum(dxv * ov.astype(f32))]
    return _rowwise(fn, [(dx, "rc"), (out, "rc"), (gate, "c")], [bf16], [1], L=cfg.L, C=cfg.D, tl=512, tc=cfg.D, name=name)


def _glu_bwd(dz, g, pre, cfg):
    def fn(i, ni, dzv, gv, pv):
        dzv = dzv.astype(f32)
        gv = gv.astype(f32)
        s = _sigmoid(pv)
        dpre = dzv * gv * s * (1.0 - s)
        return [dpre, dzv * s], [_colsum(dpre)]
    return _rowwise(fn, [(dz, "rc"), (g, "rc"), (pre, "rc")], [bf16, f32], [1], L=cfg.L, C=cfg.D, tl=512, tc=cfg.D, name="glu_bwd")


def _shift_rows(av, pv, k, i):
    rows = lax.broadcasted_iota(jnp.int32, av.shape, 0)
    cur = pltpu.roll(av, k, 0)
    prev = pltpu.roll(pv, k, 0)
    prev = jnp.where(i > 0, prev, 0.0)
    prev_full = jnp.concatenate([prev, jnp.zeros((av.shape[0] - pv.shape[0], av.shape[1]), av.dtype)], axis=0) \
        if av.shape[0] > pv.shape[0] else prev
    return jnp.where(rows >= k, cur, prev_full)


def _shift_rows_up(av, nv, k, i, ni):
    n, h = av.shape[0], nv.shape[0]
    rows = lax.broadcasted_iota(jnp.int32, av.shape, 0)
    cur = pltpu.roll(av, n - k, 0)
    nxt = pltpu.roll(nv, h - k, 0)
    nxt = jnp.where(i < ni - 1, nxt, 0.0)
    nxt_full = jnp.concatenate([jnp.zeros((n - h, av.shape[1]), av.dtype), nxt], axis=0) if n > h else nxt
    return jnp.where(rows < n - k, cur, nxt_full)


def _conv3(av, pv, w, i):
    return w[0:1] * _shift_rows(av, pv, 2, i) + w[1:2] * _shift_rows(av, pv, 1, i) + w[2:3] * av


def _conv_act_fwd(a, conv_w, conv_b, cfg):
    F = cfg.F
    tc = _tile(F, 1408)
    nb = F // tc

    def fn(i, ni, au, av, pu, pv, wu, wv, bu, bv):
        cu = _conv3(au.astype(f32), pu.astype(f32), wu, i) + bu
        cv = _conv3(av.astype(f32), pv.astype(f32), wv, i) + bv
        return [cu * _sigmoid(cu) * cv], []
    ins = [(a, "rc"), (a, "rc", nb), (a, "prev"), (a, "prev", nb), (conv_w, "c3"), (conv_w, "c3", nb), (conv_b, "c"), (conv_b, "c", nb)]
    return _rowwise(fn, ins, [bf16], [], L=cfg.L, C=F, tl=512, tc=tc, name="conv_act_fwd")[0]


def _conv_act_bwd1(dact, a, conv_w, conv_b, cfg):
    F = cfg.F
    tc = _tile(F, 1408)
    nb = F // tc

    def fn(i, ni, dav, au, av, pu, pv, wu, wv, bu, bv):
        dav = dav.astype(f32)
        au, av, pu, pv = au.astype(f32), av.astype(f32), pu.astype(f32), pv.astype(f32)
        au1, au2 = _shift_rows(au, pu, 1, i), _shift_rows(au, pu, 2, i)
        av1, av2 = _shift_rows(av, pv, 1, i), _shift_rows(av, pv, 2, i)
        cu = wu[0:1] * au2 + wu[1:2] * au1 + wu[2:3] * au + bu
        cv = wv[0:1] * av2 + wv[1:2] * av1 + wv[2:3] * av + bv
        s = _sigmoid(cu)
        dcu = dav * cv * (s * (1.0 + cu * (1.0 - s)))
        dcv = dav * cu * s
        dwu = jnp.concatenate([_colsum(dcu * au2), _colsum(dcu * au1), _colsum(dcu * au)], axis=0)
        dwv = jnp.concatenate([_colsum(dcv * av2), _colsum(dcv * av1), _colsum(dcv * av)], axis=0)
        return [dcu, dcv], [dwu, dwv, _colsum(dcu), _colsum(dcv)]
    ins = [(dact, "rc"), (a, "rc"), (a, "rc", nb), (a, "prev"), (a, "prev", nb), (conv_w, "c3"), (conv_w, "c3", nb),
           (conv_b, "c"), (conv_b, "c", nb)]
    return _rowwise(fn, ins, [bf16, bf16], [3, 3, 1, 1], L=cfg.L, C=F, tl=512, tc=tc, name="conv_act_bwd1")


def _conv_bwd2(dc, w, cfg, name):
    F = cfg.F
    tc = _tile(F, 1408)

    def fn(i, ni, dcv, nxt, wv):
        dcv, nxt = dcv.astype(f32), nxt.astype(f32)
        return [wv[2:3] * dcv + wv[1:2] * _shift_rows_up(dcv, nxt, 1, i, ni) + wv[0:1] * _shift_rows_up(dcv, nxt, 2, i, ni)], []
    return _rowwise(fn, [(dc, "rc"), (dc, "next"), (w, "c3")], [bf16], [], L=cfg.L, C=F, tl=512, tc=tc, name=name)[0]


NSLAB = 8


def _s5_tables(abar_re, abar_im, lam_re, lam_im, step, cfg):
    J = cfg.G // 8
    expo = jnp.array([r + 1 for r in range(8)] + [8 * 2 ** p for p in range(8)], f32)[:, None, None]
    mag = jnp.exp(lam_re * step * expo)
    ang = lam_im * step * expo
    t_re = (mag * jnp.cos(ang)).reshape(16, J, 8 * cfg.P).transpose(1, 0, 2)
    t_im = (mag * jnp.sin(ang)).reshape(16, J, 8 * cfg.P).transpose(1, 0, 2)
    tab = jnp.concatenate([t_re, t_im], axis=-1)
    arow = jnp.concatenate([abar_re.reshape(J, 1, 8 * cfg.P), abar_im.reshape(J, 1, 8 * cfg.P)], axis=-1)
    return arow, tab


def _s5_mats(bbar_re, bbar_im, c_re, c_im, cfg):
    J, P, H = cfg.G // 8, cfg.P, cfg.H
    eye = jnp.eye(8, dtype=f32)

    def bd_in(bb):
        bb = bb.reshape(J, 8, P, H)
        return jnp.einsum("jgph,gk->jghkp", bb, eye).reshape(J, 8 * H, 8 * P)

    def bd_out(cc):
        cc = cc.reshape(J, 8, H, P)
        return jnp.einsum("jghp,gk->jgpkh", cc, eye).reshape(J, 8 * P, 8 * H)

    bmat = jnp.concatenate([bd_in(bbar_re), bd_in(bbar_im)], axis=2).astype(bf16)
    cmat = jnp.concatenate([bd_out(c_re), -bd_out(c_im)], axis=1).astype(bf16)
    return bmat, cmat


def _s5_unmats(dbmat, dcmat, cfg):
    J, P, H = cfg.G // 8, cfg.P, cfg.H
    eye = jnp.eye(8, dtype=f32)
    db = dbmat.reshape(J, 8, H, 2, 8, P)
    db = jnp.einsum("jghckp,gk->cjgph", db, eye).reshape(2, cfg.G, P, H)
    dc = dcmat.reshape(J, 2, 8, P, 8, H)
    dc = jnp.einsum("jcgpkh,gk->cjghp", dc, eye).reshape(2, cfg.G, H, P)
    return db[0], db[1], dc[0], -dc[1]


def _chunk_scan(x_ref, row0, nt, arow_ref, tab_ref, c0, reverse):
    sg = -1.0 if reverse else 1.0
    rows = lax.broadcasted_iota(jnp.int32, (nt, LANES), 0)
    order = list(range(7, -1, -1)) if reverse else list(range(8))

    def ld(k, r):
        return x_ref[k, pl.ds(row0 + r, nt, stride=8), :]

    def tab(row, k):
        return tab_ref[pl.ds(row, 1), pl.ds(k * LANES, LANES)]

    carries = [None] * NSLAB
    for k in range(4):
        ar = arow_ref[:, pl.ds(k * LANES, LANES)]
        ai = sg * arow_ref[:, pl.ds((4 + k) * LANES, LANES)]
        sr, si = ld(k, order[0]), ld(4 + k, order[0])
        for r in order[1:]:
            sr, si = ar * sr - ai * si + ld(k, r), ar * si + ai * sr + ld(4 + k, r)
        if reverse:
            cr = jnp.where(rows == nt - 1, c0[k], pltpu.roll(sr, nt - 1, 0))
            ci = jnp.where(rows == nt - 1, c0[4 + k], pltpu.roll(si, nt - 1, 0))
        else:
            cr = jnp.where(rows == 0, c0[k], pltpu.roll(sr, 1, 0))
            ci = jnp.where(rows == 0, c0[4 + k], pltpu.roll(si, 1, 0))
        d, p = 1, 0
        while d < nt:
            qr, qi = tab(8 + p, k), sg * tab(8 + p, 4 + k)
            if reverse:
                shr, shi, m = pltpu.roll(cr, nt - d, 0), pltpu.roll(ci, nt - d, 0), rows < nt - d
            else:
                shr, shi, m = pltpu.roll(cr, d, 0), pltpu.roll(ci, d, 0), rows >= d
            cr, ci = cr + jnp.where(m, qr * shr - qi * shi, 0.0), ci + jnp.where(m, qr * shi + qi * shr, 0.0)
            d, p = 2 * d, p + 1
        carries[k], carries[4 + k] = cr, ci
        sr, si = cr, ci
        for r in order:
            sr, si = ar * sr - ai * si + ld(k, r), ar * si + ai * sr + ld(4 + k, r)
            x_ref[k, pl.ds(row0 + r, nt, stride=8), :] = sr
            x_ref[4 + k, pl.ds(row0 + r, nt, stride=8), :] = si
    return carries


def _slabs_to_mat(x_ref, row0, n):
    return jnp.concatenate([x_ref[k, pl.ds(row0, n), :] for k in range(NSLAB)], axis=1)


def _mat_to_slabs(x_ref, row0, n, m):
    for k in range(NSLAB):
        x_ref[k, pl.ds(row0, n), :] = m[:, k * LANES:(k + 1) * LANES]


def _s5_fwd(u, bmat, cmat, drow, arow, tab, cfg, rider=None):
    L, D, Tc = cfg.L, cfg.D, cfg.TC
    J, NC, nt = cfg.G // 8, L // Tc, Tc // 8
    W = NSLAB * LANES

    def body(u_ref, b_ref, c_ref, d_ref, a_ref, t_ref, y_ref, g_ref, cin_ref, x_ref, st_ref):
        c = pl.program_id(1)

        @pl.when(c == 0)
        def _():
            st_ref[...] = jnp.zeros_like(st_ref)

        cin_ref[...] = st_ref[...]
        ub = u_ref[...]
        _mat_to_slabs(x_ref, 0, Tc, jnp.dot(ub.astype(bf16), b_ref[...], preferred_element_type=f32))
        c0 = [st_ref[:, pl.ds(k * LANES, LANES)] for k in range(NSLAB)]
        _chunk_scan(x_ref, 0, nt, a_ref, t_ref, c0, False)
        for k in range(NSLAB):
            st_ref[:, pl.ds(k * LANES, LANES)] = x_ref[k, pl.ds(Tc - 1, 1), :]
        s = _slabs_to_mat(x_ref, 0, Tc).astype(bf16)
        y = jnp.dot(s, c_ref[...], preferred_element_type=f32) + d_ref[...] * ub
        y_ref[...] = y
        g_ref[...] = _gelu(y).astype(bf16)

    outs, extra = _host_call(
        body, name="s5_fwd", grid=(J, NC), rider=rider, args=(u, bmat, cmat, drow, arow, tab),
        in_specs=[pl.BlockSpec((Tc, LANES), lambda j, c: (c, j)),
                  pl.BlockSpec((None, LANES, W), lambda j, c: (j, 0, 0)),
                  pl.BlockSpec((None, W, LANES), lambda j, c: (j, 0, 0)),
                  pl.BlockSpec((1, LANES), lambda j, c: (0, j)),
                  pl.BlockSpec((None, 1, W), lambda j, c: (j, 0, 0)),
                  pl.BlockSpec((None, 16, W), lambda j, c: (j, 0, 0))],
        out_specs=[pl.BlockSpec((Tc, LANES), lambda j, c: (c, j)),
                   pl.BlockSpec((Tc, LANES), lambda j, c: (c, j)),
                   pl.BlockSpec((None, None, 1, W), lambda j, c: (j, c, 0, 0))],
        out_shape=[jax.ShapeDtypeStruct((L, D), f32), jax.ShapeDtypeStruct((L, D), bf16),
                   jax.ShapeDtypeStruct((J, NC, 1, W), f32)],
        scratch_shapes=[pltpu.VMEM((NSLAB, Tc, LANES), f32), pltpu.VMEM((1, W), f32)])
    return (*outs, extra)


def _s5_bwd(u, dy, cin, bmat, cmat, drow, arow, tab, cfg, rider=None):
    L, D, Tc = cfg.L, cfg.D, cfg.TC
    J, NC, nt = cfg.G // 8, L // Tc, Tc // 8
    W = NSLAB * LANES
    PAD = 0

    def body(u_ref, dy_ref, cin_ref, b_ref, c_ref, d_ref, a_ref, t_ref,
             du_ref, db_ref, dc_ref, da_ref, dd_ref, s_ref, g_ref, gst_ref):
        c = pl.program_id(1)

        @pl.when(c == 0)
        def _():
            gst_ref[...] = jnp.zeros_like(gst_ref)
            db_ref[...] = jnp.zeros_like(db_ref)
            dc_ref[...] = jnp.zeros_like(dc_ref)
            da_ref[...] = jnp.zeros_like(da_ref)
            dd_ref[...] = jnp.zeros_like(dd_ref)

        ub, dyb = u_ref[...], dy_ref[...]
        ub16, dy16 = ub.astype(bf16), dyb.astype(bf16)
        _mat_to_slabs(s_ref, PAD, Tc, jnp.dot(ub16, b_ref[...], preferred_element_type=f32))
        c0 = [cin_ref[:, pl.ds(k * LANES, LANES)] for k in range(NSLAB)]
        tile_in = _chunk_scan(s_ref, PAD, nt, a_ref, t_ref, c0, False)
        _mat_to_slabs(g_ref, 0, Tc, lax.dot_general(dy16, c_ref[...], _DIMS["nt"], preferred_element_type=f32))
        g0 = [gst_ref[:, pl.ds(k * LANES, LANES)] for k in range(NSLAB)]
        _chunk_scan(g_ref, 0, nt, a_ref, t_ref, g0, True)
        for k in range(NSLAB):
            gst_ref[:, pl.ds(k * LANES, LANES)] = g_ref[k, pl.ds(0, 1), :]
        for k in range(4):
            acc_r = jnp.zeros((nt, LANES), f32)
            acc_i = jnp.zeros((nt, LANES), f32)
            for r in range(8):
                gr = g_ref[k, pl.ds(r, nt, stride=8), :]
                gi = g_ref[4 + k, pl.ds(r, nt, stride=8), :]
                if r == 0:
                    pr, pi = tile_in[k], tile_in[4 + k]
                else:
                    pr = s_ref[k, pl.ds(PAD + r - 1, nt, stride=8), :]
                    pi = s_ref[4 + k, pl.ds(PAD + r - 1, nt, stride=8), :]
                acc_r += gr * pr + gi * pi
                acc_i += gi * pr - gr * pi
            da_ref[:, pl.ds(k * LANES, LANES)] += _colsum(acc_r)
            da_ref[:, pl.ds((4 + k) * LANES, LANES)] += _colsum(acc_i)
        gm = _slabs_to_mat(g_ref, 0, Tc).astype(bf16)
        sm = _slabs_to_mat(s_ref, PAD, Tc).astype(bf16)
        du = lax.dot_general(gm, b_ref[...], _DIMS["nt"], preferred_element_type=f32) + d_ref[...] * dyb
        du_ref[...] = du.astype(bf16)
        db_ref[...] += lax.dot_general(ub16, gm, _DIMS["tn"], preferred_element_type=f32)
        dc_ref[...] += lax.dot_general(sm, dy16, _DIMS["tn"], preferred_element_type=f32)
        dd_ref[...] += _colsum(dyb * ub)

    rc = lambda j, c: (NC - 1 - c, j)
    outs, extra = _host_call(
        body, name="s5_bwd", grid=(J, NC), rider=rider, args=(u, dy, cin, bmat, cmat, drow, arow, tab),
        in_specs=[pl.BlockSpec((Tc, LANES), rc), pl.BlockSpec((Tc, LANES), rc),
                  pl.BlockSpec((None, None, 1, W), lambda j, c: (j, NC - 1 - c, 0, 0)),
                  pl.BlockSpec((None, LANES, W), lambda j, c: (j, 0, 0)),
                  pl.BlockSpec((None, W, LANES), lambda j, c: (j, 0, 0)),
                  pl.BlockSpec((1, LANES), lambda j, c: (0, j)),
                  pl.BlockSpec((None, 1, W), lambda j, c: (j, 0, 0)),
                  pl.BlockSpec((None, 16, W), lambda j, c: (j, 0, 0))],
        out_specs=[pl.BlockSpec((Tc, LANES), rc),
                   pl.BlockSpec((None, LANES, W), lambda j, c: (j, 0, 0)),
                   pl.BlockSpec((None, W, LANES), lambda j, c: (j, 0, 0)),
                   pl.BlockSpec((None, 1, W), lambda j, c: (j, 0, 0)),
                   pl.BlockSpec((1, LANES), lambda j, c: (0, j))],
        out_shape=[jax.ShapeDtypeStruct((L, D), bf16), jax.ShapeDtypeStruct((J, LANES, W), f32),
                   jax.ShapeDtypeStruct((J, W, LANES), f32), jax.ShapeDtypeStruct((J, 1, W), f32),
                   jax.ShapeDtypeStruct((1, D), f32)],
        scratch_shapes=[pltpu.VMEM((NSLAB, Tc + PAD, LANES), f32), pltpu.VMEM((NSLAB, Tc, LANES), f32),
                        pltpu.VMEM((1, W), f32)])
    return (*outs, extra)


NEG = -1e30


def _tri_tables(nq, by_key):
    pairs = [(qi, ki) for ki in range(nq) for qi in range(ki, nq)] if by_key else \
            [(qi, ki) for qi in range(nq) for ki in range(qi + 1)]
    return jnp.array([p[0] for p in pairs], jnp.int32), jnp.array([p[1] for p in pairs], jnp.int32)


def _tri_call(body, name, cfg, by_key, in_specs, out_specs, out_shape, scratch_shapes, args, rider=None):
    nq = cfg.L // cfg.BQ
    outs, extra = _host_call(body, name=name, grid=(cfg.NH, nq * (nq + 1) // 2), in_specs=in_specs, out_specs=out_specs,
                             out_shape=out_shape, scratch_shapes=scratch_shapes, args=args,
                             prefetch=_tri_tables(nq, by_key), rider=rider)
    return (*outs, extra)


def _ta_fwd(q, kv, fk, cfg, rider=None):
    L, D, NH, DH, B = cfg.L, cfg.D, cfg.NH, cfg.DH, cfg.BQ
    scale = DH ** -0.5

    def body(qt_ref, kt_ref, q_ref, k_ref, v_ref, fk_ref, o_ref, lse_ref, m_ref, acc_ref, a_ref, s_ref, p_ref):
        pid = pl.program_id(1)
        qi, ki = qt_ref[pid], kt_ref[pid]

        @pl.when(ki == 0)
        def _():
            m_ref[...] = jnp.full_like(m_ref, NEG)
            acc_ref[...] = jnp.zeros_like(acc_ref)

        def compute(masked):
            s_ref[...] = lax.dot_general(q_ref[...], k_ref[...], _DIMS["nt"], preferred_element_type=f32)
            fkv = fk_ref[...]

            def strip(rows, row0, c):
                t = s_ref[rows, :] - fkv
                if masked:
                    t = jnp.where(_fa_mask(row0, t.shape), t, NEG)
                m_prev = m_ref[rows, :]
                m_new = jnp.maximum(m_prev, jnp.max(t, axis=1, keepdims=True))
                m_ref[rows, :] = m_new
                a_ref[rows, :] = jnp.exp(m_prev - m_new)
                p_ref[rows, :] = jnp.exp(t - m_new).astype(bf16)
                return c
            _fa_strips(B, strip, 0)
            v1 = jnp.concatenate([v_ref[...], jnp.ones((B, DH), bf16)], axis=1)
            acc_ref[...] = a_ref[...] * acc_ref[...] + jnp.dot(p_ref[...], v1, preferred_element_type=f32)

        @pl.when(ki < qi)
        def _():
            compute(False)

        @pl.when(ki == qi)
        def _():
            compute(True)
            l = acc_ref[:, DH:]
            o_ref[...] = (acc_ref[:, :DH] / l).astype(o_ref.dtype)
            lse_ref[...] = m_ref[...] + jnp.log(l[:, :1])

    col = pltpu.VMEM((B, 1), f32)
    return _tri_call(
        body, "attn_fwd", cfg, False,
        [pl.BlockSpec((B, DH), lambda h, p, qt, kt: (qt[p], h)),
         pl.BlockSpec((B, DH), lambda h, p, qt, kt: (kt[p], h)),
         pl.BlockSpec((B, DH), lambda h, p, qt, kt: (kt[p], NH + h)),
         pl.BlockSpec((None, 1, B), lambda h, p, qt, kt: (h, 0, kt[p]))],
        [pl.BlockSpec((B, DH), lambda h, p, qt, kt: (qt[p], h)),
         pl.BlockSpec((None, B, 1), lambda h, p, qt, kt: (h, qt[p], 0))],
        [jax.ShapeDtypeStruct((L, D), bf16), jax.ShapeDtypeStruct((NH, L, 1), f32)],
        [col, pltpu.VMEM((B, 2 * DH), f32), col, pltpu.VMEM((B, B), f32), pltpu.VMEM((B, B), bf16)],
        (q, kv, kv, fk), rider)


def _ta_bwd_dq(q, kv, do, o, lse, fk, cfg, rider=None):
    L, D, NH, DH, B = cfg.L, cfg.D, cfg.NH, cfg.DH, cfg.BQ
    scale = DH ** -0.5

    def body(qt_ref, kt_ref, q_ref, k_ref, v_ref, do_ref, o_ref, lse_ref, fk_ref, dq_ref, dfq_ref, dl_ref,
             acc_ref, s_ref, dp_ref, ds_ref):
        pid = pl.program_id(1)
        qi, ki = qt_ref[pid], kt_ref[pid]

        @pl.when(ki == 0)
        def _():
            dl_ref[...] = jnp.sum(do_ref[...].astype(f32) * o_ref[...].astype(f32), axis=1, keepdims=True)
            acc_ref[...] = jnp.zeros_like(acc_ref)

        def compute(masked):
            s_ref[...] = lax.dot_general(q_ref[...], k_ref[...], _DIMS["nt"], preferred_element_type=f32)
            dp_ref[...] = lax.dot_general(do_ref[...], v_ref[...], _DIMS["nt"], preferred_element_type=f32)
            fkv = fk_ref[...]

            def strip(rows, row0, c):
                p = jnp.exp(s_ref[rows, :] - fkv - lse_ref[rows, :])
                if masked:
                    p = jnp.where(_fa_mask(row0, p.shape), p, 0.0)
                ds_ref[rows, :] = (p * (dp_ref[rows, :] - dl_ref[rows, :])).astype(bf16)
                return c
            _fa_strips(B, strip, 0)
            k1 = jnp.concatenate([k_ref[...], jnp.ones((B, DH), bf16)], axis=1)
            acc_ref[...] += jnp.dot(ds_ref[...], k1, preferred_element_type=f32)

        @pl.when(ki < qi)
        def _():
            compute(False)

        @pl.when(ki == qi)
        def _():
            compute(True)
            dq_ref[...] = (acc_ref[:, :DH] * scale).astype(dq_ref.dtype)
            dfq_ref[...] = acc_ref[:, DH:DH + 1]

    qmap = lambda h, p, qt, kt: (qt[p], h)
    cmap = lambda h, p, qt, kt: (h, qt[p], 0)
    return _tri_call(
        body, "attn_bwd_dq", cfg, False,
        [pl.BlockSpec((B, DH), qmap),
         pl.BlockSpec((B, DH), lambda h, p, qt, kt: (kt[p], h)),
         pl.BlockSpec((B, DH), lambda h, p, qt, kt: (kt[p], NH + h)),
         pl.BlockSpec((B, DH), qmap), pl.BlockSpec((B, DH), qmap),
         pl.BlockSpec((None, B, 1), cmap),
         pl.BlockSpec((None, 1, B), lambda h, p, qt, kt: (h, 0, kt[p]))],
        [pl.BlockSpec((B, DH), qmap), pl.BlockSpec((None, B, 1), cmap), pl.BlockSpec((None, B, 1), cmap)],
        [jax.ShapeDtypeStruct((L, D), bf16), jax.ShapeDtypeStruct((NH, L, 1), f32), jax.ShapeDtypeStruct((NH, L, 1), f32)],
        [pltpu.VMEM((B, 2 * DH), f32), pltpu.VMEM((B, B), f32), pltpu.VMEM((B, B), f32), pltpu.VMEM((B, B), bf16)],
        (q, kv, kv, do, o, lse, fk), rider)


def _ta_bwd_dkv(q, kv, do, delta, lse, fk, cfg, rider=None):
    L, D, NH, DH, B = cfg.L, cfg.D, cfg.NH, cfg.DH, cfg.BQ
    nq = L // B
    scale = DH ** -0.5

    def body(qt_ref, kt_ref, q_ref, k_ref, v_ref, do_ref, dl_ref, lse_ref, fk_ref, dk_ref, dv_ref, dfk_ref,
             dka_ref, dva_ref, s_ref, dp_ref, p_ref, ds_ref):
        pid = pl.program_id(1)
        qi, ki = qt_ref[pid], kt_ref[pid]

        @pl.when(qi == ki)
        def _():
            dka_ref[...] = jnp.zeros_like(dka_ref)
            dva_ref[...] = jnp.zeros_like(dva_ref)

        def compute(masked):
            s_ref[...] = lax.dot_general(q_ref[...], k_ref[...], _DIMS["nt"], preferred_element_type=f32)
            dp_ref[...] = lax.dot_general(do_ref[...], v_ref[...], _DIMS["nt"], preferred_element_type=f32)
            fkv = fk_ref[...]

            def strip(rows, row0, c):
                p = jnp.exp(s_ref[rows, :] - fkv - lse_ref[rows, :])
                if masked:
                    p = jnp.where(_fa_mask(row0, p.shape), p, 0.0)
                p_ref[rows, :] = p.astype(bf16)
                ds_ref[rows, :] = (p * (dp_ref[rows, :] - dl_ref[rows, :])).astype(bf16)
                return c
            _fa_strips(B, strip, 0)
            q1 = jnp.concatenate([q_ref[...], jnp.ones((B, DH), bf16)], axis=1)
            dva_ref[...] += lax.dot_general(p_ref[...], do_ref[...], _DIMS["tn"], preferred_element_type=f32)
            dka_ref[...] += lax.dot_general(ds_ref[...], q1, _DIMS["tn"], preferred_element_type=f32)

        @pl.when(qi == ki)
        def _():
            compute(True)

        @pl.when(qi > ki)
        def _():
            compute(False)

        @pl.when(qi == nq - 1)
        def _():
            dk_ref[...] = dka_ref[:, :DH].astype(dk_ref.dtype)
            dv_ref[...] = dva_ref[...].astype(dv_ref.dtype)
            dfk_ref[...] = -dka_ref[:, DH:DH + 1]

    qmap = lambda h, p, qt, kt: (qt[p], h)
    cmap = lambda h, p, qt, kt: (h, qt[p], 0)
    kmap = lambda h, p, qt, kt: (kt[p], h)
    return _tri_call(
        body, "attn_bwd_dkv", cfg, True,
        [pl.BlockSpec((B, DH), qmap), pl.BlockSpec((B, DH), kmap),
         pl.BlockSpec((B, DH), lambda h, p, qt, kt: (kt[p], NH + h)),
         pl.BlockSpec((B, DH), qmap), pl.BlockSpec((None, B, 1), cmap), pl.BlockSpec((None, B, 1), cmap),
         pl.BlockSpec((None, 1, B), lambda h, p, qt, kt: (h, 0, kt[p]))],
        [pl.BlockSpec((B, DH), kmap), pl.BlockSpec((B, DH), kmap),
         pl.BlockSpec((None, B, 1), lambda h, p, qt, kt: (h, kt[p], 0))],
        [jax.ShapeDtypeStruct((L, D), bf16), jax.ShapeDtypeStruct((L, D), bf16), jax.ShapeDtypeStruct((NH, L, 1), f32)],
        [pltpu.VMEM((B, 2 * DH), f32), pltpu.VMEM((B, DH), f32), pltpu.VMEM((B, B), f32), pltpu.VMEM((B, B), f32),
         pltpu.VMEM((B, B), bf16), pltpu.VMEM((B, B), bf16)],
        (q, kv, kv, do, delta, lse, fk), rider)


STRIP = 16


def _fa_strips(nrows, fn, init):
    return lax.fori_loop(0, nrows // STRIP, lambda r, c: fn(pl.ds(pl.multiple_of(r * STRIP, STRIP), STRIP), r * STRIP, c),
                         init, unroll=True)


def _fa_mask(row0, shape):
    rows = row0 + lax.broadcasted_iota(jnp.int32, shape, 0)
    cols = lax.broadcasted_iota(jnp.int32, shape, 1)
    return cols <= rows


def _fa_fwd(q, kv, fk, cfg):
    L, D, NH, DH, B = cfg.L, cfg.D, cfg.NH, cfg.DH, cfg.BQ
    nq = L // B
    scale = DH ** -0.5

    def body(q_ref, k_ref, v_ref, fk_ref, o_ref, lse_ref, m_ref, l_ref, acc_ref, a_ref, s_ref, p_ref):
        qi, ki = pl.program_id(1), pl.program_id(2)

        @pl.when(ki == 0)
        def _():
            m_ref[...] = jnp.full_like(m_ref, NEG)
            l_ref[...] = jnp.zeros_like(l_ref)
            acc_ref[...] = jnp.zeros_like(acc_ref)

        def compute(masked):
            s_ref[...] = lax.dot_general(q_ref[...], k_ref[...], _DIMS["nt"], preferred_element_type=f32)
            fkv = fk_ref[...]

            def strip(rows, row0, c):
                t = s_ref[rows, :] - fkv
                if masked:
                    t = jnp.where(_fa_mask(row0, t.shape), t, NEG)
                m_prev = m_ref[rows, :]
                m_new = jnp.maximum(m_prev, jnp.max(t, axis=1, keepdims=True))
                p = jnp.exp(t - m_new)
                alpha = jnp.exp(m_prev - m_new)
                l_ref[rows, :] = alpha * l_ref[rows, :] + jnp.sum(p, axis=1, keepdims=True)
                m_ref[rows, :] = m_new
                a_ref[rows, :] = alpha
                p_ref[rows, :] = p.astype(bf16)
                return c
            _fa_strips(B, strip, 0)
            acc_ref[...] = a_ref[...] * acc_ref[...] + jnp.dot(p_ref[...], v_ref[...], preferred_element_type=f32)

        @pl.when(ki < qi)
        def _():
            compute(False)

        @pl.when(ki == qi)
        def _():
            compute(True)
            o_ref[...] = (acc_ref[...] / l_ref[...]).astype(o_ref.dtype)
            lse_ref[...] = m_ref[...] + jnp.log(l_ref[...])

    col = pltpu.VMEM((B, 1), f32)
    return pl.pallas_call(
        body, name="attn_fwd", grid=(NH, nq, nq),
        in_specs=[pl.BlockSpec((B, DH), lambda h, qi, ki: (qi, h)),
                  pl.BlockSpec((B, DH), lambda h, qi, ki: (jnp.minimum(ki, qi), h)),
                  pl.BlockSpec((B, DH), lambda h, qi, ki: (jnp.minimum(ki, qi), NH + h)),
                  pl.BlockSpec((None, 1, B), lambda h, qi, ki: (h, 0, jnp.minimum(ki, qi)))],
        out_specs=[pl.BlockSpec((B, DH), lambda h, qi, ki: (qi, h)),
                   pl.BlockSpec((None, B, 1), lambda h, qi, ki: (h, qi, 0))],
        out_shape=[jax.ShapeDtypeStruct((L, D), bf16), jax.ShapeDtypeStruct((NH, L, 1), f32)],
        scratch_shapes=[col, col, pltpu.VMEM((B, DH), f32), col, pltpu.VMEM((B, B), f32), pltpu.VMEM((B, B), bf16)],
        compiler_params=_cp(("parallel", "parallel", "arbitrary")),
    )(q, kv, kv, fk)


def _fa_bwd_dq(q, kv, do, o, lse, fk, cfg):
    L, D, NH, DH, B = cfg.L, cfg.D, cfg.NH, cfg.DH, cfg.BQ
    nq = L // B
    scale = DH ** -0.5

    def body(q_ref, k_ref, v_ref, do_ref, o_ref, lse_ref, fk_ref, dq_ref, dfq_ref, dl_ref, acc_ref, df_ref, s_ref, dp_ref, ds_ref):
        qi, ki = pl.program_id(1), pl.program_id(2)

        @pl.when(ki == 0)
        def _():
            dl_ref[...] = jnp.sum(do_ref[...].astype(f32) * o_ref[...].astype(f32), axis=1, keepdims=True)
            acc_ref[...] = jnp.zeros_like(acc_ref)
            df_ref[...] = jnp.zeros_like(df_ref)

        def compute(masked):
            s_ref[...] = lax.dot_general(q_ref[...], k_ref[...], _DIMS["nt"], preferred_element_type=f32)
            dp_ref[...] = lax.dot_general(do_ref[...], v_ref[...], _DIMS["nt"], preferred_element_type=f32)
            fkv = fk_ref[...]

            def strip(rows, row0, c):
                p = jnp.exp(s_ref[rows, :] - fkv - lse_ref[rows, :])
                if masked:
                    p = jnp.where(_fa_mask(row0, p.shape), p, 0.0)
                ds = p * (dp_ref[rows, :] - dl_ref[rows, :])
                df_ref[rows, :] += jnp.sum(ds, axis=1, keepdims=True)
                ds_ref[rows, :] = ds.astype(bf16)
                return c
            _fa_strips(B, strip, 0)
            acc_ref[...] += jnp.dot(ds_ref[...], k_ref[...], preferred_element_type=f32)

        @pl.when(ki < qi)
        def _():
            compute(False)

        @pl.when(ki == qi)
        def _():
            compute(True)
            dq_ref[...] = (acc_ref[...] * scale).astype(dq_ref.dtype)
            dfq_ref[...] = df_ref[...]

    qmap = lambda h, qi, ki: (qi, h)
    cmap = lambda h, qi, ki: (h, qi, 0)
    return pl.pallas_call(
        body, name="attn_bwd_dq", grid=(NH, nq, nq),
        in_specs=[pl.BlockSpec((B, DH), qmap),
                  pl.BlockSpec((B, DH), lambda h, qi, ki: (jnp.minimum(ki, qi), h)),
                  pl.BlockSpec((B, DH), lambda h, qi, ki: (jnp.minimum(ki, qi), NH + h)),
                  pl.BlockSpec((B, DH), qmap), pl.BlockSpec((B, DH), qmap),
                  pl.BlockSpec((None, B, 1), cmap),
                  pl.BlockSpec((None, 1, B), lambda h, qi, ki: (h, 0, jnp.minimum(ki, qi)))],
        out_specs=[pl.BlockSpec((B, DH), qmap), pl.BlockSpec((None, B, 1), cmap), pl.BlockSpec((None, B, 1), cmap)],
        out_shape=[jax.ShapeDtypeStruct((L, D), bf16), jax.ShapeDtypeStruct((NH, L, 1), f32),
                   jax.ShapeDtypeStruct((NH, L, 1), f32)],
        scratch_shapes=[pltpu.VMEM((B, DH), f32), pltpu.VMEM((B, 1), f32), pltpu.VMEM((B, B), f32),
                        pltpu.VMEM((B, B), f32), pltpu.VMEM((B, B), bf16)],
        compiler_params=_cp(("parallel", "parallel", "arbitrary")),
    )(q, kv, kv, do, o, lse, fk)


def _fa_bwd_dkv(q, kv, do, delta, lse, fk, cfg):
    L, D, NH, DH, B = cfg.L, cfg.D, cfg.NH, cfg.DH, cfg.BQ
    nq = L // B
    scale = DH ** -0.5

    def body(q_ref, k_ref, v_ref, do_ref, dl_ref, lse_ref, fk_ref, dk_ref, dv_ref, dfk_ref,
             dka_ref, dva_ref, dfa_ref, s_ref, dp_ref, p_ref, ds_ref):
        ki, qi = pl.program_id(1), pl.program_id(2)

        @pl.when(qi == 0)
        def _():
            dka_ref[...] = jnp.zeros_like(dka_ref)
            dva_ref[...] = jnp.zeros_like(dva_ref)
            dfa_ref[...] = jnp.zeros_like(dfa_ref)

        def compute(masked):
            s_ref[...] = lax.dot_general(q_ref[...], k_ref[...], _DIMS["nt"], preferred_element_type=f32)
            dp_ref[...] = lax.dot_general(do_ref[...], v_ref[...], _DIMS["nt"], preferred_element_type=f32)
            fkv = fk_ref[...]

            def strip(rows, row0, cs):
                p = jnp.exp(s_ref[rows, :] - fkv - lse_ref[rows, :])
                if masked:
                    p = jnp.where(_fa_mask(row0, p.shape), p, 0.0)
                ds = p * (dp_ref[rows, :] - dl_ref[rows, :])
                p_ref[rows, :] = p.astype(bf16)
                ds_ref[rows, :] = ds.astype(bf16)
                return cs + ds
            cs = _fa_strips(B, strip, jnp.zeros((STRIP, B), f32))
            dva_ref[...] += lax.dot_general(p_ref[...], do_ref[...], _DIMS["tn"], preferred_element_type=f32)
            dka_ref[...] += lax.dot_general(ds_ref[...], q_ref[...], _DIMS["tn"], preferred_element_type=f32)
            dfa_ref[...] -= jnp.sum(cs, axis=0, keepdims=True)

        @pl.when(qi == ki)
        def _():
            compute(True)

        @pl.when(qi > ki)
        def _():
            compute(False)

        @pl.when(qi == nq - 1)
        def _():
            dk_ref[...] = (dka_ref[...] * scale).astype(dk_ref.dtype)
            dv_ref[...] = dva_ref[...].astype(dv_ref.dtype)
            dfk_ref[...] = dfa_ref[...]

    qmap = lambda h, ki, qi: (jnp.maximum(qi, ki), h)
    cmap = lambda h, ki, qi: (h, jnp.maximum(qi, ki), 0)
    return pl.pallas_call(
        body, name="attn_bwd_dkv", grid=(NH, nq, nq),
        in_specs=[pl.BlockSpec((B, DH), qmap),
                  pl.BlockSpec((B, DH), lambda h, ki, qi: (ki, h)),
                  pl.BlockSpec((B, DH), lambda h, ki, qi: (ki, NH + h)),
                  pl.BlockSpec((B, DH), qmap),
                  pl.BlockSpec((None, B, 1), cmap), pl.BlockSpec((None, B, 1), cmap),
                  pl.BlockSpec((None, 1, B), lambda h, ki, qi: (h, 0, ki))],
        out_specs=[pl.BlockSpec((B, DH), lambda h, ki, qi: (ki, h)), pl.BlockSpec((B, DH), lambda h, ki, qi: (ki, h)),
                   pl.BlockSpec((None, 1, B), lambda h, ki, qi: (h, 0, ki))],
        out_shape=[jax.ShapeDtypeStruct((L, D), bf16), jax.ShapeDtypeStruct((L, D), bf16),
                   jax.ShapeDtypeStruct((NH, 1, L), f32)],
        scratch_shapes=[pltpu.VMEM((B, DH), f32), pltpu.VMEM((B, DH), f32), pltpu.VMEM((1, B), f32),
                        pltpu.VMEM((B, B), f32), pltpu.VMEM((B, B), f32), pltpu.VMEM((B, B), bf16), pltpu.VMEM((B, B), bf16)],
        compiler_params=_cp(("parallel", "parallel", "arbitrary")),
    )(q, kv, kv, do, delta, lse, fk)


def _fox_logits(q, k, fqv, fkv, scale, masked):
    s = lax.dot_general(q, k, _DIMS["nt"], preferred_element_type=f32) * scale + fqv - fkv
    if masked:
        rows = lax.broadcasted_iota(jnp.int32, s.shape, 0)
        cols = lax.broadcasted_iota(jnp.int32, s.shape, 1)
        return s, cols <= rows
    return s, None


def _fox_fwd(q, kv, fq, fk, cfg):
    L, D, NH, DH, B = cfg.L, cfg.D, cfg.NH, cfg.DH, cfg.BQ
    nq = L // B
    scale = DH ** -0.5

    def body(q_ref, k_ref, v_ref, fq_ref, fk_ref, o_ref, lse_ref):
        qi = pl.program_id(1)
        qv, fqv = q_ref[...], fq_ref[...]

        def chunk(kj, carry, masked):
            m, l, acc = carry
            rows = pl.ds(pl.multiple_of(kj * B, B), B)
            s, mask = _fox_logits(qv, k_ref[rows, :], fqv, fk_ref[kj], scale, masked)
            if masked:
                s = jnp.where(mask, s, NEG)
            m_new = jnp.maximum(m, jnp.max(s, axis=1, keepdims=True))
            alpha = jnp.exp(m - m_new)
            p = jnp.exp(s - m_new)
            l = alpha * l + jnp.sum(p, axis=1, keepdims=True)
            acc = alpha * acc + jnp.dot(p.astype(bf16), v_ref[rows, :], preferred_element_type=f32)
            return m_new, l, acc

        init = (jnp.full((B, 1), NEG, f32), jnp.zeros((B, 1), f32), jnp.zeros((B, DH), f32))
        carry = lax.fori_loop(0, qi, lambda kj, c: chunk(kj, c, False), init)
        m, l, acc = chunk(qi, carry, True)
        o_ref[...] = (acc / l).astype(o_ref.dtype)
        lse_ref[...] = m + jnp.log(l)

    return pl.pallas_call(
        body, name="attn_fwd", grid=(NH, nq),
        in_specs=[pl.BlockSpec((B, DH), lambda h, qi: (qi, h)),
                  pl.BlockSpec((L, DH), lambda h, qi: (0, h)), pl.BlockSpec((L, DH), lambda h, qi: (0, NH + h)),
                  pl.BlockSpec((None, B, 1), lambda h, qi: (h, qi, 0)),
                  pl.BlockSpec((None, nq, 1, B), lambda h, qi: (h, 0, 0, 0))],
        out_specs=[pl.BlockSpec((B, DH), lambda h, qi: (qi, h)), pl.BlockSpec((None, B, 1), lambda h, qi: (h, qi, 0))],
        out_shape=[jax.ShapeDtypeStruct((L, D), bf16), jax.ShapeDtypeStruct((NH, L, 1), f32)],
        compiler_params=_cp(("parallel", "arbitrary")),
    )(q, kv, kv, fq, fk)


def _fox_bwd_dq(q, kv, do, o, lse, fq, fk, cfg):
    L, D, NH, DH, B = cfg.L, cfg.D, cfg.NH, cfg.DH, cfg.BQ
    nq = L // B
    scale = DH ** -0.5

    def body(q_ref, k_ref, v_ref, do_ref, o_ref, lse_ref, fq_ref, fk_ref, dq_ref, dfq_ref, dl_ref):
        qi = pl.program_id(1)
        qv, fqv, dov, lsev = q_ref[...], fq_ref[...], do_ref[...], lse_ref[...]
        delta = jnp.sum(dov.astype(f32) * o_ref[...].astype(f32), axis=1, keepdims=True)

        def chunk(kj, carry, masked):
            acc, df = carry
            rows = pl.ds(pl.multiple_of(kj * B, B), B)
            kv_ = k_ref[rows, :]
            s, mask = _fox_logits(qv, kv_, fqv, fk_ref[kj], scale, masked)
            p = jnp.exp(s - lsev)
            if masked:
                p = jnp.where(mask, p, 0.0)
            dp = lax.dot_general(dov, v_ref[rows, :], _DIMS["nt"], preferred_element_type=f32)
            ds = p * (dp - delta)
            return acc + jnp.dot(ds.astype(bf16), kv_, preferred_element_type=f32), df + jnp.sum(ds, axis=1, keepdims=True)

        carry = lax.fori_loop(0, qi, lambda kj, c: chunk(kj, c, False), (jnp.zeros((B, DH), f32), jnp.zeros((B, 1), f32)))
        acc, df = chunk(qi, carry, True)
        dq_ref[...] = (acc * scale).astype(dq_ref.dtype)
        dfq_ref[...] = df
        dl_ref[...] = delta

    qmap = lambda h, qi: (qi, h)
    cmap = lambda h, qi: (h, qi, 0)
    return pl.pallas_call(
        body, name="attn_bwd_dq", grid=(NH, nq),
        in_specs=[pl.BlockSpec((B, DH), qmap),
                  pl.BlockSpec((L, DH), lambda h, qi: (0, h)), pl.BlockSpec((L, DH), lambda h, qi: (0, NH + h)),
                  pl.BlockSpec((B, DH), qmap), pl.BlockSpec((B, DH), qmap),
                  pl.BlockSpec((None, B, 1), cmap), pl.BlockSpec((None, B, 1), cmap),
                  pl.BlockSpec((None, nq, 1, B), lambda h, qi: (h, 0, 0, 0))],
        out_specs=[pl.BlockSpec((B, DH), qmap), pl.BlockSpec((None, B, 1), cmap), pl.BlockSpec((None, B, 1), cmap)],
        out_shape=[jax.ShapeDtypeStruct((L, D), bf16), jax.ShapeDtypeStruct((NH, L, 1), f32),
                   jax.ShapeDtypeStruct((NH, L, 1), f32)],
        compiler_params=_cp(("parallel", "arbitrary")),
    )(q, kv, kv, do, o, lse, fq, fk)


def _fox_bwd_dkv(q, kv, do, delta, lse, fq, fk, cfg):
    L, D, NH, DH, B = cfg.L, cfg.D, cfg.NH, cfg.DH, cfg.BQ
    nq = L // B
    scale = DH ** -0.5

    def body(q_ref, k_ref, v_ref, do_ref, dl_ref, lse_ref, fq_ref, fk_ref, dk_ref, dv_ref, dfk_ref):
        ki = pl.program_id(1)
        kv_, vv, fkv = k_ref[...], v_ref[...], fk_ref[...]

        def block(qj, carry, masked):
            dk, dv, df = carry
            rows = pl.ds(pl.multiple_of(qj * B, B), B)
            qv, dov = q_ref[rows, :], do_ref[rows, :]
            s, mask = _fox_logits(qv, kv_, fq_ref[rows, :], fkv, scale, masked)
            p = jnp.exp(s - lse_ref[rows, :])
            if masked:
                p = jnp.where(mask, p, 0.0)
            dv = dv + lax.dot_general(p.astype(bf16), dov, _DIMS["tn"], preferred_element_type=f32)
            dp = lax.dot_general(dov, vv, _DIMS["nt"], preferred_element_type=f32)
            ds = p * (dp - dl_ref[rows, :])
            dk = dk + lax.dot_general(ds.astype(bf16), qv, _DIMS["tn"], preferred_element_type=f32)
            return dk, dv, df - jnp.sum(ds, axis=0, keepdims=True)

        init = (jnp.zeros((B, DH), f32), jnp.zeros((B, DH), f32), jnp.zeros((1, B), f32))
        carry = block(ki, init, True)
        dk, dv, df = lax.fori_loop(ki + 1, nq, lambda qj, c: block(qj, c, False), carry)
        dk_ref[...] = (dk * scale).astype(dk_ref.dtype)
        dv_ref[...] = dv.astype(dv_ref.dtype)
        dfk_ref[...] = df

    whole = lambda h, ki: (0, h)
    col = lambda h, ki: (h, 0, 0)
    return pl.pallas_call(
        body, name="attn_bwd_dkv", grid=(NH, nq),
        in_specs=[pl.BlockSpec((L, DH), whole),
                  pl.BlockSpec((B, DH), lambda h, ki: (ki, h)), pl.BlockSpec((B, DH), lambda h, ki: (ki, NH + h)),
                  pl.BlockSpec((L, DH), whole),
                  pl.BlockSpec((None, L, 1), col), pl.BlockSpec((None, L, 1), col), pl.BlockSpec((None, L, 1), col),
                  pl.BlockSpec((None, None, 1, B), lambda h, ki: (h, ki, 0, 0))],
        out_specs=[pl.BlockSpec((B, DH), lambda h, ki: (ki, h)), pl.BlockSpec((B, DH), lambda h, ki: (ki, h)),
                   pl.BlockSpec((None, None, 1, B), lambda h, ki: (h, ki, 0, 0))],
        out_shape=[jax.ShapeDtypeStruct((L, D), bf16), jax.ShapeDtypeStruct((L, D), bf16),
                   jax.ShapeDtypeStruct((NH, nq, 1, B), f32)],
        compiler_params=_cp(("parallel", "arbitrary")),
    )(q, kv, kv, do, delta, lse, fq, fk)


FCH = 256


def _split3(x):
    hi = x.astype(bf16)
    r1 = x - hi.astype(f32)
    mid = r1.astype(bf16)
    lo = (r1 - mid.astype(f32)).astype(bf16)
    return hi, mid, lo


def _tri_sum(tri, x):
    hi, mid, lo = _split3(x)
    return (jnp.dot(tri, hi, preferred_element_type=f32) + jnp.dot(tri, mid, preferred_element_type=f32)
            + jnp.dot(tri, lo, preferred_element_type=f32))


def _fgate_fwd(z, fb, cfg):
    L = cfg.L

    def body(z_ref, fb_ref, f_ref):
        r = lax.broadcasted_iota(jnp.int32, (FCH, FCH), 0)
        c = lax.broadcasted_iota(jnp.int32, (FCH, FCH), 1)
        tri = (c <= r).astype(bf16)
        carry = jnp.zeros((1, LANES), f32)
        for ch in range(L // FCH):
            x = z_ref[pl.ds(ch * FCH, FCH), :] + fb_ref[...]
            lf = jnp.minimum(x, 0.0) - jnp.log(1.0 + jnp.exp(-jnp.abs(x)))
            f_ref[pl.ds(ch * FCH, FCH), :] = _tri_sum(tri, lf) + carry
            carry = f_ref[pl.ds(ch * FCH + FCH - 1, 1), :]

    vm = pl.BlockSpec(memory_space=pltpu.VMEM)
    return pl.pallas_call(body, name="fgate_fwd", in_specs=[vm, vm], out_specs=vm,
                          out_shape=jax.ShapeDtypeStruct((L, LANES), f32), compiler_params=_cp())(z, fb)


def _fgate_bwd(df, z, fb, cfg):
    L = cfg.L

    def body(df_ref, z_ref, fb_ref, dz_ref, db_ref):
        r = lax.broadcasted_iota(jnp.int32, (FCH, FCH), 0)
        c = lax.broadcasted_iota(jnp.int32, (FCH, FCH), 1)
        tri = (c >= r).astype(bf16)
        carry = jnp.zeros((1, LANES), f32)
        dbs = jnp.zeros((1, LANES), f32)
        for ch in range(L // FCH - 1, -1, -1):
            suf = _tri_sum(tri, df_ref[pl.ds(ch * FCH, FCH), :]) + carry
            x = z_ref[pl.ds(ch * FCH, FCH), :] + fb_ref[...]
            dz = suf * _sigmoid(-x)
            dz_ref[pl.ds(ch * FCH, FCH), :] = dz
            dbs = dbs + _colsum(dz)
            carry = carry + _colsum(df_ref[pl.ds(ch * FCH, FCH), :])
        db_ref[...] = dbs

    vm = pl.BlockSpec(memory_space=pltpu.VMEM)
    return pl.pallas_call(body, name="fgate_bwd", in_specs=[vm, vm, vm], out_specs=[vm, vm],
                          out_shape=[jax.ShapeDtypeStruct((L, LANES), f32), jax.ShapeDtypeStruct((1, LANES), f32)],
                          compiler_params=_cp())(df, z, fb)


def _adamw(w, g, m, v, name):
    R, C = w.shape
    c1 = 1.0 - ADAM_B1 ** ADAM_STEP
    c2 = 1.0 - ADAM_B2 ** ADAM_STEP

    def fn(i, ni, wv, gv, mv, vv):
        mn = ADAM_B1 * mv + (1.0 - ADAM_B1) * gv
        vn = ADAM_B2 * vv + (1.0 - ADAM_B2) * (gv * gv)
        delta = -ADAM_LR * ((mn / c1) / (jnp.sqrt(vn / c2) + ADAM_EPS) + ADAM_WD * wv)
        return [delta, mn, vn], []
    tc = C if C % LANES else _tile(C, 1024)
    return _rowwise(fn, [(w, "rc"), (g, "rc"), (m, "rc"), (v, "rc")], [f32, f32, f32], [], L=R, C=C, tl=512, tc=tc, name=name)


def _sum_lead(x, out_dtype, name):
    n, R, C = x.shape
    tl = _tile(R, 512, HALO)
    tc = C if C % LANES else _tile(C, 1024)

    def body(x_ref, o_ref):
        acc = x_ref[0].astype(f32)
        for k in range(1, n):
            acc = acc + x_ref[k].astype(f32)
        o_ref[...] = acc.astype(o_ref.dtype)

    return pl.pallas_call(
        body, name=name, grid=(R // tl, C // tc),
        in_specs=[pl.BlockSpec((n, tl, tc), lambda i, j: (0, i, j))], out_specs=pl.BlockSpec((tl, tc), lambda i, j: (i, j)),
        out_shape=jax.ShapeDtypeStruct((R, C), out_dtype), compiler_params=_cp(("parallel", "parallel")),
    )(x)


def _add_own_half(g, got, core, name):
    _, _, R, C = g.shape
    tl = _tile(R, 512, HALO)
    tc = C if C % LANES else _tile(C, 1024)

    def body(core_ref, g_ref, got_ref, o_ref):
        o_ref[...] = (g_ref[...].astype(f32) + got_ref[...].astype(f32)).astype(o_ref.dtype)

    blk = pl.BlockSpec((None, tl, tc), lambda k, i, j, co: (k, i, j))
    return pl.pallas_call(
        body, name=name,
        grid_spec=pltpu.PrefetchScalarGridSpec(
            num_scalar_prefetch=1, grid=(4, R // tl, C // tc),
            in_specs=[pl.BlockSpec((None, None, tl, tc), lambda k, i, j, co: (k, co[0], i, j)), blk], out_specs=blk),
        out_shape=jax.ShapeDtypeStruct((4, R, C), bf16), compiler_params=_cp(("parallel", "parallel", "parallel")),
    )(core, g, got)


ANY = pl.BlockSpec(memory_space=pl.ANY)
LOCAL_CHUNKS = 4


def _place():
    x, y, c = lax.axis_index("x"), lax.axis_index("y"), lax.axis_index("c")
    return x, y, c


def _allgather8(blocks, name):
    return _run_rider(_gather_rider(blocks), name)


def _gather_rider(blocks, middle_at=(1, 2)):
    n = len(blocks)

    def steps(ins, outs, sems):
        send_sems, recv_sems, local_sems = sems
        x, y, c = _place()
        me, sibling = (x, y, c), (x, y, 1 - c)
        chips = [(1 - x, y), (x, 1 - y), (1 - x, 1 - y)]

        def slot(a, dev):
            return outs[a].at[4 * dev[0] + 2 * dev[1] + dev[2]]

        def copy(a, k, block, to, src=None):
            return pltpu.make_async_remote_copy(
                src_ref=slot(a, block) if src is None else src, dst_ref=slot(a, block),
                send_sem=send_sems.at[a * 7 + k], recv_sem=recv_sems.at[a * 7 + k], device_id=to, device_id_type=MESH)

        def mine():
            out = []
            for a in range(n):
                rows = blocks[a].shape[0]
                k = LOCAL_CHUNKS if rows % (LOCAL_CHUNKS * HALO) == 0 else 1
                for i in range(k):
                    piece = pl.ds(i * (rows // k), rows // k)
                    out.append(pltpu.make_async_copy(ins[a].at[piece], slot(a, me).at[piece], local_sems.at[a * LOCAL_CHUNKS + i]))
            return out

        def first():
            out = []
            for a in range(n):
                out.append(copy(a, 0, me, sibling, src=ins[a]))
                out += [copy(a, 1 + j, me, (*chip, c), src=ins[a]) for j, chip in enumerate(chips)]
            return out

        def passed():
            return [copy(a, 4 + j, (*chip, c), sibling) for j, chip in enumerate(chips) for a in range(n)]

        def start():
            for cp in mine() + first():
                cp.start()

        def middle():
            for j, chip in enumerate(chips):
                for a in range(n):
                    copy(a, 1 + j, (*chip, c), me).wait_recv()
                    copy(a, 4 + j, (*chip, c), sibling).start()

        def finish():
            for a in range(n):
                copy(a, 0, sibling, me).wait_recv()
            for j, chip in enumerate(chips):
                for a in range(n):
                    copy(a, 4 + j, (*chip, 1 - c), me).wait_recv()
            for cp in first() + passed():
                cp.wait_send()
            for cp in mine():
                cp.wait()
        return start, middle, finish

    return dict(ins=list(blocks), out_shapes=[jax.ShapeDtypeStruct((N_DEV,) + b.shape, b.dtype) for b in blocks],
                sems=[pltpu.SemaphoreType.DMA((7 * n,)), pltpu.SemaphoreType.DMA((7 * n,)),
                      pltpu.SemaphoreType.DMA((LOCAL_CHUNKS * n,))],
                steps=steps, middle_at=middle_at)


def _run_rider(rider, name):
    ni, no = len(rider["ins"]), len(rider["out_shapes"])

    def body(*refs):
        start, middle, finish = rider["steps"](refs[:ni], refs[ni:ni + no], refs[ni + no:])
        start()
        if middle is not None:
            middle()
        finish()

    outs = pl.pallas_call(body, name=name, in_specs=[ANY] * ni, out_specs=[ANY] * no, out_shape=rider["out_shapes"],
                          scratch_shapes=rider["sems"])(*rider["ins"])
    return list(outs)


def _host_call(body, *, name, grid, in_specs, out_specs, out_shape, scratch_shapes, args, prefetch=(), rider=None):
    npre, nin, nout, nscr = len(prefetch), len(in_specs), len(out_specs), len(scratch_shapes)
    r_in, r_out, r_scr = (rider["ins"], rider["out_shapes"], rider["sems"]) if rider else ([], [], [])
    nri, nro = len(r_in), len(r_out)

    def kern(*refs):
        pre, rest = refs[:npre], refs[npre:]
        cin, rin = rest[:nin], rest[nin:nin + nri]
        o0 = nin + nri
        cout, rout = rest[o0:o0 + nout], rest[o0 + nout:o0 + nout + nro]
        s0 = o0 + nout + nro
        cscr, rscr = rest[s0:s0 + nscr], rest[s0 + nscr:]
        if rider:
            ids = [pl.program_id(d) for d in range(len(grid))]
            rest_zero = functools.reduce(jnp.logical_and, [i == 0 for i in ids[1:]], True)
            start, middle, finish = rider["steps"](rin, rout, rscr)
            pl.when(jnp.logical_and(ids[0] == 0, rest_zero))(start)
            if middle is not None:
                num, den = rider.get("middle_at", (1, 2))
                pl.when(jnp.logical_and(ids[0] == grid[0] * num // den, rest_zero))(middle)
        body(*pre, *cin, *cout, *cscr)
        if rider:
            pl.when(functools.reduce(jnp.logical_and, [i == g - 1 for i, g in zip(ids, grid)]))(finish)

    res = pl.pallas_call(
        kern, name=name,
        grid_spec=pltpu.PrefetchScalarGridSpec(num_scalar_prefetch=npre, grid=grid, in_specs=list(in_specs) + [ANY] * nri,
                                               out_specs=list(out_specs) + [ANY] * nro,
                                               scratch_shapes=list(scratch_shapes) + list(r_scr)),
        out_shape=list(out_shape) + list(r_out),
        compiler_params=_cp(("arbitrary",) * len(grid) if rider else ("parallel",) + ("arbitrary",) * (len(grid) - 1)),
    )(*prefetch, *args, *r_in)
    return list(res[:nout]), list(res[nout:])


def _sibling_send(halves, name):
    n = len(halves)

    def body(*refs):
        ins, outs = refs[:n], refs[n:2 * n]
        send_sems, recv_sems = refs[2 * n:]
        x, y, c = _place()
        sends = [pltpu.make_async_remote_copy(src_ref=ins[a], dst_ref=outs[a], send_sem=send_sems.at[a],
                                              recv_sem=recv_sems.at[a], device_id=(x, y, 1 - c), device_id_type=MESH)
                 for a in range(n)]
        for cp in sends:
            cp.start()
        for cp in sends:
            cp.wait_recv()
        for cp in sends:
            cp.wait_send()

    outs = pl.pallas_call(
        body, name=name, in_specs=[ANY] * n, out_specs=[ANY] * n,
        out_shape=[jax.ShapeDtypeStruct(h.shape, h.dtype) for h in halves],
        scratch_shapes=[pltpu.SemaphoreType.DMA((n,)), pltpu.SemaphoreType.DMA((n,))],
    )(*halves)
    return list(outs)


def _sibling_swap_halves(grads, name):
    n = len(grads)

    def body(*refs):
        ins, outs = refs[:n], refs[n:2 * n]
        send_sems, recv_sems = refs[2 * n:]
        x, y, c = _place()
        sends = [pltpu.make_async_remote_copy(src_ref=ins[a].at[:, 1 - c], dst_ref=outs[a], send_sem=send_sems.at[a],
                                              recv_sem=recv_sems.at[a], device_id=(x, y, 1 - c), device_id_type=MESH)
                 for a in range(n)]
        for cp in sends:
            cp.start()
        for cp in sends:
            cp.wait_recv()
        for cp in sends:
            cp.wait_send()

    outs = pl.pallas_call(
        body, name=name, in_specs=[ANY] * n, out_specs=[ANY] * n,
        out_shape=[jax.ShapeDtypeStruct((4,) + g.shape[2:], g.dtype) for g in grads],
        scratch_shapes=[pltpu.SemaphoreType.DMA((n,)), pltpu.SemaphoreType.DMA((n,))],
    )(*grads)
    return list(outs)


def _chip_scatter(parts, name):
    return _run_rider(_scatter_rider(parts), name)


def _scatter_rider(parts):
    n = len(parts)

    def steps(ins, outs, sems):
        send_sems, recv_sems = sems
        x, y, c = _place()
        chips = [(1 - x, y), (x, 1 - y), (1 - x, 1 - y)]

        def sends():
            return [pltpu.make_async_remote_copy(
                src_ref=ins[a].at[2 * px + py], dst_ref=outs[a].at[j], send_sem=send_sems.at[a * 3 + j],
                recv_sem=recv_sems.at[a * 3 + j], device_id=(px, py, c), device_id_type=MESH)
                for a in range(n) for j, (px, py) in enumerate(chips)]

        def start():
            for cp in sends():
                cp.start()

        def finish():
            for cp in sends():
                cp.wait_recv()
            for cp in sends():
                cp.wait_send()
        return start, None, finish

    return dict(ins=list(parts), out_shapes=[jax.ShapeDtypeStruct((3,) + p.shape[1:], p.dtype) for p in parts],
                sems=[pltpu.SemaphoreType.DMA((3 * n,)), pltpu.SemaphoreType.DMA((3 * n,))], steps=steps)


def _sum_parts(own, got, chip, name):
    _, R, C = own.shape
    tl = _tile(R, 512, HALO)
    tc = C if C % LANES else _tile(C, 1024)

    def body(chip_ref, own_ref, got_ref, o_ref):
        acc = own_ref[...].astype(f32)
        for k in range(3):
            acc = acc + got_ref[k].astype(f32)
        o_ref[...] = acc

    return pl.pallas_call(
        body, name=name,
        grid_spec=pltpu.PrefetchScalarGridSpec(
            num_scalar_prefetch=1, grid=(R // tl, C // tc),
            in_specs=[pl.BlockSpec((None, tl, tc), lambda i, j, ch: (ch[0], i, j)),
                      pl.BlockSpec((3, tl, tc), lambda i, j, ch: (0, i, j))],
            out_specs=pl.BlockSpec((tl, tc), lambda i, j, ch: (i, j))),
        out_shape=jax.ShapeDtypeStruct((R, C), f32), compiler_params=_cp(("parallel", "parallel")),
    )(chip, own, got)


def _adamw_halves(w, m, v, g_mine, g_other, core, name, rider=None):
    NL, R, C = w.shape
    r = R // 2
    tl = _tile(r, 512, HALO)
    tc = C if C % LANES else _tile(C, 1024)
    nh = r // tl
    c1 = 1.0 - ADAM_B1 ** ADAM_STEP
    c2 = 1.0 - ADAM_B2 ** ADAM_STEP

    def body(core_ref, w_ref, m_ref, v_ref, gm_ref, go_ref, g_out, d_out, m_out, v_out):
        i = pl.program_id(1)
        mine = lax.div(i, nh) == core_ref[0]
        gv = jnp.where(mine, gm_ref[...], go_ref[...])
        mn = ADAM_B1 * m_ref[...] + (1.0 - ADAM_B1) * gv
        vn = ADAM_B2 * v_ref[...] + (1.0 - ADAM_B2) * (gv * gv)
        g_out[...] = gv
        d_out[...] = -ADAM_LR * ((mn / c1) / (jnp.sqrt(vn / c2) + ADAM_EPS) + ADAM_WD * w_ref[...])
        m_out[...] = mn
        v_out[...] = vn

    full = pl.BlockSpec((None, tl, tc), lambda l, i, j, co: (l, i, j))
    mine_spec = pl.BlockSpec((None, tl, tc), lambda l, i, j, co: (l, jnp.clip(i - co[0] * nh, 0, nh - 1), j))
    other_spec = pl.BlockSpec((None, tl, tc), lambda l, i, j, co: (l, jnp.clip(i - (1 - co[0]) * nh, 0, nh - 1), j))
    outs, extra = _host_call(
        body, name=name, grid=(NL, R // tl, C // tc), in_specs=[full, full, full, mine_spec, other_spec],
        out_specs=[full] * 4, out_shape=[jax.ShapeDtypeStruct((NL, R, C), f32)] * 4, scratch_shapes=[],
        args=(w, m, v, g_mine, g_other), prefetch=(core,), rider=rider)
    return (*outs, extra)


def _s5_discretize(log_step, a_re, a_im, b_re, b_im):
    step = jnp.exp(log_step)[:, None]
    mag = jnp.exp(a_re * step)
    abar_re = mag * jnp.cos(a_im * step)
    abar_im = mag * jnp.sin(a_im * step)
    den = a_re * a_re + a_im * a_im
    nr = abar_re - 1.0
    fr = (nr * a_re + abar_im * a_im) / den
    fi = (abar_im * a_re - nr * a_im) / den
    bbar_re = fr[..., None] * b_re - fi[..., None] * b_im
    bbar_im = fr[..., None] * b_im + fi[..., None] * b_re
    return abar_re, abar_im, bbar_re, bbar_im


def _s5_prepare(p, cfg):
    abar_re, abar_im, bbar_re, bbar_im = _s5_discretize(p["log_step"], p["a_re"], p["a_im"], p["b_re"], p["b_im"])
    step = jnp.exp(p["log_step"])[:, None]
    arow, tab = _s5_tables(abar_re, abar_im, p["a_re"], p["a_im"], step, cfg)
    bmat, cmat = _s5_mats(bbar_re, bbar_im, p["c_re"], p["c_im"], cfg)
    return dict(arow=arow, tab=tab, bmat=bmat, cmat=cmat, drow=p["d"].reshape(1, cfg.D))


def _s5_param_grads(p, dbmat, dcmat, dabar, dd, cfg):
    J, P = cfg.G // 8, cfg.P
    dbb_re, dbb_im, dc_re, dc_im = _s5_unmats(dbmat, dcmat, cfg)
    da = dabar.reshape(J, 2, 8, P)
    da_re, da_im = da[:, 0].reshape(cfg.G, P), da[:, 1].reshape(cfg.G, P)
    _, vjp = jax.vjp(_s5_discretize, p["log_step"], p["a_re"], p["a_im"], p["b_re"], p["b_im"])
    dls, dare, daim, dbre, dbim = vjp((da_re, da_im, dbb_re, dbb_im))
    return dict(log_step=dls, a_re=dare, a_im=daim, b_re=dbre, b_im=dbim, c_re=dc_re, c_im=dc_im, d=dd.reshape(cfg.G, cfg.H))


def _resid_epi(acc, xv, gv):
    return xv + gv * acc, acc


def _ffn_fwd(x_in, g_norm, sc, sh, gate, W, exch, conv_w, conv_b, cfg, tag):
    h = _norm_mod_fwd(x_in, g_norm, sc, sh, cfg, f"ffn_norm_{tag}")
    rider = exch.rider(f"ffn_up_{tag}")
    a = _mm(h, W[f"ffn_w_up{tag}"], mode="nn", b4=True, tn=1408, out_dtypes=(bf16,), name=f"ffn_up_{tag}", rider=rider)
    if rider:
        a, extra = a
        W.update(exch.done(f"ffn_up_{tag}", extra))
    act = _conv_act_fwd(a, conv_w, conv_b, cfg)
    rider = exch.rider(f"ffn_down_{tag}")
    res = _mm(act, W[f"ffn_w_down{tag}"], mode="nn", extras=[(x_in, "mn"), (gate, "n")], epi=_resid_epi,
              out_dtypes=(f32, bf16), name=f"ffn_down_{tag}", rider=rider)
    if rider:
        res, extra = res
        W.update(exch.done(f"ffn_down_{tag}", extra))
    x_out, out = res
    return x_out, dict(h=h, a=a, act=act, out=out)


def _ffn_bwd(dx, x_in, sv, g_norm, sc, gate, w_up4, w_down, conv_w, conv_b, cfg, tag):
    F = cfg.F
    dout, dgate = _gate_bwd(dx, sv["out"], gate, cfg, f"ffn_gate_bwd_{tag}")
    dact = _mm(dout, w_down, mode="nt", tn=1408, out_dtypes=(bf16,), name=f"ffn_dact_{tag}")
    dw_down = _mm(sv["act"], dout, mode="tn", tm=1408, out_dtypes=(bf16,), name=f"ffn_dwdown_{tag}")
    dcu, dcv, dwu, dwv, dbu, dbv = _conv_act_bwd1(dact, sv["a"], conv_w, conv_b, cfg)
    dau = _conv_bwd2(dcu, conv_w[:, :F], cfg, f"conv_bwd2u_{tag}")
    dav = _conv_bwd2(dcv, conv_w[:, F:], cfg, f"conv_bwd2v_{tag}")
    da = jnp.concatenate([dau, dav], axis=1)
    dh = _mm(da, w_up4, mode="nt", b4=True, tk=1408, out_dtypes=(bf16,), name=f"ffn_dh_{tag}")
    dw_up = _mm(sv["h"], da, mode="tn", out4=True, tn=1408, out_dtypes=(bf16,), name=f"ffn_dwup_{tag}")
    dx_in, A, B = _norm_mod_bwd(dh, x_in, g_norm, sc, dx, cfg, f"ffn_norm_bwd_{tag}")
    small = dict(norm_g=(1.0 + sc) * A, sc=g_norm * A, sh=B, gate=dgate,
                 conv_w=jnp.concatenate([dwu, dwv], axis=1), conv_b=jnp.concatenate([dbu, dbv], axis=1))
    return dx_in, dw_up, dw_down, small


class _NoExchange:
    def rider(self, key, grads=None):
        return None

    def done(self, key, extra):
        return {}


def _local_step(cfg, x, tgt, mod, W, sp, exch=None):
    D, NH = cfg.D, cfg.NH
    exch = exch or _NoExchange()
    W, big = dict(W), {}

    def hand_over(key, grads):
        rider = exch.rider(key, grads)
        if rider is None:
            big.update(grads)
        return rider
    row = lambda v: v.reshape(1, -1)
    nmg0, nmg1 = row(sp["norm_mix_g"][0]), row(sp["norm_mix_g"][1])
    nfg0, nfg1 = row(sp["norm_ffn_g"][0]), row(sp["norm_ffn_g"][1])
    kvg, fng = row(sp["kv_norm_g"]), row(sp["final_norm_g"])
    cw0, cw1 = sp["ffn_conv_w"][0], sp["ffn_conv_w"][1]
    cb0, cb1 = row(sp["ffn_conv_b"][0]), row(sp["ffn_conv_b"][1])
    glu_b = row(sp["ssm_glu_b"])
    fb = jnp.zeros((1, LANES), f32).at[0, :NH].set(sp["forget_b"])
    s5p = {k: sp["ssm_" + k][0] for k in ("log_step", "a_re", "a_im", "b_re", "b_im", "c_re", "c_im", "d")}
    s5 = _s5_prepare(s5p, cfg)
    m0, m1 = mod["l0"], mod["l1"]

    h0 = _norm_mod_fwd(x, nmg0, m0["sc_m"], m0["sh_m"], cfg, "mix_norm_0")
    u = _mm(h0, W["ssm_w_in"], mode="nn", name="ssm_in")
    y, gact, cin, extra = _s5_fwd(u, s5["bmat"], s5["cmat"], s5["drow"], s5["arow"], s5["tab"], cfg, exch.rider("s5_fwd"))
    W.update(exch.done("s5_fwd", extra))

    def glu_epi(acc, bv, gv):
        pre = acc + bv
        return pre, gv.astype(f32) * _sigmoid(pre)
    pre, z = _mm(gact, W["ssm_glu_w"], mode="nn", extras=[(glu_b, "n"), (gact, "mn")], epi=glu_epi,
                 out_dtypes=(f32, bf16), name="ssm_glu")
    x1, out_m0 = _mm(z, W["ssm_w_out"], mode="nn", extras=[(x, "mn"), (m0["g_m"], "n")], epi=_resid_epi,
                     out_dtypes=(f32, bf16), name="ssm_out")
    x2, ffn0 = _ffn_fwd(x1, nfg0, m0["sc_f"], m0["sh_f"], m0["g_f"], W, exch, cw0, cb0, cfg, "0")

    hk = _norm_mod_fwd(x2, kvg, mod["sc_kv"], mod["sh_kv"], cfg, "kv_norm")
    kvb = _mm(hk, W["kv_w"], mode="nn", out_dtypes=(bf16,), name="kv_proj")
    zf = _mm(hk, W["kv_wf"], mode="nn", name="kv_fproj")
    fc = _fgate_fwd(zf, fb, cfg)
    fct = fc[:, :NH].T
    fk = fct[:, None, :]

    h1 = _norm_mod_fwd(x2, nmg1, m1["sc_m"], m1["sh_m"], cfg, "mix_norm_1")
    q = _mm(h1, W["attn_w_q"], mode="nn", epi=lambda acc: (acc * cfg.DH ** -0.5,), out_dtypes=(bf16,), name="attn_q")
    o, lse, extra = _ta_fwd(q, kvb, fk, cfg, exch.rider("attn_fwd"))
    W.update(exch.done("attn_fwd", extra))
    x3, out_m1 = _mm(o, W["attn_w_out"], mode="nn", extras=[(x2, "mn"), (m1["g_m"], "n")], epi=_resid_epi,
                     out_dtypes=(f32, bf16), name="attn_out")
    x4, ffn1 = _ffn_fwd(x3, nfg1, m1["sc_f"], m1["sh_f"], m1["g_f"], W, exch, cw1, cb1, cfg, "1")

    dx, dfng, lcol = _final_loss(x4, fng, tgt, cfg)
    loss = (0.5 / D) * jnp.sum(lcol)

    dx, dw_up1, dw_down1, sf1 = _ffn_bwd(dx, x3, ffn1, nfg1, m1["sc_f"], m1["g_f"], W["ffn_w_up1"], W["ffn_w_down1"], cw1, cb1, cfg, "1")
    dout, dgm1 = _gate_bwd(dx, out_m1, m1["g_m"], cfg, "attn_gate_bwd")
    do = _mm(dout, W["attn_w_out"], mode="nt", out_dtypes=(bf16,), name="attn_do")
    dw_ao = _mm(o, dout, mode="tn", out_dtypes=(bf16,), name="attn_dwout")
    dq, dfq, delta, extra = _ta_bwd_dq(q, kvb, do, o, lse, fk, cfg,
                                       hand_over("attn_bwd", dict(ffn_w_up1=dw_up1, ffn_w_down1=dw_down1)))
    exch.done("attn_bwd", extra)
    dw_q = _mm(h1, dq, mode="tn", out_dtypes=(bf16,), name="attn_dwq")
    dk, dv, dfk, extra = _ta_bwd_dkv(q, kvb, do, delta, lse, fk, cfg,
                                     hand_over("attn_bwd_dkv", dict(attn_w_q=dw_q, attn_w_out=dw_ao)))
    exch.done("attn_bwd_dkv", extra)
    dh1 = _mm(dq, W["attn_w_q"], mode="nt", out_dtypes=(bf16,), name="attn_dh")
    dx, A1, B1 = _norm_mod_bwd(dh1, x2, nmg1, m1["sc_m"], dx, cfg, "mix_norm_bwd_1")

    dfc = jnp.pad((dfq[:, :, 0] + dfk[:, :, 0]).T, ((0, 0), (0, LANES - NH)))
    dzf, dfb = _fgate_bwd(dfc, zf, fb, cfg)
    dkv = jnp.concatenate([dk, dv], axis=1)
    dhk1 = _mm(dkv, W["kv_w"], mode="nt", name="kv_dh1")
    dhk = _mm(dzf, W["kv_wf"], mode="nt", extras=[(dhk1, "mn")], epi=lambda acc, e: (acc + e,), out_dtypes=(bf16,), name="kv_dh2")
    dw_kv = _mm(hk, dkv, mode="tn", out_dtypes=(bf16,), name="kv_dw")
    dw_kf = _mm(hk, dzf, mode="tn", out_dtypes=(bf16,), name="kv_dwf")
    dx, Ak, Bk = _norm_mod_bwd(dhk, x2, kvg, mod["sc_kv"], dx, cfg, "kv_norm_bwd")

    dx, dw_up0, dw_down0, sf0 = _ffn_bwd(dx, x1, ffn0, nfg0, m0["sc_f"], m0["g_f"], W["ffn_w_up0"], W["ffn_w_down0"], cw0, cb0, cfg, "0")
    dout, dgm0 = _gate_bwd(dx, out_m0, m0["g_m"], cfg, "ssm_gate_bwd")
    dz = _mm(dout, W["ssm_w_out"], mode="nt", out_dtypes=(bf16,), name="ssm_dz")
    dw_so = _mm(z, dout, mode="tn", out_dtypes=(bf16,), name="ssm_dwout")
    dpre, dgd, dglub = _glu_bwd(dz, gact, pre, cfg)
    dy = _mm(dpre, W["ssm_glu_w"], mode="nt", extras=[(dgd, "mn"), (y, "mn")],
             epi=lambda acc, e, yv: ((acc + e) * _gelu_grad(yv),), name="ssm_dy")
    dw_glu = _mm(gact, dpre, mode="tn", out_dtypes=(bf16,), name="ssm_dwglu")
    rider = hand_over("s5_bwd", dict(kv_w=jnp.concatenate([dw_kv, dw_kf[:, :NH]], axis=1), ffn_w_up0=dw_up0,
                                     ffn_w_down0=dw_down0, ssm_w_out=dw_so, ssm_glu_w=dw_glu))
    du, dbm, dcm, dab, dd, extra = _s5_bwd(u, dy, cin, s5["bmat"], s5["cmat"], s5["drow"], s5["arow"], s5["tab"], cfg, rider)
    exch.done("s5_bwd", extra)
    dh0 = _mm(du, W["ssm_w_in"], mode="nt", out_dtypes=(bf16,), name="ssm_dh")
    dw_in = _mm(h0, du, mode="tn", out_dtypes=(bf16,), name="ssm_dwin")
    dx, A0, B0 = _norm_mod_bwd(dh0, x, nmg0, m0["sc_m"], dx, cfg, "mix_norm_bwd_0")

    s5g = _s5_param_grads(s5p, dbm, dcm, dab, dd, cfg)
    big["ssm_w_in"] = dw_in
    small = dict(
        norm_mix_g=jnp.concatenate([(1.0 + m0["sc_m"]) * A0, (1.0 + m1["sc_m"]) * A1], axis=0),
        norm_ffn_g=jnp.concatenate([sf0["norm_g"], sf1["norm_g"]], axis=0),
        ssm_glu_b=dglub, kv_norm_g=(1.0 + mod["sc_kv"]) * Ak, forget_b=dfb[0, :NH],
        ffn_conv_w=jnp.stack([sf0["conv_w"], sf1["conv_w"]]), ffn_conv_b=jnp.concatenate([sf0["conv_b"], sf1["conv_b"]], axis=0),
        final_norm_g=dfng, **{"ssm_" + k: v[None] for k, v in s5g.items()})
    dmod = [jnp.concatenate([B0, nmg0 * A0, dgm0, sf0["sh"], sf0["sc"], sf0["gate"]], axis=1),
            jnp.concatenate([B1, nmg1 * A1, dgm1, sf1["sh"], sf1["sc"], sf1["gate"]], axis=1),
            jnp.concatenate([Bk, kvg * Ak], axis=1)]
    return loss, dx, big, small, dmod


WEIGHTS = ["mod_w", "mod_b", "norm_mix_g", "norm_ffn_g", "ssm_w_in", "ssm_log_step", "ssm_a_re", "ssm_a_im", "ssm_b_re",
           "ssm_b_im", "ssm_c_re", "ssm_c_im", "ssm_d", "ssm_glu_w", "ssm_glu_b", "ssm_w_out", "kv_mod_w", "kv_mod_b",
           "kv_norm_g", "kv_w", "forget_b", "attn_w_q", "attn_w_out", "ffn_w_up", "ffn_conv_w", "ffn_conv_b", "ffn_w_down",
           "final_norm_g"]
ARGS = ["x", "c"] + WEIGHTS + ["loss_target"] + ["m_" + n for n in WEIGHTS] + ["v_" + n for n in WEIGHTS]
SMALL = ["mod_b", "norm_mix_g", "norm_ffn_g", "ssm_log_step", "ssm_a_re", "ssm_a_im", "ssm_b_re", "ssm_b_im", "ssm_c_re",
         "ssm_c_im", "ssm_d", "ssm_glu_b", "kv_mod_b", "kv_norm_g", "forget_b", "ffn_conv_w", "ffn_conv_b", "final_norm_g"]
PACK_ROWS = 512


def _pack(arrs):
    flat = jnp.concatenate([a.reshape(-1).astype(f32) for a in arrs])
    unit = PACK_ROWS * LANES
    n = -(-flat.shape[0] // unit) * unit
    return jnp.pad(flat, (0, n - flat.shape[0])).reshape(-1, LANES)


def _unpack(packed, shapes):
    flat, out, off = packed.reshape(-1), [], 0
    for s in shapes:
        n = math.prod(s)
        out.append(flat[off:off + n].reshape(s))
        off += n
    return out


def _silu(v):
    return v * _sigmoid(v)


def _half(w, c, axis):
    r = w.shape[axis] // 2
    return lax.dynamic_slice_in_dim(w, c * r, r, axis=axis)


class _Exchange:
    FIRST = ["ssm_w_in", "ssm_glu_w", "ssm_w_out"]
    FWD = dict(s5_fwd=["ffn_w_up0"], ffn_up_0=["ffn_w_down0", "kv_w"], ffn_down_0=["attn_w_q", "attn_w_out"],
               attn_fwd=["ffn_w_up1", "ffn_w_down1"])
    LATE = (3, 4)

    def __init__(self, cfg, blocks, core):
        self.cfg, self.blocks, self.core = cfg, blocks, core
        self.parts, self.scattered, self.names = {}, {}, {}

    def weights(self, names, gathered):
        D, F, NH = self.cfg.D, self.cfg.F, self.cfg.NH
        W = {}
        for n, g in zip(names, gathered):
            if n.startswith("ffn_w_up"):
                W[n] = g.reshape(4, D, 2 * F // 4)
            elif n == "kv_w":
                full = g.reshape(4, D, -1).transpose(1, 0, 2).reshape(D, -1)
                W["kv_w"] = full[:, :2 * D]
                W["kv_wf"] = jnp.pad(full[:, 2 * D:], ((0, 0), (0, LANES - NH)))
            else:
                W[n] = g.reshape(-1, D)
        return W

    def sibling_sum(self, key, grads):
        D = self.cfg.D

        def blocks_of(n, g):
            if n.startswith("ffn_w_up"):
                return g.reshape(4, 2, D // 2, -1)
            if n == "kv_w":
                return g.reshape(D, 4, -1).transpose(1, 0, 2).reshape(4, 2, D // 2, -1)
            return g.reshape(4, 2, g.shape[0] // 8, g.shape[1])
        names = list(grads)
        gb = [blocks_of(n, grads[n]) for n in names]
        recv = _sibling_swap_halves(gb, f"grad_sibling_swap_{key}")
        core1 = jnp.reshape(self.core, (1,)).astype(jnp.int32)
        for n, g, r in zip(names, gb, recv):
            self.parts[n] = _add_own_half(g, r, core1, f"grad_add_{n}")
        return self.parts

    def rider(self, key, grads=None):
        if key in self.FWD:
            return _gather_rider([self.blocks[n] for n in self.FWD[key]], (1, 2) if key == "attn_fwd" else self.LATE)
        if grads is None:
            return None
        self.names[key] = list(grads)
        parts = self.sibling_sum(key, grads)
        return _scatter_rider([parts[n] for n in self.names[key]])

    def done(self, key, extra):
        if key in self.FWD:
            return self.weights(self.FWD[key], extra)
        self.scattered.update(zip(self.names[key], extra))
        return {}


def kernel(x, c, mod_w, mod_b, norm_mix_g, norm_ffn_g, ssm_w_in, ssm_log_step, ssm_a_re, ssm_a_im, ssm_b_re, ssm_b_im, ssm_c_re, ssm_c_im, ssm_d, ssm_glu_w, ssm_glu_b, ssm_w_out, kv_mod_w, kv_mod_b, kv_norm_g, kv_w, forget_b, attn_w_q, attn_w_out, ffn_w_up, ffn_conv_w, ffn_conv_b, ffn_w_down, final_norm_g, loss_target, m_mod_w, m_mod_b, m_norm_mix_g, m_norm_ffn_g, m_ssm_w_in, m_ssm_log_step, m_ssm_a_re, m_ssm_a_im, m_ssm_b_re, m_ssm_b_im, m_ssm_c_re, m_ssm_c_im, m_ssm_d, m_ssm_glu_w, m_ssm_glu_b, m_ssm_w_out, m_kv_mod_w, m_kv_mod_b, m_kv_norm_g, m_kv_w, m_forget_b, m_attn_w_q, m_attn_w_out, m_ffn_w_up, m_ffn_conv_w, m_ffn_conv_b, m_ffn_w_down, m_final_norm_g, v_mod_w, v_mod_b, v_norm_mix_g, v_norm_ffn_g, v_ssm_w_in, v_ssm_log_step, v_ssm_a_re, v_ssm_a_im, v_ssm_b_re, v_ssm_b_im, v_ssm_c_re, v_ssm_c_im, v_ssm_d, v_ssm_glu_w, v_ssm_glu_b, v_ssm_w_out, v_kv_mod_w, v_kv_mod_b, v_kv_norm_g, v_kv_w, v_forget_b, v_attn_w_q, v_attn_w_out, v_ffn_w_up, v_ffn_conv_w, v_ffn_conv_b, v_ffn_w_down, v_final_norm_g):
    a = dict(locals())
    assert list(a) == ARGS
    return _step(CFG, a)


def _step(cfg, a):
    D, F, NH = cfg.D, cfg.F, cfg.NH
    x_, y_, c_ = _place()
    chip, dev = 2 * x_ + y_, 4 * x_ + 2 * y_ + c_

    big_src = dict(ssm_w_in=a["ssm_w_in"][0], ssm_glu_w=a["ssm_glu_w"][0], ssm_w_out=a["ssm_w_out"][0],
                   attn_w_q=a["attn_w_q"][0], attn_w_out=a["attn_w_out"][0],
                   ffn_w_up0=a["ffn_w_up"][0], ffn_w_up1=a["ffn_w_up"][1],
                   ffn_w_down0=a["ffn_w_down"][0], ffn_w_down1=a["ffn_w_down"][1], kv_w=a["kv_w"])
    big_names = list(big_src)
    exch = _Exchange(cfg, {n: _half(big_src[n], c_, 0).astype(bf16) for n in big_names}, c_)
    first = exch.FIRST
    blocks = [exch.blocks[n] for n in first] + [_half(a["ssm_glu_b"], c_, 1), _half(a["ffn_conv_w"], c_, 2), a["c"]]
    got = _allgather8(blocks, "gather_weights")
    W = exch.weights(first, got)
    glu_b_full = got[-3].reshape(D)
    conv_w_full = got[-2].transpose(1, 2, 0, 3).reshape(2, 3, 2 * F)
    c16 = jnp.pad(got[-1].reshape(N_DEV, D), ((0, 16 - N_DEV), (0, 0)))

    mcols = [_mm(c16, a["mod_w"][l], mode="nn", a_pro=_silu, name=f"mod_fwd_{l}") for l in range(2)]
    mcols.append(_mm(c16, a["kv_mod_w"], mode="nn", a_pro=_silu, name="mod_fwd_kv"))
    widths = [m.shape[1] for m in mcols]
    mall = _allgather8([jnp.concatenate(mcols, axis=1)[:N_DEV]], "gather_mod")[0][0::2]
    offs = [0, widths[0], widths[0] + widths[1]]
    rows = []
    for off, wd, bias in zip(offs, widths, [a["mod_b"][0], a["mod_b"][1], a["kv_mod_b"]]):
        fullm = mall[:, :, off:off + wd].transpose(1, 0, 2).reshape(N_DEV, 4 * wd) + bias
        rows.append(lax.dynamic_slice_in_dim(fullm, dev, 1, axis=0))
    mod = {}
    for l in range(2):
        mod[f"l{l}"] = dict(zip(["sh_m", "sc_m", "g_m", "sh_f", "sc_f", "g_f"], jnp.split(rows[l], 6, axis=1)))
    mod["sh_kv"], mod["sc_kv"] = jnp.split(rows[2], 2, axis=1)

    sp = {n: a[n] for n in ["norm_mix_g", "norm_ffn_g", "kv_norm_g", "final_norm_g", "ffn_conv_b", "forget_b", "ssm_log_step",
                            "ssm_a_re", "ssm_a_im", "ssm_b_re", "ssm_b_im", "ssm_c_re", "ssm_c_im", "ssm_d"]}
    sp["ssm_glu_b"], sp["ffn_conv_w"] = glu_b_full, conv_w_full
    loss, dx, big, small, dmod = _local_step(cfg, a["x"][0], a["loss_target"][0], mod, W, sp, exch)
    loss = lax.psum(loss, ("x", "y", "c"))

    small["mod_b"] = jnp.concatenate([dmod[0], dmod[1]], axis=0)
    small["kv_mod_b"] = dmod[2]
    shapes = [(2, 6 * D) if n == "mod_b" else (1, D) if n == "ssm_glu_b" else (2, 3, 2 * F) if n == "ffn_conv_w"
              else a[n].shape for n in SMALL]
    small_rider = _gather_rider([_pack([small[n] for n in SMALL])])

    last = list(big)
    exch.scattered.update(zip(last, _chip_scatter([exch.sibling_sum("tail", big)[n] for n in last], "grad_chip_scatter")))
    chip1, core1 = jnp.reshape(chip, (1,)).astype(jnp.int32), jnp.reshape(c_, (1,)).astype(jnp.int32)
    mine = {n: _sum_parts(exch.parts[n], exch.scattered[n], chip1, f"grad_sum_{n}") for n in big_names}
    other = dict(zip(big_names, _sibling_send([mine[n] for n in big_names], "grad_sibling_send")))

    grads, delta, new_m, new_v = {}, {}, {}, {}
    members = dict(ffn_w_up=["ffn_w_up0", "ffn_w_up1"], ffn_w_down=["ffn_w_down0", "ffn_w_down1"], kv_w=["kv_w"],
                   ssm_w_in=["ssm_w_in"], ssm_glu_w=["ssm_glu_w"], ssm_w_out=["ssm_w_out"], attn_w_q=["attn_w_q"],
                   attn_w_out=["attn_w_out"])
    for n, parts_ in members.items():
        shp = a[n].shape
        three = lambda t: t.reshape(len(parts_), -1, shp[-1])
        g_, d_, m_, v_, extra = _adamw_halves(three(a[n]), three(a["m_" + n]), three(a["v_" + n]),
                                              jnp.stack([mine[p] for p in parts_]), jnp.stack([other[p] for p in parts_]),
                                              core1, f"adamw_{n}", small_rider if n == "ffn_w_up" else None)
        if extra:
            packs = extra[0]
        grads[n], delta[n], new_m[n], new_v[n] = g_.reshape(shp), d_.reshape(shp), m_.reshape(shp), v_.reshape(shp)

    gsmall = dict(zip(SMALL, _unpack(_sum_lead(packs, f32, "sum_small"), shapes)))
    per_dev = packs.reshape(N_DEV, -1)
    sizes = [math.prod(s) for s in shapes]
    starts = dict(zip(SMALL, [sum(sizes[:i]) for i in range(len(sizes))]))

    def rows_of(name, l, width):
        st = starts[name] + l * 6 * D
        blk = lax.dynamic_slice(per_dev, (0, st + chip * width), (N_DEV, width))
        return jnp.pad(blk, ((0, 16 - N_DEV), (0, 0)))
    g_mod_w = jnp.stack([_mm(c16, rows_of("mod_b", l, 6 * D // 4), mode="tn", a_pro=_silu, name=f"mod_dw_{l}") for l in range(2)])
    g_kv_mod_w = _mm(c16, rows_of("kv_mod_b", 0, 2 * D // 4), mode="tn", a_pro=_silu, name="mod_dw_kv")
    gsmall["ssm_glu_b"] = lax.dynamic_slice_in_dim(gsmall["ssm_glu_b"], chip * (D // 4), D // 4, axis=1)
    gsmall["ffn_conv_w"] = lax.dynamic_slice_in_dim(gsmall["ffn_conv_w"], chip * (2 * F // 4), 2 * F // 4, axis=2)

    grads.update(gsmall)
    grads["mod_w"], grads["kv_mod_w"] = g_mod_w, g_kv_mod_w
    for n in ["mod_w", "kv_mod_w"]:
        shp = a[n].shape
        two = lambda t: t.reshape(-1, shp[-1])
        d_, m_, v_ = _adamw(two(a[n]), two(grads[n]), two(a["m_" + n]), two(a["v_" + n]), f"adamw_{n}")
        delta[n], new_m[n], new_v[n] = d_.reshape(shp), m_.reshape(shp), v_.reshape(shp)
    grads = {n: grads[n].reshape(a[n].shape) for n in WEIGHTS}
    sshapes = [a[n].shape for n in SMALL]
    d_, m_, v_ = _adamw(_pack([a[n] for n in SMALL]), _pack([grads[n] for n in SMALL]), _pack([a["m_" + n] for n in SMALL]),
                        _pack([a["v_" + n] for n in SMALL]), "adamw_small")
    for n, dd_, mm_, vv_ in zip(SMALL, _unpack(d_, sshapes), _unpack(m_, sshapes), _unpack(v_, sshapes)):
        delta[n], new_m[n], new_v[n] = dd_, mm_, vv_

    return (loss, dx[None], *[grads[n] for n in WEIGHTS], *[delta[n] for n in WEIGHTS],
            *[new_m[n] for n in WEIGHTS], *[new_v[n] for n in WEIGHTS])
```
